```python
import jax, jax.numpy as jnp
from jax import lax
import numpy as np

D_MODEL = 1024
BATCH = 2
SEQ = 8192
DEPTH = 4

GRID_W = 64
CTX_LEN = 256
HEAD_DIM = 64
N_Q_HEADS = 8
N_KV_HEADS = 2
GQA_GROUP = N_Q_HEADS // N_KV_HEADS
WINDOW = 128
ATTN_BLK = 128
ROPE_BASE = 10000.0
Q_W = N_Q_HEADS * HEAD_DIM
KV_W = N_KV_HEADS * HEAD_DIM
N_FOURIER_GROUPS = 4
FOURIER_GROUP_W = 160
FOURIER_W = N_FOURIER_GROUPS * FOURIER_GROUP_W
POOL_WINDOWS = (2, 4, 8, 16)
N_POOL_GROUPS = 4
POOL_GROUP_W = 160
POOL_W = N_POOL_GROUPS * POOL_GROUP_W
CONV_W = 512
CONV_K = 31
N_BRANCH = 4
PROJ_SPLITS = (Q_W, KV_W, KV_W, FOURIER_W, POOL_W, 2 * CONV_W)
PROJ_W = sum(PROJ_SPLITS)
N_GROUPS = 4
EXPERTS_PER_GROUP = 8
N_EXPERTS = N_GROUPS * EXPERTS_PER_GROUP
TOP_K = 2
EXPERT_HIDDEN = 512
MOE_BLK = 256
EPS = 1e-6
NEG_INF = -1e30

kernel_name = 'hybrid_gated_mixers_hmoe_dit'


def rms_norm(x, g):
    xf = x.astype(jnp.float32)
    y = xf * lax.rsqrt(jnp.mean(xf * xf, axis=-1, keepdims=True) + EPS)
    return (y * g.astype(jnp.float32)).astype(x.dtype)


def modulate(x, g, shift, scale):
    return rms_norm(x, g) * (1 + scale) + shift


def split_proj(proj):
    return jnp.split(proj, np.cumsum(PROJ_SPLITS)[:-1].tolist(), axis=-1)


def axial_rope(x, row, col):
    half = HEAD_DIM // 2
    nf = half // 2
    inv = ROPE_BASE ** (-jnp.arange(nf, dtype=jnp.float32) / nf)

    def rot(xh, pos):
        ang = pos.astype(jnp.float32)[:, None] * inv[None, :]
        cos = jnp.cos(ang)[None, :, None, :]
        sin = jnp.sin(ang)[None, :, None, :]
        x1 = xh[..., :nf].astype(jnp.float32)
        x2 = xh[..., nf:].astype(jnp.float32)
        return jnp.concatenate([x1 * cos - x2 * sin, x2 * cos + x1 * sin], axis=-1)

    return jnp.concatenate([rot(x[..., :half], row), rot(x[..., half:], col)], axis=-1).astype(x.dtype)


def window_attention(q, k, v, kc, vc, sink):
    B, S = q.shape[0], q.shape[1]
    C = kc.shape[1]
    nb = S // ATTN_BLK
    scale = HEAD_DIM ** -0.5
    qb = q.reshape(B, nb, ATTN_BLK, N_KV_HEADS, GQA_GROUP, HEAD_DIM)

    def band(t):
        tp = jnp.pad(t, ((0, 0), (ATTN_BLK, ATTN_BLK), (0, 0), (0, 0)))
        tp = tp.reshape(B, nb + 2, ATTN_BLK, N_KV_HEADS, HEAD_DIM)
        return jnp.concatenate([tp[:, :-2], tp[:, 1:-1], tp[:, 2:]], axis=2)

    kb, vb = band(k), band(v)
    s_band = jnp.einsum('bnqhgd,bnkhd->bnhgqk', qb, kb).astype(jnp.float32) * scale
    s_ctx = jnp.einsum('bnqhgd,bchd->bnhgqc', qb, kc).astype(jnp.float32) * scale
    blk = jnp.arange(nb)[:, None, None]
    r = jnp.arange(ATTN_BLK)[None, :, None]
    j = jnp.arange(3 * ATTN_BLK)[None, None, :]
    qpos = blk * ATTN_BLK + r
    kpos = (blk - 1) * ATTN_BLK + j
    valid = (jnp.abs(kpos - qpos) <= WINDOW) & (kpos >= 0) & (kpos < S)
    s_band = jnp.where(valid[None, :, None, None], s_band, NEG_INF)
    sk = jnp.broadcast_to(sink.astype(jnp.float32).reshape(1, 1, N_KV_HEADS, GQA_GROUP, 1, 1),
                          s_ctx.shape[:-1] + (1,))
    probs = jax.nn.softmax(jnp.concatenate([s_ctx, s_band, sk], axis=-1), axis=-1).astype(v.dtype)
    p_ctx = probs[..., :C]
    p_band = probs[..., C:-1]
    o = (jnp.einsum('bnhgqc,bchd->bnqhgd', p_ctx, vc)
         + jnp.einsum('bnhgqk,bnkhd->bnqhgd', p_band, vb))
    return o.reshape(B, S, Q_W)


def context_attention(qc, kc, vc, sink):
    B, C = qc.shape[0], qc.shape[1]
    qg = qc.reshape(B, C, N_KV_HEADS, GQA_GROUP, HEAD_DIM)
    s = jnp.einsum('bqhgd,bkhd->bhgqk', qg, kc).astype(jnp.float32) * (HEAD_DIM ** -0.5)
    sk = jnp.broadcast_to(sink.astype(jnp.float32).reshape(1, N_KV_HEADS, GQA_GROUP, 1, 1),
                          s.shape[:-1] + (1,))
    p = jax.nn.softmax(jnp.concatenate([s, sk], axis=-1), axis=-1)[..., :-1].astype(vc.dtype)
    o = jnp.einsum('bhgqk,bkhd->bqhgd', p, vc)
    return o.reshape(B, C, Q_W)


def fourier_mix(f):
    B, S = f.shape[0], f.shape[1]
    fg = f.astype(jnp.float32).reshape(B, S, N_FOURIER_GROUPS, FOURIER_GROUP_W)
    y = jnp.fft.fft2(fg, axes=(1, 3), norm='ortho').real
    return y.reshape(B, S, FOURIER_W).astype(f.dtype)


def pool_mix(p, pool_w, pool_scale):
    B, S = p.shape[0], p.shape[1]
    pf = p.astype(jnp.float32)
    cs = jnp.concatenate([jnp.zeros((B, 1, POOL_W), jnp.float32), jnp.cumsum(pf, axis=1)], axis=1)
    t = jnp.arange(S)
    outs = []
    for gi, w in enumerate(POOL_WINDOWS):
        lo = jnp.clip(t - w // 2, 0, S)
        hi = jnp.clip(t - w // 2 + w, 0, S)
        sl = slice(gi * POOL_GROUP_W, (gi + 1) * POOL_GROUP_W)
        csg = cs[:, :, sl]
        mean = (csg[:, hi] - csg[:, lo]) / (hi - lo).astype(jnp.float32)[None, :, None]
        outs.append(mean - pf[:, :, sl])
    z = jnp.stack(outs, axis=2).astype(p.dtype)
    z = jnp.einsum('bsgc,gcd->bsgd', z, pool_w).reshape(B, S, POOL_W)
    return z * pool_scale


def conv_module(cv, conv_w, conv_b, cn_g, cn_b):
    a, g = jnp.split(cv, 2, axis=-1)
    u = a * jax.nn.sigmoid(g)
    u = lax.conv_general_dilated(u, conv_w[:, None, :].astype(u.dtype), window_strides=(1,),
                                 padding=[(CONV_K // 2, CONV_K // 2)],
                                 dimension_numbers=('NWC', 'WIO', 'NWC'),
                                 feature_group_count=CONV_W) + conv_b
    uf = u.astype(jnp.float32)
    mu = jnp.mean(uf, axis=-1, keepdims=True)
    var = jnp.mean(jnp.square(uf - mu), axis=-1, keepdims=True)
    un = (uf - mu) * lax.rsqrt(var + EPS) * cn_g.astype(jnp.float32) + cn_b.astype(jnp.float32)
    return jax.nn.silu(un).astype(cv.dtype)


def parallel_mixer(h, attn_o, f, p, cv, lp):
    ya = attn_o @ lp['w_br_attn']
    yb = fourier_mix(f) @ lp['w_br_fourier']
    yc = pool_mix(p, lp['pool_w'], lp['pool_scale']) @ lp['w_br_pool']
    yd = conv_module(cv, lp['conv_w'], lp['conv_b'], lp['cn_g'], lp['cn_b']) @ lp['w_br_conv']
    g = jax.nn.sigmoid(h @ lp['w_gate'] + lp['b_gate']).reshape(h.shape[:-1] + (N_BRANCH, D_MODEL))
    y = g[..., 0, :] * ya + g[..., 1, :] * yb + g[..., 2, :] * yc + g[..., 3, :] * yd
    return y @ lp['w_out']


def grouped_swiglu(xt, eid, wts, w_g, w_u, w_d):
    N, D = xt.shape
    A = N * TOP_K
    e_flat = eid.reshape(A)
    order = jnp.argsort(e_flat)
    e_sorted = e_flat[order]
    counts = jnp.bincount(e_flat, length=N_EXPERTS)
    starts = jnp.cumsum(counts) - counts
    padded = (counts + MOE_BLK - 1) // MOE_BLK * MOE_BLK
    pend = jnp.cumsum(padded)
    pstart = pend - padded
    dest_sorted = (pstart[e_sorted] + jnp.arange(A) - starts[e_sorted]).astype(jnp.int32)
    n_blocks = -(-A // MOE_BLK) + N_EXPERTS
    slot_tok = jnp.full((n_blocks * MOE_BLK,), N, jnp.int32).at[dest_sorted].set(
        (order // TOP_K).astype(jnp.int32))
    xs = jnp.concatenate([xt, jnp.zeros((1, D), xt.dtype)], axis=0)[slot_tok]
    xs = xs.reshape(n_blocks, MOE_BLK, D)
    blk_e = jnp.minimum(jnp.searchsorted(pend, jnp.arange(n_blocks) * MOE_BLK, side='right'),
                        N_EXPERTS - 1)

    def expert_block(args):
        xb, e = args
        return (jax.nn.silu(xb @ w_g[e]) * (xb @ w_u[e])) @ w_d[e]

    ys = lax.map(expert_block, (xs, blk_e)).reshape(n_blocks * MOE_BLK, D)
    dest = jnp.zeros((A,), jnp.int32).at[order].set(dest_sorted)
    y = ys[dest].reshape(N, TOP_K, D)
    return jnp.einsum('nk,nkd->nd', wts, y)


def hier_moe(xt, lp):
    N = xt.shape[0]
    lg = (xt @ lp['w_router_grp']).astype(jnp.float32) + lp['b_router_grp'].astype(jnp.float32)
    grp = jnp.argmax(lg, axis=-1)
    p_grp = jnp.take_along_axis(jax.nn.softmax(lg, axis=-1), grp[:, None], axis=-1)
    le = (xt @ lp['w_router_exp']).astype(jnp.float32) + lp['b_router_exp'].astype(jnp.float32)
    le = le.reshape(N, N_GROUPS, EXPERTS_PER_GROUP)
    le = jnp.take_along_axis(le, grp[:, None, None], axis=1)[:, 0]
    top_w, top_i = lax.top_k(jax.nn.softmax(le, axis=-1), TOP_K)
    top_w = top_w / jnp.sum(top_w, axis=-1, keepdims=True) * p_grp
    eid = (grp[:, None] * EXPERTS_PER_GROUP + top_i).astype(jnp.int32)
    return grouped_swiglu(xt, eid, top_w.astype(xt.dtype), lp['w_e_gate'], lp['w_e_up'], lp['w_e_down'])


def setup_inputs(seed: int = 0) -> dict:
    key = jax.random.key(seed)
    keys = list(jax.random.split(key, 40))
    L, D = DEPTH, D_MODEL

    def nrm(shape, scale):
        return jax.random.normal(keys.pop(), shape, jnp.float32) * scale

    return {
        'x': nrm((BATCH, SEQ, D), 1.0),
        'c': nrm((BATCH, D), 1.0),
        'ctx': nrm((BATCH, CTX_LEN, D), 1.0),
        'c_ctx': nrm((D,), 1.0),
        'w_ada': nrm((L, D, 6 * D), 0.5 * D ** -0.5),
        'b_ada': nrm((L, 6 * D), 0.02),
        'g_norm_mix': 1.0 + nrm((L, D), 0.05),
        'g_norm_ffn': 1.0 + nrm((L, D), 0.05),
        'w_in': nrm((L, D, PROJ_W), D ** -0.5),
        'g_q': 1.0 + nrm((L, HEAD_DIM), 0.05),
        'g_k': 1.0 + nrm((L, HEAD_DIM), 0.05),
        'sink': nrm((L, N_Q_HEADS), 0.5),
        'w_br_attn': nrm((L, Q_W, D), Q_W ** -0.5),
        'w_br_fourier': nrm((L, FOURIER_W, D), FOURIER_W ** -0.5),
        'pool_w': nrm((L, N_POOL_GROUPS, POOL_GROUP_W, POOL_GROUP_W), POOL_GROUP_W ** -0.5),
        'pool_scale': 1.0 + nrm((L, POOL_W), 0.1),
        'w_br_pool': nrm((L, POOL_W, D), POOL_W ** -0.5),
        'conv_w': nrm((L, CONV_K, CONV_W), CONV_K ** -0.5),
        'conv_b': nrm((L, CONV_W), 0.01),
        'cn_g': 1.0 + nrm((L, CONV_W), 0.05),
        'cn_b': nrm((L, CONV_W), 0.01),
        'w_br_conv': nrm((L, CONV_W, D), CONV_W ** -0.5),
        'w_gate': nrm((L, D, N_BRANCH * D), D ** -0.5),
        'b_gate': nrm((L, N_BRANCH * D), 0.01),
        'w_out': nrm((L, D, D), D ** -0.5),
        'w_router_grp': nrm((L, D, N_GROUPS), D ** -0.5),
        'b_router_grp': nrm((L, N_GROUPS), 0.01),
        'w_router_exp': nrm((L, D, N_EXPERTS), D ** -0.5),
        'b_router_exp': nrm((L, N_EXPERTS), 0.01),
        'w_e_gate': nrm((L, N_EXPERTS, D, EXPERT_HIDDEN), D ** -0.5),
        'w_e_up': nrm((L, N_EXPERTS, D, EXPERT_HIDDEN), D ** -0.5),
        'w_e_down': nrm((L, N_EXPERTS, EXPERT_HIDDEN, D), EXPERT_HIDDEN ** -0.5),
    }


def reference(x, c, ctx, c_ctx, w_ada, b_ada, g_norm_mix, g_norm_ffn, w_in, g_q, g_k, sink,
              w_br_attn, w_br_fourier, pool_w, pool_scale, w_br_pool, conv_w, conv_b, cn_g, cn_b,
              w_br_conv, w_gate, b_gate, w_out, w_router_grp, b_router_grp, w_router_exp,
              b_router_exp, w_e_gate, w_e_up, w_e_down):
    B, S, D = x.shape
    C = ctx.shape[1]
    rows = S // GRID_W
    row = jnp.repeat(jnp.arange(rows), GRID_W)
    col = jnp.tile(jnp.arange(GRID_W), rows)
    xc = ctx
    for l in range(DEPTH):
        last = l == DEPTH - 1
        lp = dict(w_in=w_in[l], w_br_attn=w_br_attn[l], w_br_fourier=w_br_fourier[l],
                  pool_w=pool_w[l], pool_scale=pool_scale[l], w_br_pool=w_br_pool[l],
                  conv_w=conv_w[l], conv_b=conv_b[l], cn_g=cn_g[l], cn_b=cn_b[l],
                  w_br_conv=w_br_conv[l], w_gate=w_gate[l], b_gate=b_gate[l], w_out=w_out[l],
                  w_router_grp=w_router_grp[l], b_router_grp=b_router_grp[l],
                  w_router_exp=w_router_exp[l], b_router_exp=b_router_exp[l],
                  w_e_gate=w_e_gate[l], w_e_up=w_e_up[l], w_e_down=w_e_down[l])
        mod = (jax.nn.silu(c) @ w_ada[l] + b_ada[l])[:, None, :]
        mod_c = (jax.nn.silu(c_ctx) @ w_ada[l] + b_ada[l])[None, None, :]
        sh_a, sc_a, ga_a, sh_f, sc_f, ga_f = jnp.split(mod, 6, axis=-1)
        sh_ac, sc_ac, ga_ac, sh_fc, sc_fc, ga_fc = jnp.split(mod_c, 6, axis=-1)

        hc = modulate(xc, g_norm_mix[l], sh_ac, sc_ac)
        if last:
            kc_raw, vc = jnp.split(hc @ w_in[l][:, Q_W:Q_W + 2 * KV_W], 2, axis=-1)
        else:
            qc, kc_raw, vc, fc, pc, cvc = split_proj(hc @ w_in[l])
        kc = rms_norm(kc_raw.reshape(B, C, N_KV_HEADS, HEAD_DIM), g_k[l])
        vc = vc.reshape(B, C, N_KV_HEADS, HEAD_DIM)

        h = modulate(x, g_norm_mix[l], sh_a, sc_a)
        q, k, v, f, p, cv = split_proj(h @ w_in[l])
        q = axial_rope(rms_norm(q.reshape(B, S, N_Q_HEADS, HEAD_DIM), g_q[l]), row, col)
        k = axial_rope(rms_norm(k.reshape(B, S, N_KV_HEADS, HEAD_DIM), g_k[l]), row, col)
        v = v.reshape(B, S, N_KV_HEADS, HEAD_DIM)
        attn = window_attention(q, k, v, kc, vc, sink[l])
        x = x + ga_a * parallel_mixer(h, attn, f, p, cv, lp)

        if last:
            h2 = modulate(x, g_norm_ffn[l], sh_f, sc_f)
            x = x + ga_f * hier_moe(h2.reshape(B * S, D), lp).reshape(B, S, D)
        else:
            qc = rms_norm(qc.reshape(B, C, N_Q_HEADS, HEAD_DIM), g_q[l])
            attn_c = context_attention(qc, kc, vc, sink[l])
            xc = xc + ga_ac * parallel_mixer(hc, attn_c, fc, pc, cvc, lp)
            h2 = modulate(x, g_norm_ffn[l], sh_f, sc_f)
            h2c = modulate(xc, g_norm_ffn[l], sh_fc, sc_fc)
            ym = hier_moe(jnp.concatenate([h2.reshape(B * S, D), h2c.reshape(B * C, D)], axis=0), lp)
            x = x + ga_f * ym[:B * S].reshape(B, S, D)
            xc = xc + ga_fc * ym[B * S:].reshape(B, C, D)
    return x
```

```python
import functools

import numpy as np
import jax
import jax.numpy as jnp
from jax import lax
from jax.experimental import pallas as pl
from jax.experimental.pallas import tpu as pltpu

F32 = jnp.float32
BF16 = jnp.bfloat16

D = 1024
B = 2
S = 8192
C = 256
GRID_W = 64
HEAD_DIM = 64
N_Q_HEADS = 8
N_KV_HEADS = 2
GQA = N_Q_HEADS // N_KV_HEADS
WINDOW = 128
ATTN_BLK = 128
ROPE_BASE = 10000.0
Q_W = 512
KV_W = 128
FOURIER_W = 640
FOURIER_GROUP_W = 160
POOL_W = 640
POOL_GROUP_W = 160
POOL_WINDOWS = (2, 4, 8, 16)
CONV_W = 512
CONV_K = 31
PROJ_W = 3072
N_GROUPS = 4
EPG = 8
N_EXPERTS = 32
TOP_K = 2
EXPERT_HIDDEN = 512
MOE_BLK = 256
EPS = 1e-6
NEG_INF = -1e30

N_LAT = B * S
N_CTX = B * C
N_TOK = N_LAT + N_CTX
TM = 512
NT = N_TOK // TM
LAT_TILES = N_LAT // TM
TILES_PER_BATCH = S // TM
TP = 256
NTP = N_TOK // TP
HALO = 16
N_ASSIGN = N_TOK * TOP_K
N_MOE_BLOCKS = -(-N_ASSIGN // MOE_BLK) + N_EXPERTS
N_SLOTS = N_MOE_BLOCKS * MOE_BLK
FS1 = 64
FS2 = 128
F1_COLS = FS2 * FOURIER_W
F1_CW = 8192
LANES = 128
VMEM_LIMIT = 56 * 1024 * 1024


def _cparams(sem, vmem=VMEM_LIMIT):
    return pltpu.CompilerParams(dimension_semantics=sem, vmem_limit_bytes=vmem)


def _const_spec(shape):
    nd = len(shape)
    return pl.BlockSpec(shape, lambda *_: (0,) * nd, pipeline_mode=pl.Buffered(1))


def _dot(a, b):
    return jnp.dot(a, b, preferred_element_type=F32)


def _split_dot(a, b_bf16):
    hi = a.astype(BF16)
    lo = (a - hi.astype(F32)).astype(BF16)
    return _dot(hi, b_bf16) + _dot(lo, b_bf16)


def _modulate(x, g, shift, scale):
    y = x * lax.rsqrt(jnp.mean(x * x, axis=-1, keepdims=True) + EPS)
    return (y * g) * (1.0 + scale) + shift


def _sigmoid(x):
    return 1.0 / (1.0 + jnp.exp(-x))


def _ada_kernel(ct_ref, w_ref, b_ref, o_ref):
    ct = ct_ref[...]
    s = ct * _sigmoid(ct)
    w = w_ref[...]
    rows = [jnp.sum(w * s[:, r:r + 1], axis=0, keepdims=True) for r in range(3)]
    rows.append(jnp.zeros((5, w.shape[1]), F32))
    o_ref[...] = jnp.concatenate(rows, axis=0) + b_ref[...]


def _ada_all(c, c_ctx, w_ada, b_ada):
    ct = jnp.concatenate([c, c_ctx[None, :], jnp.zeros((5, D), F32)], axis=0).T
    cols = 1536
    nl = w_ada.shape[0]
    return pl.pallas_call(
        _ada_kernel,
        out_shape=jax.ShapeDtypeStruct((nl, 8, 6 * D), F32),
        grid=(nl, 6 * D // cols),
        in_specs=[pl.BlockSpec((D, 8), lambda l, j: (0, 0)),
                  pl.BlockSpec((None, D, cols), lambda l, j: (l, 0, j)),
                  pl.BlockSpec((None, 1, cols), lambda l, j: (l, 0, j))],
        out_specs=pl.BlockSpec((None, 8, cols), lambda l, j: (l, 0, j)),
        compiler_params=_cparams(("arbitrary", "arbitrary")),
        name="adaln",
    )(ct, w_ada, b_ada.reshape(nl, 1, 6 * D))


def _head_rms(t, g128, bd):
    outs = []
    for j in range(t.shape[1] // LANES):
        blk = t[:, j * LANES:(j + 1) * LANES]
        ss = _split_dot(blk * blk, bd)
        outs.append(blk * lax.rsqrt(ss * (1.0 / HEAD_DIM) + EPS) * g128)
    return outs


def _rope(blocks, cos, sa, sb):
    outs = []
    for blk in blocks:
        up = pltpu.roll(blk, LANES - 16, 1)
        dn = pltpu.roll(blk, 16, 1)
        outs.append(blk * cos + up * sa + dn * sb)
    return outs


def _proj_kernel(x_ref, mod_ref, gn_ref, w_ref, cos_ref, sa_ref, sb_ref, gq_ref, gk_ref, bd_ref,
                 q_ref, k_ref, v_ref, f_ref, p_ref, u_ref):
    m = mod_ref[...]
    h = _modulate(x_ref[...], gn_ref[...], m[0:1], m[1:2])
    proj = _dot(h.astype(BF16), w_ref[...])
    cos, sa, sb, bd = cos_ref[...], sa_ref[...], sb_ref[...], bd_ref[...]
    q = _rope(_head_rms(proj[:, 0:Q_W], gq_ref[...], bd), cos, sa, sb)
    q_ref[...] = (jnp.concatenate(q, axis=1) * (HEAD_DIM ** -0.5)).astype(BF16)
    k = _rope(_head_rms(proj[:, Q_W:Q_W + KV_W], gk_ref[...], bd), cos, sa, sb)
    k_ref[...] = k[0].astype(BF16)
    o = Q_W + KV_W
    v_ref[...] = proj[:, o:o + KV_W].astype(BF16)
    o += KV_W
    f_ref[...] = proj[:, o:o + FOURIER_W].astype(BF16)
    o += FOURIER_W
    p_ref[...] = proj[:, o:o + POOL_W].astype(BF16)
    o += POOL_W
    a = proj[:, o:o + CONV_W]
    g = proj[:, o + CONV_W:o + 2 * CONV_W]
    u_ref[...] = (a * _sigmoid(g)).astype(BF16)


def _mod_row(i):
    return jnp.minimum(i // TILES_PER_BATCH, 2)


def _proj_call(x, mods_l, gn, w_in_bf, rope_tabs, gq128, gk128, bd):
    cos, sa, sb = rope_tabs
    tok = lambda w: pl.BlockSpec((TM, w), lambda i: (i, 0))
    rope_spec = pl.BlockSpec((TM, LANES), lambda i: (jnp.where(i < LAT_TILES, i % TILES_PER_BATCH,
                                                               TILES_PER_BATCH), 0))
    widths = (Q_W, KV_W, KV_W, FOURIER_W, POOL_W, CONV_W)
    return pl.pallas_call(
        _proj_kernel,
        out_shape=[jax.ShapeDtypeStruct((N_TOK, w), BF16) for w in widths],
        grid=(NT,),
        in_specs=[tok(D),
                  pl.BlockSpec((None, 6, D), lambda i: (_mod_row(i), 0, 0)),
                  _const_spec((1, D)),
                  _const_spec((D, PROJ_W)),
                  rope_spec, rope_spec, rope_spec,
                  _const_spec((1, LANES)), _const_spec((1, LANES)),
                  _const_spec((LANES, LANES))],
        out_specs=[tok(w) for w in widths],
        compiler_params=_cparams(("parallel",)),
        name="proj",
    )(x, mods_l, gn, w_in_bf, cos, sa, sb, gq128, gk128, bd)


def _attn_kernel(sink_ref, q_ref, *refs, band):
    if band:
        kp_ref, kc_ref, kn_ref, vp_ref, vc_ref, vn_ref, kx_ref, vx_ref, o_ref = refs
        n = pl.program_id(1)
        nblk = pl.num_programs(1)
        ncol = C + 3 * ATTN_BLK
        r = lax.broadcasted_iota(jnp.int32, (ATTN_BLK, ncol), 0)
        col = lax.broadcasted_iota(jnp.int32, (ATTN_BLK, ncol), 1)
        far = jnp.int32(2 * ATTN_BLK)
        off_prev = jnp.where(n > 0, 0, far)
        off_next = jnp.where(n < nblk - 1, 0, far)
        jp = col - C
        jn = col - (C + 2 * ATTN_BLK)
        slack = jnp.where(col < C, 0,
                          jnp.where(jp < ATTN_BLK, jp - r - off_prev,
                                    jnp.where(jn < 0, 0, r - jn - off_next)))
        valid = slack >= 0
    else:
        kx_ref, vx_ref, o_ref = refs
    q = q_ref[...]
    outs = []
    for j in range(N_KV_HEADS):
        sl = slice(j * HEAD_DIM, (j + 1) * HEAD_DIM)
        if band:
            kj = jnp.concatenate([kx_ref[:, sl], kp_ref[:, sl], kc_ref[:, sl], kn_ref[:, sl]], axis=0)
            vj = jnp.concatenate([vx_ref[:, sl], vp_ref[:, sl], vc_ref[:, sl], vn_ref[:, sl]], axis=0)
        else:
            kj = kx_ref[:, sl]
            vj = vx_ref[:, sl]
        for g in range(GQA):
            hq = j * GQA + g
            qh = q[:, hq * HEAD_DIM:(hq + 1) * HEAD_DIM]
            s = lax.dot_general(qh, kj, (((1,), (1,)), ((), ())), preferred_element_type=F32)
            if band:
                s = jnp.where(valid, s, NEG_INF)
            sk = sink_ref[hq]
            mx = jnp.maximum(jnp.max(s, axis=-1, keepdims=True), sk)
            e = jnp.exp(s - mx)
            den = jnp.sum(e, axis=-1, keepdims=True) + jnp.exp(sk - mx)
            outs.append(_dot((e / den).astype(BF16), vj))
    o_ref[...] = jnp.concatenate(outs, axis=1).astype(BF16)


def _attn_call(sink_l, q, k, v):
    nb = S // ATTN_BLK
    smem = pl.BlockSpec(memory_space=pltpu.SMEM)
    qspec = pl.BlockSpec((ATTN_BLK, Q_W), lambda b, n: (b * nb + n, 0))
    prev = pl.BlockSpec((ATTN_BLK, KV_W), lambda b, n: (b * nb + jnp.maximum(n - 1, 0), 0))
    cur = pl.BlockSpec((ATTN_BLK, KV_W), lambda b, n: (b * nb + n, 0))
    nxt = pl.BlockSpec((ATTN_BLK, KV_W), lambda b, n: (b * nb + jnp.minimum(n + 1, nb - 1), 0))
    ctxs = pl.BlockSpec((C, KV_W), lambda b, n: (N_LAT // C + b, 0))
    lat = pl.pallas_call(
        functools.partial(_attn_kernel, band=True),
        out_shape=jax.ShapeDtypeStruct((N_LAT, Q_W), BF16),
        grid=(B, nb),
        in_specs=[smem, qspec, prev, cur, nxt, prev, cur, nxt, ctxs, ctxs],
        out_specs=qspec,
        compiler_params=_cparams(("parallel", "parallel")),
        name="attn_latent",
    )(sink_l, q, k, k, k, v, v, v, k, v)
    ncb = C // ATTN_BLK
    base = N_LAT // ATTN_BLK
    ctx = pl.pallas_call(
        functools.partial(_attn_kernel, band=False),
        out_shape=jax.ShapeDtypeStruct((N_CTX, Q_W), BF16),
        grid=(B, ncb),
        in_specs=[smem, pl.BlockSpec((ATTN_BLK, Q_W), lambda b, n: (base + b * ncb + n, 0)), ctxs, ctxs],
        out_specs=pl.BlockSpec((ATTN_BLK, Q_W), lambda b, n: (b * ncb + n, 0)),
        compiler_params=_cparams(("parallel", "parallel")),
        name="attn_context",
    )(sink_l, q, k, v)
    return lat, ctx


def _f1_kernel(w_ref, f_ref, re_ref, im_ref):
    res = _dot(w_ref[...], f_ref[...])
    re_ref[...] = res[:FS1].astype(BF16)
    im_ref[...] = res[FS1:].astype(BF16)


def _f2_kernel(ta_ref, tb_ref, re_ref, im_ref, o_ref):
    res = _dot(ta_ref[...], re_ref[...]) + _dot(tb_ref[...], im_ref[...])
    o_ref[:, 0:FOURIER_W] = res[:FS2].astype(BF16)
    o_ref[:, FOURIER_W:2 * FOURIER_W] = res[FS2:].astype(BF16)


def _fc_kernel(w_ref, f_ref, o_ref):
    res = _dot(w_ref[...], f_ref[...])
    o_ref[:, 0:FOURIER_W] = res[:C].astype(BF16)
    o_ref[:, FOURIER_W:2 * FOURIER_W] = res[C:].astype(BF16)


def _fourier_call(f, tabs):
    w1, ta, tb, wc = tabs
    f2d = f.reshape(N_TOK // FS2, F1_COLS)
    nchunk = F1_COLS // F1_CW
    a_re, a_im = pl.pallas_call(
        _f1_kernel,
        out_shape=[jax.ShapeDtypeStruct((B * FS1, F1_COLS), BF16)] * 2,
        grid=(B, nchunk),
        in_specs=[_const_spec((2 * FS1, FS1)),
                  pl.BlockSpec((FS1, F1_CW), lambda b, j: (b, j))],
        out_specs=[pl.BlockSpec((FS1, F1_CW), lambda b, j: (b, j))] * 2,
        compiler_params=_cparams(("parallel", "parallel")),
        name="fourier_stage1",
    )(w1, f2d)
    a_re = a_re.reshape(B * FS1, FS2, FOURIER_W)
    a_im = a_im.reshape(B * FS1, FS2, FOURIER_W)
    aspec = pl.BlockSpec((None, FS2, FOURIER_W), lambda b, k1: (b * FS1 + k1, 0, 0))
    tspec = pl.BlockSpec((None, 2 * FS2, FS2), lambda b, k1: (k1, 0, 0))
    h_t = pl.pallas_call(
        _f2_kernel,
        out_shape=jax.ShapeDtypeStruct((B, FS1, FS2, 2 * FOURIER_W), BF16),
        grid=(B, FS1),
        in_specs=[tspec, tspec, aspec, aspec],
        out_specs=pl.BlockSpec((None, None, FS2, 2 * FOURIER_W), lambda b, k1: (b, k1, 0, 0)),
        compiler_params=_cparams(("parallel", "parallel")),
        name="fourier_stage2",
    )(ta, tb, a_re, a_im)
    h_lat = jnp.transpose(h_t, (0, 2, 1, 3)).reshape(N_LAT, 2 * FOURIER_W)
    h_ctx = pl.pallas_call(
        _fc_kernel,
        out_shape=jax.ShapeDtypeStruct((N_CTX, 2 * FOURIER_W), BF16),
        grid=(B,),
        in_specs=[_const_spec((2 * C, C)),
                  pl.BlockSpec((C, FOURIER_W), lambda b: (N_LAT // C + b, 0))],
        out_specs=pl.BlockSpec((C, 2 * FOURIER_W), lambda b: (b, 0)),
        compiler_params=_cparams(("parallel",)),
        name="fourier_context",
    )(wc, f)
    return h_lat, h_ctx


def _poolconv_kernel(pc_ref, pp_ref, pn_ref, uc_ref, up_ref, un_ref, bm_ref, bh_ref, pw_ref, ps_ref,
                     cw_ref, cb_ref, cg_ref, cnb_ref, z_ref, a_ref, ext_ref, cv_ref):
    t = pl.program_id(0)
    lat_tiles = N_LAT // TP
    per_seq = S // TP
    is_ctx = t >= lat_tiles
    first = jnp.logical_or(t % per_seq == 0, is_ctx)
    last = jnp.logical_or(t % per_seq == per_seq - 1, is_ctx)
    seq_len = jnp.where(is_ctx, C, S)
    pos0 = jnp.where(is_ctx, 0, (t % per_seq) * TP)

    pcur = pc_ref[...]
    keep_prev = jnp.where(first, 0.0, 1.0)
    keep_next = jnp.where(last, 0.0, 1.0)
    halo = jnp.concatenate([pp_ref[...].astype(F32) * keep_prev,
                            pn_ref[...].astype(F32) * keep_next], axis=0).astype(BF16)
    lane = lax.broadcasted_iota(jnp.int32, (TP, POOL_W), 1)
    pos = lax.broadcasted_iota(jnp.int32, (TP, POOL_W), 0) + pos0
    zsum = None
    win = None
    for gi, w in enumerate(POOL_WINDOWS):
        sg = _dot(bm_ref[gi], pcur) + _dot(bh_ref[gi], halo)
        if gi == 0:
            zsum, win = sg, jnp.full((TP, POOL_W), w, jnp.int32)
        else:
            sel = lane >= gi * POOL_GROUP_W
            zsum = jnp.where(sel, sg, zsum)
            win = jnp.where(sel, w, win)
    half = win // 2
    lo = jnp.clip(pos - half, 0, seq_len)
    hi = jnp.clip(pos - half + win, 0, seq_len)
    z = zsum / (hi - lo).astype(F32) - pcur.astype(F32)
    z_ref[...] = (_dot(z.astype(BF16), pw_ref[...]) * ps_ref[...]).astype(BF16)

    ext_ref[0:HALO, :] = up_ref[...].astype(F32) * keep_prev
    ext_ref[HALO:HALO + TP, :] = uc_ref[...].astype(F32)
    ext_ref[HALO + TP:, :] = un_ref[...].astype(F32) * keep_next
    off = HALO - CONV_K // 2
    for cb in range(CONV_W // LANES):
        cs = slice(cb * LANES, (cb + 1) * LANES)
        acc = jnp.zeros((TP, LANES), F32) + cb_ref[:, cs]
        for j in range(CONV_K):
            acc = acc + ext_ref[pl.ds(off + j, TP), cs] * cw_ref[j:j + 1, cs]
        cv_ref[:, cs] = acc
    cv = cv_ref[...]
    mu = jnp.mean(cv, axis=-1, keepdims=True)
    var = jnp.mean(jnp.square(cv - mu), axis=-1, keepdims=True)
    un = (cv - mu) * lax.rsqrt(var + EPS) * cg_ref[...] + cnb_ref[...]
    a_ref[...] = (un * _sigmoid(un)).astype(BF16)


def _poolconv_call(p, u, band_main, band_halo, pw_bd, pool_scale, conv_w, conv_b, cn_g, cn_b):
    nh = TP // HALO
    last_h = N_TOK // HALO - 1
    cur = lambda w: pl.BlockSpec((TP, w), lambda t: (t, 0))
    prv = lambda w: pl.BlockSpec((HALO, w), lambda t: (jnp.maximum(t * nh - 1, 0), 0))
    nxt = lambda w: pl.BlockSpec((HALO, w), lambda t: (jnp.minimum((t + 1) * nh, last_h), 0))
    return pl.pallas_call(
        _poolconv_kernel,
        out_shape=[jax.ShapeDtypeStruct((N_TOK, POOL_W), BF16),
                   jax.ShapeDtypeStruct((N_TOK, CONV_W), BF16)],
        grid=(NTP,),
        in_specs=[cur(POOL_W), prv(POOL_W), nxt(POOL_W), cur(CONV_W), prv(CONV_W), nxt(CONV_W),
                  _const_spec((4, TP, TP)), _const_spec((4, TP, 2 * HALO)),
                  _const_spec((POOL_W, POOL_W)), _const_spec((1, POOL_W)),
                  _const_spec((CONV_K, CONV_W)), _const_spec((1, CONV_W)),
                  _const_spec((1, CONV_W)), _const_spec((1, CONV_W))],
        out_specs=[cur(POOL_W), cur(CONV_W)],
        scratch_shapes=[pltpu.VMEM((TP + 2 * HALO, CONV_W), F32), pltpu.VMEM((TP, CONV_W), F32)],
        compiler_params=_cparams(("parallel",)),
        name="pool_conv",
    )(p, p, p, u, u, u, band_main, band_halo, pw_bd, pool_scale, conv_w, conv_b, cn_g, cn_b)


def _route(logits):
    lane = lax.broadcasted_iota(jnp.int32, logits.shape, 1)
    big = jnp.int32(LANES)
    lg = jnp.where(lane < N_GROUPS, logits, NEG_INF)
    mg = jnp.max(lg, axis=-1, keepdims=True)
    grp = jnp.min(jnp.where(lg == mg, lane, big), axis=-1, keepdims=True)
    p_grp = 1.0 / jnp.sum(jnp.exp(lg - mg), axis=-1, keepdims=True)
    lo = N_GROUPS + grp * EPG
    le = jnp.where((lane >= lo) & (lane < lo + EPG), logits, NEG_INF)
    m1 = jnp.max(le, axis=-1, keepdims=True)
    i1 = jnp.min(jnp.where(le == m1, lane, big), axis=-1, keepdims=True)
    le2 = jnp.where(lane == i1, NEG_INF, le)
    m2 = jnp.max(le2, axis=-1, keepdims=True)
    i2 = jnp.min(jnp.where(le2 == m2, lane, big), axis=-1, keepdims=True)
    r = jnp.exp(m2 - m1)
    w1 = p_grp / (1.0 + r)
    w2 = p_grp * r / (1.0 + r)
    e1 = (i1 - N_GROUPS).astype(F32)
    e2 = (i2 - N_GROUPS).astype(F32)
    return jnp.where(lane == 0, e1, jnp.where(lane == 1, e2, jnp.where(lane == 2, w1,
                     jnp.where(lane == 3, w2, 0.0))))


def _mix_kernel(x_ref, mod_ref, gn_ref, at_ref, hl_ref, hc_ref, z_ref, cv_ref,
                wa_ref, wf_ref, wp_ref, wc_ref, wg_ref, bg_ref, wo_ref, gf_ref, rh_ref, rl_ref, rb_ref,
                xo_ref, h2_ref, rt_ref):
    i = pl.program_id(0)
    m = mod_ref[...]
    x = x_ref[...]
    hb = _modulate(x, gn_ref[...], m[0:1], m[1:2]).astype(BF16)
    four = jnp.where(i >= LAT_TILES, hc_ref[...], hl_ref[...])
    branches = ((at_ref[...], wa_ref), (four, wf_ref), (z_ref[...], wp_ref), (cv_ref[...], wc_ref))
    acc = None
    for bi, (inp, w_ref) in enumerate(branches):
        cs = slice(bi * D, (bi + 1) * D)
        gate = _sigmoid(_dot(hb, wg_ref[:, cs]) + bg_ref[:, cs])
        term = gate * _dot(inp, w_ref[...])
        acc = term if acc is None else acc + term
    x_new = x + m[2:3] * _dot(acc.astype(BF16), wo_ref[...])
    xo_ref[...] = x_new
    h2 = _modulate(x_new, gf_ref[...], m[3:4], m[4:5])
    h2_ref[...] = h2
    hi = h2.astype(BF16)
    lo = (h2 - hi.astype(F32)).astype(BF16)
    logits = _dot(hi, rh_ref[...]) + _dot(lo, rh_ref[...]) + _dot(hi, rl_ref[...]) + rb_ref[...]
    rt_ref[...] = _route(logits)


def _mix_call(x, mods_l, gn, attn, h_lat, h_ctx, zc, cact, wts):
    tok = lambda w: pl.BlockSpec((TM, w), lambda i: (i, 0))
    in_specs = [tok(D), pl.BlockSpec((None, 6, D), lambda i: (_mod_row(i), 0, 0)), _const_spec((1, D)),
                tok(Q_W),
                pl.BlockSpec((TM, 2 * FOURIER_W), lambda i: (jnp.minimum(i, LAT_TILES - 1), 0)),
                _const_spec((N_CTX, 2 * FOURIER_W)),
                tok(POOL_W), tok(CONV_W)]
    in_specs += [_const_spec(w.shape) for w in wts]
    return pl.pallas_call(
        _mix_kernel,
        out_shape=[jax.ShapeDtypeStruct((N_TOK, D), F32), jax.ShapeDtypeStruct((N_TOK, D), F32),
                   jax.ShapeDtypeStruct((N_TOK, LANES), F32)],
        grid=(NT,),
        in_specs=in_specs,
        out_specs=[tok(D), tok(D), tok(LANES)],
        compiler_params=_cparams(("parallel",)),
        name="mix",
    )(x, mods_l, gn, attn, h_lat, h_ctx, zc, cact, *wts)


def _onehots(route):
    lane = lax.broadcasted_iota(jnp.int32, route.shape, 1)
    e1 = route[:, 0:1].astype(jnp.int32)
    e2 = route[:, 1:2].astype(jnp.int32)
    return (lane == e1).astype(F32), (lane == e2).astype(F32)


def _rank_kernel(rt_ref, tri_ref, rk_ref, cnt_ref, carry_ref):
    i = pl.program_id(0)

    @pl.when(i == 0)
    def _():
        carry_ref[...] = jnp.zeros_like(carry_ref)

    oh1, oh2 = _onehots(rt_ref[...])
    both = oh1 + oh2
    carry = carry_ref[0:1, :]
    before = _dot(tri_ref[...], both.astype(BF16)) + carry
    r1 = jnp.sum(oh1 * before, axis=-1, keepdims=True)
    r2 = jnp.sum(oh2 * before, axis=-1, keepdims=True)
    lane = lax.broadcasted_iota(jnp.int32, both.shape, 1)
    rk_ref[...] = jnp.where(lane == 0, r1, jnp.where(lane == 1, r2, 0.0))
    total = carry + jnp.sum(both, axis=0, keepdims=True)
    carry_ref[...] = jnp.broadcast_to(total, carry_ref.shape)
    cnt_ref[...] = jnp.broadcast_to(total, cnt_ref.shape)


def _padded_ends(counts_row):
    lane = lax.broadcasted_iota(jnp.int32, counts_row.shape, 1)
    padded = jnp.floor((counts_row + (MOE_BLK - 1)) * (1.0 / MOE_BLK)) * MOE_BLK
    ends = padded
    sh = 1
    while sh < N_EXPERTS:
        ends = ends + jnp.where(lane >= sh, pltpu.roll(ends, sh, 1), 0.0)
        sh *= 2
    return padded, ends


def _dest_kernel(rt_ref, rk_ref, cnt_ref, dst_ref, be_ref):
    padded, ends = _padded_ends(cnt_ref[0:1, :])
    starts = ends - padded
    oh1, oh2 = _onehots(rt_ref[...])
    rk = rk_ref[...]
    d1 = jnp.sum(oh1 * starts, axis=-1, keepdims=True) + rk[:, 0:1]
    d2 = jnp.sum(oh2 * starts, axis=-1, keepdims=True) + rk[:, 1:2]
    lane = lax.broadcasted_iota(jnp.int32, rk.shape, 1)
    dst_ref[...] = jnp.where(lane == 0, d1, jnp.where(lane == 1, d2, 0.0)).astype(jnp.int32)
    blk = lax.broadcasted_iota(jnp.int32, be_ref.shape, 0).astype(F32) * MOE_BLK
    lane_b = lax.broadcasted_iota(jnp.int32, be_ref.shape, 1)
    done = jnp.where((ends <= blk) & (lane_b < N_EXPERTS), 1.0, 0.0)
    be = jnp.minimum(jnp.sum(done, axis=-1, keepdims=True), N_EXPERTS - 1.0)
    nblk = jnp.max(jnp.where(lane_b == N_EXPERTS - 1, ends, 0.0), axis=-1, keepdims=True) * (1.0 / MOE_BLK)
    be_ref[...] = jnp.where(lane_b == 0, be, jnp.where(lane_b == 1, nblk, 0.0)).astype(jnp.int32)


def _plan_call(route, tri):
    tok = pl.BlockSpec((TM, LANES), lambda i: (i, 0))
    ranks, counts = pl.pallas_call(
        _rank_kernel,
        out_shape=[jax.ShapeDtypeStruct((N_TOK, LANES), F32), jax.ShapeDtypeStruct((8, LANES), F32)],
        grid=(NT,),
        in_specs=[tok, _const_spec((TM, TM))],
        out_specs=[tok, pl.BlockSpec((8, LANES), lambda i: (0, 0))],
        scratch_shapes=[pltpu.VMEM((8, LANES), F32)],
        compiler_params=_cparams(("arbitrary",)),
        name="moe_rank",
    )(route, tri)
    dest, blk = pl.pallas_call(
        _dest_kernel,
        out_shape=[jax.ShapeDtypeStruct((N_TOK, LANES), jnp.int32),
                   jax.ShapeDtypeStruct((256, LANES), jnp.int32)],
        grid=(NT,),
        in_specs=[tok, tok, pl.BlockSpec((8, LANES), lambda i: (0, 0))],
        out_specs=[tok, pl.BlockSpec((256, LANES), lambda i: (0, 0))],
        compiler_params=_cparams(("arbitrary",)),
        name="moe_dest",
    )(route, ranks, counts)
    dest_flat = dest[:, 0:TOP_K].reshape(NT, 1, TM * TOP_K)
    blk_e = blk[:N_MOE_BLOCKS, 0]
    n_used = blk[0:1, 1]
    return dest_flat, blk_e, n_used


def _dispatch_kernel(dst_ref, h2_ref, xs_in_ref, xs_ref, sem):
    del xs_in_ref

    def row_copy(r, k):
        return pltpu.make_async_copy(h2_ref.at[pl.ds(r, 1), :],
                                     xs_ref.at[pl.ds(dst_ref[0, TOP_K * r + k], 1), :], sem)

    def start(r, carry):
        for k in range(TOP_K):
            row_copy(r, k).start()
        return carry

    def wait(r, carry):
        for k in range(TOP_K):
            row_copy(r, k).wait()
        return carry

    lax.fori_loop(0, TM, start, 0)
    lax.fori_loop(0, TM, wait, 0)


def _dispatch_call(dest_flat, h2, zeros_slots):
    return pl.pallas_call(
        _dispatch_kernel,
        out_shape=jax.ShapeDtypeStruct((N_SLOTS, D), F32),
        grid=(NT,),
        in_specs=[pl.BlockSpec((None, 1, TM * TOP_K), lambda i: (i, 0, 0), memory_space=pltpu.SMEM),
                  pl.BlockSpec((TM, D), lambda i: (i, 0)),
                  pl.BlockSpec(memory_space=pl.ANY)],
        out_specs=pl.BlockSpec(memory_space=pl.ANY),
        scratch_shapes=[pltpu.SemaphoreType.DMA],
        input_output_aliases={2: 0},
        compiler_params=_cparams(("arbitrary",)),
        name="moe_dispatch",
    )(dest_flat, h2, zeros_slots)


def _expert_kernel(be_ref, nu_ref, xs_ref, wg_ref, wu_ref, wd_ref, ys_ref):
    del be_ref
    b = pl.program_id(0)

    @pl.when(b < nu_ref[0])
    def _():
        xb = xs_ref[...].astype(BF16)
        g = _dot(xb, wg_ref[...])
        u = _dot(xb, wu_ref[...])
        hmid = (g * _sigmoid(g)) * u
        ys_ref[...] = _dot(hmid.astype(BF16), wd_ref[...])

    @pl.when(b >= nu_ref[0])
    def _():
        ys_ref[...] = jnp.zeros_like(ys_ref)


def _expert_call(blk_e, n_used, xs, wg, wu, wd):
    wspec = lambda k, n: pl.BlockSpec((None, k, n), lambda b, be, nu: (be[b], 0, 0))
    return pl.pallas_call(
        _expert_kernel,
        out_shape=jax.ShapeDtypeStruct((N_SLOTS, D), F32),
        grid_spec=pltpu.PrefetchScalarGridSpec(
            num_scalar_prefetch=2,
            grid=(N_MOE_BLOCKS,),
            in_specs=[pl.BlockSpec((MOE_BLK, D), lambda b, be, nu: (jnp.minimum(b, nu[0] - 1), 0)),
                      wspec(D, EXPERT_HIDDEN), wspec(D, EXPERT_HIDDEN), wspec(EXPERT_HIDDEN, D)],
            out_specs=pl.BlockSpec((MOE_BLK, D), lambda b, be, nu: (b, 0))),
        compiler_params=_cparams(("arbitrary",)),
        name="moe_experts",
    )(blk_e, n_used, xs, wg, wu, wd)


def _combine_kernel(dst_ref, ys_ref, x_ref, rt_ref, mod_ref, o_ref, buf1, buf2, sem):
    bufs = (buf1, buf2)

    def row_copy(r, k):
        return pltpu.make_async_copy(ys_ref.at[pl.ds(dst_ref[0, TOP_K * r + k], 1), :],
                                     bufs[k].at[pl.ds(r, 1), :], sem)

    def start(r, carry):
        for k in range(TOP_K):
            row_copy(r, k).start()
        return carry

    def wait(r, carry):
        for k in range(TOP_K):
            row_copy(r, k).wait()
        return carry

    lax.fori_loop(0, TM, start, 0)
    lax.fori_loop(0, TM, wait, 0)
    rt = rt_ref[...]
    y = rt[:, 2:3] * buf1[...] + rt[:, 3:4] * buf2[...]
    o_ref[...] = x_ref[...] + mod_ref[5:6, :] * y


def _combine_call(dest_flat, ys, x, route, mods_l):
    tok = lambda w: pl.BlockSpec((TM, w), lambda i: (i, 0))
    return pl.pallas_call(
        _combine_kernel,
        out_shape=jax.ShapeDtypeStruct((N_TOK, D), F32),
        grid=(NT,),
        in_specs=[pl.BlockSpec((None, 1, TM * TOP_K), lambda i: (i, 0, 0), memory_space=pltpu.SMEM),
                  pl.BlockSpec(memory_space=pl.ANY),
                  tok(D), tok(LANES),
                  pl.BlockSpec((None, 6, D), lambda i: (_mod_row(i), 0, 0))],
        out_specs=tok(D),
        scratch_shapes=[pltpu.VMEM((TM, D), F32), pltpu.VMEM((TM, D), F32), pltpu.SemaphoreType.DMA],
        compiler_params=_cparams(("arbitrary",)),
        name="moe_combine",
    )(dest_flat, ys, x, route, mods_l)


def _rope_tables():
    nf = HEAD_DIM // 4
    inv = ROPE_BASE ** (-jnp.arange(nf, dtype=F32) / nf)
    t = jnp.arange(S)
    row = (t // GRID_W).astype(F32)[:, None] * inv[None, :]
    col = (t % GRID_W).astype(F32)[:, None] * inv[None, :]
    zero = jnp.zeros_like(row)
    cos = jnp.concatenate([jnp.cos(row), jnp.cos(row), jnp.cos(col), jnp.cos(col)], axis=1)
    sa = jnp.concatenate([-jnp.sin(row), zero, -jnp.sin(col), zero], axis=1)
    sb = jnp.concatenate([zero, jnp.sin(row), zero, jnp.sin(col)], axis=1)
    ident = (jnp.ones((TM, HEAD_DIM), F32), jnp.zeros((TM, HEAD_DIM), F32), jnp.zeros((TM, HEAD_DIM), F32))
    return tuple(jnp.tile(jnp.concatenate([a, b], axis=0), (1, LANES // HEAD_DIM))
                 for a, b in zip((cos, sa, sb), ident))


def _fourier_tables():
    s1 = np.arange(FS1)
    ang1 = 2.0 * np.pi * np.outer(s1, s1) / FS1
    w1 = np.concatenate([np.cos(ang1), -np.sin(ang1)], axis=0) / np.sqrt(S)
    k1 = np.arange(FS1)[:, None, None]
    k2 = np.arange(FS2)[None, :, None]
    s2 = np.arange(FS2)[None, None, :]
    ang2 = 2.0 * np.pi * ((k1 + FS1 * k2) * s2 % S) / S
    c2, sn2 = np.cos(ang2), np.sin(ang2)
    ta = np.concatenate([c2, -sn2], axis=1)
    tb = np.concatenate([sn2, c2], axis=1)
    sc = np.arange(C)
    angc = 2.0 * np.pi * np.outer(sc, sc) / C
    wc = np.concatenate([np.cos(angc), -np.sin(angc)], axis=0) / np.sqrt(C)
    return tuple(jnp.asarray(a, F32).astype(BF16) for a in (w1, ta, tb, wc))


def _channel_dft():
    cidx = np.arange(FOURIER_GROUP_W)
    ang = 2.0 * np.pi * np.outer(cidx, cidx) / FOURIER_GROUP_W
    eye = np.eye(FOURIER_W // FOURIER_GROUP_W)
    cw = np.kron(eye, np.cos(ang)) / np.sqrt(FOURIER_GROUP_W)
    sw = np.kron(eye, np.sin(ang)) / np.sqrt(FOURIER_GROUP_W)
    return jnp.asarray(np.concatenate([cw, sw], axis=0), F32)


def _pool_bands():
    t = np.arange(TP)[:, None]
    main, halo = [], []
    for w in POOL_WINDOWS:
        def hit(j):
            return ((j - t >= -(w // 2)) & (j - t <= w // 2 - 1)).astype(np.float32)
        main.append(hit(np.arange(TP)[None, :]))
        halo.append(np.concatenate([hit(np.arange(-HALO, 0)[None, :]),
                                    hit(np.arange(TP, TP + HALO)[None, :])], axis=1))
    return (jnp.asarray(np.stack(main), F32).astype(BF16), jnp.asarray(np.stack(halo), F32).astype(BF16))


def _fold_kernel(a_ref, b_ref, o_ref):
    o_ref[...] = jnp.dot(a_ref[...], b_ref[...], preferred_element_type=F32,
                         precision=lax.Precision.HIGHEST).astype(BF16)


def _fold_fourier_weights(dftw, w_br_fourier):
    nl = w_br_fourier.shape[0]
    return pl.pallas_call(
        _fold_kernel,
        out_shape=jax.ShapeDtypeStruct((nl, 2 * FOURIER_W, D), BF16),
        grid=(nl,),
        in_specs=[pl.BlockSpec((2 * FOURIER_W, FOURIER_W), lambda l: (0, 0)),
                  pl.BlockSpec((None, FOURIER_W, D), lambda l: (l, 0, 0))],
        out_specs=pl.BlockSpec((None, 2 * FOURIER_W, D), lambda l: (l, 0, 0)),
        compiler_params=_cparams(("arbitrary",)),
        name="fold_fourier_proj",
    )(dftw, w_br_fourier)


def _block_diag(blocks):
    n, r, c = blocks.shape
    out = jnp.zeros((n * r, n * c), blocks.dtype)
    for i in range(n):
        out = lax.dynamic_update_slice(out, blocks[i], (i * r, i * c))
    return out


def kernel(x, c, ctx, c_ctx, w_ada, b_ada, g_norm_mix, g_norm_ffn, w_in, g_q, g_k, sink, w_br_attn,
           w_br_fourier, pool_w, pool_scale, w_br_pool, conv_w, conv_b, cn_g, cn_b, w_br_conv, w_gate,
           b_gate, w_out, w_router_grp, b_router_grp, w_router_exp, b_router_exp, w_e_gate, w_e_up,
           w_e_down):
    xs = jnp.concatenate([x.reshape(N_LAT, D), ctx.reshape(N_CTX, D)], axis=0)
    nl = w_ada.shape[0]
    mods = _ada_all(c, c_ctx, w_ada, b_ada).reshape(nl, 8, 6, D)
    rope_tabs = _rope_tables()
    four_tabs = _fourier_tables()
    band_main, band_halo = _pool_bands()
    wf_all = _fold_fourier_weights(_channel_dft(), w_br_fourier)
    bd = jnp.asarray(np.kron(np.eye(LANES // HEAD_DIM), np.ones((HEAD_DIM, HEAD_DIM))), F32).astype(BF16)
    tri = jnp.asarray(np.tril(np.ones((TM, TM)), -1), F32).astype(BF16)
    zeros_slots = jnp.zeros((N_SLOTS, D), F32)
    rpad = jnp.zeros((nl, D, LANES - N_GROUPS - N_EXPERTS), F32)
    w_router = jnp.concatenate([w_router_grp, w_router_exp, rpad], axis=-1)
    r_hi = w_router.astype(BF16)
    r_lo = (w_router - r_hi.astype(F32)).astype(BF16)
    r_b = jnp.concatenate([b_router_grp, b_router_exp, rpad[:, 0, :]], axis=-1).reshape(nl, 1, LANES)

    for l in range(nl):
        mods_l = mods[l]
        q, k, v, f, p, u = _proj_call(xs, mods_l, g_norm_mix[l].reshape(1, D), w_in[l].astype(BF16),
                                      rope_tabs, jnp.tile(g_q[l], 2).reshape(1, LANES),
                                      jnp.tile(g_k[l], 2).reshape(1, LANES), bd)
        a_lat, a_ctx = _attn_call(sink[l], q, k, v)
        attn = jnp.concatenate([a_lat, a_ctx], axis=0)
        h_lat, h_ctx = _fourier_call(f, four_tabs)
        zc, cact = _poolconv_call(p, u, band_main, band_halo, _block_diag(pool_w[l]).astype(BF16),
                                  pool_scale[l].reshape(1, POOL_W), conv_w[l], conv_b[l].reshape(1, CONV_W),
                                  cn_g[l].reshape(1, CONV_W), cn_b[l].reshape(1, CONV_W))
        wts = (w_br_attn[l].astype(BF16), wf_all[l], w_br_pool[l].astype(BF16), w_br_conv[l].astype(BF16),
               w_gate[l].astype(BF16), b_gate[l].reshape(1, 4 * D), w_out[l].astype(BF16),
               g_norm_ffn[l].reshape(1, D), r_hi[l], r_lo[l], r_b[l])
        xs, h2, route = _mix_call(xs, mods_l, g_norm_mix[l].reshape(1, D), attn, h_lat, h_ctx, zc, cact, wts)
        dest_flat, blk_e, n_used = _plan_call(route, tri)
        slots = _dispatch_call(dest_flat, h2, zeros_slots)
        ys = _expert_call(blk_e, n_used, slots, w_e_gate[l].astype(BF16), w_e_up[l].astype(BF16),
                          w_e_down[l].astype(BF16))
        xs = _combine_call(dest_flat, ys, xs, route, mods_l)
    return xs[:N_LAT].reshape(B, S, D)
```

```python
import functools

import numpy as np
import jax
import jax.numpy as jnp
from jax import lax
from jax.experimental import pallas as pl
from jax.experimental.pallas import tpu as pltpu

F32 = jnp.float32
BF16 = jnp.bfloat16

D = 1024
B = 2
S = 8192
C = 256
GRID_W = 64
HEAD_DIM = 64
N_Q_HEADS = 8
N_KV_HEADS = 2
GQA = N_Q_HEADS // N_KV_HEADS
WINDOW = 128
ATTN_BLK = 128
ROPE_BASE = 10000.0
Q_W = 512
KV_W = 128
FOURIER_W = 640
FOURIER_GROUP_W = 160
POOL_W = 640
POOL_GROUP_W = 160
POOL_WINDOWS = (2, 4, 8, 16)
CONV_W = 512
CONV_K = 31
PROJ_W = 3072
N_GROUPS = 4
EPG = 8
N_EXPERTS = 32
TOP_K = 2
EXPERT_HIDDEN = 512
MOE_BLK = 256
EPS = 1e-6
NEG_INF = -1e30
LOG2E = 1.4426950408889634

N_LAT = B * S
N_CTX = B * C
N_TOK = N_LAT + N_CTX
TM = 512
NT = N_TOK // TM
LAT_TILES = N_LAT // TM
TILES_PER_BATCH = S // TM
TP = 256
NTP = N_TOK // TP
HALO = 16
N_ASSIGN = N_TOK * TOP_K
N_MOE_BLOCKS = -(-N_ASSIGN // MOE_BLK) + N_EXPERTS
N_SLOTS = N_MOE_BLOCKS * MOE_BLK
FS1 = 64
FS2 = 128
F1_COLS = FS2 * FOURIER_W
F1_CW = 8192
LANES = 128
VMEM_LIMIT = 56 * 1024 * 1024


def _cparams(sem, vmem=VMEM_LIMIT):
    return pltpu.CompilerParams(dimension_semantics=sem, vmem_limit_bytes=vmem)


def _const_spec(shape):
    nd = len(shape)
    return pl.BlockSpec(shape, lambda *_: (0,) * nd, pipeline_mode=pl.Buffered(1))


def _dot(a, b):
    return jnp.dot(a, b, preferred_element_type=F32)


def _split_dot(a, b_bf16):
    hi = a.astype(BF16)
    lo = (a - hi.astype(F32)).astype(BF16)
    return _dot(hi, b_bf16) + _dot(lo, b_bf16)


def _modulate(x, g, shift, scale):
    y = x * lax.rsqrt(jnp.mean(x * x, axis=-1, keepdims=True) + EPS)
    return (y * g) * (1.0 + scale) + shift


def _sigmoid(x):
    return 1.0 / (1.0 + jnp.exp(-x))


def _ada_kernel(ct_ref, w_ref, b_ref, o_ref):
    ct = ct_ref[...]
    s = ct * _sigmoid(ct)
    w = w_ref[...]
    rows = [jnp.sum(w * s[:, r:r + 1], axis=0, keepdims=True) for r in range(3)]
    rows.append(jnp.zeros((5, w.shape[1]), F32))
    o_ref[...] = jnp.concatenate(rows, axis=0) + b_ref[...]


def _ada_all(c, c_ctx, w_ada, b_ada):
    ct = jnp.concatenate([c, c_ctx[None, :], jnp.zeros((5, D), F32)], axis=0).T
    cols = 1536
    nl = w_ada.shape[0]
    return pl.pallas_call(
        _ada_kernel,
        out_shape=jax.ShapeDtypeStruct((nl, 8, 6 * D), F32),
        grid=(nl, 6 * D // cols),
        in_specs=[pl.BlockSpec((D, 8), lambda l, j: (0, 0)),
                  pl.BlockSpec((None, D, cols), lambda l, j: (l, 0, j)),
                  pl.BlockSpec((None, 1, cols), lambda l, j: (l, 0, j))],
        out_specs=pl.BlockSpec((None, 8, cols), lambda l, j: (l, 0, j)),
        compiler_params=_cparams(("arbitrary", "arbitrary")),
        name="adaln",
    )(ct, w_ada, b_ada.reshape(nl, 1, 6 * D))


def _head_rms(t, g128, bd):
    outs = []
    for j in range(t.shape[1] // LANES):
        blk = t[:, j * LANES:(j + 1) * LANES]
        ss = _split_dot(blk * blk, bd)
        outs.append(blk * lax.rsqrt(ss * (1.0 / HEAD_DIM) + EPS) * g128)
    return outs


def _rope(blocks, cos, sa, sb):
    outs = []
    for blk in blocks:
        up = pltpu.roll(blk, LANES - 16, 1)
        dn = pltpu.roll(blk, 16, 1)
        outs.append(blk * cos + up * sa + dn * sb)
    return outs


def _proj_kernel(x_ref, mod_ref, gn_ref, w_ref, cos_ref, sa_ref, sb_ref, gq_ref, gk_ref, bd_ref,
                 q_ref, k_ref, v_ref, f_ref, p_ref, u_ref, wbf_ref):
    @pl.when(pl.program_id(0) == 0)
    def _():
        wbf_ref[...] = w_ref[...].astype(BF16)

    m = mod_ref[...]
    h = _modulate(x_ref[...], gn_ref[...], m[0:1], m[1:2])
    proj = _dot(h.astype(BF16), wbf_ref[...])
    cos, sa, sb, bd = cos_ref[...], sa_ref[...], sb_ref[...], bd_ref[...]
    q = _rope(_head_rms(proj[:, 0:Q_W], gq_ref[...], bd), cos, sa, sb)
    q_ref[...] = (jnp.concatenate(q, axis=1) * (LOG2E * HEAD_DIM ** -0.5)).astype(BF16)
    k = _rope(_head_rms(proj[:, Q_W:Q_W + KV_W], gk_ref[...], bd), cos, sa, sb)
    k_ref[...] = k[0].astype(BF16)
    o = Q_W + KV_W
    v_ref[...] = proj[:, o:o + KV_W].astype(BF16)
    o += KV_W
    f_ref[...] = proj[:, o:o + FOURIER_W].astype(BF16)
    o += FOURIER_W
    p_ref[...] = proj[:, o:o + POOL_W].astype(BF16)
    o += POOL_W
    a = proj[:, o:o + CONV_W]
    g = proj[:, o + CONV_W:o + 2 * CONV_W]
    u_ref[...] = (a * _sigmoid(g)).astype(BF16)


def _mod_row(i):
    return jnp.minimum(i // TILES_PER_BATCH, 2)


def _layer_spec(shape, l):
    nd = len(shape)
    return pl.BlockSpec((None,) + tuple(shape), lambda *_: (l,) + (0,) * nd, pipeline_mode=pl.Buffered(1))


def _proj_call(x, mods_l, gn, w_in, l, rope_tabs, gq128, gk128, bd):
    cos, sa, sb = rope_tabs
    tok = lambda w: pl.BlockSpec((TM, w), lambda i: (i, 0))
    rope_spec = pl.BlockSpec((TM, LANES), lambda i: (jnp.where(i < LAT_TILES, i % TILES_PER_BATCH,
                                                               TILES_PER_BATCH), 0))
    widths = (Q_W, KV_W, KV_W, FOURIER_W, POOL_W, CONV_W)
    return pl.pallas_call(
        _proj_kernel,
        out_shape=[jax.ShapeDtypeStruct((N_TOK, w), BF16) for w in widths],
        grid=(NT,),
        in_specs=[tok(D),
                  pl.BlockSpec((None, 6, D), lambda i: (_mod_row(i), 0, 0)),
                  _const_spec((1, D)),
                  _layer_spec((D, PROJ_W), l),
                  rope_spec, rope_spec, rope_spec,
                  _const_spec((1, LANES)), _const_spec((1, LANES)),
                  _const_spec((LANES, LANES))],
        out_specs=[tok(w) for w in widths],
        scratch_shapes=[pltpu.VMEM((D, PROJ_W), BF16)],
        compiler_params=_cparams(("arbitrary",)),
        name="proj",
    )(x, mods_l, gn, w_in, cos, sa, sb, gq128, gk128, bd)


def _attn_kernel(sink_ref, q_ref, *refs, band):
    if band:
        kp_ref, kc_ref, kn_ref, vp_ref, vc_ref, vn_ref, kx_ref, vx_ref, o_ref = refs
        n = pl.program_id(1)
        nblk = pl.num_programs(1)
        ncol = C + 3 * ATTN_BLK
        r = lax.broadcasted_iota(jnp.int32, (ATTN_BLK, ATTN_BLK), 0)
        jj = lax.broadcasted_iota(jnp.int32, (ATTN_BLK, ATTN_BLK), 1)
        far = jnp.int32(2 * ATTN_BLK)
        off_prev = jnp.where(n > 0, 0, far)
        off_next = jnp.where(n < nblk - 1, 0, far)
        bias_prev = jnp.where(jj - r - off_prev >= 0, 0.0, NEG_INF)
        bias_next = jnp.where(r - jj - off_next >= 0, 0.0, NEG_INF)
        edges = (C, C + ATTN_BLK, C + 2 * ATTN_BLK, ncol)
    else:
        kx_ref, vx_ref, o_ref = refs
    q = q_ref[...]
    lane = lax.broadcasted_iota(jnp.int32, (ATTN_BLK, LANES), 1)
    heads = []
    for j in range(N_KV_HEADS):
        sl = slice(j * HEAD_DIM, (j + 1) * HEAD_DIM)
        if band:
            kj = jnp.concatenate([kx_ref[:, sl], kp_ref[:, sl], kc_ref[:, sl], kn_ref[:, sl]], axis=0)
            vj = jnp.concatenate([vx_ref[:, sl], vp_ref[:, sl], vc_ref[:, sl], vn_ref[:, sl]], axis=0)
        else:
            kj = kx_ref[:, sl]
            vj = vx_ref[:, sl]
        vaug = jnp.concatenate([vj, jnp.ones_like(vj)], axis=1)
        qs = jnp.concatenate([q[:, (j * GQA + g) * HEAD_DIM:(j * GQA + g + 1) * HEAD_DIM]
                              for g in range(GQA)], axis=0)
        s = lax.dot_general(qs, kj, (((1,), (1,)), ((), ())), preferred_element_type=F32)
        probs, sink_terms = [], []
        for g in range(GQA):
            sg = s[g * ATTN_BLK:(g + 1) * ATTN_BLK]
            if band:
                sg = jnp.concatenate([sg[:, :edges[0]], sg[:, edges[0]:edges[1]] + bias_prev,
                                      sg[:, edges[1]:edges[2]], sg[:, edges[2]:] + bias_next], axis=1)
            sk = sink_ref[j * GQA + g] * LOG2E
            mx = jnp.maximum(jnp.max(sg, axis=-1, keepdims=True), sk)
            probs.append(jnp.exp2(sg - mx).astype(BF16))
            sink_terms.append(jnp.exp2(sk - mx))
        o = _dot(jnp.concatenate(probs, axis=0), vaug)
        for g in range(GQA):
            og = o[g * ATTN_BLK:(g + 1) * ATTN_BLK]
            heads.append(og / (og[:, HEAD_DIM:HEAD_DIM + 1] + sink_terms[g]))
    tiles = [jnp.where(lane < HEAD_DIM, heads[2 * t], pltpu.roll(heads[2 * t + 1], HEAD_DIM, 1))
             for t in range(N_Q_HEADS // 2)]
    o_ref[...] = jnp.concatenate(tiles, axis=1).astype(BF16)


def _attn_call(sink_l, q, k, v):
    nb = S // ATTN_BLK
    smem = pl.BlockSpec(memory_space=pltpu.SMEM)
    qspec = pl.BlockSpec((ATTN_BLK, Q_W), lambda b, n: (b * nb + n, 0))
    prev = pl.BlockSpec((ATTN_BLK, KV_W), lambda b, n: (b * nb + jnp.maximum(n - 1, 0), 0))
    cur = pl.BlockSpec((ATTN_BLK, KV_W), lambda b, n: (b * nb + n, 0))
    nxt = pl.BlockSpec((ATTN_BLK, KV_W), lambda b, n: (b * nb + jnp.minimum(n + 1, nb - 1), 0))
    ctxs = pl.BlockSpec((C, KV_W), lambda b, n: (N_LAT // C + b, 0))
    lat = pl.pallas_call(
        functools.partial(_attn_kernel, band=True),
        out_shape=jax.ShapeDtypeStruct((N_LAT, Q_W), BF16),
        grid=(B, nb),
        in_specs=[smem, qspec, prev, cur, nxt, prev, cur, nxt, ctxs, ctxs],
        out_specs=qspec,
        compiler_params=_cparams(("parallel", "parallel")),
        name="attn_latent",
    )(sink_l, q, k, k, k, v, v, v, k, v)
    ncb = C // ATTN_BLK
    base = N_LAT // ATTN_BLK
    ctx = pl.pallas_call(
        functools.partial(_attn_kernel, band=False),
        out_shape=jax.ShapeDtypeStruct((N_CTX, Q_W), BF16),
        grid=(B, ncb),
        in_specs=[smem, pl.BlockSpec((ATTN_BLK, Q_W), lambda b, n: (base + b * ncb + n, 0)), ctxs, ctxs],
        out_specs=pl.BlockSpec((ATTN_BLK, Q_W), lambda b, n: (b * ncb + n, 0)),
        compiler_params=_cparams(("parallel", "parallel")),
        name="attn_context",
    )(sink_l, q, k, v)
    return lat, ctx


def _f1_kernel(w_ref, f_ref, re_ref, im_ref):
    res = _dot(w_ref[...], f_ref[...])
    re_ref[...] = res[:FS1].astype(BF16)
    im_ref[...] = res[FS1:].astype(BF16)


def _f2_kernel(ta_ref, tb_ref, re_ref, im_ref, o_ref):
    res = _dot(ta_ref[...], re_ref[...]) + _dot(tb_ref[...], im_ref[...])
    o_ref[:, 0:FOURIER_W] = res[:FS2].astype(BF16)
    o_ref[:, FOURIER_W:2 * FOURIER_W] = res[FS2:].astype(BF16)


def _fc_kernel(w_ref, f_ref, o_ref):
    res = _dot(w_ref[...], f_ref[...])
    o_ref[:, 0:FOURIER_W] = res[:C].astype(BF16)
    o_ref[:, FOURIER_W:2 * FOURIER_W] = res[C:].astype(BF16)


def _fourier_call(f, tabs):
    w1, ta, tb, wc = tabs
    f2d = f.reshape(N_TOK // FS2, F1_COLS)
    nchunk = F1_COLS // F1_CW
    a_re, a_im = pl.pallas_call(
        _f1_kernel,
        out_shape=[jax.ShapeDtypeStruct((B * FS1, F1_COLS), BF16)] * 2,
        grid=(B, nchunk),
        in_specs=[_const_spec((2 * FS1, FS1)),
                  pl.BlockSpec((FS1, F1_CW), lambda b, j: (b, j))],
        out_specs=[pl.BlockSpec((FS1, F1_CW), lambda b, j: (b, j))] * 2,
        compiler_params=_cparams(("parallel", "parallel")),
        name="fourier_stage1",
    )(w1, f2d)
    a_re = a_re.reshape(B * FS1, FS2, FOURIER_W)
    a_im = a_im.reshape(B * FS1, FS2, FOURIER_W)
    aspec = pl.BlockSpec((None, FS2, FOURIER_W), lambda b, k1: (b * FS1 + k1, 0, 0))
    tspec = pl.BlockSpec((None, 2 * FS2, FS2), lambda b, k1: (k1, 0, 0))
    h_t = pl.pallas_call(
        _f2_kernel,
        out_shape=jax.ShapeDtypeStruct((B, FS1, FS2, 2 * FOURIER_W), BF16),
        grid=(B, FS1),
        in_specs=[tspec, tspec, aspec, aspec],
        out_specs=pl.BlockSpec((None, None, FS2, 2 * FOURIER_W), lambda b, k1: (b, k1, 0, 0)),
        compiler_params=_cparams(("parallel", "parallel")),
        name="fourier_stage2",
    )(ta, tb, a_re, a_im)
    h_lat = jnp.transpose(h_t, (0, 2, 1, 3)).reshape(N_LAT, 2 * FOURIER_W)
    h_ctx = pl.pallas_call(
        _fc_kernel,
        out_shape=jax.ShapeDtypeStruct((N_CTX, 2 * FOURIER_W), BF16),
        grid=(B,),
        in_specs=[_const_spec((2 * C, C)),
                  pl.BlockSpec((C, FOURIER_W), lambda b: (N_LAT // C + b, 0))],
        out_specs=pl.BlockSpec((C, 2 * FOURIER_W), lambda b: (b, 0)),
        compiler_params=_cparams(("parallel",)),
        name="fourier_context",
    )(wc, f)
    return h_lat, h_ctx


def _poolconv_kernel(pc_ref, pp_ref, pn_ref, uc_ref, up_ref, un_ref, bm_ref, bh_ref, pw_ref, ps_ref,
                     cw_ref, cb_ref, cg_ref, cnb_ref, z_ref, a_ref, ext_ref, cv_ref):
    t = pl.program_id(0)
    lat_tiles = N_LAT // TP
    per_seq = S // TP
    is_ctx = t >= lat_tiles
    first = jnp.logical_or(t % per_seq == 0, is_ctx)
    last = jnp.logical_or(t % per_seq == per_seq - 1, is_ctx)
    seq_len = jnp.where(is_ctx, C, S)
    pos0 = jnp.where(is_ctx, 0, (t % per_seq) * TP)

    pcur = pc_ref[...]
    keep_prev = jnp.where(first, 0.0, 1.0)
    keep_next = jnp.where(last, 0.0, 1.0)
    halo = jnp.concatenate([pp_ref[...].astype(F32) * keep_prev,
                            pn_ref[...].astype(F32) * keep_next], axis=0).astype(BF16)
    lane = lax.broadcasted_iota(jnp.int32, (TP, POOL_W), 1)
    pos = lax.broadcasted_iota(jnp.int32, (TP, POOL_W), 0) + pos0
    zsum = None
    win = None
    for gi, w in enumerate(POOL_WINDOWS):
        sg = _dot(bm_ref[gi], pcur) + _dot(bh_ref[gi], halo)
        if gi == 0:
            zsum, win = sg, jnp.full((TP, POOL_W), w, jnp.int32)
        else:
            sel = lane >= gi * POOL_GROUP_W
            zsum = jnp.where(sel, sg, zsum)
            win = jnp.where(sel, w, win)
    half = win // 2
    lo = jnp.clip(pos - half, 0, seq_len)
    hi = jnp.clip(pos - half + win, 0, seq_len)
    z = zsum / (hi - lo).astype(F32) - pcur.astype(F32)
    z_ref[...] = (_dot(z.astype(BF16), pw_ref[...]) * ps_ref[...]).astype(BF16)

    ext_ref[0:HALO, :] = up_ref[...].astype(F32) * keep_prev
    ext_ref[HALO:HALO + TP, :] = uc_ref[...].astype(F32)
    ext_ref[HALO + TP:, :] = un_ref[...].astype(F32) * keep_next
    off = HALO - CONV_K // 2
    for cb in range(CONV_W // LANES):
        cs = slice(cb * LANES, (cb + 1) * LANES)
        acc = jnp.zeros((TP, LANES), F32) + cb_ref[:, cs]
        for j in range(CONV_K):
            acc = acc + ext_ref[pl.ds(off + j, TP), cs] * cw_ref[j:j + 1, cs]
        cv_ref[:, cs] = acc
    cv = cv_ref[...]
    mu = jnp.mean(cv, axis=-1, keepdims=True)
    var = jnp.mean(jnp.square(cv - mu), axis=-1, keepdims=True)
    un = (cv - mu) * lax.rsqrt(var + EPS) * cg_ref[...] + cnb_ref[...]
    a_ref[...] = (un * _sigmoid(un)).astype(BF16)


def _poolconv_call(p, u, band_main, band_halo, pw_bd, pool_scale, conv_w, conv_b, cn_g, cn_b):
    nh = TP // HALO
    last_h = N_TOK // HALO - 1
    cur = lambda w: pl.BlockSpec((TP, w), lambda t: (t, 0))
    prv = lambda w: pl.BlockSpec((HALO, w), lambda t: (jnp.maximum(t * nh - 1, 0), 0))
    nxt = lambda w: pl.BlockSpec((HALO, w), lambda t: (jnp.minimum((t + 1) * nh, last_h), 0))
    return pl.pallas_call(
        _poolconv_kernel,
        out_shape=[jax.ShapeDtypeStruct((N_TOK, POOL_W), BF16),
                   jax.ShapeDtypeStruct((N_TOK, CONV_W), BF16)],
        grid=(NTP,),
        in_specs=[cur(POOL_W), prv(POOL_W), nxt(POOL_W), cur(CONV_W), prv(CONV_W), nxt(CONV_W),
                  _const_spec((4, TP, TP)), _const_spec((4, TP, 2 * HALO)),
                  _const_spec((POOL_W, POOL_W)), _const_spec((1, POOL_W)),
                  _const_spec((CONV_K, CONV_W)), _const_spec((1, CONV_W)),
                  _const_spec((1, CONV_W)), _const_spec((1, CONV_W))],
        out_specs=[cur(POOL_W), cur(CONV_W)],
        scratch_shapes=[pltpu.VMEM((TP + 2 * HALO, CONV_W), F32), pltpu.VMEM((TP, CONV_W), F32)],
        compiler_params=_cparams(("parallel",)),
        name="pool_conv",
    )(p, p, p, u, u, u, band_main, band_halo, pw_bd, pool_scale, conv_w, conv_b, cn_g, cn_b)


def _route(logits):
    lane = lax.broadcasted_iota(jnp.int32, logits.shape, 1)
    big = jnp.int32(LANES)
    lg = jnp.where(lane < N_GROUPS, logits, NEG_INF)
    mg = jnp.max(lg, axis=-1, keepdims=True)
    grp = jnp.min(jnp.where(lg == mg, lane, big), axis=-1, keepdims=True)
    p_grp = 1.0 / jnp.sum(jnp.exp(lg - mg), axis=-1, keepdims=True)
    lo = N_GROUPS + grp * EPG
    le = jnp.where((lane >= lo) & (lane < lo + EPG), logits, NEG_INF)
    m1 = jnp.max(le, axis=-1, keepdims=True)
    i1 = jnp.min(jnp.where(le == m1, lane, big), axis=-1, keepdims=True)
    le2 = jnp.where(lane == i1, NEG_INF, le)
    m2 = jnp.max(le2, axis=-1, keepdims=True)
    i2 = jnp.min(jnp.where(le2 == m2, lane, big), axis=-1, keepdims=True)
    r = jnp.exp(m2 - m1)
    w1 = p_grp / (1.0 + r)
    w2 = p_grp * r / (1.0 + r)
    e1 = (i1 - N_GROUPS).astype(F32)
    e2 = (i2 - N_GROUPS).astype(F32)
    return jnp.where(lane == 0, e1, jnp.where(lane == 1, e2, jnp.where(lane == 2, w1,
                     jnp.where(lane == 3, w2, 0.0))))


def _mix_kernel(x_ref, mod_ref, gn_ref, al_ref, ac_ref, hl_ref, hc_ref, z_ref, cv_ref,
                wa_ref, wf_ref, wp_ref, wc_ref, wg_ref, bg_ref, wo_ref, gf_ref, rh_ref, rl_ref, rb_ref,
                xo_ref, h2_ref, rt_ref):
    is_ctx = pl.program_id(0) >= LAT_TILES
    m = mod_ref[...]
    x = x_ref[...]
    hb = _modulate(x, gn_ref[...], m[0:1], m[1:2]).astype(BF16)
    attn = jnp.where(is_ctx, ac_ref[...], al_ref[...])
    four = jnp.where(is_ctx, hc_ref[...], hl_ref[...])
    branches = ((attn, wa_ref), (four, wf_ref), (z_ref[...], wp_ref), (cv_ref[...], wc_ref))
    acc = None
    for bi, (inp, w_ref) in enumerate(branches):
        cs = slice(bi * D, (bi + 1) * D)
        gate = _sigmoid(_dot(hb, wg_ref[:, cs]) + bg_ref[:, cs])
        term = gate * _dot(inp, w_ref[...])
        acc = term if acc is None else acc + term
    x_new = x + m[2:3] * _dot(acc.astype(BF16), wo_ref[...])
    xo_ref[...] = x_new
    h2 = _modulate(x_new, gf_ref[...], m[3:4], m[4:5])
    h2_ref[...] = h2
    hi = h2.astype(BF16)
    lo = (h2 - hi.astype(F32)).astype(BF16)
    logits = _dot(hi, rh_ref[...]) + _dot(lo, rh_ref[...]) + _dot(hi, rl_ref[...]) + rb_ref[...]
    rt_ref[...] = _route(logits)


def _mix_call(x, mods_l, gn, a_lat, a_ctx, h_lat, h_ctx, zc, cact, l, stacked, small):
    tok = lambda w: pl.BlockSpec((TM, w), lambda i: (i, 0))
    lat = lambda w: pl.BlockSpec((TM, w), lambda i: (jnp.minimum(i, LAT_TILES - 1), 0))
    wa, wf, wp, wc, wg, wo, rh, rl = stacked
    bg, gf, rb = small
    in_specs = [tok(D), pl.BlockSpec((None, 6, D), lambda i: (_mod_row(i), 0, 0)), _const_spec((1, D)),
                lat(Q_W), _const_spec((N_CTX, Q_W)),
                lat(2 * FOURIER_W), _const_spec((N_CTX, 2 * FOURIER_W)),
                tok(POOL_W), tok(CONV_W)]
    in_specs += [_layer_spec(w.shape[1:], l) for w in (wa, wf, wp, wc, wg)]
    in_specs += [_const_spec(bg.shape), _layer_spec(wo.shape[1:], l), _const_spec(gf.shape),
                 _layer_spec(rh.shape[1:], l), _layer_spec(rl.shape[1:], l), _const_spec(rb.shape)]
    return pl.pallas_call(
        _mix_kernel,
        out_shape=[jax.ShapeDtypeStruct((N_TOK, D), F32), jax.ShapeDtypeStruct((N_TOK, D), F32),
                   jax.ShapeDtypeStruct((N_TOK, LANES), F32)],
        grid=(NT,),
        in_specs=in_specs,
        out_specs=[tok(D), tok(D), tok(LANES)],
        compiler_params=_cparams(("parallel",)),
        name="mix",
    )(x, mods_l, gn, a_lat, a_ctx, h_lat, h_ctx, zc, cact, wa, wf, wp, wc, wg, bg, wo, gf, rh, rl, rb)


def _onehots(route):
    lane = lax.broadcasted_iota(jnp.int32, route.shape, 1)
    e1 = route[:, 0:1].astype(jnp.int32)
    e2 = route[:, 1:2].astype(jnp.int32)
    return (lane == e1).astype(F32), (lane == e2).astype(F32)


def _rank_kernel(rt_ref, tri_ref, rk_ref, cnt_ref, carry_ref):
    i = pl.program_id(0)

    @pl.when(i == 0)
    def _():
        carry_ref[...] = jnp.zeros_like(carry_ref)

    oh1, oh2 = _onehots(rt_ref[...])
    both = oh1 + oh2
    carry = carry_ref[0:1, :]
    before = _dot(tri_ref[...], both.astype(BF16)) + carry
    r1 = jnp.sum(oh1 * before, axis=-1, keepdims=True)
    r2 = jnp.sum(oh2 * before, axis=-1, keepdims=True)
    lane = lax.broadcasted_iota(jnp.int32, both.shape, 1)
    rk_ref[...] = jnp.where(lane == 0, r1, jnp.where(lane == 1, r2, 0.0))
    total = carry + jnp.sum(both, axis=0, keepdims=True)
    carry_ref[...] = jnp.broadcast_to(total, carry_ref.shape)
    cnt_ref[...] = jnp.broadcast_to(total, cnt_ref.shape)


def _padded_ends(counts_row):
    lane = lax.broadcasted_iota(jnp.int32, counts_row.shape, 1)
    padded = jnp.floor((counts_row + (MOE_BLK - 1)) * (1.0 / MOE_BLK)) * MOE_BLK
    ends = padded
    sh = 1
    while sh < N_EXPERTS:
        ends = ends + jnp.where(lane >= sh, pltpu.roll(ends, sh, 1), 0.0)
        sh *= 2
    return padded, ends


def _dest_kernel(rt_ref, rk_ref, cnt_ref, dst_ref, be_ref):
    padded, ends = _padded_ends(cnt_ref[0:1, :])
    starts = ends - padded
    oh1, oh2 = _onehots(rt_ref[...])
    rk = rk_ref[...]
    d1 = jnp.sum(oh1 * starts, axis=-1, keepdims=True) + rk[:, 0:1]
    d2 = jnp.sum(oh2 * starts, axis=-1, keepdims=True) + rk[:, 1:2]
    lane = lax.broadcasted_iota(jnp.int32, rk.shape, 1)
    dst_ref[...] = jnp.where(lane == 0, d1, jnp.where(lane == 1, d2, 0.0)).astype(jnp.int32)
    blk = lax.broadcasted_iota(jnp.int32, be_ref.shape, 0).astype(F32) * MOE_BLK
    lane_b = lax.broadcasted_iota(jnp.int32, be_ref.shape, 1)
    done = jnp.where((ends <= blk) & (lane_b < N_EXPERTS), 1.0, 0.0)
    be = jnp.minimum(jnp.sum(done, axis=-1, keepdims=True), N_EXPERTS - 1.0)
    nblk = jnp.max(jnp.where(lane_b == N_EXPERTS - 1, ends, 0.0), axis=-1, keepdims=True) * (1.0 / MOE_BLK)
    be_ref[...] = jnp.where(lane_b == 0, be, jnp.where(lane_b == 1, nblk, 0.0)).astype(jnp.int32)


def _plan_call(route, tri):
    tok = pl.BlockSpec((TM, LANES), lambda i: (i, 0))
    ranks, counts = pl.pallas_call(
        _rank_kernel,
        out_shape=[jax.ShapeDtypeStruct((N_TOK, LANES), F32), jax.ShapeDtypeStruct((8, LANES), F32)],
        grid=(NT,),
        in_specs=[tok, _const_spec((TM, TM))],
        out_specs=[tok, pl.BlockSpec((8, LANES), lambda i: (0, 0))],
        scratch_shapes=[pltpu.VMEM((8, LANES), F32)],
        compiler_params=_cparams(("arbitrary",)),
        name="moe_rank",
    )(route, tri)
    dest, blk = pl.pallas_call(
        _dest_kernel,
        out_shape=[jax.ShapeDtypeStruct((N_TOK, LANES), jnp.int32),
                   jax.ShapeDtypeStruct((256, LANES), jnp.int32)],
        grid=(NT,),
        in_specs=[tok, tok, pl.BlockSpec((8, LANES), lambda i: (0, 0))],
        out_specs=[tok, pl.BlockSpec((256, LANES), lambda i: (0, 0))],
        compiler_params=_cparams(("arbitrary",)),
        name="moe_dest",
    )(route, ranks, counts)
    dest_flat = dest[:, 0:TOP_K].reshape(NT, 1, TM * TOP_K)
    blk_e = blk[:N_MOE_BLOCKS, 0]
    n_used = blk[0:1, 1]
    return dest_flat, blk_e, n_used


def _dispatch_kernel(dst_ref, h2_ref, xs_in_ref, xs_ref, sem):
    del xs_in_ref

    def row_copy(r, k):
        return pltpu.make_async_copy(h2_ref.at[pl.ds(r, 1), :],
                                     xs_ref.at[pl.ds(dst_ref[0, TOP_K * r + k], 1), :], sem)

    def start(r, carry):
        for k in range(TOP_K):
            row_copy(r, k).start()
        return carry

    def wait(r, carry):
        for k in range(TOP_K):
            row_copy(r, k).wait()
        return carry

    lax.fori_loop(0, TM, start, 0)
    lax.fori_loop(0, TM, wait, 0)


def _dispatch_call(dest_flat, h2, zeros_slots):
    return pl.pallas_call(
        _dispatch_kernel,
        out_shape=jax.ShapeDtypeStruct((N_SLOTS, D), F32),
        grid=(NT,),
        in_specs=[pl.BlockSpec((None, 1, TM * TOP_K), lambda i: (i, 0, 0), memory_space=pltpu.SMEM),
                  pl.BlockSpec((TM, D), lambda i: (i, 0)),
                  pl.BlockSpec(memory_space=pl.ANY)],
        out_specs=pl.BlockSpec(memory_space=pl.ANY),
        scratch_shapes=[pltpu.SemaphoreType.DMA],
        input_output_aliases={2: 0},
        compiler_params=_cparams(("arbitrary",)),
        name="moe_dispatch",
    )(dest_flat, h2, zeros_slots)


def _expert_kernel(be_ref, nu_ref, xs_ref, wg_ref, wu_ref, wd_ref, ys_ref, wgb_ref, wub_ref, wdb_ref):
    b = pl.program_id(0)

    @pl.when(jnp.logical_or(b == 0, be_ref[b] != be_ref[jnp.maximum(b - 1, 0)]))
    def _():
        wgb_ref[...] = wg_ref[...].astype(BF16)
        wub_ref[...] = wu_ref[...].astype(BF16)
        wdb_ref[...] = wd_ref[...].astype(BF16)

    @pl.when(b < nu_ref[0])
    def _():
        xb = xs_ref[...].astype(BF16)
        g = _dot(xb, wgb_ref[...])
        u = _dot(xb, wub_ref[...])
        hmid = (g * _sigmoid(g)) * u
        ys_ref[...] = _dot(hmid.astype(BF16), wdb_ref[...])

    @pl.when(b >= nu_ref[0])
    def _():
        ys_ref[...] = jnp.zeros_like(ys_ref)


def _expert_call(blk_e, n_used, xs, wg, wu, wd, l):
    wspec = lambda k, n: pl.BlockSpec((None, None, k, n), lambda b, be, nu: (l, be[b], 0, 0))
    return pl.pallas_call(
        _expert_kernel,
        out_shape=jax.ShapeDtypeStruct((N_SLOTS, D), F32),
        grid_spec=pltpu.PrefetchScalarGridSpec(
            num_scalar_prefetch=2,
            grid=(N_MOE_BLOCKS,),
            in_specs=[pl.BlockSpec((MOE_BLK, D), lambda b, be, nu: (jnp.minimum(b, nu[0] - 1), 0)),
                      wspec(D, EXPERT_HIDDEN), wspec(D, EXPERT_HIDDEN), wspec(EXPERT_HIDDEN, D)],
            out_specs=pl.BlockSpec((MOE_BLK, D), lambda b, be, nu: (b, 0)),
            scratch_shapes=[pltpu.VMEM((D, EXPERT_HIDDEN), BF16), pltpu.VMEM((D, EXPERT_HIDDEN), BF16),
                            pltpu.VMEM((EXPERT_HIDDEN, D), BF16)]),
        compiler_params=_cparams(("arbitrary",)),
        name="moe_experts",
    )(blk_e, n_used, xs, wg, wu, wd)


def _combine_kernel(dst_ref, ys_ref, x_ref, rt_ref, mod_ref, o_ref, buf1, buf2, sem):
    bufs = (buf1, buf2)

    def row_copy(r, k):
        return pltpu.make_async_copy(ys_ref.at[pl.ds(dst_ref[0, TOP_K * r + k], 1), :],
                                     bufs[k].at[pl.ds(r, 1), :], sem)

    def start(r, carry):
        for k in range(TOP_K):
            row_copy(r, k).start()
        return carry

    def wait(r, carry):
        for k in range(TOP_K):
            row_copy(r, k).wait()
        return carry

    lax.fori_loop(0, TM, start, 0)
    lax.fori_loop(0, TM, wait, 0)
    rt = rt_ref[...]
    y = rt[:, 2:3] * buf1[...] + rt[:, 3:4] * buf2[...]
    o_ref[...] = x_ref[...] + mod_ref[5:6, :] * y


def _combine_call(dest_flat, ys, x, route, mods_l, n_tiles):
    tok = lambda w: pl.BlockSpec((TM, w), lambda i: (i, 0))
    return pl.pallas_call(
        _combine_kernel,
        out_shape=jax.ShapeDtypeStruct((n_tiles * TM, D), F32),
        grid=(n_tiles,),
        in_specs=[pl.BlockSpec((None, 1, TM * TOP_K), lambda i: (i, 0, 0), memory_space=pltpu.SMEM),
                  pl.BlockSpec(memory_space=pl.ANY),
                  tok(D), tok(LANES),
                  pl.BlockSpec((None, 6, D), lambda i: (_mod_row(i), 0, 0))],
        out_specs=tok(D),
        scratch_shapes=[pltpu.VMEM((TM, D), F32), pltpu.VMEM((TM, D), F32), pltpu.SemaphoreType.DMA],
        compiler_params=_cparams(("arbitrary",)),
        name="moe_combine",
    )(dest_flat, ys, x, route, mods_l)


def _rope_tables():
    nf = HEAD_DIM // 4
    inv = ROPE_BASE ** (-jnp.arange(nf, dtype=F32) / nf)
    t = jnp.arange(S)
    row = (t // GRID_W).astype(F32)[:, None] * inv[None, :]
    col = (t % GRID_W).astype(F32)[:, None] * inv[None, :]
    zero = jnp.zeros_like(row)
    cos = jnp.concatenate([jnp.cos(row), jnp.cos(row), jnp.cos(col), jnp.cos(col)], axis=1)
    sa = jnp.concatenate([-jnp.sin(row), zero, -jnp.sin(col), zero], axis=1)
    sb = jnp.concatenate([zero, jnp.sin(row), zero, jnp.sin(col)], axis=1)
    ident = (jnp.ones((TM, HEAD_DIM), F32), jnp.zeros((TM, HEAD_DIM), F32), jnp.zeros((TM, HEAD_DIM), F32))
    return tuple(jnp.tile(jnp.concatenate([a, b], axis=0), (1, LANES // HEAD_DIM))
                 for a, b in zip((cos, sa, sb), ident))


def _fourier_tables():
    s1 = np.arange(FS1)
    ang1 = 2.0 * np.pi * np.outer(s1, s1) / FS1
    w1 = np.concatenate([np.cos(ang1), -np.sin(ang1)], axis=0) / np.sqrt(S)
    k1 = np.arange(FS1)[:, None, None]
    k2 = np.arange(FS2)[None, :, None]
    s2 = np.arange(FS2)[None, None, :]
    ang2 = 2.0 * np.pi * ((k1 + FS1 * k2) * s2 % S) / S
    c2, sn2 = np.cos(ang2), np.sin(ang2)
    ta = np.concatenate([c2, -sn2], axis=1)
    tb = np.concatenate([sn2, c2], axis=1)
    sc = np.arange(C)
    angc = 2.0 * np.pi * np.outer(sc, sc) / C
    wc = np.concatenate([np.cos(angc), -np.sin(angc)], axis=0) / np.sqrt(C)
    return tuple(jnp.asarray(a, F32).astype(BF16) for a in (w1, ta, tb, wc))


def _channel_dft():
    cidx = np.arange(FOURIER_GROUP_W)
    ang = 2.0 * np.pi * np.outer(cidx, cidx) / FOURIER_GROUP_W
    eye = np.eye(FOURIER_W // FOURIER_GROUP_W)
    cw = np.kron(eye, np.cos(ang)) / np.sqrt(FOURIER_GROUP_W)
    sw = np.kron(eye, np.sin(ang)) / np.sqrt(FOURIER_GROUP_W)
    return jnp.asarray(np.concatenate([cw, sw], axis=0), F32)


def _pool_bands():
    t = np.arange(TP)[:, None]
    main, halo = [], []
    for w in POOL_WINDOWS:
        def hit(j):
            return ((j - t >= -(w // 2)) & (j - t <= w // 2 - 1)).astype(np.float32)
        main.append(hit(np.arange(TP)[None, :]))
        halo.append(np.concatenate([hit(np.arange(-HALO, 0)[None, :]),
                                    hit(np.arange(TP, TP + HALO)[None, :])], axis=1))
    return (jnp.asarray(np.stack(main), F32).astype(BF16), jnp.asarray(np.stack(halo), F32).astype(BF16))


def _fold_kernel(a_ref, b_ref, o_ref):
    o_ref[...] = jnp.dot(a_ref[...], b_ref[...], preferred_element_type=F32,
                         precision=lax.Precision.HIGHEST).astype(BF16)


def _fold_fourier_weights(dftw, w_br_fourier):
    nl = w_br_fourier.shape[0]
    return pl.pallas_call(
        _fold_kernel,
        out_shape=jax.ShapeDtypeStruct((nl, 2 * FOURIER_W, D), BF16),
        grid=(nl,),
        in_specs=[pl.BlockSpec((2 * FOURIER_W, FOURIER_W), lambda l: (0, 0)),
                  pl.BlockSpec((None, FOURIER_W, D), lambda l: (l, 0, 0))],
        out_specs=pl.BlockSpec((None, 2 * FOURIER_W, D), lambda l: (l, 0, 0)),
        compiler_params=_cparams(("arbitrary",)),
        name="fold_fourier_proj",
    )(dftw, w_br_fourier)


def _block_diag(blocks):
    n, r, c = blocks.shape
    out = jnp.zeros((n * r, n * c), blocks.dtype)
    for i in range(n):
        out = lax.dynamic_update_slice(out, blocks[i], (i * r, i * c))
    return out


def kernel(x, c, ctx, c_ctx, w_ada, b_ada, g_norm_mix, g_norm_ffn, w_in, g_q, g_k, sink, w_br_attn,
           w_br_fourier, pool_w, pool_scale, w_br_pool, conv_w, conv_b, cn_g, cn_b, w_br_conv, w_gate,
           b_gate, w_out, w_router_grp, b_router_grp, w_router_exp, b_router_exp, w_e_gate, w_e_up,
           w_e_down):
    xs = jnp.concatenate([x.reshape(N_LAT, D), ctx.reshape(N_CTX, D)], axis=0)
    nl = w_ada.shape[0]
    mods = _ada_all(c, c_ctx, w_ada, b_ada).reshape(nl, 8, 6, D)
    rope_tabs = _rope_tables()
    four_tabs = _fourier_tables()
    band_main, band_halo = _pool_bands()
    wf_all = _fold_fourier_weights(_channel_dft(), w_br_fourier)
    bd = jnp.asarray(np.kron(np.eye(LANES // HEAD_DIM), np.ones((HEAD_DIM, HEAD_DIM))), F32).astype(BF16)
    tri = jnp.asarray(np.tril(np.ones((TM, TM)), -1), F32).astype(BF16)
    zeros_slots = jnp.zeros((N_SLOTS, D), F32)
    rpad = jnp.zeros((nl, D, LANES - N_GROUPS - N_EXPERTS), F32)
    w_router = jnp.concatenate([w_router_grp, w_router_exp, rpad], axis=-1)
    r_hi = w_router.astype(BF16)
    r_lo = (w_router - r_hi.astype(F32)).astype(BF16)
    r_b = jnp.concatenate([b_router_grp, b_router_exp, rpad[:, 0, :]], axis=-1).reshape(nl, 1, LANES)
    stacked = tuple(w.astype(BF16) for w in (w_br_attn,)) + (wf_all,) + tuple(
        w.astype(BF16) for w in (w_br_pool, w_br_conv, w_gate, w_out)) + (r_hi, r_lo)

    for l in range(nl):
        mods_l = mods[l]
        gn = g_norm_mix[l].reshape(1, D)
        q, k, v, f, p, u = _proj_call(xs, mods_l, gn, w_in, l, rope_tabs,
                                      jnp.tile(g_q[l], 2).reshape(1, LANES),
                                      jnp.tile(g_k[l], 2).reshape(1, LANES), bd)
        a_lat, a_ctx = _attn_call(sink[l], q, k, v)
        h_lat, h_ctx = _fourier_call(f, four_tabs)
        zc, cact = _poolconv_call(p, u, band_main, band_halo, _block_diag(pool_w[l]).astype(BF16),
                                  pool_scale[l].reshape(1, POOL_W), conv_w[l], conv_b[l].reshape(1, CONV_W),
                                  cn_g[l].reshape(1, CONV_W), cn_b[l].reshape(1, CONV_W))
        small = (b_gate[l].reshape(1, 4 * D), g_norm_ffn[l].reshape(1, D), r_b[l])
        xs, h2, route = _mix_call(xs, mods_l, gn, a_lat, a_ctx, h_lat, h_ctx, zc, cact, l, stacked, small)
        dest_flat, blk_e, n_used = _plan_call(route, tri)
        slots = _dispatch_call(dest_flat, h2, zeros_slots)
        ys = _expert_call(blk_e, n_used, slots, w_e_gate, w_e_up, w_e_down, l)
        xs = _combine_call(dest_flat, ys, xs, route, mods_l, LAT_TILES if l == nl - 1 else NT)
    return xs.reshape(B, S, D)
```

```python
import functools

import numpy as np
import jax
import jax.numpy as jnp
from jax import lax
from jax.experimental import pallas as pl
from jax.experimental.pallas import tpu as pltpu

F32 = jnp.float32
BF16 = jnp.bfloat16

D = 1024
B = 2
S = 8192
C = 256
GRID_W = 64
HEAD_DIM = 64
N_Q_HEADS = 8
N_KV_HEADS = 2
GQA = N_Q_HEADS // N_KV_HEADS
WINDOW = 128
ATTN_BLK = 128
ROPE_BASE = 10000.0
Q_W = 512
KV_W = 128
FOURIER_W = 640
FOURIER_GROUP_W = 160
POOL_W = 640
POOL_GROUP_W = 160
POOL_WINDOWS = (2, 4, 8, 16)
CONV_W = 512
CONV_K = 31
PROJ_W = 3072
N_GROUPS = 4
EPG = 8
N_EXPERTS = 32
TOP_K = 2
EXPERT_HIDDEN = 512
MOE_BLK = 256
EPS = 1e-6
NEG_INF = -1e30
LOG2E = 1.4426950408889634

N_LAT = B * S
N_CTX = B * C
N_TOK = N_LAT + N_CTX
TM = 512
NT = N_TOK // TM
LAT_TILES = N_LAT // TM
TILES_PER_BATCH = S // TM
TP = 256
NTP = N_TOK // TP
HALO = 16
CONV_WIN = TP // 2 + 2 * HALO
N_ASSIGN = N_TOK * TOP_K
RUN_ROWS = 64
PK = D // 2 // 128
TS = 1152
N_MOE_BLOCKS = (N_ASSIGN + NT * N_EXPERTS + N_EXPERTS * (RUN_ROWS - 1 + MOE_BLK - 1)) // MOE_BLK
N_SLOTS = N_MOE_BLOCKS * MOE_BLK
FS1 = 64
FS2 = 128
F1_COLS = FS2 * FOURIER_W
F1_CW = 8192
LANES = 128
VMEM_LIMIT = 56 * 1024 * 1024


def _cparams(sem, vmem=VMEM_LIMIT):
    return pltpu.CompilerParams(dimension_semantics=sem, vmem_limit_bytes=vmem)


def _const_spec(shape):
    nd = len(shape)
    return pl.BlockSpec(shape, lambda *_: (0,) * nd, pipeline_mode=pl.Buffered(1))


def _dot(a, b):
    return jnp.dot(a, b, preferred_element_type=F32)


def _split_dot(a, b_bf16):
    hi = a.astype(BF16)
    lo = (a - hi.astype(F32)).astype(BF16)
    return _dot(hi, b_bf16) + _dot(lo, b_bf16)


def _modulate(x, g, shift, scale):
    y = x * lax.rsqrt(jnp.mean(x * x, axis=-1, keepdims=True) + EPS)
    return (y * g) * (1.0 + scale) + shift


def _sigmoid(x):
    return 1.0 / (1.0 + jnp.exp(-x))


def _ada_kernel(ct_ref, w_ref, b_ref, o_ref):
    ct = ct_ref[...]
    s = ct * _sigmoid(ct)
    w = w_ref[...]
    rows = [jnp.sum(w * s[:, r:r + 1], axis=0, keepdims=True) for r in range(3)]
    rows.append(jnp.zeros((5, w.shape[1]), F32))
    o_ref[...] = jnp.concatenate(rows, axis=0) + b_ref[...]


def _ada_all(c, c_ctx, w_ada, b_ada):
    ct = jnp.concatenate([c, c_ctx[None, :], jnp.zeros((5, D), F32)], axis=0).T
    cols = 1536
    nl = w_ada.shape[0]
    return pl.pallas_call(
        _ada_kernel,
        out_shape=jax.ShapeDtypeStruct((nl, 8, 6 * D), F32),
        grid=(nl, 6 * D // cols),
        in_specs=[pl.BlockSpec((D, 8), lambda l, j: (0, 0)),
                  pl.BlockSpec((None, D, cols), lambda l, j: (l, 0, j)),
                  pl.BlockSpec((None, 1, cols), lambda l, j: (l, 0, j))],
        out_specs=pl.BlockSpec((None, 8, cols), lambda l, j: (l, 0, j)),
        compiler_params=_cparams(("arbitrary", "arbitrary")),
        name="adaln",
    )(ct, w_ada, b_ada.reshape(nl, 1, 6 * D))


def _head_rms(t, g128, bd):
    outs = []
    for j in range(t.shape[1] // LANES):
        blk = t[:, j * LANES:(j + 1) * LANES]
        ss = _split_dot(blk * blk, bd)
        outs.append(blk * lax.rsqrt(ss * (1.0 / HEAD_DIM) + EPS) * g128)
    return outs


def _rope(blocks, cos, sa, sb):
    outs = []
    for blk in blocks:
        up = pltpu.roll(blk, LANES - 16, 1)
        dn = pltpu.roll(blk, 16, 1)
        outs.append(blk * cos + up * sa + dn * sb)
    return outs


def _proj_kernel(x_ref, mod_ref, gn_ref, w_ref, cos_ref, sa_ref, sb_ref, gq_ref, gk_ref, bd_ref,
                 q_ref, k_ref, v_ref, f_ref, p_ref, u_ref, wbf_ref):
    @pl.when(pl.program_id(0) == 0)
    def _():
        wbf_ref[...] = w_ref[...].astype(BF16)

    m = mod_ref[...]
    h = _modulate(x_ref[...], gn_ref[...], m[0:1], m[1:2])
    proj = _dot(h.astype(BF16), wbf_ref[...])
    cos, sa, sb, bd = cos_ref[...], sa_ref[...], sb_ref[...], bd_ref[...]
    q = _rope(_head_rms(proj[:, 0:Q_W], gq_ref[...], bd), cos, sa, sb)
    q_ref[...] = (jnp.concatenate(q, axis=1) * (LOG2E * HEAD_DIM ** -0.5)).astype(BF16)
    k = _rope(_head_rms(proj[:, Q_W:Q_W + KV_W], gk_ref[...], bd), cos, sa, sb)
    k_ref[...] = k[0].astype(BF16)
    o = Q_W + KV_W
    v_ref[...] = proj[:, o:o + KV_W].astype(BF16)
    o += KV_W
    f_ref[...] = proj[:, o:o + FOURIER_W].astype(BF16)
    o += FOURIER_W
    p_ref[...] = proj[:, o:o + POOL_W].astype(BF16)
    o += POOL_W
    a = proj[:, o:o + CONV_W]
    g = proj[:, o + CONV_W:o + 2 * CONV_W]
    u_ref[...] = (a * _sigmoid(g)).astype(BF16)


def _mod_row(i):
    return jnp.minimum(i // TILES_PER_BATCH, 2)


def _layer_spec(shape, l):
    nd = len(shape)
    return pl.BlockSpec((None,) + tuple(shape), lambda *_: (l,) + (0,) * nd, pipeline_mode=pl.Buffered(1))


def _proj_call(x, mods_l, gn, w_in, l, rope_tabs, gq128, gk128, bd):
    cos, sa, sb = rope_tabs
    tok = lambda w: pl.BlockSpec((TM, w), lambda i: (i, 0))
    rope_spec = pl.BlockSpec((TM, LANES), lambda i: (jnp.where(i < LAT_TILES, i % TILES_PER_BATCH,
                                                               TILES_PER_BATCH), 0))
    widths = (Q_W, KV_W, KV_W, FOURIER_W, POOL_W, CONV_W)
    return pl.pallas_call(
        _proj_kernel,
        out_shape=[jax.ShapeDtypeStruct((N_TOK, w), BF16) for w in widths],
        grid=(NT,),
        in_specs=[tok(D),
                  pl.BlockSpec((None, 6, D), lambda i: (_mod_row(i), 0, 0)),
                  _const_spec((1, D)),
                  _layer_spec((D, PROJ_W), l),
                  rope_spec, rope_spec, rope_spec,
                  _const_spec((1, LANES)), _const_spec((1, LANES)),
                  _const_spec((LANES, LANES))],
        out_specs=[tok(w) for w in widths],
        scratch_shapes=[pltpu.VMEM((D, PROJ_W), BF16)],
        compiler_params=_cparams(("arbitrary",)),
        name="proj",
    )(x, mods_l, gn, w_in, cos, sa, sb, gq128, gk128, bd)


def _attn_kernel(sink_ref, q_ref, *refs, band):
    if band:
        kp_ref, kc_ref, kn_ref, vp_ref, vc_ref, vn_ref, kx_ref, vx_ref, o_ref = refs
        n = pl.program_id(1)
        nblk = pl.num_programs(1)
        ncol = C + 3 * ATTN_BLK
        r = lax.broadcasted_iota(jnp.int32, (ATTN_BLK, ATTN_BLK), 0)
        jj = lax.broadcasted_iota(jnp.int32, (ATTN_BLK, ATTN_BLK), 1)
        far = jnp.int32(2 * ATTN_BLK)
        off_prev = jnp.where(n > 0, 0, far)
        off_next = jnp.where(n < nblk - 1, 0, far)
        bias_prev = jnp.where(jj - r - off_prev >= 0, 0.0, NEG_INF)
        bias_next = jnp.where(r - jj - off_next >= 0, 0.0, NEG_INF)
        edges = (C, C + ATTN_BLK, C + 2 * ATTN_BLK, ncol)
    else:
        kx_ref, vx_ref, o_ref = refs
    q = q_ref[...]
    lane = lax.broadcasted_iota(jnp.int32, (ATTN_BLK, LANES), 1)
    heads = []
    for j in range(N_KV_HEADS):
        sl = slice(j * HEAD_DIM, (j + 1) * HEAD_DIM)
        if band:
            kj = jnp.concatenate([kx_ref[:, sl], kp_ref[:, sl], kc_ref[:, sl], kn_ref[:, sl]], axis=0)
            vj = jnp.concatenate([vx_ref[:, sl], vp_ref[:, sl], vc_ref[:, sl], vn_ref[:, sl]], axis=0)
        else:
            kj = kx_ref[:, sl]
            vj = vx_ref[:, sl]
        vaug = jnp.concatenate([vj, jnp.ones_like(vj)], axis=1)
        qs = jnp.concatenate([q[:, (j * GQA + g) * HEAD_DIM:(j * GQA + g + 1) * HEAD_DIM]
                              for g in range(GQA)], axis=0)
        s = lax.dot_general(qs, kj, (((1,), (1,)), ((), ())), preferred_element_type=F32)
        probs, sink_terms = [], []
        for g in range(GQA):
            sg = s[g * ATTN_BLK:(g + 1) * ATTN_BLK]
            if band:
                sg = jnp.concatenate([sg[:, :edges[0]], sg[:, edges[0]:edges[1]] + bias_prev,
                                      sg[:, edges[1]:edges[2]], sg[:, edges[2]:] + bias_next], axis=1)
            sk = sink_ref[j * GQA + g] * LOG2E
            mx = jnp.maximum(jnp.max(sg, axis=-1, keepdims=True), sk)
            probs.append(jnp.exp2(sg - mx).astype(BF16))
            sink_terms.append(jnp.exp2(sk - mx))
        o = _dot(jnp.concatenate(probs, axis=0), vaug)
        for g in range(GQA):
            og = o[g * ATTN_BLK:(g + 1) * ATTN_BLK]
            heads.append(og / (og[:, HEAD_DIM:HEAD_DIM + 1] + sink_terms[g]))
    tiles = [jnp.where(lane < HEAD_DIM, heads[2 * t], pltpu.roll(heads[2 * t + 1], HEAD_DIM, 1))
             for t in range(N_Q_HEADS // 2)]
    o_ref[...] = jnp.concatenate(tiles, axis=1).astype(BF16)


def _attn_call(sink_l, q, k, v):
    nb = S // ATTN_BLK
    smem = pl.BlockSpec(memory_space=pltpu.SMEM)
    qspec = pl.BlockSpec((ATTN_BLK, Q_W), lambda b, n: (b * nb + n, 0))
    prev = pl.BlockSpec((ATTN_BLK, KV_W), lambda b, n: (b * nb + jnp.maximum(n - 1, 0), 0))
    cur = pl.BlockSpec((ATTN_BLK, KV_W), lambda b, n: (b * nb + n, 0))
    nxt = pl.BlockSpec((ATTN_BLK, KV_W), lambda b, n: (b * nb + jnp.minimum(n + 1, nb - 1), 0))
    ctxs = pl.BlockSpec((C, KV_W), lambda b, n: (N_LAT // C + b, 0))
    lat = pl.pallas_call(
        functools.partial(_attn_kernel, band=True),
        out_shape=jax.ShapeDtypeStruct((N_LAT, Q_W), BF16),
        grid=(B, nb),
        in_specs=[smem, qspec, prev, cur, nxt, prev, cur, nxt, ctxs, ctxs],
        out_specs=qspec,
        compiler_params=_cparams(("parallel", "parallel")),
        name="attn_latent",
    )(sink_l, q, k, k, k, v, v, v, k, v)
    ncb = C // ATTN_BLK
    base = N_LAT // ATTN_BLK
    ctx = pl.pallas_call(
        functools.partial(_attn_kernel, band=False),
        out_shape=jax.ShapeDtypeStruct((N_CTX, Q_W), BF16),
        grid=(B, ncb),
        in_specs=[smem, pl.BlockSpec((ATTN_BLK, Q_W), lambda b, n: (base + b * ncb + n, 0)), ctxs, ctxs],
        out_specs=pl.BlockSpec((ATTN_BLK, Q_W), lambda b, n: (b * ncb + n, 0)),
        compiler_params=_cparams(("parallel", "parallel")),
        name="attn_context",
    )(sink_l, q, k, v)
    return lat, ctx


def _f1_kernel(w_ref, f_ref, re_ref, im_ref):
    res = _dot(w_ref[...], f_ref[...])
    re_ref[...] = res[:FS1].astype(BF16)
    im_ref[...] = res[FS1:].astype(BF16)


def _f2_kernel(ta_ref, tb_ref, re_ref, im_ref, o_ref):
    res = _dot(ta_ref[...], re_ref[...]) + _dot(tb_ref[...], im_ref[...])
    o_ref[:, 0:FOURIER_W] = res[:FS2].astype(BF16)
    o_ref[:, FOURIER_W:2 * FOURIER_W] = res[FS2:].astype(BF16)


def _fc_kernel(w_ref, f_ref, o_ref):
    res = _dot(w_ref[...], f_ref[...])
    o_ref[:, 0:FOURIER_W] = res[:C].astype(BF16)
    o_ref[:, FOURIER_W:2 * FOURIER_W] = res[C:].astype(BF16)


def _fourier_call(f, tabs):
    w1, ta, tb, wc = tabs
    f2d = f.reshape(N_TOK // FS2, F1_COLS)
    nchunk = F1_COLS // F1_CW
    a_re, a_im = pl.pallas_call(
        _f1_kernel,
        out_shape=[jax.ShapeDtypeStruct((B * FS1, F1_COLS), BF16)] * 2,
        grid=(B, nchunk),
        in_specs=[_const_spec((2 * FS1, FS1)),
                  pl.BlockSpec((FS1, F1_CW), lambda b, j: (b, j))],
        out_specs=[pl.BlockSpec((FS1, F1_CW), lambda b, j: (b, j))] * 2,
        compiler_params=_cparams(("parallel", "parallel")),
        name="fourier_stage1",
    )(w1, f2d)
    a_re = a_re.reshape(B * FS1, FS2, FOURIER_W)
    a_im = a_im.reshape(B * FS1, FS2, FOURIER_W)
    aspec = pl.BlockSpec((None, FS2, FOURIER_W), lambda b, k1: (b * FS1 + k1, 0, 0))
    tspec = pl.BlockSpec((None, 2 * FS2, FS2), lambda b, k1: (k1, 0, 0))
    h_t = pl.pallas_call(
        _f2_kernel,
        out_shape=jax.ShapeDtypeStruct((B, FS1, FS2, 2 * FOURIER_W), BF16),
        grid=(B, FS1),
        in_specs=[tspec, tspec, aspec, aspec],
        out_specs=pl.BlockSpec((None, None, FS2, 2 * FOURIER_W), lambda b, k1: (b, k1, 0, 0)),
        compiler_params=_cparams(("parallel", "parallel")),
        name="fourier_stage2",
    )(ta, tb, a_re, a_im)
    h_lat = jnp.transpose(h_t, (0, 2, 1, 3)).reshape(N_LAT, 2 * FOURIER_W)
    h_ctx = pl.pallas_call(
        _fc_kernel,
        out_shape=jax.ShapeDtypeStruct((N_CTX, 2 * FOURIER_W), BF16),
        grid=(B,),
        in_specs=[_const_spec((2 * C, C)),
                  pl.BlockSpec((C, FOURIER_W), lambda b: (N_LAT // C + b, 0))],
        out_specs=pl.BlockSpec((C, 2 * FOURIER_W), lambda b: (b, 0)),
        compiler_params=_cparams(("parallel",)),
        name="fourier_context",
    )(wc, f)
    return h_lat, h_ctx


def _poolconv_kernel(pc_ref, pp_ref, pn_ref, uc_ref, up_ref, un_ref, bm_ref, bh_ref, pw_ref, ps_ref,
                     sh_ref, cw_ref, cb_ref, cg_ref, cnb_ref, z_ref, a_ref, win0_ref, win1_ref, cv_ref):
    t = pl.program_id(0)
    lat_tiles = N_LAT // TP
    per_seq = S // TP
    is_ctx = t >= lat_tiles
    first = jnp.logical_or(t % per_seq == 0, is_ctx)
    last = jnp.logical_or(t % per_seq == per_seq - 1, is_ctx)
    seq_len = jnp.where(is_ctx, C, S)
    pos0 = jnp.where(is_ctx, 0, (t % per_seq) * TP)

    pcur = pc_ref[...]
    keep_prev = jnp.where(first, 0.0, 1.0)
    keep_next = jnp.where(last, 0.0, 1.0)
    halo = jnp.concatenate([pp_ref[...].astype(F32) * keep_prev,
                            pn_ref[...].astype(F32) * keep_next], axis=0).astype(BF16)
    lane = lax.broadcasted_iota(jnp.int32, (TP, POOL_W), 1)
    pos = lax.broadcasted_iota(jnp.int32, (TP, POOL_W), 0) + pos0
    sums = []
    for gi in range(len(POOL_WINDOWS)):
        cs = slice(gi * LANES, (gi + 2) * LANES)
        sums.append(_dot(bm_ref[gi], pcur[:, cs]) + _dot(bh_ref[gi], halo[:, cs]))
    lane_t = lax.broadcasted_iota(jnp.int32, (TP, LANES), 1)
    tiles = [sums[0][:, :LANES]]
    for gi in range(1, len(POOL_WINDOWS)):
        split = gi * POOL_GROUP_W - gi * LANES
        tiles.append(jnp.where(lane_t < split, sums[gi - 1][:, LANES:], sums[gi][:, :LANES]))
    tiles.append(sums[-1][:, LANES:])
    zsum = jnp.concatenate(tiles, axis=1)
    win = jnp.full((TP, POOL_W), POOL_WINDOWS[0], jnp.int32)
    for gi in range(1, len(POOL_WINDOWS)):
        win = jnp.where(lane >= gi * POOL_GROUP_W, POOL_WINDOWS[gi], win)
    half = win // 2
    lo = jnp.clip(pos - half, 0, seq_len)
    hi = jnp.clip(pos - half + win, 0, seq_len)
    z = zsum / (hi - lo).astype(F32) - pcur.astype(F32)
    z_ref[...] = (_dot(z.astype(BF16), pw_ref[...]) * ps_ref[...]).astype(BF16)

    ub = jnp.concatenate([(up_ref[...].astype(F32) * keep_prev).astype(BF16), uc_ref[...],
                          (un_ref[...].astype(F32) * keep_next).astype(BF16)], axis=0)
    off = HALO - CONV_K // 2
    half_rows = TP // 2
    for hf, win_ref in enumerate((win0_ref, win1_ref)):
        base = hf * half_rows
        window = ub[base:base + CONV_WIN]
        win_ref[0] = window.astype(F32)
        for s in range(1, 8):
            win_ref[s] = _dot(sh_ref[s - 1], window)
        for cb in range(CONV_W // LANES):
            cs = slice(cb * LANES, (cb + 1) * LANES)
            acc = jnp.zeros((half_rows, LANES), F32) + cb_ref[:, cs]
            for j in range(CONV_K):
                s, m = (off + j) % 8, (off + j) // 8
                acc = acc + win_ref[s, 8 * m:8 * m + half_rows, cs] * cw_ref[j:j + 1, cs]
            cv_ref[base:base + half_rows, cs] = acc
    cv = cv_ref[...]
    mu = jnp.mean(cv, axis=-1, keepdims=True)
    var = jnp.mean(jnp.square(cv - mu), axis=-1, keepdims=True)
    un = (cv - mu) * lax.rsqrt(var + EPS) * cg_ref[...] + cnb_ref[...]
    a_ref[...] = (un * _sigmoid(un)).astype(BF16)


def _poolconv_call(p, u, band_main, band_halo, pw_bd, pool_scale, shifts, conv_w, conv_b, cn_g, cn_b):
    nh = TP // HALO
    last_h = N_TOK // HALO - 1
    cur = lambda w: pl.BlockSpec((TP, w), lambda t: (t, 0))
    prv = lambda w: pl.BlockSpec((HALO, w), lambda t: (jnp.maximum(t * nh - 1, 0), 0))
    nxt = lambda w: pl.BlockSpec((HALO, w), lambda t: (jnp.minimum((t + 1) * nh, last_h), 0))
    return pl.pallas_call(
        _poolconv_kernel,
        out_shape=[jax.ShapeDtypeStruct((N_TOK, POOL_W), BF16),
                   jax.ShapeDtypeStruct((N_TOK, CONV_W), BF16)],
        grid=(NTP,),
        in_specs=[cur(POOL_W), prv(POOL_W), nxt(POOL_W), cur(CONV_W), prv(CONV_W), nxt(CONV_W),
                  _const_spec((4, TP, TP)), _const_spec((4, TP, 2 * HALO)),
                  _const_spec((POOL_W, POOL_W)), _const_spec((1, POOL_W)),
                  _const_spec((7, CONV_WIN, CONV_WIN)),
                  _const_spec((CONV_K, CONV_W)), _const_spec((1, CONV_W)),
                  _const_spec((1, CONV_W)), _const_spec((1, CONV_W))],
        out_specs=[cur(POOL_W), cur(CONV_W)],
        scratch_shapes=[pltpu.VMEM((8, CONV_WIN, CONV_W), F32), pltpu.VMEM((8, CONV_WIN, CONV_W), F32),
                        pltpu.VMEM((TP, CONV_W), F32)],
        compiler_params=_cparams(("parallel",)),
        name="pool_conv",
    )(p, p, p, u, u, u, band_main, band_halo, pw_bd, pool_scale, shifts, conv_w, conv_b, cn_g, cn_b)


def _route(logits):
    lane = lax.broadcasted_iota(jnp.int32, logits.shape, 1)
    big = jnp.int32(LANES)
    lg = jnp.where(lane < N_GROUPS, logits, NEG_INF)
    mg = jnp.max(lg, axis=-1, keepdims=True)
    grp = jnp.min(jnp.where(lg == mg, lane, big), axis=-1, keepdims=True)
    p_grp = 1.0 / jnp.sum(jnp.exp(lg - mg), axis=-1, keepdims=True)
    lo = N_GROUPS + grp * EPG
    le = jnp.where((lane >= lo) & (lane < lo + EPG), logits, NEG_INF)
    m1 = jnp.max(le, axis=-1, keepdims=True)
    i1 = jnp.min(jnp.where(le == m1, lane, big), axis=-1, keepdims=True)
    le2 = jnp.where(lane == i1, NEG_INF, le)
    m2 = jnp.max(le2, axis=-1, keepdims=True)
    i2 = jnp.min(jnp.where(le2 == m2, lane, big), axis=-1, keepdims=True)
    r = jnp.exp(m2 - m1)
    w1 = p_grp / (1.0 + r)
    w2 = p_grp * r / (1.0 + r)
    e1 = (i1 - N_GROUPS).astype(F32)
    e2 = (i2 - N_GROUPS).astype(F32)
    return jnp.where(lane == 0, e1, jnp.where(lane == 1, e2, jnp.where(lane == 2, w1,
                     jnp.where(lane == 3, w2, 0.0))))


def _mix_kernel(x_ref, mod_ref, gn_ref, al_ref, ac_ref, hl_ref, hc_ref, z_ref, cv_ref,
                wa_ref, wf_ref, wp_ref, wc_ref, wg_ref, bg_ref, wo_ref, gf_ref, rh_ref, rl_ref, rb_ref,
                xo_ref, h2_ref, rt_ref):
    is_ctx = pl.program_id(0) >= LAT_TILES
    m = mod_ref[...]
    x = x_ref[...]
    hb = _modulate(x, gn_ref[...], m[0:1], m[1:2]).astype(BF16)
    attn = jnp.where(is_ctx, ac_ref[...], al_ref[...])
    four = jnp.where(is_ctx, hc_ref[...], hl_ref[...])
    branches = ((attn, wa_ref), (four, wf_ref), (z_ref[...], wp_ref), (cv_ref[...], wc_ref))
    acc = None
    for bi, (inp, w_ref) in enumerate(branches):
        cs = slice(bi * D, (bi + 1) * D)
        gate = _sigmoid(_dot(hb, wg_ref[:, cs]) + bg_ref[:, cs])
        term = gate * _dot(inp, w_ref[...])
        acc = term if acc is None else acc + term
    x_new = x + m[2:3] * _dot(acc.astype(BF16), wo_ref[...])
    xo_ref[...] = x_new
    h2 = _modulate(x_new, gf_ref[...], m[3:4], m[4:5])
    hi = h2.astype(BF16)
    h2_ref[...] = hi
    lo = (h2 - hi.astype(F32)).astype(BF16)
    logits = _dot(hi, rh_ref[...]) + _dot(lo, rh_ref[...]) + _dot(hi, rl_ref[...]) + rb_ref[...]
    rt_ref[...] = _route(logits)


def _mix_call(x, mods_l, gn, a_lat, a_ctx, h_lat, h_ctx, zc, cact, l, stacked, small):
    tok = lambda w: pl.BlockSpec((TM, w), lambda i: (i, 0))
    lat = lambda w: pl.BlockSpec((TM, w), lambda i: (jnp.minimum(i, LAT_TILES - 1), 0))
    wa, wf, wp, wc, wg, wo, rh, rl = stacked
    bg, gf, rb = small
    in_specs = [tok(D), pl.BlockSpec((None, 6, D), lambda i: (_mod_row(i), 0, 0)), _const_spec((1, D)),
                lat(Q_W), _const_spec((N_CTX, Q_W)),
                lat(2 * FOURIER_W), _const_spec((N_CTX, 2 * FOURIER_W)),
                tok(POOL_W), tok(CONV_W)]
    in_specs += [_layer_spec(w.shape[1:], l) for w in (wa, wf, wp, wc, wg)]
    in_specs += [_const_spec(bg.shape), _layer_spec(wo.shape[1:], l), _const_spec(gf.shape),
                 _layer_spec(rh.shape[1:], l), _layer_spec(rl.shape[1:], l), _const_spec(rb.shape)]
    return pl.pallas_call(
        _mix_kernel,
        out_shape=[jax.ShapeDtypeStruct((N_TOK, D), F32), jax.ShapeDtypeStruct((N_TOK, D), BF16),
                   jax.ShapeDtypeStruct((N_TOK, LANES), F32)],
        grid=(NT,),
        in_specs=in_specs,
        out_specs=[tok(D), tok(D), tok(LANES)],
        compiler_params=_cparams(("parallel",)),
        name="mix",
    )(x, mods_l, gn, a_lat, a_ctx, h_lat, h_ctx, zc, cact, wa, wf, wp, wc, wg, bg, wo, gf, rh, rl, rb)


def _onehots(route):
    lane = lax.broadcasted_iota(jnp.int32, route.shape, 1)
    e1 = route[:, 0:1].astype(jnp.int32)
    e2 = route[:, 1:2].astype(jnp.int32)
    return (lane == e1).astype(F32), (lane == e2).astype(F32)


def _lane_cumsum(row):
    lane = lax.broadcasted_iota(jnp.int32, row.shape, 1)
    sh = 1
    while sh < N_EXPERTS:
        row = row + jnp.where(lane >= sh, pltpu.roll(row, sh, 1), 0.0)
        sh *= 2
    return row


def _rank_kernel(rt_ref, tri_ref, pos_ref, meta_ref, cnt_ref, carry_ref):
    i = pl.program_id(0)

    @pl.when(i == 0)
    def _():
        carry_ref[...] = jnp.zeros_like(carry_ref)

    oh1, oh2 = _onehots(rt_ref[...])
    both = oh1 + oh2
    carry = carry_ref[0:1, :]
    tile_cnt = jnp.sum(both, axis=0, keepdims=True)
    tile_cnt = tile_cnt + (tile_cnt - 2.0 * jnp.floor(tile_cnt * 0.5))
    tile_off = _lane_cumsum(tile_cnt) - tile_cnt
    where = _dot(tri_ref[...], both.astype(BF16)) + tile_off
    p1 = jnp.sum(oh1 * where, axis=-1, keepdims=True)
    p2 = jnp.sum(oh2 * where, axis=-1, keepdims=True)
    lane = lax.broadcasted_iota(jnp.int32, both.shape, 1)
    pos_ref[...] = jnp.where(lane == 0, p1, jnp.where(lane == 1, p2, 0.0))
    row = lax.broadcasted_iota(jnp.int32, meta_ref.shape, 0)
    meta_ref[...] = jnp.where(row == 0, tile_off, jnp.where(row == 1, tile_cnt, jnp.where(row == 2, carry, 0.0)))
    total = carry + tile_cnt
    carry_ref[...] = jnp.broadcast_to(total, carry_ref.shape)
    cnt_ref[...] = jnp.broadcast_to(total, cnt_ref.shape)


def _runs_kernel(meta_ref, cnt_ref, runs_ref, be_ref):
    lane = lax.broadcasted_iota(jnp.int32, (1, LANES), 1)
    counts = cnt_ref[0:1, :]
    padded = jnp.floor((counts + (RUN_ROWS - 1 + MOE_BLK - 1)) * (1.0 / MOE_BLK)) * MOE_BLK
    padded = jnp.where(lane < N_EXPERTS, padded, 0.0)
    ends = _lane_cumsum(padded)
    starts = ends - padded
    for t in range(NT):
        m = meta_ref[t]
        row = lax.broadcasted_iota(jnp.int32, m.shape, 0)
        runs_ref[t] = jnp.where(row == 2, m + starts, m).astype(jnp.int32)
    blk = lax.broadcasted_iota(jnp.int32, be_ref.shape, 0).astype(F32) * MOE_BLK
    lane_b = lax.broadcasted_iota(jnp.int32, be_ref.shape, 1)
    done = jnp.where((ends <= blk) & (lane_b < N_EXPERTS), 1.0, 0.0)
    be = jnp.minimum(jnp.sum(done, axis=-1, keepdims=True), N_EXPERTS - 1.0)
    nblk = jnp.max(jnp.where(lane_b == N_EXPERTS - 1, ends, 0.0), axis=-1, keepdims=True) * (1.0 / MOE_BLK)
    be_ref[...] = jnp.where(lane_b == 0, be, jnp.where(lane_b == 1, nblk, 0.0)).astype(jnp.int32)


def _plan_call(route, tri):
    tok = pl.BlockSpec((TM, LANES), lambda i: (i, 0))
    pos, meta, counts = pl.pallas_call(
        _rank_kernel,
        out_shape=[jax.ShapeDtypeStruct((N_TOK, LANES), F32), jax.ShapeDtypeStruct((NT, 8, LANES), F32),
                   jax.ShapeDtypeStruct((8, LANES), F32)],
        grid=(NT,),
        in_specs=[tok, _const_spec((TM, TM))],
        out_specs=[tok, pl.BlockSpec((None, 8, LANES), lambda i: (i, 0, 0)),
                   pl.BlockSpec((8, LANES), lambda i: (0, 0))],
        scratch_shapes=[pltpu.VMEM((8, LANES), F32)],
        compiler_params=_cparams(("arbitrary",)),
        name="moe_rank",
    )(route, tri)
    runs, blk = pl.pallas_call(
        _runs_kernel,
        out_shape=[jax.ShapeDtypeStruct((NT, 8, LANES), jnp.int32),
                   jax.ShapeDtypeStruct((256, LANES), jnp.int32)],
        name="moe_runs",
    )(meta, counts)
    runs_flat = runs[:, 0:3, 0:N_EXPERTS].reshape(NT, 1, 3 * N_EXPERTS)
    return pos, runs_flat, blk[:N_MOE_BLOCKS, 0], blk[0:1, 1]


def _pack_pairs(x):
    half = x.shape[1] // 2
    lo = pltpu.bitcast(x[:, :half], jnp.uint32)
    hi = pltpu.bitcast(x[:, half:], jnp.uint32)
    return (lo >> 16) | (hi & jnp.uint32(0xFFFF0000))


def _unpack_pairs(w):
    lo = pltpu.bitcast(w << 16, F32)
    hi = pltpu.bitcast(w & jnp.uint32(0xFFFF0000), F32)
    return jnp.concatenate([lo, hi], axis=1).astype(BF16)


def _run_fields(runs_ref, e):
    return runs_ref[0, e], runs_ref[0, N_EXPERTS + e], runs_ref[0, 2 * N_EXPERTS + e]


def _store_rows(lin_ref, packed):
    rows = packed.shape[0]
    for c in range(PK):
        lin_ref[pl.ds(c, rows, stride=PK), :] = packed[:, c * LANES:(c + 1) * LANES]


def _load_rows(lin_ref, rows):
    return jnp.concatenate([lin_ref[pl.ds(c, rows, stride=PK), :] for c in range(PK)], axis=1)


def _lin(ref, row, nrows):
    return ref.at[pl.ds(pl.multiple_of(row * PK, 8), nrows * PK), :]


def _dispatch_kernel(runs_ref, pos_ref, h2_ref, xs_in_ref, xs_ref, buf_ref, sem):
    del xs_in_ref

    @pl.when(pl.program_id(0) == 0)
    def _():
        buf_ref[TS * PK:, :] = jnp.zeros((RUN_ROWS * PK, LANES), jnp.uint32)

    pos = pos_ref[...]
    col = lax.broadcasted_iota(jnp.int32, (TM, TS), 1).astype(F32)
    sel = jnp.where((col == pos[:, 0:1]) | (col == pos[:, 1:2]), 1.0, 0.0).astype(BF16)
    srt = lax.dot_general(sel, h2_ref[...], (((0,), (0,)), ((), ())), preferred_element_type=F32)
    _store_rows(buf_ref, _pack_pairs(srt))

    def chunk_copy(off, dst, k):
        return pltpu.make_async_copy(_lin(buf_ref, off + k * RUN_ROWS, RUN_ROWS),
                                     _lin(xs_ref, dst + k * RUN_ROWS, RUN_ROWS), sem)

    def start(e, carry):
        off, n, dst = _run_fields(runs_ref, e)
        lax.fori_loop(0, (n + RUN_ROWS - 1) // RUN_ROWS, lambda k, c: (chunk_copy(off, dst, k).start(), c)[1], 0)
        return carry

    def wait(e, carry):
        off, n, dst = _run_fields(runs_ref, e)
        lax.fori_loop(0, (n + RUN_ROWS - 1) // RUN_ROWS, lambda k, c: (chunk_copy(off, dst, k).wait(), c)[1], 0)
        return carry

    lax.fori_loop(0, N_EXPERTS, start, 0)
    lax.fori_loop(0, N_EXPERTS, wait, 0)


def _dispatch_call(runs_flat, pos, h2, zeros_slots):
    return pl.pallas_call(
        _dispatch_kernel,
        out_shape=jax.ShapeDtypeStruct((N_SLOTS * PK, LANES), jnp.uint32),
        grid=(NT,),
        in_specs=[pl.BlockSpec((None, 1, 3 * N_EXPERTS), lambda i: (i, 0, 0), memory_space=pltpu.SMEM),
                  pl.BlockSpec((TM, LANES), lambda i: (i, 0)),
                  pl.BlockSpec((TM, D), lambda i: (i, 0)),
                  pl.BlockSpec(memory_space=pl.ANY)],
        out_specs=pl.BlockSpec(memory_space=pl.ANY),
        scratch_shapes=[pltpu.VMEM(((TS + RUN_ROWS) * PK, LANES), jnp.uint32), pltpu.SemaphoreType.DMA],
        input_output_aliases={3: 0},
        compiler_params=_cparams(("arbitrary",)),
        name="moe_dispatch",
    )(runs_flat, pos, h2, zeros_slots)


def _expert_kernel(be_ref, nu_ref, xs_ref, wg_ref, wu_ref, wd_ref, ys_ref, wgb_ref, wub_ref, wdb_ref):
    b = pl.program_id(0)

    @pl.when(jnp.logical_or(b == 0, be_ref[b] != be_ref[jnp.maximum(b - 1, 0)]))
    def _():
        wgb_ref[...] = wg_ref[...].astype(BF16)
        wub_ref[...] = wu_ref[...].astype(BF16)
        wdb_ref[...] = wd_ref[...].astype(BF16)

    @pl.when(b < nu_ref[0])
    def _():
        xb = _unpack_pairs(_load_rows(xs_ref, MOE_BLK))
        g = _dot(xb, wgb_ref[...])
        u = _dot(xb, wub_ref[...])
        hmid = (g * _sigmoid(g)) * u
        y = _dot(hmid.astype(BF16), wdb_ref[...])
        _store_rows(ys_ref, _pack_pairs(y.astype(BF16).astype(F32)))

    @pl.when(b >= nu_ref[0])
    def _():
        ys_ref[...] = jnp.zeros_like(ys_ref)


def _expert_call(blk_e, n_used, xs, wg, wu, wd, l):
    wspec = lambda k, n: pl.BlockSpec((None, None, k, n), lambda b, be, nu: (l, be[b], 0, 0))
    return pl.pallas_call(
        _expert_kernel,
        out_shape=jax.ShapeDtypeStruct((N_SLOTS * PK, LANES), jnp.uint32),
        grid_spec=pltpu.PrefetchScalarGridSpec(
            num_scalar_prefetch=2,
            grid=(N_MOE_BLOCKS,),
            in_specs=[pl.BlockSpec((MOE_BLK * PK, LANES), lambda b, be, nu: (jnp.minimum(b, nu[0] - 1), 0)),
                      wspec(D, EXPERT_HIDDEN), wspec(D, EXPERT_HIDDEN), wspec(EXPERT_HIDDEN, D)],
            out_specs=pl.BlockSpec((MOE_BLK * PK, LANES), lambda b, be, nu: (b, 0)),
            scratch_shapes=[pltpu.VMEM((D, EXPERT_HIDDEN), BF16), pltpu.VMEM((D, EXPERT_HIDDEN), BF16),
                            pltpu.VMEM((EXPERT_HIDDEN, D), BF16)]),
        compiler_params=_cparams(("arbitrary",)),
        name="moe_experts",
    )(blk_e, n_used, xs, wg, wu, wd)


RUN_PIECES = (32, 16, 8, 4, 2)


def _combine_kernel(runs_ref, ys_ref, pos_ref, x_ref, rt_ref, mod_ref, o_ref, buf_ref, sem):
    @pl.when(pl.program_id(0) == 0)
    def _():
        buf_ref[...] = jnp.zeros_like(buf_ref)

    def piece(off, dst, size):
        return pltpu.make_async_copy(_lin(ys_ref, dst, size), _lin(buf_ref, off, size), sem)

    def run_copies(e, act):
        off, n, dst = _run_fields(runs_ref, e)
        whole = n // RUN_ROWS

        def chunk(k, c):
            act(piece(off + k * RUN_ROWS, dst + k * RUN_ROWS, RUN_ROWS))
            return c

        lax.fori_loop(0, whole, chunk, 0)
        done = whole * RUN_ROWS
        for size in RUN_PIECES:
            @pl.when((n & size) != 0)
            def _(done=done, size=size):
                act(piece(off + done, dst + done, size))
            done = done + (n & size)

    lax.fori_loop(0, N_EXPERTS, lambda e, c: (run_copies(e, lambda d: d.start()), c)[1], 0)
    lax.fori_loop(0, N_EXPERTS, lambda e, c: (run_copies(e, lambda d: d.wait()), c)[1], 0)

    ysb = _unpack_pairs(_load_rows(buf_ref, TS))
    pos = pos_ref[...]
    rt = rt_ref[...]
    col = lax.broadcasted_iota(jnp.int32, (TM, TS), 1).astype(F32)
    y = None
    for k in range(TOP_K):
        pick = jnp.where(col == pos[:, k:k + 1], 1.0, 0.0).astype(BF16)
        term = rt[:, TOP_K + k:TOP_K + k + 1] * _dot(pick, ysb)
        y = term if y is None else y + term
    o_ref[...] = x_ref[...] + mod_ref[5:6, :] * y


def _combine_call(runs_flat, ys, pos, x, route, mods_l, n_tiles):
    tok = lambda w: pl.BlockSpec((TM, w), lambda i: (i, 0))
    return pl.pallas_call(
        _combine_kernel,
        out_shape=jax.ShapeDtypeStruct((n_tiles * TM, D), F32),
        grid=(n_tiles,),
        in_specs=[pl.BlockSpec((None, 1, 3 * N_EXPERTS), lambda i: (i, 0, 0), memory_space=pltpu.SMEM),
                  pl.BlockSpec(memory_space=pl.ANY),
                  tok(LANES), tok(D), tok(LANES),
                  pl.BlockSpec((None, 6, D), lambda i: (_mod_row(i), 0, 0))],
        out_specs=tok(D),
        scratch_shapes=[pltpu.VMEM((TS * PK, LANES), jnp.uint32), pltpu.SemaphoreType.DMA],
        compiler_params=_cparams(("arbitrary",)),
        name="moe_combine",
    )(runs_flat, ys, pos, x, route, mods_l)


def _rope_tables():
    nf = HEAD_DIM // 4
    inv = ROPE_BASE ** (-jnp.arange(nf, dtype=F32) / nf)
    t = jnp.arange(S)
    row = (t // GRID_W).astype(F32)[:, None] * inv[None, :]
    col = (t % GRID_W).astype(F32)[:, None] * inv[None, :]
    zero = jnp.zeros_like(row)
    cos = jnp.concatenate([jnp.cos(row), jnp.cos(row), jnp.cos(col), jnp.cos(col)], axis=1)
    sa = jnp.concatenate([-jnp.sin(row), zero, -jnp.sin(col), zero], axis=1)
    sb = jnp.concatenate([zero, jnp.sin(row), zero, jnp.sin(col)], axis=1)
    ident = (jnp.ones((TM, HEAD_DIM), F32), jnp.zeros((TM, HEAD_DIM), F32), jnp.zeros((TM, HEAD_DIM), F32))
    return tuple(jnp.tile(jnp.concatenate([a, b], axis=0), (1, LANES // HEAD_DIM))
                 for a, b in zip((cos, sa, sb), ident))


def _fourier_tables():
    s1 = np.arange(FS1)
    ang1 = 2.0 * np.pi * np.outer(s1, s1) / FS1
    w1 = np.concatenate([np.cos(ang1), -np.sin(ang1)], axis=0) / np.sqrt(S)
    k1 = np.arange(FS1)[:, None, None]
    k2 = np.arange(FS2)[None, :, None]
    s2 = np.arange(FS2)[None, None, :]
    ang2 = 2.0 * np.pi * ((k1 + FS1 * k2) * s2 % S) / S
    c2, sn2 = np.cos(ang2), np.sin(ang2)
    ta = np.concatenate([c2, -sn2], axis=1)
    tb = np.concatenate([sn2, c2], axis=1)
    sc = np.arange(C)
    angc = 2.0 * np.pi * np.outer(sc, sc) / C
    wc = np.concatenate([np.cos(angc), -np.sin(angc)], axis=0) / np.sqrt(C)
    return tuple(jnp.asarray(a, F32).astype(BF16) for a in (w1, ta, tb, wc))


def _channel_dft():
    cidx = np.arange(FOURIER_GROUP_W)
    ang = 2.0 * np.pi * np.outer(cidx, cidx) / FOURIER_GROUP_W
    eye = np.eye(FOURIER_W // FOURIER_GROUP_W)
    cw = np.kron(eye, np.cos(ang)) / np.sqrt(FOURIER_GROUP_W)
    sw = np.kron(eye, np.sin(ang)) / np.sqrt(FOURIER_GROUP_W)
    return jnp.asarray(np.concatenate([cw, sw], axis=0), F32)


def _pool_bands():
    t = np.arange(TP)[:, None]
    main, halo = [], []
    for w in POOL_WINDOWS:
        def hit(j):
            return ((j - t >= -(w // 2)) & (j - t <= w // 2 - 1)).astype(np.float32)
        main.append(hit(np.arange(TP)[None, :]))
        halo.append(np.concatenate([hit(np.arange(-HALO, 0)[None, :]),
                                    hit(np.arange(TP, TP + HALO)[None, :])], axis=1))
    return (jnp.asarray(np.stack(main), F32).astype(BF16), jnp.asarray(np.stack(halo), F32).astype(BF16))


def _conv_shifts():
    i = np.arange(CONV_WIN)
    return jnp.asarray(np.stack([(i[None, :] == i[:, None] + s) for s in range(1, 8)]), F32).astype(BF16)


def _fold_kernel(a_ref, b_ref, o_ref):
    o_ref[...] = jnp.dot(a_ref[...], b_ref[...], preferred_element_type=F32,
                         precision=lax.Precision.HIGHEST).astype(BF16)


def _fold_fourier_weights(dftw, w_br_fourier):
    nl = w_br_fourier.shape[0]
    return pl.pallas_call(
        _fold_kernel,
        out_shape=jax.ShapeDtypeStruct((nl, 2 * FOURIER_W, D), BF16),
        grid=(nl,),
        in_specs=[pl.BlockSpec((2 * FOURIER_W, FOURIER_W), lambda l: (0, 0)),
                  pl.BlockSpec((None, FOURIER_W, D), lambda l: (l, 0, 0))],
        out_specs=pl.BlockSpec((None, 2 * FOURIER_W, D), lambda l: (l, 0, 0)),
        compiler_params=_cparams(("arbitrary",)),
        name="fold_fourier_proj",
    )(dftw, w_br_fourier)


def _block_diag(blocks):
    n, r, c = blocks.shape
    out = jnp.zeros((n * r, n * c), blocks.dtype)
    for i in range(n):
        out = lax.dynamic_update_slice(out, blocks[i], (i * r, i * c))
    return out


def kernel(x, c, ctx, c_ctx, w_ada, b_ada, g_norm_mix, g_norm_ffn, w_in, g_q, g_k, sink, w_br_attn,
           w_br_fourier, pool_w, pool_scale, w_br_pool, conv_w, conv_b, cn_g, cn_b, w_br_conv, w_gate,
           b_gate, w_out, w_router_grp, b_router_grp, w_router_exp, b_router_exp, w_e_gate, w_e_up,
           w_e_down):
    xs = jnp.concatenate([x.reshape(N_LAT, D), ctx.reshape(N_CTX, D)], axis=0)
    nl = w_ada.shape[0]
    mods = _ada_all(c, c_ctx, w_ada, b_ada).reshape(nl, 8, 6, D)
    rope_tabs = _rope_tables()
    four_tabs = _fourier_tables()
    band_main, band_halo = _pool_bands()
    shifts = _conv_shifts()
    wf_all = _fold_fourier_weights(_channel_dft(), w_br_fourier)
    bd = jnp.asarray(np.kron(np.eye(LANES // HEAD_DIM), np.ones((HEAD_DIM, HEAD_DIM))), F32).astype(BF16)
    tri = jnp.asarray(np.tril(np.ones((TM, TM)), -1), F32).astype(BF16)
    zeros_slots = jnp.zeros((N_SLOTS * PK, LANES), jnp.uint32)
    rpad = jnp.zeros((nl, D, LANES - N_GROUPS - N_EXPERTS), F32)
    w_router = jnp.concatenate([w_router_grp, w_router_exp, rpad], axis=-1)
    r_hi = w_router.astype(BF16)
    r_lo = (w_router - r_hi.astype(F32)).astype(BF16)
    r_b = jnp.concatenate([b_router_grp, b_router_exp, rpad[:, 0, :]], axis=-1).reshape(nl, 1, LANES)
    stacked = tuple(w.astype(BF16) for w in (w_br_attn,)) + (wf_all,) + tuple(
        w.astype(BF16) for w in (w_br_pool, w_br_conv, w_gate, w_out)) + (r_hi, r_lo)

    for l in range(nl):
        mods_l = mods[l]
        gn = g_norm_mix[l].reshape(1, D)
        q, k, v, f, p, u = _proj_call(xs, mods_l, gn, w_in, l, rope_tabs,
                                      jnp.tile(g_q[l], 2).reshape(1, LANES),
                                      jnp.tile(g_k[l], 2).reshape(1, LANES), bd)
        a_lat, a_ctx = _attn_call(sink[l], q, k, v)
        h_lat, h_ctx = _fourier_call(f, four_tabs)
        zc, cact = _poolconv_call(p, u, band_main, band_halo, _block_diag(pool_w[l]).astype(BF16),
                                  pool_scale[l].reshape(1, POOL_W), shifts, conv_w[l], conv_b[l].reshape(1, CONV_W),
                                  cn_g[l].reshape(1, CONV_W), cn_b[l].reshape(1, CONV_W))
        small = (b_gate[l].reshape(1, 4 * D), g_norm_ffn[l].reshape(1, D), r_b[l])
        xs, h2, route = _mix_call(xs, mods_l, gn, a_lat, a_ctx, h_lat, h_ctx, zc, cact, l, stacked, small)
        pos, runs_flat, blk_e, n_used = _plan_call(route, tri)
        slots = _dispatch_call(runs_flat, pos, h2, zeros_slots)
        ys = _expert_call(blk_e, n_used, slots, w_e_gate, w_e_up, w_e_down, l)
        xs = _combine_call(runs_flat, ys, pos, xs, route, mods_l, LAT_TILES if l == nl - 1 else NT)
    return xs.reshape(B, S, D)
```

```python
import numpy as np
import jax
import jax.numpy as jnp
from jax import lax
from jax.experimental import pallas as pl
from jax.experimental.pallas import tpu as pltpu

F32 = jnp.float32
BF16 = jnp.bfloat16

D = 1024
B = 2
S = 8192
C = 256
GRID_W = 64
HEAD_DIM = 64
N_Q_HEADS = 8
N_KV_HEADS = 2
GQA = N_Q_HEADS // N_KV_HEADS
WINDOW = 128
ATTN_BLK = 128
ROPE_BASE = 10000.0
Q_W = 512
KV_W = 128
FOURIER_W = 640
FOURIER_GROUP_W = 160
POOL_W = 640
POOL_GROUP_W = 160
POOL_WINDOWS = (2, 4, 8, 16)
CONV_W = 512
CONV_K = 31
PROJ_W = 3072
N_GROUPS = 4
EPG = 8
N_EXPERTS = 32
TOP_K = 2
EXPERT_HIDDEN = 512
MOE_BLK = 512
EPS = 1e-6
NEG_INF = -1e30
LOG2E = 1.4426950408889634

N_LAT = B * S
N_CTX = B * C
N_TOK = N_LAT + N_CTX
TM = 512
NT = N_TOK // TM
LAT_TILES = N_LAT // TM
TILES_PER_BATCH = S // TM
TP = 256
NTP = N_TOK // TP
HALO = 16
CONV_WIN = TP // 2 + 2 * HALO
N_ASSIGN = N_TOK * TOP_K
RUN_ROWS = 64
PK = D // 2 // 128
TS = 1152
N_MOE_BLOCKS = (N_ASSIGN + NT * N_EXPERTS + N_EXPERTS * (RUN_ROWS - 1 + MOE_BLK - 1)) // MOE_BLK
N_SLOTS = N_MOE_BLOCKS * MOE_BLK
FS1 = 64
FS2 = 128
F1_COLS = FS2 * FOURIER_W
F1_CW = 8192
F2_K1 = 8
LANES = 128
VMEM_LIMIT = 56 * 1024 * 1024


def _cparams(sem, vmem=VMEM_LIMIT):
    return pltpu.CompilerParams(dimension_semantics=sem, vmem_limit_bytes=vmem)


def _const_spec(shape):
    nd = len(shape)
    return pl.BlockSpec(shape, lambda *_: (0,) * nd, pipeline_mode=pl.Buffered(1))


def _dot(a, b):
    return jnp.dot(a, b, preferred_element_type=F32)


def _modulate(x, g, shift, scale):
    y = x * lax.rsqrt(jnp.mean(x * x, axis=-1, keepdims=True) + EPS)
    return (y * g) * (1.0 + scale) + shift


def _sigmoid(x):
    return 1.0 / (1.0 + jnp.exp(-x))


def _ada_kernel(ct_ref, w_ref, b_ref, o_ref):
    ct = ct_ref[...]
    s = ct * _sigmoid(ct)
    w = w_ref[...]
    rows = [jnp.sum(w * s[:, r:r + 1], axis=0, keepdims=True) for r in range(3)]
    rows.append(jnp.zeros((5, w.shape[1]), F32))
    o_ref[...] = jnp.concatenate(rows, axis=0) + b_ref[...]


def _ada_all(c, c_ctx, w_ada, b_ada):
    ct = jnp.concatenate([c, c_ctx[None, :], jnp.zeros((5, D), F32)], axis=0).T
    cols = 1536
    nl = w_ada.shape[0]
    return pl.pallas_call(
        _ada_kernel,
        out_shape=jax.ShapeDtypeStruct((nl, 8, 6 * D), F32),
        grid=(nl, 6 * D // cols),
        in_specs=[pl.BlockSpec((D, 8), lambda l, j: (0, 0)),
                  pl.BlockSpec((None, D, cols), lambda l, j: (l, 0, j)),
                  pl.BlockSpec((None, 1, cols), lambda l, j: (l, 0, j))],
        out_specs=pl.BlockSpec((None, 8, cols), lambda l, j: (l, 0, j)),
        compiler_params=_cparams(("arbitrary", "arbitrary")),
        name="adaln",
    )(ct, w_ada, b_ada.reshape(nl, 1, 6 * D))


def _head_rms(t, g128, bd):
    outs = []
    for j in range(t.shape[1] // LANES):
        blk = t[:, j * LANES:(j + 1) * LANES]
        ss = _dot((blk * blk).astype(BF16), bd)
        outs.append(blk * lax.rsqrt(ss * (1.0 / HEAD_DIM) + EPS) * g128)
    return outs


def _rope(blocks, cos, sa, sb):
    outs = []
    for blk in blocks:
        up = pltpu.roll(blk, LANES - 16, 1)
        dn = pltpu.roll(blk, 16, 1)
        outs.append(blk * cos + up * sa + dn * sb)
    return outs


def _proj_kernel(x_ref, mod_ref, gn_ref, w_ref, cos_ref, sa_ref, sb_ref, gq_ref, gk_ref, bd_ref,
                 q_ref, kv_ref, f_ref, p_ref, u_ref, wbf_ref):
    @pl.when(pl.program_id(0) == 0)
    def _():
        wbf_ref[...] = w_ref[...].astype(BF16)

    m = mod_ref[...]
    h = _modulate(x_ref[...], gn_ref[...], m[0:1], m[1:2])
    proj = _dot(h.astype(BF16), wbf_ref[...])
    cos, sa, sb, bd = cos_ref[...], sa_ref[...], sb_ref[...], bd_ref[...]
    q = _rope(_head_rms(proj[:, 0:Q_W], gq_ref[...], bd), cos, sa, sb)
    q_ref[...] = (jnp.concatenate(q, axis=1) * (LOG2E * HEAD_DIM ** -0.5)).astype(BF16)
    k = _rope(_head_rms(proj[:, Q_W:Q_W + KV_W], gk_ref[...], bd), cos, sa, sb)
    kv_ref[:, 0:KV_W] = k[0].astype(BF16)
    o = Q_W + KV_W
    kv_ref[:, KV_W:2 * KV_W] = proj[:, o:o + KV_W].astype(BF16)
    o += KV_W
    f_ref[...] = proj[:, o:o + FOURIER_W].astype(BF16)
    o += FOURIER_W
    p_ref[...] = proj[:, o:o + POOL_W].astype(BF16)
    o += POOL_W
    a = proj[:, o:o + CONV_W]
    g = proj[:, o + CONV_W:o + 2 * CONV_W]
    u_ref[...] = (a * _sigmoid(g)).astype(BF16)


def _mod_row(i):
    return jnp.minimum(i // TILES_PER_BATCH, 2)


def _layer_spec(shape, l):
    nd = len(shape)
    return pl.BlockSpec((None,) + tuple(shape), lambda *_: (l,) + (0,) * nd, pipeline_mode=pl.Buffered(1))


def _proj_call(x, mods_l, gn, w_in, l, rope_tabs, gq128, gk128, bd):
    cos, sa, sb = rope_tabs
    tok = lambda w: pl.BlockSpec((TM, w), lambda i: (i, 0))
    rope_spec = pl.BlockSpec((TM, LANES), lambda i: (jnp.where(i < LAT_TILES, i % TILES_PER_BATCH,
                                                               TILES_PER_BATCH), 0))
    widths = (Q_W, 2 * KV_W, FOURIER_W, POOL_W, CONV_W)
    return pl.pallas_call(
        _proj_kernel,
        out_shape=[jax.ShapeDtypeStruct((N_TOK, w), BF16) for w in widths],
        grid=(NT,),
        in_specs=[tok(D),
                  pl.BlockSpec((None, 6, D), lambda i: (_mod_row(i), 0, 0)),
                  _const_spec((1, D)),
                  _layer_spec((D, PROJ_W), l),
                  rope_spec, rope_spec, rope_spec,
                  _const_spec((1, LANES)), _const_spec((1, LANES)),
                  _const_spec((LANES, LANES))],
        out_specs=[tok(w) for w in widths],
        scratch_shapes=[pltpu.VMEM((D, PROJ_W), BF16)],
        compiler_params=_cparams(("arbitrary",)),
        name="proj",
    )(x, mods_l, gn, w_in, cos, sa, sb, gq128, gk128, bd)


def _attend(q, kv_blocks, biases, sink_ref):
    lane = lax.broadcasted_iota(jnp.int32, (ATTN_BLK, LANES), 1)
    heads = []
    for j in range(N_KV_HEADS):
        ks = slice(j * HEAD_DIM, (j + 1) * HEAD_DIM)
        vs = slice(KV_W + j * HEAD_DIM, KV_W + (j + 1) * HEAD_DIM)
        kj = jnp.concatenate([blk[:, ks] for blk in kv_blocks], axis=0)
        vj = jnp.concatenate([blk[:, vs] for blk in kv_blocks], axis=0)
        vaug = jnp.concatenate([vj, jnp.ones_like(vj)], axis=1)
        qs = jnp.concatenate([q[:, (j * GQA + g) * HEAD_DIM:(j * GQA + g + 1) * HEAD_DIM]
                              for g in range(GQA)], axis=0)
        s = lax.dot_general(qs, kj, (((1,), (1,)), ((), ())), preferred_element_type=F32)
        probs, sink_terms = [], []
        for g in range(GQA):
            sg = s[g * ATTN_BLK:(g + 1) * ATTN_BLK]
            pieces, col = [], 0
            for blk, bias in zip(kv_blocks, biases):
                piece = sg[:, col:col + blk.shape[0]]
                pieces.append(piece if bias is None else piece + bias)
                col += blk.shape[0]
            sg = jnp.concatenate(pieces, axis=1)
            sk = sink_ref[j * GQA + g] * LOG2E
            mx = jnp.maximum(jnp.max(sg, axis=-1, keepdims=True), sk)
            probs.append(jnp.exp2(sg - mx).astype(BF16))
            sink_terms.append(jnp.exp2(sk - mx))
        o = _dot(jnp.concatenate(probs, axis=0), vaug)
        for g in range(GQA):
            og = o[g * ATTN_BLK:(g + 1) * ATTN_BLK]
            heads.append(og / (og[:, HEAD_DIM:HEAD_DIM + 1] + sink_terms[g]))
    tiles = [jnp.where(lane < HEAD_DIM, heads[2 * t], pltpu.roll(heads[2 * t + 1], HEAD_DIM, 1))
             for t in range(N_Q_HEADS // 2)]
    return jnp.concatenate(tiles, axis=1).astype(BF16)


def _attn_latent_kernel(sink_ref, q_ref, prev_ref, cur_ref, next_ref, ctx_ref, o_ref):
    n = pl.program_id(1)
    r = lax.broadcasted_iota(jnp.int32, (ATTN_BLK, ATTN_BLK), 0)
    jj = lax.broadcasted_iota(jnp.int32, (ATTN_BLK, ATTN_BLK), 1)
    far = jnp.int32(2 * ATTN_BLK)
    off_prev = jnp.where(n > 0, 0, far)
    off_next = jnp.where(n < pl.num_programs(1) - 1, 0, far)
    prev_ok = jnp.where(jj - r >= 0, 0.0, NEG_INF)
    next_ok = jnp.where(r - jj >= 0, 0.0, NEG_INF)
    prev_edge = jnp.where(jj - r - off_prev >= 0, 0.0, NEG_INF)
    next_edge = jnp.where(r - jj - off_next >= 0, 0.0, NEG_INF)
    ctx = ctx_ref[...]
    cur0, cur1 = cur_ref[0:ATTN_BLK, :], cur_ref[ATTN_BLK:, :]
    o_ref[0:ATTN_BLK, :] = _attend(q_ref[0:ATTN_BLK, :], [ctx, prev_ref[...], cur0, cur1],
                                   [None, prev_edge, None, next_ok], sink_ref)
    o_ref[ATTN_BLK:, :] = _attend(q_ref[ATTN_BLK:, :], [ctx, cur0, cur1, next_ref[...]],
                                  [None, prev_ok, None, next_edge], sink_ref)


def _attn_context_kernel(sink_ref, q_ref, ctx_ref, o_ref):
    o_ref[...] = _attend(q_ref[...], [ctx_ref[...]], [None], sink_ref)


def _attn_call(sink_l, q, kv):
    nb = S // ATTN_BLK
    nq = nb // 2
    smem = pl.BlockSpec(memory_space=pltpu.SMEM)
    pair = lambda w: pl.BlockSpec((2 * ATTN_BLK, w), lambda b, n: (b * nq + n, 0))
    prev = pl.BlockSpec((ATTN_BLK, 2 * KV_W), lambda b, n: (b * nb + jnp.maximum(2 * n - 1, 0), 0))
    nxt = pl.BlockSpec((ATTN_BLK, 2 * KV_W), lambda b, n: (b * nb + jnp.minimum(2 * n + 2, nb - 1), 0))
    ctxs = pl.BlockSpec((C, 2 * KV_W), lambda b, n: (N_LAT // C + b, 0))
    lat = pl.pallas_call(
        _attn_latent_kernel,
        out_shape=jax.ShapeDtypeStruct((N_LAT, Q_W), BF16),
        grid=(B, nq),
        in_specs=[smem, pair(Q_W), prev, pair(2 * KV_W), nxt, ctxs],
        out_specs=pair(Q_W),
        compiler_params=_cparams(("parallel", "parallel")),
        name="attn_latent",
    )(sink_l, q, kv, kv, kv, kv)
    ncb = C // ATTN_BLK
    base = N_LAT // ATTN_BLK
    ctx = pl.pallas_call(
        _attn_context_kernel,
        out_shape=jax.ShapeDtypeStruct((N_CTX, Q_W), BF16),
        grid=(B, ncb),
        in_specs=[smem, pl.BlockSpec((ATTN_BLK, Q_W), lambda b, n: (base + b * ncb + n, 0)), ctxs],
        out_specs=pl.BlockSpec((ATTN_BLK, Q_W), lambda b, n: (b * ncb + n, 0)),
        compiler_params=_cparams(("parallel", "parallel")),
        name="attn_context",
    )(sink_l, q, kv)
    return lat, ctx


def _f1_kernel(w_ref, f_ref, re_ref, im_ref):
    res = _dot(w_ref[...], f_ref[...])
    re_ref[...] = res[:FS1].astype(BF16)
    im_ref[...] = res[FS1:].astype(BF16)


def _f2_kernel(ta_ref, tb_ref, re_ref, im_ref, o_ref):
    for i in range(F2_K1):
        res = _dot(ta_ref[i], re_ref[i]) + _dot(tb_ref[i], im_ref[i])
        o_ref[i, :, 0:FOURIER_W] = res[:FS2].astype(BF16)
        o_ref[i, :, FOURIER_W:2 * FOURIER_W] = res[FS2:].astype(BF16)


def _fc_kernel(w_ref, f_ref, o_ref):
    res = _dot(w_ref[...], f_ref[...])
    o_ref[:, 0:FOURIER_W] = res[:C].astype(BF16)
    o_ref[:, FOURIER_W:2 * FOURIER_W] = res[C:].astype(BF16)


def _fourier_call(f, tabs):
    w1, ta, tb, wc = tabs
    f2d = f.reshape(N_TOK // FS2, F1_COLS)
    nchunk = F1_COLS // F1_CW
    a_re, a_im = pl.pallas_call(
        _f1_kernel,
        out_shape=[jax.ShapeDtypeStruct((B * FS1, F1_COLS), BF16)] * 2,
        grid=(B, nchunk),
        in_specs=[_const_spec((2 * FS1, FS1)),
                  pl.BlockSpec((FS1, F1_CW), lambda b, j: (b, j))],
        out_specs=[pl.BlockSpec((FS1, F1_CW), lambda b, j: (b, j))] * 2,
        compiler_params=_cparams(("parallel", "parallel")),
        name="fourier_stage1",
    )(w1, f2d)
    a_re = a_re.reshape(B * FS1, FS2, FOURIER_W)
    a_im = a_im.reshape(B * FS1, FS2, FOURIER_W)
    nk = FS1 // F2_K1
    aspec = pl.BlockSpec((F2_K1, FS2, FOURIER_W), lambda b, k1: (b * nk + k1, 0, 0))
    tspec = pl.BlockSpec((F2_K1, 2 * FS2, FS2), lambda b, k1: (k1, 0, 0))
    h_t = pl.pallas_call(
        _f2_kernel,
        out_shape=jax.ShapeDtypeStruct((B, FS1, FS2, 2 * FOURIER_W), BF16),
        grid=(B, nk),
        in_specs=[tspec, tspec, aspec, aspec],
        out_specs=pl.BlockSpec((None, F2_K1, FS2, 2 * FOURIER_W), lambda b, k1: (b, k1, 0, 0)),
        compiler_params=_cparams(("parallel", "parallel")),
        name="fourier_stage2",
    )(ta, tb, a_re, a_im)
    h_lat = jnp.transpose(h_t, (0, 2, 1, 3)).reshape(N_LAT, 2 * FOURIER_W)
    h_ctx = pl.pallas_call(
        _fc_kernel,
        out_shape=jax.ShapeDtypeStruct((N_CTX, 2 * FOURIER_W), BF16),
        grid=(B,),
        in_specs=[_const_spec((2 * C, C)),
                  pl.BlockSpec((C, FOURIER_W), lambda b: (N_LAT // C + b, 0))],
        out_specs=pl.BlockSpec((C, 2 * FOURIER_W), lambda b: (b, 0)),
        compiler_params=_cparams(("parallel",)),
        name="fourier_context",
    )(wc, f)
    return h_lat, h_ctx


def _poolconv_kernel(pc_ref, pp_ref, pn_ref, uc_ref, up_ref, un_ref, bm_ref, bh_ref, pw_ref, ps_ref,
                     sh_ref, cw_ref, cb_ref, cg_ref, cnb_ref, z_ref, a_ref, win0_ref, win1_ref, cv_ref):
    t = pl.program_id(0)
    lat_tiles = N_LAT // TP
    per_seq = S // TP
    is_ctx = t >= lat_tiles
    first = jnp.logical_or(t % per_seq == 0, is_ctx)
    last = jnp.logical_or(t % per_seq == per_seq - 1, is_ctx)
    seq_len = jnp.where(is_ctx, C, S)
    pos0 = jnp.where(is_ctx, 0, (t % per_seq) * TP)

    pcur = pc_ref[...]
    keep_prev = jnp.where(first, 0.0, 1.0)
    keep_next = jnp.where(last, 0.0, 1.0)
    halo = jnp.concatenate([pp_ref[...].astype(F32) * keep_prev,
                            pn_ref[...].astype(F32) * keep_next], axis=0).astype(BF16)
    lane = lax.broadcasted_iota(jnp.int32, (TP, POOL_W), 1)
    pos = lax.broadcasted_iota(jnp.int32, (TP, POOL_W), 0) + pos0
    sums = []
    for gi in range(len(POOL_WINDOWS)):
        cs = slice(gi * LANES, (gi + 2) * LANES)
        sums.append(_dot(bm_ref[gi], pcur[:, cs]) + _dot(bh_ref[gi], halo[:, cs]))
    lane_t = lax.broadcasted_iota(jnp.int32, (TP, LANES), 1)
    tiles = [sums[0][:, :LANES]]
    for gi in range(1, len(POOL_WINDOWS)):
        split = gi * POOL_GROUP_W - gi * LANES
        tiles.append(jnp.where(lane_t < split, sums[gi - 1][:, LANES:], sums[gi][:, :LANES]))
    tiles.append(sums[-1][:, LANES:])
    zsum = jnp.concatenate(tiles, axis=1)
    win = jnp.full((TP, POOL_W), POOL_WINDOWS[0], jnp.int32)
    for gi in range(1, len(POOL_WINDOWS)):
        win = jnp.where(lane >= gi * POOL_GROUP_W, POOL_WINDOWS[gi], win)
    half = win // 2
    lo = jnp.clip(pos - half, 0, seq_len)
    hi = jnp.clip(pos - half + win, 0, seq_len)
    z = zsum / (hi - lo).astype(F32) - pcur.astype(F32)
    z_ref[...] = (_dot(z.astype(BF16), pw_ref[...]) * ps_ref[...]).astype(BF16)

    ub = jnp.concatenate([(up_ref[...].astype(F32) * keep_prev).astype(BF16), uc_ref[...],
                          (un_ref[...].astype(F32) * keep_next).astype(BF16)], axis=0)
    off = HALO - CONV_K // 2
    half_rows = TP // 2
    for hf, win_ref in enumerate((win0_ref, win1_ref)):
        base = hf * half_rows
        window = ub[base:base + CONV_WIN]
        win_ref[0] = window.astype(F32)
        for s in range(1, 8):
            win_ref[s] = _dot(sh_ref[s - 1], window)
        for cb in range(CONV_W // LANES):
            cs = slice(cb * LANES, (cb + 1) * LANES)
            acc = jnp.zeros((half_rows, LANES), F32) + cb_ref[:, cs]
            for j in range(CONV_K):
                s, m = (off + j) % 8, (off + j) // 8
                acc = acc + win_ref[s, 8 * m:8 * m + half_rows, cs] * cw_ref[j:j + 1, cs]
            cv_ref[base:base + half_rows, cs] = acc
    cv = cv_ref[...]
    mu = jnp.mean(cv, axis=-1, keepdims=True)
    var = jnp.mean(jnp.square(cv - mu), axis=-1, keepdims=True)
    un = (cv - mu) * lax.rsqrt(var + EPS) * cg_ref[...] + cnb_ref[...]
    a_ref[...] = (un * _sigmoid(un)).astype(BF16)


def _poolconv_call(p, u, band_main, band_halo, pw_bd, pool_scale, shifts, conv_w, conv_b, cn_g, cn_b):
    nh = TP // HALO
    last_h = N_TOK // HALO - 1
    cur = lambda w: pl.BlockSpec((TP, w), lambda t: (t, 0))
    prv = lambda w: pl.BlockSpec((HALO, w), lambda t: (jnp.maximum(t * nh - 1, 0), 0))
    nxt = lambda w: pl.BlockSpec((HALO, w), lambda t: (jnp.minimum((t + 1) * nh, last_h), 0))
    return pl.pallas_call(
        _poolconv_kernel,
        out_shape=[jax.ShapeDtypeStruct((N_TOK, POOL_W), BF16),
                   jax.ShapeDtypeStruct((N_TOK, CONV_W), BF16)],
        grid=(NTP,),
        in_specs=[cur(POOL_W), prv(POOL_W), nxt(POOL_W), cur(CONV_W), prv(CONV_W), nxt(CONV_W),
                  _const_spec((4, TP, TP)), _const_spec((4, TP, 2 * HALO)),
                  _const_spec((POOL_W, POOL_W)), _const_spec((1, POOL_W)),
                  _const_spec((7, CONV_WIN, CONV_WIN)),
                  _const_spec((CONV_K, CONV_W)), _const_spec((1, CONV_W)),
                  _const_spec((1, CONV_W)), _const_spec((1, CONV_W))],
        out_specs=[cur(POOL_W), cur(CONV_W)],
        scratch_shapes=[pltpu.VMEM((8, CONV_WIN, CONV_W), F32), pltpu.VMEM((8, CONV_WIN, CONV_W), F32),
                        pltpu.VMEM((TP, CONV_W), F32)],
        compiler_params=_cparams(("parallel",)),
        name="pool_conv",
    )(p, p, p, u, u, u, band_main, band_halo, pw_bd, pool_scale, shifts, conv_w, conv_b, cn_g, cn_b)


def _route(logits):
    lane = lax.broadcasted_iota(jnp.int32, logits.shape, 1)
    big = jnp.int32(LANES)
    lg = jnp.where(lane < N_GROUPS, logits, NEG_INF)
    mg = jnp.max(lg, axis=-1, keepdims=True)
    grp = jnp.min(jnp.where(lg == mg, lane, big), axis=-1, keepdims=True)
    p_grp = 1.0 / jnp.sum(jnp.exp(lg - mg), axis=-1, keepdims=True)
    lo = N_GROUPS + grp * EPG
    le = jnp.where((lane >= lo) & (lane < lo + EPG), logits, NEG_INF)
    m1 = jnp.max(le, axis=-1, keepdims=True)
    i1 = jnp.min(jnp.where(le == m1, lane, big), axis=-1, keepdims=True)
    le2 = jnp.where(lane == i1, NEG_INF, le)
    m2 = jnp.max(le2, axis=-1, keepdims=True)
    i2 = jnp.min(jnp.where(le2 == m2, lane, big), axis=-1, keepdims=True)
    r = jnp.exp(m2 - m1)
    w1 = p_grp / (1.0 + r)
    w2 = p_grp * r / (1.0 + r)
    e1 = (i1 - N_GROUPS).astype(F32)
    e2 = (i2 - N_GROUPS).astype(F32)
    return jnp.where(lane == 0, e1, jnp.where(lane == 1, e2, jnp.where(lane == 2, w1,
                     jnp.where(lane == 3, w2, 0.0))))


def _mix_kernel(x_ref, mod_ref, gn_ref, al_ref, ac_ref, hl_ref, hc_ref, z_ref, cv_ref,
                wa_ref, wf_ref, wp_ref, wc_ref, wg_ref, bg_ref, wo_ref, gf_ref, rh_ref, rl_ref, rb_ref,
                xo_ref, h2_ref, rt_ref):
    is_ctx = pl.program_id(0) >= LAT_TILES
    m = mod_ref[...]
    x = x_ref[...]
    hb = _modulate(x, gn_ref[...], m[0:1], m[1:2]).astype(BF16)
    attn = jnp.where(is_ctx, ac_ref[...], al_ref[...])
    four = jnp.where(is_ctx, hc_ref[...], hl_ref[...])
    branches = ((attn, wa_ref), (four, wf_ref), (z_ref[...], wp_ref), (cv_ref[...], wc_ref))
    acc = None
    for bi, (inp, w_ref) in enumerate(branches):
        cs = slice(bi * D, (bi + 1) * D)
        gate = _sigmoid(_dot(hb, wg_ref[:, cs]) + bg_ref[:, cs])
        term = gate * _dot(inp, w_ref[...])
        acc = term if acc is None else acc + term
    x_new = x + m[2:3] * _dot(acc.astype(BF16), wo_ref[...])
    xo_ref[...] = x_new
    h2 = _modulate(x_new, gf_ref[...], m[3:4], m[4:5])
    hi = h2.astype(BF16)
    h2_ref[...] = hi
    lo = (h2 - hi.astype(F32)).astype(BF16)
    logits = _dot(hi, rh_ref[...]) + _dot(lo, rh_ref[...]) + _dot(hi, rl_ref[...]) + rb_ref[...]
    rt_ref[...] = _route(logits)


def _mix_call(x, mods_l, gn, a_lat, a_ctx, h_lat, h_ctx, zc, cact, l, stacked, small):
    tok = lambda w: pl.BlockSpec((TM, w), lambda i: (i, 0))
    lat = lambda w: pl.BlockSpec((TM, w), lambda i: (jnp.minimum(i, LAT_TILES - 1), 0))
    wa, wf, wp, wc, wg, wo, rh, rl = stacked
    bg, gf, rb = small
    in_specs = [tok(D), pl.BlockSpec((None, 6, D), lambda i: (_mod_row(i), 0, 0)), _const_spec((1, D)),
                lat(Q_W), _const_spec((N_CTX, Q_W)),
                lat(2 * FOURIER_W), _const_spec((N_CTX, 2 * FOURIER_W)),
                tok(POOL_W), tok(CONV_W)]
    in_specs += [_layer_spec(w.shape[1:], l) for w in (wa, wf, wp, wc, wg)]
    in_specs += [_const_spec(bg.shape), _layer_spec(wo.shape[1:], l), _const_spec(gf.shape),
                 _layer_spec(rh.shape[1:], l), _layer_spec(rl.shape[1:], l), _const_spec(rb.shape)]
    return pl.pallas_call(
        _mix_kernel,
        out_shape=[jax.ShapeDtypeStruct((N_TOK, D), F32), jax.ShapeDtypeStruct((N_TOK, D), BF16),
                   jax.ShapeDtypeStruct((N_TOK, LANES), F32)],
        grid=(NT,),
        in_specs=in_specs,
        out_specs=[tok(D), tok(D), tok(LANES)],
        compiler_params=_cparams(("parallel",)),
        name="mix",
    )(x, mods_l, gn, a_lat, a_ctx, h_lat, h_ctx, zc, cact, wa, wf, wp, wc, wg, bg, wo, gf, rh, rl, rb)


def _onehots(route):
    lane = lax.broadcasted_iota(jnp.int32, route.shape, 1)
    e1 = route[:, 0:1].astype(jnp.int32)
    e2 = route[:, 1:2].astype(jnp.int32)
    return (lane == e1).astype(F32), (lane == e2).astype(F32)


def _lane_cumsum(row):
    lane = lax.broadcasted_iota(jnp.int32, row.shape, 1)
    sh = 1
    while sh < N_EXPERTS:
        row = row + jnp.where(lane >= sh, pltpu.roll(row, sh, 1), 0.0)
        sh *= 2
    return row


def _rank_kernel(rt_ref, tri_ref, pos_ref, meta_ref, cnt_ref, carry_ref):
    i = pl.program_id(0)

    @pl.when(i == 0)
    def _():
        carry_ref[...] = jnp.zeros_like(carry_ref)

    oh1, oh2 = _onehots(rt_ref[...])
    both = oh1 + oh2
    carry = carry_ref[0:1, :]
    tile_cnt = jnp.sum(both, axis=0, keepdims=True)
    tile_cnt = tile_cnt + (tile_cnt - 2.0 * jnp.floor(tile_cnt * 0.5))
    tile_off = _lane_cumsum(tile_cnt) - tile_cnt
    where = _dot(tri_ref[...], both.astype(BF16)) + tile_off
    p1 = jnp.sum(oh1 * where, axis=-1, keepdims=True)
    p2 = jnp.sum(oh2 * where, axis=-1, keepdims=True)
    lane = lax.broadcasted_iota(jnp.int32, both.shape, 1)
    pos_ref[...] = jnp.where(lane == 0, p1, jnp.where(lane == 1, p2, 0.0))
    row = lax.broadcasted_iota(jnp.int32, meta_ref.shape, 0)
    meta_ref[...] = jnp.where(row == 0, tile_off, jnp.where(row == 1, tile_cnt, jnp.where(row == 2, carry, 0.0)))
    total = carry + tile_cnt
    carry_ref[...] = jnp.broadcast_to(total, carry_ref.shape)
    cnt_ref[...] = jnp.broadcast_to(total, cnt_ref.shape)


def _runs_kernel(meta_ref, cnt_ref, runs_ref, be_ref):
    lane = lax.broadcasted_iota(jnp.int32, (1, LANES), 1)
    counts = cnt_ref[0:1, :]
    padded = jnp.floor((counts + (RUN_ROWS - 1 + MOE_BLK - 1)) * (1.0 / MOE_BLK)) * MOE_BLK
    padded = jnp.where(lane < N_EXPERTS, padded, 0.0)
    ends = _lane_cumsum(padded)
    starts = ends - padded
    for t in range(NT):
        m = meta_ref[t]
        row = lax.broadcasted_iota(jnp.int32, m.shape, 0)
        runs_ref[t] = jnp.where(row == 2, m + starts, m).astype(jnp.int32)
    blk = lax.broadcasted_iota(jnp.int32, be_ref.shape, 0).astype(F32) * MOE_BLK
    lane_b = lax.broadcasted_iota(jnp.int32, be_ref.shape, 1)
    done = jnp.where((ends <= blk) & (lane_b < N_EXPERTS), 1.0, 0.0)
    be = jnp.minimum(jnp.sum(done, axis=-1, keepdims=True), N_EXPERTS - 1.0)
    nblk = jnp.max(jnp.where(lane_b == N_EXPERTS - 1, ends, 0.0), axis=-1, keepdims=True) * (1.0 / MOE_BLK)
    be_ref[...] = jnp.where(lane_b == 0, be, jnp.where(lane_b == 1, nblk, 0.0)).astype(jnp.int32)


def _plan_call(route, tri):
    tok = pl.BlockSpec((TM, LANES), lambda i: (i, 0))
    pos, meta, counts = pl.pallas_call(
        _rank_kernel,
        out_shape=[jax.ShapeDtypeStruct((N_TOK, LANES), F32), jax.ShapeDtypeStruct((NT, 8, LANES), F32),
                   jax.ShapeDtypeStruct((8, LANES), F32)],
        grid=(NT,),
        in_specs=[tok, _const_spec((TM, TM))],
        out_specs=[tok, pl.BlockSpec((None, 8, LANES), lambda i: (i, 0, 0)),
                   pl.BlockSpec((8, LANES), lambda i: (0, 0))],
        scratch_shapes=[pltpu.VMEM((8, LANES), F32)],
        compiler_params=_cparams(("arbitrary",)),
        name="moe_rank",
    )(route, tri)
    runs, blk = pl.pallas_call(
        _runs_kernel,
        out_shape=[jax.ShapeDtypeStruct((NT, 8, LANES), jnp.int32),
                   jax.ShapeDtypeStruct((256, LANES), jnp.int32)],
        name="moe_runs",
    )(meta, counts)
    runs_flat = runs[:, 0:3, 0:N_EXPERTS].reshape(NT, 1, 3 * N_EXPERTS)
    return pos, runs_flat, blk[:N_MOE_BLOCKS, 0], blk[0:1, 1]


def _pack_pairs(x):
    half = x.shape[1] // 2
    lo = pltpu.bitcast(x[:, :half], jnp.uint32)
    hi = pltpu.bitcast(x[:, half:], jnp.uint32)
    return (lo >> 16) | (hi & jnp.uint32(0xFFFF0000))


def _unpack_pairs(w):
    lo = pltpu.bitcast(w << 16, F32)
    hi = pltpu.bitcast(w & jnp.uint32(0xFFFF0000), F32)
    return jnp.concatenate([lo, hi], axis=1).astype(BF16)


def _run_fields(runs_ref, e):
    return runs_ref[0, e], runs_ref[0, N_EXPERTS + e], runs_ref[0, 2 * N_EXPERTS + e]


def _store_rows(lin_ref, packed):
    rows = packed.shape[0]
    for c in range(PK):
        lin_ref[pl.ds(c, rows, stride=PK), :] = packed[:, c * LANES:(c + 1) * LANES]


def _load_rows(lin_ref, rows):
    return jnp.concatenate([lin_ref[pl.ds(c, rows, stride=PK), :] for c in range(PK)], axis=1)


def _lin(ref, row, nrows):
    return ref.at[pl.ds(pl.multiple_of(row * PK, 8), nrows * PK), :]


def _dispatch_kernel(runs_ref, pos_ref, h2_ref, xs_in_ref, xs_ref, buf_ref, sem):
    del xs_in_ref

    @pl.when(pl.program_id(0) == 0)
    def _():
        buf_ref[TS * PK:, :] = jnp.zeros((RUN_ROWS * PK, LANES), jnp.uint32)

    pos = pos_ref[...]
    col = lax.broadcasted_iota(jnp.int32, (TM, TS), 1).astype(F32)
    sel = jnp.where((col == pos[:, 0:1]) | (col == pos[:, 1:2]), 1.0, 0.0).astype(BF16)
    srt = lax.dot_general(sel, h2_ref[...], (((0,), (0,)), ((), ())), preferred_element_type=F32)
    _store_rows(buf_ref, _pack_pairs(srt))

    def chunk_copy(off, dst, k):
        return pltpu.make_async_copy(_lin(buf_ref, off + k * RUN_ROWS, RUN_ROWS),
                                     _lin(xs_ref, dst + k * RUN_ROWS, RUN_ROWS), sem)

    def start(e, carry):
        off, n, dst = _run_fields(runs_ref, e)
        lax.fori_loop(0, (n + RUN_ROWS - 1) // RUN_ROWS, lambda k, c: (chunk_copy(off, dst, k).start(), c)[1], 0)
        return carry

    def wait(e, carry):
        off, n, dst = _run_fields(runs_ref, e)
        lax.fori_loop(0, (n + RUN_ROWS - 1) // RUN_ROWS, lambda k, c: (chunk_copy(off, dst, k).wait(), c)[1], 0)
        return carry

    lax.fori_loop(0, N_EXPERTS, start, 0)
    lax.fori_loop(0, N_EXPERTS, wait, 0)


def _dispatch_call(runs_flat, pos, h2, zeros_slots):
    return pl.pallas_call(
        _dispatch_kernel,
        out_shape=jax.ShapeDtypeStruct((N_SLOTS * PK, LANES), jnp.uint32),
        grid=(NT,),
        in_specs=[pl.BlockSpec((None, 1, 3 * N_EXPERTS), lambda i: (i, 0, 0), memory_space=pltpu.SMEM),
                  pl.BlockSpec((TM, LANES), lambda i: (i, 0)),
                  pl.BlockSpec((TM, D), lambda i: (i, 0)),
                  pl.BlockSpec(memory_space=pl.ANY)],
        out_specs=pl.BlockSpec(memory_space=pl.ANY),
        scratch_shapes=[pltpu.VMEM(((TS + RUN_ROWS) * PK, LANES), jnp.uint32), pltpu.SemaphoreType.DMA],
        input_output_aliases={3: 0},
        compiler_params=_cparams(("arbitrary",)),
        name="moe_dispatch",
    )(runs_flat, pos, h2, zeros_slots)


def _expert_kernel(be_ref, nu_ref, xs_ref, wg_ref, wu_ref, wd_ref, ys_ref, wgb_ref, wub_ref, wdb_ref):
    b = pl.program_id(0)

    @pl.when(jnp.logical_or(b == 0, be_ref[b] != be_ref[jnp.maximum(b - 1, 0)]))
    def _():
        wgb_ref[...] = wg_ref[...].astype(BF16)
        wub_ref[...] = wu_ref[...].astype(BF16)
        wdb_ref[...] = wd_ref[...].astype(BF16)

    @pl.when(b < nu_ref[0])
    def _():
        xb = _unpack_pairs(_load_rows(xs_ref, MOE_BLK))
        g = _dot(xb, wgb_ref[...])
        u = _dot(xb, wub_ref[...])
        hmid = (g * _sigmoid(g)) * u
        y = _dot(hmid.astype(BF16), wdb_ref[...])
        _store_rows(ys_ref, _pack_pairs(y.astype(BF16).astype(F32)))

    @pl.when(b >= nu_ref[0])
    def _():
        ys_ref[...] = jnp.zeros_like(ys_ref)


def _expert_call(blk_e, n_used, xs, wg, wu, wd, l):
    wspec = lambda k, n: pl.BlockSpec((None, None, k, n), lambda b, be, nu: (l, be[b], 0, 0))
    return pl.pallas_call(
        _expert_kernel,
        out_shape=jax.ShapeDtypeStruct((N_SLOTS * PK, LANES), jnp.uint32),
        grid_spec=pltpu.PrefetchScalarGridSpec(
            num_scalar_prefetch=2,
            grid=(N_MOE_BLOCKS,),
            in_specs=[pl.BlockSpec((MOE_BLK * PK, LANES), lambda b, be, nu: (jnp.minimum(b, nu[0] - 1), 0)),
                      wspec(D, EXPERT_HIDDEN), wspec(D, EXPERT_HIDDEN), wspec(EXPERT_HIDDEN, D)],
            out_specs=pl.BlockSpec((MOE_BLK * PK, LANES), lambda b, be, nu: (b, 0)),
            scratch_shapes=[pltpu.VMEM((D, EXPERT_HIDDEN), BF16), pltpu.VMEM((D, EXPERT_HIDDEN), BF16),
                            pltpu.VMEM((EXPERT_HIDDEN, D), BF16)]),
        compiler_params=_cparams(("arbitrary",)),
        name="moe_experts",
    )(blk_e, n_used, xs, wg, wu, wd)


RUN_PIECES = (32, 16, 8, 4, 2)


def _combine_kernel(runs_ref, ys_ref, pos_ref, x_ref, rt_ref, mod_ref, o_ref, buf_ref, sem):
    @pl.when(pl.program_id(0) == 0)
    def _():
        buf_ref[...] = jnp.zeros_like(buf_ref)

    def piece(off, dst, size):
        return pltpu.make_async_copy(_lin(ys_ref, dst, size), _lin(buf_ref, off, size), sem)

    def run_copies(e, act):
        off, n, dst = _run_fields(runs_ref, e)
        whole = n // RUN_ROWS

        def chunk(k, c):
            act(piece(off + k * RUN_ROWS, dst + k * RUN_ROWS, RUN_ROWS))
            return c

        lax.fori_loop(0, whole, chunk, 0)
        done = whole * RUN_ROWS
        for size in RUN_PIECES:
            @pl.when((n & size) != 0)
            def _(done=done, size=size):
                act(piece(off + done, dst + done, size))
            done = done + (n & size)

    lax.fori_loop(0, N_EXPERTS, lambda e, c: (run_copies(e, lambda d: d.start()), c)[1], 0)
    lax.fori_loop(0, N_EXPERTS, lambda e, c: (run_copies(e, lambda d: d.wait()), c)[1], 0)

    ysb = _unpack_pairs(_load_rows(buf_ref, TS))
    pos = pos_ref[...]
    rt = rt_ref[...]
    col = lax.broadcasted_iota(jnp.int32, (TM, TS), 1).astype(F32)
    pick = (jnp.where(col == pos[:, 0:1], rt[:, TOP_K:TOP_K + 1], 0.0)
            + jnp.where(col == pos[:, 1:2], rt[:, TOP_K + 1:TOP_K + 2], 0.0)).astype(BF16)
    y = _dot(pick, ysb)
    o_ref[...] = x_ref[...] + mod_ref[5:6, :] * y


def _combine_call(runs_flat, ys, pos, x, route, mods_l, n_tiles):
    tok = lambda w: pl.BlockSpec((TM, w), lambda i: (i, 0))
    return pl.pallas_call(
        _combine_kernel,
        out_shape=jax.ShapeDtypeStruct((n_tiles * TM, D), F32),
        grid=(n_tiles,),
        in_specs=[pl.BlockSpec((None, 1, 3 * N_EXPERTS), lambda i: (i, 0, 0), memory_space=pltpu.SMEM),
                  pl.BlockSpec(memory_space=pl.ANY),
                  tok(LANES), tok(D), tok(LANES),
                  pl.BlockSpec((None, 6, D), lambda i: (_mod_row(i), 0, 0))],
        out_specs=tok(D),
        scratch_shapes=[pltpu.VMEM((TS * PK, LANES), jnp.uint32), pltpu.SemaphoreType.DMA],
        compiler_params=_cparams(("arbitrary",)),
        name="moe_combine",
    )(runs_flat, ys, pos, x, route, mods_l)


def _rope_tables():
    nf = HEAD_DIM // 4
    inv = ROPE_BASE ** (-jnp.arange(nf, dtype=F32) / nf)
    t = jnp.arange(S)
    row = (t // GRID_W).astype(F32)[:, None] * inv[None, :]
    col = (t % GRID_W).astype(F32)[:, None] * inv[None, :]
    zero = jnp.zeros_like(row)
    cos = jnp.concatenate([jnp.cos(row), jnp.cos(row), jnp.cos(col), jnp.cos(col)], axis=1)
    sa = jnp.concatenate([-jnp.sin(row), zero, -jnp.sin(col), zero], axis=1)
    sb = jnp.concatenate([zero, jnp.sin(row), zero, jnp.sin(col)], axis=1)
    ident = (jnp.ones((TM, HEAD_DIM), F32), jnp.zeros((TM, HEAD_DIM), F32), jnp.zeros((TM, HEAD_DIM), F32))
    return tuple(jnp.tile(jnp.concatenate([a, b], axis=0), (1, LANES // HEAD_DIM))
                 for a, b in zip((cos, sa, sb), ident))


def _fourier_tables():
    s1 = np.arange(FS1)
    ang1 = 2.0 * np.pi * np.outer(s1, s1) / FS1
    w1 = np.concatenate([np.cos(ang1), -np.sin(ang1)], axis=0) / np.sqrt(S)
    k1 = np.arange(FS1)[:, None, None]
    k2 = np.arange(FS2)[None, :, None]
    s2 = np.arange(FS2)[None, None, :]
    ang2 = 2.0 * np.pi * ((k1 + FS1 * k2) * s2 % S) / S
    c2, sn2 = np.cos(ang2), np.sin(ang2)
    ta = np.concatenate([c2, -sn2], axis=1)
    tb = np.concatenate([sn2, c2], axis=1)
    sc = np.arange(C)
    angc = 2.0 * np.pi * np.outer(sc, sc) / C
    wc = np.concatenate([np.cos(angc), -np.sin(angc)], axis=0) / np.sqrt(C)
    return tuple(jnp.asarray(a, F32).astype(BF16) for a in (w1, ta, tb, wc))


def _channel_dft():
    cidx = np.arange(FOURIER_GROUP_W)
    ang = 2.0 * np.pi * np.outer(cidx, cidx) / FOURIER_GROUP_W
    eye = np.eye(FOURIER_W // FOURIER_GROUP_W)
    cw = np.kron(eye, np.cos(ang)) / np.sqrt(FOURIER_GROUP_W)
    sw = np.kron(eye, np.sin(ang)) / np.sqrt(FOURIER_GROUP_W)
    return jnp.asarray(np.concatenate([cw, sw], axis=0), F32)


def _pool_bands():
    t = np.arange(TP)[:, None]
    main, halo = [], []
    for w in POOL_WINDOWS:
        def hit(j):
            return ((j - t >= -(w // 2)) & (j - t <= w // 2 - 1)).astype(np.float32)
        main.append(hit(np.arange(TP)[None, :]))
        halo.append(np.concatenate([hit(np.arange(-HALO, 0)[None, :]),
                                    hit(np.arange(TP, TP + HALO)[None, :])], axis=1))
    return (jnp.asarray(np.stack(main), F32).astype(BF16), jnp.asarray(np.stack(halo), F32).astype(BF16))


def _conv_shifts():
    i = np.arange(CONV_WIN)
    return jnp.asarray(np.stack([(i[None, :] == i[:, None] + s) for s in range(1, 8)]), F32).astype(BF16)


def _fold_kernel(a_ref, b_ref, o_ref):
    o_ref[...] = jnp.dot(a_ref[...], b_ref[...], preferred_element_type=F32,
                         precision=lax.Precision.HIGHEST).astype(BF16)


def _fold_fourier_weights(dftw, w_br_fourier):
    nl = w_br_fourier.shape[0]
    return pl.pallas_call(
        _fold_kernel,
        out_shape=jax.ShapeDtypeStruct((nl, 2 * FOURIER_W, D), BF16),
        grid=(nl,),
        in_specs=[pl.BlockSpec((2 * FOURIER_W, FOURIER_W), lambda l: (0, 0)),
                  pl.BlockSpec((None, FOURIER_W, D), lambda l: (l, 0, 0))],
        out_specs=pl.BlockSpec((None, 2 * FOURIER_W, D), lambda l: (l, 0, 0)),
        compiler_params=_cparams(("arbitrary",)),
        name="fold_fourier_proj",
    )(dftw, w_br_fourier)


def _block_diag(blocks):
    n, r, c = blocks.shape
    out = jnp.zeros((n * r, n * c), blocks.dtype)
    for i in range(n):
        out = lax.dynamic_update_slice(out, blocks[i], (i * r, i * c))
    return out


def kernel(x, c, ctx, c_ctx, w_ada, b_ada, g_norm_mix, g_norm_ffn, w_in, g_q, g_k, sink, w_br_attn,
           w_br_fourier, pool_w, pool_scale, w_br_pool, conv_w, conv_b, cn_g, cn_b, w_br_conv, w_gate,
           b_gate, w_out, w_router_grp, b_router_grp, w_router_exp, b_router_exp, w_e_gate, w_e_up,
           w_e_down):
    xs = jnp.concatenate([x.reshape(N_LAT, D), ctx.reshape(N_CTX, D)], axis=0)
    nl = w_ada.shape[0]
    mods = _ada_all(c, c_ctx, w_ada, b_ada).reshape(nl, 8, 6, D)
    rope_tabs = _rope_tables()
    four_tabs = _fourier_tables()
    band_main, band_halo = _pool_bands()
    shifts = _conv_shifts()
    wf_all = _fold_fourier_weights(_channel_dft(), w_br_fourier)
    bd = jnp.asarray(np.kron(np.eye(LANES // HEAD_DIM), np.ones((HEAD_DIM, HEAD_DIM))), F32).astype(BF16)
    tri = jnp.asarray(np.tril(np.ones((TM, TM)), -1), F32).astype(BF16)
    zeros_slots = jnp.zeros((N_SLOTS * PK, LANES), jnp.uint32)
    rpad = jnp.zeros((nl, D, LANES - N_GROUPS - N_EXPERTS), F32)
    w_router = jnp.concatenate([w_router_grp, w_router_exp, rpad], axis=-1)
    r_hi = w_router.astype(BF16)
    r_lo = (w_router - r_hi.astype(F32)).astype(BF16)
    r_b = jnp.concatenate([b_router_grp, b_router_exp, rpad[:, 0, :]], axis=-1).reshape(nl, 1, LANES)
    stacked = tuple(w.astype(BF16) for w in (w_br_attn,)) + (wf_all,) + tuple(
        w.astype(BF16) for w in (w_br_pool, w_br_conv, w_gate, w_out)) + (r_hi, r_lo)

    for l in range(nl):
        mods_l = mods[l]
        gn = g_norm_mix[l].reshape(1, D)
        q, kv, f, p, u = _proj_call(xs, mods_l, gn, w_in, l, rope_tabs,
                                    jnp.tile(g_q[l], 2).reshape(1, LANES),
                                    jnp.tile(g_k[l], 2).reshape(1, LANES), bd)
        a_lat, a_ctx = _attn_call(sink[l], q, kv)
        h_lat, h_ctx = _fourier_call(f, four_tabs)
        zc, cact = _poolconv_call(p, u, band_main, band_halo, _block_diag(pool_w[l]).astype(BF16),
                                  pool_scale[l].reshape(1, POOL_W), shifts, conv_w[l], conv_b[l].reshape(1, CONV_W),
                                  cn_g[l].reshape(1, CONV_W), cn_b[l].reshape(1, CONV_W))
        small = (b_gate[l].reshape(1, 4 * D), g_norm_ffn[l].reshape(1, D), r_b[l])
        xs, h2, route = _mix_call(xs, mods_l, gn, a_lat, a_ctx, h_lat, h_ctx, zc, cact, l, stacked, small)
        pos, runs_flat, blk_e, n_used = _plan_call(route, tri)
        slots = _dispatch_call(runs_flat, pos, h2, zeros_slots)
        ys = _expert_call(blk_e, n_used, slots, w_e_gate, w_e_up, w_e_down, l)
        xs = _combine_call(runs_flat, ys, pos, xs, route, mods_l, LAT_TILES if l == nl - 1 else NT)
    return xs.reshape(B, S, D)
```

```python
import functools

import numpy as np
import jax
import jax.numpy as jnp
from jax import lax
from jax.experimental import pallas as pl
from jax.experimental.pallas import tpu as pltpu

F32 = jnp.float32
BF16 = jnp.bfloat16

D = 1024
B = 2
S = 8192
C = 256
GRID_W = 64
HEAD_DIM = 64
N_Q_HEADS = 8
N_KV_HEADS = 2
GQA = N_Q_HEADS // N_KV_HEADS
WINDOW = 128
ATTN_BLK = 128
ROPE_BASE = 10000.0
Q_W = 512
KV_W = 128
FOURIER_W = 640
FOURIER_GROUP_W = 160
POOL_W = 640
POOL_GROUP_W = 160
POOL_WINDOWS = (2, 4, 8, 16)
CONV_W = 512
CONV_K = 31
PROJ_W = 3072
N_GROUPS = 4
EPG = 8
N_EXPERTS = 32
TOP_K = 2
EXPERT_HIDDEN = 512
MOE_BLK = 512
EPS = 1e-6
NEG_INF = -1e30
LOG2E = 1.4426950408889634

N_LAT = B * S
N_CTX = B * C
N_TOK = N_LAT + N_CTX
TM = 512
NT = N_TOK // TM
LAT_TILES = N_LAT // TM
TILES_PER_BATCH = S // TM
TP = 256
NTP = N_TOK // TP
HALO = 16
CONV_WIN = TP // 2 + 2 * HALO
N_ASSIGN = N_TOK * TOP_K
RUN_ROWS = 64
PK = D // 2 // 128
TS = 1152
N_MOE_BLOCKS = (N_ASSIGN + NT * N_EXPERTS + N_EXPERTS * (RUN_ROWS - 1 + MOE_BLK - 1)) // MOE_BLK
N_SLOTS = N_MOE_BLOCKS * MOE_BLK
FS1 = 64
FS2 = 128
F1_COLS = FS2 * FOURIER_W
F1_CW = 8192
F2_K1 = 8
LANES = 128
VMEM_LIMIT = 56 * 1024 * 1024


def _cparams(sem, vmem=VMEM_LIMIT):
    return pltpu.CompilerParams(dimension_semantics=sem, vmem_limit_bytes=vmem)


def _const_spec(shape):
    nd = len(shape)
    return pl.BlockSpec(shape, lambda *_: (0,) * nd, pipeline_mode=pl.Buffered(1))


def _dot(a, b):
    return jnp.dot(a, b, preferred_element_type=F32)


def _modulate(x, g, shift, scale):
    y = x * lax.rsqrt(jnp.mean(x * x, axis=-1, keepdims=True) + EPS)
    return (y * g) * (1.0 + scale) + shift


def _sigmoid(x):
    return 1.0 / (1.0 + jnp.exp(-x))


def _ada_kernel(ct_ref, w_ref, b_ref, o_ref):
    ct = ct_ref[...]
    s = ct * _sigmoid(ct)
    w = w_ref[...]
    rows = [jnp.sum(w * s[:, r:r + 1], axis=0, keepdims=True) for r in range(3)]
    rows.append(jnp.zeros((5, w.shape[1]), F32))
    o_ref[...] = jnp.concatenate(rows, axis=0) + b_ref[...]


def _ada_all(c, c_ctx, w_ada, b_ada):
    ct = jnp.concatenate([c, c_ctx[None, :], jnp.zeros((5, D), F32)], axis=0).T
    cols = 1536
    nl = w_ada.shape[0]
    return pl.pallas_call(
        _ada_kernel,
        out_shape=jax.ShapeDtypeStruct((nl, 8, 6 * D), F32),
        grid=(nl, 6 * D // cols),
        in_specs=[pl.BlockSpec((D, 8), lambda l, j: (0, 0)),
                  pl.BlockSpec((None, D, cols), lambda l, j: (l, 0, j)),
                  pl.BlockSpec((None, 1, cols), lambda l, j: (l, 0, j))],
        out_specs=pl.BlockSpec((None, 8, cols), lambda l, j: (l, 0, j)),
        compiler_params=_cparams(("arbitrary", "arbitrary")),
        name="adaln",
    )(ct, w_ada, b_ada.reshape(nl, 1, 6 * D))


def _head_rms(t, g128, bd):
    outs = []
    for j in range(t.shape[1] // LANES):
        blk = t[:, j * LANES:(j + 1) * LANES]
        ss = _dot((blk * blk).astype(BF16), bd)
        outs.append(blk * lax.rsqrt(ss * (1.0 / HEAD_DIM) + EPS) * g128)
    return outs


def _rope(blocks, cos, sa, sb):
    outs = []
    for blk in blocks:
        up = pltpu.roll(blk, LANES - 16, 1)
        dn = pltpu.roll(blk, 16, 1)
        outs.append(blk * cos + up * sa + dn * sb)
    return outs


def _proj_kernel(x_ref, mod_ref, gn_ref, w_ref, cos_ref, sa_ref, sb_ref, gq_ref, gk_ref, bd_ref,
                 q_ref, kv_ref, f_ref, p_ref, u_ref, wbf_ref):
    @pl.when(pl.program_id(0) == 0)
    def _():
        wbf_ref[...] = w_ref[...].astype(BF16)

    m = mod_ref[...]
    h = _modulate(x_ref[...], gn_ref[...], m[0:1], m[1:2])
    proj = _dot(h.astype(BF16), wbf_ref[...])
    cos, sa, sb, bd = cos_ref[...], sa_ref[...], sb_ref[...], bd_ref[...]
    q = _rope(_head_rms(proj[:, 0:Q_W], gq_ref[...], bd), cos, sa, sb)
    q_ref[...] = (jnp.concatenate(q, axis=1) * (LOG2E * HEAD_DIM ** -0.5)).astype(BF16)
    k = _rope(_head_rms(proj[:, Q_W:Q_W + KV_W], gk_ref[...], bd), cos, sa, sb)
    kv_ref[:, 0:KV_W] = k[0].astype(BF16)
    o = Q_W + KV_W
    kv_ref[:, KV_W:2 * KV_W] = proj[:, o:o + KV_W].astype(BF16)
    o += KV_W
    f_ref[...] = proj[:, o:o + FOURIER_W].astype(BF16)
    o += FOURIER_W
    p_ref[...] = proj[:, o:o + POOL_W].astype(BF16)
    o += POOL_W
    a = proj[:, o:o + CONV_W]
    g = proj[:, o + CONV_W:o + 2 * CONV_W]
    u_ref[...] = (a * _sigmoid(g)).astype(BF16)


def _mod_row(i):
    return jnp.minimum(i // TILES_PER_BATCH, 2)


def _layer_spec(shape, l):
    nd = len(shape)
    return pl.BlockSpec((None,) + tuple(shape), lambda *_: (l,) + (0,) * nd, pipeline_mode=pl.Buffered(1))


def _proj_call(x, mods_l, gn, w_in, l, rope_tabs, gq128, gk128, bd):
    cos, sa, sb = rope_tabs
    tok = lambda w: pl.BlockSpec((TM, w), lambda i: (i, 0))
    rope_spec = pl.BlockSpec((TM, LANES), lambda i: (jnp.where(i < LAT_TILES, i % TILES_PER_BATCH,
                                                               TILES_PER_BATCH), 0))
    widths = (Q_W, 2 * KV_W, FOURIER_W, POOL_W, CONV_W)
    return pl.pallas_call(
        _proj_kernel,
        out_shape=[jax.ShapeDtypeStruct((N_TOK, w), BF16) for w in widths],
        grid=(NT,),
        in_specs=[tok(D),
                  pl.BlockSpec((None, 6, D), lambda i: (_mod_row(i), 0, 0)),
                  _const_spec((1, D)),
                  _layer_spec((D, PROJ_W), l),
                  rope_spec, rope_spec, rope_spec,
                  _const_spec((1, LANES)), _const_spec((1, LANES)),
                  _const_spec((LANES, LANES))],
        out_specs=[tok(w) for w in widths],
        scratch_shapes=[pltpu.VMEM((D, PROJ_W), BF16)],
        compiler_params=_cparams(("arbitrary",)),
        name="proj",
    )(x, mods_l, gn, w_in, cos, sa, sb, gq128, gk128, bd)


def _attend(q, kv_blocks, biases, sink_ref):
    lane = lax.broadcasted_iota(jnp.int32, (ATTN_BLK, LANES), 1)
    heads = []
    for j in range(N_KV_HEADS):
        ks = slice(j * HEAD_DIM, (j + 1) * HEAD_DIM)
        vs = slice(KV_W + j * HEAD_DIM, KV_W + (j + 1) * HEAD_DIM)
        kj = jnp.concatenate([blk[:, ks] for blk in kv_blocks], axis=0)
        vj = jnp.concatenate([blk[:, vs] for blk in kv_blocks], axis=0)
        vaug = jnp.concatenate([vj, jnp.ones_like(vj)], axis=1)
        qs = jnp.concatenate([q[:, (j * GQA + g) * HEAD_DIM:(j * GQA + g + 1) * HEAD_DIM]
                              for g in range(GQA)], axis=0)
        s = lax.dot_general(qs, kj, (((1,), (1,)), ((), ())), preferred_element_type=F32)
        probs, sink_terms = [], []
        for g in range(GQA):
            sg = s[g * ATTN_BLK:(g + 1) * ATTN_BLK]
            pieces, col = [], 0
            for blk, bias in zip(kv_blocks, biases):
                piece = sg[:, col:col + blk.shape[0]]
                pieces.append(piece if bias is None else piece + bias)
                col += blk.shape[0]
            sg = jnp.concatenate(pieces, axis=1)
            sk = sink_ref[j * GQA + g] * LOG2E
            mx = jnp.maximum(jnp.max(sg, axis=-1, keepdims=True), sk)
            probs.append(jnp.exp2(sg - mx).astype(BF16))
            sink_terms.append(jnp.exp2(sk - mx))
        o = _dot(jnp.concatenate(probs, axis=0), vaug)
        for g in range(GQA):
            og = o[g * ATTN_BLK:(g + 1) * ATTN_BLK]
            heads.append(og / (og[:, HEAD_DIM:HEAD_DIM + 1] + sink_terms[g]))
    tiles = [jnp.where(lane < HEAD_DIM, heads[2 * t], pltpu.roll(heads[2 * t + 1], HEAD_DIM, 1))
             for t in range(N_Q_HEADS // 2)]
    return jnp.concatenate(tiles, axis=1).astype(BF16)


def _attn_latent_kernel(sink_ref, q_ref, prev_ref, cur_ref, next_ref, ctx_ref, o_ref):
    n = pl.program_id(1)
    r = lax.broadcasted_iota(jnp.int32, (ATTN_BLK, ATTN_BLK), 0)
    jj = lax.broadcasted_iota(jnp.int32, (ATTN_BLK, ATTN_BLK), 1)
    far = jnp.int32(2 * ATTN_BLK)
    off_prev = jnp.where(n > 0, 0, far)
    off_next = jnp.where(n < S // (2 * ATTN_BLK) - 1, 0, far)
    prev_ok = jnp.where(jj - r >= 0, 0.0, NEG_INF)
    next_ok = jnp.where(r - jj >= 0, 0.0, NEG_INF)
    prev_edge = jnp.where(jj - r - off_prev >= 0, 0.0, NEG_INF)
    next_edge = jnp.where(r - jj - off_next >= 0, 0.0, NEG_INF)
    ctx = ctx_ref[...]
    cur0, cur1 = cur_ref[0:ATTN_BLK, :], cur_ref[ATTN_BLK:, :]
    o_ref[0:ATTN_BLK, :] = _attend(q_ref[0:ATTN_BLK, :], [ctx, prev_ref[...], cur0, cur1],
                                   [None, prev_edge, None, next_ok], sink_ref)
    o_ref[ATTN_BLK:, :] = _attend(q_ref[ATTN_BLK:, :], [ctx, cur0, cur1, next_ref[...]],
                                  [None, prev_ok, None, next_edge], sink_ref)


def _attn_context_kernel(sink_ref, q_ref, ctx_ref, o_ref):
    o_ref[...] = _attend(q_ref[...], [ctx_ref[...]], [None], sink_ref)


def _attn_call(sink_l, q, kv):
    nb = S // ATTN_BLK
    nq = nb // 2
    smem = pl.BlockSpec(memory_space=pltpu.SMEM)
    pair = lambda w: pl.BlockSpec((2 * ATTN_BLK, w), lambda b, n: (b * nq + n, 0))
    prev = pl.BlockSpec((ATTN_BLK, 2 * KV_W), lambda b, n: (b * nb + jnp.maximum(2 * n - 1, 0), 0))
    nxt = pl.BlockSpec((ATTN_BLK, 2 * KV_W), lambda b, n: (b * nb + jnp.minimum(2 * n + 2, nb - 1), 0))
    ctxs = pl.BlockSpec((C, 2 * KV_W), lambda b, n: (N_LAT // C + b, 0))
    lat = pl.pallas_call(
        _attn_latent_kernel,
        out_shape=jax.ShapeDtypeStruct((N_LAT, Q_W), BF16),
        grid=(B, nq),
        in_specs=[smem, pair(Q_W), prev, pair(2 * KV_W), nxt, ctxs],
        out_specs=pair(Q_W),
        compiler_params=_cparams(("parallel", "parallel")),
        name="attn_latent",
    )(sink_l, q, kv, kv, kv, kv)
    ncb = C // ATTN_BLK
    base = N_LAT // ATTN_BLK
    ctx = pl.pallas_call(
        _attn_context_kernel,
        out_shape=jax.ShapeDtypeStruct((N_CTX, Q_W), BF16),
        grid=(B, ncb),
        in_specs=[smem, pl.BlockSpec((ATTN_BLK, Q_W), lambda b, n: (base + b * ncb + n, 0)), ctxs],
        out_specs=pl.BlockSpec((ATTN_BLK, Q_W), lambda b, n: (b * ncb + n, 0)),
        compiler_params=_cparams(("parallel", "parallel")),
        name="attn_context",
    )(sink_l, q, kv)
    return lat, ctx


def _f1_kernel(w_ref, f_ref, re_ref, im_ref):
    res = _dot(w_ref[...], f_ref[...])
    re_ref[...] = res[:FS1].astype(BF16)
    im_ref[...] = res[FS1:].astype(BF16)


def _f2_kernel(ta_ref, tb_ref, re_ref, im_ref, o_ref):
    for i in range(F2_K1):
        res = _dot(ta_ref[i], re_ref[i]) + _dot(tb_ref[i], im_ref[i])
        o_ref[i, :, 0:FOURIER_W] = res[:FS2].astype(BF16)
        o_ref[i, :, FOURIER_W:2 * FOURIER_W] = res[FS2:].astype(BF16)


def _fc_kernel(w_ref, f_ref, o_ref):
    res = _dot(w_ref[...], f_ref[...])
    o_ref[:, 0:FOURIER_W] = res[:C].astype(BF16)
    o_ref[:, FOURIER_W:2 * FOURIER_W] = res[C:].astype(BF16)


def _fourier_call(f, tabs):
    w1, ta, tb, wc = tabs
    f2d = f.reshape(N_TOK // FS2, F1_COLS)
    nchunk = F1_COLS // F1_CW
    a_re, a_im = pl.pallas_call(
        _f1_kernel,
        out_shape=[jax.ShapeDtypeStruct((B * FS1, F1_COLS), BF16)] * 2,
        grid=(B, nchunk),
        in_specs=[_const_spec((2 * FS1, FS1)),
                  pl.BlockSpec((FS1, F1_CW), lambda b, j: (b, j))],
        out_specs=[pl.BlockSpec((FS1, F1_CW), lambda b, j: (b, j))] * 2,
        compiler_params=_cparams(("parallel", "parallel")),
        name="fourier_stage1",
    )(w1, f2d)
    a_re = a_re.reshape(B * FS1, FS2, FOURIER_W)
    a_im = a_im.reshape(B * FS1, FS2, FOURIER_W)
    nk = FS1 // F2_K1
    aspec = pl.BlockSpec((F2_K1, FS2, FOURIER_W), lambda b, k1: (b * nk + k1, 0, 0))
    tspec = pl.BlockSpec((F2_K1, 2 * FS2, FS2), lambda b, k1: (k1, 0, 0))
    h_t = pl.pallas_call(
        _f2_kernel,
        out_shape=jax.ShapeDtypeStruct((B, FS1, FS2, 2 * FOURIER_W), BF16),
        grid=(B, nk),
        in_specs=[tspec, tspec, aspec, aspec],
        out_specs=pl.BlockSpec((None, F2_K1, FS2, 2 * FOURIER_W), lambda b, k1: (b, k1, 0, 0)),
        compiler_params=_cparams(("parallel", "parallel")),
        name="fourier_stage2",
    )(ta, tb, a_re, a_im)
    h_lat = jnp.transpose(h_t, (0, 2, 1, 3)).reshape(N_LAT, 2 * FOURIER_W)
    h_ctx = pl.pallas_call(
        _fc_kernel,
        out_shape=jax.ShapeDtypeStruct((N_CTX, 2 * FOURIER_W), BF16),
        grid=(B,),
        in_specs=[_const_spec((2 * C, C)),
                  pl.BlockSpec((C, FOURIER_W), lambda b: (N_LAT // C + b, 0))],
        out_specs=pl.BlockSpec((C, 2 * FOURIER_W), lambda b: (b, 0)),
        compiler_params=_cparams(("parallel",)),
        name="fourier_context",
    )(wc, f)
    return h_lat, h_ctx


def _poolconv_kernel(pc_ref, pp_ref, pn_ref, uc_ref, up_ref, un_ref, bm_ref, bh_ref, ic_ref, pw_ref, ps_ref,
                     sh_ref, cw_ref, cb_ref, cg_ref, cnb_ref, z_ref, a_ref, win0_ref, win1_ref, cv_ref):
    t = pl.program_id(0)
    lat_tiles = N_LAT // TP
    per_seq = S // TP
    is_ctx = t >= lat_tiles
    first = jnp.logical_or(t % per_seq == 0, is_ctx)
    last = jnp.logical_or(t % per_seq == per_seq - 1, is_ctx)

    pcur = pc_ref[...]
    keep_prev = jnp.where(first, 0.0, 1.0)
    keep_next = jnp.where(last, 0.0, 1.0)
    halo = jnp.concatenate([pp_ref[...].astype(F32) * keep_prev,
                            pn_ref[...].astype(F32) * keep_next], axis=0).astype(BF16)
    sums = []
    for gi in range(len(POOL_WINDOWS)):
        cs = slice(gi * LANES, (gi + 2) * LANES)
        sums.append(_dot(bm_ref[gi], pcur[:, cs]) + _dot(bh_ref[gi], halo[:, cs]))
    lane_t = lax.broadcasted_iota(jnp.int32, (TP, LANES), 1)
    tiles = [sums[0][:, :LANES]]
    for gi in range(1, len(POOL_WINDOWS)):
        split = gi * POOL_GROUP_W - gi * LANES
        tiles.append(jnp.where(lane_t < split, sums[gi - 1][:, LANES:], sums[gi][:, :LANES]))
    tiles.append(sums[-1][:, LANES:])
    zsum = jnp.concatenate(tiles, axis=1)
    z = zsum * ic_ref[...] - pcur.astype(F32)
    z_ref[...] = (_dot(z.astype(BF16), pw_ref[...]) * ps_ref[...]).astype(BF16)

    ub = jnp.concatenate([(up_ref[...].astype(F32) * keep_prev).astype(BF16), uc_ref[...],
                          (un_ref[...].astype(F32) * keep_next).astype(BF16)], axis=0)
    off = HALO - CONV_K // 2
    half_rows = TP // 2
    for hf, win_ref in enumerate((win0_ref, win1_ref)):
        base = hf * half_rows
        window = ub[base:base + CONV_WIN]
        win_ref[0] = window.astype(F32)
        for s in range(1, 8):
            win_ref[s] = _dot(sh_ref[s - 1], window)
        for cb in range(CONV_W // LANES):
            cs = slice(cb * LANES, (cb + 1) * LANES)
            acc = jnp.zeros((half_rows, LANES), F32) + cb_ref[:, cs]
            for j in range(CONV_K):
                s, m = (off + j) % 8, (off + j) // 8
                acc = acc + win_ref[s, 8 * m:8 * m + half_rows, cs] * cw_ref[j:j + 1, cs]
            cv_ref[base:base + half_rows, cs] = acc
    cv = cv_ref[...]
    mu = jnp.mean(cv, axis=-1, keepdims=True)
    var = jnp.mean(jnp.square(cv - mu), axis=-1, keepdims=True)
    un = (cv - mu) * lax.rsqrt(var + EPS) * cg_ref[...] + cnb_ref[...]
    a_ref[...] = (un * _sigmoid(un)).astype(BF16)


def _poolconv_call(p, u, band_main, band_halo, inv_cnt, pw_bd, pool_scale, shifts, conv_w, conv_b, cn_g, cn_b):
    nh = TP // HALO
    last_h = N_TOK // HALO - 1
    cur = lambda w: pl.BlockSpec((TP, w), lambda t: (t, 0))
    prv = lambda w: pl.BlockSpec((HALO, w), lambda t: (jnp.maximum(t * nh - 1, 0), 0))
    nxt = lambda w: pl.BlockSpec((HALO, w), lambda t: (jnp.minimum((t + 1) * nh, last_h), 0))
    per_seq = S // TP

    def kind(t):
        return jnp.where(t >= N_LAT // TP, 3, jnp.where(t % per_seq == 0, 1, jnp.where(t % per_seq == per_seq - 1, 2, 0)))

    return pl.pallas_call(
        _poolconv_kernel,
        out_shape=[jax.ShapeDtypeStruct((N_TOK, POOL_W), BF16),
                   jax.ShapeDtypeStruct((N_TOK, CONV_W), BF16)],
        grid=(NTP,),
        in_specs=[cur(POOL_W), prv(POOL_W), nxt(POOL_W), cur(CONV_W), prv(CONV_W), nxt(CONV_W),
                  _const_spec((4, TP, TP)), _const_spec((4, TP, 2 * HALO)),
                  pl.BlockSpec((None, TP, POOL_W), lambda t: (kind(t), 0, 0)),
                  _const_spec((POOL_W, POOL_W)), _const_spec((1, POOL_W)),
                  _const_spec((7, CONV_WIN, CONV_WIN)),
                  _const_spec((CONV_K, CONV_W)), _const_spec((1, CONV_W)),
                  _const_spec((1, CONV_W)), _const_spec((1, CONV_W))],
        out_specs=[cur(POOL_W), cur(CONV_W)],
        scratch_shapes=[pltpu.VMEM((8, CONV_WIN, CONV_W), F32), pltpu.VMEM((8, CONV_WIN, CONV_W), F32),
                        pltpu.VMEM((TP, CONV_W), F32)],
        compiler_params=_cparams(("parallel",)),
        name="pool_conv",
    )(p, p, p, u, u, u, band_main, band_halo, inv_cnt, pw_bd, pool_scale, shifts, conv_w, conv_b, cn_g, cn_b)


def _route(logits):
    lane = lax.broadcasted_iota(jnp.int32, logits.shape, 1)
    big = jnp.int32(LANES)
    lg = jnp.where(lane < N_GROUPS, logits, NEG_INF)
    mg = jnp.max(lg, axis=-1, keepdims=True)
    grp = jnp.min(jnp.where(lg == mg, lane, big), axis=-1, keepdims=True)
    p_grp = 1.0 / jnp.sum(jnp.exp(lg - mg), axis=-1, keepdims=True)
    lo = N_GROUPS + grp * EPG
    le = jnp.where((lane >= lo) & (lane < lo + EPG), logits, NEG_INF)
    m1 = jnp.max(le, axis=-1, keepdims=True)
    i1 = jnp.min(jnp.where(le == m1, lane, big), axis=-1, keepdims=True)
    le2 = jnp.where(lane == i1, NEG_INF, le)
    m2 = jnp.max(le2, axis=-1, keepdims=True)
    i2 = jnp.min(jnp.where(le2 == m2, lane, big), axis=-1, keepdims=True)
    r = jnp.exp(m2 - m1)
    w1 = p_grp / (1.0 + r)
    w2 = p_grp * r / (1.0 + r)
    e1 = (i1 - N_GROUPS).astype(F32)
    e2 = (i2 - N_GROUPS).astype(F32)
    return jnp.where(lane == 0, e1, jnp.where(lane == 1, e2, jnp.where(lane == 2, w1,
                     jnp.where(lane == 3, w2, 0.0))))


def _mix_kernel(x_ref, mod_ref, gn_ref, al_ref, ac_ref, hl_ref, hc_ref, z_ref, cv_ref,
                wa_ref, wf_ref, wp_ref, wc_ref, wg_ref, bg_ref, wo_ref, gf_ref, rh_ref, rl_ref, rb_ref,
                xo_ref, h2_ref, rt_ref):
    is_ctx = pl.program_id(0) >= LAT_TILES
    m = mod_ref[...]
    x = x_ref[...]
    hb = _modulate(x, gn_ref[...], m[0:1], m[1:2]).astype(BF16)
    attn = jnp.where(is_ctx, ac_ref[...], al_ref[...])
    four = jnp.where(is_ctx, hc_ref[...], hl_ref[...])
    branches = ((attn, wa_ref), (four, wf_ref), (z_ref[...], wp_ref), (cv_ref[...], wc_ref))
    acc = None
    for bi, (inp, w_ref) in enumerate(branches):
        cs = slice(bi * D, (bi + 1) * D)
        gate = _sigmoid(_dot(hb, wg_ref[:, cs]) + bg_ref[:, cs])
        term = gate * _dot(inp, w_ref[...])
        acc = term if acc is None else acc + term
    x_new = x + m[2:3] * _dot(acc.astype(BF16), wo_ref[...])
    xo_ref[...] = x_new
    h2 = _modulate(x_new, gf_ref[...], m[3:4], m[4:5])
    hi = h2.astype(BF16)
    h2_ref[...] = hi
    lo = (h2 - hi.astype(F32)).astype(BF16)
    logits = _dot(hi, rh_ref[...]) + _dot(lo, rh_ref[...]) + _dot(hi, rl_ref[...]) + rb_ref[...]
    rt_ref[...] = _route(logits)


def _mix_call(x, mods_l, gn, a_lat, a_ctx, h_lat, h_ctx, zc, cact, l, stacked, small):
    tok = lambda w: pl.BlockSpec((TM, w), lambda i: (i, 0))
    lat = lambda w: pl.BlockSpec((TM, w), lambda i: (jnp.minimum(i, LAT_TILES - 1), 0))
    wa, wf, wp, wc, wg, wo, rh, rl = stacked
    bg, gf, rb = small
    in_specs = [tok(D), pl.BlockSpec((None, 6, D), lambda i: (_mod_row(i), 0, 0)), _const_spec((1, D)),
                lat(Q_W), _const_spec((N_CTX, Q_W)),
                lat(2 * FOURIER_W), _const_spec((N_CTX, 2 * FOURIER_W)),
                tok(POOL_W), tok(CONV_W)]
    in_specs += [_layer_spec(w.shape[1:], l) for w in (wa, wf, wp, wc, wg)]
    in_specs += [_const_spec(bg.shape), _layer_spec(wo.shape[1:], l), _const_spec(gf.shape),
                 _layer_spec(rh.shape[1:], l), _layer_spec(rl.shape[1:], l), _const_spec(rb.shape)]
    return pl.pallas_call(
        _mix_kernel,
        out_shape=[jax.ShapeDtypeStruct((N_TOK, D), F32), jax.ShapeDtypeStruct((N_TOK, D), BF16),
                   jax.ShapeDtypeStruct((N_TOK, LANES), F32)],
        grid=(NT,),
        in_specs=in_specs,
        out_specs=[tok(D), tok(D), tok(LANES)],
        compiler_params=_cparams(("parallel",)),
        name="mix",
    )(x, mods_l, gn, a_lat, a_ctx, h_lat, h_ctx, zc, cact, wa, wf, wp, wc, wg, bg, wo, gf, rh, rl, rb)


def _onehots(route):
    lane = lax.broadcasted_iota(jnp.int32, route.shape, 1)
    e1 = route[:, 0:1].astype(jnp.int32)
    e2 = route[:, 1:2].astype(jnp.int32)
    return (lane == e1).astype(F32), (lane == e2).astype(F32)


def _lane_cumsum(row):
    lane = lax.broadcasted_iota(jnp.int32, row.shape, 1)
    sh = 1
    while sh < N_EXPERTS:
        row = row + jnp.where(lane >= sh, pltpu.roll(row, sh, 1), 0.0)
        sh *= 2
    return row


def _rank_kernel(rt_ref, tri_ref, pos_ref, meta_ref, cnt_ref, carry_ref):
    i = pl.program_id(0)

    @pl.when(i == 0)
    def _():
        carry_ref[...] = jnp.zeros_like(carry_ref)

    oh1, oh2 = _onehots(rt_ref[...])
    both = oh1 + oh2
    carry = carry_ref[0:1, :]
    tile_cnt = jnp.sum(both, axis=0, keepdims=True)
    tile_cnt = tile_cnt + (tile_cnt - 2.0 * jnp.floor(tile_cnt * 0.5))
    tile_off = _lane_cumsum(tile_cnt) - tile_cnt
    where = _dot(tri_ref[...], both.astype(BF16)) + tile_off
    p1 = jnp.sum(oh1 * where, axis=-1, keepdims=True)
    p2 = jnp.sum(oh2 * where, axis=-1, keepdims=True)
    lane = lax.broadcasted_iota(jnp.int32, both.shape, 1)
    pos_ref[...] = jnp.where(lane == 0, p1, jnp.where(lane == 1, p2, 0.0))
    row = lax.broadcasted_iota(jnp.int32, meta_ref.shape, 0)
    meta_ref[...] = jnp.where(row == 0, tile_off, jnp.where(row == 1, tile_cnt, jnp.where(row == 2, carry, 0.0)))
    total = carry + tile_cnt
    carry_ref[...] = jnp.broadcast_to(total, carry_ref.shape)
    cnt_ref[...] = jnp.broadcast_to(total, cnt_ref.shape)


def _runs_kernel(meta_ref, cnt_ref, runs_ref, be_ref):
    lane = lax.broadcasted_iota(jnp.int32, (1, LANES), 1)
    counts = cnt_ref[0:1, :]
    padded = jnp.floor((counts + (RUN_ROWS - 1 + MOE_BLK - 1)) * (1.0 / MOE_BLK)) * MOE_BLK
    padded = jnp.where(lane < N_EXPERTS, padded, 0.0)
    ends = _lane_cumsum(padded)
    starts = ends - padded
    for t in range(NT):
        m = meta_ref[t]
        row = lax.broadcasted_iota(jnp.int32, m.shape, 0)
        runs_ref[t] = jnp.where(row == 2, m + starts, m).astype(jnp.int32)
    blk = lax.broadcasted_iota(jnp.int32, be_ref.shape, 0).astype(F32) * MOE_BLK
    lane_b = lax.broadcasted_iota(jnp.int32, be_ref.shape, 1)
    done = jnp.where((ends <= blk) & (lane_b < N_EXPERTS), 1.0, 0.0)
    be = jnp.minimum(jnp.sum(done, axis=-1, keepdims=True), N_EXPERTS - 1.0)
    nblk = jnp.max(jnp.where(lane_b == N_EXPERTS - 1, ends, 0.0), axis=-1, keepdims=True) * (1.0 / MOE_BLK)
    be_ref[...] = jnp.where(lane_b == 0, be, jnp.where(lane_b == 1, nblk, 0.0)).astype(jnp.int32)


def _plan_call(route, tri):
    tok = pl.BlockSpec((TM, LANES), lambda i: (i, 0))
    pos, meta, counts = pl.pallas_call(
        _rank_kernel,
        out_shape=[jax.ShapeDtypeStruct((N_TOK, LANES), F32), jax.ShapeDtypeStruct((NT, 8, LANES), F32),
                   jax.ShapeDtypeStruct((8, LANES), F32)],
        grid=(NT,),
        in_specs=[tok, _const_spec((TM, TM))],
        out_specs=[tok, pl.BlockSpec((None, 8, LANES), lambda i: (i, 0, 0)),
                   pl.BlockSpec((8, LANES), lambda i: (0, 0))],
        scratch_shapes=[pltpu.VMEM((8, LANES), F32)],
        compiler_params=_cparams(("arbitrary",)),
        name="moe_rank",
    )(route, tri)
    runs, blk = pl.pallas_call(
        _runs_kernel,
        out_shape=[jax.ShapeDtypeStruct((NT, 8, LANES), jnp.int32),
                   jax.ShapeDtypeStruct((256, LANES), jnp.int32)],
        name="moe_runs",
    )(meta, counts)
    runs_flat = runs[:, 0:3, 0:N_EXPERTS].reshape(NT, 1, 3 * N_EXPERTS)
    return pos, runs_flat, blk[:N_MOE_BLOCKS, 0], blk[0:1, 1]


def _pack_pairs(x):
    half = x.shape[1] // 2
    lo = pltpu.bitcast(x[:, :half], jnp.uint32)
    hi = pltpu.bitcast(x[:, half:], jnp.uint32)
    return (lo >> 16) | (hi & jnp.uint32(0xFFFF0000))


def _unpack_pairs(w):
    lo = pltpu.bitcast(w << 16, F32)
    hi = pltpu.bitcast(w & jnp.uint32(0xFFFF0000), F32)
    return jnp.concatenate([lo, hi], axis=1).astype(BF16)


def _run_fields(runs_ref, e):
    return runs_ref[0, e], runs_ref[0, N_EXPERTS + e], runs_ref[0, 2 * N_EXPERTS + e]


def _store_rows(lin_ref, packed):
    rows = packed.shape[0]
    for c in range(PK):
        lin_ref[pl.ds(c, rows, stride=PK), :] = packed[:, c * LANES:(c + 1) * LANES]


def _load_rows(lin_ref, rows):
    return jnp.concatenate([lin_ref[pl.ds(c, rows, stride=PK), :] for c in range(PK)], axis=1)


def _lin(ref, row, nrows):
    return ref.at[pl.ds(pl.multiple_of(row * PK, 8), nrows * PK), :]


def _dispatch_kernel(runs_ref, prev_runs_ref, pos_ref, h2_ref, xs_in_ref, xs_ref, buf0, buf1, sem0, sem1):
    del xs_in_ref
    i = pl.program_id(0)
    last = NT - 1

    @pl.when(i == 0)
    def _():
        for buf in (buf0, buf1):
            buf[TS * PK:, :] = jnp.zeros((RUN_ROWS * PK, LANES), jnp.uint32)

    def run_copies(runs, buf, sem, act):
        def chunk_copy(off, dst, k):
            return pltpu.make_async_copy(_lin(buf, off + k * RUN_ROWS, RUN_ROWS),
                                         _lin(xs_ref, dst + k * RUN_ROWS, RUN_ROWS), sem)

        def per_expert(e, carry):
            off, n, dst = _run_fields(runs, e)
            lax.fori_loop(0, (n + RUN_ROWS - 1) // RUN_ROWS, lambda k, c: (act(chunk_copy(off, dst, k)), c)[1], 0)
            return carry

        lax.fori_loop(0, N_EXPERTS, per_expert, 0)

    start = lambda d: d.start()
    wait = lambda d: d.wait()

    pos = pos_ref[...]
    col = lax.broadcasted_iota(jnp.int32, (TM, TS), 1).astype(F32)
    sel = jnp.where((col == pos[:, 0:1]) | (col == pos[:, 1:2]), 1.0, 0.0).astype(BF16)
    srt = lax.dot_general(sel, h2_ref[...], (((0,), (0,)), ((), ())), preferred_element_type=F32)
    packed = _pack_pairs(srt)

    for par, (buf, sem, obuf, osem) in enumerate(((buf0, sem0, buf1, sem1), (buf1, sem1, buf0, sem0))):
        @pl.when(i % 2 == par)
        def _(buf=buf, sem=sem, obuf=obuf, osem=osem):
            _store_rows(buf, packed)

            @pl.when(i > 0)
            def _():
                run_copies(prev_runs_ref, obuf, osem, wait)

            run_copies(runs_ref, buf, sem, start)

            @pl.when(i == last)
            def _():
                run_copies(runs_ref, buf, sem, wait)


def _dispatch_call(runs_flat, pos, h2, zeros_slots):
    runs_spec = lambda shift: pl.BlockSpec((None, 1, 3 * N_EXPERTS), lambda i: (jnp.maximum(i - shift, 0), 0, 0),
                                           memory_space=pltpu.SMEM)
    sorted_buf = pltpu.VMEM(((TS + RUN_ROWS) * PK, LANES), jnp.uint32)
    return pl.pallas_call(
        _dispatch_kernel,
        out_shape=jax.ShapeDtypeStruct((N_SLOTS * PK, LANES), jnp.uint32),
        grid=(NT,),
        in_specs=[runs_spec(0), runs_spec(1),
                  pl.BlockSpec((TM, LANES), lambda i: (i, 0)),
                  pl.BlockSpec((TM, D), lambda i: (i, 0)),
                  pl.BlockSpec(memory_space=pl.ANY)],
        out_specs=pl.BlockSpec(memory_space=pl.ANY),
        scratch_shapes=[sorted_buf, sorted_buf, pltpu.SemaphoreType.DMA, pltpu.SemaphoreType.DMA],
        input_output_aliases={4: 0},
        compiler_params=_cparams(("arbitrary",)),
        name="moe_dispatch",
    )(runs_flat, runs_flat, pos, h2, zeros_slots)


def _expert_kernel(be_ref, nu_ref, xs_ref, wg_ref, wu_ref, wd_ref, ys_ref, wgb_ref, wub_ref, wdb_ref):
    b = pl.program_id(0)

    @pl.when(jnp.logical_or(b == 0, be_ref[b] != be_ref[jnp.maximum(b - 1, 0)]))
    def _():
        wgb_ref[...] = wg_ref[...].astype(BF16)
        wub_ref[...] = wu_ref[...].astype(BF16)
        wdb_ref[...] = wd_ref[...].astype(BF16)

    @pl.when(b < nu_ref[0])
    def _():
        xb = _unpack_pairs(_load_rows(xs_ref, MOE_BLK))
        g = _dot(xb, wgb_ref[...])
        u = _dot(xb, wub_ref[...])
        hmid = (g * _sigmoid(g)) * u
        y = _dot(hmid.astype(BF16), wdb_ref[...])
        _store_rows(ys_ref, _pack_pairs(y.astype(BF16).astype(F32)))

    @pl.when(b >= nu_ref[0])
    def _():
        ys_ref[...] = jnp.zeros_like(ys_ref)


def _expert_call(blk_e, n_used, xs, wg, wu, wd, l):
    wspec = lambda k, n: pl.BlockSpec((None, None, k, n), lambda b, be, nu: (l, be[b], 0, 0))
    return pl.pallas_call(
        _expert_kernel,
        out_shape=jax.ShapeDtypeStruct((N_SLOTS * PK, LANES), jnp.uint32),
        grid_spec=pltpu.PrefetchScalarGridSpec(
            num_scalar_prefetch=2,
            grid=(N_MOE_BLOCKS,),
            in_specs=[pl.BlockSpec((MOE_BLK * PK, LANES), lambda b, be, nu: (jnp.minimum(b, nu[0] - 1), 0)),
                      wspec(D, EXPERT_HIDDEN), wspec(D, EXPERT_HIDDEN), wspec(EXPERT_HIDDEN, D)],
            out_specs=pl.BlockSpec((MOE_BLK * PK, LANES), lambda b, be, nu: (b, 0)),
            scratch_shapes=[pltpu.VMEM((D, EXPERT_HIDDEN), BF16), pltpu.VMEM((D, EXPERT_HIDDEN), BF16),
                            pltpu.VMEM((EXPERT_HIDDEN, D), BF16)]),
        compiler_params=_cparams(("arbitrary",)),
        name="moe_experts",
    )(blk_e, n_used, xs, wg, wu, wd)


RUN_PIECES = (32, 16, 8, 4, 2)


def _combine_kernel(runs_ref, next_runs_ref, ys_ref, pos_ref, x_ref, rt_ref, mod_ref, o_ref, buf0, buf1, sem0, sem1,
                    *, n_tiles):
    i = pl.program_id(0)
    last = n_tiles - 1

    def run_copies(runs, buf, sem, act):
        def piece(off, dst, size):
            return pltpu.make_async_copy(_lin(ys_ref, dst, size), _lin(buf, off, size), sem)

        def per_expert(e, carry):
            off, n, dst = _run_fields(runs, e)
            whole = n // RUN_ROWS

            def chunk(k, c):
                act(piece(off + k * RUN_ROWS, dst + k * RUN_ROWS, RUN_ROWS))
                return c

            lax.fori_loop(0, whole, chunk, 0)
            done = whole * RUN_ROWS
            for size in RUN_PIECES:
                @pl.when((n & size) != 0)
                def _(done=done, size=size):
                    act(piece(off + done, dst + done, size))
                done = done + (n & size)
            return carry

        lax.fori_loop(0, N_EXPERTS, per_expert, 0)

    start = lambda d: d.start()
    wait = lambda d: d.wait()

    @pl.when(i == 0)
    def _():
        buf0[...] = jnp.zeros_like(buf0)
        buf1[...] = jnp.zeros_like(buf1)
        run_copies(runs_ref, buf0, sem0, start)

    pos = pos_ref[...]
    rt = rt_ref[...]
    col = lax.broadcasted_iota(jnp.int32, (TM, TS), 1).astype(F32)
    pick = (jnp.where(col == pos[:, 0:1], rt[:, TOP_K:TOP_K + 1], 0.0)
            + jnp.where(col == pos[:, 1:2], rt[:, TOP_K + 1:TOP_K + 2], 0.0)).astype(BF16)

    for par, (buf, sem, obuf, osem) in enumerate(((buf0, sem0, buf1, sem1), (buf1, sem1, buf0, sem0))):
        @pl.when(i % 2 == par)
        def _(buf=buf, sem=sem, obuf=obuf, osem=osem):
            @pl.when(i < last)
            def _():
                run_copies(next_runs_ref, obuf, osem, start)

            run_copies(runs_ref, buf, sem, wait)
            ysb = _unpack_pairs(_load_rows(buf, TS))
            o_ref[...] = x_ref[...] + mod_ref[5:6, :] * _dot(pick, ysb)


def _combine_call(runs_flat, ys, pos, x, route, mods_l, n_tiles):
    tok = lambda w: pl.BlockSpec((TM, w), lambda i: (i, 0))
    runs_spec = lambda shift: pl.BlockSpec((None, 1, 3 * N_EXPERTS),
                                           lambda i: (jnp.minimum(i + shift, n_tiles - 1), 0, 0),
                                           memory_space=pltpu.SMEM)
    sorted_buf = pltpu.VMEM((TS * PK, LANES), jnp.uint32)
    return pl.pallas_call(
        functools.partial(_combine_kernel, n_tiles=n_tiles),
        out_shape=jax.ShapeDtypeStruct((n_tiles * TM, D), F32),
        grid=(n_tiles,),
        in_specs=[runs_spec(0), runs_spec(1),
                  pl.BlockSpec(memory_space=pl.ANY),
                  tok(LANES), tok(D), tok(LANES),
                  pl.BlockSpec((None, 6, D), lambda i: (_mod_row(i), 0, 0))],
        out_specs=tok(D),
        scratch_shapes=[sorted_buf, sorted_buf, pltpu.SemaphoreType.DMA, pltpu.SemaphoreType.DMA],
        compiler_params=_cparams(("arbitrary",)),
        name="moe_combine",
    )(runs_flat, runs_flat, ys, pos, x, route, mods_l)


def _rope_tables():
    nf = HEAD_DIM // 4
    inv = ROPE_BASE ** (-jnp.arange(nf, dtype=F32) / nf)
    t = jnp.arange(S)
    row = (t // GRID_W).astype(F32)[:, None] * inv[None, :]
    col = (t % GRID_W).astype(F32)[:, None] * inv[None, :]
    zero = jnp.zeros_like(row)
    cos = jnp.concatenate([jnp.cos(row), jnp.cos(row), jnp.cos(col), jnp.cos(col)], axis=1)
    sa = jnp.concatenate([-jnp.sin(row), zero, -jnp.sin(col), zero], axis=1)
    sb = jnp.concatenate([zero, jnp.sin(row), zero, jnp.sin(col)], axis=1)
    ident = (jnp.ones((TM, HEAD_DIM), F32), jnp.zeros((TM, HEAD_DIM), F32), jnp.zeros((TM, HEAD_DIM), F32))
    return tuple(jnp.tile(jnp.concatenate([a, b], axis=0), (1, LANES // HEAD_DIM))
                 for a, b in zip((cos, sa, sb), ident))


def _fourier_tables():
    s1 = np.arange(FS1)
    ang1 = 2.0 * np.pi * np.outer(s1, s1) / FS1
    w1 = np.concatenate([np.cos(ang1), -np.sin(ang1)], axis=0) / np.sqrt(S)
    k1 = np.arange(FS1)[:, None, None]
    k2 = np.arange(FS2)[None, :, None]
    s2 = np.arange(FS2)[None, None, :]
    ang2 = 2.0 * np.pi * ((k1 + FS1 * k2) * s2 % S) / S
    c2, sn2 = np.cos(ang2), np.sin(ang2)
    ta = np.concatenate([c2, -sn2], axis=1)
    tb = np.concatenate([sn2, c2], axis=1)
    sc = np.arange(C)
    angc = 2.0 * np.pi * np.outer(sc, sc) / C
    wc = np.concatenate([np.cos(angc), -np.sin(angc)], axis=0) / np.sqrt(C)
    return tuple(jnp.asarray(a, F32).astype(BF16) for a in (w1, ta, tb, wc))


def _channel_dft():
    cidx = np.arange(FOURIER_GROUP_W)
    ang = 2.0 * np.pi * np.outer(cidx, cidx) / FOURIER_GROUP_W
    eye = np.eye(FOURIER_W // FOURIER_GROUP_W)
    cw = np.kron(eye, np.cos(ang)) / np.sqrt(FOURIER_GROUP_W)
    sw = np.kron(eye, np.sin(ang)) / np.sqrt(FOURIER_GROUP_W)
    return jnp.asarray(np.concatenate([cw, sw], axis=0), F32)


def _pool_bands():
    t = np.arange(TP)[:, None]
    main, halo = [], []
    for w in POOL_WINDOWS:
        def hit(j):
            return ((j - t >= -(w // 2)) & (j - t <= w // 2 - 1)).astype(np.float32)
        main.append(hit(np.arange(TP)[None, :]))
        halo.append(np.concatenate([hit(np.arange(-HALO, 0)[None, :]),
                                    hit(np.arange(TP, TP + HALO)[None, :])], axis=1))
    return (jnp.asarray(np.stack(main), F32).astype(BF16), jnp.asarray(np.stack(halo), F32).astype(BF16))


def _pool_inv_counts():
    win = np.repeat(np.array(POOL_WINDOWS), POOL_GROUP_W)[None, :]

    def table(pos0, seq_len):
        pos = (pos0 + np.arange(TP))[:, None]
        lo = np.clip(pos - win // 2, 0, seq_len)
        hi = np.clip(pos - win // 2 + win, 0, seq_len)
        return 1.0 / (hi - lo)

    tabs = [table(TP, S), table(0, S), table(S - TP, S), table(0, C)]
    return jnp.asarray(np.stack(tabs), F32)


def _conv_shifts():
    i = np.arange(CONV_WIN)
    return jnp.asarray(np.stack([(i[None, :] == i[:, None] + s) for s in range(1, 8)]), F32).astype(BF16)


def _fold_kernel(a_ref, b_ref, o_ref):
    o_ref[...] = jnp.dot(a_ref[...], b_ref[...], preferred_element_type=F32,
                         precision=lax.Precision.HIGHEST).astype(BF16)


def _fold_fourier_weights(dftw, w_br_fourier):
    nl = w_br_fourier.shape[0]
    return pl.pallas_call(
        _fold_kernel,
        out_shape=jax.ShapeDtypeStruct((nl, 2 * FOURIER_W, D), BF16),
        grid=(nl,),
        in_specs=[pl.BlockSpec((2 * FOURIER_W, FOURIER_W), lambda l: (0, 0)),
                  pl.BlockSpec((None, FOURIER_W, D), lambda l: (l, 0, 0))],
        out_specs=pl.BlockSpec((None, 2 * FOURIER_W, D), lambda l: (l, 0, 0)),
        compiler_params=_cparams(("arbitrary",)),
        name="fold_fourier_proj",
    )(dftw, w_br_fourier)


def _block_diag(blocks):
    n, r, c = blocks.shape
    out = jnp.zeros((n * r, n * c), blocks.dtype)
    for i in range(n):
        out = lax.dynamic_update_slice(out, blocks[i], (i * r, i * c))
    return out


def kernel(x, c, ctx, c_ctx, w_ada, b_ada, g_norm_mix, g_norm_ffn, w_in, g_q, g_k, sink, w_br_attn,
           w_br_fourier, pool_w, pool_scale, w_br_pool, conv_w, conv_b, cn_g, cn_b, w_br_conv, w_gate,
           b_gate, w_out, w_router_grp, b_router_grp, w_router_exp, b_router_exp, w_e_gate, w_e_up,
           w_e_down):
    xs = jnp.concatenate([x.reshape(N_LAT, D), ctx.reshape(N_CTX, D)], axis=0)
    nl = w_ada.shape[0]
    mods = _ada_all(c, c_ctx, w_ada, b_ada).reshape(nl, 8, 6, D)
    rope_tabs = _rope_tables()
    four_tabs = _fourier_tables()
    band_main, band_halo = _pool_bands()
    inv_cnt = _pool_inv_counts()
    shifts = _conv_shifts()
    wf_all = _fold_fourier_weights(_channel_dft(), w_br_fourier)
    bd = jnp.asarray(np.kron(np.eye(LANES // HEAD_DIM), np.ones((HEAD_DIM, HEAD_DIM))), F32).astype(BF16)
    tri = jnp.asarray(np.tril(np.ones((TM, TM)), -1), F32).astype(BF16)
    zeros_slots = jnp.zeros((N_SLOTS * PK, LANES), jnp.uint32)
    rpad = jnp.zeros((nl, D, LANES - N_GROUPS - N_EXPERTS), F32)
    w_router = jnp.concatenate([w_router_grp, w_router_exp, rpad], axis=-1)
    r_hi = w_router.astype(BF16)
    r_lo = (w_router - r_hi.astype(F32)).astype(BF16)
    r_b = jnp.concatenate([b_router_grp, b_router_exp, rpad[:, 0, :]], axis=-1).reshape(nl, 1, LANES)
    stacked = tuple(w.astype(BF16) for w in (w_br_attn,)) + (wf_all,) + tuple(
        w.astype(BF16) for w in (w_br_pool, w_br_conv, w_gate, w_out)) + (r_hi, r_lo)

    for l in range(nl):
        mods_l = mods[l]
        gn = g_norm_mix[l].reshape(1, D)
        q, kv, f, p, u = _proj_call(xs, mods_l, gn, w_in, l, rope_tabs,
                                    jnp.tile(g_q[l], 2).reshape(1, LANES),
                                    jnp.tile(g_k[l], 2).reshape(1, LANES), bd)
        a_lat, a_ctx = _attn_call(sink[l], q, kv)
        h_lat, h_ctx = _fourier_call(f, four_tabs)
        zc, cact = _poolconv_call(p, u, band_main, band_halo, inv_cnt, _block_diag(pool_w[l]).astype(BF16),
                                  pool_scale[l].reshape(1, POOL_W), shifts, conv_w[l], conv_b[l].reshape(1, CONV_W),
                                  cn_g[l].reshape(1, CONV_W), cn_b[l].reshape(1, CONV_W))
        small = (b_gate[l].reshape(1, 4 * D), g_norm_ffn[l].reshape(1, D), r_b[l])
        xs, h2, route = _mix_call(xs, mods_l, gn, a_lat, a_ctx, h_lat, h_ctx, zc, cact, l, stacked, small)
        pos, runs_flat, blk_e, n_used = _plan_call(route, tri)
        slots = _dispatch_call(runs_flat, pos, h2, zeros_slots)
        ys = _expert_call(blk_e, n_used, slots, w_e_gate, w_e_up, w_e_down, l)
        xs = _combine_call(runs_flat, ys, pos, xs, route, mods_l, LAT_TILES if l == nl - 1 else NT)
    return xs.reshape(B, S, D)
```

```python
import functools

import numpy as np
import jax
import jax.numpy as jnp
from jax import lax
from jax.experimental import pallas as pl
from jax.experimental.pallas import tpu as pltpu

F32 = jnp.float32
BF16 = jnp.bfloat16

D = 1024
B = 2
S = 8192
C = 256
GRID_W = 64
HEAD_DIM = 64
N_Q_HEADS = 8
N_KV_HEADS = 2
GQA = N_Q_HEADS // N_KV_HEADS
WINDOW = 128
ATTN_BLK = 128
ATTN_QB = 4
ROPE_BASE = 10000.0
Q_W = 512
KV_W = 128
FOURIER_W = 640
FOURIER_GROUP_W = 160
POOL_W = 640
POOL_GROUP_W = 160
POOL_WINDOWS = (2, 4, 8, 16)
CONV_W = 512
CONV_K = 31
PROJ_W = 3072
N_GROUPS = 4
EPG = 8
N_EXPERTS = 32
TOP_K = 2
EXPERT_HIDDEN = 512
MOE_BLK = 512
EPS = 1e-6
NEG_INF = -1e30
LOG2E = 1.4426950408889634

N_LAT = B * S
N_CTX = B * C
N_TOK = N_LAT + N_CTX
TM = 512
NT = N_TOK // TM
LAT_TILES = N_LAT // TM
TILES_PER_BATCH = S // TM
TP = 256
NTP = N_TOK // TP
HALO = 16
CONV_WIN = TP // 2 + 2 * HALO
N_ASSIGN = N_TOK * TOP_K
RUN_ROWS = 48
RUN_FIELDS = 5
PK = D // 2 // 128
TS = 1152
N_MOE_BLOCKS = (N_ASSIGN + NT * N_EXPERTS + N_EXPERTS * (RUN_ROWS - 1 + MOE_BLK - 1)) // MOE_BLK
N_SLOTS = N_MOE_BLOCKS * MOE_BLK
FS1 = 64
FS2 = 128
F1_COLS = FS2 * FOURIER_W
F1_CW = 8192
F2_K1 = 8
LANES = 128
VMEM_LIMIT = 56 * 1024 * 1024


def _cparams(sem, vmem=VMEM_LIMIT):
    return pltpu.CompilerParams(dimension_semantics=sem, vmem_limit_bytes=vmem)


def _const_spec(shape):
    nd = len(shape)
    return pl.BlockSpec(shape, lambda *_: (0,) * nd, pipeline_mode=pl.Buffered(1))


def _dot(a, b):
    return jnp.dot(a, b, preferred_element_type=F32)


def _modulate(x, g, shift, scale):
    y = x * lax.rsqrt(jnp.mean(x * x, axis=-1, keepdims=True) + EPS)
    return (y * g) * (1.0 + scale) + shift


def _sigmoid(x):
    return 1.0 / (1.0 + jnp.exp(-x))


def _ada_kernel(ct_ref, w_ref, b_ref, o_ref):
    ct = ct_ref[...]
    s = ct * _sigmoid(ct)
    w = w_ref[...]
    rows = [jnp.sum(w * s[:, r:r + 1], axis=0, keepdims=True) for r in range(3)]
    rows.append(jnp.zeros((5, w.shape[1]), F32))
    o_ref[...] = jnp.concatenate(rows, axis=0) + b_ref[...]


def _ada_all(c, c_ctx, w_ada, b_ada):
    ct = jnp.concatenate([c, c_ctx[None, :], jnp.zeros((5, D), F32)], axis=0).T
    cols = 1536
    nl = w_ada.shape[0]
    return pl.pallas_call(
        _ada_kernel,
        out_shape=jax.ShapeDtypeStruct((nl, 8, 6 * D), F32),
        grid=(nl, 6 * D // cols),
        in_specs=[pl.BlockSpec((D, 8), lambda l, j: (0, 0)),
                  pl.BlockSpec((None, D, cols), lambda l, j: (l, 0, j)),
                  pl.BlockSpec((None, 1, cols), lambda l, j: (l, 0, j))],
        out_specs=pl.BlockSpec((None, 8, cols), lambda l, j: (l, 0, j)),
        compiler_params=_cparams(("arbitrary", "arbitrary")),
        name="adaln",
    )(ct, w_ada, b_ada.reshape(nl, 1, 6 * D))


def _head_rms(t, g128, bd):
    outs = []
    for j in range(t.shape[1] // LANES):
        blk = t[:, j * LANES:(j + 1) * LANES]
        ss = _dot((blk * blk).astype(BF16), bd)
        outs.append(blk * lax.rsqrt(ss * (1.0 / HEAD_DIM) + EPS) * g128)
    return outs


def _rope(blocks, cos, sa, sb):
    outs = []
    for blk in blocks:
        up = pltpu.roll(blk, LANES - 16, 1)
        dn = pltpu.roll(blk, 16, 1)
        outs.append(blk * cos + up * sa + dn * sb)
    return outs


def _proj_kernel(x_ref, mod_ref, gn_ref, w_ref, cos_ref, sa_ref, sb_ref, gq_ref, gk_ref, bd_ref,
                 q_ref, kv_ref, f_ref, p_ref, u_ref, wbf_ref):
    @pl.when(pl.program_id(0) == 0)
    def _():
        wbf_ref[...] = w_ref[...].astype(BF16)

    m = mod_ref[...]
    h = _modulate(x_ref[...], gn_ref[...], m[0:1], m[1:2])
    proj = _dot(h.astype(BF16), wbf_ref[...])
    cos, sa, sb, bd = cos_ref[...], sa_ref[...], sb_ref[...], bd_ref[...]
    q = _rope(_head_rms(proj[:, 0:Q_W], gq_ref[...], bd), cos, sa, sb)
    q_ref[...] = (jnp.concatenate(q, axis=1) * (LOG2E * HEAD_DIM ** -0.5)).astype(BF16)
    k = _rope(_head_rms(proj[:, Q_W:Q_W + KV_W], gk_ref[...], bd), cos, sa, sb)
    kv_ref[:, 0:KV_W] = k[0].astype(BF16)
    o = Q_W + KV_W
    kv_ref[:, KV_W:2 * KV_W] = proj[:, o:o + KV_W].astype(BF16)
    o += KV_W
    f_ref[...] = proj[:, o:o + FOURIER_W].astype(BF16)
    o += FOURIER_W
    p_ref[...] = proj[:, o:o + POOL_W].astype(BF16)
    o += POOL_W
    a = proj[:, o:o + CONV_W]
    g = proj[:, o + CONV_W:o + 2 * CONV_W]
    u_ref[...] = (a * _sigmoid(g)).astype(BF16)


def _mod_row(i):
    return jnp.minimum(i // TILES_PER_BATCH, 2)


def _layer_spec(shape, l):
    nd = len(shape)
    return pl.BlockSpec((None,) + tuple(shape), lambda *_: (l,) + (0,) * nd, pipeline_mode=pl.Buffered(1))


def _proj_call(x, mods_l, gn, w_in, l, rope_tabs, gq128, gk128, bd):
    cos, sa, sb = rope_tabs
    tok = lambda w: pl.BlockSpec((TM, w), lambda i: (i, 0))
    rope_spec = pl.BlockSpec((TM, LANES), lambda i: (jnp.where(i < LAT_TILES, i % TILES_PER_BATCH,
                                                               TILES_PER_BATCH), 0))
    widths = (Q_W, 2 * KV_W, FOURIER_W, POOL_W, CONV_W)
    return pl.pallas_call(
        _proj_kernel,
        out_shape=[jax.ShapeDtypeStruct((N_TOK, w), BF16) for w in widths],
        grid=(NT,),
        in_specs=[tok(D),
                  pl.BlockSpec((None, 6, D), lambda i: (_mod_row(i), 0, 0)),
                  _const_spec((1, D)),
                  _layer_spec((D, PROJ_W), l),
                  rope_spec, rope_spec, rope_spec,
                  _const_spec((1, LANES)), _const_spec((1, LANES)),
                  _const_spec((LANES, LANES))],
        out_specs=[tok(w) for w in widths],
        scratch_shapes=[pltpu.VMEM((D, PROJ_W), BF16)],
        compiler_params=_cparams(("arbitrary",)),
        name="proj",
    )(x, mods_l, gn, w_in, cos, sa, sb, gq128, gk128, bd)


def _attend_many(jobs, sink_ref):
    lane = lax.broadcasted_iota(jnp.int32, (ATTN_BLK, LANES), 1)
    chains = []
    for q, kv_blocks, biases in jobs:
        for j in range(N_KV_HEADS):
            ks = slice(j * HEAD_DIM, (j + 1) * HEAD_DIM)
            vs = slice(KV_W + j * HEAD_DIM, KV_W + (j + 1) * HEAD_DIM)
            kj = jnp.concatenate([blk[:, ks] for blk in kv_blocks], axis=0)
            vj = jnp.concatenate([blk[:, vs] for blk in kv_blocks], axis=0)
            vaug = jnp.concatenate([vj, jnp.ones_like(vj)], axis=1)
            qs = jnp.concatenate([q[:, (j * GQA + g) * HEAD_DIM:(j * GQA + g + 1) * HEAD_DIM]
                                  for g in range(GQA)], axis=0)
            s = lax.dot_general(qs, kj, (((1,), (1,)), ((), ())), preferred_element_type=F32)
            chains.append((j, s, vaug, kv_blocks, biases))
    soft = []
    for j, s, vaug, kv_blocks, biases in chains:
        probs, sink_terms = [], []
        for g in range(GQA):
            sg = s[g * ATTN_BLK:(g + 1) * ATTN_BLK]
            pieces, col = [], 0
            for blk, bias in zip(kv_blocks, biases):
                piece = sg[:, col:col + blk.shape[0]]
                pieces.append(piece if bias is None else piece + bias)
                col += blk.shape[0]
            sg = jnp.concatenate(pieces, axis=1)
            sk = sink_ref[j * GQA + g] * LOG2E
            mx = jnp.maximum(jnp.max(sg, axis=-1, keepdims=True), sk)
            probs.append(jnp.exp2(sg - mx).astype(BF16))
            sink_terms.append(jnp.exp2(sk - mx))
        soft.append((jnp.concatenate(probs, axis=0), vaug, sink_terms))
    heads = []
    for p, vaug, sink_terms in soft:
        o = _dot(p, vaug)
        for g in range(GQA):
            og = o[g * ATTN_BLK:(g + 1) * ATTN_BLK]
            heads.append(og / (og[:, HEAD_DIM:HEAD_DIM + 1] + sink_terms[g]))
    outs = []
    for n in range(len(jobs)):
        hs = heads[n * N_Q_HEADS:(n + 1) * N_Q_HEADS]
        tiles = [jnp.where(lane < HEAD_DIM, hs[2 * t], pltpu.roll(hs[2 * t + 1], HEAD_DIM, 1))
                 for t in range(N_Q_HEADS // 2)]
        outs.append(jnp.concatenate(tiles, axis=1).astype(BF16))
    return outs


def _attn_latent_kernel(sink_ref, q_ref, prev_ref, cur_ref, next_ref, ctx_ref, o_ref):
    n = pl.program_id(1)
    r = lax.broadcasted_iota(jnp.int32, (ATTN_BLK, ATTN_BLK), 0)
    jj = lax.broadcasted_iota(jnp.int32, (ATTN_BLK, ATTN_BLK), 1)
    far = jnp.int32(2 * ATTN_BLK)
    off_prev = jnp.where(n > 0, 0, far)
    off_next = jnp.where(n < S // (ATTN_QB * ATTN_BLK) - 1, 0, far)
    prev_ok = jnp.where(jj - r >= 0, 0.0, NEG_INF)
    next_ok = jnp.where(r - jj >= 0, 0.0, NEG_INF)
    prev_edge = jnp.where(jj - r - off_prev >= 0, 0.0, NEG_INF)
    next_edge = jnp.where(r - jj - off_next >= 0, 0.0, NEG_INF)
    ctx = ctx_ref[...]
    rows = lambda b: slice(b * ATTN_BLK, (b + 1) * ATTN_BLK)
    blocks = [prev_ref[...]] + [cur_ref[rows(b), :] for b in range(ATTN_QB)] + [next_ref[...]]
    jobs = [(q_ref[rows(b), :], [ctx] + blocks[b:b + 3],
             [None, prev_edge if b == 0 else prev_ok, None, next_edge if b == ATTN_QB - 1 else next_ok])
            for b in range(ATTN_QB)]
    for b, out in enumerate(_attend_many(jobs, sink_ref)):
        o_ref[rows(b), :] = out


def _attn_context_kernel(sink_ref, q_ref, ctx_ref, o_ref):
    o_ref[...] = _attend_many([(q_ref[...], [ctx_ref[...]], [None])], sink_ref)[0]


def _attn_call(sink_l, q, kv):
    nb = S // ATTN_BLK
    nq = nb // ATTN_QB
    smem = pl.BlockSpec(memory_space=pltpu.SMEM)
    pair = lambda w: pl.BlockSpec((ATTN_QB * ATTN_BLK, w), lambda b, n: (b * nq + n, 0))
    prev = pl.BlockSpec((ATTN_BLK, 2 * KV_W), lambda b, n: (b * nb + jnp.maximum(ATTN_QB * n - 1, 0), 0))
    nxt = pl.BlockSpec((ATTN_BLK, 2 * KV_W),
                       lambda b, n: (b * nb + jnp.minimum(ATTN_QB * (n + 1), nb - 1), 0))
    ctxs = pl.BlockSpec((C, 2 * KV_W), lambda b, n: (N_LAT // C + b, 0))
    lat = pl.pallas_call(
        _attn_latent_kernel,
        out_shape=jax.ShapeDtypeStruct((N_LAT, Q_W), BF16),
        grid=(B, nq),
        in_specs=[smem, pair(Q_W), prev, pair(2 * KV_W), nxt, ctxs],
        out_specs=pair(Q_W),
        compiler_params=_cparams(("parallel", "parallel")),
        name="attn_latent",
    )(sink_l, q, kv, kv, kv, kv)
    ncb = C // ATTN_BLK
    base = N_LAT // ATTN_BLK
    ctx = pl.pallas_call(
        _attn_context_kernel,
        out_shape=jax.ShapeDtypeStruct((N_CTX, Q_W), BF16),
        grid=(B, ncb),
        in_specs=[smem, pl.BlockSpec((ATTN_BLK, Q_W), lambda b, n: (base + b * ncb + n, 0)), ctxs],
        out_specs=pl.BlockSpec((ATTN_BLK, Q_W), lambda b, n: (b * ncb + n, 0)),
        compiler_params=_cparams(("parallel", "parallel")),
        name="attn_context",
    )(sink_l, q, kv)
    return lat, ctx


def _f1_kernel(w_ref, f_ref, re_ref, im_ref):
    res = _dot(w_ref[...], f_ref[...])
    re_ref[...] = res[:FS1].astype(BF16)
    im_ref[...] = res[FS1:].astype(BF16)


def _f2_kernel(ta_ref, tb_ref, re_ref, im_ref, o_ref):
    for i in range(F2_K1):
        res = _dot(ta_ref[i], re_ref[i]) + _dot(tb_ref[i], im_ref[i])
        o_ref[i, :, 0:FOURIER_W] = res[:FS2].astype(BF16)
        o_ref[i, :, FOURIER_W:2 * FOURIER_W] = res[FS2:].astype(BF16)


def _fc_kernel(w_ref, f_ref, o_ref):
    res = _dot(w_ref[...], f_ref[...])
    o_ref[:, 0:FOURIER_W] = res[:C].astype(BF16)
    o_ref[:, FOURIER_W:2 * FOURIER_W] = res[C:].astype(BF16)


def _fourier_call(f, tabs):
    w1, ta, tb, wc = tabs
    f2d = f.reshape(N_TOK // FS2, F1_COLS)
    nchunk = F1_COLS // F1_CW
    a_re, a_im = pl.pallas_call(
        _f1_kernel,
        out_shape=[jax.ShapeDtypeStruct((B * FS1, F1_COLS), BF16)] * 2,
        grid=(B, nchunk),
        in_specs=[_const_spec((2 * FS1, FS1)),
                  pl.BlockSpec((FS1, F1_CW), lambda b, j: (b, j))],
        out_specs=[pl.BlockSpec((FS1, F1_CW), lambda b, j: (b, j))] * 2,
        compiler_params=_cparams(("parallel", "parallel")),
        name="fourier_stage1",
    )(w1, f2d)
    a_re = a_re.reshape(B * FS1, FS2, FOURIER_W)
    a_im = a_im.reshape(B * FS1, FS2, FOURIER_W)
    nk = FS1 // F2_K1
    aspec = pl.BlockSpec((F2_K1, FS2, FOURIER_W), lambda b, k1: (b * nk + k1, 0, 0))
    tspec = pl.BlockSpec((F2_K1, 2 * FS2, FS2), lambda b, k1: (k1, 0, 0))
    h_t = pl.pallas_call(
        _f2_kernel,
        out_shape=jax.ShapeDtypeStruct((B, FS1, FS2, 2 * FOURIER_W), BF16),
        grid=(B, nk),
        in_specs=[tspec, tspec, aspec, aspec],
        out_specs=pl.BlockSpec((None, F2_K1, FS2, 2 * FOURIER_W), lambda b, k1: (b, k1, 0, 0)),
        compiler_params=_cparams(("parallel", "parallel")),
        name="fourier_stage2",
    )(ta, tb, a_re, a_im)
    h_lat = jnp.transpose(h_t, (0, 2, 1, 3)).reshape(N_LAT, 2 * FOURIER_W)
    h_ctx = pl.pallas_call(
        _fc_kernel,
        out_shape=jax.ShapeDtypeStruct((N_CTX, 2 * FOURIER_W), BF16),
        grid=(B,),
        in_specs=[_const_spec((2 * C, C)),
                  pl.BlockSpec((C, FOURIER_W), lambda b: (N_LAT // C + b, 0))],
        out_specs=pl.BlockSpec((C, 2 * FOURIER_W), lambda b: (b, 0)),
        compiler_params=_cparams(("parallel",)),
        name="fourier_context",
    )(wc, f)
    return h_lat, h_ctx


def _poolconv_kernel(pc_ref, pp_ref, pn_ref, uc_ref, up_ref, un_ref, bm_ref, bh_ref, ic_ref, pw_ref, ps_ref,
                     sh_ref, cw_ref, cb_ref, cg_ref, cnb_ref, z_ref, a_ref, win0_ref, win1_ref, cv_ref):
    t = pl.program_id(0)
    lat_tiles = N_LAT // TP
    per_seq = S // TP
    is_ctx = t >= lat_tiles
    first = jnp.logical_or(t % per_seq == 0, is_ctx)
    last = jnp.logical_or(t % per_seq == per_seq - 1, is_ctx)

    keep_prev = jnp.where(first, 0.0, 1.0)
    keep_next = jnp.where(last, 0.0, 1.0)

    ub = jnp.concatenate([(up_ref[...].astype(F32) * keep_prev).astype(BF16), uc_ref[...],
                          (un_ref[...].astype(F32) * keep_next).astype(BF16)], axis=0)
    off = HALO - CONV_K // 2
    half_rows = TP // 2
    wins = (win0_ref, win1_ref)
    for hf, win_ref in enumerate(wins):
        window = ub[hf * half_rows:hf * half_rows + CONV_WIN]
        win_ref[0] = window.astype(F32)
        for s in range(1, 8):
            win_ref[s] = _dot(sh_ref[s - 1], window)

    pcur = pc_ref[...]
    halo = jnp.concatenate([pp_ref[...].astype(F32) * keep_prev,
                            pn_ref[...].astype(F32) * keep_next], axis=0).astype(BF16)
    sums = []
    for gi in range(len(POOL_WINDOWS)):
        cs = slice(gi * LANES, (gi + 2) * LANES)
        sums.append(_dot(bm_ref[gi], pcur[:, cs]) + _dot(bh_ref[gi], halo[:, cs]))
    lane_t = lax.broadcasted_iota(jnp.int32, (TP, LANES), 1)
    tiles = [sums[0][:, :LANES]]
    for gi in range(1, len(POOL_WINDOWS)):
        split = gi * POOL_GROUP_W - gi * LANES
        tiles.append(jnp.where(lane_t < split, sums[gi - 1][:, LANES:], sums[gi][:, :LANES]))
    tiles.append(sums[-1][:, LANES:])
    zsum = jnp.concatenate(tiles, axis=1)
    z = zsum * ic_ref[...] - pcur.astype(F32)
    z_ref[...] = (_dot(z.astype(BF16), pw_ref[...]) * ps_ref[...]).astype(BF16)

    for hf, win_ref in enumerate(wins):
        base = hf * half_rows
        for cb in range(CONV_W // LANES):
            cs = slice(cb * LANES, (cb + 1) * LANES)
            acc = jnp.zeros((half_rows, LANES), F32) + cb_ref[:, cs]
            for j in range(CONV_K):
                s, m = (off + j) % 8, (off + j) // 8
                acc = acc + win_ref[s, 8 * m:8 * m + half_rows, cs] * cw_ref[j:j + 1, cs]
            cv_ref[base:base + half_rows, cs] = acc
    cv = cv_ref[...]
    mu = jnp.mean(cv, axis=-1, keepdims=True)
    var = jnp.mean(jnp.square(cv - mu), axis=-1, keepdims=True)
    un = (cv - mu) * lax.rsqrt(var + EPS) * cg_ref[...] + cnb_ref[...]
    a_ref[...] = (un * _sigmoid(un)).astype(BF16)


def _poolconv_call(p, u, band_main, band_halo, inv_cnt, pw_bd, pool_scale, shifts, conv_w, conv_b, cn_g, cn_b):
    nh = TP // HALO
    last_h = N_TOK // HALO - 1
    cur = lambda w: pl.BlockSpec((TP, w), lambda t: (t, 0))
    prv = lambda w: pl.BlockSpec((HALO, w), lambda t: (jnp.maximum(t * nh - 1, 0), 0))
    nxt = lambda w: pl.BlockSpec((HALO, w), lambda t: (jnp.minimum((t + 1) * nh, last_h), 0))
    per_seq = S // TP

    def kind(t):
        return jnp.where(t >= N_LAT // TP, 3, jnp.where(t % per_seq == 0, 1, jnp.where(t % per_seq == per_seq - 1, 2, 0)))

    return pl.pallas_call(
        _poolconv_kernel,
        out_shape=[jax.ShapeDtypeStruct((N_TOK, POOL_W), BF16),
                   jax.ShapeDtypeStruct((N_TOK, CONV_W), BF16)],
        grid=(NTP,),
        in_specs=[cur(POOL_W), prv(POOL_W), nxt(POOL_W), cur(CONV_W), prv(CONV_W), nxt(CONV_W),
                  _const_spec((4, TP, TP)), _const_spec((4, TP, 2 * HALO)),
                  pl.BlockSpec((None, TP, POOL_W), lambda t: (kind(t), 0, 0)),
                  _const_spec((POOL_W, POOL_W)), _const_spec((1, POOL_W)),
                  _const_spec((7, CONV_WIN, CONV_WIN)),
                  _const_spec((CONV_K, CONV_W)), _const_spec((1, CONV_W)),
                  _const_spec((1, CONV_W)), _const_spec((1, CONV_W))],
        out_specs=[cur(POOL_W), cur(CONV_W)],
        scratch_shapes=[pltpu.VMEM((8, CONV_WIN, CONV_W), F32), pltpu.VMEM((8, CONV_WIN, CONV_W), F32),
                        pltpu.VMEM((TP, CONV_W), F32)],
        compiler_params=_cparams(("parallel",)),
        name="pool_conv",
    )(p, p, p, u, u, u, band_main, band_halo, inv_cnt, pw_bd, pool_scale, shifts, conv_w, conv_b, cn_g, cn_b)


def _route(logits):
    lane = lax.broadcasted_iota(jnp.int32, logits.shape, 1)
    big = jnp.int32(LANES)
    lg = jnp.where(lane < N_GROUPS, logits, NEG_INF)
    mg = jnp.max(lg, axis=-1, keepdims=True)
    grp = jnp.min(jnp.where(lg == mg, lane, big), axis=-1, keepdims=True)
    p_grp = 1.0 / jnp.sum(jnp.exp(lg - mg), axis=-1, keepdims=True)
    lo = N_GROUPS + grp * EPG
    le = jnp.where((lane >= lo) & (lane < lo + EPG), logits, NEG_INF)
    m1 = jnp.max(le, axis=-1, keepdims=True)
    i1 = jnp.min(jnp.where(le == m1, lane, big), axis=-1, keepdims=True)
    le2 = jnp.where(lane == i1, NEG_INF, le)
    m2 = jnp.max(le2, axis=-1, keepdims=True)
    i2 = jnp.min(jnp.where(le2 == m2, lane, big), axis=-1, keepdims=True)
    r = jnp.exp(m2 - m1)
    w1 = p_grp / (1.0 + r)
    w2 = p_grp * r / (1.0 + r)
    e1 = (i1 - N_GROUPS).astype(F32)
    e2 = (i2 - N_GROUPS).astype(F32)
    return jnp.where(lane == 0, e1, jnp.where(lane == 1, e2, jnp.where(lane == 2, w1,
                     jnp.where(lane == 3, w2, 0.0))))


def _mix_kernel(x_ref, mod_ref, gn_ref, al_ref, ac_ref, hl_ref, hc_ref, z_ref, cv_ref,
                wa_ref, wf_ref, wp_ref, wc_ref, wg_ref, bg_ref, wo_ref, gf_ref, rh_ref, rl_ref, rb_ref,
                xo_ref, h2_ref, rt_ref):
    is_ctx = pl.program_id(0) >= LAT_TILES
    m = mod_ref[...]
    x = x_ref[...]
    hb = _modulate(x, gn_ref[...], m[0:1], m[1:2]).astype(BF16)
    attn = jnp.where(is_ctx, ac_ref[...], al_ref[...])
    four = jnp.where(is_ctx, hc_ref[...], hl_ref[...])
    branches = ((attn, wa_ref), (four, wf_ref), (z_ref[...], wp_ref), (cv_ref[...], wc_ref))
    acc = None
    for bi, (inp, w_ref) in enumerate(branches):
        cs = slice(bi * D, (bi + 1) * D)
        gate = _sigmoid(_dot(hb, wg_ref[:, cs]) + bg_ref[:, cs])
        term = gate * _dot(inp, w_ref[...])
        acc = term if acc is None else acc + term
    x_new = x + m[2:3] * _dot(acc.astype(BF16), wo_ref[...])
    xo_ref[...] = x_new
    h2 = _modulate(x_new, gf_ref[...], m[3:4], m[4:5])
    hi = h2.astype(BF16)
    h2_ref[...] = hi
    lo = (h2 - hi.astype(F32)).astype(BF16)
    logits = _dot(hi, rh_ref[...]) + _dot(lo, rh_ref[...]) + _dot(hi, rl_ref[...]) + rb_ref[...]
    rt_ref[...] = _route(logits)


def _mix_call(x, mods_l, gn, a_lat, a_ctx, h_lat, h_ctx, zc, cact, l, stacked, small):
    tok = lambda w: pl.BlockSpec((TM, w), lambda i: (i, 0))
    lat = lambda w: pl.BlockSpec((TM, w), lambda i: (jnp.minimum(i, LAT_TILES - 1), 0))
    wa, wf, wp, wc, wg, wo, rh, rl = stacked
    bg, gf, rb = small
    in_specs = [tok(D), pl.BlockSpec((None, 6, D), lambda i: (_mod_row(i), 0, 0)), _const_spec((1, D)),
                lat(Q_W), _const_spec((N_CTX, Q_W)),
                lat(2 * FOURIER_W), _const_spec((N_CTX, 2 * FOURIER_W)),
                tok(POOL_W), tok(CONV_W)]
    in_specs += [_layer_spec(w.shape[1:], l) for w in (wa, wf, wp, wc, wg)]
    in_specs += [_const_spec(bg.shape), _layer_spec(wo.shape[1:], l), _const_spec(gf.shape),
                 _layer_spec(rh.shape[1:], l), _layer_spec(rl.shape[1:], l), _const_spec(rb.shape)]
    return pl.pallas_call(
        _mix_kernel,
        out_shape=[jax.ShapeDtypeStruct((N_TOK, D), F32), jax.ShapeDtypeStruct((N_TOK, D), BF16),
                   jax.ShapeDtypeStruct((N_TOK, LANES), F32)],
        grid=(NT,),
        in_specs=in_specs,
        out_specs=[tok(D), tok(D), tok(LANES)],
        compiler_params=_cparams(("parallel",)),
        name="mix",
    )(x, mods_l, gn, a_lat, a_ctx, h_lat, h_ctx, zc, cact, wa, wf, wp, wc, wg, bg, wo, gf, rh, rl, rb)


def _onehots(route):
    lane = lax.broadcasted_iota(jnp.int32, route.shape, 1)
    e1 = route[:, 0:1].astype(jnp.int32)
    e2 = route[:, 1:2].astype(jnp.int32)
    return (lane == e1).astype(F32), (lane == e2).astype(F32)


def _lane_cumsum(row):
    lane = lax.broadcasted_iota(jnp.int32, row.shape, 1)
    sh = 1
    while sh < N_EXPERTS:
        row = row + jnp.where(lane >= sh, pltpu.roll(row, sh, 1), 0.0)
        sh *= 2
    return row


def _rank_kernel(rt_ref, tri_ref, pos_ref, meta_ref, cnt_ref, carry_ref):
    i = pl.program_id(0)

    @pl.when(i == 0)
    def _():
        carry_ref[...] = jnp.zeros_like(carry_ref)

    oh1, oh2 = _onehots(rt_ref[...])
    both = oh1 + oh2
    carry = carry_ref[0:1, :]
    tile_cnt = jnp.sum(both, axis=0, keepdims=True)
    tile_cnt = tile_cnt + (tile_cnt - 2.0 * jnp.floor(tile_cnt * 0.5))
    tile_off = _lane_cumsum(tile_cnt) - tile_cnt
    where = _dot(tri_ref[...], both.astype(BF16)) + tile_off
    p1 = jnp.sum(oh1 * where, axis=-1, keepdims=True)
    p2 = jnp.sum(oh2 * where, axis=-1, keepdims=True)
    lane = lax.broadcasted_iota(jnp.int32, both.shape, 1)
    pos_ref[...] = jnp.where(lane == 0, p1, jnp.where(lane == 1, p2, 0.0))
    row = lax.broadcasted_iota(jnp.int32, meta_ref.shape, 0)
    meta_ref[...] = jnp.where(row == 0, tile_off, jnp.where(row == 1, tile_cnt, jnp.where(row == 2, carry, 0.0)))
    total = carry + tile_cnt
    carry_ref[...] = jnp.broadcast_to(total, carry_ref.shape)
    cnt_ref[...] = jnp.broadcast_to(total, cnt_ref.shape)


def _runs_kernel(meta_ref, cnt_ref, runs_ref, be_ref):
    lane = lax.broadcasted_iota(jnp.int32, (1, LANES), 1)
    counts = cnt_ref[0:1, :]
    padded = jnp.floor((counts + (RUN_ROWS - 1 + MOE_BLK - 1)) * (1.0 / MOE_BLK)) * MOE_BLK
    padded = jnp.where(lane < N_EXPERTS, padded, 0.0)
    ends = _lane_cumsum(padded)
    starts = ends - padded
    for t in range(NT):
        m = meta_ref[t]
        row = lax.broadcasted_iota(jnp.int32, m.shape, 0)
        m = jnp.where(row == 2, m + starts, m)
        m = jnp.where(row == 3, starts + counts, jnp.where(row == 4, padded - counts, m))
        runs_ref[t] = m.astype(jnp.int32)
    blk = lax.broadcasted_iota(jnp.int32, be_ref.shape, 0).astype(F32) * MOE_BLK
    lane_b = lax.broadcasted_iota(jnp.int32, be_ref.shape, 1)
    done = jnp.where((ends <= blk) & (lane_b < N_EXPERTS), 1.0, 0.0)
    be = jnp.minimum(jnp.sum(done, axis=-1, keepdims=True), N_EXPERTS - 1.0)
    nblk = jnp.max(jnp.where(lane_b == N_EXPERTS - 1, ends, 0.0), axis=-1, keepdims=True) * (1.0 / MOE_BLK)
    be_ref[...] = jnp.where(lane_b == 0, be, jnp.where(lane_b == 1, nblk, 0.0)).astype(jnp.int32)


def _plan_call(route, tri):
    tok = pl.BlockSpec((TM, LANES), lambda i: (i, 0))
    pos, meta, counts = pl.pallas_call(
        _rank_kernel,
        out_shape=[jax.ShapeDtypeStruct((N_TOK, LANES), F32), jax.ShapeDtypeStruct((NT, 8, LANES), F32),
                   jax.ShapeDtypeStruct((8, LANES), F32)],
        grid=(NT,),
        in_specs=[tok, _const_spec((TM, TM))],
        out_specs=[tok, pl.BlockSpec((None, 8, LANES), lambda i: (i, 0, 0)),
                   pl.BlockSpec((8, LANES), lambda i: (0, 0))],
        scratch_shapes=[pltpu.VMEM((8, LANES), F32)],
        compiler_params=_cparams(("arbitrary",)),
        name="moe_rank",
    )(route, tri)
    runs, blk = pl.pallas_call(
        _runs_kernel,
        out_shape=[jax.ShapeDtypeStruct((NT, 8, LANES), jnp.int32),
                   jax.ShapeDtypeStruct((256, LANES), jnp.int32)],
        name="moe_runs",
    )(meta, counts)
    runs_flat = runs[:, 0:RUN_FIELDS, 0:N_EXPERTS].reshape(NT, 1, RUN_FIELDS * N_EXPERTS)
    return pos, runs_flat, blk[:N_MOE_BLOCKS, 0], blk[0:1, 1]


def _pack_pairs(x):
    half = x.shape[1] // 2
    lo = pltpu.bitcast(x[:, :half], jnp.uint32)
    hi = pltpu.bitcast(x[:, half:], jnp.uint32)
    return (lo >> 16) | (hi & jnp.uint32(0xFFFF0000))


def _unpack_pairs(w):
    lo = pltpu.bitcast(w << 16, F32)
    hi = pltpu.bitcast(w & jnp.uint32(0xFFFF0000), F32)
    return jnp.concatenate([lo, hi], axis=1).astype(BF16)


def _run_fields(runs_ref, e):
    return runs_ref[0, e], runs_ref[0, N_EXPERTS + e], runs_ref[0, 2 * N_EXPERTS + e]


def _store_rows(lin_ref, packed):
    rows = packed.shape[0]
    for c in range(PK):
        lin_ref[pl.ds(c, rows, stride=PK), :] = packed[:, c * LANES:(c + 1) * LANES]


def _load_rows(lin_ref, rows):
    return jnp.concatenate([lin_ref[pl.ds(c, rows, stride=PK), :] for c in range(PK)], axis=1)


def _lin(ref, row, nrows):
    return ref.at[pl.ds(pl.multiple_of(row * PK, 8), nrows * PK), :]


TAIL_PIECES = (512, 256, 128, 64, 32, 16, 8, 4, 2)


def _dispatch_kernel(runs_ref, prev_runs_ref, pos_ref, h2_ref, xs_ref, buf0, buf1, zero_ref, sem0, sem1, zsem):
    i = pl.program_id(0)
    last = NT - 1

    def tail_copies(act):
        def per_expert(e, carry):
            row = runs_ref[0, 3 * N_EXPERTS + e]
            n = runs_ref[0, 4 * N_EXPERTS + e]
            done = jnp.int32(0)
            for size in TAIL_PIECES:
                @pl.when((n & size) != 0)
                def _(done=done, size=size):
                    act(pltpu.make_async_copy(zero_ref.at[pl.ds(0, size * PK), :], _lin(xs_ref, row + done, size), zsem))
                done = done + (n & size)
            return carry

        lax.fori_loop(0, N_EXPERTS, per_expert, 0)
        used = runs_ref[0, 3 * N_EXPERTS + N_EXPERTS - 1] + runs_ref[0, 4 * N_EXPERTS + N_EXPERTS - 1]

        def spare_block(k, carry):
            act(pltpu.make_async_copy(zero_ref.at[pl.ds(0, MOE_BLK * PK), :],
                                      _lin(xs_ref, used + k * MOE_BLK, MOE_BLK), zsem))
            return carry

        lax.fori_loop(0, N_MOE_BLOCKS - used // MOE_BLK, spare_block, 0)

    start = lambda d: d.start()
    wait = lambda d: d.wait()

    @pl.when(i == 0)
    def _():
        for buf in (buf0, buf1):
            buf[TS * PK:, :] = jnp.zeros((RUN_ROWS * PK, LANES), jnp.uint32)
        zero_ref[...] = jnp.zeros_like(zero_ref)
        tail_copies(start)

    def run_copies(runs, buf, sem, act):
        def chunk_copy(off, dst, k):
            return pltpu.make_async_copy(_lin(buf, off + k * RUN_ROWS, RUN_ROWS),
                                         _lin(xs_ref, dst + k * RUN_ROWS, RUN_ROWS), sem)

        def per_expert(e, carry):
            off, n, dst = _run_fields(runs, e)
            lax.fori_loop(0, (n + RUN_ROWS - 1) // RUN_ROWS, lambda k, c: (act(chunk_copy(off, dst, k)), c)[1], 0)
            return carry

        lax.fori_loop(0, N_EXPERTS, per_expert, 0)

    pos = pos_ref[...]
    col = lax.broadcasted_iota(jnp.int32, (TM, TS), 1).astype(F32)
    sel = jnp.where((col == pos[:, 0:1]) | (col == pos[:, 1:2]), 1.0, 0.0).astype(BF16)
    srt = lax.dot_general(sel, h2_ref[...], (((0,), (0,)), ((), ())), preferred_element_type=F32)
    packed = _pack_pairs(srt)

    for par, (buf, sem, obuf, osem) in enumerate(((buf0, sem0, buf1, sem1), (buf1, sem1, buf0, sem0))):
        @pl.when(i % 2 == par)
        def _(buf=buf, sem=sem, obuf=obuf, osem=osem):
            _store_rows(buf, packed)

            @pl.when(i == 0)
            def _():
                tail_copies(wait)

            @pl.when(i > 0)
            def _():
                run_copies(prev_runs_ref, obuf, osem, wait)

            run_copies(runs_ref, buf, sem, start)

            @pl.when(i == last)
            def _():
                run_copies(runs_ref, buf, sem, wait)


def _dispatch_call(runs_flat, pos, h2):
    runs_spec = lambda shift: pl.BlockSpec((None, 1, RUN_FIELDS * N_EXPERTS), lambda i: (jnp.maximum(i - shift, 0), 0, 0),
                                           memory_space=pltpu.SMEM)
    sorted_buf = pltpu.VMEM(((TS + RUN_ROWS) * PK, LANES), jnp.uint32)
    return pl.pallas_call(
        _dispatch_kernel,
        out_shape=jax.ShapeDtypeStruct((N_SLOTS * PK, LANES), jnp.uint32),
        grid=(NT,),
        in_specs=[runs_spec(0), runs_spec(1),
                  pl.BlockSpec((TM, LANES), lambda i: (i, 0)),
                  pl.BlockSpec((TM, D), lambda i: (i, 0))],
        out_specs=pl.BlockSpec(memory_space=pl.ANY),
        scratch_shapes=[sorted_buf, sorted_buf, pltpu.VMEM((TAIL_PIECES[0] * PK, LANES), jnp.uint32),
                        pltpu.SemaphoreType.DMA, pltpu.SemaphoreType.DMA, pltpu.SemaphoreType.DMA],
        compiler_params=_cparams(("arbitrary",)),
        name="moe_dispatch",
    )(runs_flat, runs_flat, pos, h2)


def _expert_kernel(be_ref, nu_ref, xs_ref, wg_ref, wu_ref, wd_ref, ys_ref, wgb_ref, wub_ref, wdb_ref):
    b = pl.program_id(0)

    @pl.when(jnp.logical_or(b == 0, be_ref[b] != be_ref[jnp.maximum(b - 1, 0)]))
    def _():
        wgb_ref[...] = wg_ref[...].astype(BF16)
        wub_ref[...] = wu_ref[...].astype(BF16)
        wdb_ref[...] = wd_ref[...].astype(BF16)

    @pl.when(b < nu_ref[0])
    def _():
        xb = _unpack_pairs(_load_rows(xs_ref, MOE_BLK))
        g = _dot(xb, wgb_ref[...])
        u = _dot(xb, wub_ref[...])
        hmid = (g * _sigmoid(g)) * u
        y = _dot(hmid.astype(BF16), wdb_ref[...])
        _store_rows(ys_ref, _pack_pairs(y.astype(BF16).astype(F32)))

    @pl.when(b >= nu_ref[0])
    def _():
        ys_ref[...] = jnp.zeros_like(ys_ref)


def _expert_call(blk_e, n_used, xs, wg, wu, wd, l):
    wspec = lambda k, n: pl.BlockSpec((None, None, k, n), lambda b, be, nu: (l, be[b], 0, 0))
    return pl.pallas_call(
        _expert_kernel,
        out_shape=jax.ShapeDtypeStruct((N_SLOTS * PK, LANES), jnp.uint32),
        grid_spec=pltpu.PrefetchScalarGridSpec(
            num_scalar_prefetch=2,
            grid=(N_MOE_BLOCKS,),
            in_specs=[pl.BlockSpec((MOE_BLK * PK, LANES), lambda b, be, nu: (jnp.minimum(b, nu[0] - 1), 0)),
                      wspec(D, EXPERT_HIDDEN), wspec(D, EXPERT_HIDDEN), wspec(EXPERT_HIDDEN, D)],
            out_specs=pl.BlockSpec((MOE_BLK * PK, LANES), lambda b, be, nu: (b, 0)),
            scratch_shapes=[pltpu.VMEM((D, EXPERT_HIDDEN), BF16), pltpu.VMEM((D, EXPERT_HIDDEN), BF16),
                            pltpu.VMEM((EXPERT_HIDDEN, D), BF16)]),
        compiler_params=_cparams(("arbitrary",)),
        name="moe_experts",
    )(blk_e, n_used, xs, wg, wu, wd)


FETCH_ROWS = 64
RUN_PIECES = (32, 16, 8, 4, 2)


def _combine_kernel(runs_ref, next_runs_ref, ys_ref, pos_ref, x_ref, rt_ref, mod_ref, o_ref, buf0, buf1, sem0, sem1,
                    *, n_tiles):
    i = pl.program_id(0)
    last = n_tiles - 1

    def run_copies(runs, buf, sem, act):
        def piece(off, dst, size):
            return pltpu.make_async_copy(_lin(ys_ref, dst, size), _lin(buf, off, size), sem)

        def per_expert(e, carry):
            off, n, dst = _run_fields(runs, e)
            whole = n // FETCH_ROWS

            def chunk(k, c):
                act(piece(off + k * FETCH_ROWS, dst + k * FETCH_ROWS, FETCH_ROWS))
                return c

            lax.fori_loop(0, whole, chunk, 0)
            done = whole * FETCH_ROWS
            for size in RUN_PIECES:
                @pl.when((n & size) != 0)
                def _(done=done, size=size):
                    act(piece(off + done, dst + done, size))
                done = done + (n & size)
            return carry

        lax.fori_loop(0, N_EXPERTS, per_expert, 0)

    start = lambda d: d.start()
    wait = lambda d: d.wait()

    @pl.when(i == 0)
    def _():
        buf0[...] = jnp.zeros_like(buf0)
        buf1[...] = jnp.zeros_like(buf1)
        run_copies(runs_ref, buf0, sem0, start)

    pos = pos_ref[...]
    rt = rt_ref[...]
    col = lax.broadcasted_iota(jnp.int32, (TM, TS), 1).astype(F32)
    pick = (jnp.where(col == pos[:, 0:1], rt[:, TOP_K:TOP_K + 1], 0.0)
            + jnp.where(col == pos[:, 1:2], rt[:, TOP_K + 1:TOP_K + 2], 0.0)).astype(BF16)

    for par, (buf, sem, obuf, osem) in enumerate(((buf0, sem0, buf1, sem1), (buf1, sem1, buf0, sem0))):
        @pl.when(i % 2 == par)
        def _(buf=buf, sem=sem, obuf=obuf, osem=osem):
            @pl.when(i < last)
            def _():
                run_copies(next_runs_ref, obuf, osem, start)

            run_copies(runs_ref, buf, sem, wait)
            ysb = _unpack_pairs(_load_rows(buf, TS))
            o_ref[...] = x_ref[...] + mod_ref[5:6, :] * _dot(pick, ysb)


def _combine_call(runs_flat, ys, pos, x, route, mods_l, n_tiles):
    tok = lambda w: pl.BlockSpec((TM, w), lambda i: (i, 0))
    runs_spec = lambda shift: pl.BlockSpec((None, 1, RUN_FIELDS * N_EXPERTS),
                                           lambda i: (jnp.minimum(i + shift, n_tiles - 1), 0, 0),
                                           memory_space=pltpu.SMEM)
    sorted_buf = pltpu.VMEM((TS * PK, LANES), jnp.uint32)
    return pl.pallas_call(
        functools.partial(_combine_kernel, n_tiles=n_tiles),
        out_shape=jax.ShapeDtypeStruct((n_tiles * TM, D), F32),
        grid=(n_tiles,),
        in_specs=[runs_spec(0), runs_spec(1),
                  pl.BlockSpec(memory_space=pl.ANY),
                  tok(LANES), tok(D), tok(LANES),
                  pl.BlockSpec((None, 6, D), lambda i: (_mod_row(i), 0, 0))],
        out_specs=tok(D),
        scratch_shapes=[sorted_buf, sorted_buf, pltpu.SemaphoreType.DMA, pltpu.SemaphoreType.DMA],
        compiler_params=_cparams(("arbitrary",)),
        name="moe_combine",
    )(runs_flat, runs_flat, ys, pos, x, route, mods_l)


def _rope_tables():
    nf = HEAD_DIM // 4
    inv = ROPE_BASE ** (-jnp.arange(nf, dtype=F32) / nf)
    t = jnp.arange(S)
    row = (t // GRID_W).astype(F32)[:, None] * inv[None, :]
    col = (t % GRID_W).astype(F32)[:, None] * inv[None, :]
    zero = jnp.zeros_like(row)
    cos = jnp.concatenate([jnp.cos(row), jnp.cos(row), jnp.cos(col), jnp.cos(col)], axis=1)
    sa = jnp.concatenate([-jnp.sin(row), zero, -jnp.sin(col), zero], axis=1)
    sb = jnp.concatenate([zero, jnp.sin(row), zero, jnp.sin(col)], axis=1)
    ident = (jnp.ones((TM, HEAD_DIM), F32), jnp.zeros((TM, HEAD_DIM), F32), jnp.zeros((TM, HEAD_DIM), F32))
    return tuple(jnp.tile(jnp.concatenate([a, b], axis=0), (1, LANES // HEAD_DIM))
                 for a, b in zip((cos, sa, sb), ident))


def _fourier_tables():
    s1 = np.arange(FS1)
    ang1 = 2.0 * np.pi * np.outer(s1, s1) / FS1
    w1 = np.concatenate([np.cos(ang1), -np.sin(ang1)], axis=0) / np.sqrt(S)
    k1 = np.arange(FS1)[:, None, None]
    k2 = np.arange(FS2)[None, :, None]
    s2 = np.arange(FS2)[None, None, :]
    ang2 = 2.0 * np.pi * ((k1 + FS1 * k2) * s2 % S) / S
    c2, sn2 = np.cos(ang2), np.sin(ang2)
    ta = np.concatenate([c2, -sn2], axis=1)
    tb = np.concatenate([sn2, c2], axis=1)
    sc = np.arange(C)
    angc = 2.0 * np.pi * np.outer(sc, sc) / C
    wc = np.concatenate([np.cos(angc), -np.sin(angc)], axis=0) / np.sqrt(C)
    return tuple(jnp.asarray(a, F32).astype(BF16) for a in (w1, ta, tb, wc))


def _channel_dft():
    cidx = np.arange(FOURIER_GROUP_W)
    ang = 2.0 * np.pi * np.outer(cidx, cidx) / FOURIER_GROUP_W
    eye = np.eye(FOURIER_W // FOURIER_GROUP_W)
    cw = np.kron(eye, np.cos(ang)) / np.sqrt(FOURIER_GROUP_W)
    sw = np.kron(eye, np.sin(ang)) / np.sqrt(FOURIER_GROUP_W)
    return jnp.asarray(np.concatenate([cw, sw], axis=0), F32)


def _pool_bands():
    t = np.arange(TP)[:, None]
    main, halo = [], []
    for w in POOL_WINDOWS:
        def hit(j):
            return ((j - t >= -(w // 2)) & (j - t <= w // 2 - 1)).astype(np.float32)
        main.append(hit(np.arange(TP)[None, :]))
        halo.append(np.concatenate([hit(np.arange(-HALO, 0)[None, :]),
                                    hit(np.arange(TP, TP + HALO)[None, :])], axis=1))
    return (jnp.asarray(np.stack(main), F32).astype(BF16), jnp.asarray(np.stack(halo), F32).astype(BF16))


def _pool_inv_counts():
    win = np.repeat(np.array(POOL_WINDOWS), POOL_GROUP_W)[None, :]

    def table(pos0, seq_len):
        pos = (pos0 + np.arange(TP))[:, None]
        lo = np.clip(pos - win // 2, 0, seq_len)
        hi = np.clip(pos - win // 2 + win, 0, seq_len)
        return 1.0 / (hi - lo)

    tabs = [table(TP, S), table(0, S), table(S - TP, S), table(0, C)]
    return jnp.asarray(np.stack(tabs), F32)


def _conv_shifts():
    i = np.arange(CONV_WIN)
    return jnp.asarray(np.stack([(i[None, :] == i[:, None] + s) for s in range(1, 8)]), F32).astype(BF16)


def _fold_kernel(a_ref, b_ref, o_ref):
    a, b = a_ref[...], b_ref[...]
    a_hi, b_hi = a.astype(BF16), b.astype(BF16)
    a_lo = (a - a_hi.astype(F32)).astype(BF16)
    b_lo = (b - b_hi.astype(F32)).astype(BF16)
    o_ref[...] = (_dot(a_hi, b_hi) + _dot(a_lo, b_hi) + _dot(a_hi, b_lo)).astype(BF16)


def _fold_fourier_weights(dftw, w_br_fourier):
    nl = w_br_fourier.shape[0]
    return pl.pallas_call(
        _fold_kernel,
        out_shape=jax.ShapeDtypeStruct((nl, 2 * FOURIER_W, D), BF16),
        grid=(nl,),
        in_specs=[pl.BlockSpec((2 * FOURIER_W, FOURIER_W), lambda l: (0, 0)),
                  pl.BlockSpec((None, FOURIER_W, D), lambda l: (l, 0, 0))],
        out_specs=pl.BlockSpec((None, 2 * FOURIER_W, D), lambda l: (l, 0, 0)),
        compiler_params=_cparams(("arbitrary",)),
        name="fold_fourier_proj",
    )(dftw, w_br_fourier)


def _block_diag(blocks):
    n, r, c = blocks.shape
    out = jnp.zeros((n * r, n * c), blocks.dtype)
    for i in range(n):
        out = lax.dynamic_update_slice(out, blocks[i], (i * r, i * c))
    return out


def kernel(x, c, ctx, c_ctx, w_ada, b_ada, g_norm_mix, g_norm_ffn, w_in, g_q, g_k, sink, w_br_attn,
           w_br_fourier, pool_w, pool_scale, w_br_pool, conv_w, conv_b, cn_g, cn_b, w_br_conv, w_gate,
           b_gate, w_out, w_router_grp, b_router_grp, w_router_exp, b_router_exp, w_e_gate, w_e_up,
           w_e_down):
    xs = jnp.concatenate([x.reshape(N_LAT, D), ctx.reshape(N_CTX, D)], axis=0)
    nl = w_ada.shape[0]
    mods = _ada_all(c, c_ctx, w_ada, b_ada).reshape(nl, 8, 6, D)
    rope_tabs = _rope_tables()
    four_tabs = _fourier_tables()
    band_main, band_halo = _pool_bands()
    inv_cnt = _pool_inv_counts()
    shifts = _conv_shifts()
    wf_all = _fold_fourier_weights(_channel_dft(), w_br_fourier)
    bd = jnp.asarray(np.kron(np.eye(LANES // HEAD_DIM), np.ones((HEAD_DIM, HEAD_DIM))), F32).astype(BF16)
    tri = jnp.asarray(np.tril(np.ones((TM, TM)), -1), F32).astype(BF16)
    rpad = jnp.zeros((nl, D, LANES - N_GROUPS - N_EXPERTS), F32)
    w_router = jnp.concatenate([w_router_grp, w_router_exp, rpad], axis=-1)
    r_hi = w_router.astype(BF16)
    r_lo = (w_router - r_hi.astype(F32)).astype(BF16)
    r_b = jnp.concatenate([b_router_grp, b_router_exp, rpad[:, 0, :]], axis=-1).reshape(nl, 1, LANES)
    stacked = tuple(w.astype(BF16) for w in (w_br_attn,)) + (wf_all,) + tuple(
        w.astype(BF16) for w in (w_br_pool, w_br_conv, w_gate, w_out)) + (r_hi, r_lo)

    for l in range(nl):
        mods_l = mods[l]
        gn = g_norm_mix[l].reshape(1, D)
        q, kv, f, p, u = _proj_call(xs, mods_l, gn, w_in, l, rope_tabs,
                                    jnp.tile(g_q[l], 2).reshape(1, LANES),
                                    jnp.tile(g_k[l], 2).reshape(1, LANES), bd)
        a_lat, a_ctx = _attn_call(sink[l], q, kv)
        h_lat, h_ctx = _fourier_call(f, four_tabs)
        zc, cact = _poolconv_call(p, u, band_main, band_halo, inv_cnt, _block_diag(pool_w[l]).astype(BF16),
                                  pool_scale[l].reshape(1, POOL_W), shifts, conv_w[l], conv_b[l].reshape(1, CONV_W),
                                  cn_g[l].reshape(1, CONV_W), cn_b[l].reshape(1, CONV_W))
        small = (b_gate[l].reshape(1, 4 * D), g_norm_ffn[l].reshape(1, D), r_b[l])
        xs, h2, route = _mix_call(xs, mods_l, gn, a_lat, a_ctx, h_lat, h_ctx, zc, cact, l, stacked, small)
        pos, runs_flat, blk_e, n_used = _plan_call(route, tri)
        slots = _dispatch_call(runs_flat, pos, h2)
        ys = _expert_call(blk_e, n_used, slots, w_e_gate, w_e_up, w_e_down, l)
        xs = _combine_call(runs_flat, ys, pos, xs, route, mods_l, LAT_TILES if l == nl - 1 else NT)
    return xs.reshape(B, S, D)
```

```python
import functools

import numpy as np
import jax
import jax.numpy as jnp
from jax import lax
from jax.experimental import pallas as pl
from jax.experimental.pallas import tpu as pltpu

F32 = jnp.float32
BF16 = jnp.bfloat16

D = 1024
B = 2
S = 8192
C = 256
GRID_W = 64
HEAD_DIM = 64
N_Q_HEADS = 8
N_KV_HEADS = 2
GQA = N_Q_HEADS // N_KV_HEADS
WINDOW = 128
ATTN_BLK = 128
ATTN_QB = 4
ROPE_BASE = 10000.0
Q_W = 512
KV_W = 128
FOURIER_W = 640
FOURIER_GROUP_W = 160
POOL_W = 640
POOL_GROUP_W = 160
POOL_WINDOWS = (2, 4, 8, 16)
CONV_W = 512
CONV_K = 31
PROJ_W = 3072
N_GROUPS = 4
EPG = 8
N_EXPERTS = 32
TOP_K = 2
EXPERT_HIDDEN = 512
MOE_BLK = 512
EPS = 1e-6
NEG_INF = -1e30
LOG2E = 1.4426950408889634

N_LAT = B * S
N_CTX = B * C
N_TOK = N_LAT + N_CTX
TM = 512
NT = N_TOK // TM
LAT_TILES = N_LAT // TM
TILES_PER_BATCH = S // TM
TP = 256
NTP = N_TOK // TP
HALO = 16
CONV_WIN = TP // 2 + 2 * HALO
N_ASSIGN = N_TOK * TOP_K
RUN_ROWS = 48
RUN_FIELDS = 5
PK = D // 2 // 128
TS = 1152
N_MOE_BLOCKS = (N_ASSIGN + NT * N_EXPERTS + N_EXPERTS * (RUN_ROWS - 1 + MOE_BLK - 1)) // MOE_BLK
N_SLOTS = N_MOE_BLOCKS * MOE_BLK
FS1 = 64
FS2 = 128
F1_COLS = FS2 * FOURIER_W
F1_CW = 8192
F2_K1 = 8
LANES = 128
VMEM_LIMIT = 56 * 1024 * 1024


def _cparams(sem, vmem=VMEM_LIMIT):
    return pltpu.CompilerParams(dimension_semantics=sem, vmem_limit_bytes=vmem)


def _const_spec(shape):
    nd = len(shape)
    return pl.BlockSpec(shape, lambda *_: (0,) * nd, pipeline_mode=pl.Buffered(1))


def _dot(a, b):
    return jnp.dot(a, b, preferred_element_type=F32)


def _modulate(x, g, shift, scale):
    y = x * lax.rsqrt(jnp.mean(x * x, axis=-1, keepdims=True) + EPS)
    return (y * g) * (1.0 + scale) + shift


def _sigmoid(x):
    return 1.0 / (1.0 + jnp.exp(-x))


def _ada_kernel(ct_ref, w_ref, b_ref, o_ref):
    ct = ct_ref[...]
    s = ct * _sigmoid(ct)
    w = w_ref[...]
    rows = [jnp.sum(w * s[:, r:r + 1], axis=0, keepdims=True) for r in range(3)]
    rows.append(jnp.zeros((5, w.shape[1]), F32))
    o_ref[...] = jnp.concatenate(rows, axis=0) + b_ref[...]


def _ada_all(c, c_ctx, w_ada, b_ada):
    ct = jnp.concatenate([c, c_ctx[None, :], jnp.zeros((5, D), F32)], axis=0).T
    cols = 1536
    nl = w_ada.shape[0]
    return pl.pallas_call(
        _ada_kernel,
        out_shape=jax.ShapeDtypeStruct((nl, 8, 6 * D), F32),
        grid=(nl, 6 * D // cols),
        in_specs=[pl.BlockSpec((D, 8), lambda l, j: (0, 0)),
                  pl.BlockSpec((None, D, cols), lambda l, j: (l, 0, j)),
                  pl.BlockSpec((None, 1, cols), lambda l, j: (l, 0, j))],
        out_specs=pl.BlockSpec((None, 8, cols), lambda l, j: (l, 0, j)),
        compiler_params=_cparams(("arbitrary", "arbitrary")),
        name="adaln",
    )(ct, w_ada, b_ada.reshape(nl, 1, 6 * D))


def _head_rms(t, g128, bd):
    outs = []
    for j in range(t.shape[1] // LANES):
        blk = t[:, j * LANES:(j + 1) * LANES]
        ss = _dot((blk * blk).astype(BF16), bd)
        outs.append(blk * lax.rsqrt(ss * (1.0 / HEAD_DIM) + EPS) * g128)
    return outs


def _rope(blocks, cos, sa, sb):
    outs = []
    for blk in blocks:
        up = pltpu.roll(blk, LANES - 16, 1)
        dn = pltpu.roll(blk, 16, 1)
        outs.append(blk * cos + up * sa + dn * sb)
    return outs


def _proj_kernel(x_ref, mod_ref, gn_ref, w_ref, cos_ref, sa_ref, sb_ref, gq_ref, gk_ref, bd_ref,
                 q_ref, kv_ref, f_ref, p_ref, u_ref, wbf_ref):
    @pl.when(pl.program_id(0) == 0)
    def _():
        wbf_ref[...] = w_ref[...].astype(BF16)

    m = mod_ref[...]
    hb = _modulate(x_ref[...], gn_ref[...], m[0:1], m[1:2]).astype(BF16)
    cos, sa, sb, bd = cos_ref[...], sa_ref[...], sb_ref[...], bd_ref[...]
    o_kv, o_f, o_a = Q_W, Q_W + 2 * KV_W, Q_W + 2 * KV_W + FOURIER_W + POOL_W
    qkv = _dot(hb, wbf_ref[:, 0:o_f])
    fp = _dot(hb, wbf_ref[:, o_f:o_a])
    q = _rope(_head_rms(qkv[:, 0:Q_W], gq_ref[...], bd), cos, sa, sb)
    q_ref[...] = (jnp.concatenate(q, axis=1) * (LOG2E * HEAD_DIM ** -0.5)).astype(BF16)
    k = _rope(_head_rms(qkv[:, o_kv:o_kv + KV_W], gk_ref[...], bd), cos, sa, sb)
    kv_ref[:, 0:KV_W] = k[0].astype(BF16)
    kv_ref[:, KV_W:2 * KV_W] = qkv[:, o_kv + KV_W:o_f].astype(BF16)
    ag = _dot(hb, wbf_ref[:, o_a:PROJ_W])
    f_ref[...] = fp[:, 0:FOURIER_W].astype(BF16)
    p_ref[...] = fp[:, FOURIER_W:].astype(BF16)
    u_ref[...] = (ag[:, 0:CONV_W] * _sigmoid(ag[:, CONV_W:])).astype(BF16)


def _mod_row(i):
    return jnp.minimum(i // TILES_PER_BATCH, 2)


def _layer_spec(shape, l):
    nd = len(shape)
    return pl.BlockSpec((None,) + tuple(shape), lambda *_: (l,) + (0,) * nd, pipeline_mode=pl.Buffered(1))


def _proj_call(x, mods_l, gn, w_in, l, rope_tabs, gq128, gk128, bd):
    cos, sa, sb = rope_tabs
    tok = lambda w: pl.BlockSpec((TM, w), lambda i: (i, 0))
    rope_spec = pl.BlockSpec((TM, LANES), lambda i: (jnp.where(i < LAT_TILES, i % TILES_PER_BATCH,
                                                               TILES_PER_BATCH), 0))
    widths = (Q_W, 2 * KV_W, FOURIER_W, POOL_W, CONV_W)
    return pl.pallas_call(
        _proj_kernel,
        out_shape=[jax.ShapeDtypeStruct((N_TOK, w), BF16) for w in widths],
        grid=(NT,),
        in_specs=[tok(D),
                  pl.BlockSpec((None, 6, D), lambda i: (_mod_row(i), 0, 0)),
                  _const_spec((1, D)),
                  _layer_spec((D, PROJ_W), l),
                  rope_spec, rope_spec, rope_spec,
                  _const_spec((1, LANES)), _const_spec((1, LANES)),
                  _const_spec((LANES, LANES))],
        out_specs=[tok(w) for w in widths],
        scratch_shapes=[pltpu.VMEM((D, PROJ_W), BF16)],
        compiler_params=_cparams(("arbitrary",)),
        name="proj",
    )(x, mods_l, gn, w_in, cos, sa, sb, gq128, gk128, bd)


def _attend_many(jobs, sink_ref):
    lane = lax.broadcasted_iota(jnp.int32, (ATTN_BLK, LANES), 1)
    chains = []
    for q, kv_blocks, biases in jobs:
        for j in range(N_KV_HEADS):
            ks = slice(j * HEAD_DIM, (j + 1) * HEAD_DIM)
            vs = slice(KV_W + j * HEAD_DIM, KV_W + (j + 1) * HEAD_DIM)
            kj = jnp.concatenate([blk[:, ks] for blk in kv_blocks], axis=0)
            vj = jnp.concatenate([blk[:, vs] for blk in kv_blocks], axis=0)
            vaug = jnp.concatenate([vj, jnp.ones_like(vj)], axis=1)
            qs = jnp.concatenate([q[:, (j * GQA + g) * HEAD_DIM:(j * GQA + g + 1) * HEAD_DIM]
                                  for g in range(GQA)], axis=0)
            s = lax.dot_general(qs, kj, (((1,), (1,)), ((), ())), preferred_element_type=F32)
            chains.append((j, s, vaug, kv_blocks, biases))
    soft = []
    for j, s, vaug, kv_blocks, biases in chains:
        probs, sink_terms = [], []
        for g in range(GQA):
            sg = s[g * ATTN_BLK:(g + 1) * ATTN_BLK]
            pieces, col = [], 0
            for blk, bias in zip(kv_blocks, biases):
                piece = sg[:, col:col + blk.shape[0]]
                pieces.append(piece if bias is None else piece + bias)
                col += blk.shape[0]
            sg = jnp.concatenate(pieces, axis=1)
            sk = sink_ref[j * GQA + g] * LOG2E
            mx = jnp.maximum(jnp.max(sg, axis=-1, keepdims=True), sk)
            probs.append(jnp.exp2(sg - mx).astype(BF16))
            sink_terms.append(jnp.exp2(sk - mx))
        soft.append((jnp.concatenate(probs, axis=0), vaug, sink_terms))
    heads = []
    for p, vaug, sink_terms in soft:
        o = _dot(p, vaug)
        for g in range(GQA):
            og = o[g * ATTN_BLK:(g + 1) * ATTN_BLK]
            heads.append(og / (og[:, HEAD_DIM:HEAD_DIM + 1] + sink_terms[g]))
    outs = []
    for n in range(len(jobs)):
        hs = heads[n * N_Q_HEADS:(n + 1) * N_Q_HEADS]
        tiles = [jnp.where(lane < HEAD_DIM, hs[2 * t], pltpu.roll(hs[2 * t + 1], HEAD_DIM, 1))
                 for t in range(N_Q_HEADS // 2)]
        outs.append(jnp.concatenate(tiles, axis=1).astype(BF16))
    return outs


def _attn_latent_kernel(sink_ref, q_ref, prev_ref, cur_ref, next_ref, ctx_ref, o_ref):
    n = pl.program_id(1)
    r = lax.broadcasted_iota(jnp.int32, (ATTN_BLK, ATTN_BLK), 0)
    jj = lax.broadcasted_iota(jnp.int32, (ATTN_BLK, ATTN_BLK), 1)
    far = jnp.int32(2 * ATTN_BLK)
    off_prev = jnp.where(n > 0, 0, far)
    off_next = jnp.where(n < S // (ATTN_QB * ATTN_BLK) - 1, 0, far)
    prev_ok = jnp.where(jj - r >= 0, 0.0, NEG_INF)
    next_ok = jnp.where(r - jj >= 0, 0.0, NEG_INF)
    prev_edge = jnp.where(jj - r - off_prev >= 0, 0.0, NEG_INF)
    next_edge = jnp.where(r - jj - off_next >= 0, 0.0, NEG_INF)
    ctx = ctx_ref[...]
    rows = lambda b: slice(b * ATTN_BLK, (b + 1) * ATTN_BLK)
    blocks = [prev_ref[...]] + [cur_ref[rows(b), :] for b in range(ATTN_QB)] + [next_ref[...]]
    jobs = [(q_ref[rows(b), :], [ctx] + blocks[b:b + 3],
             [None, prev_edge if b == 0 else prev_ok, None, next_edge if b == ATTN_QB - 1 else next_ok])
            for b in range(ATTN_QB)]
    for b, out in enumerate(_attend_many(jobs, sink_ref)):
        o_ref[rows(b), :] = out


def _attn_context_kernel(sink_ref, q_ref, ctx_ref, o_ref):
    o_ref[...] = _attend_many([(q_ref[...], [ctx_ref[...]], [None])], sink_ref)[0]


def _attn_call(sink_l, q, kv):
    nb = S // ATTN_BLK
    nq = nb // ATTN_QB
    smem = pl.BlockSpec(memory_space=pltpu.SMEM)
    pair = lambda w: pl.BlockSpec((ATTN_QB * ATTN_BLK, w), lambda b, n: (b * nq + n, 0))
    prev = pl.BlockSpec((ATTN_BLK, 2 * KV_W), lambda b, n: (b * nb + jnp.maximum(ATTN_QB * n - 1, 0), 0))
    nxt = pl.BlockSpec((ATTN_BLK, 2 * KV_W),
                       lambda b, n: (b * nb + jnp.minimum(ATTN_QB * (n + 1), nb - 1), 0))
    ctxs = pl.BlockSpec((C, 2 * KV_W), lambda b, n: (N_LAT // C + b, 0))
    lat = pl.pallas_call(
        _attn_latent_kernel,
        out_shape=jax.ShapeDtypeStruct((N_LAT, Q_W), BF16),
        grid=(B, nq),
        in_specs=[smem, pair(Q_W), prev, pair(2 * KV_W), nxt, ctxs],
        out_specs=pair(Q_W),
        compiler_params=_cparams(("parallel", "parallel")),
        name="attn_latent",
    )(sink_l, q, kv, kv, kv, kv)
    ncb = C // ATTN_BLK
    base = N_LAT // ATTN_BLK
    ctx = pl.pallas_call(
        _attn_context_kernel,
        out_shape=jax.ShapeDtypeStruct((N_CTX, Q_W), BF16),
        grid=(B, ncb),
        in_specs=[smem, pl.BlockSpec((ATTN_BLK, Q_W), lambda b, n: (base + b * ncb + n, 0)), ctxs],
        out_specs=pl.BlockSpec((ATTN_BLK, Q_W), lambda b, n: (b * ncb + n, 0)),
        compiler_params=_cparams(("parallel", "parallel")),
        name="attn_context",
    )(sink_l, q, kv)
    return lat, ctx


def _f1_kernel(w_ref, f_ref, re_ref, im_ref):
    res = _dot(w_ref[...], f_ref[...])
    re_ref[...] = res[:FS1].astype(BF16)
    im_ref[...] = res[FS1:].astype(BF16)


def _f2_kernel(ta_ref, tb_ref, re_ref, im_ref, o_ref):
    for i in range(F2_K1):
        res = _dot(ta_ref[i], re_ref[i]) + _dot(tb_ref[i], im_ref[i])
        o_ref[i, :, 0:FOURIER_W] = res[:FS2].astype(BF16)
        o_ref[i, :, FOURIER_W:2 * FOURIER_W] = res[FS2:].astype(BF16)


def _fc_kernel(w_ref, f_ref, o_ref):
    res = _dot(w_ref[...], f_ref[...])
    o_ref[:, 0:FOURIER_W] = res[:C].astype(BF16)
    o_ref[:, FOURIER_W:2 * FOURIER_W] = res[C:].astype(BF16)


def _fourier_stage1_call(f, tabs):
    w1 = tabs[0]
    f2d = f.reshape(N_TOK // FS2, F1_COLS)
    nchunk = F1_COLS // F1_CW
    a_re, a_im = pl.pallas_call(
        _f1_kernel,
        out_shape=[jax.ShapeDtypeStruct((B * FS1, F1_COLS), BF16)] * 2,
        grid=(B, nchunk),
        in_specs=[_const_spec((2 * FS1, FS1)),
                  pl.BlockSpec((FS1, F1_CW), lambda b, j: (b, j))],
        out_specs=[pl.BlockSpec((FS1, F1_CW), lambda b, j: (b, j))] * 2,
        compiler_params=_cparams(("parallel", "parallel")),
        name="fourier_stage1",
    )(w1, f2d)
    return a_re.reshape(B * FS1, FS2, FOURIER_W), a_im.reshape(B * FS1, FS2, FOURIER_W)


def _fourier_stage2_call(a_re, a_im, f, tabs):
    _, ta, tb, wc = tabs
    nk = FS1 // F2_K1
    aspec = pl.BlockSpec((F2_K1, FS2, FOURIER_W), lambda b, k1: (b * nk + k1, 0, 0))
    tspec = pl.BlockSpec((F2_K1, 2 * FS2, FS2), lambda b, k1: (k1, 0, 0))
    h_t = pl.pallas_call(
        _f2_kernel,
        out_shape=jax.ShapeDtypeStruct((B, FS1, FS2, 2 * FOURIER_W), BF16),
        grid=(B, nk),
        in_specs=[tspec, tspec, aspec, aspec],
        out_specs=pl.BlockSpec((None, F2_K1, FS2, 2 * FOURIER_W), lambda b, k1: (b, k1, 0, 0)),
        compiler_params=_cparams(("parallel", "parallel")),
        name="fourier_stage2",
    )(ta, tb, a_re, a_im)
    h_lat = jnp.transpose(h_t, (0, 2, 1, 3)).reshape(N_LAT, 2 * FOURIER_W)
    h_ctx = pl.pallas_call(
        _fc_kernel,
        out_shape=jax.ShapeDtypeStruct((N_CTX, 2 * FOURIER_W), BF16),
        grid=(B,),
        in_specs=[_const_spec((2 * C, C)),
                  pl.BlockSpec((C, FOURIER_W), lambda b: (N_LAT // C + b, 0))],
        out_specs=pl.BlockSpec((C, 2 * FOURIER_W), lambda b: (b, 0)),
        compiler_params=_cparams(("parallel",)),
        name="fourier_context",
    )(wc, f)
    return h_lat, h_ctx


def _poolconv_kernel(pc_ref, pp_ref, pn_ref, uc_ref, up_ref, un_ref, bm_ref, bh_ref, ic_ref, pw_ref, ps_ref,
                     sh_ref, cw_ref, cb_ref, cg_ref, cnb_ref, z_ref, a_ref, win0_ref, win1_ref, cv_ref):
    t = pl.program_id(0)
    lat_tiles = N_LAT // TP
    per_seq = S // TP
    is_ctx = t >= lat_tiles
    first = jnp.logical_or(t % per_seq == 0, is_ctx)
    last = jnp.logical_or(t % per_seq == per_seq - 1, is_ctx)

    keep_prev = jnp.where(first, 0.0, 1.0)
    keep_next = jnp.where(last, 0.0, 1.0)

    ub = jnp.concatenate([(up_ref[...].astype(F32) * keep_prev).astype(BF16), uc_ref[...],
                          (un_ref[...].astype(F32) * keep_next).astype(BF16)], axis=0)
    off = HALO - CONV_K // 2
    half_rows = TP // 2
    wins = (win0_ref, win1_ref)
    for hf, win_ref in enumerate(wins):
        window = ub[hf * half_rows:hf * half_rows + CONV_WIN]
        win_ref[0] = window.astype(F32)
        for s in range(1, 8):
            win_ref[s] = _dot(sh_ref[s - 1], window)

    pcur = pc_ref[...]
    halo = jnp.concatenate([pp_ref[...].astype(F32) * keep_prev,
                            pn_ref[...].astype(F32) * keep_next], axis=0).astype(BF16)
    sums = []
    for gi in range(len(POOL_WINDOWS)):
        cs = slice(gi * LANES, (gi + 2) * LANES)
        sums.append(_dot(bm_ref[gi], pcur[:, cs]) + _dot(bh_ref[gi], halo[:, cs]))
    lane_t = lax.broadcasted_iota(jnp.int32, (TP, LANES), 1)
    tiles = [sums[0][:, :LANES]]
    for gi in range(1, len(POOL_WINDOWS)):
        split = gi * POOL_GROUP_W - gi * LANES
        tiles.append(jnp.where(lane_t < split, sums[gi - 1][:, LANES:], sums[gi][:, :LANES]))
    tiles.append(sums[-1][:, LANES:])
    zsum = jnp.concatenate(tiles, axis=1)
    z = zsum * ic_ref[...] - pcur.astype(F32)
    z_ref[...] = (_dot(z.astype(BF16), pw_ref[...]) * ps_ref[...]).astype(BF16)

    for hf, win_ref in enumerate(wins):
        base = hf * half_rows
        for cb in range(CONV_W // LANES):
            cs = slice(cb * LANES, (cb + 1) * LANES)
            acc = jnp.zeros((half_rows, LANES), F32) + cb_ref[:, cs]
            for j in range(CONV_K):
                s, m = (off + j) % 8, (off + j) // 8
                acc = acc + win_ref[s, 8 * m:8 * m + half_rows, cs] * cw_ref[j:j + 1, cs]
            cv_ref[base:base + half_rows, cs] = acc
    cv = cv_ref[...]
    mu = jnp.mean(cv, axis=-1, keepdims=True)
    var = jnp.mean(jnp.square(cv - mu), axis=-1, keepdims=True)
    un = (cv - mu) * lax.rsqrt(var + EPS) * cg_ref[...] + cnb_ref[...]
    a_ref[...] = (un * _sigmoid(un)).astype(BF16)


def _poolconv_call(p, u, band_main, band_halo, inv_cnt, pw_bd, pool_scale, shifts, conv_w, conv_b, cn_g, cn_b):
    nh = TP // HALO
    last_h = N_TOK // HALO - 1
    cur = lambda w: pl.BlockSpec((TP, w), lambda t: (t, 0))
    prv = lambda w: pl.BlockSpec((HALO, w), lambda t: (jnp.maximum(t * nh - 1, 0), 0))
    nxt = lambda w: pl.BlockSpec((HALO, w), lambda t: (jnp.minimum((t + 1) * nh, last_h), 0))
    per_seq = S // TP

    def kind(t):
        return jnp.where(t >= N_LAT // TP, 3, jnp.where(t % per_seq == 0, 1, jnp.where(t % per_seq == per_seq - 1, 2, 0)))

    return pl.pallas_call(
        _poolconv_kernel,
        out_shape=[jax.ShapeDtypeStruct((N_TOK, POOL_W), BF16),
                   jax.ShapeDtypeStruct((N_TOK, CONV_W), BF16)],
        grid=(NTP,),
        in_specs=[cur(POOL_W), prv(POOL_W), nxt(POOL_W), cur(CONV_W), prv(CONV_W), nxt(CONV_W),
                  _const_spec((4, TP, TP)), _const_spec((4, TP, 2 * HALO)),
                  pl.BlockSpec((None, TP, POOL_W), lambda t: (kind(t), 0, 0)),
                  _const_spec((POOL_W, POOL_W)), _const_spec((1, POOL_W)),
                  _const_spec((7, CONV_WIN, CONV_WIN)),
                  _const_spec((CONV_K, CONV_W)), _const_spec((1, CONV_W)),
                  _const_spec((1, CONV_W)), _const_spec((1, CONV_W))],
        out_specs=[cur(POOL_W), cur(CONV_W)],
        scratch_shapes=[pltpu.VMEM((8, CONV_WIN, CONV_W), F32), pltpu.VMEM((8, CONV_WIN, CONV_W), F32),
                        pltpu.VMEM((TP, CONV_W), F32)],
        compiler_params=_cparams(("parallel",)),
        name="pool_conv",
    )(p, p, p, u, u, u, band_main, band_halo, inv_cnt, pw_bd, pool_scale, shifts, conv_w, conv_b, cn_g, cn_b)


def _route(logits):
    lane = lax.broadcasted_iota(jnp.int32, logits.shape, 1)
    big = jnp.int32(LANES)
    lg = jnp.where(lane < N_GROUPS, logits, NEG_INF)
    mg = jnp.max(lg, axis=-1, keepdims=True)
    grp = jnp.min(jnp.where(lg == mg, lane, big), axis=-1, keepdims=True)
    p_grp = 1.0 / jnp.sum(jnp.exp(lg - mg), axis=-1, keepdims=True)
    lo = N_GROUPS + grp * EPG
    le = jnp.where((lane >= lo) & (lane < lo + EPG), logits, NEG_INF)
    m1 = jnp.max(le, axis=-1, keepdims=True)
    i1 = jnp.min(jnp.where(le == m1, lane, big), axis=-1, keepdims=True)
    le2 = jnp.where(lane == i1, NEG_INF, le)
    m2 = jnp.max(le2, axis=-1, keepdims=True)
    i2 = jnp.min(jnp.where(le2 == m2, lane, big), axis=-1, keepdims=True)
    r = jnp.exp(m2 - m1)
    w1 = p_grp / (1.0 + r)
    w2 = p_grp * r / (1.0 + r)
    e1 = (i1 - N_GROUPS).astype(F32)
    e2 = (i2 - N_GROUPS).astype(F32)
    return jnp.where(lane == 0, e1, jnp.where(lane == 1, e2, jnp.where(lane == 2, w1,
                     jnp.where(lane == 3, w2, 0.0))))


def _mix_kernel(x_ref, mod_ref, gn_ref, al_ref, ac_ref, hl_ref, hc_ref, z_ref, cv_ref,
                wa_ref, wf_ref, wp_ref, wc_ref, wg_ref, bg_ref, wo_ref, gf_ref, rh_ref, rl_ref, rb_ref,
                xo_ref, h2_ref, rt_ref):
    is_ctx = pl.program_id(0) >= LAT_TILES
    m = mod_ref[...]
    x = x_ref[...]
    hb = _modulate(x, gn_ref[...], m[0:1], m[1:2]).astype(BF16)
    attn = jnp.where(is_ctx, ac_ref[...], al_ref[...])
    four = jnp.where(is_ctx, hc_ref[...], hl_ref[...])
    branches = ((attn, wa_ref), (four, wf_ref), (z_ref[...], wp_ref), (cv_ref[...], wc_ref))
    acc = None
    for bi, (inp, w_ref) in enumerate(branches):
        cs = slice(bi * D, (bi + 1) * D)
        gate = _sigmoid(_dot(hb, wg_ref[:, cs]) + bg_ref[:, cs])
        term = gate * _dot(inp, w_ref[...])
        acc = term if acc is None else acc + term
    x_new = x + m[2:3] * _dot(acc.astype(BF16), wo_ref[...])
    xo_ref[...] = x_new
    h2 = _modulate(x_new, gf_ref[...], m[3:4], m[4:5])
    hi = h2.astype(BF16)
    h2_ref[...] = hi
    lo = (h2 - hi.astype(F32)).astype(BF16)
    logits = _dot(hi, rh_ref[...]) + _dot(lo, rh_ref[...]) + _dot(hi, rl_ref[...]) + rb_ref[...]
    rt_ref[...] = _route(logits)


def _mix_call(x, mods_l, gn, a_lat, a_ctx, h_lat, h_ctx, zc, cact, l, stacked, small):
    tok = lambda w: pl.BlockSpec((TM, w), lambda i: (i, 0))
    lat = lambda w: pl.BlockSpec((TM, w), lambda i: (jnp.minimum(i, LAT_TILES - 1), 0))
    wa, wf, wp, wc, wg, wo, rh, rl = stacked
    bg, gf, rb = small
    in_specs = [tok(D), pl.BlockSpec((None, 6, D), lambda i: (_mod_row(i), 0, 0)), _const_spec((1, D)),
                lat(Q_W), _const_spec((N_CTX, Q_W)),
                lat(2 * FOURIER_W), _const_spec((N_CTX, 2 * FOURIER_W)),
                tok(POOL_W), tok(CONV_W)]
    in_specs += [_layer_spec(w.shape[1:], l) for w in (wa, wf, wp, wc, wg)]
    in_specs += [_const_spec(bg.shape), _layer_spec(wo.shape[1:], l), _const_spec(gf.shape),
                 _layer_spec(rh.shape[1:], l), _layer_spec(rl.shape[1:], l), _const_spec(rb.shape)]
    return pl.pallas_call(
        _mix_kernel,
        out_shape=[jax.ShapeDtypeStruct((N_TOK, D), F32), jax.ShapeDtypeStruct((N_TOK, D), BF16),
                   jax.ShapeDtypeStruct((N_TOK, LANES), F32)],
        grid=(NT,),
        in_specs=in_specs,
        out_specs=[tok(D), tok(D), tok(LANES)],
        compiler_params=_cparams(("parallel",)),
        name="mix",
    )(x, mods_l, gn, a_lat, a_ctx, h_lat, h_ctx, zc, cact, wa, wf, wp, wc, wg, bg, wo, gf, rh, rl, rb)


def _onehots(route):
    lane = lax.broadcasted_iota(jnp.int32, route.shape, 1)
    e1 = route[:, 0:1].astype(jnp.int32)
    e2 = route[:, 1:2].astype(jnp.int32)
    return (lane == e1).astype(F32), (lane == e2).astype(F32)


def _lane_cumsum(row):
    lane = lax.broadcasted_iota(jnp.int32, row.shape, 1)
    sh = 1
    while sh < N_EXPERTS:
        row = row + jnp.where(lane >= sh, pltpu.roll(row, sh, 1), 0.0)
        sh *= 2
    return row


def _rank_kernel(rt_ref, tri_ref, pos_ref, meta_ref, cnt_ref, carry_ref):
    i = pl.program_id(0)

    @pl.when(i == 0)
    def _():
        carry_ref[...] = jnp.zeros_like(carry_ref)

    oh1, oh2 = _onehots(rt_ref[...])
    both = oh1 + oh2
    carry = carry_ref[0:1, :]
    tile_cnt = jnp.sum(both, axis=0, keepdims=True)
    tile_cnt = tile_cnt + (tile_cnt - 2.0 * jnp.floor(tile_cnt * 0.5))
    tile_off = _lane_cumsum(tile_cnt) - tile_cnt
    where = _dot(tri_ref[...], both.astype(BF16)) + tile_off
    p1 = jnp.sum(oh1 * where, axis=-1, keepdims=True)
    p2 = jnp.sum(oh2 * where, axis=-1, keepdims=True)
    lane = lax.broadcasted_iota(jnp.int32, both.shape, 1)
    pos_ref[...] = jnp.where(lane == 0, p1, jnp.where(lane == 1, p2, 0.0))
    row = lax.broadcasted_iota(jnp.int32, meta_ref.shape, 0)
    meta_ref[...] = jnp.where(row == 0, tile_off, jnp.where(row == 1, tile_cnt, jnp.where(row == 2, carry, 0.0)))
    total = carry + tile_cnt
    carry_ref[...] = jnp.broadcast_to(total, carry_ref.shape)
    cnt_ref[...] = jnp.broadcast_to(total, cnt_ref.shape)


def _runs_kernel(meta_ref, cnt_ref, runs_ref, be_ref):
    lane = lax.broadcasted_iota(jnp.int32, (1, LANES), 1)
    counts = cnt_ref[0:1, :]
    padded = jnp.floor((counts + (RUN_ROWS - 1 + MOE_BLK - 1)) * (1.0 / MOE_BLK)) * MOE_BLK
    padded = jnp.where(lane < N_EXPERTS, padded, 0.0)
    ends = _lane_cumsum(padded)
    starts = ends - padded
    for t in range(NT):
        m = meta_ref[t]
        row = lax.broadcasted_iota(jnp.int32, m.shape, 0)
        m = jnp.where(row == 2, m + starts, m)
        m = jnp.where(row == 3, starts + counts, jnp.where(row == 4, padded - counts, m))
        runs_ref[t] = m.astype(jnp.int32)
    blk = lax.broadcasted_iota(jnp.int32, be_ref.shape, 0).astype(F32) * MOE_BLK
    lane_b = lax.broadcasted_iota(jnp.int32, be_ref.shape, 1)
    done = jnp.where((ends <= blk) & (lane_b < N_EXPERTS), 1.0, 0.0)
    be = jnp.minimum(jnp.sum(done, axis=-1, keepdims=True), N_EXPERTS - 1.0)
    nblk = jnp.max(jnp.where(lane_b == N_EXPERTS - 1, ends, 0.0), axis=-1, keepdims=True) * (1.0 / MOE_BLK)
    be_ref[...] = jnp.where(lane_b == 0, be, jnp.where(lane_b == 1, nblk, 0.0)).astype(jnp.int32)


def _plan_call(route, tri):
    tok = pl.BlockSpec((TM, LANES), lambda i: (i, 0))
    pos, meta, counts = pl.pallas_call(
        _rank_kernel,
        out_shape=[jax.ShapeDtypeStruct((N_TOK, LANES), F32), jax.ShapeDtypeStruct((NT, 8, LANES), F32),
                   jax.ShapeDtypeStruct((8, LANES), F32)],
        grid=(NT,),
        in_specs=[tok, _const_spec((TM, TM))],
        out_specs=[tok, pl.BlockSpec((None, 8, LANES), lambda i: (i, 0, 0)),
                   pl.BlockSpec((8, LANES), lambda i: (0, 0))],
        scratch_shapes=[pltpu.VMEM((8, LANES), F32)],
        compiler_params=_cparams(("arbitrary",)),
        name="moe_rank",
    )(route, tri)
    runs, blk = pl.pallas_call(
        _runs_kernel,
        out_shape=[jax.ShapeDtypeStruct((NT, 8, LANES), jnp.int32),
                   jax.ShapeDtypeStruct((256, LANES), jnp.int32)],
        name="moe_runs",
    )(meta, counts)
    runs_flat = runs[:, 0:RUN_FIELDS, 0:N_EXPERTS].reshape(NT, 1, RUN_FIELDS * N_EXPERTS)
    return pos, runs_flat, blk[:N_MOE_BLOCKS, 0], blk[0:1, 1]


def _pack_pairs(x):
    half = x.shape[1] // 2
    lo = pltpu.bitcast(x[:, :half], jnp.uint32)
    hi = pltpu.bitcast(x[:, half:], jnp.uint32)
    return (lo >> 16) | (hi & jnp.uint32(0xFFFF0000))


def _unpack_pairs(w):
    lo = pltpu.bitcast(w << 16, F32)
    hi = pltpu.bitcast(w & jnp.uint32(0xFFFF0000), F32)
    return jnp.concatenate([lo, hi], axis=1).astype(BF16)


def _run_fields(runs_ref, e):
    return runs_ref[0, e], runs_ref[0, N_EXPERTS + e], runs_ref[0, 2 * N_EXPERTS + e]


def _store_rows(lin_ref, packed):
    rows = packed.shape[0]
    for c in range(PK):
        lin_ref[pl.ds(c, rows, stride=PK), :] = packed[:, c * LANES:(c + 1) * LANES]


def _load_rows(lin_ref, rows):
    return jnp.concatenate([lin_ref[pl.ds(c, rows, stride=PK), :] for c in range(PK)], axis=1)


def _lin(ref, row, nrows):
    return ref.at[pl.ds(pl.multiple_of(row * PK, 8), nrows * PK), :]


TAIL_PIECES = (512, 256, 128, 64, 32, 16, 8, 4, 2)


def _dispatch_kernel(runs_ref, prev_runs_ref, pos_ref, h2_ref, xs_ref, buf0, buf1, zero_ref, sem0, sem1, zsem,
                     ssem):
    i = pl.program_id(0)
    last = NT - 1

    def tail_copies(act):
        def per_expert(e, carry):
            row = runs_ref[0, 3 * N_EXPERTS + e]
            n = runs_ref[0, 4 * N_EXPERTS + e]
            done = jnp.int32(0)
            for size in TAIL_PIECES:
                @pl.when((n & size) != 0)
                def _(done=done, size=size):
                    act(pltpu.make_async_copy(zero_ref.at[pl.ds(0, size * PK), :], _lin(xs_ref, row + done, size), zsem))
                done = done + (n & size)
            return carry

        lax.fori_loop(0, N_EXPERTS, per_expert, 0)

    def spare_copies(act):
        used = runs_ref[0, 3 * N_EXPERTS + N_EXPERTS - 1] + runs_ref[0, 4 * N_EXPERTS + N_EXPERTS - 1]

        def spare_block(k, carry):
            act(pltpu.make_async_copy(zero_ref.at[pl.ds(0, MOE_BLK * PK), :],
                                      _lin(xs_ref, used + k * MOE_BLK, MOE_BLK), ssem))
            return carry

        lax.fori_loop(0, N_MOE_BLOCKS - used // MOE_BLK, spare_block, 0)

    start = lambda d: d.start()
    wait = lambda d: d.wait()

    @pl.when(i == 0)
    def _():
        for buf in (buf0, buf1):
            buf[TS * PK:, :] = jnp.zeros((RUN_ROWS * PK, LANES), jnp.uint32)
        zero_ref[...] = jnp.zeros_like(zero_ref)
        tail_copies(start)
        spare_copies(start)

    def run_copies(runs, buf, sem, act):
        def chunk_copy(off, dst, k):
            return pltpu.make_async_copy(_lin(buf, off + k * RUN_ROWS, RUN_ROWS),
                                         _lin(xs_ref, dst + k * RUN_ROWS, RUN_ROWS), sem)

        def per_expert(e, carry):
            off, n, dst = _run_fields(runs, e)
            lax.fori_loop(0, (n + RUN_ROWS - 1) // RUN_ROWS, lambda k, c: (act(chunk_copy(off, dst, k)), c)[1], 0)
            return carry

        lax.fori_loop(0, N_EXPERTS, per_expert, 0)

    pos = pos_ref[...]
    col = lax.broadcasted_iota(jnp.int32, (TM, TS), 1).astype(F32)
    sel = jnp.where((col == pos[:, 0:1]) | (col == pos[:, 1:2]), 1.0, 0.0).astype(BF16)
    srt = lax.dot_general(sel, h2_ref[...], (((0,), (0,)), ((), ())), preferred_element_type=F32)
    packed = _pack_pairs(srt)

    for par, (buf, sem, obuf, osem) in enumerate(((buf0, sem0, buf1, sem1), (buf1, sem1, buf0, sem0))):
        @pl.when(i % 2 == par)
        def _(buf=buf, sem=sem, obuf=obuf, osem=osem):
            _store_rows(buf, packed)

            @pl.when(i == 0)
            def _():
                tail_copies(wait)

            @pl.when(i > 0)
            def _():
                run_copies(prev_runs_ref, obuf, osem, wait)

            run_copies(runs_ref, buf, sem, start)

            @pl.when(i == last)
            def _():
                run_copies(runs_ref, buf, sem, wait)
                spare_copies(wait)


def _dispatch_call(runs_flat, pos, h2):
    runs_spec = lambda shift: pl.BlockSpec((None, 1, RUN_FIELDS * N_EXPERTS), lambda i: (jnp.maximum(i - shift, 0), 0, 0),
                                           memory_space=pltpu.SMEM)
    sorted_buf = pltpu.VMEM(((TS + RUN_ROWS) * PK, LANES), jnp.uint32)
    return pl.pallas_call(
        _dispatch_kernel,
        out_shape=jax.ShapeDtypeStruct((N_SLOTS * PK, LANES), jnp.uint32),
        grid=(NT,),
        in_specs=[runs_spec(0), runs_spec(1),
                  pl.BlockSpec((TM, LANES), lambda i: (i, 0)),
                  pl.BlockSpec((TM, D), lambda i: (i, 0))],
        out_specs=pl.BlockSpec(memory_space=pl.ANY),
        scratch_shapes=[sorted_buf, sorted_buf, pltpu.VMEM((TAIL_PIECES[0] * PK, LANES), jnp.uint32),
                        pltpu.SemaphoreType.DMA, pltpu.SemaphoreType.DMA, pltpu.SemaphoreType.DMA,
                        pltpu.SemaphoreType.DMA],
        compiler_params=_cparams(("arbitrary",)),
        name="moe_dispatch",
    )(runs_flat, runs_flat, pos, h2)


def _expert_kernel(be_ref, nu_ref, xs_ref, wg_ref, wu_ref, wd_ref, ys_ref, wgb_ref, wub_ref, wdb_ref):
    b = pl.program_id(0)

    @pl.when(jnp.logical_or(b == 0, be_ref[b] != be_ref[jnp.maximum(b - 1, 0)]))
    def _():
        wgb_ref[...] = wg_ref[...].astype(BF16)
        wub_ref[...] = wu_ref[...].astype(BF16)
        wdb_ref[...] = wd_ref[...].astype(BF16)

    @pl.when(b < nu_ref[0])
    def _():
        xb = _unpack_pairs(_load_rows(xs_ref, MOE_BLK))
        g = _dot(xb, wgb_ref[...])
        u = _dot(xb, wub_ref[...])
        hmid = (g * _sigmoid(g)) * u
        y = _dot(hmid.astype(BF16), wdb_ref[...])
        _store_rows(ys_ref, _pack_pairs(y.astype(BF16).astype(F32)))

    @pl.when(b >= nu_ref[0])
    def _():
        ys_ref[...] = jnp.zeros_like(ys_ref)


def _expert_call(blk_e, n_used, xs, wg, wu, wd, l):
    wspec = lambda k, n: pl.BlockSpec((None, None, k, n), lambda b, be, nu: (l, be[b], 0, 0))
    return pl.pallas_call(
        _expert_kernel,
        out_shape=jax.ShapeDtypeStruct((N_SLOTS * PK, LANES), jnp.uint32),
        grid_spec=pltpu.PrefetchScalarGridSpec(
            num_scalar_prefetch=2,
            grid=(N_MOE_BLOCKS,),
            in_specs=[pl.BlockSpec((MOE_BLK * PK, LANES), lambda b, be, nu: (jnp.minimum(b, nu[0] - 1), 0)),
                      wspec(D, EXPERT_HIDDEN), wspec(D, EXPERT_HIDDEN), wspec(EXPERT_HIDDEN, D)],
            out_specs=pl.BlockSpec((MOE_BLK * PK, LANES), lambda b, be, nu: (b, 0)),
            scratch_shapes=[pltpu.VMEM((D, EXPERT_HIDDEN), BF16), pltpu.VMEM((D, EXPERT_HIDDEN), BF16),
                            pltpu.VMEM((EXPERT_HIDDEN, D), BF16)]),
        compiler_params=_cparams(("arbitrary",)),
        name="moe_experts",
    )(blk_e, n_used, xs, wg, wu, wd)


FETCH_ROWS = 64
RUN_PIECES = (32, 16, 8, 4, 2)


def _combine_kernel(runs_ref, next_runs_ref, ys_ref, pos_ref, x_ref, rt_ref, mod_ref, o_ref, buf0, buf1, sem0, sem1,
                    *, n_tiles):
    i = pl.program_id(0)
    last = n_tiles - 1

    def run_copies(runs, buf, sem, act):
        def piece(off, dst, size):
            return pltpu.make_async_copy(_lin(ys_ref, dst, size), _lin(buf, off, size), sem)

        def per_expert(e, carry):
            off, n, dst = _run_fields(runs, e)
            whole = n // FETCH_ROWS

            def chunk(k, c):
                act(piece(off + k * FETCH_ROWS, dst + k * FETCH_ROWS, FETCH_ROWS))
                return c

            lax.fori_loop(0, whole, chunk, 0)
            done = whole * FETCH_ROWS
            for size in RUN_PIECES:
                @pl.when((n & size) != 0)
                def _(done=done, size=size):
                    act(piece(off + done, dst + done, size))
                done = done + (n & size)
            return carry

        lax.fori_loop(0, N_EXPERTS, per_expert, 0)

    start = lambda d: d.start()
    wait = lambda d: d.wait()

    @pl.when(i == 0)
    def _():
        buf0[...] = jnp.zeros_like(buf0)
        buf1[...] = jnp.zeros_like(buf1)
        run_copies(runs_ref, buf0, sem0, start)

    pos = pos_ref[...]
    rt = rt_ref[...]
    col = lax.broadcasted_iota(jnp.int32, (TM, TS), 1).astype(F32)
    pick = (jnp.where(col == pos[:, 0:1], rt[:, TOP_K:TOP_K + 1], 0.0)
            + jnp.where(col == pos[:, 1:2], rt[:, TOP_K + 1:TOP_K + 2], 0.0)).astype(BF16)

    for par, (buf, sem, obuf, osem) in enumerate(((buf0, sem0, buf1, sem1), (buf1, sem1, buf0, sem0))):
        @pl.when(i % 2 == par)
        def _(buf=buf, sem=sem, obuf=obuf, osem=osem):
            @pl.when(i < last)
            def _():
                run_copies(next_runs_ref, obuf, osem, start)

            run_copies(runs_ref, buf, sem, wait)
            ysb = _unpack_pairs(_load_rows(buf, TS))
            o_ref[...] = x_ref[...] + mod_ref[5:6, :] * _dot(pick, ysb)


def _combine_call(runs_flat, ys, pos, x, route, mods_l, n_tiles):
    tok = lambda w: pl.BlockSpec((TM, w), lambda i: (i, 0))
    runs_spec = lambda shift: pl.BlockSpec((None, 1, RUN_FIELDS * N_EXPERTS),
                                           lambda i: (jnp.minimum(i + shift, n_tiles - 1), 0, 0),
                                           memory_space=pltpu.SMEM)
    sorted_buf = pltpu.VMEM((TS * PK, LANES), jnp.uint32)
    return pl.pallas_call(
        functools.partial(_combine_kernel, n_tiles=n_tiles),
        out_shape=jax.ShapeDtypeStruct((n_tiles * TM, D), F32),
        grid=(n_tiles,),
        in_specs=[runs_spec(0), runs_spec(1),
                  pl.BlockSpec(memory_space=pl.ANY),
                  tok(LANES), tok(D), tok(LANES),
                  pl.BlockSpec((None, 6, D), lambda i: (_mod_row(i), 0, 0))],
        out_specs=tok(D),
        scratch_shapes=[sorted_buf, sorted_buf, pltpu.SemaphoreType.DMA, pltpu.SemaphoreType.DMA],
        compiler_params=_cparams(("arbitrary",)),
        name="moe_combine",
    )(runs_flat, runs_flat, ys, pos, x, route, mods_l)


def _rope_tables():
    nf = HEAD_DIM // 4
    inv = ROPE_BASE ** (-jnp.arange(nf, dtype=F32) / nf)
    t = jnp.arange(S)
    row = (t // GRID_W).astype(F32)[:, None] * inv[None, :]
    col = (t % GRID_W).astype(F32)[:, None] * inv[None, :]
    zero = jnp.zeros_like(row)
    cos = jnp.concatenate([jnp.cos(row), jnp.cos(row), jnp.cos(col), jnp.cos(col)], axis=1)
    sa = jnp.concatenate([-jnp.sin(row), zero, -jnp.sin(col), zero], axis=1)
    sb = jnp.concatenate([zero, jnp.sin(row), zero, jnp.sin(col)], axis=1)
    ident = (jnp.ones((TM, HEAD_DIM), F32), jnp.zeros((TM, HEAD_DIM), F32), jnp.zeros((TM, HEAD_DIM), F32))
    return tuple(jnp.tile(jnp.concatenate([a, b], axis=0), (1, LANES // HEAD_DIM))
                 for a, b in zip((cos, sa, sb), ident))


def _fourier_tables():
    s1 = np.arange(FS1)
    ang1 = 2.0 * np.pi * np.outer(s1, s1) / FS1
    w1 = np.concatenate([np.cos(ang1), -np.sin(ang1)], axis=0) / np.sqrt(S)
    k1 = np.arange(FS1)[:, None, None]
    k2 = np.arange(FS2)[None, :, None]
    s2 = np.arange(FS2)[None, None, :]
    ang2 = 2.0 * np.pi * ((k1 + FS1 * k2) * s2 % S) / S
    c2, sn2 = np.cos(ang2), np.sin(ang2)
    ta = np.concatenate([c2, -sn2], axis=1)
    tb = np.concatenate([sn2, c2], axis=1)
    sc = np.arange(C)
    angc = 2.0 * np.pi * np.outer(sc, sc) / C
    wc = np.concatenate([np.cos(angc), -np.sin(angc)], axis=0) / np.sqrt(C)
    return tuple(jnp.asarray(a, F32).astype(BF16) for a in (w1, ta, tb, wc))


def _channel_dft():
    cidx = np.arange(FOURIER_GROUP_W)
    ang = 2.0 * np.pi * np.outer(cidx, cidx) / FOURIER_GROUP_W
    eye = np.eye(FOURIER_W // FOURIER_GROUP_W)
    cw = np.kron(eye, np.cos(ang)) / np.sqrt(FOURIER_GROUP_W)
    sw = np.kron(eye, np.sin(ang)) / np.sqrt(FOURIER_GROUP_W)
    return jnp.asarray(np.concatenate([cw, sw], axis=0), F32)


def _pool_bands():
    t = np.arange(TP)[:, None]
    main, halo = [], []
    for w in POOL_WINDOWS:
        def hit(j):
            return ((j - t >= -(w // 2)) & (j - t <= w // 2 - 1)).astype(np.float32)
        main.append(hit(np.arange(TP)[None, :]))
        halo.append(np.concatenate([hit(np.arange(-HALO, 0)[None, :]),
                                    hit(np.arange(TP, TP + HALO)[None, :])], axis=1))
    return (jnp.asarray(np.stack(main), F32).astype(BF16), jnp.asarray(np.stack(halo), F32).astype(BF16))


def _pool_inv_counts():
    win = np.repeat(np.array(POOL_WINDOWS), POOL_GROUP_W)[None, :]

    def table(pos0, seq_len):
        pos = (pos0 + np.arange(TP))[:, None]
        lo = np.clip(pos - win // 2, 0, seq_len)
        hi = np.clip(pos - win // 2 + win, 0, seq_len)
        return 1.0 / (hi - lo)

    tabs = [table(TP, S), table(0, S), table(S - TP, S), table(0, C)]
    return jnp.asarray(np.stack(tabs), F32)


def _conv_shifts():
    i = np.arange(CONV_WIN)
    return jnp.asarray(np.stack([(i[None, :] == i[:, None] + s) for s in range(1, 8)]), F32).astype(BF16)


def _fold_kernel(a_ref, b_ref, o_ref):
    a, b = a_ref[...], b_ref[...]
    a_hi, b_hi = a.astype(BF16), b.astype(BF16)
    a_lo = (a - a_hi.astype(F32)).astype(BF16)
    b_lo = (b - b_hi.astype(F32)).astype(BF16)
    o_ref[...] = (_dot(a_hi, b_hi) + _dot(a_lo, b_hi) + _dot(a_hi, b_lo)).astype(BF16)


def _fold_fourier_weights(dftw, w_br_fourier):
    nl = w_br_fourier.shape[0]
    return pl.pallas_call(
        _fold_kernel,
        out_shape=jax.ShapeDtypeStruct((nl, 2 * FOURIER_W, D), BF16),
        grid=(nl,),
        in_specs=[pl.BlockSpec((2 * FOURIER_W, FOURIER_W), lambda l: (0, 0)),
                  pl.BlockSpec((None, FOURIER_W, D), lambda l: (l, 0, 0))],
        out_specs=pl.BlockSpec((None, 2 * FOURIER_W, D), lambda l: (l, 0, 0)),
        compiler_params=_cparams(("arbitrary",)),
        name="fold_fourier_proj",
    )(dftw, w_br_fourier)


def _block_diag(blocks):
    n, r, c = blocks.shape
    eye = jnp.eye(n, dtype=blocks.dtype)
    return (blocks[:, :, None, :] * eye[:, None, :, None]).reshape(n * r, n * c)


def kernel(x, c, ctx, c_ctx, w_ada, b_ada, g_norm_mix, g_norm_ffn, w_in, g_q, g_k, sink, w_br_attn,
           w_br_fourier, pool_w, pool_scale, w_br_pool, conv_w, conv_b, cn_g, cn_b, w_br_conv, w_gate,
           b_gate, w_out, w_router_grp, b_router_grp, w_router_exp, b_router_exp, w_e_gate, w_e_up,
           w_e_down):
    xs = jnp.concatenate([x.reshape(N_LAT, D), ctx.reshape(N_CTX, D)], axis=0)
    nl = w_ada.shape[0]
    mods = _ada_all(c, c_ctx, w_ada, b_ada).reshape(nl, 8, 6, D)
    rope_tabs = _rope_tables()
    four_tabs = _fourier_tables()
    band_main, band_halo = _pool_bands()
    inv_cnt = _pool_inv_counts()
    shifts = _conv_shifts()
    wf_all = _fold_fourier_weights(_channel_dft(), w_br_fourier)
    bd = jnp.asarray(np.kron(np.eye(LANES // HEAD_DIM), np.ones((HEAD_DIM, HEAD_DIM))), F32).astype(BF16)
    tri = jnp.asarray(np.tril(np.ones((TM, TM)), -1), F32).astype(BF16)
    rpad = jnp.zeros((nl, D, LANES - N_GROUPS - N_EXPERTS), F32)
    w_router = jnp.concatenate([w_router_grp, w_router_exp, rpad], axis=-1)
    r_hi = w_router.astype(BF16)
    r_lo = (w_router - r_hi.astype(F32)).astype(BF16)
    r_b = jnp.concatenate([b_router_grp, b_router_exp, rpad[:, 0, :]], axis=-1).reshape(nl, 1, LANES)
    stacked = tuple(w.astype(BF16) for w in (w_br_attn,)) + (wf_all,) + tuple(
        w.astype(BF16) for w in (w_br_pool, w_br_conv, w_gate, w_out)) + (r_hi, r_lo)

    for l in range(nl):
        mods_l = mods[l]
        gn = g_norm_mix[l].reshape(1, D)
        q, kv, f, p, u = _proj_call(xs, mods_l, gn, w_in, l, rope_tabs,
                                    jnp.tile(g_q[l], 2).reshape(1, LANES),
                                    jnp.tile(g_k[l], 2).reshape(1, LANES), bd)
        a_re, a_im = _fourier_stage1_call(f, four_tabs)
        a_lat, a_ctx = _attn_call(sink[l], q, kv)
        h_lat, h_ctx = _fourier_stage2_call(a_re, a_im, f, four_tabs)
        zc, cact = _poolconv_call(p, u, band_main, band_halo, inv_cnt, _block_diag(pool_w[l]).astype(BF16),
                                  pool_scale[l].reshape(1, POOL_W), shifts, conv_w[l], conv_b[l].reshape(1, CONV_W),
                                  cn_g[l].reshape(1, CONV_W), cn_b[l].reshape(1, CONV_W))
        small = (b_gate[l].reshape(1, 4 * D), g_norm_ffn[l].reshape(1, D), r_b[l])
        xs, h2, route = _mix_call(xs, mods_l, gn, a_lat, a_ctx, h_lat, h_ctx, zc, cact, l, stacked, small)
        pos, runs_flat, blk_e, n_used = _plan_call(route, tri)
        slots = _dispatch_call(runs_flat, pos, h2)
        ys = _expert_call(blk_e, n_used, slots, w_e_gate, w_e_up, w_e_down, l)
        xs = _combine_call(runs_flat, ys, pos, xs, route, mods_l, LAT_TILES if l == nl - 1 else NT)
    return xs.reshape(B, S, D)
```

```python
import functools

import numpy as np
import jax
import jax.numpy as jnp
from jax import lax
from jax.experimental import pallas as pl
from jax.experimental.pallas import tpu as pltpu

F32 = jnp.float32
BF16 = jnp.bfloat16

D = 1024
B = 2
S = 8192
C = 256
GRID_W = 64
HEAD_DIM = 64
N_Q_HEADS = 8
N_KV_HEADS = 2
GQA = N_Q_HEADS // N_KV_HEADS
WINDOW = 128
ATTN_BLK = 128
ATTN_QB = 4
ROPE_BASE = 10000.0
Q_W = 512
KV_W = 128
FOURIER_W = 640
FOURIER_GROUP_W = 160
POOL_W = 640
POOL_GROUP_W = 160
POOL_WINDOWS = (2, 4, 8, 16)
CONV_W = 512
CONV_K = 31
PROJ_W = 3072
N_GROUPS = 4
EPG = 8
N_EXPERTS = 32
TOP_K = 2
EXPERT_HIDDEN = 512
MOE_BLK = 512
EPS = 1e-6
NEG_INF = -1e30
LOG2E = 1.4426950408889634

N_LAT = B * S
N_CTX = B * C
N_TOK = N_LAT + N_CTX
TM = 512
NT = N_TOK // TM
LAT_TILES = N_LAT // TM
TILES_PER_BATCH = S // TM
TP = 256
NTP = N_TOK // TP
HALO = 16
CONV_WIN = TP // 2 + 2 * HALO
N_ASSIGN = N_TOK * TOP_K
RUN_ROWS = 48
RUN_FIELDS = 5
PK = D // 2 // 128
TS = 1152
N_MOE_BLOCKS = (N_ASSIGN + NT * N_EXPERTS + N_EXPERTS * (RUN_ROWS - 1 + MOE_BLK - 1)) // MOE_BLK
N_SLOTS = N_MOE_BLOCKS * MOE_BLK
FS1 = 64
FS2 = 128
F1_COLS = FS2 * FOURIER_W
F1_CW = 8192
F2_K1 = 8
LANES = 128
VMEM_LIMIT = 56 * 1024 * 1024


def _cparams(sem, vmem=VMEM_LIMIT):
    return pltpu.CompilerParams(dimension_semantics=sem, vmem_limit_bytes=vmem)


def _const_spec(shape):
    nd = len(shape)
    return pl.BlockSpec(shape, lambda *_: (0,) * nd, pipeline_mode=pl.Buffered(1))


def _dot(a, b):
    return jnp.dot(a, b, preferred_element_type=F32)


def _modulate(x, g, shift, scale):
    y = x * lax.rsqrt(jnp.mean(x * x, axis=-1, keepdims=True) + EPS)
    return (y * g) * (1.0 + scale) + shift


def _sigmoid(x):
    return 1.0 / (1.0 + jnp.exp(-x))


def _ada_kernel(ct_ref, w_ref, b_ref, o_ref):
    ct = ct_ref[...]
    s = ct * _sigmoid(ct)
    w = w_ref[...]
    rows = [jnp.sum(w * s[:, r:r + 1], axis=0, keepdims=True) for r in range(3)]
    rows.append(jnp.zeros((5, w.shape[1]), F32))
    o_ref[...] = jnp.concatenate(rows, axis=0) + b_ref[...]


def _ada_all(c, c_ctx, w_ada, b_ada):
    ct = jnp.concatenate([c, c_ctx[None, :], jnp.zeros((5, D), F32)], axis=0).T
    cols = 1536
    nl = w_ada.shape[0]
    return pl.pallas_call(
        _ada_kernel,
        out_shape=jax.ShapeDtypeStruct((nl, 8, 6 * D), F32),
        grid=(nl, 6 * D // cols),
        in_specs=[pl.BlockSpec((D, 8), lambda l, j: (0, 0)),
                  pl.BlockSpec((None, D, cols), lambda l, j: (l, 0, j)),
                  pl.BlockSpec((None, 1, cols), lambda l, j: (l, 0, j))],
        out_specs=pl.BlockSpec((None, 8, cols), lambda l, j: (l, 0, j)),
        compiler_params=_cparams(("arbitrary", "arbitrary")),
        name="adaln",
    )(ct, w_ada, b_ada.reshape(nl, 1, 6 * D))


def _head_rms(t, g128, bd):
    outs = []
    for j in range(t.shape[1] // LANES):
        blk = t[:, j * LANES:(j + 1) * LANES]
        ss = _dot((blk * blk).astype(BF16), bd)
        outs.append(blk * lax.rsqrt(ss * (1.0 / HEAD_DIM) + EPS) * g128)
    return outs


def _rope(blocks, cos, sa, sb):
    outs = []
    for blk in blocks:
        up = pltpu.roll(blk, LANES - 16, 1)
        dn = pltpu.roll(blk, 16, 1)
        outs.append(blk * cos + up * sa + dn * sb)
    return outs


def _proj_kernel(x_ref, mod_ref, gn_ref, w_ref, cos_ref, sa_ref, sb_ref, gq_ref, gk_ref, bd_ref,
                 q_ref, kv_ref, f_ref, p_ref, u_ref, wbf_ref):
    @pl.when(pl.program_id(0) == 0)
    def _():
        wbf_ref[...] = w_ref[...].astype(BF16)

    m = mod_ref[...]
    hb = _modulate(x_ref[...], gn_ref[...], m[0:1], m[1:2]).astype(BF16)
    cos, sa, sb, bd = cos_ref[...], sa_ref[...], sb_ref[...], bd_ref[...]
    o_kv, o_f, o_a = Q_W, Q_W + 2 * KV_W, Q_W + 2 * KV_W + FOURIER_W + POOL_W
    qkv = _dot(hb, wbf_ref[:, 0:o_f])
    fp = _dot(hb, wbf_ref[:, o_f:o_a])
    q = _rope(_head_rms(qkv[:, 0:Q_W], gq_ref[...], bd), cos, sa, sb)
    q_ref[...] = (jnp.concatenate(q, axis=1) * (LOG2E * HEAD_DIM ** -0.5)).astype(BF16)
    k = _rope(_head_rms(qkv[:, o_kv:o_kv + KV_W], gk_ref[...], bd), cos, sa, sb)
    kv_ref[:, 0:KV_W] = k[0].astype(BF16)
    kv_ref[:, KV_W:2 * KV_W] = qkv[:, o_kv + KV_W:o_f].astype(BF16)
    ag = _dot(hb, wbf_ref[:, o_a:PROJ_W])
    f_ref[...] = fp[:, 0:FOURIER_W].astype(BF16)
    p_ref[...] = fp[:, FOURIER_W:].astype(BF16)
    u_ref[...] = (ag[:, 0:CONV_W] * _sigmoid(ag[:, CONV_W:])).astype(BF16)


def _mod_row(i):
    return jnp.minimum(i // TILES_PER_BATCH, 2)


def _layer_spec(shape, l):
    nd = len(shape)
    return pl.BlockSpec((None,) + tuple(shape), lambda *_: (l,) + (0,) * nd, pipeline_mode=pl.Buffered(1))


def _proj_call(x, mods_l, gn, w_in, l, rope_tabs, gq128, gk128, bd):
    cos, sa, sb = rope_tabs
    tok = lambda w: pl.BlockSpec((TM, w), lambda i: (i, 0))
    rope_spec = pl.BlockSpec((TM, LANES), lambda i: (jnp.where(i < LAT_TILES, i % TILES_PER_BATCH,
                                                               TILES_PER_BATCH), 0))
    widths = (Q_W, 2 * KV_W, FOURIER_W, POOL_W, CONV_W)
    return pl.pallas_call(
        _proj_kernel,
        out_shape=[jax.ShapeDtypeStruct((N_TOK, w), BF16) for w in widths],
        grid=(NT,),
        in_specs=[tok(D),
                  pl.BlockSpec((None, 6, D), lambda i: (_mod_row(i), 0, 0)),
                  _const_spec((1, D)),
                  _layer_spec((D, PROJ_W), l),
                  rope_spec, rope_spec, rope_spec,
                  _const_spec((1, LANES)), _const_spec((1, LANES)),
                  _const_spec((LANES, LANES))],
        out_specs=[tok(w) for w in widths],
        scratch_shapes=[pltpu.VMEM((D, PROJ_W), BF16)],
        compiler_params=_cparams(("arbitrary",)),
        name="proj",
    )(x, mods_l, gn, w_in, cos, sa, sb, gq128, gk128, bd)


def _attend_many(jobs, sink_ref):
    lane = lax.broadcasted_iota(jnp.int32, (ATTN_BLK, LANES), 1)
    chains = []
    for q, kv_blocks, biases in jobs:
        for j in range(N_KV_HEADS):
            ks = slice(j * HEAD_DIM, (j + 1) * HEAD_DIM)
            vs = slice(KV_W + j * HEAD_DIM, KV_W + (j + 1) * HEAD_DIM)
            kj = jnp.concatenate([blk[:, ks] for blk in kv_blocks], axis=0)
            vj = jnp.concatenate([blk[:, vs] for blk in kv_blocks], axis=0)
            vaug = jnp.concatenate([vj, jnp.ones_like(vj)], axis=1)
            qs = jnp.concatenate([q[:, (j * GQA + g) * HEAD_DIM:(j * GQA + g + 1) * HEAD_DIM]
                                  for g in range(GQA)], axis=0)
            s = lax.dot_general(qs, kj, (((1,), (1,)), ((), ())), preferred_element_type=F32)
            chains.append((j, s, vaug, kv_blocks, biases))
    soft = []
    for j, s, vaug, kv_blocks, biases in chains:
        probs, sink_terms = [], []
        for g in range(GQA):
            sg = s[g * ATTN_BLK:(g + 1) * ATTN_BLK]
            pieces, col = [], 0
            for blk, bias in zip(kv_blocks, biases):
                piece = sg[:, col:col + blk.shape[0]]
                pieces.append(piece if bias is None else piece + bias)
                col += blk.shape[0]
            sg = jnp.concatenate(pieces, axis=1)
            sk = sink_ref[j * GQA + g] * LOG2E
            mx = jnp.maximum(jnp.max(sg, axis=-1, keepdims=True), sk)
            probs.append(jnp.exp2(sg - mx).astype(BF16))
            sink_terms.append(jnp.exp2(sk - mx))
        soft.append((jnp.concatenate(probs, axis=0), vaug, sink_terms))
    heads = []
    for p, vaug, sink_terms in soft:
        o = _dot(p, vaug)
        for g in range(GQA):
            og = o[g * ATTN_BLK:(g + 1) * ATTN_BLK]
            heads.append(og / (og[:, HEAD_DIM:HEAD_DIM + 1] + sink_terms[g]))
    outs = []
    for n in range(len(jobs)):
        hs = heads[n * N_Q_HEADS:(n + 1) * N_Q_HEADS]
        tiles = [jnp.where(lane < HEAD_DIM, hs[2 * t], pltpu.roll(hs[2 * t + 1], HEAD_DIM, 1))
                 for t in range(N_Q_HEADS // 2)]
        outs.append(jnp.concatenate(tiles, axis=1).astype(BF16))
    return outs


def _attn_latent_kernel(sink_ref, q_ref, prev_ref, cur_ref, next_ref, ctx_ref, o_ref):
    n = pl.program_id(1)
    r = lax.broadcasted_iota(jnp.int32, (ATTN_BLK, ATTN_BLK), 0)
    jj = lax.broadcasted_iota(jnp.int32, (ATTN_BLK, ATTN_BLK), 1)
    far = jnp.int32(2 * ATTN_BLK)
    off_prev = jnp.where(n > 0, 0, far)
    off_next = jnp.where(n < S // (ATTN_QB * ATTN_BLK) - 1, 0, far)
    prev_ok = jnp.where(jj - r >= 0, 0.0, NEG_INF)
    next_ok = jnp.where(r - jj >= 0, 0.0, NEG_INF)
    prev_edge = jnp.where(jj - r - off_prev >= 0, 0.0, NEG_INF)
    next_edge = jnp.where(r - jj - off_next >= 0, 0.0, NEG_INF)
    ctx = ctx_ref[...]
    rows = lambda b: slice(b * ATTN_BLK, (b + 1) * ATTN_BLK)
    blocks = [prev_ref[...]] + [cur_ref[rows(b), :] for b in range(ATTN_QB)] + [next_ref[...]]
    jobs = [(q_ref[rows(b), :], [ctx] + blocks[b:b + 3],
             [None, prev_edge if b == 0 else prev_ok, None, next_edge if b == ATTN_QB - 1 else next_ok])
            for b in range(ATTN_QB)]
    for b, out in enumerate(_attend_many(jobs, sink_ref)):
        o_ref[rows(b), :] = out


def _attn_context_kernel(sink_ref, q_ref, ctx_ref, o_ref):
    o_ref[...] = _attend_many([(q_ref[...], [ctx_ref[...]], [None])], sink_ref)[0]


def _attn_call(sink_l, q, kv):
    nb = S // ATTN_BLK
    nq = nb // ATTN_QB
    smem = pl.BlockSpec(memory_space=pltpu.SMEM)
    pair = lambda w: pl.BlockSpec((ATTN_QB * ATTN_BLK, w), lambda b, n: (b * nq + n, 0))
    prev = pl.BlockSpec((ATTN_BLK, 2 * KV_W), lambda b, n: (b * nb + jnp.maximum(ATTN_QB * n - 1, 0), 0))
    nxt = pl.BlockSpec((ATTN_BLK, 2 * KV_W),
                       lambda b, n: (b * nb + jnp.minimum(ATTN_QB * (n + 1), nb - 1), 0))
    ctxs = pl.BlockSpec((C, 2 * KV_W), lambda b, n: (N_LAT // C + b, 0))
    lat = pl.pallas_call(
        _attn_latent_kernel,
        out_shape=jax.ShapeDtypeStruct((N_LAT, Q_W), BF16),
        grid=(B, nq),
        in_specs=[smem, pair(Q_W), prev, pair(2 * KV_W), nxt, ctxs],
        out_specs=pair(Q_W),
        compiler_params=_cparams(("parallel", "parallel")),
        name="attn_latent",
    )(sink_l, q, kv, kv, kv, kv)
    ncb = C // ATTN_BLK
    base = N_LAT // ATTN_BLK
    ctx = pl.pallas_call(
        _attn_context_kernel,
        out_shape=jax.ShapeDtypeStruct((N_CTX, Q_W), BF16),
        grid=(B, ncb),
        in_specs=[smem, pl.BlockSpec((ATTN_BLK, Q_W), lambda b, n: (base + b * ncb + n, 0)), ctxs],
        out_specs=pl.BlockSpec((ATTN_BLK, Q_W), lambda b, n: (b * ncb + n, 0)),
        compiler_params=_cparams(("parallel", "parallel")),
        name="attn_context",
    )(sink_l, q, kv)
    return lat, ctx


def _f1_kernel(w_ref, f_ref, re_ref, im_ref):
    res = _dot(w_ref[...], f_ref[...])
    re_ref[...] = res[:FS1].astype(BF16)
    im_ref[...] = res[FS1:].astype(BF16)


def _f2_kernel(ta_ref, tb_ref, re_ref, im_ref, after_ref, o_ref):
    del after_ref
    for i in range(F2_K1):
        res = _dot(ta_ref[i], re_ref[i]) + _dot(tb_ref[i], im_ref[i])
        o_ref[i, :, 0:FOURIER_W] = res[:FS2].astype(BF16)
        o_ref[i, :, FOURIER_W:2 * FOURIER_W] = res[FS2:].astype(BF16)


def _fc_kernel(w_ref, f_ref, o_ref):
    res = _dot(w_ref[...], f_ref[...])
    o_ref[:, 0:FOURIER_W] = res[:C].astype(BF16)
    o_ref[:, FOURIER_W:2 * FOURIER_W] = res[C:].astype(BF16)


def _fourier_stage1_call(f, tabs):
    w1 = tabs[0]
    f2d = f.reshape(N_TOK // FS2, F1_COLS)
    nchunk = F1_COLS // F1_CW
    a_re, a_im = pl.pallas_call(
        _f1_kernel,
        out_shape=[jax.ShapeDtypeStruct((B * FS1, F1_COLS), BF16)] * 2,
        grid=(B, nchunk),
        in_specs=[_const_spec((2 * FS1, FS1)),
                  pl.BlockSpec((FS1, F1_CW), lambda b, j: (b, j))],
        out_specs=[pl.BlockSpec((FS1, F1_CW), lambda b, j: (b, j))] * 2,
        compiler_params=_cparams(("parallel", "parallel")),
        name="fourier_stage1",
    )(w1, f2d)
    return a_re.reshape(B * FS1, FS2, FOURIER_W), a_im.reshape(B * FS1, FS2, FOURIER_W)


def _fourier_stage2_call(a_re, a_im, f, tabs, after):
    _, ta, tb, wc = tabs
    nk = FS1 // F2_K1
    aspec = pl.BlockSpec((F2_K1, FS2, FOURIER_W), lambda b, k1: (b * nk + k1, 0, 0))
    tspec = pl.BlockSpec((F2_K1, 2 * FS2, FS2), lambda b, k1: (k1, 0, 0))
    h_t = pl.pallas_call(
        _f2_kernel,
        out_shape=jax.ShapeDtypeStruct((B, FS1, FS2, 2 * FOURIER_W), BF16),
        grid=(B, nk),
        in_specs=[tspec, tspec, aspec, aspec, pl.BlockSpec(memory_space=pl.ANY)],
        out_specs=pl.BlockSpec((None, F2_K1, FS2, 2 * FOURIER_W), lambda b, k1: (b, k1, 0, 0)),
        compiler_params=_cparams(("parallel", "parallel")),
        name="fourier_stage2",
    )(ta, tb, a_re, a_im, after)
    h_lat = jnp.transpose(h_t, (0, 2, 1, 3)).reshape(N_LAT, 2 * FOURIER_W)
    h_ctx = pl.pallas_call(
        _fc_kernel,
        out_shape=jax.ShapeDtypeStruct((N_CTX, 2 * FOURIER_W), BF16),
        grid=(B,),
        in_specs=[_const_spec((2 * C, C)),
                  pl.BlockSpec((C, FOURIER_W), lambda b: (N_LAT // C + b, 0))],
        out_specs=pl.BlockSpec((C, 2 * FOURIER_W), lambda b: (b, 0)),
        compiler_params=_cparams(("parallel",)),
        name="fourier_context",
    )(wc, f)
    return h_lat, h_ctx


def _poolconv_kernel(pc_ref, pp_ref, pn_ref, uc_ref, up_ref, un_ref, bm_ref, bh_ref, ic_ref, pw_ref, ps_ref,
                     sh_ref, cw_ref, cb_ref, cg_ref, cnb_ref, z_ref, a_ref, win0_ref, win1_ref, cv_ref):
    t = pl.program_id(0)
    lat_tiles = N_LAT // TP
    per_seq = S // TP
    is_ctx = t >= lat_tiles
    first = jnp.logical_or(t % per_seq == 0, is_ctx)
    last = jnp.logical_or(t % per_seq == per_seq - 1, is_ctx)

    keep_prev = jnp.where(first, 0.0, 1.0)
    keep_next = jnp.where(last, 0.0, 1.0)

    ub = jnp.concatenate([(up_ref[...].astype(F32) * keep_prev).astype(BF16), uc_ref[...],
                          (un_ref[...].astype(F32) * keep_next).astype(BF16)], axis=0)
    off = HALO - CONV_K // 2
    half_rows = TP // 2
    wins = (win0_ref, win1_ref)
    for hf, win_ref in enumerate(wins):
        window = ub[hf * half_rows:hf * half_rows + CONV_WIN]
        win_ref[0] = window.astype(F32)
        for s in range(1, 8):
            win_ref[s] = _dot(sh_ref[s - 1], window)

    pcur = pc_ref[...]
    halo = jnp.concatenate([pp_ref[...].astype(F32) * keep_prev,
                            pn_ref[...].astype(F32) * keep_next], axis=0).astype(BF16)
    sums = []
    for gi in range(len(POOL_WINDOWS)):
        cs = slice(gi * LANES, (gi + 2) * LANES)
        sums.append(_dot(bm_ref[gi], pcur[:, cs]) + _dot(bh_ref[gi], halo[:, cs]))
    lane_t = lax.broadcasted_iota(jnp.int32, (TP, LANES), 1)
    tiles = [sums[0][:, :LANES]]
    for gi in range(1, len(POOL_WINDOWS)):
        split = gi * POOL_GROUP_W - gi * LANES
        tiles.append(jnp.where(lane_t < split, sums[gi - 1][:, LANES:], sums[gi][:, :LANES]))
    tiles.append(sums[-1][:, LANES:])
    zsum = jnp.concatenate(tiles, axis=1)
    z = zsum * ic_ref[...] - pcur.astype(F32)
    z_ref[...] = (_dot(z.astype(BF16), pw_ref[...]) * ps_ref[...]).astype(BF16)

    for hf, win_ref in enumerate(wins):
        base = hf * half_rows
        for cb in range(CONV_W // LANES):
            cs = slice(cb * LANES, (cb + 1) * LANES)
            acc = jnp.zeros((half_rows, LANES), F32) + cb_ref[:, cs]
            for j in range(CONV_K):
                s, m = (off + j) % 8, (off + j) // 8
                acc = acc + win_ref[s, 8 * m:8 * m + half_rows, cs] * cw_ref[j:j + 1, cs]
            cv_ref[base:base + half_rows, cs] = acc
    cv = cv_ref[...]
    mu = jnp.mean(cv, axis=-1, keepdims=True)
    var = jnp.mean(jnp.square(cv - mu), axis=-1, keepdims=True)
    un = (cv - mu) * lax.rsqrt(var + EPS) * cg_ref[...] + cnb_ref[...]
    a_ref[...] = (un * _sigmoid(un)).astype(BF16)


def _poolconv_call(p, u, band_main, band_halo, inv_cnt, pw_bd, pool_scale, shifts, conv_w, conv_b, cn_g, cn_b):
    nh = TP // HALO
    last_h = N_TOK // HALO - 1
    cur = lambda w: pl.BlockSpec((TP, w), lambda t: (t, 0))
    prv = lambda w: pl.BlockSpec((HALO, w), lambda t: (jnp.maximum(t * nh - 1, 0), 0))
    nxt = lambda w: pl.BlockSpec((HALO, w), lambda t: (jnp.minimum((t + 1) * nh, last_h), 0))
    per_seq = S // TP

    def kind(t):
        return jnp.where(t >= N_LAT // TP, 3, jnp.where(t % per_seq == 0, 1, jnp.where(t % per_seq == per_seq - 1, 2, 0)))

    return pl.pallas_call(
        _poolconv_kernel,
        out_shape=[jax.ShapeDtypeStruct((N_TOK, POOL_W), BF16),
                   jax.ShapeDtypeStruct((N_TOK, CONV_W), BF16)],
        grid=(NTP,),
        in_specs=[cur(POOL_W), prv(POOL_W), nxt(POOL_W), cur(CONV_W), prv(CONV_W), nxt(CONV_W),
                  _const_spec((4, TP, TP)), _const_spec((4, TP, 2 * HALO)),
                  pl.BlockSpec((None, TP, POOL_W), lambda t: (kind(t), 0, 0)),
                  _const_spec((POOL_W, POOL_W)), _const_spec((1, POOL_W)),
                  _const_spec((7, CONV_WIN, CONV_WIN)),
                  _const_spec((CONV_K, CONV_W)), _const_spec((1, CONV_W)),
                  _const_spec((1, CONV_W)), _const_spec((1, CONV_W))],
        out_specs=[cur(POOL_W), cur(CONV_W)],
        scratch_shapes=[pltpu.VMEM((8, CONV_WIN, CONV_W), F32), pltpu.VMEM((8, CONV_WIN, CONV_W), F32),
                        pltpu.VMEM((TP, CONV_W), F32)],
        compiler_params=_cparams(("parallel",)),
        name="pool_conv",
    )(p, p, p, u, u, u, band_main, band_halo, inv_cnt, pw_bd, pool_scale, shifts, conv_w, conv_b, cn_g, cn_b)


def _route(logits):
    lane = lax.broadcasted_iota(jnp.int32, logits.shape, 1)
    big = jnp.int32(LANES)
    lg = jnp.where(lane < N_GROUPS, logits, NEG_INF)
    mg = jnp.max(lg, axis=-1, keepdims=True)
    grp = jnp.min(jnp.where(lg == mg, lane, big), axis=-1, keepdims=True)
    p_grp = 1.0 / jnp.sum(jnp.exp(lg - mg), axis=-1, keepdims=True)
    lo = N_GROUPS + grp * EPG
    le = jnp.where((lane >= lo) & (lane < lo + EPG), logits, NEG_INF)
    m1 = jnp.max(le, axis=-1, keepdims=True)
    i1 = jnp.min(jnp.where(le == m1, lane, big), axis=-1, keepdims=True)
    le2 = jnp.where(lane == i1, NEG_INF, le)
    m2 = jnp.max(le2, axis=-1, keepdims=True)
    i2 = jnp.min(jnp.where(le2 == m2, lane, big), axis=-1, keepdims=True)
    r = jnp.exp(m2 - m1)
    w1 = p_grp / (1.0 + r)
    w2 = p_grp * r / (1.0 + r)
    e1 = (i1 - N_GROUPS).astype(F32)
    e2 = (i2 - N_GROUPS).astype(F32)
    return jnp.where(lane == 0, e1, jnp.where(lane == 1, e2, jnp.where(lane == 2, w1,
                     jnp.where(lane == 3, w2, 0.0))))


def _mix_kernel(x_ref, mod_ref, gn_ref, al_ref, ac_ref, hl_ref, hc_ref, z_ref, cv_ref,
                wa_ref, wf_ref, wp_ref, wc_ref, wg_ref, bg_ref, wo_ref, gf_ref, rh_ref, rl_ref, rb_ref,
                xo_ref, h2_ref, rt_ref):
    is_ctx = pl.program_id(0) >= LAT_TILES
    m = mod_ref[...]
    x = x_ref[...]
    hb = _modulate(x, gn_ref[...], m[0:1], m[1:2]).astype(BF16)
    attn = jnp.where(is_ctx, ac_ref[...], al_ref[...])
    four = jnp.where(is_ctx, hc_ref[...], hl_ref[...])
    branches = ((attn, wa_ref), (four, wf_ref), (z_ref[...], wp_ref), (cv_ref[...], wc_ref))
    acc = None
    for bi, (inp, w_ref) in enumerate(branches):
        cs = slice(bi * D, (bi + 1) * D)
        gate = _sigmoid(_dot(hb, wg_ref[:, cs]) + bg_ref[:, cs])
        term = gate * _dot(inp, w_ref[...])
        acc = term if acc is None else acc + term
    x_new = x + m[2:3] * _dot(acc.astype(BF16), wo_ref[...])
    xo_ref[...] = x_new
    h2 = _modulate(x_new, gf_ref[...], m[3:4], m[4:5])
    hi = h2.astype(BF16)
    h2_ref[...] = hi
    lo = (h2 - hi.astype(F32)).astype(BF16)
    logits = _dot(hi, rh_ref[...]) + _dot(lo, rh_ref[...]) + _dot(hi, rl_ref[...]) + rb_ref[...]
    rt_ref[...] = _route(logits)


def _mix_call(x, mods_l, gn, a_lat, a_ctx, h_lat, h_ctx, zc, cact, l, stacked, small):
    tok = lambda w: pl.BlockSpec((TM, w), lambda i: (i, 0))
    lat = lambda w: pl.BlockSpec((TM, w), lambda i: (jnp.minimum(i, LAT_TILES - 1), 0))
    wa, wf, wp, wc, wg, wo, rh, rl = stacked
    bg, gf, rb = small
    in_specs = [tok(D), pl.BlockSpec((None, 6, D), lambda i: (_mod_row(i), 0, 0)), _const_spec((1, D)),
                lat(Q_W), _const_spec((N_CTX, Q_W)),
                lat(2 * FOURIER_W), _const_spec((N_CTX, 2 * FOURIER_W)),
                tok(POOL_W), tok(CONV_W)]
    in_specs += [_layer_spec(w.shape[1:], l) for w in (wa, wf, wp, wc, wg)]
    in_specs += [_const_spec(bg.shape), _layer_spec(wo.shape[1:], l), _const_spec(gf.shape),
                 _layer_spec(rh.shape[1:], l), _layer_spec(rl.shape[1:], l), _const_spec(rb.shape)]
    return pl.pallas_call(
        _mix_kernel,
        out_shape=[jax.ShapeDtypeStruct((N_TOK, D), F32), jax.ShapeDtypeStruct((N_TOK, D), BF16),
                   jax.ShapeDtypeStruct((N_TOK, LANES), F32)],
        grid=(NT,),
        in_specs=in_specs,
        out_specs=[tok(D), tok(D), tok(LANES)],
        compiler_params=_cparams(("parallel",)),
        name="mix",
    )(x, mods_l, gn, a_lat, a_ctx, h_lat, h_ctx, zc, cact, wa, wf, wp, wc, wg, bg, wo, gf, rh, rl, rb)


def _onehots(route):
    lane = lax.broadcasted_iota(jnp.int32, route.shape, 1)
    e1 = route[:, 0:1].astype(jnp.int32)
    e2 = route[:, 1:2].astype(jnp.int32)
    return (lane == e1).astype(F32), (lane == e2).astype(F32)


def _lane_cumsum(row):
    lane = lax.broadcasted_iota(jnp.int32, row.shape, 1)
    sh = 1
    while sh < N_EXPERTS:
        row = row + jnp.where(lane >= sh, pltpu.roll(row, sh, 1), 0.0)
        sh *= 2
    return row


def _rank_kernel(rt_ref, tri_ref, pos_ref, meta_ref, cnt_ref, carry_ref):
    i = pl.program_id(0)

    @pl.when(i == 0)
    def _():
        carry_ref[...] = jnp.zeros_like(carry_ref)

    oh1, oh2 = _onehots(rt_ref[...])
    both = oh1 + oh2
    carry = carry_ref[0:1, :]
    tile_cnt = jnp.sum(both, axis=0, keepdims=True)
    tile_cnt = tile_cnt + (tile_cnt - 2.0 * jnp.floor(tile_cnt * 0.5))
    tile_off = _lane_cumsum(tile_cnt) - tile_cnt
    where = _dot(tri_ref[...], both.astype(BF16)) + tile_off
    p1 = jnp.sum(oh1 * where, axis=-1, keepdims=True)
    p2 = jnp.sum(oh2 * where, axis=-1, keepdims=True)
    lane = lax.broadcasted_iota(jnp.int32, both.shape, 1)
    pos_ref[...] = jnp.where(lane == 0, p1, jnp.where(lane == 1, p2, 0.0))
    row = lax.broadcasted_iota(jnp.int32, meta_ref.shape, 0)
    meta_ref[...] = jnp.where(row == 0, tile_off, jnp.where(row == 1, tile_cnt, jnp.where(row == 2, carry, 0.0)))
    total = carry + tile_cnt
    carry_ref[...] = jnp.broadcast_to(total, carry_ref.shape)
    cnt_ref[...] = jnp.broadcast_to(total, cnt_ref.shape)


def _runs_kernel(meta_ref, cnt_ref, runs_ref, be_ref):
    lane = lax.broadcasted_iota(jnp.int32, (1, LANES), 1)
    counts = cnt_ref[0:1, :]
    padded = jnp.floor((counts + (RUN_ROWS - 1 + MOE_BLK - 1)) * (1.0 / MOE_BLK)) * MOE_BLK
    padded = jnp.where(lane < N_EXPERTS, padded, 0.0)
    ends = _lane_cumsum(padded)
    starts = ends - padded
    for t in range(NT):
        m = meta_ref[t]
        row = lax.broadcasted_iota(jnp.int32, m.shape, 0)
        m = jnp.where(row == 2, m + starts, m)
        m = jnp.where(row == 3, starts + counts, jnp.where(row == 4, padded - counts, m))
        runs_ref[t] = m.astype(jnp.int32)
    blk = lax.broadcasted_iota(jnp.int32, be_ref.shape, 0).astype(F32) * MOE_BLK
    lane_b = lax.broadcasted_iota(jnp.int32, be_ref.shape, 1)
    done = jnp.where((ends <= blk) & (lane_b < N_EXPERTS), 1.0, 0.0)
    be = jnp.minimum(jnp.sum(done, axis=-1, keepdims=True), N_EXPERTS - 1.0)
    nblk = jnp.max(jnp.where(lane_b == N_EXPERTS - 1, ends, 0.0), axis=-1, keepdims=True) * (1.0 / MOE_BLK)
    be_ref[...] = jnp.where(lane_b == 0, be, jnp.where(lane_b == 1, nblk, 0.0)).astype(jnp.int32)


def _plan_call(route, tri):
    tok = pl.BlockSpec((TM, LANES), lambda i: (i, 0))
    pos, meta, counts = pl.pallas_call(
        _rank_kernel,
        out_shape=[jax.ShapeDtypeStruct((N_TOK, LANES), F32), jax.ShapeDtypeStruct((NT, 8, LANES), F32),
                   jax.ShapeDtypeStruct((8, LANES), F32)],
        grid=(NT,),
        in_specs=[tok, _const_spec((TM, TM))],
        out_specs=[tok, pl.BlockSpec((None, 8, LANES), lambda i: (i, 0, 0)),
                   pl.BlockSpec((8, LANES), lambda i: (0, 0))],
        scratch_shapes=[pltpu.VMEM((8, LANES), F32)],
        compiler_params=_cparams(("arbitrary",)),
        name="moe_rank",
    )(route, tri)
    runs, blk = pl.pallas_call(
        _runs_kernel,
        out_shape=[jax.ShapeDtypeStruct((NT, 8, LANES), jnp.int32),
                   jax.ShapeDtypeStruct((256, LANES), jnp.int32)],
        name="moe_runs",
    )(meta, counts)
    runs_flat = runs[:, 0:RUN_FIELDS, 0:N_EXPERTS].reshape(NT, 1, RUN_FIELDS * N_EXPERTS)
    return pos, runs_flat, blk[:N_MOE_BLOCKS, 0], blk[0:1, 1]


def _pack_pairs(x):
    half = x.shape[1] // 2
    lo = pltpu.bitcast(x[:, :half], jnp.uint32)
    hi = pltpu.bitcast(x[:, half:], jnp.uint32)
    return (lo >> 16) | (hi & jnp.uint32(0xFFFF0000))


def _unpack_pairs(w):
    lo = pltpu.bitcast(w << 16, F32)
    hi = pltpu.bitcast(w & jnp.uint32(0xFFFF0000), F32)
    return jnp.concatenate([lo, hi], axis=1).astype(BF16)


def _run_fields(runs_ref, e):
    return runs_ref[0, e], runs_ref[0, N_EXPERTS + e], runs_ref[0, 2 * N_EXPERTS + e]


def _store_rows(lin_ref, packed):
    rows = packed.shape[0]
    for c in range(PK):
        lin_ref[pl.ds(c, rows, stride=PK), :] = packed[:, c * LANES:(c + 1) * LANES]


def _load_rows(lin_ref, rows):
    return jnp.concatenate([lin_ref[pl.ds(c, rows, stride=PK), :] for c in range(PK)], axis=1)


def _lin(ref, row, nrows):
    return ref.at[pl.ds(pl.multiple_of(row * PK, 8), nrows * PK), :]


TAIL_PIECES = (512, 256, 128, 64, 32, 16, 8, 4, 2)


def _dispatch_kernel(runs_ref, prev_runs_ref, pos_ref, h2_ref, xs_ref, buf0, buf1, zero_ref, sem0, sem1, zsem,
                     ssem):
    i = pl.program_id(0)
    last = NT - 1

    def tail_copies(act):
        def per_expert(e, carry):
            row = runs_ref[0, 3 * N_EXPERTS + e]
            n = runs_ref[0, 4 * N_EXPERTS + e]
            done = jnp.int32(0)
            for size in TAIL_PIECES:
                @pl.when((n & size) != 0)
                def _(done=done, size=size):
                    act(pltpu.make_async_copy(zero_ref.at[pl.ds(0, size * PK), :], _lin(xs_ref, row + done, size), zsem))
                done = done + (n & size)
            return carry

        lax.fori_loop(0, N_EXPERTS, per_expert, 0)

    def spare_copies(act):
        used = runs_ref[0, 3 * N_EXPERTS + N_EXPERTS - 1] + runs_ref[0, 4 * N_EXPERTS + N_EXPERTS - 1]

        def spare_block(k, carry):
            act(pltpu.make_async_copy(zero_ref.at[pl.ds(0, MOE_BLK * PK), :],
                                      _lin(xs_ref, used + k * MOE_BLK, MOE_BLK), ssem))
            return carry

        lax.fori_loop(0, N_MOE_BLOCKS - used // MOE_BLK, spare_block, 0)

    start = lambda d: d.start()
    wait = lambda d: d.wait()

    def chunks_wait(runs, buf, sem):
        chunks = lax.fori_loop(0, N_EXPERTS,
                               lambda e, acc: acc + (runs[0, N_EXPERTS + e] + RUN_ROWS - 1) // RUN_ROWS, jnp.int32(0))
        unit = pltpu.make_async_copy(_lin(buf, 0, RUN_ROWS), _lin(xs_ref, 0, RUN_ROWS), sem)
        lax.fori_loop(0, chunks, lambda k, c: (unit.wait(), c)[1], 0)

    @pl.when(i == 0)
    def _():
        for buf in (buf0, buf1):
            buf[TS * PK:, :] = jnp.zeros((RUN_ROWS * PK, LANES), jnp.uint32)
        zero_ref[...] = jnp.zeros_like(zero_ref)
        tail_copies(start)
        spare_copies(start)

    def run_copies(runs, buf, sem, act):
        def chunk_copy(off, dst, k):
            return pltpu.make_async_copy(_lin(buf, off + k * RUN_ROWS, RUN_ROWS),
                                         _lin(xs_ref, dst + k * RUN_ROWS, RUN_ROWS), sem)

        def per_expert(e, carry):
            off, n, dst = _run_fields(runs, e)
            lax.fori_loop(0, (n + RUN_ROWS - 1) // RUN_ROWS, lambda k, c: (act(chunk_copy(off, dst, k)), c)[1], 0)
            return carry

        lax.fori_loop(0, N_EXPERTS, per_expert, 0)

    pos = pos_ref[...]
    col = lax.broadcasted_iota(jnp.int32, (TM, TS), 1).astype(F32)
    sel = jnp.where((col == pos[:, 0:1]) | (col == pos[:, 1:2]), 1.0, 0.0).astype(BF16)
    srt = lax.dot_general(sel, h2_ref[...], (((0,), (0,)), ((), ())), preferred_element_type=F32)
    packed = _pack_pairs(srt)

    for par, (buf, sem, obuf, osem) in enumerate(((buf0, sem0, buf1, sem1), (buf1, sem1, buf0, sem0))):
        @pl.when(i % 2 == par)
        def _(buf=buf, sem=sem, obuf=obuf, osem=osem):
            _store_rows(buf, packed)

            @pl.when(i == 0)
            def _():
                tail_copies(wait)

            @pl.when(i > 0)
            def _():
                chunks_wait(prev_runs_ref, obuf, osem)

            run_copies(runs_ref, buf, sem, start)

            @pl.when(i == last)
            def _():
                chunks_wait(runs_ref, buf, sem)
                spare_copies(wait)


def _dispatch_call(runs_flat, pos, h2):
    runs_spec = lambda shift: pl.BlockSpec((None, 1, RUN_FIELDS * N_EXPERTS), lambda i: (jnp.maximum(i - shift, 0), 0, 0),
                                           memory_space=pltpu.SMEM)
    sorted_buf = pltpu.VMEM(((TS + RUN_ROWS) * PK, LANES), jnp.uint32)
    return pl.pallas_call(
        _dispatch_kernel,
        out_shape=jax.ShapeDtypeStruct((N_SLOTS * PK, LANES), jnp.uint32),
        grid=(NT,),
        in_specs=[runs_spec(0), runs_spec(1),
                  pl.BlockSpec((TM, LANES), lambda i: (i, 0)),
                  pl.BlockSpec((TM, D), lambda i: (i, 0))],
        out_specs=pl.BlockSpec(memory_space=pl.ANY),
        scratch_shapes=[sorted_buf, sorted_buf, pltpu.VMEM((TAIL_PIECES[0] * PK, LANES), jnp.uint32),
                        pltpu.SemaphoreType.DMA, pltpu.SemaphoreType.DMA, pltpu.SemaphoreType.DMA,
                        pltpu.SemaphoreType.DMA],
        compiler_params=_cparams(("arbitrary",)),
        name="moe_dispatch",
    )(runs_flat, runs_flat, pos, h2)


def _expert_kernel(be_ref, nu_ref, xs_ref, wg_ref, wu_ref, wd_ref, ys_ref, wgb_ref, wub_ref, wdb_ref):
    b = pl.program_id(0)

    @pl.when(jnp.logical_or(b == 0, be_ref[b] != be_ref[jnp.maximum(b - 1, 0)]))
    def _():
        wgb_ref[...] = wg_ref[...].astype(BF16)
        wub_ref[...] = wu_ref[...].astype(BF16)
        wdb_ref[...] = wd_ref[...].astype(BF16)

    @pl.when(b < nu_ref[0])
    def _():
        xb = _unpack_pairs(_load_rows(xs_ref, MOE_BLK))
        g = _dot(xb, wgb_ref[...])
        u = _dot(xb, wub_ref[...])
        hmid = (g * _sigmoid(g)) * u
        y = _dot(hmid.astype(BF16), wdb_ref[...])
        _store_rows(ys_ref, _pack_pairs(y.astype(BF16).astype(F32)))

    @pl.when(b >= nu_ref[0])
    def _():
        ys_ref[...] = jnp.zeros_like(ys_ref)


def _expert_call(blk_e, n_used, xs, wg, wu, wd, l):
    wspec = lambda k, n: pl.BlockSpec((None, None, k, n), lambda b, be, nu: (l, be[b], 0, 0))
    return pl.pallas_call(
        _expert_kernel,
        out_shape=jax.ShapeDtypeStruct((N_SLOTS * PK, LANES), jnp.uint32),
        grid_spec=pltpu.PrefetchScalarGridSpec(
            num_scalar_prefetch=2,
            grid=(N_MOE_BLOCKS,),
            in_specs=[pl.BlockSpec((MOE_BLK * PK, LANES), lambda b, be, nu: (jnp.minimum(b, nu[0] - 1), 0)),
                      wspec(D, EXPERT_HIDDEN), wspec(D, EXPERT_HIDDEN), wspec(EXPERT_HIDDEN, D)],
            out_specs=pl.BlockSpec((MOE_BLK * PK, LANES), lambda b, be, nu: (b, 0)),
            scratch_shapes=[pltpu.VMEM((D, EXPERT_HIDDEN), BF16), pltpu.VMEM((D, EXPERT_HIDDEN), BF16),
                            pltpu.VMEM((EXPERT_HIDDEN, D), BF16)]),
        compiler_params=_cparams(("arbitrary",)),
        name="moe_experts",
    )(blk_e, n_used, xs, wg, wu, wd)


FETCH_ROWS = 64
RUN_PIECES = (32, 16, 8, 4, 2)


def _combine_kernel(runs_ref, next_runs_ref, ys_ref, pos_ref, x_ref, rt_ref, mod_ref, o_ref, buf0, buf1, sem0, sem1,
                    *, n_tiles):
    i = pl.program_id(0)
    last = n_tiles - 1

    def run_copies(runs, buf, sem, act):
        def piece(off, dst, size):
            return pltpu.make_async_copy(_lin(ys_ref, dst, size), _lin(buf, off, size), sem)

        def per_expert(e, carry):
            off, n, dst = _run_fields(runs, e)
            whole = n // FETCH_ROWS

            def chunk(k, c):
                act(piece(off + k * FETCH_ROWS, dst + k * FETCH_ROWS, FETCH_ROWS))
                return c

            lax.fori_loop(0, whole, chunk, 0)
            done = whole * FETCH_ROWS
            for size in RUN_PIECES:
                @pl.when((n & size) != 0)
                def _(done=done, size=size):
                    act(piece(off + done, dst + done, size))
                done = done + (n & size)
            return carry

        lax.fori_loop(0, N_EXPERTS, per_expert, 0)

    start = lambda d: d.start()

    def rows_wait(runs, buf, sem):
        total = lax.fori_loop(0, N_EXPERTS, lambda e, acc: acc + runs[0, N_EXPERTS + e], jnp.int32(0))

        def unit(size):
            return pltpu.make_async_copy(_lin(ys_ref, 0, size), _lin(buf, 0, size), sem)

        lax.fori_loop(0, total // FETCH_ROWS, lambda k, c: (unit(FETCH_ROWS).wait(), c)[1], 0)
        for size in RUN_PIECES:
            @pl.when((total & size) != 0)
            def _(size=size):
                unit(size).wait()

    @pl.when(i == 0)
    def _():
        buf0[...] = jnp.zeros_like(buf0)
        buf1[...] = jnp.zeros_like(buf1)
        run_copies(runs_ref, buf0, sem0, start)

    pos = pos_ref[...]
    rt = rt_ref[...]
    col = lax.broadcasted_iota(jnp.int32, (TM, TS), 1).astype(F32)
    pick = (jnp.where(col == pos[:, 0:1], rt[:, TOP_K:TOP_K + 1], 0.0)
            + jnp.where(col == pos[:, 1:2], rt[:, TOP_K + 1:TOP_K + 2], 0.0)).astype(BF16)

    for par, (buf, sem, obuf, osem) in enumerate(((buf0, sem0, buf1, sem1), (buf1, sem1, buf0, sem0))):
        @pl.when(i % 2 == par)
        def _(buf=buf, sem=sem, obuf=obuf, osem=osem):
            @pl.when(i < last)
            def _():
                run_copies(next_runs_ref, obuf, osem, start)

            rows_wait(runs_ref, buf, sem)
            ysb = _unpack_pairs(_load_rows(buf, TS))
            o_ref[...] = x_ref[...] + mod_ref[5:6, :] * _dot(pick, ysb)


def _combine_call(runs_flat, ys, pos, x, route, mods_l, n_tiles):
    tok = lambda w: pl.BlockSpec((TM, w), lambda i: (i, 0))
    runs_spec = lambda shift: pl.BlockSpec((None, 1, RUN_FIELDS * N_EXPERTS),
                                           lambda i: (jnp.minimum(i + shift, n_tiles - 1), 0, 0),
                                           memory_space=pltpu.SMEM)
    sorted_buf = pltpu.VMEM((TS * PK, LANES), jnp.uint32)
    return pl.pallas_call(
        functools.partial(_combine_kernel, n_tiles=n_tiles),
        out_shape=jax.ShapeDtypeStruct((n_tiles * TM, D), F32),
        grid=(n_tiles,),
        in_specs=[runs_spec(0), runs_spec(1),
                  pl.BlockSpec(memory_space=pl.ANY),
                  tok(LANES), tok(D), tok(LANES),
                  pl.BlockSpec((None, 6, D), lambda i: (_mod_row(i), 0, 0))],
        out_specs=tok(D),
        scratch_shapes=[sorted_buf, sorted_buf, pltpu.SemaphoreType.DMA, pltpu.SemaphoreType.DMA],
        compiler_params=_cparams(("arbitrary",)),
        name="moe_combine",
    )(runs_flat, runs_flat, ys, pos, x, route, mods_l)


def _rope_tables():
    nf = HEAD_DIM // 4
    inv = ROPE_BASE ** (-jnp.arange(nf, dtype=F32) / nf)
    t = jnp.arange(S)
    row = (t // GRID_W).astype(F32)[:, None] * inv[None, :]
    col = (t % GRID_W).astype(F32)[:, None] * inv[None, :]
    zero = jnp.zeros_like(row)
    cos = jnp.concatenate([jnp.cos(row), jnp.cos(row), jnp.cos(col), jnp.cos(col)], axis=1)
    sa = jnp.concatenate([-jnp.sin(row), zero, -jnp.sin(col), zero], axis=1)
    sb = jnp.concatenate([zero, jnp.sin(row), zero, jnp.sin(col)], axis=1)
    ident = (jnp.ones((TM, HEAD_DIM), F32), jnp.zeros((TM, HEAD_DIM), F32), jnp.zeros((TM, HEAD_DIM), F32))
    return tuple(jnp.tile(jnp.concatenate([a, b], axis=0), (1, LANES // HEAD_DIM))
                 for a, b in zip((cos, sa, sb), ident))


def _fourier_tables():
    s1 = np.arange(FS1)
    ang1 = 2.0 * np.pi * np.outer(s1, s1) / FS1
    w1 = np.concatenate([np.cos(ang1), -np.sin(ang1)], axis=0) / np.sqrt(S)
    k1 = np.arange(FS1)[:, None, None]
    k2 = np.arange(FS2)[None, :, None]
    s2 = np.arange(FS2)[None, None, :]
    ang2 = 2.0 * np.pi * ((k1 + FS1 * k2) * s2 % S) / S
    c2, sn2 = np.cos(ang2), np.sin(ang2)
    ta = np.concatenate([c2, -sn2], axis=1)
    tb = np.concatenate([sn2, c2], axis=1)
    sc = np.arange(C)
    angc = 2.0 * np.pi * np.outer(sc, sc) / C
    wc = np.concatenate([np.cos(angc), -np.sin(angc)], axis=0) / np.sqrt(C)
    return tuple(jnp.asarray(a, F32).astype(BF16) for a in (w1, ta, tb, wc))


def _channel_dft():
    cidx = np.arange(FOURIER_GROUP_W)
    ang = 2.0 * np.pi * np.outer(cidx, cidx) / FOURIER_GROUP_W
    eye = np.eye(FOURIER_W // FOURIER_GROUP_W)
    cw = np.kron(eye, np.cos(ang)) / np.sqrt(FOURIER_GROUP_W)
    sw = np.kron(eye, np.sin(ang)) / np.sqrt(FOURIER_GROUP_W)
    return jnp.asarray(np.concatenate([cw, sw], axis=0), F32)


def _pool_bands():
    t = np.arange(TP)[:, None]
    main, halo = [], []
    for w in POOL_WINDOWS:
        def hit(j):
            return ((j - t >= -(w // 2)) & (j - t <= w // 2 - 1)).astype(np.float32)
        main.append(hit(np.arange(TP)[None, :]))
        halo.append(np.concatenate([hit(np.arange(-HALO, 0)[None, :]),
                                    hit(np.arange(TP, TP + HALO)[None, :])], axis=1))
    return (jnp.asarray(np.stack(main), F32).astype(BF16), jnp.asarray(np.stack(halo), F32).astype(BF16))


def _pool_inv_counts():
    win = np.repeat(np.array(POOL_WINDOWS), POOL_GROUP_W)[None, :]

    def table(pos0, seq_len):
        pos = (pos0 + np.arange(TP))[:, None]
        lo = np.clip(pos - win // 2, 0, seq_len)
        hi = np.clip(pos - win // 2 + win, 0, seq_len)
        return 1.0 / (hi - lo)

    tabs = [table(TP, S), table(0, S), table(S - TP, S), table(0, C)]
    return jnp.asarray(np.stack(tabs), F32)


def _conv_shifts():
    i = np.arange(CONV_WIN)
    return jnp.asarray(np.stack([(i[None, :] == i[:, None] + s) for s in range(1, 8)]), F32).astype(BF16)


def _fold_kernel(a_ref, b_ref, o_ref):
    a, b = a_ref[...], b_ref[...]
    a_hi, b_hi = a.astype(BF16), b.astype(BF16)
    a_lo = (a - a_hi.astype(F32)).astype(BF16)
    b_lo = (b - b_hi.astype(F32)).astype(BF16)
    o_ref[...] = (_dot(a_hi, b_hi) + _dot(a_lo, b_hi) + _dot(a_hi, b_lo)).astype(BF16)


def _fold_fourier_weights(dftw, w_br_fourier):
    nl = w_br_fourier.shape[0]
    return pl.pallas_call(
        _fold_kernel,
        out_shape=jax.ShapeDtypeStruct((nl, 2 * FOURIER_W, D), BF16),
        grid=(nl,),
        in_specs=[pl.BlockSpec((2 * FOURIER_W, FOURIER_W), lambda l: (0, 0)),
                  pl.BlockSpec((None, FOURIER_W, D), lambda l: (l, 0, 0))],
        out_specs=pl.BlockSpec((None, 2 * FOURIER_W, D), lambda l: (l, 0, 0)),
        compiler_params=_cparams(("arbitrary",)),
        name="fold_fourier_proj",
    )(dftw, w_br_fourier)


def _block_diag(blocks):
    n, r, c = blocks.shape
    eye = jnp.eye(n, dtype=blocks.dtype)
    return (blocks[:, :, None, :] * eye[:, None, :, None]).reshape(n * r, n * c)


def kernel(x, c, ctx, c_ctx, w_ada, b_ada, g_norm_mix, g_norm_ffn, w_in, g_q, g_k, sink, w_br_attn,
           w_br_fourier, pool_w, pool_scale, w_br_pool, conv_w, conv_b, cn_g, cn_b, w_br_conv, w_gate,
           b_gate, w_out, w_router_grp, b_router_grp, w_router_exp, b_router_exp, w_e_gate, w_e_up,
           w_e_down):
    xs = jnp.concatenate([x.reshape(N_LAT, D), ctx.reshape(N_CTX, D)], axis=0)
    nl = w_ada.shape[0]
    mods = _ada_all(c, c_ctx, w_ada, b_ada).reshape(nl, 8, 6, D)
    rope_tabs = _rope_tables()
    four_tabs = _fourier_tables()
    band_main, band_halo = _pool_bands()
    inv_cnt = _pool_inv_counts()
    shifts = _conv_shifts()
    wf_all = _fold_fourier_weights(_channel_dft(), w_br_fourier)
    bd = jnp.asarray(np.kron(np.eye(LANES // HEAD_DIM), np.ones((HEAD_DIM, HEAD_DIM))), F32).astype(BF16)
    tri = jnp.asarray(np.tril(np.ones((TM, TM)), -1), F32).astype(BF16)
    rpad = jnp.zeros((nl, D, LANES - N_GROUPS - N_EXPERTS), F32)
    w_router = jnp.concatenate([w_router_grp, w_router_exp, rpad], axis=-1)
    r_hi = w_router.astype(BF16)
    r_lo = (w_router - r_hi.astype(F32)).astype(BF16)
    r_b = jnp.concatenate([b_router_grp, b_router_exp, rpad[:, 0, :]], axis=-1).reshape(nl, 1, LANES)
    stacked = tuple(w.astype(BF16) for w in (w_br_attn,)) + (wf_all,) + tuple(
        w.astype(BF16) for w in (w_br_pool, w_br_conv, w_gate, w_out)) + (r_hi, r_lo)

    for l in range(nl):
        mods_l = mods[l]
        gn = g_norm_mix[l].reshape(1, D)
        q, kv, f, p, u = _proj_call(xs, mods_l, gn, w_in, l, rope_tabs,
                                    jnp.tile(g_q[l], 2).reshape(1, LANES),
                                    jnp.tile(g_k[l], 2).reshape(1, LANES), bd)
        a_re, a_im = _fourier_stage1_call(f, four_tabs)
        a_lat, a_ctx = _attn_call(sink[l], q, kv)
        h_lat, h_ctx = _fourier_stage2_call(a_re, a_im, f, four_tabs, a_lat)
        zc, cact = _poolconv_call(p, u, band_main, band_halo, inv_cnt, _block_diag(pool_w[l]).astype(BF16),
                                  pool_scale[l].reshape(1, POOL_W), shifts, conv_w[l], conv_b[l].reshape(1, CONV_W),
                                  cn_g[l].reshape(1, CONV_W), cn_b[l].reshape(1, CONV_W))
        small = (b_gate[l].reshape(1, 4 * D), g_norm_ffn[l].reshape(1, D), r_b[l])
        xs, h2, route = _mix_call(xs, mods_l, gn, a_lat, a_ctx, h_lat, h_ctx, zc, cact, l, stacked, small)
        pos, runs_flat, blk_e, n_used = _plan_call(route, tri)
        slots = _dispatch_call(runs_flat, pos, h2)
        ys = _expert_call(blk_e, n_used, slots, w_e_gate, w_e_up, w_e_down, l)
        xs = _combine_call(runs_flat, ys, pos, xs, route, mods_l, LAT_TILES if l == nl - 1 else NT)
    return xs.reshape(B, S, D)
```

```python
import functools

import numpy as np
import jax
import jax.numpy as jnp
from jax import lax
from jax.experimental import pallas as pl
from jax.experimental.pallas import tpu as pltpu

F32 = jnp.float32
BF16 = jnp.bfloat16

D = 1024
B = 2
S = 8192
C = 256
GRID_W = 64
HEAD_DIM = 64
N_Q_HEADS = 8
N_KV_HEADS = 2
GQA = N_Q_HEADS // N_KV_HEADS
WINDOW = 128
ATTN_BLK = 128
ATTN_QB = 4
ROPE_BASE = 10000.0
Q_W = 512
KV_W = 128
FOURIER_W = 640
FOURIER_GROUP_W = 160
POOL_W = 640
POOL_GROUP_W = 160
POOL_WINDOWS = (2, 4, 8, 16)
CONV_W = 512
CONV_K = 31
PROJ_W = 3072
N_GROUPS = 4
EPG = 8
N_EXPERTS = 32
TOP_K = 2
EXPERT_HIDDEN = 512
MOE_BLK = 512
EPS = 1e-6
NEG_INF = -1e30
LOG2E = 1.4426950408889634

N_LAT = B * S
N_CTX = B * C
N_TOK = N_LAT + N_CTX
TM = 512
NT = N_TOK // TM
LAT_TILES = N_LAT // TM
TILES_PER_BATCH = S // TM
TP = 256
NTP = N_TOK // TP
HALO = 16
CONV_WIN = TP // 2 + 2 * HALO
N_ASSIGN = N_TOK * TOP_K
RUN_ROWS = 48
RUN_FIELDS = 5
PK = D // 2 // 128
TS = 1152
N_MOE_BLOCKS = (N_ASSIGN + NT * N_EXPERTS + N_EXPERTS * (RUN_ROWS - 1 + MOE_BLK - 1)) // MOE_BLK
N_SLOTS = N_MOE_BLOCKS * MOE_BLK
FS1 = 64
FS2 = 128
F1_ROWS = 32
F2_K1 = 8
LANES = 128
VMEM_LIMIT = 56 * 1024 * 1024


def _cparams(sem, vmem=VMEM_LIMIT):
    return pltpu.CompilerParams(dimension_semantics=sem, vmem_limit_bytes=vmem)


def _const_spec(shape):
    nd = len(shape)
    return pl.BlockSpec(shape, lambda *_: (0,) * nd, pipeline_mode=pl.Buffered(1))


def _dot(a, b):
    return jnp.dot(a, b, preferred_element_type=F32)


def _modulate(x, g, shift, scale):
    y = x * lax.rsqrt(jnp.mean(x * x, axis=-1, keepdims=True) + EPS)
    return (y * g) * (1.0 + scale) + shift


def _sigmoid(x):
    return 1.0 / (1.0 + jnp.exp(-x))


def _ada_kernel(ct_ref, w_ref, b_ref, o_ref):
    ct = ct_ref[...]
    s = ct * _sigmoid(ct)
    w = w_ref[...]
    rows = [jnp.sum(w * s[:, r:r + 1], axis=0, keepdims=True) for r in range(3)]
    rows.append(jnp.zeros((5, w.shape[1]), F32))
    o_ref[...] = jnp.concatenate(rows, axis=0) + b_ref[...]


def _ada_all(c, c_ctx, w_ada, b_ada):
    ct = jnp.concatenate([c, c_ctx[None, :], jnp.zeros((5, D), F32)], axis=0).T
    cols = 1536
    nl = w_ada.shape[0]
    return pl.pallas_call(
        _ada_kernel,
        out_shape=jax.ShapeDtypeStruct((nl, 8, 6 * D), F32),
        grid=(nl, 6 * D // cols),
        in_specs=[pl.BlockSpec((D, 8), lambda l, j: (0, 0)),
                  pl.BlockSpec((None, D, cols), lambda l, j: (l, 0, j)),
                  pl.BlockSpec((None, 1, cols), lambda l, j: (l, 0, j))],
        out_specs=pl.BlockSpec((None, 8, cols), lambda l, j: (l, 0, j)),
        compiler_params=_cparams(("arbitrary", "arbitrary")),
        name="adaln",
    )(ct, w_ada, b_ada.reshape(nl, 1, 6 * D))


def _head_rms(t, g128, bd):
    outs = []
    for j in range(t.shape[1] // LANES):
        blk = t[:, j * LANES:(j + 1) * LANES]
        ss = _dot((blk * blk).astype(BF16), bd)
        outs.append(blk * lax.rsqrt(ss * (1.0 / HEAD_DIM) + EPS) * g128)
    return outs


def _rope(blocks, cos, sa, sb):
    outs = []
    for blk in blocks:
        up = pltpu.roll(blk, LANES - 16, 1)
        dn = pltpu.roll(blk, 16, 1)
        outs.append(blk * cos + up * sa + dn * sb)
    return outs


def _proj_kernel(x_ref, mod_ref, gn_ref, w_ref, cos_ref, sa_ref, sb_ref, gq_ref, gk_ref, bd_ref,
                 q_ref, kv_ref, f_ref, p_ref, u_ref, wbf_ref):
    @pl.when(pl.program_id(0) == 0)
    def _():
        wbf_ref[...] = w_ref[...].astype(BF16)

    m = mod_ref[...]
    hb = _modulate(x_ref[...], gn_ref[...], m[0:1], m[1:2]).astype(BF16)
    cos, sa, sb, bd = cos_ref[...], sa_ref[...], sb_ref[...], bd_ref[...]
    o_kv, o_f, o_a = Q_W, Q_W + 2 * KV_W, Q_W + 2 * KV_W + FOURIER_W + POOL_W
    qkv = _dot(hb, wbf_ref[:, 0:o_f])
    fp = _dot(hb, wbf_ref[:, o_f:o_a])
    q = _rope(_head_rms(qkv[:, 0:Q_W], gq_ref[...], bd), cos, sa, sb)
    q_ref[...] = (jnp.concatenate(q, axis=1) * (LOG2E * HEAD_DIM ** -0.5)).astype(BF16)
    k = _rope(_head_rms(qkv[:, o_kv:o_kv + KV_W], gk_ref[...], bd), cos, sa, sb)
    kv_ref[:, 0:KV_W] = k[0].astype(BF16)
    kv_ref[:, KV_W:2 * KV_W] = qkv[:, o_kv + KV_W:o_f].astype(BF16)
    ag = _dot(hb, wbf_ref[:, o_a:PROJ_W])
    f_ref[...] = fp[:, 0:FOURIER_W].astype(BF16)
    p_ref[...] = fp[:, FOURIER_W:].astype(BF16)
    u_ref[...] = (ag[:, 0:CONV_W] * _sigmoid(ag[:, CONV_W:])).astype(BF16)


def _mod_row(i):
    return jnp.minimum(i // TILES_PER_BATCH, 2)


def _layer_spec(shape, l):
    nd = len(shape)
    return pl.BlockSpec((None,) + tuple(shape), lambda *_: (l,) + (0,) * nd, pipeline_mode=pl.Buffered(1))


def _proj_call(x, mods_l, gn, w_in, l, rope_tabs, gq128, gk128, bd):
    cos, sa, sb = rope_tabs
    tok = lambda w: pl.BlockSpec((TM, w), lambda i: (i, 0))
    rope_spec = pl.BlockSpec((TM, LANES), lambda i: (jnp.where(i < LAT_TILES, i % TILES_PER_BATCH,
                                                               TILES_PER_BATCH), 0))
    widths = (Q_W, 2 * KV_W, FOURIER_W, POOL_W, CONV_W)
    return pl.pallas_call(
        _proj_kernel,
        out_shape=[jax.ShapeDtypeStruct((N_TOK, w), BF16) for w in widths],
        grid=(NT,),
        in_specs=[tok(D),
                  pl.BlockSpec((None, 6, D), lambda i: (_mod_row(i), 0, 0)),
                  _const_spec((1, D)),
                  _layer_spec((D, PROJ_W), l),
                  rope_spec, rope_spec, rope_spec,
                  _const_spec((1, LANES)), _const_spec((1, LANES)),
                  _const_spec((LANES, LANES))],
        out_specs=[tok(w) for w in widths],
        scratch_shapes=[pltpu.VMEM((D, PROJ_W), BF16)],
        compiler_params=_cparams(("arbitrary",)),
        name="proj",
    )(x, mods_l, gn, w_in, cos, sa, sb, gq128, gk128, bd)


def _attend_many(jobs, sink_ref):
    lane = lax.broadcasted_iota(jnp.int32, (ATTN_BLK, LANES), 1)
    chains = []
    for q, kv_blocks, biases in jobs:
        for j in range(N_KV_HEADS):
            ks = slice(j * HEAD_DIM, (j + 1) * HEAD_DIM)
            vs = slice(KV_W + j * HEAD_DIM, KV_W + (j + 1) * HEAD_DIM)
            kj = jnp.concatenate([blk[:, ks] for blk in kv_blocks], axis=0)
            vj = jnp.concatenate([blk[:, vs] for blk in kv_blocks], axis=0)
            vaug = jnp.concatenate([vj, jnp.ones_like(vj)], axis=1)
            qs = jnp.concatenate([q[:, (j * GQA + g) * HEAD_DIM:(j * GQA + g + 1) * HEAD_DIM]
                                  for g in range(GQA)], axis=0)
            s = lax.dot_general(qs, kj, (((1,), (1,)), ((), ())), preferred_element_type=F32)
            chains.append((j, s, vaug, kv_blocks, biases))
    soft = []
    for j, s, vaug, kv_blocks, biases in chains:
        probs, sink_terms = [], []
        for g in range(GQA):
            sg = s[g * ATTN_BLK:(g + 1) * ATTN_BLK]
            pieces, col = [], 0
            for blk, bias in zip(kv_blocks, biases):
                piece = sg[:, col:col + blk.shape[0]]
                pieces.append(piece if bias is None else piece + bias)
                col += blk.shape[0]
            sg = jnp.concatenate(pieces, axis=1)
            sk = sink_ref[j * GQA + g] * LOG2E
            mx = jnp.maximum(jnp.max(sg, axis=-1, keepdims=True), sk)
            probs.append(jnp.exp2(sg - mx).astype(BF16))
            sink_terms.append(jnp.exp2(sk - mx))
        soft.append((jnp.concatenate(probs, axis=0), vaug, sink_terms))
    heads = []
    for p, vaug, sink_terms in soft:
        o = _dot(p, vaug)
        for g in range(GQA):
            og = o[g * ATTN_BLK:(g + 1) * ATTN_BLK]
            heads.append(og / (og[:, HEAD_DIM:HEAD_DIM + 1] + sink_terms[g]))
    outs = []
    for n in range(len(jobs)):
        hs = heads[n * N_Q_HEADS:(n + 1) * N_Q_HEADS]
        tiles = [jnp.where(lane < HEAD_DIM, hs[2 * t], pltpu.roll(hs[2 * t + 1], HEAD_DIM, 1))
                 for t in range(N_Q_HEADS // 2)]
        outs.append(jnp.concatenate(tiles, axis=1).astype(BF16))
    return outs


def _attn_latent_kernel(sink_ref, q_ref, prev_ref, cur_ref, next_ref, ctx_ref, o_ref):
    n = pl.program_id(1)
    r = lax.broadcasted_iota(jnp.int32, (ATTN_BLK, ATTN_BLK), 0)
    jj = lax.broadcasted_iota(jnp.int32, (ATTN_BLK, ATTN_BLK), 1)
    far = jnp.int32(2 * ATTN_BLK)
    off_prev = jnp.where(n > 0, 0, far)
    off_next = jnp.where(n < S // (ATTN_QB * ATTN_BLK) - 1, 0, far)
    prev_ok = jnp.where(jj - r >= 0, 0.0, NEG_INF)
    next_ok = jnp.where(r - jj >= 0, 0.0, NEG_INF)
    prev_edge = jnp.where(jj - r - off_prev >= 0, 0.0, NEG_INF)
    next_edge = jnp.where(r - jj - off_next >= 0, 0.0, NEG_INF)
    ctx = ctx_ref[...]
    rows = lambda b: slice(b * ATTN_BLK, (b + 1) * ATTN_BLK)
    blocks = [prev_ref[...]] + [cur_ref[rows(b), :] for b in range(ATTN_QB)] + [next_ref[...]]
    jobs = [(q_ref[rows(b), :], [ctx] + blocks[b:b + 3],
             [None, prev_edge if b == 0 else prev_ok, None, next_edge if b == ATTN_QB - 1 else next_ok])
            for b in range(ATTN_QB)]
    for b, out in enumerate(_attend_many(jobs, sink_ref)):
        o_ref[rows(b), :] = out


def _attn_context_kernel(sink_ref, q_ref, ctx_ref, o_ref):
    o_ref[...] = _attend_many([(q_ref[...], [ctx_ref[...]], [None])], sink_ref)[0]


def _attn_call(sink_l, q, kv):
    nb = S // ATTN_BLK
    nq = nb // ATTN_QB
    smem = pl.BlockSpec(memory_space=pltpu.SMEM)
    pair = lambda w: pl.BlockSpec((ATTN_QB * ATTN_BLK, w), lambda b, n: (b * nq + n, 0))
    prev = pl.BlockSpec((ATTN_BLK, 2 * KV_W), lambda b, n: (b * nb + jnp.maximum(ATTN_QB * n - 1, 0), 0))
    nxt = pl.BlockSpec((ATTN_BLK, 2 * KV_W),
                       lambda b, n: (b * nb + jnp.minimum(ATTN_QB * (n + 1), nb - 1), 0))
    ctxs = pl.BlockSpec((C, 2 * KV_W), lambda b, n: (N_LAT // C + b, 0))
    lat = pl.pallas_call(
        _attn_latent_kernel,
        out_shape=jax.ShapeDtypeStruct((N_LAT, Q_W), BF16),
        grid=(B, nq),
        in_specs=[smem, pair(Q_W), prev, pair(2 * KV_W), nxt, ctxs],
        out_specs=pair(Q_W),
        compiler_params=_cparams(("parallel", "parallel")),
        name="attn_latent",
    )(sink_l, q, kv, kv, kv, kv)
    ncb = C // ATTN_BLK
    base = N_LAT // ATTN_BLK
    ctx = pl.pallas_call(
        _attn_context_kernel,
        out_shape=jax.ShapeDtypeStruct((N_CTX, Q_W), BF16),
        grid=(B, ncb),
        in_specs=[smem, pl.BlockSpec((ATTN_BLK, Q_W), lambda b, n: (base + b * ncb + n, 0)), ctxs],
        out_specs=pl.BlockSpec((ATTN_BLK, Q_W), lambda b, n: (b * ncb + n, 0)),
        compiler_params=_cparams(("parallel", "parallel")),
        name="attn_context",
    )(sink_l, q, kv)
    return lat, ctx


def _f1_kernel(w_ref, f_ref, re_ref, im_ref):
    res = lax.dot_general(w_ref[...], f_ref[...], (((1,), (0,)), ((), ())), preferred_element_type=F32)
    re_ref[...] = res[:FS1].astype(BF16)
    im_ref[...] = res[FS1:].astype(BF16)


def _f2_kernel(ta_ref, tb_ref, re_ref, im_ref, after_ref, o_ref):
    del after_ref
    for i in range(F2_K1):
        res = _dot(ta_ref[i], re_ref[i]) + _dot(tb_ref[i], im_ref[i])
        o_ref[i, :, 0:FOURIER_W] = res[:FS2].astype(BF16)
        o_ref[i, :, FOURIER_W:2 * FOURIER_W] = res[FS2:].astype(BF16)


def _fc_kernel(w_ref, f_ref, o_ref):
    res = _dot(w_ref[...], f_ref[...])
    o_ref[:, 0:FOURIER_W] = res[:C].astype(BF16)
    o_ref[:, FOURIER_W:2 * FOURIER_W] = res[C:].astype(BF16)


def _fourier_stage1_call(f, tabs):
    w1 = tabs[0]
    f3 = f.reshape(N_TOK // FS2, FS2, FOURIER_W)
    blk = pl.BlockSpec((FS1, F1_ROWS, FOURIER_W), lambda b, j: (b, j, 0))
    return pl.pallas_call(
        _f1_kernel,
        out_shape=[jax.ShapeDtypeStruct((B * FS1, FS2, FOURIER_W), BF16)] * 2,
        grid=(B, FS2 // F1_ROWS),
        in_specs=[_const_spec((2 * FS1, FS1)), blk],
        out_specs=[blk, blk],
        compiler_params=_cparams(("parallel", "parallel")),
        name="fourier_stage1",
    )(w1, f3)


def _fourier_stage2_call(a_re, a_im, f, tabs, after):
    _, ta, tb, wc = tabs
    nk = FS1 // F2_K1
    aspec = pl.BlockSpec((F2_K1, FS2, FOURIER_W), lambda b, k1: (b * nk + k1, 0, 0))
    tspec = pl.BlockSpec((F2_K1, 2 * FS2, FS2), lambda b, k1: (k1, 0, 0))
    h_t = pl.pallas_call(
        _f2_kernel,
        out_shape=jax.ShapeDtypeStruct((B, FS1, FS2, 2 * FOURIER_W), BF16),
        grid=(B, nk),
        in_specs=[tspec, tspec, aspec, aspec, pl.BlockSpec(memory_space=pl.ANY)],
        out_specs=pl.BlockSpec((None, F2_K1, FS2, 2 * FOURIER_W), lambda b, k1: (b, k1, 0, 0)),
        compiler_params=_cparams(("parallel", "parallel")),
        name="fourier_stage2",
    )(ta, tb, a_re, a_im, after)
    h_lat = jnp.transpose(h_t, (0, 2, 1, 3)).reshape(N_LAT, 2 * FOURIER_W)
    h_ctx = pl.pallas_call(
        _fc_kernel,
        out_shape=jax.ShapeDtypeStruct((N_CTX, 2 * FOURIER_W), BF16),
        grid=(B,),
        in_specs=[_const_spec((2 * C, C)),
                  pl.BlockSpec((C, FOURIER_W), lambda b: (N_LAT // C + b, 0))],
        out_specs=pl.BlockSpec((C, 2 * FOURIER_W), lambda b: (b, 0)),
        compiler_params=_cparams(("parallel",)),
        name="fourier_context",
    )(wc, f)
    return h_lat, h_ctx


def _poolconv_kernel(pc_ref, pp_ref, pn_ref, uc_ref, up_ref, un_ref, bm_ref, bh_ref, ic_ref, pw_ref, ps_ref,
                     sh_ref, cw_ref, cb_ref, cg_ref, cnb_ref, z_ref, a_ref, win0_ref, win1_ref, cv_ref):
    t = pl.program_id(0)
    lat_tiles = N_LAT // TP
    per_seq = S // TP
    is_ctx = t >= lat_tiles
    first = jnp.logical_or(t % per_seq == 0, is_ctx)
    last = jnp.logical_or(t % per_seq == per_seq - 1, is_ctx)

    keep_prev = jnp.where(first, 0.0, 1.0)
    keep_next = jnp.where(last, 0.0, 1.0)

    ub = jnp.concatenate([(up_ref[...].astype(F32) * keep_prev).astype(BF16), uc_ref[...],
                          (un_ref[...].astype(F32) * keep_next).astype(BF16)], axis=0)
    off = HALO - CONV_K // 2
    half_rows = TP // 2
    wins = (win0_ref, win1_ref)
    for hf, win_ref in enumerate(wins):
        window = ub[hf * half_rows:hf * half_rows + CONV_WIN]
        win_ref[0] = window.astype(F32)
        for s in range(1, 8):
            win_ref[s] = _dot(sh_ref[s - 1], window)

    pcur = pc_ref[...]
    halo = jnp.concatenate([pp_ref[...].astype(F32) * keep_prev,
                            pn_ref[...].astype(F32) * keep_next], axis=0).astype(BF16)
    sums = []
    for gi in range(len(POOL_WINDOWS)):
        cs = slice(gi * LANES, (gi + 2) * LANES)
        sums.append(_dot(bm_ref[gi], pcur[:, cs]) + _dot(bh_ref[gi], halo[:, cs]))
    lane_t = lax.broadcasted_iota(jnp.int32, (TP, LANES), 1)
    tiles = [sums[0][:, :LANES]]
    for gi in range(1, len(POOL_WINDOWS)):
        split = gi * POOL_GROUP_W - gi * LANES
        tiles.append(jnp.where(lane_t < split, sums[gi - 1][:, LANES:], sums[gi][:, :LANES]))
    tiles.append(sums[-1][:, LANES:])
    zsum = jnp.concatenate(tiles, axis=1)
    z = zsum * ic_ref[...] - pcur.astype(F32)
    z_ref[...] = (_dot(z.astype(BF16), pw_ref[...]) * ps_ref[...]).astype(BF16)

    for hf, win_ref in enumerate(wins):
        base = hf * half_rows
        for cb in range(CONV_W // LANES):
            cs = slice(cb * LANES, (cb + 1) * LANES)
            acc = jnp.zeros((half_rows, LANES), F32) + cb_ref[:, cs]
            for j in range(CONV_K):
                s, m = (off + j) % 8, (off + j) // 8
                acc = acc + win_ref[s, 8 * m:8 * m + half_rows, cs] * cw_ref[j:j + 1, cs]
            cv_ref[base:base + half_rows, cs] = acc
    cv = cv_ref[...]
    mu = jnp.mean(cv, axis=-1, keepdims=True)
    var = jnp.mean(jnp.square(cv - mu), axis=-1, keepdims=True)
    un = (cv - mu) * lax.rsqrt(var + EPS) * cg_ref[...] + cnb_ref[...]
    a_ref[...] = (un * _sigmoid(un)).astype(BF16)


def _poolconv_call(p, u, band_main, band_halo, inv_cnt, pw_bd, pool_scale, shifts, conv_w, conv_b, cn_g, cn_b):
    nh = TP // HALO
    last_h = N_TOK // HALO - 1
    cur = lambda w: pl.BlockSpec((TP, w), lambda t: (t, 0))
    prv = lambda w: pl.BlockSpec((HALO, w), lambda t: (jnp.maximum(t * nh - 1, 0), 0))
    nxt = lambda w: pl.BlockSpec((HALO, w), lambda t: (jnp.minimum((t + 1) * nh, last_h), 0))
    per_seq = S // TP

    def kind(t):
        return jnp.where(t >= N_LAT // TP, 3, jnp.where(t % per_seq == 0, 1, jnp.where(t % per_seq == per_seq - 1, 2, 0)))

    return pl.pallas_call(
        _poolconv_kernel,
        out_shape=[jax.ShapeDtypeStruct((N_TOK, POOL_W), BF16),
                   jax.ShapeDtypeStruct((N_TOK, CONV_W), BF16)],
        grid=(NTP,),
        in_specs=[cur(POOL_W), prv(POOL_W), nxt(POOL_W), cur(CONV_W), prv(CONV_W), nxt(CONV_W),
                  _const_spec((4, TP, TP)), _const_spec((4, TP, 2 * HALO)),
                  pl.BlockSpec((None, TP, POOL_W), lambda t: (kind(t), 0, 0)),
                  _const_spec((POOL_W, POOL_W)), _const_spec((1, POOL_W)),
                  _const_spec((7, CONV_WIN, CONV_WIN)),
                  _const_spec((CONV_K, CONV_W)), _const_spec((1, CONV_W)),
                  _const_spec((1, CONV_W)), _const_spec((1, CONV_W))],
        out_specs=[cur(POOL_W), cur(CONV_W)],
        scratch_shapes=[pltpu.VMEM((8, CONV_WIN, CONV_W), F32), pltpu.VMEM((8, CONV_WIN, CONV_W), F32),
                        pltpu.VMEM((TP, CONV_W), F32)],
        compiler_params=_cparams(("parallel",)),
        name="pool_conv",
    )(p, p, p, u, u, u, band_main, band_halo, inv_cnt, pw_bd, pool_scale, shifts, conv_w, conv_b, cn_g, cn_b)


def _route(logits):
    lane = lax.broadcasted_iota(jnp.int32, logits.shape, 1)
    big = jnp.int32(LANES)
    lg = jnp.where(lane < N_GROUPS, logits, NEG_INF)
    mg = jnp.max(lg, axis=-1, keepdims=True)
    grp = jnp.min(jnp.where(lg == mg, lane, big), axis=-1, keepdims=True)
    p_grp = 1.0 / jnp.sum(jnp.exp(lg - mg), axis=-1, keepdims=True)
    lo = N_GROUPS + grp * EPG
    le = jnp.where((lane >= lo) & (lane < lo + EPG), logits, NEG_INF)
    m1 = jnp.max(le, axis=-1, keepdims=True)
    i1 = jnp.min(jnp.where(le == m1, lane, big), axis=-1, keepdims=True)
    le2 = jnp.where(lane == i1, NEG_INF, le)
    m2 = jnp.max(le2, axis=-1, keepdims=True)
    i2 = jnp.min(jnp.where(le2 == m2, lane, big), axis=-1, keepdims=True)
    r = jnp.exp(m2 - m1)
    w1 = p_grp / (1.0 + r)
    w2 = p_grp * r / (1.0 + r)
    e1 = (i1 - N_GROUPS).astype(F32)
    e2 = (i2 - N_GROUPS).astype(F32)
    return jnp.where(lane == 0, e1, jnp.where(lane == 1, e2, jnp.where(lane == 2, w1,
                     jnp.where(lane == 3, w2, 0.0))))


def _mix_kernel(x_ref, mod_ref, gn_ref, al_ref, ac_ref, hl_ref, hc_ref, z_ref, cv_ref,
                wa_ref, wf_ref, wp_ref, wc_ref, wg_ref, bg_ref, wo_ref, gf_ref, rh_ref, rl_ref, rb_ref,
                xo_ref, h2_ref, rt_ref):
    is_ctx = pl.program_id(0) >= LAT_TILES
    m = mod_ref[...]
    x = x_ref[...]
    hb = _modulate(x, gn_ref[...], m[0:1], m[1:2]).astype(BF16)
    attn = jnp.where(is_ctx, ac_ref[...], al_ref[...])
    four = jnp.where(is_ctx, hc_ref[...], hl_ref[...])
    branches = ((attn, wa_ref), (four, wf_ref), (z_ref[...], wp_ref), (cv_ref[...], wc_ref))
    acc = None
    for bi, (inp, w_ref) in enumerate(branches):
        cs = slice(bi * D, (bi + 1) * D)
        gate = _sigmoid(_dot(hb, wg_ref[:, cs]) + bg_ref[:, cs])
        term = gate * _dot(inp, w_ref[...])
        acc = term if acc is None else acc + term
    x_new = x + m[2:3] * _dot(acc.astype(BF16), wo_ref[...])
    xo_ref[...] = x_new
    h2 = _modulate(x_new, gf_ref[...], m[3:4], m[4:5])
    hi = h2.astype(BF16)
    h2_ref[...] = hi
    lo = (h2 - hi.astype(F32)).astype(BF16)
    logits = _dot(hi, rh_ref[...]) + _dot(lo, rh_ref[...]) + _dot(hi, rl_ref[...]) + rb_ref[...]
    rt_ref[...] = _route(logits)


def _mix_call(x, mods_l, gn, a_lat, a_ctx, h_lat, h_ctx, zc, cact, l, stacked, small):
    tok = lambda w: pl.BlockSpec((TM, w), lambda i: (i, 0))
    lat = lambda w: pl.BlockSpec((TM, w), lambda i: (jnp.minimum(i, LAT_TILES - 1), 0))
    wa, wf, wp, wc, wg, wo, rh, rl = stacked
    bg, gf, rb = small
    in_specs = [tok(D), pl.BlockSpec((None, 6, D), lambda i: (_mod_row(i), 0, 0)), _const_spec((1, D)),
                lat(Q_W), _const_spec((N_CTX, Q_W)),
                lat(2 * FOURIER_W), _const_spec((N_CTX, 2 * FOURIER_W)),
                tok(POOL_W), tok(CONV_W)]
    in_specs += [_layer_spec(w.shape[1:], l) for w in (wa, wf, wp, wc, wg)]
    in_specs += [_const_spec(bg.shape), _layer_spec(wo.shape[1:], l), _const_spec(gf.shape),
                 _layer_spec(rh.shape[1:], l), _layer_spec(rl.shape[1:], l), _const_spec(rb.shape)]
    return pl.pallas_call(
        _mix_kernel,
        out_shape=[jax.ShapeDtypeStruct((N_TOK, D), F32), jax.ShapeDtypeStruct((N_TOK, D), BF16),
                   jax.ShapeDtypeStruct((N_TOK, LANES), F32)],
        grid=(NT,),
        in_specs=in_specs,
        out_specs=[tok(D), tok(D), tok(LANES)],
        compiler_params=_cparams(("parallel",)),
        name="mix",
    )(x, mods_l, gn, a_lat, a_ctx, h_lat, h_ctx, zc, cact, wa, wf, wp, wc, wg, bg, wo, gf, rh, rl, rb)


def _onehots(route):
    lane = lax.broadcasted_iota(jnp.int32, route.shape, 1)
    e1 = route[:, 0:1].astype(jnp.int32)
    e2 = route[:, 1:2].astype(jnp.int32)
    return (lane == e1).astype(F32), (lane == e2).astype(F32)


def _lane_cumsum(row):
    lane = lax.broadcasted_iota(jnp.int32, row.shape, 1)
    sh = 1
    while sh < N_EXPERTS:
        row = row + jnp.where(lane >= sh, pltpu.roll(row, sh, 1), 0.0)
        sh *= 2
    return row


def _rank_kernel(rt_ref, tri_ref, pos_ref, meta_ref, cnt_ref, carry_ref):
    i = pl.program_id(0)

    @pl.when(i == 0)
    def _():
        carry_ref[...] = jnp.zeros_like(carry_ref)

    oh1, oh2 = _onehots(rt_ref[...])
    both = oh1 + oh2
    carry = carry_ref[0:1, :]
    tile_cnt = jnp.sum(both, axis=0, keepdims=True)
    tile_cnt = tile_cnt + (tile_cnt - 2.0 * jnp.floor(tile_cnt * 0.5))
    tile_off = _lane_cumsum(tile_cnt) - tile_cnt
    where = _dot(tri_ref[...], both.astype(BF16)) + tile_off
    p1 = jnp.sum(oh1 * where, axis=-1, keepdims=True)
    p2 = jnp.sum(oh2 * where, axis=-1, keepdims=True)
    lane = lax.broadcasted_iota(jnp.int32, both.shape, 1)
    pos_ref[...] = jnp.where(lane == 0, p1, jnp.where(lane == 1, p2, 0.0))
    row = lax.broadcasted_iota(jnp.int32, meta_ref.shape, 0)
    meta_ref[...] = jnp.where(row == 0, tile_off, jnp.where(row == 1, tile_cnt, jnp.where(row == 2, carry, 0.0)))
    total = carry + tile_cnt
    carry_ref[...] = jnp.broadcast_to(total, carry_ref.shape)
    cnt_ref[...] = jnp.broadcast_to(total, cnt_ref.shape)


def _runs_kernel(meta_ref, cnt_ref, runs_ref, be_ref):
    lane = lax.broadcasted_iota(jnp.int32, (1, LANES), 1)
    counts = cnt_ref[0:1, :]
    padded = jnp.floor((counts + (RUN_ROWS - 1 + MOE_BLK - 1)) * (1.0 / MOE_BLK)) * MOE_BLK
    padded = jnp.where(lane < N_EXPERTS, padded, 0.0)
    ends = _lane_cumsum(padded)
    starts = ends - padded
    for t in range(NT):
        m = meta_ref[t]
        row = lax.broadcasted_iota(jnp.int32, m.shape, 0)
        m = jnp.where(row == 2, m + starts, m)
        m = jnp.where(row == 3, starts + counts, jnp.where(row == 4, padded - counts, m))
        runs_ref[t] = m.astype(jnp.int32)
    blk = lax.broadcasted_iota(jnp.int32, be_ref.shape, 0).astype(F32) * MOE_BLK
    lane_b = lax.broadcasted_iota(jnp.int32, be_ref.shape, 1)
    done = jnp.where((ends <= blk) & (lane_b < N_EXPERTS), 1.0, 0.0)
    be = jnp.minimum(jnp.sum(done, axis=-1, keepdims=True), N_EXPERTS - 1.0)
    nblk = jnp.max(jnp.where(lane_b == N_EXPERTS - 1, ends, 0.0), axis=-1, keepdims=True) * (1.0 / MOE_BLK)
    be_ref[...] = jnp.where(lane_b == 0, be, jnp.where(lane_b == 1, nblk, 0.0)).astype(jnp.int32)


def _plan_call(route, tri):
    tok = pl.BlockSpec((TM, LANES), lambda i: (i, 0))
    pos, meta, counts = pl.pallas_call(
        _rank_kernel,
        out_shape=[jax.ShapeDtypeStruct((N_TOK, LANES), F32), jax.ShapeDtypeStruct((NT, 8, LANES), F32),
                   jax.ShapeDtypeStruct((8, LANES), F32)],
        grid=(NT,),
        in_specs=[tok, _const_spec((TM, TM))],
        out_specs=[tok, pl.BlockSpec((None, 8, LANES), lambda i: (i, 0, 0)),
                   pl.BlockSpec((8, LANES), lambda i: (0, 0))],
        scratch_shapes=[pltpu.VMEM((8, LANES), F32)],
        compiler_params=_cparams(("arbitrary",)),
        name="moe_rank",
    )(route, tri)
    runs, blk = pl.pallas_call(
        _runs_kernel,
        out_shape=[jax.ShapeDtypeStruct((NT, 8, LANES), jnp.int32),
                   jax.ShapeDtypeStruct((256, LANES), jnp.int32)],
        name="moe_runs",
    )(meta, counts)
    runs_flat = runs[:, 0:RUN_FIELDS, 0:N_EXPERTS].reshape(NT, 1, RUN_FIELDS * N_EXPERTS)
    return pos, runs_flat, blk[:N_MOE_BLOCKS, 0], blk[0:1, 1]


def _pack_pairs(x):
    half = x.shape[1] // 2
    lo = pltpu.bitcast(x[:, :half], jnp.uint32)
    hi = pltpu.bitcast(x[:, half:], jnp.uint32)
    return (lo >> 16) | (hi & jnp.uint32(0xFFFF0000))


def _unpack_pairs(w):
    lo = pltpu.bitcast(w << 16, F32)
    hi = pltpu.bitcast(w & jnp.uint32(0xFFFF0000), F32)
    return jnp.concatenate([lo, hi], axis=1).astype(BF16)


def _run_fields(runs_ref, e):
    return runs_ref[0, e], runs_ref[0, N_EXPERTS + e], runs_ref[0, 2 * N_EXPERTS + e]


def _store_rows(lin_ref, packed):
    rows = packed.shape[0]
    for c in range(PK):
        lin_ref[pl.ds(c, rows, stride=PK), :] = packed[:, c * LANES:(c + 1) * LANES]


def _load_rows(lin_ref, rows):
    return jnp.concatenate([lin_ref[pl.ds(c, rows, stride=PK), :] for c in range(PK)], axis=1)


def _lin(ref, row, nrows):
    return ref.at[pl.ds(pl.multiple_of(row * PK, 8), nrows * PK), :]


TAIL_PIECES = (512, 256, 128, 64, 32, 16, 8, 4, 2)


def _dispatch_kernel(runs_ref, prev_runs_ref, pos_ref, h2_ref, xs_ref, buf0, buf1, zero_ref, sem0, sem1, zsem,
                     ssem):
    i = pl.program_id(0)
    last = NT - 1

    def tail_copies(act):
        def per_expert(e, carry):
            row = runs_ref[0, 3 * N_EXPERTS + e]
            n = runs_ref[0, 4 * N_EXPERTS + e]
            done = jnp.int32(0)
            for size in TAIL_PIECES:
                @pl.when((n & size) != 0)
                def _(done=done, size=size):
                    act(pltpu.make_async_copy(zero_ref.at[pl.ds(0, size * PK), :], _lin(xs_ref, row + done, size), zsem))
                done = done + (n & size)
            return carry

        lax.fori_loop(0, N_EXPERTS, per_expert, 0)

    def spare_copies(act):
        used = runs_ref[0, 3 * N_EXPERTS + N_EXPERTS - 1] + runs_ref[0, 4 * N_EXPERTS + N_EXPERTS - 1]

        def spare_block(k, carry):
            act(pltpu.make_async_copy(zero_ref.at[pl.ds(0, MOE_BLK * PK), :],
                                      _lin(xs_ref, used + k * MOE_BLK, MOE_BLK), ssem))
            return carry

        lax.fori_loop(0, N_MOE_BLOCKS - used // MOE_BLK, spare_block, 0)

    start = lambda d: d.start()
    wait = lambda d: d.wait()

    def chunks_wait(runs, buf, sem):
        chunks = lax.fori_loop(0, N_EXPERTS,
                               lambda e, acc: acc + (runs[0, N_EXPERTS + e] + RUN_ROWS - 1) // RUN_ROWS, jnp.int32(0))
        unit = pltpu.make_async_copy(_lin(buf, 0, RUN_ROWS), _lin(xs_ref, 0, RUN_ROWS), sem)
        lax.fori_loop(0, chunks, lambda k, c: (unit.wait(), c)[1], 0)

    @pl.when(i == 0)
    def _():
        for buf in (buf0, buf1):
            buf[TS * PK:, :] = jnp.zeros((RUN_ROWS * PK, LANES), jnp.uint32)
        zero_ref[...] = jnp.zeros_like(zero_ref)
        tail_copies(start)
        spare_copies(start)

    def run_copies(runs, buf, sem, act):
        def chunk_copy(off, dst, k):
            return pltpu.make_async_copy(_lin(buf, off + k * RUN_ROWS, RUN_ROWS),
                                         _lin(xs_ref, dst + k * RUN_ROWS, RUN_ROWS), sem)

        def per_expert(e, carry):
            off, n, dst = _run_fields(runs, e)
            lax.fori_loop(0, (n + RUN_ROWS - 1) // RUN_ROWS, lambda k, c: (act(chunk_copy(off, dst, k)), c)[1], 0)
            return carry

        lax.fori_loop(0, N_EXPERTS, per_expert, 0)

    pos = pos_ref[...]
    col = lax.broadcasted_iota(jnp.int32, (TM, TS), 1).astype(F32)
    sel = jnp.where((col == pos[:, 0:1]) | (col == pos[:, 1:2]), 1.0, 0.0).astype(BF16)
    srt = lax.dot_general(sel, h2_ref[...], (((0,), (0,)), ((), ())), preferred_element_type=F32)
    packed = _pack_pairs(srt)

    for par, (buf, sem, obuf, osem) in enumerate(((buf0, sem0, buf1, sem1), (buf1, sem1, buf0, sem0))):
        @pl.when(i % 2 == par)
        def _(buf=buf, sem=sem, obuf=obuf, osem=osem):
            _store_rows(buf, packed)

            @pl.when(i == 0)
            def _():
                tail_copies(wait)

            @pl.when(i > 0)
            def _():
                chunks_wait(prev_runs_ref, obuf, osem)

            run_copies(runs_ref, buf, sem, start)

            @pl.when(i == last)
            def _():
                chunks_wait(runs_ref, buf, sem)
                spare_copies(wait)


def _dispatch_call(runs_flat, pos, h2):
    runs_spec = lambda shift: pl.BlockSpec((None, 1, RUN_FIELDS * N_EXPERTS), lambda i: (jnp.maximum(i - shift, 0), 0, 0),
                                           memory_space=pltpu.SMEM)
    sorted_buf = pltpu.VMEM(((TS + RUN_ROWS) * PK, LANES), jnp.uint32)
    return pl.pallas_call(
        _dispatch_kernel,
        out_shape=jax.ShapeDtypeStruct((N_SLOTS * PK, LANES), jnp.uint32),
        grid=(NT,),
        in_specs=[runs_spec(0), runs_spec(1),
                  pl.BlockSpec((TM, LANES), lambda i: (i, 0)),
                  pl.BlockSpec((TM, D), lambda i: (i, 0))],
        out_specs=pl.BlockSpec(memory_space=pl.ANY),
        scratch_shapes=[sorted_buf, sorted_buf, pltpu.VMEM((TAIL_PIECES[0] * PK, LANES), jnp.uint32),
                        pltpu.SemaphoreType.DMA, pltpu.SemaphoreType.DMA, pltpu.SemaphoreType.DMA,
                        pltpu.SemaphoreType.DMA],
        compiler_params=_cparams(("arbitrary",)),
        name="moe_dispatch",
    )(runs_flat, runs_flat, pos, h2)


def _expert_kernel(be_ref, nu_ref, xs_ref, wg_ref, wu_ref, wd_ref, ys_ref, wgb_ref, wub_ref, wdb_ref):
    b = pl.program_id(0)

    @pl.when(jnp.logical_or(b == 0, be_ref[b] != be_ref[jnp.maximum(b - 1, 0)]))
    def _():
        wgb_ref[...] = wg_ref[...].astype(BF16)
        wub_ref[...] = wu_ref[...].astype(BF16)
        wdb_ref[...] = wd_ref[...].astype(BF16)

    @pl.when(b < nu_ref[0])
    def _():
        xb = _unpack_pairs(_load_rows(xs_ref, MOE_BLK))
        g = _dot(xb, wgb_ref[...])
        u = _dot(xb, wub_ref[...])
        hmid = (g * _sigmoid(g)) * u
        y = _dot(hmid.astype(BF16), wdb_ref[...])
        _store_rows(ys_ref, _pack_pairs(y.astype(BF16).astype(F32)))

    @pl.when(b >= nu_ref[0])
    def _():
        ys_ref[...] = jnp.zeros_like(ys_ref)


def _expert_call(blk_e, n_used, xs, wg, wu, wd, l):
    wspec = lambda k, n: pl.BlockSpec((None, None, k, n), lambda b, be, nu: (l, be[b], 0, 0))
    return pl.pallas_call(
        _expert_kernel,
        out_shape=jax.ShapeDtypeStruct((N_SLOTS * PK, LANES), jnp.uint32),
        grid_spec=pltpu.PrefetchScalarGridSpec(
            num_scalar_prefetch=2,
            grid=(N_MOE_BLOCKS,),
            in_specs=[pl.BlockSpec((MOE_BLK * PK, LANES), lambda b, be, nu: (jnp.minimum(b, nu[0] - 1), 0)),
                      wspec(D, EXPERT_HIDDEN), wspec(D, EXPERT_HIDDEN), wspec(EXPERT_HIDDEN, D)],
            out_specs=pl.BlockSpec((MOE_BLK * PK, LANES), lambda b, be, nu: (b, 0)),
            scratch_shapes=[pltpu.VMEM((D, EXPERT_HIDDEN), BF16), pltpu.VMEM((D, EXPERT_HIDDEN), BF16),
                            pltpu.VMEM((EXPERT_HIDDEN, D), BF16)]),
        compiler_params=_cparams(("arbitrary",)),
        name="moe_experts",
    )(blk_e, n_used, xs, wg, wu, wd)


FETCH_ROWS = 64
RUN_PIECES = (32, 16, 8, 4, 2)


def _combine_kernel(runs_ref, next_runs_ref, ys_ref, pos_ref, x_ref, rt_ref, mod_ref, o_ref, buf0, buf1, sem0, sem1,
                    *, n_tiles):
    i = pl.program_id(0)
    last = n_tiles - 1

    def run_copies(runs, buf, sem, act):
        def piece(off, dst, size):
            return pltpu.make_async_copy(_lin(ys_ref, dst, size), _lin(buf, off, size), sem)

        def per_expert(e, carry):
            off, n, dst = _run_fields(runs, e)
            whole = n // FETCH_ROWS

            def chunk(k, c):
                act(piece(off + k * FETCH_ROWS, dst + k * FETCH_ROWS, FETCH_ROWS))
                return c

            lax.fori_loop(0, whole, chunk, 0)
            done = whole * FETCH_ROWS
            for size in RUN_PIECES:
                @pl.when((n & size) != 0)
                def _(done=done, size=size):
                    act(piece(off + done, dst + done, size))
                done = done + (n & size)
            return carry

        lax.fori_loop(0, N_EXPERTS, per_expert, 0)

    start = lambda d: d.start()

    def rows_wait(runs, buf, sem):
        total = lax.fori_loop(0, N_EXPERTS, lambda e, acc: acc + runs[0, N_EXPERTS + e], jnp.int32(0))

        def unit(size):
            return pltpu.make_async_copy(_lin(ys_ref, 0, size), _lin(buf, 0, size), sem)

        lax.fori_loop(0, total // FETCH_ROWS, lambda k, c: (unit(FETCH_ROWS).wait(), c)[1], 0)
        for size in RUN_PIECES:
            @pl.when((total & size) != 0)
            def _(size=size):
                unit(size).wait()

    @pl.when(i == 0)
    def _():
        buf0[...] = jnp.zeros_like(buf0)
        buf1[...] = jnp.zeros_like(buf1)
        run_copies(runs_ref, buf0, sem0, start)

    pos = pos_ref[...]
    rt = rt_ref[...]
    col = lax.broadcasted_iota(jnp.int32, (TM, TS), 1).astype(F32)
    pick = (jnp.where(col == pos[:, 0:1], rt[:, TOP_K:TOP_K + 1], 0.0)
            + jnp.where(col == pos[:, 1:2], rt[:, TOP_K + 1:TOP_K + 2], 0.0)).astype(BF16)

    for par, (buf, sem, obuf, osem) in enumerate(((buf0, sem0, buf1, sem1), (buf1, sem1, buf0, sem0))):
        @pl.when(i % 2 == par)
        def _(buf=buf, sem=sem, obuf=obuf, osem=osem):
            @pl.when(i < last)
            def _():
                run_copies(next_runs_ref, obuf, osem, start)

            rows_wait(runs_ref, buf, sem)
            ysb = _unpack_pairs(_load_rows(buf, TS))
            o_ref[...] = x_ref[...] + mod_ref[5:6, :] * _dot(pick, ysb)


def _combine_call(runs_flat, ys, pos, x, route, mods_l, n_tiles):
    tok = lambda w: pl.BlockSpec((TM, w), lambda i: (i, 0))
    runs_spec = lambda shift: pl.BlockSpec((None, 1, RUN_FIELDS * N_EXPERTS),
                                           lambda i: (jnp.minimum(i + shift, n_tiles - 1), 0, 0),
                                           memory_space=pltpu.SMEM)
    sorted_buf = pltpu.VMEM((TS * PK, LANES), jnp.uint32)
    return pl.pallas_call(
        functools.partial(_combine_kernel, n_tiles=n_tiles),
        out_shape=jax.ShapeDtypeStruct((n_tiles * TM, D), F32),
        grid=(n_tiles,),
        in_specs=[runs_spec(0), runs_spec(1),
                  pl.BlockSpec(memory_space=pl.ANY),
                  tok(LANES), tok(D), tok(LANES),
                  pl.BlockSpec((None, 6, D), lambda i: (_mod_row(i), 0, 0))],
        out_specs=tok(D),
        scratch_shapes=[sorted_buf, sorted_buf, pltpu.SemaphoreType.DMA, pltpu.SemaphoreType.DMA],
        compiler_params=_cparams(("arbitrary",)),
        name="moe_combine",
    )(runs_flat, runs_flat, ys, pos, x, route, mods_l)


def _rope_tables():
    nf = HEAD_DIM // 4
    inv = ROPE_BASE ** (-jnp.arange(nf, dtype=F32) / nf)
    t = jnp.arange(S)
    row = (t // GRID_W).astype(F32)[:, None] * inv[None, :]
    col = (t % GRID_W).astype(F32)[:, None] * inv[None, :]
    zero = jnp.zeros_like(row)
    cos = jnp.concatenate([jnp.cos(row), jnp.cos(row), jnp.cos(col), jnp.cos(col)], axis=1)
    sa = jnp.concatenate([-jnp.sin(row), zero, -jnp.sin(col), zero], axis=1)
    sb = jnp.concatenate([zero, jnp.sin(row), zero, jnp.sin(col)], axis=1)
    ident = (jnp.ones((TM, HEAD_DIM), F32), jnp.zeros((TM, HEAD_DIM), F32), jnp.zeros((TM, HEAD_DIM), F32))
    return tuple(jnp.tile(jnp.concatenate([a, b], axis=0), (1, LANES // HEAD_DIM))
                 for a, b in zip((cos, sa, sb), ident))


def _fourier_tables():
    s1 = np.arange(FS1)
    ang1 = 2.0 * np.pi * np.outer(s1, s1) / FS1
    w1 = np.concatenate([np.cos(ang1), -np.sin(ang1)], axis=0) / np.sqrt(S)
    k1 = np.arange(FS1)[:, None, None]
    k2 = np.arange(FS2)[None, :, None]
    s2 = np.arange(FS2)[None, None, :]
    ang2 = 2.0 * np.pi * ((k1 + FS1 * k2) * s2 % S) / S
    c2, sn2 = np.cos(ang2), np.sin(ang2)
    ta = np.concatenate([c2, -sn2], axis=1)
    tb = np.concatenate([sn2, c2], axis=1)
    sc = np.arange(C)
    angc = 2.0 * np.pi * np.outer(sc, sc) / C
    wc = np.concatenate([np.cos(angc), -np.sin(angc)], axis=0) / np.sqrt(C)
    return tuple(jnp.asarray(a, F32).astype(BF16) for a in (w1, ta, tb, wc))


def _channel_dft():
    cidx = np.arange(FOURIER_GROUP_W)
    ang = 2.0 * np.pi * np.outer(cidx, cidx) / FOURIER_GROUP_W
    eye = np.eye(FOURIER_W // FOURIER_GROUP_W)
    cw = np.kron(eye, np.cos(ang)) / np.sqrt(FOURIER_GROUP_W)
    sw = np.kron(eye, np.sin(ang)) / np.sqrt(FOURIER_GROUP_W)
    return jnp.asarray(np.concatenate([cw, sw], axis=0), F32)


def _pool_bands():
    t = np.arange(TP)[:, None]
    main, halo = [], []
    for w in POOL_WINDOWS:
        def hit(j):
            return ((j - t >= -(w // 2)) & (j - t <= w // 2 - 1)).astype(np.float32)
        main.append(hit(np.arange(TP)[None, :]))
        halo.append(np.concatenate([hit(np.arange(-HALO, 0)[None, :]),
                                    hit(np.arange(TP, TP + HALO)[None, :])], axis=1))
    return (jnp.asarray(np.stack(main), F32).astype(BF16), jnp.asarray(np.stack(halo), F32).astype(BF16))


def _pool_inv_counts():
    win = np.repeat(np.array(POOL_WINDOWS), POOL_GROUP_W)[None, :]

    def table(pos0, seq_len):
        pos = (pos0 + np.arange(TP))[:, None]
        lo = np.clip(pos - win // 2, 0, seq_len)
        hi = np.clip(pos - win // 2 + win, 0, seq_len)
        return 1.0 / (hi - lo)

    tabs = [table(TP, S), table(0, S), table(S - TP, S), table(0, C)]
    return jnp.asarray(np.stack(tabs), F32)


def _conv_shifts():
    i = np.arange(CONV_WIN)
    return jnp.asarray(np.stack([(i[None, :] == i[:, None] + s) for s in range(1, 8)]), F32).astype(BF16)


def _fold_kernel(a_ref, b_ref, o_ref):
    a, b = a_ref[...], b_ref[...]
    a_hi, b_hi = a.astype(BF16), b.astype(BF16)
    a_lo = (a - a_hi.astype(F32)).astype(BF16)
    b_lo = (b - b_hi.astype(F32)).astype(BF16)
    o_ref[...] = (_dot(a_hi, b_hi) + _dot(a_lo, b_hi) + _dot(a_hi, b_lo)).astype(BF16)


def _fold_fourier_weights(dftw, w_br_fourier):
    nl = w_br_fourier.shape[0]
    return pl.pallas_call(
        _fold_kernel,
        out_shape=jax.ShapeDtypeStruct((nl, 2 * FOURIER_W, D), BF16),
        grid=(nl,),
        in_specs=[pl.BlockSpec((2 * FOURIER_W, FOURIER_W), lambda l: (0, 0)),
                  pl.BlockSpec((None, FOURIER_W, D), lambda l: (l, 0, 0))],
        out_specs=pl.BlockSpec((None, 2 * FOURIER_W, D), lambda l: (l, 0, 0)),
        compiler_params=_cparams(("arbitrary",)),
        name="fold_fourier_proj",
    )(dftw, w_br_fourier)


def _block_diag(blocks):
    n, r, c = blocks.shape
    eye = jnp.eye(n, dtype=blocks.dtype)
    return (blocks[:, :, None, :] * eye[:, None, :, None]).reshape(n * r, n * c)


def kernel(x, c, ctx, c_ctx, w_ada, b_ada, g_norm_mix, g_norm_ffn, w_in, g_q, g_k, sink, w_br_attn,
           w_br_fourier, pool_w, pool_scale, w_br_pool, conv_w, conv_b, cn_g, cn_b, w_br_conv, w_gate,
           b_gate, w_out, w_router_grp, b_router_grp, w_router_exp, b_router_exp, w_e_gate, w_e_up,
           w_e_down):
    xs = jnp.concatenate([x.reshape(N_LAT, D), ctx.reshape(N_CTX, D)], axis=0)
    nl = w_ada.shape[0]
    mods = _ada_all(c, c_ctx, w_ada, b_ada).reshape(nl, 8, 6, D)
    rope_tabs = _rope_tables()
    four_tabs = _fourier_tables()
    band_main, band_halo = _pool_bands()
    inv_cnt = _pool_inv_counts()
    shifts = _conv_shifts()
    wf_all = _fold_fourier_weights(_channel_dft(), w_br_fourier)
    bd = jnp.asarray(np.kron(np.eye(LANES // HEAD_DIM), np.ones((HEAD_DIM, HEAD_DIM))), F32).astype(BF16)
    tri = jnp.asarray(np.tril(np.ones((TM, TM)), -1), F32).astype(BF16)
    rpad = jnp.zeros((nl, D, LANES - N_GROUPS - N_EXPERTS), F32)
    w_router = jnp.concatenate([w_router_grp, w_router_exp, rpad], axis=-1)
    r_hi = w_router.astype(BF16)
    r_lo = (w_router - r_hi.astype(F32)).astype(BF16)
    r_b = jnp.concatenate([b_router_grp, b_router_exp, rpad[:, 0, :]], axis=-1).reshape(nl, 1, LANES)
    stacked = tuple(w.astype(BF16) for w in (w_br_attn,)) + (wf_all,) + tuple(
        w.astype(BF16) for w in (w_br_pool, w_br_conv, w_gate, w_out)) + (r_hi, r_lo)

    for l in range(nl):
        mods_l = mods[l]
        gn = g_norm_mix[l].reshape(1, D)
        q, kv, f, p, u = _proj_call(xs, mods_l, gn, w_in, l, rope_tabs,
                                    jnp.tile(g_q[l], 2).reshape(1, LANES),
                                    jnp.tile(g_k[l], 2).reshape(1, LANES), bd)
        a_re, a_im = _fourier_stage1_call(f, four_tabs)
        a_lat, a_ctx = _attn_call(sink[l], q, kv)
        h_lat, h_ctx = _fourier_stage2_call(a_re, a_im, f, four_tabs, a_lat)
        zc, cact = _poolconv_call(p, u, band_main, band_halo, inv_cnt, _block_diag(pool_w[l]).astype(BF16),
                                  pool_scale[l].reshape(1, POOL_W), shifts, conv_w[l], conv_b[l].reshape(1, CONV_W),
                                  cn_g[l].reshape(1, CONV_W), cn_b[l].reshape(1, CONV_W))
        small = (b_gate[l].reshape(1, 4 * D), g_norm_ffn[l].reshape(1, D), r_b[l])
        xs, h2, route = _mix_call(xs, mods_l, gn, a_lat, a_ctx, h_lat, h_ctx, zc, cact, l, stacked, small)
        pos, runs_flat, blk_e, n_used = _plan_call(route, tri)
        slots = _dispatch_call(runs_flat, pos, h2)
        ys = _expert_call(blk_e, n_used, slots, w_e_gate, w_e_up, w_e_down, l)
        xs = _combine_call(runs_flat, ys, pos, xs, route, mods_l, LAT_TILES if l == nl - 1 else NT)
    return xs.reshape(B, S, D)
```

```python
import functools

import numpy as np
import jax
import jax.numpy as jnp
from jax import lax
from jax.experimental import pallas as pl
from jax.experimental.pallas import tpu as pltpu

F32 = jnp.float32
BF16 = jnp.bfloat16

D = 1024
B = 2
S = 8192
C = 256
GRID_W = 64
HEAD_DIM = 64
N_Q_HEADS = 8
N_KV_HEADS = 2
GQA = N_Q_HEADS // N_KV_HEADS
WINDOW = 128
ATTN_BLK = 128
ATTN_QB = 4
ROPE_BASE = 10000.0
Q_W = 512
KV_W = 128
FOURIER_W = 640
FOURIER_GROUP_W = 160
POOL_W = 640
POOL_GROUP_W = 160
POOL_WINDOWS = (2, 4, 8, 16)
CONV_W = 512
CONV_K = 31
PROJ_W = 3072
N_GROUPS = 4
EPG = 8
N_EXPERTS = 32
TOP_K = 2
EXPERT_HIDDEN = 512
MOE_BLK = 512
EPS = 1e-6
NEG_INF = -1e30
LOG2E = 1.4426950408889634

N_LAT = B * S
N_CTX = B * C
N_TOK = N_LAT + N_CTX
TM = 512
NT = N_TOK // TM
LAT_TILES = N_LAT // TM
TILES_PER_BATCH = S // TM
TP = 256
NTP = N_TOK // TP
HALO = 16
CONV_WIN = TP // 2 + 2 * HALO
N_ASSIGN = N_TOK * TOP_K
RUN_FIELDS = 5
PK = D // 2 // 128
TS = 1152
N_MOE_BLOCKS = (N_ASSIGN + NT * N_EXPERTS + N_EXPERTS * (MOE_BLK - 1)) // MOE_BLK
N_SLOTS = N_MOE_BLOCKS * MOE_BLK
FS1 = 64
FS2 = 128
F1_ROWS = 32
F2_K1 = 8
LANES = 128
VMEM_LIMIT = 56 * 1024 * 1024


def _cparams(sem, vmem=VMEM_LIMIT):
    return pltpu.CompilerParams(dimension_semantics=sem, vmem_limit_bytes=vmem)


def _const_spec(shape):
    nd = len(shape)
    return pl.BlockSpec(shape, lambda *_: (0,) * nd, pipeline_mode=pl.Buffered(1))


def _dot(a, b):
    return jnp.dot(a, b, preferred_element_type=F32)


def _modulate(x, g, shift, scale):
    y = x * lax.rsqrt(jnp.mean(x * x, axis=-1, keepdims=True) + EPS)
    return (y * g) * (1.0 + scale) + shift


def _sigmoid(x):
    return 1.0 / (1.0 + jnp.exp(-x))


def _ada_kernel(ct_ref, w_ref, b_ref, o_ref):
    ct = ct_ref[...]
    s = ct * _sigmoid(ct)
    w = w_ref[...]
    rows = [jnp.sum(w * s[:, r:r + 1], axis=0, keepdims=True) for r in range(3)]
    rows.append(jnp.zeros((5, w.shape[1]), F32))
    o_ref[...] = jnp.concatenate(rows, axis=0) + b_ref[...]


def _ada_all(c, c_ctx, w_ada, b_ada):
    ct = jnp.concatenate([c, c_ctx[None, :], jnp.zeros((5, D), F32)], axis=0).T
    cols = 1536
    nl = w_ada.shape[0]
    return pl.pallas_call(
        _ada_kernel,
        out_shape=jax.ShapeDtypeStruct((nl, 8, 6 * D), F32),
        grid=(nl, 6 * D // cols),
        in_specs=[pl.BlockSpec((D, 8), lambda l, j: (0, 0)),
                  pl.BlockSpec((None, D, cols), lambda l, j: (l, 0, j)),
                  pl.BlockSpec((None, 1, cols), lambda l, j: (l, 0, j))],
        out_specs=pl.BlockSpec((None, 8, cols), lambda l, j: (l, 0, j)),
        compiler_params=_cparams(("arbitrary", "arbitrary")),
        name="adaln",
    )(ct, w_ada, b_ada.reshape(nl, 1, 6 * D))


def _head_rms(t, g128, bd):
    outs = []
    for j in range(t.shape[1] // LANES):
        blk = t[:, j * LANES:(j + 1) * LANES]
        ss = _dot((blk * blk).astype(BF16), bd)
        outs.append(blk * lax.rsqrt(ss * (1.0 / HEAD_DIM) + EPS) * g128)
    return outs


def _rope(blocks, cos, sa, sb):
    outs = []
    for blk in blocks:
        up = pltpu.roll(blk, LANES - 16, 1)
        dn = pltpu.roll(blk, 16, 1)
        outs.append(blk * cos + up * sa + dn * sb)
    return outs


def _proj_kernel(x_ref, mod_ref, gn_ref, w_ref, cos_ref, sa_ref, sb_ref, gq_ref, gk_ref, bd_ref,
                 q_ref, kv_ref, f_ref, p_ref, u_ref, wbf_ref):
    @pl.when(pl.program_id(0) == 0)
    def _():
        wbf_ref[...] = w_ref[...].astype(BF16)

    m = mod_ref[...]
    hb = _modulate(x_ref[...], gn_ref[...], m[0:1], m[1:2]).astype(BF16)
    cos, sa, sb, bd = cos_ref[...], sa_ref[...], sb_ref[...], bd_ref[...]
    o_kv, o_f, o_a = Q_W, Q_W + 2 * KV_W, Q_W + 2 * KV_W + FOURIER_W + POOL_W
    qkv = _dot(hb, wbf_ref[:, 0:o_f])
    fp = _dot(hb, wbf_ref[:, o_f:o_a])
    q = _rope(_head_rms(qkv[:, 0:Q_W], gq_ref[...], bd), cos, sa, sb)
    q_ref[...] = (jnp.concatenate(q, axis=1) * (LOG2E * HEAD_DIM ** -0.5)).astype(BF16)
    k = _rope(_head_rms(qkv[:, o_kv:o_kv + KV_W], gk_ref[...], bd), cos, sa, sb)
    kv_ref[:, 0:KV_W] = k[0].astype(BF16)
    kv_ref[:, KV_W:2 * KV_W] = qkv[:, o_kv + KV_W:o_f].astype(BF16)
    ag = _dot(hb, wbf_ref[:, o_a:PROJ_W])
    f_ref[...] = fp[:, 0:FOURIER_W].astype(BF16)
    p_ref[...] = fp[:, FOURIER_W:].astype(BF16)
    u_ref[...] = (ag[:, 0:CONV_W] * _sigmoid(ag[:, CONV_W:])).astype(BF16)


def _mod_row(i):
    return jnp.minimum(i // TILES_PER_BATCH, 2)


def _layer_spec(shape, l):
    nd = len(shape)
    return pl.BlockSpec((None,) + tuple(shape), lambda *_: (l,) + (0,) * nd, pipeline_mode=pl.Buffered(1))


def _proj_call(x, mods_l, gn, w_in, l, rope_tabs, gq128, gk128, bd):
    cos, sa, sb = rope_tabs
    tok = lambda w: pl.BlockSpec((TM, w), lambda i: (i, 0))
    rope_spec = pl.BlockSpec((TM, LANES), lambda i: (jnp.where(i < LAT_TILES, i % TILES_PER_BATCH,
                                                               TILES_PER_BATCH), 0))
    widths = (Q_W, 2 * KV_W, FOURIER_W, POOL_W, CONV_W)
    return pl.pallas_call(
        _proj_kernel,
        out_shape=[jax.ShapeDtypeStruct((N_TOK, w), BF16) for w in widths],
        grid=(NT,),
        in_specs=[tok(D),
                  pl.BlockSpec((None, 6, D), lambda i: (_mod_row(i), 0, 0)),
                  _const_spec((1, D)),
                  _layer_spec((D, PROJ_W), l),
                  rope_spec, rope_spec, rope_spec,
                  _const_spec((1, LANES)), _const_spec((1, LANES)),
                  _const_spec((LANES, LANES))],
        out_specs=[tok(w) for w in widths],
        scratch_shapes=[pltpu.VMEM((D, PROJ_W), BF16)],
        compiler_params=_cparams(("arbitrary",)),
        name="proj",
    )(x, mods_l, gn, w_in, cos, sa, sb, gq128, gk128, bd)


def _attend_many(jobs, sink_ref):
    lane = lax.broadcasted_iota(jnp.int32, (ATTN_BLK, LANES), 1)
    chains = []
    for q, kv_blocks, biases in jobs:
        for j in range(N_KV_HEADS):
            ks = slice(j * HEAD_DIM, (j + 1) * HEAD_DIM)
            vs = slice(KV_W + j * HEAD_DIM, KV_W + (j + 1) * HEAD_DIM)
            kj = jnp.concatenate([blk[:, ks] for blk in kv_blocks], axis=0)
            vj = jnp.concatenate([blk[:, vs] for blk in kv_blocks], axis=0)
            vaug = jnp.concatenate([vj, jnp.ones_like(vj)], axis=1)
            qs = jnp.concatenate([q[:, (j * GQA + g) * HEAD_DIM:(j * GQA + g + 1) * HEAD_DIM]
                                  for g in range(GQA)], axis=0)
            s = lax.dot_general(qs, kj, (((1,), (1,)), ((), ())), preferred_element_type=F32)
            chains.append((j, s, vaug, kv_blocks, biases))
    soft = []
    for j, s, vaug, kv_blocks, biases in chains:
        probs, sink_terms = [], []
        for g in range(GQA):
            sg = s[g * ATTN_BLK:(g + 1) * ATTN_BLK]
            pieces, col = [], 0
            for blk, bias in zip(kv_blocks, biases):
                piece = sg[:, col:col + blk.shape[0]]
                pieces.append(piece if bias is None else piece + bias)
                col += blk.shape[0]
            sg = jnp.concatenate(pieces, axis=1)
            sk = sink_ref[j * GQA + g] * LOG2E
            mx = jnp.maximum(jnp.max(sg, axis=-1, keepdims=True), sk)
            probs.append(jnp.exp2(sg - mx).astype(BF16))
            sink_terms.append(jnp.exp2(sk - mx))
        soft.append((jnp.concatenate(probs, axis=0), vaug, sink_terms))
    heads = []
    for p, vaug, sink_terms in soft:
        o = _dot(p, vaug)
        for g in range(GQA):
            og = o[g * ATTN_BLK:(g + 1) * ATTN_BLK]
            heads.append(og / (og[:, HEAD_DIM:HEAD_DIM + 1] + sink_terms[g]))
    outs = []
    for n in range(len(jobs)):
        hs = heads[n * N_Q_HEADS:(n + 1) * N_Q_HEADS]
        tiles = [jnp.where(lane < HEAD_DIM, hs[2 * t], pltpu.roll(hs[2 * t + 1], HEAD_DIM, 1))
                 for t in range(N_Q_HEADS // 2)]
        outs.append(jnp.concatenate(tiles, axis=1).astype(BF16))
    return outs


def _attn_latent_kernel(sink_ref, q_ref, prev_ref, cur_ref, next_ref, ctx_ref, o_ref):
    n = pl.program_id(1)
    r = lax.broadcasted_iota(jnp.int32, (ATTN_BLK, ATTN_BLK), 0)
    jj = lax.broadcasted_iota(jnp.int32, (ATTN_BLK, ATTN_BLK), 1)
    far = jnp.int32(2 * ATTN_BLK)
    off_prev = jnp.where(n > 0, 0, far)
    off_next = jnp.where(n < S // (ATTN_QB * ATTN_BLK) - 1, 0, far)
    prev_ok = jnp.where(jj - r >= 0, 0.0, NEG_INF)
    next_ok = jnp.where(r - jj >= 0, 0.0, NEG_INF)
    prev_edge = jnp.where(jj - r - off_prev >= 0, 0.0, NEG_INF)
    next_edge = jnp.where(r - jj - off_next >= 0, 0.0, NEG_INF)
    ctx = ctx_ref[...]
    rows = lambda b: slice(b * ATTN_BLK, (b + 1) * ATTN_BLK)
    blocks = [prev_ref[...]] + [cur_ref[rows(b), :] for b in range(ATTN_QB)] + [next_ref[...]]
    jobs = [(q_ref[rows(b), :], [ctx] + blocks[b:b + 3],
             [None, prev_edge if b == 0 else prev_ok, None, next_edge if b == ATTN_QB - 1 else next_ok])
            for b in range(ATTN_QB)]
    for b, out in enumerate(_attend_many(jobs, sink_ref)):
        o_ref[rows(b), :] = out


def _attn_context_kernel(sink_ref, q_ref, ctx_ref, o_ref):
    o_ref[...] = _attend_many([(q_ref[...], [ctx_ref[...]], [None])], sink_ref)[0]


def _attn_call(sink_l, q, kv):
    nb = S // ATTN_BLK
    nq = nb // ATTN_QB
    smem = pl.BlockSpec(memory_space=pltpu.SMEM)
    pair = lambda w: pl.BlockSpec((ATTN_QB * ATTN_BLK, w), lambda b, n: (b * nq + n, 0))
    prev = pl.BlockSpec((ATTN_BLK, 2 * KV_W), lambda b, n: (b * nb + jnp.maximum(ATTN_QB * n - 1, 0), 0))
    nxt = pl.BlockSpec((ATTN_BLK, 2 * KV_W),
                       lambda b, n: (b * nb + jnp.minimum(ATTN_QB * (n + 1), nb - 1), 0))
    ctxs = pl.BlockSpec((C, 2 * KV_W), lambda b, n: (N_LAT // C + b, 0))
    lat = pl.pallas_call(
        _attn_latent_kernel,
        out_shape=jax.ShapeDtypeStruct((N_LAT, Q_W), BF16),
        grid=(B, nq),
        in_specs=[smem, pair(Q_W), prev, pair(2 * KV_W), nxt, ctxs],
        out_specs=pair(Q_W),
        compiler_params=_cparams(("parallel", "parallel")),
        name="attn_latent",
    )(sink_l, q, kv, kv, kv, kv)
    ncb = C // ATTN_BLK
    base = N_LAT // ATTN_BLK
    ctx = pl.pallas_call(
        _attn_context_kernel,
        out_shape=jax.ShapeDtypeStruct((N_CTX, Q_W), BF16),
        grid=(B, ncb),
        in_specs=[smem, pl.BlockSpec((ATTN_BLK, Q_W), lambda b, n: (base + b * ncb + n, 0)), ctxs],
        out_specs=pl.BlockSpec((ATTN_BLK, Q_W), lambda b, n: (b * ncb + n, 0)),
        compiler_params=_cparams(("parallel", "parallel")),
        name="attn_context",
    )(sink_l, q, kv)
    return lat, ctx


def _f1_kernel(w_ref, f_ref, re_ref, im_ref):
    res = lax.dot_general(w_ref[...], f_ref[...], (((1,), (0,)), ((), ())), preferred_element_type=F32)
    re_ref[...] = res[:FS1].astype(BF16)
    im_ref[...] = res[FS1:].astype(BF16)


def _f2_kernel(ta_ref, tb_ref, re_ref, im_ref, after_ref, o_ref):
    del after_ref
    for i in range(F2_K1):
        res = _dot(ta_ref[i], re_ref[i]) + _dot(tb_ref[i], im_ref[i])
        o_ref[i, :, 0:FOURIER_W] = res[:FS2].astype(BF16)
        o_ref[i, :, FOURIER_W:2 * FOURIER_W] = res[FS2:].astype(BF16)


def _fc_kernel(w_ref, f_ref, o_ref):
    res = _dot(w_ref[...], f_ref[...])
    o_ref[:, 0:FOURIER_W] = res[:C].astype(BF16)
    o_ref[:, FOURIER_W:2 * FOURIER_W] = res[C:].astype(BF16)


def _fourier_stage1_call(f, tabs):
    w1 = tabs[0]
    f3 = f.reshape(N_TOK // FS2, FS2, FOURIER_W)
    blk = pl.BlockSpec((FS1, F1_ROWS, FOURIER_W), lambda b, j: (b, j, 0))
    return pl.pallas_call(
        _f1_kernel,
        out_shape=[jax.ShapeDtypeStruct((B * FS1, FS2, FOURIER_W), BF16)] * 2,
        grid=(B, FS2 // F1_ROWS),
        in_specs=[_const_spec((2 * FS1, FS1)), blk],
        out_specs=[blk, blk],
        compiler_params=_cparams(("parallel", "parallel")),
        name="fourier_stage1",
    )(w1, f3)


def _fourier_stage2_call(a_re, a_im, f, tabs, after):
    _, ta, tb, wc = tabs
    nk = FS1 // F2_K1
    aspec = pl.BlockSpec((F2_K1, FS2, FOURIER_W), lambda b, k1: (b * nk + k1, 0, 0))
    tspec = pl.BlockSpec((F2_K1, 2 * FS2, FS2), lambda b, k1: (k1, 0, 0))
    h_t = pl.pallas_call(
        _f2_kernel,
        out_shape=jax.ShapeDtypeStruct((B, FS1, FS2, 2 * FOURIER_W), BF16),
        grid=(B, nk),
        in_specs=[tspec, tspec, aspec, aspec, pl.BlockSpec(memory_space=pl.ANY)],
        out_specs=pl.BlockSpec((None, F2_K1, FS2, 2 * FOURIER_W), lambda b, k1: (b, k1, 0, 0)),
        compiler_params=_cparams(("parallel", "parallel")),
        name="fourier_stage2",
    )(ta, tb, a_re, a_im, after)
    h_lat = jnp.transpose(h_t, (0, 2, 1, 3)).reshape(N_LAT, 2 * FOURIER_W)
    h_ctx = pl.pallas_call(
        _fc_kernel,
        out_shape=jax.ShapeDtypeStruct((N_CTX, 2 * FOURIER_W), BF16),
        grid=(B,),
        in_specs=[_const_spec((2 * C, C)),
                  pl.BlockSpec((C, FOURIER_W), lambda b: (N_LAT // C + b, 0))],
        out_specs=pl.BlockSpec((C, 2 * FOURIER_W), lambda b: (b, 0)),
        compiler_params=_cparams(("parallel",)),
        name="fourier_context",
    )(wc, f)
    return h_lat, h_ctx


def _poolconv_kernel(pc_ref, pp_ref, pn_ref, uc_ref, up_ref, un_ref, bm_ref, bh_ref, ic_ref, pw_ref, ps_ref,
                     sh_ref, cw_ref, cb_ref, cg_ref, cnb_ref, z_ref, a_ref, win0_ref, win1_ref, cv_ref):
    t = pl.program_id(0)
    lat_tiles = N_LAT // TP
    per_seq = S // TP
    is_ctx = t >= lat_tiles
    first = jnp.logical_or(t % per_seq == 0, is_ctx)
    last = jnp.logical_or(t % per_seq == per_seq - 1, is_ctx)

    keep_prev = jnp.where(first, 0.0, 1.0)
    keep_next = jnp.where(last, 0.0, 1.0)

    ub = jnp.concatenate([(up_ref[...].astype(F32) * keep_prev).astype(BF16), uc_ref[...],
                          (un_ref[...].astype(F32) * keep_next).astype(BF16)], axis=0)
    off = HALO - CONV_K // 2
    half_rows = TP // 2
    wins = (win0_ref, win1_ref)
    for hf, win_ref in enumerate(wins):
        window = ub[hf * half_rows:hf * half_rows + CONV_WIN]
        win_ref[0] = window.astype(F32)
        for s in range(1, 8):
            win_ref[s] = _dot(sh_ref[s - 1], window)

    pcur = pc_ref[...]
    halo = jnp.concatenate([pp_ref[...].astype(F32) * keep_prev,
                            pn_ref[...].astype(F32) * keep_next], axis=0).astype(BF16)
    sums = []
    for gi in range(len(POOL_WINDOWS)):
        cs = slice(gi * LANES, (gi + 2) * LANES)
        sums.append(_dot(bm_ref[gi], pcur[:, cs]) + _dot(bh_ref[gi], halo[:, cs]))
    lane_t = lax.broadcasted_iota(jnp.int32, (TP, LANES), 1)
    tiles = [sums[0][:, :LANES]]
    for gi in range(1, len(POOL_WINDOWS)):
        split = gi * POOL_GROUP_W - gi * LANES
        tiles.append(jnp.where(lane_t < split, sums[gi - 1][:, LANES:], sums[gi][:, :LANES]))
    tiles.append(sums[-1][:, LANES:])
    zsum = jnp.concatenate(tiles, axis=1)
    z = zsum * ic_ref[...] - pcur.astype(F32)
    z_ref[...] = (_dot(z.astype(BF16), pw_ref[...]) * ps_ref[...]).astype(BF16)

    for hf, win_ref in enumerate(wins):
        base = hf * half_rows
        for cb in range(CONV_W // LANES):
            cs = slice(cb * LANES, (cb + 1) * LANES)
            acc = jnp.zeros((half_rows, LANES), F32) + cb_ref[:, cs]
            for j in range(CONV_K):
                s, m = (off + j) % 8, (off + j) // 8
                acc = acc + win_ref[s, 8 * m:8 * m + half_rows, cs] * cw_ref[j:j + 1, cs]
            cv_ref[base:base + half_rows, cs] = acc
    cv = cv_ref[...]
    mu = jnp.mean(cv, axis=-1, keepdims=True)
    var = jnp.mean(jnp.square(cv - mu), axis=-1, keepdims=True)
    un = (cv - mu) * lax.rsqrt(var + EPS) * cg_ref[...] + cnb_ref[...]
    a_ref[...] = (un * _sigmoid(un)).astype(BF16)


def _poolconv_call(p, u, band_main, band_halo, inv_cnt, pw_bd, pool_scale, shifts, conv_w, conv_b, cn_g, cn_b):
    nh = TP // HALO
    last_h = N_TOK // HALO - 1
    cur = lambda w: pl.BlockSpec((TP, w), lambda t: (t, 0))
    prv = lambda w: pl.BlockSpec((HALO, w), lambda t: (jnp.maximum(t * nh - 1, 0), 0))
    nxt = lambda w: pl.BlockSpec((HALO, w), lambda t: (jnp.minimum((t + 1) * nh, last_h), 0))
    per_seq = S // TP

    def kind(t):
        return jnp.where(t >= N_LAT // TP, 3, jnp.where(t % per_seq == 0, 1, jnp.where(t % per_seq == per_seq - 1, 2, 0)))

    return pl.pallas_call(
        _poolconv_kernel,
        out_shape=[jax.ShapeDtypeStruct((N_TOK, POOL_W), BF16),
                   jax.ShapeDtypeStruct((N_TOK, CONV_W), BF16)],
        grid=(NTP,),
        in_specs=[cur(POOL_W), prv(POOL_W), nxt(POOL_W), cur(CONV_W), prv(CONV_W), nxt(CONV_W),
                  _const_spec((4, TP, TP)), _const_spec((4, TP, 2 * HALO)),
                  pl.BlockSpec((None, TP, POOL_W), lambda t: (kind(t), 0, 0)),
                  _const_spec((POOL_W, POOL_W)), _const_spec((1, POOL_W)),
                  _const_spec((7, CONV_WIN, CONV_WIN)),
                  _const_spec((CONV_K, CONV_W)), _const_spec((1, CONV_W)),
                  _const_spec((1, CONV_W)), _const_spec((1, CONV_W))],
        out_specs=[cur(POOL_W), cur(CONV_W)],
        scratch_shapes=[pltpu.VMEM((8, CONV_WIN, CONV_W), F32), pltpu.VMEM((8, CONV_WIN, CONV_W), F32),
                        pltpu.VMEM((TP, CONV_W), F32)],
        compiler_params=_cparams(("parallel",)),
        name="pool_conv",
    )(p, p, p, u, u, u, band_main, band_halo, inv_cnt, pw_bd, pool_scale, shifts, conv_w, conv_b, cn_g, cn_b)


def _route(logits):
    lane = lax.broadcasted_iota(jnp.int32, logits.shape, 1)
    big = jnp.int32(LANES)
    lg = jnp.where(lane < N_GROUPS, logits, NEG_INF)
    mg = jnp.max(lg, axis=-1, keepdims=True)
    grp = jnp.min(jnp.where(lg == mg, lane, big), axis=-1, keepdims=True)
    p_grp = 1.0 / jnp.sum(jnp.exp(lg - mg), axis=-1, keepdims=True)
    lo = N_GROUPS + grp * EPG
    le = jnp.where((lane >= lo) & (lane < lo + EPG), logits, NEG_INF)
    m1 = jnp.max(le, axis=-1, keepdims=True)
    i1 = jnp.min(jnp.where(le == m1, lane, big), axis=-1, keepdims=True)
    le2 = jnp.where(lane == i1, NEG_INF, le)
    m2 = jnp.max(le2, axis=-1, keepdims=True)
    i2 = jnp.min(jnp.where(le2 == m2, lane, big), axis=-1, keepdims=True)
    r = jnp.exp(m2 - m1)
    w1 = p_grp / (1.0 + r)
    w2 = p_grp * r / (1.0 + r)
    e1 = (i1 - N_GROUPS).astype(F32)
    e2 = (i2 - N_GROUPS).astype(F32)
    return jnp.where(lane == 0, e1, jnp.where(lane == 1, e2, jnp.where(lane == 2, w1,
                     jnp.where(lane == 3, w2, 0.0))))


def _mix_kernel(x_ref, mod_ref, gn_ref, al_ref, ac_ref, hl_ref, hc_ref, z_ref, cv_ref,
                wa_ref, wf_ref, wp_ref, wc_ref, wg_ref, bg_ref, wo_ref, gf_ref, rh_ref, rl_ref, rb_ref,
                xo_ref, h2_ref, rt_ref):
    is_ctx = pl.program_id(0) >= LAT_TILES
    m = mod_ref[...]
    x = x_ref[...]
    hb = _modulate(x, gn_ref[...], m[0:1], m[1:2]).astype(BF16)
    attn = jnp.where(is_ctx, ac_ref[...], al_ref[...])
    four = jnp.where(is_ctx, hc_ref[...], hl_ref[...])
    branches = ((attn, wa_ref), (four, wf_ref), (z_ref[...], wp_ref), (cv_ref[...], wc_ref))
    acc = None
    for bi, (inp, w_ref) in enumerate(branches):
        cs = slice(bi * D, (bi + 1) * D)
        gate = _sigmoid(_dot(hb, wg_ref[:, cs]) + bg_ref[:, cs])
        term = gate * _dot(inp, w_ref[...])
        acc = term if acc is None else acc + term
    x_new = x + m[2:3] * _dot(acc.astype(BF16), wo_ref[...])
    xo_ref[...] = x_new
    h2 = _modulate(x_new, gf_ref[...], m[3:4], m[4:5])
    hi = h2.astype(BF16)
    h2_ref[...] = hi
    lo = (h2 - hi.astype(F32)).astype(BF16)
    logits = _dot(hi, rh_ref[...]) + _dot(lo, rh_ref[...]) + _dot(hi, rl_ref[...]) + rb_ref[...]
    rt_ref[...] = _route(logits)


def _mix_call(x, mods_l, gn, a_lat, a_ctx, h_lat, h_ctx, zc, cact, l, stacked, small):
    tok = lambda w: pl.BlockSpec((TM, w), lambda i: (i, 0))
    lat = lambda w: pl.BlockSpec((TM, w), lambda i: (jnp.minimum(i, LAT_TILES - 1), 0))
    wa, wf, wp, wc, wg, wo, rh, rl = stacked
    bg, gf, rb = small
    in_specs = [tok(D), pl.BlockSpec((None, 6, D), lambda i: (_mod_row(i), 0, 0)), _const_spec((1, D)),
                lat(Q_W), _const_spec((N_CTX, Q_W)),
                lat(2 * FOURIER_W), _const_spec((N_CTX, 2 * FOURIER_W)),
                tok(POOL_W), tok(CONV_W)]
    in_specs += [_layer_spec(w.shape[1:], l) for w in (wa, wf, wp, wc, wg)]
    in_specs += [_const_spec(bg.shape), _layer_spec(wo.shape[1:], l), _const_spec(gf.shape),
                 _layer_spec(rh.shape[1:], l), _layer_spec(rl.shape[1:], l), _const_spec(rb.shape)]
    return pl.pallas_call(
        _mix_kernel,
        out_shape=[jax.ShapeDtypeStruct((N_TOK, D), F32), jax.ShapeDtypeStruct((N_TOK, D), BF16),
                   jax.ShapeDtypeStruct((N_TOK, LANES), F32)],
        grid=(NT,),
        in_specs=in_specs,
        out_specs=[tok(D), tok(D), tok(LANES)],
        compiler_params=_cparams(("parallel",)),
        name="mix",
    )(x, mods_l, gn, a_lat, a_ctx, h_lat, h_ctx, zc, cact, wa, wf, wp, wc, wg, bg, wo, gf, rh, rl, rb)


def _onehots(route):
    lane = lax.broadcasted_iota(jnp.int32, route.shape, 1)
    e1 = route[:, 0:1].astype(jnp.int32)
    e2 = route[:, 1:2].astype(jnp.int32)
    return (lane == e1).astype(F32), (lane == e2).astype(F32)


def _lane_cumsum(row):
    lane = lax.broadcasted_iota(jnp.int32, row.shape, 1)
    sh = 1
    while sh < N_EXPERTS:
        row = row + jnp.where(lane >= sh, pltpu.roll(row, sh, 1), 0.0)
        sh *= 2
    return row


def _rank_kernel(rt_ref, tri_ref, pos_ref, meta_ref, cnt_ref, carry_ref):
    i = pl.program_id(0)

    @pl.when(i == 0)
    def _():
        carry_ref[...] = jnp.zeros_like(carry_ref)

    oh1, oh2 = _onehots(rt_ref[...])
    both = oh1 + oh2
    carry = carry_ref[0:1, :]
    tile_cnt = jnp.sum(both, axis=0, keepdims=True)
    tile_cnt = tile_cnt + (tile_cnt - 2.0 * jnp.floor(tile_cnt * 0.5))
    tile_off = _lane_cumsum(tile_cnt) - tile_cnt
    where = _dot(tri_ref[...], both.astype(BF16)) + tile_off
    p1 = jnp.sum(oh1 * where, axis=-1, keepdims=True)
    p2 = jnp.sum(oh2 * where, axis=-1, keepdims=True)
    lane = lax.broadcasted_iota(jnp.int32, both.shape, 1)
    pos_ref[...] = jnp.where(lane == 0, p1, jnp.where(lane == 1, p2, 0.0))
    row = lax.broadcasted_iota(jnp.int32, meta_ref.shape, 0)
    meta_ref[...] = jnp.where(row == 0, tile_off, jnp.where(row == 1, tile_cnt, jnp.where(row == 2, carry, 0.0)))
    total = carry + tile_cnt
    carry_ref[...] = jnp.broadcast_to(total, carry_ref.shape)
    cnt_ref[...] = jnp.broadcast_to(total, cnt_ref.shape)


def _runs_kernel(meta_ref, cnt_ref, runs_ref, be_ref):
    lane = lax.broadcasted_iota(jnp.int32, (1, LANES), 1)
    counts = cnt_ref[0:1, :]
    padded = jnp.floor((counts + (MOE_BLK - 1)) * (1.0 / MOE_BLK)) * MOE_BLK
    padded = jnp.where(lane < N_EXPERTS, padded, 0.0)
    ends = _lane_cumsum(padded)
    starts = ends - padded
    for t in range(NT):
        m = meta_ref[t]
        row = lax.broadcasted_iota(jnp.int32, m.shape, 0)
        m = jnp.where(row == 2, m + starts, m)
        m = jnp.where(row == 3, starts + counts, jnp.where(row == 4, padded - counts, m))
        runs_ref[t] = m.astype(jnp.int32)
    blk = lax.broadcasted_iota(jnp.int32, be_ref.shape, 0).astype(F32) * MOE_BLK
    lane_b = lax.broadcasted_iota(jnp.int32, be_ref.shape, 1)
    done = jnp.where((ends <= blk) & (lane_b < N_EXPERTS), 1.0, 0.0)
    be = jnp.minimum(jnp.sum(done, axis=-1, keepdims=True), N_EXPERTS - 1.0)
    nblk = jnp.max(jnp.where(lane_b == N_EXPERTS - 1, ends, 0.0), axis=-1, keepdims=True) * (1.0 / MOE_BLK)
    be_ref[...] = jnp.where(lane_b == 0, be, jnp.where(lane_b == 1, nblk, 0.0)).astype(jnp.int32)


def _plan_call(route, tri):
    tok = pl.BlockSpec((TM, LANES), lambda i: (i, 0))
    pos, meta, counts = pl.pallas_call(
        _rank_kernel,
        out_shape=[jax.ShapeDtypeStruct((N_TOK, LANES), F32), jax.ShapeDtypeStruct((NT, 8, LANES), F32),
                   jax.ShapeDtypeStruct((8, LANES), F32)],
        grid=(NT,),
        in_specs=[tok, _const_spec((TM, TM))],
        out_specs=[tok, pl.BlockSpec((None, 8, LANES), lambda i: (i, 0, 0)),
                   pl.BlockSpec((8, LANES), lambda i: (0, 0))],
        scratch_shapes=[pltpu.VMEM((8, LANES), F32)],
        compiler_params=_cparams(("arbitrary",)),
        name="moe_rank",
    )(route, tri)
    runs, blk = pl.pallas_call(
        _runs_kernel,
        out_shape=[jax.ShapeDtypeStruct((NT, 8, LANES), jnp.int32),
                   jax.ShapeDtypeStruct((256, LANES), jnp.int32)],
        name="moe_runs",
    )(meta, counts)
    runs_flat = runs[:, 0:RUN_FIELDS, 0:N_EXPERTS].reshape(NT, 1, RUN_FIELDS * N_EXPERTS)
    return pos, runs_flat, blk[:N_MOE_BLOCKS, 0], blk[0:1, 1]


def _pack_pairs(x):
    half = x.shape[1] // 2
    lo = pltpu.bitcast(x[:, :half], jnp.uint32)
    hi = pltpu.bitcast(x[:, half:], jnp.uint32)
    return (lo >> 16) | (hi & jnp.uint32(0xFFFF0000))


def _unpack_pairs(w):
    lo = pltpu.bitcast(w << 16, F32)
    hi = pltpu.bitcast(w & jnp.uint32(0xFFFF0000), F32)
    return jnp.concatenate([lo, hi], axis=1).astype(BF16)


def _run_fields(runs_ref, e):
    return runs_ref[0, e], runs_ref[0, N_EXPERTS + e], runs_ref[0, 2 * N_EXPERTS + e]


def _store_rows(lin_ref, packed):
    rows = packed.shape[0]
    for c in range(PK):
        lin_ref[pl.ds(c, rows, stride=PK), :] = packed[:, c * LANES:(c + 1) * LANES]


def _load_rows(lin_ref, rows):
    return jnp.concatenate([lin_ref[pl.ds(c, rows, stride=PK), :] for c in range(PK)], axis=1)


def _lin(ref, row, nrows):
    return ref.at[pl.ds(pl.multiple_of(row * PK, 8), nrows * PK), :]


FETCH_ROWS = 64
RUN_PIECES = (32, 16, 8, 4, 2)
TAIL_PIECES = (256, 128, 64, 32, 16, 8, 4, 2)


def _run_copies(runs, make_copy, act):
    def per_expert(e, carry):
        off, n, dst = _run_fields(runs, e)
        whole = n // FETCH_ROWS

        def chunk(k, c):
            act(make_copy(off + k * FETCH_ROWS, dst + k * FETCH_ROWS, FETCH_ROWS))
            return c

        lax.fori_loop(0, whole, chunk, 0)
        done = whole * FETCH_ROWS
        for size in RUN_PIECES:
            @pl.when((n & size) != 0)
            def _(done=done, size=size):
                act(make_copy(off + done, dst + done, size))
            done = done + (n & size)
        return carry

    lax.fori_loop(0, N_EXPERTS, per_expert, 0)


def _rows_wait(runs, make_copy):
    total = lax.fori_loop(0, N_EXPERTS, lambda e, acc: acc + runs[0, N_EXPERTS + e], jnp.int32(0))
    lax.fori_loop(0, total // FETCH_ROWS, lambda k, c: (make_copy(0, 0, FETCH_ROWS).wait(), c)[1], 0)
    for size in RUN_PIECES:
        @pl.when((total & size) != 0)
        def _(size=size):
            make_copy(0, 0, size).wait()


def _dispatch_kernel(runs_ref, prev_runs_ref, pos_ref, h2_ref, xs_ref, buf0, buf1, zero_ref, sem0, sem1, zsem,
                     ssem):
    i = pl.program_id(0)
    last = NT - 1

    def tail_copies(act):
        def per_expert(e, carry):
            row = runs_ref[0, 3 * N_EXPERTS + e]
            n = runs_ref[0, 4 * N_EXPERTS + e]
            done = jnp.int32(0)
            for size in TAIL_PIECES:
                @pl.when((n & size) != 0)
                def _(done=done, size=size):
                    act(pltpu.make_async_copy(zero_ref.at[pl.ds(0, size * PK), :], _lin(xs_ref, row + done, size), zsem))
                done = done + (n & size)
            return carry

        lax.fori_loop(0, N_EXPERTS, per_expert, 0)

    def spare_copies(act):
        used = runs_ref[0, 3 * N_EXPERTS + N_EXPERTS - 1] + runs_ref[0, 4 * N_EXPERTS + N_EXPERTS - 1]

        def spare_block(k, carry):
            act(pltpu.make_async_copy(zero_ref.at[pl.ds(0, MOE_BLK * PK), :],
                                      _lin(xs_ref, used + k * MOE_BLK, MOE_BLK), ssem))
            return carry

        lax.fori_loop(0, N_MOE_BLOCKS - used // MOE_BLK, spare_block, 0)

    start = lambda d: d.start()
    wait = lambda d: d.wait()

    @pl.when(i == 0)
    def _():
        zero_ref[...] = jnp.zeros_like(zero_ref)
        tail_copies(start)
        spare_copies(start)

    pos = pos_ref[...]
    col = lax.broadcasted_iota(jnp.int32, (TM, TS), 1).astype(F32)
    sel = jnp.where((col == pos[:, 0:1]) | (col == pos[:, 1:2]), 1.0, 0.0).astype(BF16)
    srt = lax.dot_general(sel, h2_ref[...], (((0,), (0,)), ((), ())), preferred_element_type=F32)
    packed = _pack_pairs(srt)

    for par, (buf, sem, obuf, osem) in enumerate(((buf0, sem0, buf1, sem1), (buf1, sem1, buf0, sem0))):
        @pl.when(i % 2 == par)
        def _(buf=buf, sem=sem, obuf=obuf, osem=osem):
            copy = lambda off, dst, rows: pltpu.make_async_copy(_lin(buf, off, rows), _lin(xs_ref, dst, rows), sem)
            ocopy = lambda off, dst, rows: pltpu.make_async_copy(_lin(obuf, off, rows), _lin(xs_ref, dst, rows), osem)
            _store_rows(buf, packed)

            @pl.when(i == 0)
            def _():
                tail_copies(wait)

            _run_copies(runs_ref, copy, start)

            @pl.when(i > 0)
            def _():
                _rows_wait(prev_runs_ref, ocopy)

            @pl.when(i == last)
            def _():
                _rows_wait(runs_ref, copy)
                spare_copies(wait)


def _dispatch_call(runs_flat, pos, h2):
    runs_spec = lambda shift: pl.BlockSpec((None, 1, RUN_FIELDS * N_EXPERTS), lambda i: (jnp.maximum(i - shift, 0), 0, 0),
                                           memory_space=pltpu.SMEM)
    sorted_buf = pltpu.VMEM((TS * PK, LANES), jnp.uint32)
    return pl.pallas_call(
        _dispatch_kernel,
        out_shape=jax.ShapeDtypeStruct((N_SLOTS * PK, LANES), jnp.uint32),
        grid=(NT,),
        in_specs=[runs_spec(0), runs_spec(1),
                  pl.BlockSpec((TM, LANES), lambda i: (i, 0)),
                  pl.BlockSpec((TM, D), lambda i: (i, 0))],
        out_specs=pl.BlockSpec(memory_space=pl.ANY),
        scratch_shapes=[sorted_buf, sorted_buf, pltpu.VMEM((MOE_BLK * PK, LANES), jnp.uint32),
                        pltpu.SemaphoreType.DMA, pltpu.SemaphoreType.DMA, pltpu.SemaphoreType.DMA,
                        pltpu.SemaphoreType.DMA],
        compiler_params=_cparams(("arbitrary",)),
        name="moe_dispatch",
    )(runs_flat, runs_flat, pos, h2)


def _expert_kernel(be_ref, nu_ref, xs_ref, wg_ref, wu_ref, wd_ref, ys_ref, wgb_ref, wub_ref, wdb_ref):
    b = pl.program_id(0)

    @pl.when(jnp.logical_or(b == 0, be_ref[b] != be_ref[jnp.maximum(b - 1, 0)]))
    def _():
        wgb_ref[...] = wg_ref[...].astype(BF16)
        wub_ref[...] = wu_ref[...].astype(BF16)
        wdb_ref[...] = wd_ref[...].astype(BF16)

    @pl.when(b < nu_ref[0])
    def _():
        xb = _unpack_pairs(_load_rows(xs_ref, MOE_BLK))
        g = _dot(xb, wgb_ref[...])
        u = _dot(xb, wub_ref[...])
        hmid = (g * _sigmoid(g)) * u
        y = _dot(hmid.astype(BF16), wdb_ref[...])
        _store_rows(ys_ref, _pack_pairs(y.astype(BF16).astype(F32)))

    @pl.when(b >= nu_ref[0])
    def _():
        ys_ref[...] = jnp.zeros_like(ys_ref)


def _expert_call(blk_e, n_used, xs, wg, wu, wd, l):
    wspec = lambda k, n: pl.BlockSpec((None, None, k, n), lambda b, be, nu: (l, be[b], 0, 0))
    return pl.pallas_call(
        _expert_kernel,
        out_shape=jax.ShapeDtypeStruct((N_SLOTS * PK, LANES), jnp.uint32),
        grid_spec=pltpu.PrefetchScalarGridSpec(
            num_scalar_prefetch=2,
            grid=(N_MOE_BLOCKS,),
            in_specs=[pl.BlockSpec((MOE_BLK * PK, LANES), lambda b, be, nu: (jnp.minimum(b, nu[0] - 1), 0)),
                      wspec(D, EXPERT_HIDDEN), wspec(D, EXPERT_HIDDEN), wspec(EXPERT_HIDDEN, D)],
            out_specs=pl.BlockSpec((MOE_BLK * PK, LANES), lambda b, be, nu: (b, 0)),
            scratch_shapes=[pltpu.VMEM((D, EXPERT_HIDDEN), BF16), pltpu.VMEM((D, EXPERT_HIDDEN), BF16),
                            pltpu.VMEM((EXPERT_HIDDEN, D), BF16)]),
        compiler_params=_cparams(("arbitrary",)),
        name="moe_experts",
    )(blk_e, n_used, xs, wg, wu, wd)


def _combine_kernel(runs_ref, next_runs_ref, ys_ref, pos_ref, x_ref, rt_ref, mod_ref, o_ref, buf0, buf1, sem0, sem1,
                    *, n_tiles):
    i = pl.program_id(0)
    last = n_tiles - 1
    start = lambda d: d.start()
    fetch = lambda buf, sem: (
        lambda off, dst, rows: pltpu.make_async_copy(_lin(ys_ref, dst, rows), _lin(buf, off, rows), sem))

    @pl.when(i == 0)
    def _():
        buf0[...] = jnp.zeros_like(buf0)
        buf1[...] = jnp.zeros_like(buf1)
        _run_copies(runs_ref, fetch(buf0, sem0), start)

    pos = pos_ref[...]
    rt = rt_ref[...]
    col = lax.broadcasted_iota(jnp.int32, (TM, TS), 1).astype(F32)
    pick = (jnp.where(col == pos[:, 0:1], rt[:, TOP_K:TOP_K + 1], 0.0)
            + jnp.where(col == pos[:, 1:2], rt[:, TOP_K + 1:TOP_K + 2], 0.0)).astype(BF16)

    for par, (buf, sem, obuf, osem) in enumerate(((buf0, sem0, buf1, sem1), (buf1, sem1, buf0, sem0))):
        @pl.when(i % 2 == par)
        def _(buf=buf, sem=sem, obuf=obuf, osem=osem):
            @pl.when(i < last)
            def _():
                _run_copies(next_runs_ref, fetch(obuf, osem), start)

            _rows_wait(runs_ref, fetch(buf, sem))
            ysb = _unpack_pairs(_load_rows(buf, TS))
            o_ref[...] = x_ref[...] + mod_ref[5:6, :] * _dot(pick, ysb)


def _combine_call(runs_flat, ys, pos, x, route, mods_l, n_tiles):
    tok = lambda w: pl.BlockSpec((TM, w), lambda i: (i, 0))
    runs_spec = lambda shift: pl.BlockSpec((None, 1, RUN_FIELDS * N_EXPERTS),
                                           lambda i: (jnp.minimum(i + shift, n_tiles - 1), 0, 0),
                                           memory_space=pltpu.SMEM)
    sorted_buf = pltpu.VMEM((TS * PK, LANES), jnp.uint32)
    return pl.pallas_call(
        functools.partial(_combine_kernel, n_tiles=n_tiles),
        out_shape=jax.ShapeDtypeStruct((n_tiles * TM, D), F32),
        grid=(n_tiles,),
        in_specs=[runs_spec(0), runs_spec(1),
                  pl.BlockSpec(memory_space=pl.ANY),
                  tok(LANES), tok(D), tok(LANES),
                  pl.BlockSpec((None, 6, D), lambda i: (_mod_row(i), 0, 0))],
        out_specs=tok(D),
        scratch_shapes=[sorted_buf, sorted_buf, pltpu.SemaphoreType.DMA, pltpu.SemaphoreType.DMA],
        compiler_params=_cparams(("arbitrary",)),
        name="moe_combine",
    )(runs_flat, runs_flat, ys, pos, x, route, mods_l)


def _rope_tables():
    nf = HEAD_DIM // 4
    inv = ROPE_BASE ** (-jnp.arange(nf, dtype=F32) / nf)
    t = jnp.arange(S)
    row = (t // GRID_W).astype(F32)[:, None] * inv[None, :]
    col = (t % GRID_W).astype(F32)[:, None] * inv[None, :]
    zero = jnp.zeros_like(row)
    cos = jnp.concatenate([jnp.cos(row), jnp.cos(row), jnp.cos(col), jnp.cos(col)], axis=1)
    sa = jnp.concatenate([-jnp.sin(row), zero, -jnp.sin(col), zero], axis=1)
    sb = jnp.concatenate([zero, jnp.sin(row), zero, jnp.sin(col)], axis=1)
    ident = (jnp.ones((TM, HEAD_DIM), F32), jnp.zeros((TM, HEAD_DIM), F32), jnp.zeros((TM, HEAD_DIM), F32))
    return tuple(jnp.tile(jnp.concatenate([a, b], axis=0), (1, LANES // HEAD_DIM))
                 for a, b in zip((cos, sa, sb), ident))


def _fourier_tables():
    s1 = np.arange(FS1)
    ang1 = 2.0 * np.pi * np.outer(s1, s1) / FS1
    w1 = np.concatenate([np.cos(ang1), -np.sin(ang1)], axis=0) / np.sqrt(S)
    k1 = np.arange(FS1)[:, None, None]
    k2 = np.arange(FS2)[None, :, None]
    s2 = np.arange(FS2)[None, None, :]
    ang2 = 2.0 * np.pi * ((k1 + FS1 * k2) * s2 % S) / S
    c2, sn2 = np.cos(ang2), np.sin(ang2)
    ta = np.concatenate([c2, -sn2], axis=1)
    tb = np.concatenate([sn2, c2], axis=1)
    sc = np.arange(C)
    angc = 2.0 * np.pi * np.outer(sc, sc) / C
    wc = np.concatenate([np.cos(angc), -np.sin(angc)], axis=0) / np.sqrt(C)
    return tuple(jnp.asarray(a, F32).astype(BF16) for a in (w1, ta, tb, wc))


def _channel_dft():
    cidx = np.arange(FOURIER_GROUP_W)
    ang = 2.0 * np.pi * np.outer(cidx, cidx) / FOURIER_GROUP_W
    eye = np.eye(FOURIER_W // FOURIER_GROUP_W)
    cw = np.kron(eye, np.cos(ang)) / np.sqrt(FOURIER_GROUP_W)
    sw = np.kron(eye, np.sin(ang)) / np.sqrt(FOURIER_GROUP_W)
    return jnp.asarray(np.concatenate([cw, sw], axis=0), F32)


def _pool_bands():
    t = np.arange(TP)[:, None]
    main, halo = [], []
    for w in POOL_WINDOWS:
        def hit(j):
            return ((j - t >= -(w // 2)) & (j - t <= w // 2 - 1)).astype(np.float32)
        main.append(hit(np.arange(TP)[None, :]))
        halo.append(np.concatenate([hit(np.arange(-HALO, 0)[None, :]),
                                    hit(np.arange(TP, TP + HALO)[None, :])], axis=1))
    return (jnp.asarray(np.stack(main), F32).astype(BF16), jnp.asarray(np.stack(halo), F32).astype(BF16))


def _pool_inv_counts():
    win = np.repeat(np.array(POOL_WINDOWS), POOL_GROUP_W)[None, :]

    def table(pos0, seq_len):
        pos = (pos0 + np.arange(TP))[:, None]
        lo = np.clip(pos - win // 2, 0, seq_len)
        hi = np.clip(pos - win // 2 + win, 0, seq_len)
        return 1.0 / (hi - lo)

    tabs = [table(TP, S), table(0, S), table(S - TP, S), table(0, C)]
    return jnp.asarray(np.stack(tabs), F32)


def _conv_shifts():
    i = np.arange(CONV_WIN)
    return jnp.asarray(np.stack([(i[None, :] == i[:, None] + s) for s in range(1, 8)]), F32).astype(BF16)


def _fold_kernel(a_ref, b_ref, o_ref):
    a, b = a_ref[...], b_ref[...]
    a_hi, b_hi = a.astype(BF16), b.astype(BF16)
    a_lo = (a - a_hi.astype(F32)).astype(BF16)
    b_lo = (b - b_hi.astype(F32)).astype(BF16)
    o_ref[...] = (_dot(a_hi, b_hi) + _dot(a_lo, b_hi) + _dot(a_hi, b_lo)).astype(BF16)


def _fold_fourier_weights(dftw, w_br_fourier):
    nl = w_br_fourier.shape[0]
    return pl.pallas_call(
        _fold_kernel,
        out_shape=jax.ShapeDtypeStruct((nl, 2 * FOURIER_W, D), BF16),
        grid=(nl,),
        in_specs=[pl.BlockSpec((2 * FOURIER_W, FOURIER_W), lambda l: (0, 0)),
                  pl.BlockSpec((None, FOURIER_W, D), lambda l: (l, 0, 0))],
        out_specs=pl.BlockSpec((None, 2 * FOURIER_W, D), lambda l: (l, 0, 0)),
        compiler_params=_cparams(("arbitrary",)),
        name="fold_fourier_proj",
    )(dftw, w_br_fourier)


def _block_diag(blocks):
    n, r, c = blocks.shape
    eye = jnp.eye(n, dtype=blocks.dtype)
    return (blocks[:, :, None, :] * eye[:, None, :, None]).reshape(n * r, n * c)


def kernel(x, c, ctx, c_ctx, w_ada, b_ada, g_norm_mix, g_norm_ffn, w_in, g_q, g_k, sink, w_br_attn,
           w_br_fourier, pool_w, pool_scale, w_br_pool, conv_w, conv_b, cn_g, cn_b, w_br_conv, w_gate,
           b_gate, w_out, w_router_grp, b_router_grp, w_router_exp, b_router_exp, w_e_gate, w_e_up,
           w_e_down):
    xs = jnp.concatenate([x.reshape(N_LAT, D), ctx.reshape(N_CTX, D)], axis=0)
    nl = w_ada.shape[0]
    mods = _ada_all(c, c_ctx, w_ada, b_ada).reshape(nl, 8, 6, D)
    rope_tabs = _rope_tables()
    four_tabs = _fourier_tables()
    band_main, band_halo = _pool_bands()
    inv_cnt = _pool_inv_counts()
    shifts = _conv_shifts()
    wf_all = _fold_fourier_weights(_channel_dft(), w_br_fourier)
    bd = jnp.asarray(np.kron(np.eye(LANES // HEAD_DIM), np.ones((HEAD_DIM, HEAD_DIM))), F32).astype(BF16)
    tri = jnp.asarray(np.tril(np.ones((TM, TM)), -1), F32).astype(BF16)
    rpad = jnp.zeros((nl, D, LANES - N_GROUPS - N_EXPERTS), F32)
    w_router = jnp.concatenate([w_router_grp, w_router_exp, rpad], axis=-1)
    r_hi = w_router.astype(BF16)
    r_lo = (w_router - r_hi.astype(F32)).astype(BF16)
    r_b = jnp.concatenate([b_router_grp, b_router_exp, rpad[:, 0, :]], axis=-1).reshape(nl, 1, LANES)
    stacked = tuple(w.astype(BF16) for w in (w_br_attn,)) + (wf_all,) + tuple(
        w.astype(BF16) for w in (w_br_pool, w_br_conv, w_gate, w_out)) + (r_hi, r_lo)

    for l in range(nl):
        mods_l = mods[l]
        gn = g_norm_mix[l].reshape(1, D)
        q, kv, f, p, u = _proj_call(xs, mods_l, gn, w_in, l, rope_tabs,
                                    jnp.tile(g_q[l], 2).reshape(1, LANES),
                                    jnp.tile(g_k[l], 2).reshape(1, LANES), bd)
        a_re, a_im = _fourier_stage1_call(f, four_tabs)
        a_lat, a_ctx = _attn_call(sink[l], q, kv)
        h_lat, h_ctx = _fourier_stage2_call(a_re, a_im, f, four_tabs, a_lat)
        zc, cact = _poolconv_call(p, u, band_main, band_halo, inv_cnt, _block_diag(pool_w[l]).astype(BF16),
                                  pool_scale[l].reshape(1, POOL_W), shifts, conv_w[l], conv_b[l].reshape(1, CONV_W),
                                  cn_g[l].reshape(1, CONV_W), cn_b[l].reshape(1, CONV_W))
        small = (b_gate[l].reshape(1, 4 * D), g_norm_ffn[l].reshape(1, D), r_b[l])
        xs, h2, route = _mix_call(xs, mods_l, gn, a_lat, a_ctx, h_lat, h_ctx, zc, cact, l, stacked, small)
        pos, runs_flat, blk_e, n_used = _plan_call(route, tri)
        slots = _dispatch_call(runs_flat, pos, h2)
        ys = _expert_call(blk_e, n_used, slots, w_e_gate, w_e_up, w_e_down, l)
        xs = _combine_call(runs_flat, ys, pos, xs, route, mods_l, LAT_TILES if l == nl - 1 else NT)
    return xs.reshape(B, S, D)
```

```python
import functools

import numpy as np
import jax
import jax.numpy as jnp
from jax import lax
from jax.experimental import pallas as pl
from jax.experimental.pallas import tpu as pltpu

F32 = jnp.float32
BF16 = jnp.bfloat16

D = 1024
B = 2
S = 8192
C = 256
GRID_W = 64
HEAD_DIM = 64
N_Q_HEADS = 8
N_KV_HEADS = 2
GQA = N_Q_HEADS // N_KV_HEADS
WINDOW = 128
ATTN_BLK = 128
ATTN_QB = 8
ROPE_BASE = 10000.0
Q_W = 512
KV_W = 128
FOURIER_W = 640
FOURIER_GROUP_W = 160
POOL_W = 640
POOL_GROUP_W = 160
POOL_WINDOWS = (2, 4, 8, 16)
CONV_W = 512
CONV_K = 31
PROJ_W = 3072
N_GROUPS = 4
EPG = 8
N_EXPERTS = 32
TOP_K = 2
EXPERT_HIDDEN = 512
MOE_BLK = 512
EPS = 1e-6
NEG_INF = -1e30
LOG2E = 1.4426950408889634

N_LAT = B * S
N_CTX = B * C
N_TOK = N_LAT + N_CTX
TM = 512
NT = N_TOK // TM
LAT_TILES = N_LAT // TM
TILES_PER_BATCH = S // TM
TP = 256
NTP = N_TOK // TP
HALO = 16
CONV_WIN = TP // 2 + 2 * HALO
N_ASSIGN = N_TOK * TOP_K
RANK_TILES = 3
RUN_FIELDS = 5
PK = D // 2 // 128
TS = 1152
N_MOE_BLOCKS = (N_ASSIGN + NT * N_EXPERTS + N_EXPERTS * (MOE_BLK - 1)) // MOE_BLK
N_SLOTS = N_MOE_BLOCKS * MOE_BLK
SUBLANES = 8
BLK_TABLE_ROWS = -(-N_MOE_BLOCKS // SUBLANES) * SUBLANES
ROPE_PAIR = HEAD_DIM // 4
FS1 = 64
FS2 = 128
F1_ROWS = 32
F2_K1 = 16
LANES = 128
VMEM_LIMIT = 56 * 1024 * 1024


def _cparams(sem, vmem=VMEM_LIMIT):
    return pltpu.CompilerParams(dimension_semantics=sem, vmem_limit_bytes=vmem)


def _const_spec(shape):
    nd = len(shape)
    return pl.BlockSpec(shape, lambda *_: (0,) * nd, pipeline_mode=pl.Buffered(1))


def _dot(a, b):
    return jnp.dot(a, b, preferred_element_type=F32)


def _modulate(x, g, shift, scale):
    y = x * lax.rsqrt(jnp.mean(x * x, axis=-1, keepdims=True) + EPS)
    return (y * g) * (1.0 + scale) + shift


def _sigmoid(x):
    return 1.0 / (1.0 + jnp.exp(-x))


def _ada_kernel(ct_ref, w_ref, b_ref, o_ref):
    ct = ct_ref[...]
    s = ct * _sigmoid(ct)
    w = w_ref[...]
    rows = [jnp.sum(w * s[:, r:r + 1], axis=0, keepdims=True) for r in range(3)]
    rows.append(jnp.zeros((5, w.shape[1]), F32))
    o_ref[...] = jnp.concatenate(rows, axis=0) + b_ref[...]


def _ada_all(c, c_ctx, w_ada, b_ada):
    ct = jnp.concatenate([c, c_ctx[None, :], jnp.zeros((5, D), F32)], axis=0).T
    cols = 1536
    nl = w_ada.shape[0]
    return pl.pallas_call(
        _ada_kernel,
        out_shape=jax.ShapeDtypeStruct((nl, 8, 6 * D), F32),
        grid=(nl, 6 * D // cols),
        in_specs=[pl.BlockSpec((D, 8), lambda l, j: (0, 0)),
                  pl.BlockSpec((None, D, cols), lambda l, j: (l, 0, j)),
                  pl.BlockSpec((None, 1, cols), lambda l, j: (l, 0, j))],
        out_specs=pl.BlockSpec((None, 8, cols), lambda l, j: (l, 0, j)),
        compiler_params=_cparams(("arbitrary", "arbitrary")),
        name="adaln",
    )(ct, w_ada, b_ada.reshape(nl, 1, 6 * D))


def _head_rms(t, g128, bd):
    outs = []
    for j in range(t.shape[1] // LANES):
        blk = t[:, j * LANES:(j + 1) * LANES]
        ss = _dot((blk * blk).astype(BF16), bd)
        outs.append(blk * lax.rsqrt(ss * (1.0 / HEAD_DIM) + EPS) * g128)
    return outs


def _rope(blocks, cos, sa, sb):
    outs = []
    for blk in blocks:
        up = pltpu.roll(blk, LANES - ROPE_PAIR, 1)
        dn = pltpu.roll(blk, ROPE_PAIR, 1)
        outs.append(blk * cos + up * sa + dn * sb)
    return outs


def _proj_kernel(x_ref, mod_ref, gn_ref, w_ref, cos_ref, sa_ref, sb_ref, gq_ref, gk_ref, bd_ref,
                 q_ref, kv_ref, f_ref, p_ref, u_ref, wbf_ref):
    @pl.when(pl.program_id(0) == 0)
    def _():
        wbf_ref[...] = w_ref[...].astype(BF16)

    m = mod_ref[...]
    hb = _modulate(x_ref[...], gn_ref[...], m[0:1], m[1:2]).astype(BF16)
    cos, sa, sb, bd = cos_ref[...], sa_ref[...], sb_ref[...], bd_ref[...]
    o_kv, o_f, o_a = Q_W, Q_W + 2 * KV_W, Q_W + 2 * KV_W + FOURIER_W + POOL_W
    qkv = _dot(hb, wbf_ref[:, 0:o_f])
    fp = _dot(hb, wbf_ref[:, o_f:o_a])
    q = _rope(_head_rms(qkv[:, 0:Q_W], gq_ref[...], bd), cos, sa, sb)
    q_ref[...] = (jnp.concatenate(q, axis=1) * (LOG2E * HEAD_DIM ** -0.5)).astype(BF16)
    k = _rope(_head_rms(qkv[:, o_kv:o_kv + KV_W], gk_ref[...], bd), cos, sa, sb)
    kv_ref[:, 0:KV_W] = k[0].astype(BF16)
    kv_ref[:, KV_W:2 * KV_W] = qkv[:, o_kv + KV_W:o_f].astype(BF16)
    ag = _dot(hb, wbf_ref[:, o_a:PROJ_W])
    f_ref[...] = fp[:, 0:FOURIER_W].astype(BF16)
    p_ref[...] = fp[:, FOURIER_W:].astype(BF16)
    u_ref[...] = (ag[:, 0:CONV_W] * _sigmoid(ag[:, CONV_W:])).astype(BF16)


def _mod_row(i):
    return jnp.minimum(i // TILES_PER_BATCH, 2)


def _layer_spec(shape, l):
    nd = len(shape)
    return pl.BlockSpec((None,) + tuple(shape), lambda *_: (l,) + (0,) * nd, pipeline_mode=pl.Buffered(1))


def _proj_call(x, mods_l, gn, w_in, l, rope_tabs, gq128, gk128, bd):
    cos, sa, sb = rope_tabs
    tok = lambda w: pl.BlockSpec((TM, w), lambda i: (i, 0))
    rope_spec = pl.BlockSpec((TM, LANES), lambda i: (jnp.where(i < LAT_TILES, i % TILES_PER_BATCH,
                                                               TILES_PER_BATCH), 0))
    widths = (Q_W, 2 * KV_W, FOURIER_W, POOL_W, CONV_W)
    return pl.pallas_call(
        _proj_kernel,
        out_shape=[jax.ShapeDtypeStruct((N_TOK, w), BF16) for w in widths],
        grid=(NT,),
        in_specs=[tok(D),
                  pl.BlockSpec((None, 6, D), lambda i: (_mod_row(i), 0, 0)),
                  _const_spec((1, D)),
                  _layer_spec((D, PROJ_W), l),
                  rope_spec, rope_spec, rope_spec,
                  _const_spec((1, LANES)), _const_spec((1, LANES)),
                  _const_spec((LANES, LANES))],
        out_specs=[tok(w) for w in widths],
        scratch_shapes=[pltpu.VMEM((D, PROJ_W), BF16)],
        compiler_params=_cparams(("arbitrary",)),
        name="proj",
    )(x, mods_l, gn, w_in, cos, sa, sb, gq128, gk128, bd)


def _attend_many(jobs, sink_ref):
    lane = lax.broadcasted_iota(jnp.int32, (ATTN_BLK, LANES), 1)
    chains = []
    for q, kv_blocks, biases in jobs:
        for j in range(N_KV_HEADS):
            ks = slice(j * HEAD_DIM, (j + 1) * HEAD_DIM)
            vs = slice(KV_W + j * HEAD_DIM, KV_W + (j + 1) * HEAD_DIM)
            kj = jnp.concatenate([blk[:, ks] for blk in kv_blocks], axis=0)
            vj = jnp.concatenate([blk[:, vs] for blk in kv_blocks], axis=0)
            vaug = jnp.concatenate([vj, jnp.ones_like(vj)], axis=1)
            qs = jnp.concatenate([q[:, (j * GQA + g) * HEAD_DIM:(j * GQA + g + 1) * HEAD_DIM]
                                  for g in range(GQA)], axis=0)
            s = lax.dot_general(qs, kj, (((1,), (1,)), ((), ())), preferred_element_type=F32)
            chains.append((j, s, vaug, kv_blocks, biases))
    soft = []
    for j, s, vaug, kv_blocks, biases in chains:
        probs, sink_terms = [], []
        for g in range(GQA):
            sg = s[g * ATTN_BLK:(g + 1) * ATTN_BLK]
            pieces, col = [], 0
            for blk, bias in zip(kv_blocks, biases):
                piece = sg[:, col:col + blk.shape[0]]
                pieces.append(piece if bias is None else piece + bias)
                col += blk.shape[0]
            sg = jnp.concatenate(pieces, axis=1)
            sk = sink_ref[j * GQA + g] * LOG2E
            mx = jnp.maximum(jnp.max(sg, axis=-1, keepdims=True), sk)
            probs.append(jnp.exp2(sg - mx).astype(BF16))
            sink_terms.append(jnp.exp2(sk - mx))
        soft.append((jnp.concatenate(probs, axis=0), vaug, sink_terms))
    heads = []
    for p, vaug, sink_terms in soft:
        o = _dot(p, vaug)
        for g in range(GQA):
            og = o[g * ATTN_BLK:(g + 1) * ATTN_BLK]
            heads.append(og / (og[:, HEAD_DIM:HEAD_DIM + 1] + sink_terms[g]))
    outs = []
    for n in range(len(jobs)):
        hs = heads[n * N_Q_HEADS:(n + 1) * N_Q_HEADS]
        tiles = [jnp.where(lane < HEAD_DIM, hs[2 * t], pltpu.roll(hs[2 * t + 1], HEAD_DIM, 1))
                 for t in range(N_Q_HEADS // 2)]
        outs.append(jnp.concatenate(tiles, axis=1).astype(BF16))
    return outs


def _attn_latent_kernel(sink_ref, q_ref, prev_ref, cur_ref, next_ref, ctx_ref, o_ref):
    n = pl.program_id(1)
    r = lax.broadcasted_iota(jnp.int32, (ATTN_BLK, ATTN_BLK), 0)
    jj = lax.broadcasted_iota(jnp.int32, (ATTN_BLK, ATTN_BLK), 1)
    far = jnp.int32(2 * ATTN_BLK)
    off_prev = jnp.where(n > 0, 0, far)
    off_next = jnp.where(n < S // (ATTN_QB * ATTN_BLK) - 1, 0, far)
    prev_ok = jnp.where(jj - r >= 0, 0.0, NEG_INF)
    next_ok = jnp.where(r - jj >= 0, 0.0, NEG_INF)
    prev_edge = jnp.where(jj - r - off_prev >= 0, 0.0, NEG_INF)
    next_edge = jnp.where(r - jj - off_next >= 0, 0.0, NEG_INF)
    ctx = ctx_ref[...]
    rows = lambda b: slice(b * ATTN_BLK, (b + 1) * ATTN_BLK)
    blocks = [prev_ref[...]] + [cur_ref[rows(b), :] for b in range(ATTN_QB)] + [next_ref[...]]
    jobs = [(q_ref[rows(b), :], [ctx] + blocks[b:b + 3],
             [None, prev_edge if b == 0 else prev_ok, None, next_edge if b == ATTN_QB - 1 else next_ok])
            for b in range(ATTN_QB)]
    for b, out in enumerate(_attend_many(jobs, sink_ref)):
        o_ref[rows(b), :] = out


def _attn_context_kernel(sink_ref, q_ref, ctx_ref, o_ref):
    o_ref[...] = _attend_many([(q_ref[...], [ctx_ref[...]], [None])], sink_ref)[0]


def _attn_call(sink_l, q, kv):
    nb = S // ATTN_BLK
    nq = nb // ATTN_QB
    smem = pl.BlockSpec(memory_space=pltpu.SMEM)
    pair = lambda w: pl.BlockSpec((ATTN_QB * ATTN_BLK, w), lambda b, n: (b * nq + n, 0))
    prev = pl.BlockSpec((ATTN_BLK, 2 * KV_W), lambda b, n: (b * nb + jnp.maximum(ATTN_QB * n - 1, 0), 0))
    nxt = pl.BlockSpec((ATTN_BLK, 2 * KV_W),
                       lambda b, n: (b * nb + jnp.minimum(ATTN_QB * (n + 1), nb - 1), 0))
    ctxs = pl.BlockSpec((C, 2 * KV_W), lambda b, n: (N_LAT // C + b, 0))
    lat = pl.pallas_call(
        _attn_latent_kernel,
        out_shape=jax.ShapeDtypeStruct((N_LAT, Q_W), BF16),
        grid=(B, nq),
        in_specs=[smem, pair(Q_W), prev, pair(2 * KV_W), nxt, ctxs],
        out_specs=pair(Q_W),
        compiler_params=_cparams(("parallel", "parallel")),
        name="attn_latent",
    )(sink_l, q, kv, kv, kv, kv)
    ncb = C // ATTN_BLK
    base = N_LAT // ATTN_BLK
    ctx = pl.pallas_call(
        _attn_context_kernel,
        out_shape=jax.ShapeDtypeStruct((N_CTX, Q_W), BF16),
        grid=(B, ncb),
        in_specs=[smem, pl.BlockSpec((ATTN_BLK, Q_W), lambda b, n: (base + b * ncb + n, 0)), ctxs],
        out_specs=pl.BlockSpec((ATTN_BLK, Q_W), lambda b, n: (b * ncb + n, 0)),
        compiler_params=_cparams(("parallel", "parallel")),
        name="attn_context",
    )(sink_l, q, kv)
    return lat, ctx


def _f1_kernel(w_ref, f_ref, re_ref, im_ref):
    res = lax.dot_general(w_ref[...], f_ref[...], (((1,), (0,)), ((), ())), preferred_element_type=F32)
    re_ref[...] = res[:FS1].astype(BF16)
    im_ref[...] = res[FS1:].astype(BF16)


def _f2_kernel(ta_ref, tb_ref, re_ref, im_ref, after_ref, o_ref):
    del after_ref
    for i in range(F2_K1):
        res = _dot(ta_ref[i], re_ref[i]) + _dot(tb_ref[i], im_ref[i])
        o_ref[i, :, 0:FOURIER_W] = res[:FS2].astype(BF16)
        o_ref[i, :, FOURIER_W:2 * FOURIER_W] = res[FS2:].astype(BF16)


def _fc_kernel(w_ref, f_ref, o_ref):
    res = _dot(w_ref[...], f_ref[...])
    o_ref[:, 0:FOURIER_W] = res[:C].astype(BF16)
    o_ref[:, FOURIER_W:2 * FOURIER_W] = res[C:].astype(BF16)


def _fourier_stage1_call(f, tabs):
    w1 = tabs[0]
    f3 = f.reshape(N_TOK // FS2, FS2, FOURIER_W)
    blk = pl.BlockSpec((FS1, F1_ROWS, FOURIER_W), lambda b, j: (b, j, 0))
    return pl.pallas_call(
        _f1_kernel,
        out_shape=[jax.ShapeDtypeStruct((B * FS1, FS2, FOURIER_W), BF16)] * 2,
        grid=(B, FS2 // F1_ROWS),
        in_specs=[_const_spec((2 * FS1, FS1)), blk],
        out_specs=[blk, blk],
        compiler_params=_cparams(("parallel", "parallel")),
        name="fourier_stage1",
    )(w1, f3)


def _fourier_stage2_call(a_re, a_im, f, tabs, after):
    _, ta, tb, wc = tabs
    nk = FS1 // F2_K1
    aspec = pl.BlockSpec((F2_K1, FS2, FOURIER_W), lambda b, k1: (b * nk + k1, 0, 0))
    tspec = pl.BlockSpec((F2_K1, 2 * FS2, FS2), lambda b, k1: (k1, 0, 0))
    h_t = pl.pallas_call(
        _f2_kernel,
        out_shape=jax.ShapeDtypeStruct((B, FS1, FS2, 2 * FOURIER_W), BF16),
        grid=(B, nk),
        in_specs=[tspec, tspec, aspec, aspec, pl.BlockSpec(memory_space=pl.ANY)],
        out_specs=pl.BlockSpec((None, F2_K1, FS2, 2 * FOURIER_W), lambda b, k1: (b, k1, 0, 0)),
        compiler_params=_cparams(("parallel", "parallel")),
        name="fourier_stage2",
    )(ta, tb, a_re, a_im, after)
    h_lat = jnp.transpose(h_t, (0, 2, 1, 3)).reshape(N_LAT, 2 * FOURIER_W)
    h_ctx = pl.pallas_call(
        _fc_kernel,
        out_shape=jax.ShapeDtypeStruct((N_CTX, 2 * FOURIER_W), BF16),
        grid=(B,),
        in_specs=[_const_spec((2 * C, C)),
                  pl.BlockSpec((C, FOURIER_W), lambda b: (N_LAT // C + b, 0))],
        out_specs=pl.BlockSpec((C, 2 * FOURIER_W), lambda b: (b, 0)),
        compiler_params=_cparams(("parallel",)),
        name="fourier_context",
    )(wc, f)
    return h_lat, h_ctx


def _poolconv_kernel(pc_ref, pp_ref, pn_ref, uc_ref, up_ref, un_ref, bm_ref, bh_ref, ic_ref, pw_ref, ps_ref,
                     sh_ref, cw_ref, cb_ref, cg_ref, cnb_ref, z_ref, a_ref, win0_ref, win1_ref, cv_ref):
    t = pl.program_id(0)
    lat_tiles = N_LAT // TP
    per_seq = S // TP
    is_ctx = t >= lat_tiles
    first = jnp.logical_or(t % per_seq == 0, is_ctx)
    last = jnp.logical_or(t % per_seq == per_seq - 1, is_ctx)

    keep_prev = jnp.where(first, 0.0, 1.0)
    keep_next = jnp.where(last, 0.0, 1.0)

    ub = jnp.concatenate([(up_ref[...].astype(F32) * keep_prev).astype(BF16), uc_ref[...],
                          (un_ref[...].astype(F32) * keep_next).astype(BF16)], axis=0)
    off = HALO - CONV_K // 2
    half_rows = TP // 2
    wins = (win0_ref, win1_ref)
    for hf, win_ref in enumerate(wins):
        window = ub[hf * half_rows:hf * half_rows + CONV_WIN]
        win_ref[0] = window.astype(F32)
        for s in range(1, 8):
            win_ref[s] = _dot(sh_ref[s - 1], window)

    pcur = pc_ref[...]
    halo = jnp.concatenate([pp_ref[...].astype(F32) * keep_prev,
                            pn_ref[...].astype(F32) * keep_next], axis=0).astype(BF16)
    sums = []
    for gi in range(len(POOL_WINDOWS)):
        cs = slice(gi * LANES, (gi + 2) * LANES)
        sums.append(_dot(bm_ref[gi], pcur[:, cs]) + _dot(bh_ref[gi], halo[:, cs]))
    lane_t = lax.broadcasted_iota(jnp.int32, (TP, LANES), 1)
    tiles = [sums[0][:, :LANES]]
    for gi in range(1, len(POOL_WINDOWS)):
        split = gi * POOL_GROUP_W - gi * LANES
        tiles.append(jnp.where(lane_t < split, sums[gi - 1][:, LANES:], sums[gi][:, :LANES]))
    tiles.append(sums[-1][:, LANES:])
    zsum = jnp.concatenate(tiles, axis=1)
    z = zsum * ic_ref[...] - pcur.astype(F32)
    z_ref[...] = (_dot(z.astype(BF16), pw_ref[...]) * ps_ref[...]).astype(BF16)

    for hf, win_ref in enumerate(wins):
        base = hf * half_rows
        for cb in range(CONV_W // LANES):
            cs = slice(cb * LANES, (cb + 1) * LANES)
            acc = jnp.zeros((half_rows, LANES), F32) + cb_ref[:, cs]
            for j in range(CONV_K):
                s, m = (off + j) % 8, (off + j) // 8
                acc = acc + win_ref[s, 8 * m:8 * m + half_rows, cs] * cw_ref[j:j + 1, cs]
            cv_ref[base:base + half_rows, cs] = acc
    cv = cv_ref[...]
    mu = jnp.mean(cv, axis=-1, keepdims=True)
    var = jnp.mean(jnp.square(cv - mu), axis=-1, keepdims=True)
    un = (cv - mu) * lax.rsqrt(var + EPS) * cg_ref[...] + cnb_ref[...]
    a_ref[...] = (un * _sigmoid(un)).astype(BF16)


def _poolconv_call(p, u, band_main, band_halo, inv_cnt, pw_bd, pool_scale, shifts, conv_w, conv_b, cn_g, cn_b):
    nh = TP // HALO
    last_h = N_TOK // HALO - 1
    cur = lambda w: pl.BlockSpec((TP, w), lambda t: (t, 0))
    prv = lambda w: pl.BlockSpec((HALO, w), lambda t: (jnp.maximum(t * nh - 1, 0), 0))
    nxt = lambda w: pl.BlockSpec((HALO, w), lambda t: (jnp.minimum((t + 1) * nh, last_h), 0))
    per_seq = S // TP

    def kind(t):
        return jnp.where(t >= N_LAT // TP, 3, jnp.where(t % per_seq == 0, 1, jnp.where(t % per_seq == per_seq - 1, 2, 0)))

    return pl.pallas_call(
        _poolconv_kernel,
        out_shape=[jax.ShapeDtypeStruct((N_TOK, POOL_W), BF16),
                   jax.ShapeDtypeStruct((N_TOK, CONV_W), BF16)],
        grid=(NTP,),
        in_specs=[cur(POOL_W), prv(POOL_W), nxt(POOL_W), cur(CONV_W), prv(CONV_W), nxt(CONV_W),
                  _const_spec((4, TP, TP)), _const_spec((4, TP, 2 * HALO)),
                  pl.BlockSpec((None, TP, POOL_W), lambda t: (kind(t), 0, 0)),
                  _const_spec((POOL_W, POOL_W)), _const_spec((1, POOL_W)),
                  _const_spec((7, CONV_WIN, CONV_WIN)),
                  _const_spec((CONV_K, CONV_W)), _const_spec((1, CONV_W)),
                  _const_spec((1, CONV_W)), _const_spec((1, CONV_W))],
        out_specs=[cur(POOL_W), cur(CONV_W)],
        scratch_shapes=[pltpu.VMEM((8, CONV_WIN, CONV_W), F32), pltpu.VMEM((8, CONV_WIN, CONV_W), F32),
                        pltpu.VMEM((TP, CONV_W), F32)],
        compiler_params=_cparams(("parallel",)),
        name="pool_conv",
    )(p, p, p, u, u, u, band_main, band_halo, inv_cnt, pw_bd, pool_scale, shifts, conv_w, conv_b, cn_g, cn_b)


def _route(logits):
    lane = lax.broadcasted_iota(jnp.int32, logits.shape, 1)
    big = jnp.int32(LANES)
    lg = jnp.where(lane < N_GROUPS, logits, NEG_INF)
    mg = jnp.max(lg, axis=-1, keepdims=True)
    grp = jnp.min(jnp.where(lg == mg, lane, big), axis=-1, keepdims=True)
    p_grp = 1.0 / jnp.sum(jnp.exp(lg - mg), axis=-1, keepdims=True)
    lo = N_GROUPS + grp * EPG
    le = jnp.where((lane >= lo) & (lane < lo + EPG), logits, NEG_INF)
    m1 = jnp.max(le, axis=-1, keepdims=True)
    i1 = jnp.min(jnp.where(le == m1, lane, big), axis=-1, keepdims=True)
    le2 = jnp.where(lane == i1, NEG_INF, le)
    m2 = jnp.max(le2, axis=-1, keepdims=True)
    i2 = jnp.min(jnp.where(le2 == m2, lane, big), axis=-1, keepdims=True)
    r = jnp.exp(m2 - m1)
    w1 = p_grp / (1.0 + r)
    w2 = p_grp * r / (1.0 + r)
    e1 = (i1 - N_GROUPS).astype(F32)
    e2 = (i2 - N_GROUPS).astype(F32)
    return jnp.where(lane == 0, e1, jnp.where(lane == 1, e2, jnp.where(lane == 2, w1,
                     jnp.where(lane == 3, w2, 0.0))))


def _mix_kernel(x_ref, mod_ref, gn_ref, al_ref, ac_ref, hl_ref, hc_ref, z_ref, cv_ref,
                wa_ref, wf_ref, wp_ref, wc_ref, wg_ref, bg_ref, wo_ref, gf_ref, rh_ref, rl_ref, rb_ref,
                xo_ref, h2_ref, rt_ref):
    is_ctx = pl.program_id(0) >= LAT_TILES
    m = mod_ref[...]
    x = x_ref[...]
    hb = _modulate(x, gn_ref[...], m[0:1], m[1:2]).astype(BF16)
    attn = jnp.where(is_ctx, ac_ref[...], al_ref[...])
    four = jnp.where(is_ctx, hc_ref[...], hl_ref[...])
    branches = ((attn, wa_ref), (four, wf_ref), (z_ref[...], wp_ref), (cv_ref[...], wc_ref))
    acc = None
    for bi, (inp, w_ref) in enumerate(branches):
        cs = slice(bi * D, (bi + 1) * D)
        gate = _sigmoid(_dot(hb, wg_ref[:, cs]) + bg_ref[:, cs])
        term = gate * _dot(inp, w_ref[...])
        acc = term if acc is None else acc + term
    x_new = x + m[2:3] * _dot(acc.astype(BF16), wo_ref[...])
    xo_ref[...] = x_new
    h2 = _modulate(x_new, gf_ref[...], m[3:4], m[4:5])
    hi = h2.astype(BF16)
    h2_ref[...] = hi
    lo = (h2 - hi.astype(F32)).astype(BF16)
    logits = _dot(hi, rh_ref[...]) + _dot(lo, rh_ref[...]) + _dot(hi, rl_ref[...]) + rb_ref[...]
    rt_ref[...] = _route(logits)


def _mix_call(x, mods_l, gn, a_lat, a_ctx, h_lat, h_ctx, zc, cact, l, stacked, small):
    tok = lambda w: pl.BlockSpec((TM, w), lambda i: (i, 0))
    lat = lambda w: pl.BlockSpec((TM, w), lambda i: (jnp.minimum(i, LAT_TILES - 1), 0))
    wa, wf, wp, wc, wg, wo, rh, rl = stacked
    bg, gf, rb = small
    in_specs = [tok(D), pl.BlockSpec((None, 6, D), lambda i: (_mod_row(i), 0, 0)), _const_spec((1, D)),
                lat(Q_W), _const_spec((N_CTX, Q_W)),
                lat(2 * FOURIER_W), _const_spec((N_CTX, 2 * FOURIER_W)),
                tok(POOL_W), tok(CONV_W)]
    in_specs += [_layer_spec(w.shape[1:], l) for w in (wa, wf, wp, wc, wg)]
    in_specs += [_const_spec(bg.shape), _layer_spec(wo.shape[1:], l), _const_spec(gf.shape),
                 _layer_spec(rh.shape[1:], l), _layer_spec(rl.shape[1:], l), _const_spec(rb.shape)]
    return pl.pallas_call(
        _mix_kernel,
        out_shape=[jax.ShapeDtypeStruct((N_TOK, D), F32), jax.ShapeDtypeStruct((N_TOK, D), BF16),
                   jax.ShapeDtypeStruct((N_TOK, LANES), F32)],
        grid=(NT,),
        in_specs=in_specs,
        out_specs=[tok(D), tok(D), tok(LANES)],
        compiler_params=_cparams(("parallel",)),
        name="mix",
    )(x, mods_l, gn, a_lat, a_ctx, h_lat, h_ctx, zc, cact, wa, wf, wp, wc, wg, bg, wo, gf, rh, rl, rb)


def _onehots(route):
    lane = lax.broadcasted_iota(jnp.int32, route.shape, 1)
    e1 = route[:, 0:1].astype(jnp.int32)
    e2 = route[:, 1:2].astype(jnp.int32)
    return (lane == e1).astype(F32), (lane == e2).astype(F32)


def _lane_cumsum(row):
    lane = lax.broadcasted_iota(jnp.int32, row.shape, 1)
    sh = 1
    while sh < N_EXPERTS:
        row = row + jnp.where(lane >= sh, pltpu.roll(row, sh, 1), 0.0)
        sh *= 2
    return row


def _rank_kernel(rt_ref, tri_ref, pos_ref, meta_ref, cnt_ref, carry_ref):
    @pl.when(pl.program_id(0) == 0)
    def _():
        carry_ref[...] = jnp.zeros_like(carry_ref)

    tiles = []
    for t in range(RANK_TILES):
        oh1, oh2 = _onehots(rt_ref[t * TM:(t + 1) * TM, :])
        both = oh1 + oh2
        tiles.append((oh1, oh2, both, _dot(tri_ref[...], both.astype(BF16))))
    carry = carry_ref[0:1, :]
    lane = lax.broadcasted_iota(jnp.int32, (TM, LANES), 1)
    row = lax.broadcasted_iota(jnp.int32, meta_ref.shape[1:], 0)
    for t, (oh1, oh2, both, before) in enumerate(tiles):
        tile_cnt = jnp.sum(both, axis=0, keepdims=True)
        tile_cnt = tile_cnt + (tile_cnt - 2.0 * jnp.floor(tile_cnt * 0.5))
        tile_off = _lane_cumsum(tile_cnt) - tile_cnt
        where = before + tile_off
        p1 = jnp.sum(oh1 * where, axis=-1, keepdims=True)
        p2 = jnp.sum(oh2 * where, axis=-1, keepdims=True)
        pos_ref[t * TM:(t + 1) * TM, :] = jnp.where(lane == 0, p1, jnp.where(lane == 1, p2, 0.0))
        meta_ref[t] = jnp.where(row == 0, tile_off, jnp.where(row == 1, tile_cnt, jnp.where(row == 2, carry, 0.0)))
        carry = carry + tile_cnt
    carry_ref[...] = jnp.broadcast_to(carry, carry_ref.shape)
    cnt_ref[...] = jnp.broadcast_to(carry, cnt_ref.shape)


def _runs_kernel(meta_ref, cnt_ref, runs_ref, be_ref):
    lane = lax.broadcasted_iota(jnp.int32, (1, LANES), 1)
    counts = cnt_ref[0:1, :]
    padded = jnp.floor((counts + (MOE_BLK - 1)) * (1.0 / MOE_BLK)) * MOE_BLK
    padded = jnp.where(lane < N_EXPERTS, padded, 0.0)
    ends = _lane_cumsum(padded)
    starts = ends - padded
    for t in range(NT):
        m = meta_ref[t]
        row = lax.broadcasted_iota(jnp.int32, m.shape, 0)
        m = jnp.where(row == 2, m + starts, m)
        m = jnp.where(row == 3, starts + counts, jnp.where(row == 4, padded - counts, m))
        runs_ref[t] = m.astype(jnp.int32)
    blk = lax.broadcasted_iota(jnp.int32, be_ref.shape, 0).astype(F32) * MOE_BLK
    lane_b = lax.broadcasted_iota(jnp.int32, be_ref.shape, 1)
    done = jnp.where((ends <= blk) & (lane_b < N_EXPERTS), 1.0, 0.0)
    be = jnp.minimum(jnp.sum(done, axis=-1, keepdims=True), N_EXPERTS - 1.0)
    nblk = jnp.max(jnp.where(lane_b == N_EXPERTS - 1, ends, 0.0), axis=-1, keepdims=True) * (1.0 / MOE_BLK)
    be_ref[...] = jnp.where(lane_b == 0, be, jnp.where(lane_b == 1, nblk, 0.0)).astype(jnp.int32)


def _plan_call(route, tri):
    tok = pl.BlockSpec((RANK_TILES * TM, LANES), lambda i: (i, 0))
    pos, meta, counts = pl.pallas_call(
        _rank_kernel,
        out_shape=[jax.ShapeDtypeStruct((N_TOK, LANES), F32), jax.ShapeDtypeStruct((NT, 8, LANES), F32),
                   jax.ShapeDtypeStruct((8, LANES), F32)],
        grid=(NT // RANK_TILES,),
        in_specs=[tok, _const_spec((TM, TM))],
        out_specs=[tok, pl.BlockSpec((RANK_TILES, 8, LANES), lambda i: (i, 0, 0)),
                   pl.BlockSpec((8, LANES), lambda i: (0, 0))],
        scratch_shapes=[pltpu.VMEM((8, LANES), F32)],
        compiler_params=_cparams(("arbitrary",)),
        name="moe_rank",
    )(route, tri)
    runs, blk = pl.pallas_call(
        _runs_kernel,
        out_shape=[jax.ShapeDtypeStruct((NT, 8, LANES), jnp.int32),
                   jax.ShapeDtypeStruct((BLK_TABLE_ROWS, LANES), jnp.int32)],
        name="moe_runs",
    )(meta, counts)
    runs_flat = runs[:, 0:RUN_FIELDS, 0:N_EXPERTS].reshape(NT, 1, RUN_FIELDS * N_EXPERTS)
    return pos, runs_flat, blk[:N_MOE_BLOCKS, 0], blk[0:1, 1]


def _pack_pairs(x):
    half = x.shape[1] // 2
    lo = pltpu.bitcast(x[:, :half], jnp.uint32)
    hi = pltpu.bitcast(x[:, half:], jnp.uint32)
    return (lo >> 16) | (hi & jnp.uint32(0xFFFF0000))


def _unpack_pairs(w):
    lo = pltpu.bitcast(w << 16, F32)
    hi = pltpu.bitcast(w & jnp.uint32(0xFFFF0000), F32)
    return jnp.concatenate([lo, hi], axis=1).astype(BF16)


def _run_fields(runs_ref, e):
    return runs_ref[0, e], runs_ref[0, N_EXPERTS + e], runs_ref[0, 2 * N_EXPERTS + e]


def _store_rows(lin_ref, packed):
    rows = packed.shape[0]
    for c in range(PK):
        lin_ref[pl.ds(c, rows, stride=PK), :] = packed[:, c * LANES:(c + 1) * LANES]


def _load_rows(lin_ref, rows):
    return jnp.concatenate([lin_ref[pl.ds(c, rows, stride=PK), :] for c in range(PK)], axis=1)


def _lin(ref, row, nrows):
    return ref.at[pl.ds(pl.multiple_of(row * PK, SUBLANES), nrows * PK), :]


FETCH_ROWS = 64
RUN_PIECES = (32, 16, 8, 4, 2)
TAIL_PIECES = (256, 128, 64, 32, 16, 8, 4, 2)


def _run_copies(runs, make_copy, act):
    def per_expert(e, carry):
        off, n, dst = _run_fields(runs, e)
        whole = n // FETCH_ROWS

        def chunk(k, c):
            act(make_copy(off + k * FETCH_ROWS, dst + k * FETCH_ROWS, FETCH_ROWS))
            return c

        lax.fori_loop(0, whole, chunk, 0)
        done = whole * FETCH_ROWS
        for size in RUN_PIECES:
            @pl.when((n & size) != 0)
            def _(done=done, size=size):
                act(make_copy(off + done, dst + done, size))
            done = done + (n & size)
        return carry

    lax.fori_loop(0, N_EXPERTS, per_expert, 0)


def _rows_wait(runs, make_copy):
    total = lax.fori_loop(0, N_EXPERTS, lambda e, acc: acc + runs[0, N_EXPERTS + e], jnp.int32(0))
    lax.fori_loop(0, total // FETCH_ROWS, lambda k, c: (make_copy(0, 0, FETCH_ROWS).wait(), c)[1], 0)
    for size in RUN_PIECES:
        @pl.when((total & size) != 0)
        def _(size=size):
            make_copy(0, 0, size).wait()


def _dispatch_kernel(runs_ref, prev_runs_ref, pos_ref, h2_ref, xs_ref, buf0, buf1, zero_ref, sem0, sem1, zsem,
                     ssem):
    i = pl.program_id(0)
    last = NT - 1

    def tail_copies(act):
        def per_expert(e, carry):
            row = runs_ref[0, 3 * N_EXPERTS + e]
            n = runs_ref[0, 4 * N_EXPERTS + e]
            done = jnp.int32(0)
            for size in TAIL_PIECES:
                @pl.when((n & size) != 0)
                def _(done=done, size=size):
                    act(pltpu.make_async_copy(zero_ref.at[pl.ds(0, size * PK), :], _lin(xs_ref, row + done, size), zsem))
                done = done + (n & size)
            return carry

        lax.fori_loop(0, N_EXPERTS, per_expert, 0)

    def spare_copies(act):
        used = runs_ref[0, 3 * N_EXPERTS + N_EXPERTS - 1] + runs_ref[0, 4 * N_EXPERTS + N_EXPERTS - 1]

        def spare_block(k, carry):
            act(pltpu.make_async_copy(zero_ref.at[pl.ds(0, MOE_BLK * PK), :],
                                      _lin(xs_ref, used + k * MOE_BLK, MOE_BLK), ssem))
            return carry

        lax.fori_loop(0, N_MOE_BLOCKS - used // MOE_BLK, spare_block, 0)

    start = lambda d: d.start()
    wait = lambda d: d.wait()

    @pl.when(i == 0)
    def _():
        zero_ref[...] = jnp.zeros_like(zero_ref)
        tail_copies(start)
        spare_copies(start)

    pos = pos_ref[...]
    col = lax.broadcasted_iota(jnp.int32, (TM, TS), 1).astype(F32)
    sel = jnp.where((col == pos[:, 0:1]) | (col == pos[:, 1:2]), 1.0, 0.0).astype(BF16)
    srt = lax.dot_general(sel, h2_ref[...], (((0,), (0,)), ((), ())), preferred_element_type=F32)
    packed = _pack_pairs(srt)

    for par, (buf, sem, obuf, osem) in enumerate(((buf0, sem0, buf1, sem1), (buf1, sem1, buf0, sem0))):
        @pl.when(i % 2 == par)
        def _(buf=buf, sem=sem, obuf=obuf, osem=osem):
            copy = lambda off, dst, rows: pltpu.make_async_copy(_lin(buf, off, rows), _lin(xs_ref, dst, rows), sem)
            ocopy = lambda off, dst, rows: pltpu.make_async_copy(_lin(obuf, off, rows), _lin(xs_ref, dst, rows), osem)
            _store_rows(buf, packed)

            @pl.when(i == 0)
            def _():
                tail_copies(wait)

            _run_copies(runs_ref, copy, start)

            @pl.when(i > 0)
            def _():
                _rows_wait(prev_runs_ref, ocopy)

            @pl.when(i == last)
            def _():
                _rows_wait(runs_ref, copy)
                spare_copies(wait)


def _dispatch_call(runs_flat, pos, h2):
    runs_spec = lambda shift: pl.BlockSpec((None, 1, RUN_FIELDS * N_EXPERTS), lambda i: (jnp.maximum(i - shift, 0), 0, 0),
                                           memory_space=pltpu.SMEM)
    sorted_buf = pltpu.VMEM((TS * PK, LANES), jnp.uint32)
    return pl.pallas_call(
        _dispatch_kernel,
        out_shape=jax.ShapeDtypeStruct((N_SLOTS * PK, LANES), jnp.uint32),
        grid=(NT,),
        in_specs=[runs_spec(0), runs_spec(1),
                  pl.BlockSpec((TM, LANES), lambda i: (i, 0)),
                  pl.BlockSpec((TM, D), lambda i: (i, 0))],
        out_specs=pl.BlockSpec(memory_space=pl.ANY),
        scratch_shapes=[sorted_buf, sorted_buf, pltpu.VMEM((MOE_BLK * PK, LANES), jnp.uint32),
                        pltpu.SemaphoreType.DMA, pltpu.SemaphoreType.DMA, pltpu.SemaphoreType.DMA,
                        pltpu.SemaphoreType.DMA],
        compiler_params=_cparams(("arbitrary",)),
        name="moe_dispatch",
    )(runs_flat, runs_flat, pos, h2)


def _expert_kernel(be_ref, nu_ref, xs_ref, wg_ref, wu_ref, wd_ref, ys_ref, wgb_ref, wub_ref, wdb_ref):
    b = pl.program_id(0)

    @pl.when(jnp.logical_or(b == 0, be_ref[b] != be_ref[jnp.maximum(b - 1, 0)]))
    def _():
        wgb_ref[...] = wg_ref[...].astype(BF16)
        wub_ref[...] = wu_ref[...].astype(BF16)
        wdb_ref[...] = wd_ref[...].astype(BF16)

    @pl.when(b < nu_ref[0])
    def _():
        xb = _unpack_pairs(_load_rows(xs_ref, MOE_BLK))
        g = _dot(xb, wgb_ref[...])
        u = _dot(xb, wub_ref[...])
        hmid = (g * _sigmoid(g)) * u
        y = _dot(hmid.astype(BF16), wdb_ref[...])
        _store_rows(ys_ref, _pack_pairs(y.astype(BF16).astype(F32)))

    @pl.when(b >= nu_ref[0])
    def _():
        ys_ref[...] = jnp.zeros_like(ys_ref)


def _expert_call(blk_e, n_used, xs, wg, wu, wd, l):
    wspec = lambda k, n: pl.BlockSpec((None, None, k, n), lambda b, be, nu: (l, be[b], 0, 0))
    return pl.pallas_call(
        _expert_kernel,
        out_shape=jax.ShapeDtypeStruct((N_SLOTS * PK, LANES), jnp.uint32),
        grid_spec=pltpu.PrefetchScalarGridSpec(
            num_scalar_prefetch=2,
            grid=(N_MOE_BLOCKS,),
            in_specs=[pl.BlockSpec((MOE_BLK * PK, LANES), lambda b, be, nu: (jnp.minimum(b, nu[0] - 1), 0)),
                      wspec(D, EXPERT_HIDDEN), wspec(D, EXPERT_HIDDEN), wspec(EXPERT_HIDDEN, D)],
            out_specs=pl.BlockSpec((MOE_BLK * PK, LANES), lambda b, be, nu: (b, 0)),
            scratch_shapes=[pltpu.VMEM((D, EXPERT_HIDDEN), BF16), pltpu.VMEM((D, EXPERT_HIDDEN), BF16),
                            pltpu.VMEM((EXPERT_HIDDEN, D), BF16)]),
        compiler_params=_cparams(("arbitrary",)),
        name="moe_experts",
    )(blk_e, n_used, xs, wg, wu, wd)


def _combine_kernel(runs_ref, next_runs_ref, ys_ref, pos_ref, x_ref, rt_ref, mod_ref, o_ref, buf0, buf1, sem0, sem1,
                    *, n_tiles):
    i = pl.program_id(0)
    last = n_tiles - 1
    start = lambda d: d.start()
    fetch = lambda buf, sem: (
        lambda off, dst, rows: pltpu.make_async_copy(_lin(ys_ref, dst, rows), _lin(buf, off, rows), sem))

    @pl.when(i == 0)
    def _():
        buf0[...] = jnp.zeros_like(buf0)
        buf1[...] = jnp.zeros_like(buf1)
        _run_copies(runs_ref, fetch(buf0, sem0), start)

    pos = pos_ref[...]
    rt = rt_ref[...]
    col = lax.broadcasted_iota(jnp.int32, (TM, TS), 1).astype(F32)
    pick = (jnp.where(col == pos[:, 0:1], rt[:, TOP_K:TOP_K + 1], 0.0)
            + jnp.where(col == pos[:, 1:2], rt[:, TOP_K + 1:TOP_K + 2], 0.0)).astype(BF16)

    for par, (buf, sem, obuf, osem) in enumerate(((buf0, sem0, buf1, sem1), (buf1, sem1, buf0, sem0))):
        @pl.when(i % 2 == par)
        def _(buf=buf, sem=sem, obuf=obuf, osem=osem):
            @pl.when(i < last)
            def _():
                _run_copies(next_runs_ref, fetch(obuf, osem), start)

            _rows_wait(runs_ref, fetch(buf, sem))
            ysb = _unpack_pairs(_load_rows(buf, TS))
            o_ref[...] = x_ref[...] + mod_ref[5:6, :] * _dot(pick, ysb)


def _combine_call(runs_flat, ys, pos, x, route, mods_l, n_tiles):
    tok = lambda w: pl.BlockSpec((TM, w), lambda i: (i, 0))
    runs_spec = lambda shift: pl.BlockSpec((None, 1, RUN_FIELDS * N_EXPERTS),
                                           lambda i: (jnp.minimum(i + shift, n_tiles - 1), 0, 0),
                                           memory_space=pltpu.SMEM)
    sorted_buf = pltpu.VMEM((TS * PK, LANES), jnp.uint32)
    return pl.pallas_call(
        functools.partial(_combine_kernel, n_tiles=n_tiles),
        out_shape=jax.ShapeDtypeStruct((n_tiles * TM, D), F32),
        grid=(n_tiles,),
        in_specs=[runs_spec(0), runs_spec(1),
                  pl.BlockSpec(memory_space=pl.ANY),
                  tok(LANES), tok(D), tok(LANES),
                  pl.BlockSpec((None, 6, D), lambda i: (_mod_row(i), 0, 0))],
        out_specs=tok(D),
        scratch_shapes=[sorted_buf, sorted_buf, pltpu.SemaphoreType.DMA, pltpu.SemaphoreType.DMA],
        compiler_params=_cparams(("arbitrary",)),
        name="moe_combine",
    )(runs_flat, runs_flat, ys, pos, x, route, mods_l)


def _rope_tables():
    nf = HEAD_DIM // 4
    inv = ROPE_BASE ** (-jnp.arange(nf, dtype=F32) / nf)
    t = jnp.arange(S)
    row = (t // GRID_W).astype(F32)[:, None] * inv[None, :]
    col = (t % GRID_W).astype(F32)[:, None] * inv[None, :]
    zero = jnp.zeros_like(row)
    cos = jnp.concatenate([jnp.cos(row), jnp.cos(row), jnp.cos(col), jnp.cos(col)], axis=1)
    sa = jnp.concatenate([-jnp.sin(row), zero, -jnp.sin(col), zero], axis=1)
    sb = jnp.concatenate([zero, jnp.sin(row), zero, jnp.sin(col)], axis=1)
    ident = (jnp.ones((TM, HEAD_DIM), F32), jnp.zeros((TM, HEAD_DIM), F32), jnp.zeros((TM, HEAD_DIM), F32))
    return tuple(jnp.tile(jnp.concatenate([a, b], axis=0), (1, LANES // HEAD_DIM))
                 for a, b in zip((cos, sa, sb), ident))


def _fourier_tables():
    s1 = np.arange(FS1)
    ang1 = 2.0 * np.pi * np.outer(s1, s1) / FS1
    w1 = np.concatenate([np.cos(ang1), -np.sin(ang1)], axis=0) / np.sqrt(S)
    k1 = np.arange(FS1)[:, None, None]
    k2 = np.arange(FS2)[None, :, None]
    s2 = np.arange(FS2)[None, None, :]
    ang2 = 2.0 * np.pi * ((k1 + FS1 * k2) * s2 % S) / S
    c2, sn2 = np.cos(ang2), np.sin(ang2)
    ta = np.concatenate([c2, -sn2], axis=1)
    tb = np.concatenate([sn2, c2], axis=1)
    sc = np.arange(C)
    angc = 2.0 * np.pi * np.outer(sc, sc) / C
    wc = np.concatenate([np.cos(angc), -np.sin(angc)], axis=0) / np.sqrt(C)
    return tuple(jnp.asarray(a, F32).astype(BF16) for a in (w1, ta, tb, wc))


def _channel_dft():
    cidx = np.arange(FOURIER_GROUP_W)
    ang = 2.0 * np.pi * np.outer(cidx, cidx) / FOURIER_GROUP_W
    eye = np.eye(FOURIER_W // FOURIER_GROUP_W)
    cw = np.kron(eye, np.cos(ang)) / np.sqrt(FOURIER_GROUP_W)
    sw = np.kron(eye, np.sin(ang)) / np.sqrt(FOURIER_GROUP_W)
    return jnp.asarray(np.concatenate([cw, sw], axis=0), F32)


def _pool_bands():
    t = np.arange(TP)[:, None]
    main, halo = [], []
    for w in POOL_WINDOWS:
        def hit(j):
            return ((j - t >= -(w // 2)) & (j - t <= w // 2 - 1)).astype(np.float32)
        main.append(hit(np.arange(TP)[None, :]))
        halo.append(np.concatenate([hit(np.arange(-HALO, 0)[None, :]),
                                    hit(np.arange(TP, TP + HALO)[None, :])], axis=1))
    return (jnp.asarray(np.stack(main), F32).astype(BF16), jnp.asarray(np.stack(halo), F32).astype(BF16))


def _pool_inv_counts():
    win = np.repeat(np.array(POOL_WINDOWS), POOL_GROUP_W)[None, :]

    def table(pos0, seq_len):
        pos = (pos0 + np.arange(TP))[:, None]
        lo = np.clip(pos - win // 2, 0, seq_len)
        hi = np.clip(pos - win // 2 + win, 0, seq_len)
        return 1.0 / (hi - lo)

    tabs = [table(TP, S), table(0, S), table(S - TP, S), table(0, C)]
    return jnp.asarray(np.stack(tabs), F32)


def _conv_shifts():
    i = np.arange(CONV_WIN)
    return jnp.asarray(np.stack([(i[None, :] == i[:, None] + s) for s in range(1, 8)]), F32).astype(BF16)


def _fold_kernel(a_ref, b_ref, o_ref):
    a, b = a_ref[...], b_ref[...]
    a_hi, b_hi = a.astype(BF16), b.astype(BF16)
    a_lo = (a - a_hi.astype(F32)).astype(BF16)
    b_lo = (b - b_hi.astype(F32)).astype(BF16)
    o_ref[...] = (_dot(a_hi, b_hi) + _dot(a_lo, b_hi) + _dot(a_hi, b_lo)).astype(BF16)


def _fold_fourier_weights(dftw, w_br_fourier):
    nl = w_br_fourier.shape[0]
    return pl.pallas_call(
        _fold_kernel,
        out_shape=jax.ShapeDtypeStruct((nl, 2 * FOURIER_W, D), BF16),
        grid=(nl,),
        in_specs=[pl.BlockSpec((2 * FOURIER_W, FOURIER_W), lambda l: (0, 0)),
                  pl.BlockSpec((None, FOURIER_W, D), lambda l: (l, 0, 0))],
        out_specs=pl.BlockSpec((None, 2 * FOURIER_W, D), lambda l: (l, 0, 0)),
        compiler_params=_cparams(("arbitrary",)),
        name="fold_fourier_proj",
    )(dftw, w_br_fourier)


def _block_diag(blocks):
    n, r, c = blocks.shape
    eye = jnp.eye(n, dtype=blocks.dtype)
    return (blocks[:, :, None, :] * eye[:, None, :, None]).reshape(n * r, n * c)


def kernel(x, c, ctx, c_ctx, w_ada, b_ada, g_norm_mix, g_norm_ffn, w_in, g_q, g_k, sink, w_br_attn,
           w_br_fourier, pool_w, pool_scale, w_br_pool, conv_w, conv_b, cn_g, cn_b, w_br_conv, w_gate,
           b_gate, w_out, w_router_grp, b_router_grp, w_router_exp, b_router_exp, w_e_gate, w_e_up,
           w_e_down):
    xs = jnp.concatenate([x.reshape(N_LAT, D), ctx.reshape(N_CTX, D)], axis=0)
    nl = w_ada.shape[0]
    mods = _ada_all(c, c_ctx, w_ada, b_ada).reshape(nl, 8, 6, D)
    rope_tabs = _rope_tables()
    four_tabs = _fourier_tables()
    band_main, band_halo = _pool_bands()
    inv_cnt = _pool_inv_counts()
    shifts = _conv_shifts()
    wf_all = _fold_fourier_weights(_channel_dft(), w_br_fourier)
    bd = jnp.asarray(np.kron(np.eye(LANES // HEAD_DIM), np.ones((HEAD_DIM, HEAD_DIM))), F32).astype(BF16)
    tri = jnp.asarray(np.tril(np.ones((TM, TM)), -1), F32).astype(BF16)
    rpad = jnp.zeros((nl, D, LANES - N_GROUPS - N_EXPERTS), F32)
    w_router = jnp.concatenate([w_router_grp, w_router_exp, rpad], axis=-1)
    r_hi = w_router.astype(BF16)
    r_lo = (w_router - r_hi.astype(F32)).astype(BF16)
    r_b = jnp.concatenate([b_router_grp, b_router_exp, rpad[:, 0, :]], axis=-1).reshape(nl, 1, LANES)
    stacked = tuple(w.astype(BF16) for w in (w_br_attn,)) + (wf_all,) + tuple(
        w.astype(BF16) for w in (w_br_pool, w_br_conv, w_gate, w_out)) + (r_hi, r_lo)

    for l in range(nl):
        mods_l = mods[l]
        gn = g_norm_mix[l].reshape(1, D)
        q, kv, f, p, u = _proj_call(xs, mods_l, gn, w_in, l, rope_tabs,
                                    jnp.tile(g_q[l], 2).reshape(1, LANES),
                                    jnp.tile(g_k[l], 2).reshape(1, LANES), bd)
        a_re, a_im = _fourier_stage1_call(f, four_tabs)
        a_lat, a_ctx = _attn_call(sink[l], q, kv)
        h_lat, h_ctx = _fourier_stage2_call(a_re, a_im, f, four_tabs, a_lat)
        zc, cact = _poolconv_call(p, u, band_main, band_halo, inv_cnt, _block_diag(pool_w[l]).astype(BF16),
                                  pool_scale[l].reshape(1, POOL_W), shifts, conv_w[l], conv_b[l].reshape(1, CONV_W),
                                  cn_g[l].reshape(1, CONV_W), cn_b[l].reshape(1, CONV_W))
        small = (b_gate[l].reshape(1, 4 * D), g_norm_ffn[l].reshape(1, D), r_b[l])
        xs, h2, route = _mix_call(xs, mods_l, gn, a_lat, a_ctx, h_lat, h_ctx, zc, cact, l, stacked, small)
        pos, runs_flat, blk_e, n_used = _plan_call(route, tri)
        slots = _dispatch_call(runs_flat, pos, h2)
        ys = _expert_call(blk_e, n_used, slots, w_e_gate, w_e_up, w_e_down, l)
        xs = _combine_call(runs_flat, ys, pos, xs, route, mods_l, LAT_TILES if l == nl - 1 else NT)
    return xs.reshape(B, S, D)
```

```python
import functools

import numpy as np
import jax
import jax.numpy as jnp
from jax import lax
from jax.experimental import pallas as pl
from jax.experimental.pallas import tpu as pltpu

F32 = jnp.float32
BF16 = jnp.bfloat16

D = 1024
B = 2
S = 8192
C = 256
GRID_W = 64
HEAD_DIM = 64
N_Q_HEADS = 8
N_KV_HEADS = 2
GQA = N_Q_HEADS // N_KV_HEADS
WINDOW = 128
ATTN_BLK = 128
ATTN_QB = 8
ROPE_BASE = 10000.0
Q_W = 512
KV_W = 128
FOURIER_W = 640
FOURIER_GROUP_W = 160
POOL_W = 640
POOL_GROUP_W = 160
POOL_WINDOWS = (2, 4, 8, 16)
CONV_W = 512
CONV_K = 31
PROJ_W = 3072
N_GROUPS = 4
EPG = 8
N_EXPERTS = 32
TOP_K = 2
EXPERT_HIDDEN = 512
MOE_BLK = 512
EPS = 1e-6
NEG_INF = -1e30
LOG2E = 1.4426950408889634

N_LAT = B * S
N_CTX = B * C
N_TOK = N_LAT + N_CTX
TM = 512
NT = N_TOK // TM
LAT_TILES = N_LAT // TM
TILES_PER_BATCH = S // TM
TP = 256
NTP = N_TOK // TP
HALO = 16
CONV_WIN = TP // 2 + 2 * HALO
N_ASSIGN = N_TOK * TOP_K
RANK_TILES = 3
RUN_FIELDS = 5
PK = D // 2 // 128
TS = 1152
N_MOE_BLOCKS = (N_ASSIGN + NT * N_EXPERTS + N_EXPERTS * (MOE_BLK - 1)) // MOE_BLK
N_SLOTS = N_MOE_BLOCKS * MOE_BLK
SUBLANES = 8
BLK_TABLE_ROWS = -(-N_MOE_BLOCKS // SUBLANES) * SUBLANES
ROPE_PAIR = HEAD_DIM // 4
FS1 = 64
FS2 = 128
F1_ROWS = 32
F2_K1 = 16
LANES = 128
VMEM_LIMIT = 56 * 1024 * 1024


def _cparams(sem, vmem=VMEM_LIMIT):
    return pltpu.CompilerParams(dimension_semantics=sem, vmem_limit_bytes=vmem)


def _const_spec(shape):
    nd = len(shape)
    return pl.BlockSpec(shape, lambda *_: (0,) * nd, pipeline_mode=pl.Buffered(1))


def _dot(a, b):
    return jnp.dot(a, b, preferred_element_type=F32)


def _modulate(x, g, shift, scale):
    y = x * lax.rsqrt(jnp.mean(x * x, axis=-1, keepdims=True) + EPS)
    return (y * g) * (1.0 + scale) + shift


def _sigmoid(x):
    return 1.0 / (1.0 + jnp.exp(-x))


def _ada_kernel(ct_ref, w_ref, b_ref, o_ref):
    ct = ct_ref[...]
    s = ct * _sigmoid(ct)
    w = w_ref[...]
    rows = [jnp.sum(w * s[:, r:r + 1], axis=0, keepdims=True) for r in range(3)]
    rows.append(jnp.zeros((5, w.shape[1]), F32))
    o_ref[...] = jnp.concatenate(rows, axis=0) + b_ref[...]


def _ada_all(c, c_ctx, w_ada, b_ada):
    ct = jnp.concatenate([c, c_ctx[None, :], jnp.zeros((5, D), F32)], axis=0).T
    cols = 1536
    nl = w_ada.shape[0]
    return pl.pallas_call(
        _ada_kernel,
        out_shape=jax.ShapeDtypeStruct((nl, 8, 6 * D), F32),
        grid=(nl, 6 * D // cols),
        in_specs=[pl.BlockSpec((D, 8), lambda l, j: (0, 0)),
                  pl.BlockSpec((None, D, cols), lambda l, j: (l, 0, j)),
                  pl.BlockSpec((None, 1, cols), lambda l, j: (l, 0, j))],
        out_specs=pl.BlockSpec((None, 8, cols), lambda l, j: (l, 0, j)),
        compiler_params=_cparams(("arbitrary", "arbitrary")),
        name="adaln",
    )(ct, w_ada, b_ada.reshape(nl, 1, 6 * D))


def _head_rms(t, g128, bd):
    outs = []
    for j in range(t.shape[1] // LANES):
        blk = t[:, j * LANES:(j + 1) * LANES]
        ss = _dot((blk * blk).astype(BF16), bd)
        outs.append(blk * lax.rsqrt(ss * (1.0 / HEAD_DIM) + EPS) * g128)
    return outs


def _rope(blocks, cos, sa, sb):
    outs = []
    for blk in blocks:
        up = pltpu.roll(blk, LANES - ROPE_PAIR, 1)
        dn = pltpu.roll(blk, ROPE_PAIR, 1)
        outs.append(blk * cos + up * sa + dn * sb)
    return outs


def _proj_kernel(x_ref, mod_ref, gn_ref, w_ref, cos_ref, sa_ref, sb_ref, gq_ref, gk_ref, bd_ref,
                 q_ref, kv_ref, f_ref, p_ref, u_ref, wbf_ref):
    @pl.when(pl.program_id(0) == 0)
    def _():
        wbf_ref[...] = w_ref[...].astype(BF16)

    m = mod_ref[...]
    hb = _modulate(x_ref[...], gn_ref[...], m[0:1], m[1:2]).astype(BF16)
    cos, sa, sb, bd = cos_ref[...], sa_ref[...], sb_ref[...], bd_ref[...]
    o_kv, o_f, o_a = Q_W, Q_W + 2 * KV_W, Q_W + 2 * KV_W + FOURIER_W + POOL_W
    qkv = _dot(hb, wbf_ref[:, 0:o_f])
    fp = _dot(hb, wbf_ref[:, o_f:o_a])
    q = _rope(_head_rms(qkv[:, 0:Q_W], gq_ref[...], bd), cos, sa, sb)
    q_ref[...] = (jnp.concatenate(q, axis=1) * (LOG2E * HEAD_DIM ** -0.5)).astype(BF16)
    k = _rope(_head_rms(qkv[:, o_kv:o_kv + KV_W], gk_ref[...], bd), cos, sa, sb)
    kv_ref[:, 0:KV_W] = k[0].astype(BF16)
    kv_ref[:, KV_W:2 * KV_W] = qkv[:, o_kv + KV_W:o_f].astype(BF16)
    ag = _dot(hb, wbf_ref[:, o_a:PROJ_W])
    f_ref[...] = fp[:, 0:FOURIER_W].astype(BF16)
    p_ref[...] = fp[:, FOURIER_W:].astype(BF16)
    u_ref[...] = (ag[:, 0:CONV_W] * _sigmoid(ag[:, CONV_W:])).astype(BF16)


def _mod_row(i):
    return jnp.minimum(i // TILES_PER_BATCH, 2)


def _layer_spec(shape, l):
    nd = len(shape)
    return pl.BlockSpec((None,) + tuple(shape), lambda *_: (l,) + (0,) * nd, pipeline_mode=pl.Buffered(1))


def _proj_call(x, mods_l, gn, w_in, l, rope_tabs, gq128, gk128, bd):
    cos, sa, sb = rope_tabs
    tok = lambda w: pl.BlockSpec((TM, w), lambda i: (i, 0))
    rope_spec = pl.BlockSpec((TM, LANES), lambda i: (jnp.where(i < LAT_TILES, i % TILES_PER_BATCH,
                                                               TILES_PER_BATCH), 0))
    widths = (Q_W, 2 * KV_W, FOURIER_W, POOL_W, CONV_W)
    return pl.pallas_call(
        _proj_kernel,
        out_shape=[jax.ShapeDtypeStruct((N_TOK, w), BF16) for w in widths],
        grid=(NT,),
        in_specs=[tok(D),
                  pl.BlockSpec((None, 6, D), lambda i: (_mod_row(i), 0, 0)),
                  _const_spec((1, D)),
                  _layer_spec((D, PROJ_W), l),
                  rope_spec, rope_spec, rope_spec,
                  _const_spec((1, LANES)), _const_spec((1, LANES)),
                  _const_spec((LANES, LANES))],
        out_specs=[tok(w) for w in widths],
        scratch_shapes=[pltpu.VMEM((D, PROJ_W), BF16)],
        compiler_params=_cparams(("arbitrary",)),
        name="proj",
    )(x, mods_l, gn, w_in, cos, sa, sb, gq128, gk128, bd)


def _attend_many(jobs, sink_ref):
    lane = lax.broadcasted_iota(jnp.int32, (ATTN_BLK, LANES), 1)
    chains = []
    for q, kv_blocks, biases in jobs:
        for j in range(N_KV_HEADS):
            ks = slice(j * HEAD_DIM, (j + 1) * HEAD_DIM)
            vs = slice(KV_W + j * HEAD_DIM, KV_W + (j + 1) * HEAD_DIM)
            kj = jnp.concatenate([blk[:, ks] for blk in kv_blocks], axis=0)
            vj = jnp.concatenate([blk[:, vs] for blk in kv_blocks], axis=0)
            vaug = jnp.concatenate([vj, jnp.ones_like(vj)], axis=1)
            qs = jnp.concatenate([q[:, (j * GQA + g) * HEAD_DIM:(j * GQA + g + 1) * HEAD_DIM]
                                  for g in range(GQA)], axis=0)
            s = lax.dot_general(qs, kj, (((1,), (1,)), ((), ())), preferred_element_type=F32)
            chains.append((j, s, vaug, kv_blocks, biases))
    soft = []
    for j, s, vaug, kv_blocks, biases in chains:
        probs, sink_terms = [], []
        for g in range(GQA):
            sg = s[g * ATTN_BLK:(g + 1) * ATTN_BLK]
            pieces, col = [], 0
            for blk, bias in zip(kv_blocks, biases):
                piece = sg[:, col:col + blk.shape[0]]
                pieces.append(piece if bias is None else piece + bias)
                col += blk.shape[0]
            sg = jnp.concatenate(pieces, axis=1)
            sk = sink_ref[j * GQA + g] * LOG2E
            mx = jnp.maximum(jnp.max(sg, axis=-1, keepdims=True), sk)
            probs.append(jnp.exp2(sg - mx).astype(BF16))
            sink_terms.append(jnp.exp2(sk - mx))
        soft.append((jnp.concatenate(probs, axis=0), vaug, sink_terms))
    heads = []
    for p, vaug, sink_terms in soft:
        o = _dot(p, vaug)
        for g in range(GQA):
            og = o[g * ATTN_BLK:(g + 1) * ATTN_BLK]
            heads.append(og / (og[:, HEAD_DIM:HEAD_DIM + 1] + sink_terms[g]))
    outs = []
    for n in range(len(jobs)):
        hs = heads[n * N_Q_HEADS:(n + 1) * N_Q_HEADS]
        tiles = [jnp.where(lane < HEAD_DIM, hs[2 * t], pltpu.roll(hs[2 * t + 1], HEAD_DIM, 1))
                 for t in range(N_Q_HEADS // 2)]
        outs.append(jnp.concatenate(tiles, axis=1).astype(BF16))
    return outs


def _attn_latent_kernel(sink_ref, q_ref, prev_ref, cur_ref, next_ref, ctx_ref, o_ref):
    n = pl.program_id(1)
    r = lax.broadcasted_iota(jnp.int32, (ATTN_BLK, ATTN_BLK), 0)
    jj = lax.broadcasted_iota(jnp.int32, (ATTN_BLK, ATTN_BLK), 1)
    far = jnp.int32(2 * ATTN_BLK)
    off_prev = jnp.where(n > 0, 0, far)
    off_next = jnp.where(n < S // (ATTN_QB * ATTN_BLK) - 1, 0, far)
    prev_ok = jnp.where(jj - r >= 0, 0.0, NEG_INF)
    next_ok = jnp.where(r - jj >= 0, 0.0, NEG_INF)
    prev_edge = jnp.where(jj - r - off_prev >= 0, 0.0, NEG_INF)
    next_edge = jnp.where(r - jj - off_next >= 0, 0.0, NEG_INF)
    ctx = ctx_ref[...]
    rows = lambda b: slice(b * ATTN_BLK, (b + 1) * ATTN_BLK)
    blocks = [prev_ref[...]] + [cur_ref[rows(b), :] for b in range(ATTN_QB)] + [next_ref[...]]
    jobs = [(q_ref[rows(b), :], [ctx] + blocks[b:b + 3],
             [None, prev_edge if b == 0 else prev_ok, None, next_edge if b == ATTN_QB - 1 else next_ok])
            for b in range(ATTN_QB)]
    for b, out in enumerate(_attend_many(jobs, sink_ref)):
        o_ref[rows(b), :] = out


def _attn_context_kernel(sink_ref, q_ref, ctx_ref, o_ref):
    o_ref[...] = _attend_many([(q_ref[...], [ctx_ref[...]], [None])], sink_ref)[0]


def _attn_call(sink_l, q, kv):
    nb = S // ATTN_BLK
    nq = nb // ATTN_QB
    smem = pl.BlockSpec(memory_space=pltpu.SMEM)
    pair = lambda w: pl.BlockSpec((ATTN_QB * ATTN_BLK, w), lambda b, n: (b * nq + n, 0))
    prev = pl.BlockSpec((ATTN_BLK, 2 * KV_W), lambda b, n: (b * nb + jnp.maximum(ATTN_QB * n - 1, 0), 0))
    nxt = pl.BlockSpec((ATTN_BLK, 2 * KV_W),
                       lambda b, n: (b * nb + jnp.minimum(ATTN_QB * (n + 1), nb - 1), 0))
    ctxs = pl.BlockSpec((C, 2 * KV_W), lambda b, n: (N_LAT // C + b, 0))
    lat = pl.pallas_call(
        _attn_latent_kernel,
        out_shape=jax.ShapeDtypeStruct((N_LAT, Q_W), BF16),
        grid=(B, nq),
        in_specs=[smem, pair(Q_W), prev, pair(2 * KV_W), nxt, ctxs],
        out_specs=pair(Q_W),
        compiler_params=_cparams(("parallel", "parallel")),
        name="attn_latent",
    )(sink_l, q, kv, kv, kv, kv)
    ncb = C // ATTN_BLK
    base = N_LAT // ATTN_BLK
    ctx = pl.pallas_call(
        _attn_context_kernel,
        out_shape=jax.ShapeDtypeStruct((N_CTX, Q_W), BF16),
        grid=(B, ncb),
        in_specs=[smem, pl.BlockSpec((ATTN_BLK, Q_W), lambda b, n: (base + b * ncb + n, 0)), ctxs],
        out_specs=pl.BlockSpec((ATTN_BLK, Q_W), lambda b, n: (b * ncb + n, 0)),
        compiler_params=_cparams(("parallel", "parallel")),
        name="attn_context",
    )(sink_l, q, kv)
    return lat, ctx


def _f1_kernel(w_ref, f_ref, re_ref, im_ref):
    res = lax.dot_general(w_ref[...], f_ref[...], (((1,), (0,)), ((), ())), preferred_element_type=F32)
    re_ref[...] = res[:FS1].astype(BF16)
    im_ref[...] = res[FS1:].astype(BF16)


def _f2_kernel(ta_ref, tb_ref, re_ref, im_ref, after_ref, o_ref):
    del after_ref
    for i in range(F2_K1):
        res = _dot(ta_ref[i], re_ref[i]) + _dot(tb_ref[i], im_ref[i])
        o_ref[i, :, 0:FOURIER_W] = res[:FS2].astype(BF16)
        o_ref[i, :, FOURIER_W:2 * FOURIER_W] = res[FS2:].astype(BF16)


def _fc_kernel(w_ref, f_ref, o_ref):
    res = _dot(w_ref[...], f_ref[...])
    o_ref[:, 0:FOURIER_W] = res[:C].astype(BF16)
    o_ref[:, FOURIER_W:2 * FOURIER_W] = res[C:].astype(BF16)


def _fourier_stage1_call(f, tabs):
    w1 = tabs[0]
    f3 = f.reshape(N_TOK // FS2, FS2, FOURIER_W)
    blk = pl.BlockSpec((FS1, F1_ROWS, FOURIER_W), lambda b, j: (b, j, 0))
    return pl.pallas_call(
        _f1_kernel,
        out_shape=[jax.ShapeDtypeStruct((B * FS1, FS2, FOURIER_W), BF16)] * 2,
        grid=(B, FS2 // F1_ROWS),
        in_specs=[_const_spec((2 * FS1, FS1)), blk],
        out_specs=[blk, blk],
        compiler_params=_cparams(("parallel", "parallel")),
        name="fourier_stage1",
    )(w1, f3)


def _fourier_stage2_call(a_re, a_im, f, tabs, after):
    _, ta, tb, wc = tabs
    nk = FS1 // F2_K1
    aspec = pl.BlockSpec((F2_K1, FS2, FOURIER_W), lambda b, k1: (b * nk + k1, 0, 0))
    tspec = pl.BlockSpec((F2_K1, 2 * FS2, FS2), lambda b, k1: (k1, 0, 0))
    h_t = pl.pallas_call(
        _f2_kernel,
        out_shape=jax.ShapeDtypeStruct((B, FS1, FS2, 2 * FOURIER_W), BF16),
        grid=(B, nk),
        in_specs=[tspec, tspec, aspec, aspec, pl.BlockSpec(memory_space=pl.ANY)],
        out_specs=pl.BlockSpec((None, F2_K1, FS2, 2 * FOURIER_W), lambda b, k1: (b, k1, 0, 0)),
        compiler_params=_cparams(("parallel", "parallel")),
        name="fourier_stage2",
    )(ta, tb, a_re, a_im, after)
    h_lat = jnp.transpose(h_t, (0, 2, 1, 3)).reshape(N_LAT, 2 * FOURIER_W)
    h_ctx = pl.pallas_call(
        _fc_kernel,
        out_shape=jax.ShapeDtypeStruct((N_CTX, 2 * FOURIER_W), BF16),
        grid=(B,),
        in_specs=[_const_spec((2 * C, C)),
                  pl.BlockSpec((C, FOURIER_W), lambda b: (N_LAT // C + b, 0))],
        out_specs=pl.BlockSpec((C, 2 * FOURIER_W), lambda b: (b, 0)),
        compiler_params=_cparams(("parallel",)),
        name="fourier_context",
    )(wc, f)
    return h_lat, h_ctx


def _poolconv_kernel(pc_ref, pp_ref, pn_ref, uc_ref, up_ref, un_ref, bm_ref, bh_ref, ic_ref, pw_ref, ps_ref,
                     sh_ref, cw_ref, cb_ref, cg_ref, cnb_ref, z_ref, a_ref, win0_ref, win1_ref, cv_ref):
    t = pl.program_id(0)
    lat_tiles = N_LAT // TP
    per_seq = S // TP
    is_ctx = t >= lat_tiles
    first = jnp.logical_or(t % per_seq == 0, is_ctx)
    last = jnp.logical_or(t % per_seq == per_seq - 1, is_ctx)

    keep_prev = jnp.where(first, 0.0, 1.0)
    keep_next = jnp.where(last, 0.0, 1.0)

    ub = jnp.concatenate([(up_ref[...].astype(F32) * keep_prev).astype(BF16), uc_ref[...],
                          (un_ref[...].astype(F32) * keep_next).astype(BF16)], axis=0)
    off = HALO - CONV_K // 2
    half_rows = TP // 2
    wins = (win0_ref, win1_ref)
    for hf, win_ref in enumerate(wins):
        window = ub[hf * half_rows:hf * half_rows + CONV_WIN]
        win_ref[0] = window.astype(F32)
        for s in range(1, 8):
            win_ref[s] = _dot(sh_ref[s - 1], window)

    pcur = pc_ref[...]
    halo = jnp.concatenate([pp_ref[...].astype(F32) * keep_prev,
                            pn_ref[...].astype(F32) * keep_next], axis=0).astype(BF16)
    sums = []
    for gi in range(len(POOL_WINDOWS)):
        cs = slice(gi * LANES, (gi + 2) * LANES)
        sums.append(_dot(bm_ref[gi], pcur[:, cs]) + _dot(bh_ref[gi], halo[:, cs]))
    lane_t = lax.broadcasted_iota(jnp.int32, (TP, LANES), 1)
    tiles = [sums[0][:, :LANES]]
    for gi in range(1, len(POOL_WINDOWS)):
        split = gi * POOL_GROUP_W - gi * LANES
        tiles.append(jnp.where(lane_t < split, sums[gi - 1][:, LANES:], sums[gi][:, :LANES]))
    tiles.append(sums[-1][:, LANES:])
    zsum = jnp.concatenate(tiles, axis=1)
    z = zsum * ic_ref[...] - pcur.astype(F32)
    z_ref[...] = (_dot(z.astype(BF16), pw_ref[...]) * ps_ref[...]).astype(BF16)

    for hf, win_ref in enumerate(wins):
        base = hf * half_rows
        for cb in range(CONV_W // LANES):
            cs = slice(cb * LANES, (cb + 1) * LANES)
            acc = jnp.zeros((half_rows, LANES), F32) + cb_ref[:, cs]
            for j in range(CONV_K):
                s, m = (off + j) % 8, (off + j) // 8
                acc = acc + win_ref[s, 8 * m:8 * m + half_rows, cs] * cw_ref[j:j + 1, cs]
            cv_ref[base:base + half_rows, cs] = acc
    cv = cv_ref[...]
    mu = jnp.mean(cv, axis=-1, keepdims=True)
    var = jnp.mean(jnp.square(cv - mu), axis=-1, keepdims=True)
    un = (cv - mu) * lax.rsqrt(var + EPS) * cg_ref[...] + cnb_ref[...]
    a_ref[...] = (un * _sigmoid(un)).astype(BF16)


def _poolconv_call(p, u, band_main, band_halo, inv_cnt, pw_bd, pool_scale, shifts, conv_w, conv_b, cn_g, cn_b):
    nh = TP // HALO
    last_h = N_TOK // HALO - 1
    cur = lambda w: pl.BlockSpec((TP, w), lambda t: (t, 0))
    prv = lambda w: pl.BlockSpec((HALO, w), lambda t: (jnp.maximum(t * nh - 1, 0), 0))
    nxt = lambda w: pl.BlockSpec((HALO, w), lambda t: (jnp.minimum((t + 1) * nh, last_h), 0))
    per_seq = S // TP

    def kind(t):
        return jnp.where(t >= N_LAT // TP, 3, jnp.where(t % per_seq == 0, 1, jnp.where(t % per_seq == per_seq - 1, 2, 0)))

    return pl.pallas_call(
        _poolconv_kernel,
        out_shape=[jax.ShapeDtypeStruct((N_TOK, POOL_W), BF16),
                   jax.ShapeDtypeStruct((N_TOK, CONV_W), BF16)],
        grid=(NTP,),
        in_specs=[cur(POOL_W), prv(POOL_W), nxt(POOL_W), cur(CONV_W), prv(CONV_W), nxt(CONV_W),
                  _const_spec((4, TP, TP)), _const_spec((4, TP, 2 * HALO)),
                  pl.BlockSpec((None, TP, POOL_W), lambda t: (kind(t), 0, 0)),
                  _const_spec((POOL_W, POOL_W)), _const_spec((1, POOL_W)),
                  _const_spec((7, CONV_WIN, CONV_WIN)),
                  _const_spec((CONV_K, CONV_W)), _const_spec((1, CONV_W)),
                  _const_spec((1, CONV_W)), _const_spec((1, CONV_W))],
        out_specs=[cur(POOL_W), cur(CONV_W)],
        scratch_shapes=[pltpu.VMEM((8, CONV_WIN, CONV_W), F32), pltpu.VMEM((8, CONV_WIN, CONV_W), F32),
                        pltpu.VMEM((TP, CONV_W), F32)],
        compiler_params=_cparams(("parallel",)),
        name="pool_conv",
    )(p, p, p, u, u, u, band_main, band_halo, inv_cnt, pw_bd, pool_scale, shifts, conv_w, conv_b, cn_g, cn_b)


def _route(logits):
    lane = lax.broadcasted_iota(jnp.int32, logits.shape, 1)
    big = jnp.int32(LANES)
    lg = jnp.where(lane < N_GROUPS, logits, NEG_INF)
    mg = jnp.max(lg, axis=-1, keepdims=True)
    grp = jnp.min(jnp.where(lg == mg, lane, big), axis=-1, keepdims=True)
    p_grp = 1.0 / jnp.sum(jnp.exp(lg - mg), axis=-1, keepdims=True)
    lo = N_GROUPS + grp * EPG
    le = jnp.where((lane >= lo) & (lane < lo + EPG), logits, NEG_INF)
    m1 = jnp.max(le, axis=-1, keepdims=True)
    i1 = jnp.min(jnp.where(le == m1, lane, big), axis=-1, keepdims=True)
    le2 = jnp.where(lane == i1, NEG_INF, le)
    m2 = jnp.max(le2, axis=-1, keepdims=True)
    i2 = jnp.min(jnp.where(le2 == m2, lane, big), axis=-1, keepdims=True)
    r = jnp.exp(m2 - m1)
    w1 = p_grp / (1.0 + r)
    w2 = p_grp * r / (1.0 + r)
    e1 = (i1 - N_GROUPS).astype(F32)
    e2 = (i2 - N_GROUPS).astype(F32)
    return jnp.where(lane == 0, e1, jnp.where(lane == 1, e2, jnp.where(lane == 2, w1,
                     jnp.where(lane == 3, w2, 0.0))))


def _mix_kernel(x_ref, mod_ref, gn_ref, al_ref, ac_ref, hl_ref, hc_ref, z_ref, cv_ref,
                wa_ref, wf_ref, wp_ref, wc_ref, wg_ref, bg_ref, wo_ref, gf_ref, rh_ref, rl_ref, rb_ref,
                xo_ref, h2_ref, rt_ref):
    is_ctx = pl.program_id(0) >= LAT_TILES
    m = mod_ref[...]
    x = x_ref[...]
    hb = _modulate(x, gn_ref[...], m[0:1], m[1:2]).astype(BF16)
    attn = jnp.where(is_ctx, ac_ref[...], al_ref[...])
    four = jnp.where(is_ctx, hc_ref[...], hl_ref[...])
    branches = ((attn, wa_ref), (four, wf_ref), (z_ref[...], wp_ref), (cv_ref[...], wc_ref))
    acc = None
    for bi, (inp, w_ref) in enumerate(branches):
        cs = slice(bi * D, (bi + 1) * D)
        gate = _sigmoid(_dot(hb, wg_ref[:, cs]) + bg_ref[:, cs])
        term = gate * _dot(inp, w_ref[...])
        acc = term if acc is None else acc + term
    x_new = x + m[2:3] * _dot(acc.astype(BF16), wo_ref[...])
    xo_ref[...] = x_new
    h2 = _modulate(x_new, gf_ref[...], m[3:4], m[4:5])
    hi = h2.astype(BF16)
    h2_ref[...] = hi
    lo = (h2 - hi.astype(F32)).astype(BF16)
    logits = _dot(hi, rh_ref[...]) + _dot(lo, rh_ref[...]) + _dot(hi, rl_ref[...]) + rb_ref[...]
    rt_ref[...] = _route(logits)


def _mix_call(x, mods_l, gn, a_lat, a_ctx, h_lat, h_ctx, zc, cact, l, stacked, small):
    tok = lambda w: pl.BlockSpec((TM, w), lambda i: (i, 0))
    lat = lambda w: pl.BlockSpec((TM, w), lambda i: (jnp.minimum(i, LAT_TILES - 1), 0))
    wa, wf, wp, wc, wg, wo, rh, rl = stacked
    bg, gf, rb = small
    in_specs = [tok(D), pl.BlockSpec((None, 6, D), lambda i: (_mod_row(i), 0, 0)), _const_spec((1, D)),
                lat(Q_W), _const_spec((N_CTX, Q_W)),
                lat(2 * FOURIER_W), _const_spec((N_CTX, 2 * FOURIER_W)),
                tok(POOL_W), tok(CONV_W)]
    in_specs += [_layer_spec(w.shape[1:], l) for w in (wa, wf, wp, wc, wg)]
    in_specs += [_const_spec(bg.shape), _layer_spec(wo.shape[1:], l), _const_spec(gf.shape),
                 _layer_spec(rh.shape[1:], l), _layer_spec(rl.shape[1:], l), _const_spec(rb.shape)]
    return pl.pallas_call(
        _mix_kernel,
        out_shape=[jax.ShapeDtypeStruct((N_TOK, D), F32), jax.ShapeDtypeStruct((N_TOK, D), BF16),
                   jax.ShapeDtypeStruct((N_TOK, LANES), F32)],
        grid=(NT,),
        in_specs=in_specs,
        out_specs=[tok(D), tok(D), tok(LANES)],
        compiler_params=_cparams(("parallel",)),
        name="mix",
    )(x, mods_l, gn, a_lat, a_ctx, h_lat, h_ctx, zc, cact, wa, wf, wp, wc, wg, bg, wo, gf, rh, rl, rb)


def _onehots(route):
    lane = lax.broadcasted_iota(jnp.int32, route.shape, 1)
    e1 = route[:, 0:1].astype(jnp.int32)
    e2 = route[:, 1:2].astype(jnp.int32)
    return (lane == e1).astype(F32), (lane == e2).astype(F32)


def _lane_cumsum(row):
    lane = lax.broadcasted_iota(jnp.int32, row.shape, 1)
    sh = 1
    while sh < N_EXPERTS:
        row = row + jnp.where(lane >= sh, pltpu.roll(row, sh, 1), 0.0)
        sh *= 2
    return row


def _rank_kernel(rt_ref, tri_ref, pos_ref, meta_ref, cnt_ref, carry_ref):
    @pl.when(pl.program_id(0) == 0)
    def _():
        carry_ref[...] = jnp.zeros_like(carry_ref)

    tiles = []
    for t in range(RANK_TILES):
        oh1, oh2 = _onehots(rt_ref[t * TM:(t + 1) * TM, :])
        both = oh1 + oh2
        tiles.append((oh1, oh2, both, _dot(tri_ref[...], both.astype(BF16))))
    carry = carry_ref[0:1, :]
    lane = lax.broadcasted_iota(jnp.int32, (TM, LANES), 1)
    row = lax.broadcasted_iota(jnp.int32, meta_ref.shape[1:], 0)
    for t, (oh1, oh2, both, before) in enumerate(tiles):
        tile_cnt = jnp.sum(both, axis=0, keepdims=True)
        tile_cnt = tile_cnt + (tile_cnt - 2.0 * jnp.floor(tile_cnt * 0.5))
        tile_off = _lane_cumsum(tile_cnt) - tile_cnt
        where = before + tile_off
        p1 = jnp.sum(oh1 * where, axis=-1, keepdims=True)
        p2 = jnp.sum(oh2 * where, axis=-1, keepdims=True)
        pos_ref[t * TM:(t + 1) * TM, :] = jnp.where(lane == 0, p1, jnp.where(lane == 1, p2, 0.0))
        meta_ref[t] = jnp.where(row == 0, tile_off, jnp.where(row == 1, tile_cnt, jnp.where(row == 2, carry, 0.0)))
        carry = carry + tile_cnt
    carry_ref[...] = jnp.broadcast_to(carry, carry_ref.shape)
    cnt_ref[...] = jnp.broadcast_to(carry, cnt_ref.shape)


def _runs_kernel(meta_ref, cnt_ref, runs_ref, pieces_ref, be_ref):
    lane = lax.broadcasted_iota(jnp.int32, (1, LANES), 1)
    counts = cnt_ref[0:1, :]
    padded = jnp.floor((counts + (MOE_BLK - 1)) * (1.0 / MOE_BLK)) * MOE_BLK
    padded = jnp.where(lane < N_EXPERTS, padded, 0.0)
    ends = _lane_cumsum(padded)
    starts = ends - padded
    sub = lax.broadcasted_iota(jnp.int32, (N_EXPERTS, LANES), 0)
    lan = lax.broadcasted_iota(jnp.int32, (N_EXPERTS, LANES), 1)
    prow = lax.broadcasted_iota(jnp.int32, pieces_ref.shape[1:], 0)

    def column(r):
        return jnp.sum(jnp.where(sub == lan, jnp.broadcast_to(r, sub.shape), 0.0), axis=1, keepdims=True)

    for t in range(NT):
        m = meta_ref[t]
        row = lax.broadcasted_iota(jnp.int32, m.shape, 0)
        m = jnp.where(row == 2, m + starts, m)
        m = jnp.where(row == 3, starts + counts, jnp.where(row == 4, padded - counts, m))
        runs_ref[t] = m.astype(jnp.int32)
        n_r = m[1:2].astype(jnp.int32)
        off_c, n_cf, dst_c = column(m[0:1]), column(m[1:2]), column(m[2:3])
        n_c = n_cf.astype(jnp.int32)
        table = jnp.zeros(pieces_ref.shape[1:], F32)
        tally = jnp.zeros((1, LANES), F32)
        for k, size in enumerate(RUN_PIECES):
            flag_r = jnp.where((n_r & size) != 0, 1.0, 0.0)
            before = jnp.sum(jnp.where(lan < sub, jnp.broadcast_to(flag_r, sub.shape), 0.0), axis=1, keepdims=True)
            done = (n_c & (-2 * size)).astype(F32)
            hit = ((n_c & size) != 0) & (before == lan.astype(F32))
            src = jnp.sum(jnp.where(hit, off_c + done, 0.0), axis=0, keepdims=True)
            dst = jnp.sum(jnp.where(hit, dst_c + done, 0.0), axis=0, keepdims=True)
            table = jnp.where(prow == 2 * k, src, jnp.where(prow == 2 * k + 1, dst, table))
            tally = jnp.where(lane == k, jnp.sum(flag_r, axis=1, keepdims=True), tally)
        whole = jnp.sum(jnp.floor(m[1:2] * (1.0 / FETCH_ROWS)), axis=1, keepdims=True)
        tally = jnp.where(lane == len(RUN_PIECES), whole, tally)
        pieces_ref[t] = jnp.where(prow == PIECE_TALLY_ROW, tally, table).astype(jnp.int32)
    blk = lax.broadcasted_iota(jnp.int32, be_ref.shape, 0).astype(F32) * MOE_BLK
    lane_b = lax.broadcasted_iota(jnp.int32, be_ref.shape, 1)
    done = jnp.where((ends <= blk) & (lane_b < N_EXPERTS), 1.0, 0.0)
    be = jnp.minimum(jnp.sum(done, axis=-1, keepdims=True), N_EXPERTS - 1.0)
    nblk = jnp.max(jnp.where(lane_b == N_EXPERTS - 1, ends, 0.0), axis=-1, keepdims=True) * (1.0 / MOE_BLK)
    be_ref[...] = jnp.where(lane_b == 0, be, jnp.where(lane_b == 1, nblk, 0.0)).astype(jnp.int32)


def _plan_call(route, tri):
    tok = pl.BlockSpec((RANK_TILES * TM, LANES), lambda i: (i, 0))
    pos, meta, counts = pl.pallas_call(
        _rank_kernel,
        out_shape=[jax.ShapeDtypeStruct((N_TOK, LANES), F32), jax.ShapeDtypeStruct((NT, 8, LANES), F32),
                   jax.ShapeDtypeStruct((8, LANES), F32)],
        grid=(NT // RANK_TILES,),
        in_specs=[tok, _const_spec((TM, TM))],
        out_specs=[tok, pl.BlockSpec((RANK_TILES, 8, LANES), lambda i: (i, 0, 0)),
                   pl.BlockSpec((8, LANES), lambda i: (0, 0))],
        scratch_shapes=[pltpu.VMEM((8, LANES), F32)],
        compiler_params=_cparams(("arbitrary",)),
        name="moe_rank",
    )(route, tri)
    runs, pieces, blk = pl.pallas_call(
        _runs_kernel,
        out_shape=[jax.ShapeDtypeStruct((NT, 8, LANES), jnp.int32),
                   jax.ShapeDtypeStruct((NT, 16, LANES), jnp.int32),
                   jax.ShapeDtypeStruct((BLK_TABLE_ROWS, LANES), jnp.int32)],
        name="moe_runs",
    )(meta, counts)
    runs_flat = runs[:, 0:RUN_FIELDS, 0:N_EXPERTS].reshape(NT, 1, RUN_FIELDS * N_EXPERTS)
    pieces_flat = pieces[:, 0:PIECE_ROWS, 0:N_EXPERTS].reshape(NT, 1, PIECE_ROWS * N_EXPERTS)
    return pos, runs_flat, pieces_flat, blk[:N_MOE_BLOCKS, 0], blk[0:1, 1]


def _pack_pairs(x):
    half = x.shape[1] // 2
    lo = pltpu.bitcast(x[:, :half], jnp.uint32)
    hi = pltpu.bitcast(x[:, half:], jnp.uint32)
    return (lo >> 16) | (hi & jnp.uint32(0xFFFF0000))


def _unpack_pairs(w):
    lo = pltpu.bitcast(w << 16, F32)
    hi = pltpu.bitcast(w & jnp.uint32(0xFFFF0000), F32)
    return jnp.concatenate([lo, hi], axis=1).astype(BF16)


def _run_fields(runs_ref, e):
    return runs_ref[0, e], runs_ref[0, N_EXPERTS + e], runs_ref[0, 2 * N_EXPERTS + e]


def _store_rows(lin_ref, packed):
    rows = packed.shape[0]
    for c in range(PK):
        lin_ref[pl.ds(c, rows, stride=PK), :] = packed[:, c * LANES:(c + 1) * LANES]


def _load_rows(lin_ref, rows):
    return jnp.concatenate([lin_ref[pl.ds(c, rows, stride=PK), :] for c in range(PK)], axis=1)


def _lin(ref, row, nrows):
    return ref.at[pl.ds(pl.multiple_of(row * PK, SUBLANES), nrows * PK), :]


FETCH_ROWS = 64
RUN_PIECES = (32, 16, 8, 4, 2)
TAIL_PIECES = (256, 128, 64, 32, 16, 8, 4, 2)


PIECE_TALLY_ROW = 2 * len(RUN_PIECES)
PIECE_ROWS = PIECE_TALLY_ROW + 1


def _run_copies(runs, pieces, make_copy, act):
    for k, size in enumerate(RUN_PIECES):
        def piece(j, c, k=k, size=size):
            act(make_copy(pieces[0, 2 * k * N_EXPERTS + j], pieces[0, (2 * k + 1) * N_EXPERTS + j], size))
            return c

        lax.fori_loop(0, pieces[0, PIECE_TALLY_ROW * N_EXPERTS + k], piece, 0)

    @pl.when(pieces[0, PIECE_TALLY_ROW * N_EXPERTS + len(RUN_PIECES)] > 0)
    def _():
        def per_expert(e, carry):
            off, n, dst = _run_fields(runs, e)

            def chunk(k, c):
                act(make_copy(off + k * FETCH_ROWS, dst + k * FETCH_ROWS, FETCH_ROWS))
                return c

            lax.fori_loop(0, n // FETCH_ROWS, chunk, 0)
            return carry

        lax.fori_loop(0, N_EXPERTS, per_expert, 0)


def _rows_wait(runs, make_copy):
    total = lax.fori_loop(0, N_EXPERTS, lambda e, acc: acc + runs[0, N_EXPERTS + e], jnp.int32(0))
    lax.fori_loop(0, total // FETCH_ROWS, lambda k, c: (make_copy(0, 0, FETCH_ROWS).wait(), c)[1], 0)
    for size in RUN_PIECES:
        @pl.when((total & size) != 0)
        def _(size=size):
            make_copy(0, 0, size).wait()


def _dispatch_kernel(runs_ref, prev_runs_ref, pieces_ref, pos_ref, h2_ref, xs_ref, buf0, buf1, zero_ref, sem0, sem1,
                     zsem, ssem):
    i = pl.program_id(0)
    last = NT - 1

    def tail_copies(act):
        def per_expert(e, carry):
            row = runs_ref[0, 3 * N_EXPERTS + e]
            n = runs_ref[0, 4 * N_EXPERTS + e]
            done = jnp.int32(0)
            for size in TAIL_PIECES:
                @pl.when((n & size) != 0)
                def _(done=done, size=size):
                    act(pltpu.make_async_copy(zero_ref.at[pl.ds(0, size * PK), :], _lin(xs_ref, row + done, size), zsem))
                done = done + (n & size)
            return carry

        lax.fori_loop(0, N_EXPERTS, per_expert, 0)

    def spare_copies(act):
        used = runs_ref[0, 3 * N_EXPERTS + N_EXPERTS - 1] + runs_ref[0, 4 * N_EXPERTS + N_EXPERTS - 1]

        def spare_block(k, carry):
            act(pltpu.make_async_copy(zero_ref.at[pl.ds(0, MOE_BLK * PK), :],
                                      _lin(xs_ref, used + k * MOE_BLK, MOE_BLK), ssem))
            return carry

        lax.fori_loop(0, N_MOE_BLOCKS - used // MOE_BLK, spare_block, 0)

    start = lambda d: d.start()
    wait = lambda d: d.wait()

    @pl.when(i == 0)
    def _():
        zero_ref[...] = jnp.zeros_like(zero_ref)
        tail_copies(start)
        spare_copies(start)

    pos = pos_ref[...]
    col = lax.broadcasted_iota(jnp.int32, (TM, TS), 1).astype(F32)
    sel = jnp.where((col == pos[:, 0:1]) | (col == pos[:, 1:2]), 1.0, 0.0).astype(BF16)
    srt = lax.dot_general(sel, h2_ref[...], (((0,), (0,)), ((), ())), preferred_element_type=F32)
    packed = _pack_pairs(srt)

    for par, (buf, sem, obuf, osem) in enumerate(((buf0, sem0, buf1, sem1), (buf1, sem1, buf0, sem0))):
        @pl.when(i % 2 == par)
        def _(buf=buf, sem=sem, obuf=obuf, osem=osem):
            copy = lambda off, dst, rows: pltpu.make_async_copy(_lin(buf, off, rows), _lin(xs_ref, dst, rows), sem)
            ocopy = lambda off, dst, rows: pltpu.make_async_copy(_lin(obuf, off, rows), _lin(xs_ref, dst, rows), osem)
            _store_rows(buf, packed)

            @pl.when(i == 0)
            def _():
                tail_copies(wait)

            _run_copies(runs_ref, pieces_ref, copy, start)

            @pl.when(i > 0)
            def _():
                _rows_wait(prev_runs_ref, ocopy)

            @pl.when(i == last)
            def _():
                _rows_wait(runs_ref, copy)
                spare_copies(wait)


def _dispatch_call(runs_flat, pieces_flat, pos, h2):
    runs_spec = lambda shift: pl.BlockSpec((None, 1, RUN_FIELDS * N_EXPERTS), lambda i: (jnp.maximum(i - shift, 0), 0, 0),
                                           memory_space=pltpu.SMEM)
    sorted_buf = pltpu.VMEM((TS * PK, LANES), jnp.uint32)
    return pl.pallas_call(
        _dispatch_kernel,
        out_shape=jax.ShapeDtypeStruct((N_SLOTS * PK, LANES), jnp.uint32),
        grid=(NT,),
        in_specs=[runs_spec(0), runs_spec(1),
                  pl.BlockSpec((None, 1, PIECE_ROWS * N_EXPERTS), lambda i: (i, 0, 0), memory_space=pltpu.SMEM),
                  pl.BlockSpec((TM, LANES), lambda i: (i, 0)),
                  pl.BlockSpec((TM, D), lambda i: (i, 0))],
        out_specs=pl.BlockSpec(memory_space=pl.ANY),
        scratch_shapes=[sorted_buf, sorted_buf, pltpu.VMEM((MOE_BLK * PK, LANES), jnp.uint32),
                        pltpu.SemaphoreType.DMA, pltpu.SemaphoreType.DMA, pltpu.SemaphoreType.DMA,
                        pltpu.SemaphoreType.DMA],
        compiler_params=_cparams(("arbitrary",)),
        name="moe_dispatch",
    )(runs_flat, runs_flat, pieces_flat, pos, h2)


def _expert_kernel(be_ref, nu_ref, xs_ref, wg_ref, wu_ref, wd_ref, ys_ref, wgb_ref, wub_ref, wdb_ref):
    b = pl.program_id(0)

    @pl.when(jnp.logical_or(b == 0, be_ref[b] != be_ref[jnp.maximum(b - 1, 0)]))
    def _():
        wgb_ref[...] = wg_ref[...].astype(BF16)
        wub_ref[...] = wu_ref[...].astype(BF16)
        wdb_ref[...] = wd_ref[...].astype(BF16)

    @pl.when(b < nu_ref[0])
    def _():
        xb = _unpack_pairs(_load_rows(xs_ref, MOE_BLK))
        g = _dot(xb, wgb_ref[...])
        u = _dot(xb, wub_ref[...])
        hmid = (g * _sigmoid(g)) * u
        y = _dot(hmid.astype(BF16), wdb_ref[...])
        _store_rows(ys_ref, _pack_pairs(y.astype(BF16).astype(F32)))

    @pl.when(b >= nu_ref[0])
    def _():
        ys_ref[...] = jnp.zeros_like(ys_ref)


def _expert_call(blk_e, n_used, xs, wg, wu, wd, l):
    wspec = lambda k, n: pl.BlockSpec((None, None, k, n), lambda b, be, nu: (l, be[b], 0, 0))
    return pl.pallas_call(
        _expert_kernel,
        out_shape=jax.ShapeDtypeStruct((N_SLOTS * PK, LANES), jnp.uint32),
        grid_spec=pltpu.PrefetchScalarGridSpec(
            num_scalar_prefetch=2,
            grid=(N_MOE_BLOCKS,),
            in_specs=[pl.BlockSpec((MOE_BLK * PK, LANES), lambda b, be, nu: (jnp.minimum(b, nu[0] - 1), 0)),
                      wspec(D, EXPERT_HIDDEN), wspec(D, EXPERT_HIDDEN), wspec(EXPERT_HIDDEN, D)],
            out_specs=pl.BlockSpec((MOE_BLK * PK, LANES), lambda b, be, nu: (b, 0)),
            scratch_shapes=[pltpu.VMEM((D, EXPERT_HIDDEN), BF16), pltpu.VMEM((D, EXPERT_HIDDEN), BF16),
                            pltpu.VMEM((EXPERT_HIDDEN, D), BF16)]),
        compiler_params=_cparams(("arbitrary",)),
        name="moe_experts",
    )(blk_e, n_used, xs, wg, wu, wd)


def _combine_kernel(runs_ref, next_runs_ref, pieces_ref, next_pieces_ref, ys_ref, pos_ref, x_ref, rt_ref, mod_ref, o_ref,
                    buf0, buf1, sem0, sem1, *, n_tiles):
    i = pl.program_id(0)
    last = n_tiles - 1
    start = lambda d: d.start()
    fetch = lambda buf, sem: (
        lambda off, dst, rows: pltpu.make_async_copy(_lin(ys_ref, dst, rows), _lin(buf, off, rows), sem))

    @pl.when(i == 0)
    def _():
        buf0[...] = jnp.zeros_like(buf0)
        buf1[...] = jnp.zeros_like(buf1)
        _run_copies(runs_ref, pieces_ref, fetch(buf0, sem0), start)

    pos = pos_ref[...]
    rt = rt_ref[...]
    col = lax.broadcasted_iota(jnp.int32, (TM, TS), 1).astype(F32)
    pick = (jnp.where(col == pos[:, 0:1], rt[:, TOP_K:TOP_K + 1], 0.0)
            + jnp.where(col == pos[:, 1:2], rt[:, TOP_K + 1:TOP_K + 2], 0.0)).astype(BF16)

    for par, (buf, sem, obuf, osem) in enumerate(((buf0, sem0, buf1, sem1), (buf1, sem1, buf0, sem0))):
        @pl.when(i % 2 == par)
        def _(buf=buf, sem=sem, obuf=obuf, osem=osem):
            @pl.when(i < last)
            def _():
                _run_copies(next_runs_ref, next_pieces_ref, fetch(obuf, osem), start)

            _rows_wait(runs_ref, fetch(buf, sem))
            ysb = _unpack_pairs(_load_rows(buf, TS))
            o_ref[...] = x_ref[...] + mod_ref[5:6, :] * _dot(pick, ysb)


def _combine_call(runs_flat, pieces_flat, ys, pos, x, route, mods_l, n_tiles):
    tok = lambda w: pl.BlockSpec((TM, w), lambda i: (i, 0))
    runs_spec = lambda shift: pl.BlockSpec((None, 1, RUN_FIELDS * N_EXPERTS),
                                           lambda i: (jnp.minimum(i + shift, n_tiles - 1), 0, 0),
                                           memory_space=pltpu.SMEM)
    pieces_spec = lambda shift: pl.BlockSpec((None, 1, PIECE_ROWS * N_EXPERTS),
                                             lambda i: (jnp.minimum(i + shift, n_tiles - 1), 0, 0),
                                             memory_space=pltpu.SMEM)
    sorted_buf = pltpu.VMEM((TS * PK, LANES), jnp.uint32)
    return pl.pallas_call(
        functools.partial(_combine_kernel, n_tiles=n_tiles),
        out_shape=jax.ShapeDtypeStruct((n_tiles * TM, D), F32),
        grid=(n_tiles,),
        in_specs=[runs_spec(0), runs_spec(1), pieces_spec(0), pieces_spec(1),
                  pl.BlockSpec(memory_space=pl.ANY),
                  tok(LANES), tok(D), tok(LANES),
                  pl.BlockSpec((None, 6, D), lambda i: (_mod_row(i), 0, 0))],
        out_specs=tok(D),
        scratch_shapes=[sorted_buf, sorted_buf, pltpu.SemaphoreType.DMA, pltpu.SemaphoreType.DMA],
        compiler_params=_cparams(("arbitrary",)),
        name="moe_combine",
    )(runs_flat, runs_flat, pieces_flat, pieces_flat, ys, pos, x, route, mods_l)


def _rope_tables():
    nf = HEAD_DIM // 4
    inv = ROPE_BASE ** (-jnp.arange(nf, dtype=F32) / nf)
    t = jnp.arange(S)
    row = (t // GRID_W).astype(F32)[:, None] * inv[None, :]
    col = (t % GRID_W).astype(F32)[:, None] * inv[None, :]
    zero = jnp.zeros_like(row)
    cos = jnp.concatenate([jnp.cos(row), jnp.cos(row), jnp.cos(col), jnp.cos(col)], axis=1)
    sa = jnp.concatenate([-jnp.sin(row), zero, -jnp.sin(col), zero], axis=1)
    sb = jnp.concatenate([zero, jnp.sin(row), zero, jnp.sin(col)], axis=1)
    ident = (jnp.ones((TM, HEAD_DIM), F32), jnp.zeros((TM, HEAD_DIM), F32), jnp.zeros((TM, HEAD_DIM), F32))
    return tuple(jnp.tile(jnp.concatenate([a, b], axis=0), (1, LANES // HEAD_DIM))
                 for a, b in zip((cos, sa, sb), ident))


def _fourier_tables():
    s1 = np.arange(FS1)
    ang1 = 2.0 * np.pi * np.outer(s1, s1) / FS1
    w1 = np.concatenate([np.cos(ang1), -np.sin(ang1)], axis=0) / np.sqrt(S)
    k1 = np.arange(FS1)[:, None, None]
    k2 = np.arange(FS2)[None, :, None]
    s2 = np.arange(FS2)[None, None, :]
    ang2 = 2.0 * np.pi * ((k1 + FS1 * k2) * s2 % S) / S
    c2, sn2 = np.cos(ang2), np.sin(ang2)
    ta = np.concatenate([c2, -sn2], axis=1)
    tb = np.concatenate([sn2, c2], axis=1)
    sc = np.arange(C)
    angc = 2.0 * np.pi * np.outer(sc, sc) / C
    wc = np.concatenate([np.cos(angc), -np.sin(angc)], axis=0) / np.sqrt(C)
    return tuple(jnp.asarray(a, F32).astype(BF16) for a in (w1, ta, tb, wc))


def _channel_dft():
    cidx = np.arange(FOURIER_GROUP_W)
    ang = 2.0 * np.pi * np.outer(cidx, cidx) / FOURIER_GROUP_W
    eye = np.eye(FOURIER_W // FOURIER_GROUP_W)
    cw = np.kron(eye, np.cos(ang)) / np.sqrt(FOURIER_GROUP_W)
    sw = np.kron(eye, np.sin(ang)) / np.sqrt(FOURIER_GROUP_W)
    return jnp.asarray(np.concatenate([cw, sw], axis=0), F32)


def _pool_bands():
    t = np.arange(TP)[:, None]
    main, halo = [], []
    for w in POOL_WINDOWS:
        def hit(j):
            return ((j - t >= -(w // 2)) & (j - t <= w // 2 - 1)).astype(np.float32)
        main.append(hit(np.arange(TP)[None, :]))
        halo.append(np.concatenate([hit(np.arange(-HALO, 0)[None, :]),
                                    hit(np.arange(TP, TP + HALO)[None, :])], axis=1))
    return (jnp.asarray(np.stack(main), F32).astype(BF16), jnp.asarray(np.stack(halo), F32).astype(BF16))


def _pool_inv_counts():
    win = np.repeat(np.array(POOL_WINDOWS), POOL_GROUP_W)[None, :]

    def table(pos0, seq_len):
        pos = (pos0 + np.arange(TP))[:, None]
        lo = np.clip(pos - win // 2, 0, seq_len)
        hi = np.clip(pos - win // 2 + win, 0, seq_len)
        return 1.0 / (hi - lo)

    tabs = [table(TP, S), table(0, S), table(S - TP, S), table(0, C)]
    return jnp.asarray(np.stack(tabs), F32)


def _conv_shifts():
    i = np.arange(CONV_WIN)
    return jnp.asarray(np.stack([(i[None, :] == i[:, None] + s) for s in range(1, 8)]), F32).astype(BF16)


def _fold_kernel(a_ref, b_ref, o_ref):
    a, b = a_ref[...], b_ref[...]
    a_hi, b_hi = a.astype(BF16), b.astype(BF16)
    a_lo = (a - a_hi.astype(F32)).astype(BF16)
    b_lo = (b - b_hi.astype(F32)).astype(BF16)
    o_ref[...] = (_dot(a_hi, b_hi) + _dot(a_lo, b_hi) + _dot(a_hi, b_lo)).astype(BF16)


def _fold_fourier_weights(dftw, w_br_fourier):
    nl = w_br_fourier.shape[0]
    return pl.pallas_call(
        _fold_kernel,
        out_shape=jax.ShapeDtypeStruct((nl, 2 * FOURIER_W, D), BF16),
        grid=(nl,),
        in_specs=[pl.BlockSpec((2 * FOURIER_W, FOURIER_W), lambda l: (0, 0)),
                  pl.BlockSpec((None, FOURIER_W, D), lambda l: (l, 0, 0))],
        out_specs=pl.BlockSpec((None, 2 * FOURIER_W, D), lambda l: (l, 0, 0)),
        compiler_params=_cparams(("arbitrary",)),
        name="fold_fourier_proj",
    )(dftw, w_br_fourier)


def _block_diag(blocks):
    n, r, c = blocks.shape
    eye = jnp.eye(n, dtype=blocks.dtype)
    return (blocks[:, :, None, :] * eye[:, None, :, None]).reshape(n * r, n * c)


def kernel(x, c, ctx, c_ctx, w_ada, b_ada, g_norm_mix, g_norm_ffn, w_in, g_q, g_k, sink, w_br_attn,
           w_br_fourier, pool_w, pool_scale, w_br_pool, conv_w, conv_b, cn_g, cn_b, w_br_conv, w_gate,
           b_gate, w_out, w_router_grp, b_router_grp, w_router_exp, b_router_exp, w_e_gate, w_e_up,
           w_e_down):
    xs = jnp.concatenate([x.reshape(N_LAT, D), ctx.reshape(N_CTX, D)], axis=0)
    nl = w_ada.shape[0]
    mods = _ada_all(c, c_ctx, w_ada, b_ada).reshape(nl, 8, 6, D)
    rope_tabs = _rope_tables()
    four_tabs = _fourier_tables()
    band_main, band_halo = _pool_bands()
    inv_cnt = _pool_inv_counts()
    shifts = _conv_shifts()
    wf_all = _fold_fourier_weights(_channel_dft(), w_br_fourier)
    bd = jnp.asarray(np.kron(np.eye(LANES // HEAD_DIM), np.ones((HEAD_DIM, HEAD_DIM))), F32).astype(BF16)
    tri = jnp.asarray(np.tril(np.ones((TM, TM)), -1), F32).astype(BF16)
    rpad = jnp.zeros((nl, D, LANES - N_GROUPS - N_EXPERTS), F32)
    w_router = jnp.concatenate([w_router_grp, w_router_exp, rpad], axis=-1)
    r_hi = w_router.astype(BF16)
    r_lo = (w_router - r_hi.astype(F32)).astype(BF16)
    r_b = jnp.concatenate([b_router_grp, b_router_exp, rpad[:, 0, :]], axis=-1).reshape(nl, 1, LANES)
    stacked = tuple(w.astype(BF16) for w in (w_br_attn,)) + (wf_all,) + tuple(
        w.astype(BF16) for w in (w_br_pool, w_br_conv, w_gate, w_out)) + (r_hi, r_lo)

    for l in range(nl):
        mods_l = mods[l]
        gn = g_norm_mix[l].reshape(1, D)
        q, kv, f, p, u = _proj_call(xs, mods_l, gn, w_in, l, rope_tabs,
                                    jnp.tile(g_q[l], 2).reshape(1, LANES),
                                    jnp.tile(g_k[l], 2).reshape(1, LANES), bd)
        a_re, a_im = _fourier_stage1_call(f, four_tabs)
        a_lat, a_ctx = _attn_call(sink[l], q, kv)
        h_lat, h_ctx = _fourier_stage2_call(a_re, a_im, f, four_tabs, a_lat)
        zc, cact = _poolconv_call(p, u, band_main, band_halo, inv_cnt, _block_diag(pool_w[l]).astype(BF16),
                                  pool_scale[l].reshape(1, POOL_W), shifts, conv_w[l], conv_b[l].reshape(1, CONV_W),
                                  cn_g[l].reshape(1, CONV_W), cn_b[l].reshape(1, CONV_W))
        small = (b_gate[l].reshape(1, 4 * D), g_norm_ffn[l].reshape(1, D), r_b[l])
        xs, h2, route = _mix_call(xs, mods_l, gn, a_lat, a_ctx, h_lat, h_ctx, zc, cact, l, stacked, small)
        pos, runs_flat, pieces_flat, blk_e, n_used = _plan_call(route, tri)
        slots = _dispatch_call(runs_flat, pieces_flat, pos, h2)
        ys = _expert_call(blk_e, n_used, slots, w_e_gate, w_e_up, w_e_down, l)
        xs = _combine_call(runs_flat, pieces_flat, ys, pos, xs, route, mods_l, LAT_TILES if l == nl - 1 else NT)
    return xs.reshape(B, S, D)
```

```python
import functools

import numpy as np
import jax
import jax.numpy as jnp
from jax import lax
from jax.experimental import pallas as pl
from jax.experimental.pallas import tpu as pltpu

F32 = jnp.float32
BF16 = jnp.bfloat16

D = 1024
B = 2
S = 8192
C = 256
GRID_W = 64
HEAD_DIM = 64
N_Q_HEADS = 8
N_KV_HEADS = 2
GQA = N_Q_HEADS // N_KV_HEADS
WINDOW = 128
ATTN_BLK = 128
ATTN_QB = 8
ROPE_BASE = 10000.0
Q_W = 512
KV_W = 128
FOURIER_W = 640
FOURIER_GROUP_W = 160
POOL_W = 640
POOL_GROUP_W = 160
POOL_WINDOWS = (2, 4, 8, 16)
CONV_W = 512
CONV_K = 31
PROJ_W = 3072
N_GROUPS = 4
EPG = 8
N_EXPERTS = 32
TOP_K = 2
EXPERT_HIDDEN = 512
MOE_BLK = 512
EPS = 1e-6
NEG_INF = -1e30
LOG2E = 1.4426950408889634

N_LAT = B * S
N_CTX = B * C
N_TOK = N_LAT + N_CTX
TM = 512
NT = N_TOK // TM
LAT_TILES = N_LAT // TM
TILES_PER_BATCH = S // TM
TP = 256
NTP = N_TOK // TP
HALO = 16
CONV_WIN = TP // 2 + 2 * HALO
N_ASSIGN = N_TOK * TOP_K
RANK_TILES = 3
RUN_FIELDS = 5
PK = D // 2 // 128
TS = 1152
N_MOE_BLOCKS = (N_ASSIGN + NT * N_EXPERTS + N_EXPERTS * (MOE_BLK - 1)) // MOE_BLK
N_SLOTS = N_MOE_BLOCKS * MOE_BLK
SUBLANES = 8
BLK_TABLE_ROWS = -(-N_MOE_BLOCKS // SUBLANES) * SUBLANES
ROPE_PAIR = HEAD_DIM // 4
FS1 = 64
FS2 = 128
F1_ROWS = 32
F2_K1 = 16
LANES = 128
VMEM_LIMIT = 56 * 1024 * 1024


def _cparams(sem, vmem=VMEM_LIMIT):
    return pltpu.CompilerParams(dimension_semantics=sem, vmem_limit_bytes=vmem)


def _const_spec(shape):
    nd = len(shape)
    return pl.BlockSpec(shape, lambda *_: (0,) * nd, pipeline_mode=pl.Buffered(1))


def _dot(a, b):
    return jnp.dot(a, b, preferred_element_type=F32)


def _modulate(x, g, shift, scale):
    y = x * lax.rsqrt(jnp.mean(x * x, axis=-1, keepdims=True) + EPS)
    return (y * g) * (1.0 + scale) + shift


def _sigmoid(x):
    return 1.0 / (1.0 + jnp.exp(-x))


def _ada_kernel(ct_ref, w_ref, b_ref, o_ref):
    ct = ct_ref[...]
    s = ct * _sigmoid(ct)
    w = w_ref[...]
    rows = [jnp.sum(w * s[:, r:r + 1], axis=0, keepdims=True) for r in range(3)]
    rows.append(jnp.zeros((5, w.shape[1]), F32))
    o_ref[...] = jnp.concatenate(rows, axis=0) + b_ref[...]


def _ada_all(c, c_ctx, w_ada, b_ada):
    ct = jnp.concatenate([c, c_ctx[None, :], jnp.zeros((5, D), F32)], axis=0).T
    cols = 1536
    nl = w_ada.shape[0]
    return pl.pallas_call(
        _ada_kernel,
        out_shape=jax.ShapeDtypeStruct((nl, 8, 6 * D), F32),
        grid=(nl, 6 * D // cols),
        in_specs=[pl.BlockSpec((D, 8), lambda l, j: (0, 0)),
                  pl.BlockSpec((None, D, cols), lambda l, j: (l, 0, j)),
                  pl.BlockSpec((None, 1, cols), lambda l, j: (l, 0, j))],
        out_specs=pl.BlockSpec((None, 8, cols), lambda l, j: (l, 0, j)),
        compiler_params=_cparams(("arbitrary", "arbitrary")),
        name="adaln",
    )(ct, w_ada, b_ada.reshape(nl, 1, 6 * D))


def _head_rms(t, g128, bd):
    outs = []
    for j in range(t.shape[1] // LANES):
        blk = t[:, j * LANES:(j + 1) * LANES]
        ss = _dot((blk * blk).astype(BF16), bd)
        outs.append(blk * lax.rsqrt(ss * (1.0 / HEAD_DIM) + EPS) * g128)
    return outs


def _rope(blocks, cos, sa, sb):
    outs = []
    for blk in blocks:
        up = pltpu.roll(blk, LANES - ROPE_PAIR, 1)
        dn = pltpu.roll(blk, ROPE_PAIR, 1)
        outs.append(blk * cos + up * sa + dn * sb)
    return outs


def _proj_kernel(xl_ref, xc_ref, mod_ref, gn_ref, w_ref, cos_ref, sa_ref, sb_ref, gq_ref, gk_ref, bd_ref,
                 q_ref, kv_ref, f_ref, p_ref, u_ref, wbf_ref, *, split):
    @pl.when(pl.program_id(0) == 0)
    def _():
        wbf_ref[...] = w_ref[...].astype(BF16)

    m = mod_ref[...]
    hb = _modulate(_stream_tile(xl_ref, xc_ref, split), gn_ref[...], m[0:1], m[1:2]).astype(BF16)
    cos, sa, sb, bd = cos_ref[...], sa_ref[...], sb_ref[...], bd_ref[...]
    o_kv, o_f, o_a = Q_W, Q_W + 2 * KV_W, Q_W + 2 * KV_W + FOURIER_W + POOL_W
    qkv = _dot(hb, wbf_ref[:, 0:o_f])
    fp = _dot(hb, wbf_ref[:, o_f:o_a])
    q = _rope(_head_rms(qkv[:, 0:Q_W], gq_ref[...], bd), cos, sa, sb)
    q_ref[...] = (jnp.concatenate(q, axis=1) * (LOG2E * HEAD_DIM ** -0.5)).astype(BF16)
    k = _rope(_head_rms(qkv[:, o_kv:o_kv + KV_W], gk_ref[...], bd), cos, sa, sb)
    kv_ref[:, 0:KV_W] = k[0].astype(BF16)
    kv_ref[:, KV_W:2 * KV_W] = qkv[:, o_kv + KV_W:o_f].astype(BF16)
    ag = _dot(hb, wbf_ref[:, o_a:PROJ_W])
    f_ref[...] = fp[:, 0:FOURIER_W].astype(BF16)
    p_ref[...] = fp[:, FOURIER_W:].astype(BF16)
    u_ref[...] = (ag[:, 0:CONV_W] * _sigmoid(ag[:, CONV_W:])).astype(BF16)


def _mod_row(i):
    return jnp.minimum(i // TILES_PER_BATCH, 2)


def _layer_spec(shape, l):
    nd = len(shape)
    return pl.BlockSpec((None,) + tuple(shape), lambda *_: (l,) + (0,) * nd, pipeline_mode=pl.Buffered(1))


def _stream_specs(split):
    first = (lambda i: (jnp.minimum(i, LAT_TILES - 1), 0)) if split else (lambda i: (i, 0))
    return [pl.BlockSpec((TM, D), first),
            pl.BlockSpec((TM, D), lambda i: (0, 0), pipeline_mode=pl.Buffered(1))]


def _stream_tile(xl_ref, xc_ref, split):
    return jnp.where(pl.program_id(0) >= LAT_TILES, xc_ref[...], xl_ref[...]) if split else xl_ref[...]


def _proj_call(x_lat, x_ctx, split, mods_l, gn, w_in, l, rope_tabs, gq128, gk128, bd):
    cos, sa, sb = rope_tabs
    tok = lambda w: pl.BlockSpec((TM, w), lambda i: (i, 0))
    rope_spec = pl.BlockSpec((TM, LANES), lambda i: (jnp.where(i < LAT_TILES, i % TILES_PER_BATCH,
                                                               TILES_PER_BATCH), 0))
    widths = (Q_W, 2 * KV_W, FOURIER_W, POOL_W, CONV_W)
    return pl.pallas_call(
        functools.partial(_proj_kernel, split=split),
        out_shape=[jax.ShapeDtypeStruct((N_TOK, w), BF16) for w in widths],
        grid=(NT,),
        in_specs=_stream_specs(split) + [
                  pl.BlockSpec((None, 6, D), lambda i: (_mod_row(i), 0, 0)),
                  _const_spec((1, D)),
                  _layer_spec((D, PROJ_W), l),
                  rope_spec, rope_spec, rope_spec,
                  _const_spec((1, LANES)), _const_spec((1, LANES)),
                  _const_spec((LANES, LANES))],
        out_specs=[tok(w) for w in widths],
        scratch_shapes=[pltpu.VMEM((D, PROJ_W), BF16)],
        compiler_params=_cparams(("arbitrary",)),
        name="proj",
    )(x_lat, x_ctx, mods_l, gn, w_in, cos, sa, sb, gq128, gk128, bd)


def _attend_many(jobs, sink_ref):
    lane = lax.broadcasted_iota(jnp.int32, (ATTN_BLK, LANES), 1)
    chains = []
    for q, kv_blocks, biases in jobs:
        for j in range(N_KV_HEADS):
            ks = slice(j * HEAD_DIM, (j + 1) * HEAD_DIM)
            vs = slice(KV_W + j * HEAD_DIM, KV_W + (j + 1) * HEAD_DIM)
            kj = jnp.concatenate([blk[:, ks] for blk in kv_blocks], axis=0)
            vj = jnp.concatenate([blk[:, vs] for blk in kv_blocks], axis=0)
            vaug = jnp.concatenate([vj, jnp.ones_like(vj)], axis=1)
            qs = jnp.concatenate([q[:, (j * GQA + g) * HEAD_DIM:(j * GQA + g + 1) * HEAD_DIM]
                                  for g in range(GQA)], axis=0)
            s = lax.dot_general(qs, kj, (((1,), (1,)), ((), ())), preferred_element_type=F32)
            chains.append((j, s, vaug, kv_blocks, biases))
    soft = []
    for j, s, vaug, kv_blocks, biases in chains:
        probs, sink_terms = [], []
        for g in range(GQA):
            sg = s[g * ATTN_BLK:(g + 1) * ATTN_BLK]
            pieces, col = [], 0
            for blk, bias in zip(kv_blocks, biases):
                piece = sg[:, col:col + blk.shape[0]]
                pieces.append(piece if bias is None else piece + bias)
                col += blk.shape[0]
            sg = jnp.concatenate(pieces, axis=1)
            sk = sink_ref[j * GQA + g] * LOG2E
            mx = jnp.maximum(jnp.max(sg, axis=-1, keepdims=True), sk)
            probs.append(jnp.exp2(sg - mx).astype(BF16))
            sink_terms.append(jnp.exp2(sk - mx))
        soft.append((jnp.concatenate(probs, axis=0), vaug, sink_terms))
    heads = []
    for p, vaug, sink_terms in soft:
        o = _dot(p, vaug)
        for g in range(GQA):
            og = o[g * ATTN_BLK:(g + 1) * ATTN_BLK]
            heads.append(og / (og[:, HEAD_DIM:HEAD_DIM + 1] + sink_terms[g]))
    outs = []
    for n in range(len(jobs)):
        hs = heads[n * N_Q_HEADS:(n + 1) * N_Q_HEADS]
        tiles = [jnp.where(lane < HEAD_DIM, hs[2 * t], pltpu.roll(hs[2 * t + 1], HEAD_DIM, 1))
                 for t in range(N_Q_HEADS // 2)]
        outs.append(jnp.concatenate(tiles, axis=1).astype(BF16))
    return outs


def _attn_latent_kernel(sink_ref, q_ref, prev_ref, cur_ref, next_ref, ctx_ref, o_ref):
    n = pl.program_id(1)
    r = lax.broadcasted_iota(jnp.int32, (ATTN_BLK, ATTN_BLK), 0)
    jj = lax.broadcasted_iota(jnp.int32, (ATTN_BLK, ATTN_BLK), 1)
    far = jnp.int32(2 * ATTN_BLK)
    off_prev = jnp.where(n > 0, 0, far)
    off_next = jnp.where(n < S // (ATTN_QB * ATTN_BLK) - 1, 0, far)
    prev_ok = jnp.where(jj - r >= 0, 0.0, NEG_INF)
    next_ok = jnp.where(r - jj >= 0, 0.0, NEG_INF)
    prev_edge = jnp.where(jj - r - off_prev >= 0, 0.0, NEG_INF)
    next_edge = jnp.where(r - jj - off_next >= 0, 0.0, NEG_INF)
    ctx = ctx_ref[...]
    rows = lambda b: slice(b * ATTN_BLK, (b + 1) * ATTN_BLK)
    blocks = [prev_ref[...]] + [cur_ref[rows(b), :] for b in range(ATTN_QB)] + [next_ref[...]]
    jobs = [(q_ref[rows(b), :], [ctx] + blocks[b:b + 3],
             [None, prev_edge if b == 0 else prev_ok, None, next_edge if b == ATTN_QB - 1 else next_ok])
            for b in range(ATTN_QB)]
    for b, out in enumerate(_attend_many(jobs, sink_ref)):
        o_ref[rows(b), :] = out


def _attn_context_kernel(sink_ref, q_ref, ctx_ref, o_ref):
    o_ref[...] = _attend_many([(q_ref[...], [ctx_ref[...]], [None])], sink_ref)[0]


def _attn_call(sink_l, q, kv):
    nb = S // ATTN_BLK
    nq = nb // ATTN_QB
    smem = pl.BlockSpec(memory_space=pltpu.SMEM)
    pair = lambda w: pl.BlockSpec((ATTN_QB * ATTN_BLK, w), lambda b, n: (b * nq + n, 0))
    prev = pl.BlockSpec((ATTN_BLK, 2 * KV_W), lambda b, n: (b * nb + jnp.maximum(ATTN_QB * n - 1, 0), 0))
    nxt = pl.BlockSpec((ATTN_BLK, 2 * KV_W),
                       lambda b, n: (b * nb + jnp.minimum(ATTN_QB * (n + 1), nb - 1), 0))
    ctxs = pl.BlockSpec((C, 2 * KV_W), lambda b, n: (N_LAT // C + b, 0))
    lat = pl.pallas_call(
        _attn_latent_kernel,
        out_shape=jax.ShapeDtypeStruct((N_LAT, Q_W), BF16),
        grid=(B, nq),
        in_specs=[smem, pair(Q_W), prev, pair(2 * KV_W), nxt, ctxs],
        out_specs=pair(Q_W),
        compiler_params=_cparams(("parallel", "parallel")),
        name="attn_latent",
    )(sink_l, q, kv, kv, kv, kv)
    ncb = C // ATTN_BLK
    base = N_LAT // ATTN_BLK
    ctx = pl.pallas_call(
        _attn_context_kernel,
        out_shape=jax.ShapeDtypeStruct((N_CTX, Q_W), BF16),
        grid=(B, ncb),
        in_specs=[smem, pl.BlockSpec((ATTN_BLK, Q_W), lambda b, n: (base + b * ncb + n, 0)), ctxs],
        out_specs=pl.BlockSpec((ATTN_BLK, Q_W), lambda b, n: (b * ncb + n, 0)),
        compiler_params=_cparams(("parallel", "parallel")),
        name="attn_context",
    )(sink_l, q, kv)
    return lat, ctx


def _f1_kernel(w_ref, f_ref, re_ref, im_ref):
    res = lax.dot_general(w_ref[...], f_ref[...], (((1,), (0,)), ((), ())), preferred_element_type=F32)
    re_ref[...] = res[:FS1].astype(BF16)
    im_ref[...] = res[FS1:].astype(BF16)


def _f2_kernel(ta_ref, tb_ref, re_ref, im_ref, after_ref, o_ref):
    del after_ref
    for i in range(F2_K1):
        res = _dot(ta_ref[i], re_ref[i]) + _dot(tb_ref[i], im_ref[i])
        o_ref[i, :, 0:FOURIER_W] = res[:FS2].astype(BF16)
        o_ref[i, :, FOURIER_W:2 * FOURIER_W] = res[FS2:].astype(BF16)


def _fc_kernel(w_ref, f_ref, o_ref):
    res = _dot(w_ref[...], f_ref[...])
    o_ref[:, 0:FOURIER_W] = res[:C].astype(BF16)
    o_ref[:, FOURIER_W:2 * FOURIER_W] = res[C:].astype(BF16)


def _fourier_stage1_call(f, tabs):
    w1 = tabs[0]
    f3 = f.reshape(N_TOK // FS2, FS2, FOURIER_W)
    blk = pl.BlockSpec((FS1, F1_ROWS, FOURIER_W), lambda b, j: (b, j, 0))
    return pl.pallas_call(
        _f1_kernel,
        out_shape=[jax.ShapeDtypeStruct((B * FS1, FS2, FOURIER_W), BF16)] * 2,
        grid=(B, FS2 // F1_ROWS),
        in_specs=[_const_spec((2 * FS1, FS1)), blk],
        out_specs=[blk, blk],
        compiler_params=_cparams(("parallel", "parallel")),
        name="fourier_stage1",
    )(w1, f3)


def _fourier_stage2_call(a_re, a_im, f, tabs, after):
    _, ta, tb, wc = tabs
    nk = FS1 // F2_K1
    aspec = pl.BlockSpec((F2_K1, FS2, FOURIER_W), lambda b, k1: (b * nk + k1, 0, 0))
    tspec = pl.BlockSpec((F2_K1, 2 * FS2, FS2), lambda b, k1: (k1, 0, 0))
    h_t = pl.pallas_call(
        _f2_kernel,
        out_shape=jax.ShapeDtypeStruct((B, FS1, FS2, 2 * FOURIER_W), BF16),
        grid=(B, nk),
        in_specs=[tspec, tspec, aspec, aspec, pl.BlockSpec(memory_space=pl.ANY)],
        out_specs=pl.BlockSpec((None, F2_K1, FS2, 2 * FOURIER_W), lambda b, k1: (b, k1, 0, 0)),
        compiler_params=_cparams(("parallel", "parallel")),
        name="fourier_stage2",
    )(ta, tb, a_re, a_im, after)
    h_lat = jnp.transpose(h_t, (0, 2, 1, 3)).reshape(N_LAT, 2 * FOURIER_W)
    h_ctx = pl.pallas_call(
        _fc_kernel,
        out_shape=jax.ShapeDtypeStruct((N_CTX, 2 * FOURIER_W), BF16),
        grid=(B,),
        in_specs=[_const_spec((2 * C, C)),
                  pl.BlockSpec((C, FOURIER_W), lambda b: (N_LAT // C + b, 0))],
        out_specs=pl.BlockSpec((C, 2 * FOURIER_W), lambda b: (b, 0)),
        compiler_params=_cparams(("parallel",)),
        name="fourier_context",
    )(wc, f)
    return h_lat, h_ctx


def _poolconv_kernel(pc_ref, pp_ref, pn_ref, uc_ref, up_ref, un_ref, bm_ref, bh_ref, ic_ref, pw_ref, ps_ref,
                     sh_ref, cw_ref, cb_ref, cg_ref, cnb_ref, z_ref, a_ref, win0_ref, win1_ref, cv_ref):
    t = pl.program_id(0)
    lat_tiles = N_LAT // TP
    per_seq = S // TP
    is_ctx = t >= lat_tiles
    first = jnp.logical_or(t % per_seq == 0, is_ctx)
    last = jnp.logical_or(t % per_seq == per_seq - 1, is_ctx)

    keep_prev = jnp.where(first, 0.0, 1.0)
    keep_next = jnp.where(last, 0.0, 1.0)

    ub = jnp.concatenate([(up_ref[...].astype(F32) * keep_prev).astype(BF16), uc_ref[...],
                          (un_ref[...].astype(F32) * keep_next).astype(BF16)], axis=0)
    off = HALO - CONV_K // 2
    half_rows = TP // 2
    wins = (win0_ref, win1_ref)
    for hf, win_ref in enumerate(wins):
        window = ub[hf * half_rows:hf * half_rows + CONV_WIN]
        win_ref[0] = window.astype(F32)
        for s in range(1, 8):
            win_ref[s] = _dot(sh_ref[s - 1], window)

    pcur = pc_ref[...]
    halo = jnp.concatenate([pp_ref[...].astype(F32) * keep_prev,
                            pn_ref[...].astype(F32) * keep_next], axis=0).astype(BF16)
    sums = []
    for gi in range(len(POOL_WINDOWS)):
        cs = slice(gi * LANES, (gi + 2) * LANES)
        sums.append(_dot(bm_ref[gi], pcur[:, cs]) + _dot(bh_ref[gi], halo[:, cs]))
    lane_t = lax.broadcasted_iota(jnp.int32, (TP, LANES), 1)
    tiles = [sums[0][:, :LANES]]
    for gi in range(1, len(POOL_WINDOWS)):
        split = gi * POOL_GROUP_W - gi * LANES
        tiles.append(jnp.where(lane_t < split, sums[gi - 1][:, LANES:], sums[gi][:, :LANES]))
    tiles.append(sums[-1][:, LANES:])
    zsum = jnp.concatenate(tiles, axis=1)
    z = zsum * ic_ref[...] - pcur.astype(F32)
    z_ref[...] = (_dot(z.astype(BF16), pw_ref[...]) * ps_ref[...]).astype(BF16)

    for hf, win_ref in enumerate(wins):
        base = hf * half_rows
        for cb in range(CONV_W // LANES):
            cs = slice(cb * LANES, (cb + 1) * LANES)
            acc = jnp.zeros((half_rows, LANES), F32) + cb_ref[:, cs]
            for j in range(CONV_K):
                s, m = (off + j) % 8, (off + j) // 8
                acc = acc + win_ref[s, 8 * m:8 * m + half_rows, cs] * cw_ref[j:j + 1, cs]
            cv_ref[base:base + half_rows, cs] = acc
    cv = cv_ref[...]
    mu = jnp.mean(cv, axis=-1, keepdims=True)
    var = jnp.mean(jnp.square(cv - mu), axis=-1, keepdims=True)
    un = (cv - mu) * lax.rsqrt(var + EPS) * cg_ref[...] + cnb_ref[...]
    a_ref[...] = (un * _sigmoid(un)).astype(BF16)


def _poolconv_call(p, u, band_main, band_halo, inv_cnt, pw_bd, pool_scale, shifts, conv_w, conv_b, cn_g, cn_b):
    nh = TP // HALO
    last_h = N_TOK // HALO - 1
    cur = lambda w: pl.BlockSpec((TP, w), lambda t: (t, 0))
    prv = lambda w: pl.BlockSpec((HALO, w), lambda t: (jnp.maximum(t * nh - 1, 0), 0))
    nxt = lambda w: pl.BlockSpec((HALO, w), lambda t: (jnp.minimum((t + 1) * nh, last_h), 0))
    per_seq = S // TP

    def kind(t):
        return jnp.where(t >= N_LAT // TP, 3, jnp.where(t % per_seq == 0, 1, jnp.where(t % per_seq == per_seq - 1, 2, 0)))

    return pl.pallas_call(
        _poolconv_kernel,
        out_shape=[jax.ShapeDtypeStruct((N_TOK, POOL_W), BF16),
                   jax.ShapeDtypeStruct((N_TOK, CONV_W), BF16)],
        grid=(NTP,),
        in_specs=[cur(POOL_W), prv(POOL_W), nxt(POOL_W), cur(CONV_W), prv(CONV_W), nxt(CONV_W),
                  _const_spec((4, TP, TP)), _const_spec((4, TP, 2 * HALO)),
                  pl.BlockSpec((None, TP, POOL_W), lambda t: (kind(t), 0, 0)),
                  _const_spec((POOL_W, POOL_W)), _const_spec((1, POOL_W)),
                  _const_spec((7, CONV_WIN, CONV_WIN)),
                  _const_spec((CONV_K, CONV_W)), _const_spec((1, CONV_W)),
                  _const_spec((1, CONV_W)), _const_spec((1, CONV_W))],
        out_specs=[cur(POOL_W), cur(CONV_W)],
        scratch_shapes=[pltpu.VMEM((8, CONV_WIN, CONV_W), F32), pltpu.VMEM((8, CONV_WIN, CONV_W), F32),
                        pltpu.VMEM((TP, CONV_W), F32)],
        compiler_params=_cparams(("parallel",)),
        name="pool_conv",
    )(p, p, p, u, u, u, band_main, band_halo, inv_cnt, pw_bd, pool_scale, shifts, conv_w, conv_b, cn_g, cn_b)


def _route(logits):
    lane = lax.broadcasted_iota(jnp.int32, logits.shape, 1)
    big = jnp.int32(LANES)
    lg = jnp.where(lane < N_GROUPS, logits, NEG_INF)
    mg = jnp.max(lg, axis=-1, keepdims=True)
    grp = jnp.min(jnp.where(lg == mg, lane, big), axis=-1, keepdims=True)
    p_grp = 1.0 / jnp.sum(jnp.exp(lg - mg), axis=-1, keepdims=True)
    lo = N_GROUPS + grp * EPG
    le = jnp.where((lane >= lo) & (lane < lo + EPG), logits, NEG_INF)
    m1 = jnp.max(le, axis=-1, keepdims=True)
    i1 = jnp.min(jnp.where(le == m1, lane, big), axis=-1, keepdims=True)
    le2 = jnp.where(lane == i1, NEG_INF, le)
    m2 = jnp.max(le2, axis=-1, keepdims=True)
    i2 = jnp.min(jnp.where(le2 == m2, lane, big), axis=-1, keepdims=True)
    r = jnp.exp(m2 - m1)
    w1 = p_grp / (1.0 + r)
    w2 = p_grp * r / (1.0 + r)
    e1 = (i1 - N_GROUPS).astype(F32)
    e2 = (i2 - N_GROUPS).astype(F32)
    return jnp.where(lane == 0, e1, jnp.where(lane == 1, e2, jnp.where(lane == 2, w1,
                     jnp.where(lane == 3, w2, 0.0))))


def _mix_kernel(xl_ref, xc_ref, mod_ref, gn_ref, al_ref, ac_ref, hl_ref, hc_ref, z_ref, cv_ref,
                wa_ref, wf_ref, wp_ref, wc_ref, wg_ref, bg_ref, wo_ref, gf_ref, rh_ref, rl_ref, rb_ref,
                xo_ref, h2_ref, rt_ref, *, split):
    is_ctx = pl.program_id(0) >= LAT_TILES
    m = mod_ref[...]
    x = _stream_tile(xl_ref, xc_ref, split)
    hb = _modulate(x, gn_ref[...], m[0:1], m[1:2]).astype(BF16)
    attn = jnp.where(is_ctx, ac_ref[...], al_ref[...])
    four = jnp.where(is_ctx, hc_ref[...], hl_ref[...])
    branches = ((attn, wa_ref), (four, wf_ref), (z_ref[...], wp_ref), (cv_ref[...], wc_ref))
    acc = None
    for bi, (inp, w_ref) in enumerate(branches):
        cs = slice(bi * D, (bi + 1) * D)
        gate = _sigmoid(_dot(hb, wg_ref[:, cs]) + bg_ref[:, cs])
        term = gate * _dot(inp, w_ref[...])
        acc = term if acc is None else acc + term
    x_new = x + m[2:3] * _dot(acc.astype(BF16), wo_ref[...])
    xo_ref[...] = x_new
    h2 = _modulate(x_new, gf_ref[...], m[3:4], m[4:5])
    hi = h2.astype(BF16)
    h2_ref[...] = hi
    lo = (h2 - hi.astype(F32)).astype(BF16)
    logits = _dot(hi, rh_ref[...]) + _dot(lo, rh_ref[...]) + _dot(hi, rl_ref[...]) + rb_ref[...]
    rt_ref[...] = _route(logits)


def _mix_call(x_lat, x_ctx, split, mods_l, gn, a_lat, a_ctx, h_lat, h_ctx, zc, cact, l, stacked, small):
    tok = lambda w: pl.BlockSpec((TM, w), lambda i: (i, 0))
    lat = lambda w: pl.BlockSpec((TM, w), lambda i: (jnp.minimum(i, LAT_TILES - 1), 0))
    wa, wf, wp, wc, wg, wo, rh, rl = stacked
    bg, gf, rb = small
    in_specs = _stream_specs(split) + [
                pl.BlockSpec((None, 6, D), lambda i: (_mod_row(i), 0, 0)), _const_spec((1, D)),
                lat(Q_W), _const_spec((N_CTX, Q_W)),
                lat(2 * FOURIER_W), _const_spec((N_CTX, 2 * FOURIER_W)),
                tok(POOL_W), tok(CONV_W)]
    in_specs += [_layer_spec(w.shape[1:], l) for w in (wa, wf, wp, wc, wg)]
    in_specs += [_const_spec(bg.shape), _layer_spec(wo.shape[1:], l), _const_spec(gf.shape),
                 _layer_spec(rh.shape[1:], l), _layer_spec(rl.shape[1:], l), _const_spec(rb.shape)]
    return pl.pallas_call(
        functools.partial(_mix_kernel, split=split),
        out_shape=[jax.ShapeDtypeStruct((N_TOK, D), F32), jax.ShapeDtypeStruct((N_TOK, D), BF16),
                   jax.ShapeDtypeStruct((N_TOK, LANES), F32)],
        grid=(NT,),
        in_specs=in_specs,
        out_specs=[tok(D), tok(D), tok(LANES)],
        compiler_params=_cparams(("parallel",)),
        name="mix",
    )(x_lat, x_ctx, mods_l, gn, a_lat, a_ctx, h_lat, h_ctx, zc, cact, wa, wf, wp, wc, wg, bg, wo, gf, rh, rl, rb)


def _onehots(route):
    lane = lax.broadcasted_iota(jnp.int32, route.shape, 1)
    e1 = route[:, 0:1].astype(jnp.int32)
    e2 = route[:, 1:2].astype(jnp.int32)
    return (lane == e1).astype(F32), (lane == e2).astype(F32)


def _lane_cumsum(row):
    lane = lax.broadcasted_iota(jnp.int32, row.shape, 1)
    sh = 1
    while sh < N_EXPERTS:
        row = row + jnp.where(lane >= sh, pltpu.roll(row, sh, 1), 0.0)
        sh *= 2
    return row


def _rank_kernel(rt_ref, tri_ref, pos_ref, meta_ref, cnt_ref, carry_ref):
    @pl.when(pl.program_id(0) == 0)
    def _():
        carry_ref[...] = jnp.zeros_like(carry_ref)

    tiles = []
    for t in range(RANK_TILES):
        oh1, oh2 = _onehots(rt_ref[t * TM:(t + 1) * TM, :])
        both = oh1 + oh2
        tiles.append((oh1, oh2, both, _dot(tri_ref[...], both.astype(BF16))))
    carry = carry_ref[0:1, :]
    lane = lax.broadcasted_iota(jnp.int32, (TM, LANES), 1)
    row = lax.broadcasted_iota(jnp.int32, meta_ref.shape[1:], 0)
    for t, (oh1, oh2, both, before) in enumerate(tiles):
        tile_cnt = jnp.sum(both, axis=0, keepdims=True)
        tile_cnt = tile_cnt + (tile_cnt - 2.0 * jnp.floor(tile_cnt * 0.5))
        tile_off = _lane_cumsum(tile_cnt) - tile_cnt
        where = before + tile_off
        p1 = jnp.sum(oh1 * where, axis=-1, keepdims=True)
        p2 = jnp.sum(oh2 * where, axis=-1, keepdims=True)
        pos_ref[t * TM:(t + 1) * TM, :] = jnp.where(lane == 0, p1, jnp.where(lane == 1, p2, 0.0))
        meta_ref[t] = jnp.where(row == 0, tile_off, jnp.where(row == 1, tile_cnt, jnp.where(row == 2, carry, 0.0)))
        carry = carry + tile_cnt
    carry_ref[...] = jnp.broadcast_to(carry, carry_ref.shape)
    cnt_ref[...] = jnp.broadcast_to(carry, cnt_ref.shape)


def _runs_kernel(meta_ref, cnt_ref, runs_ref, be_ref):
    lane = lax.broadcasted_iota(jnp.int32, (1, LANES), 1)
    counts = cnt_ref[0:1, :]
    padded = jnp.floor((counts + (MOE_BLK - 1)) * (1.0 / MOE_BLK)) * MOE_BLK
    padded = jnp.where(lane < N_EXPERTS, padded, 0.0)
    ends = _lane_cumsum(padded)
    starts = ends - padded
    for t in range(NT):
        m = meta_ref[t]
        row = lax.broadcasted_iota(jnp.int32, m.shape, 0)
        m = jnp.where(row == 2, m + starts, m)
        m = jnp.where(row == 3, starts + counts, jnp.where(row == 4, padded - counts, m))
        runs_ref[t] = m.astype(jnp.int32)
    blk = lax.broadcasted_iota(jnp.int32, be_ref.shape, 0).astype(F32) * MOE_BLK
    lane_b = lax.broadcasted_iota(jnp.int32, be_ref.shape, 1)
    done = jnp.where((ends <= blk) & (lane_b < N_EXPERTS), 1.0, 0.0)
    be = jnp.minimum(jnp.sum(done, axis=-1, keepdims=True), N_EXPERTS - 1.0)
    nblk = jnp.max(jnp.where(lane_b == N_EXPERTS - 1, ends, 0.0), axis=-1, keepdims=True) * (1.0 / MOE_BLK)
    be_ref[...] = jnp.where(lane_b == 0, be, jnp.where(lane_b == 1, nblk, 0.0)).astype(jnp.int32)


def _plan_call(route, tri):
    tok = pl.BlockSpec((RANK_TILES * TM, LANES), lambda i: (i, 0))
    pos, meta, counts = pl.pallas_call(
        _rank_kernel,
        out_shape=[jax.ShapeDtypeStruct((N_TOK, LANES), F32), jax.ShapeDtypeStruct((NT, 8, LANES), F32),
                   jax.ShapeDtypeStruct((8, LANES), F32)],
        grid=(NT // RANK_TILES,),
        in_specs=[tok, _const_spec((TM, TM))],
        out_specs=[tok, pl.BlockSpec((RANK_TILES, 8, LANES), lambda i: (i, 0, 0)),
                   pl.BlockSpec((8, LANES), lambda i: (0, 0))],
        scratch_shapes=[pltpu.VMEM((8, LANES), F32)],
        compiler_params=_cparams(("arbitrary",)),
        name="moe_rank",
    )(route, tri)
    runs, blk = pl.pallas_call(
        _runs_kernel,
        out_shape=[jax.ShapeDtypeStruct((NT, 8, LANES), jnp.int32),
                   jax.ShapeDtypeStruct((BLK_TABLE_ROWS, LANES), jnp.int32)],
        name="moe_runs",
    )(meta, counts)
    runs_flat = runs[:, 0:RUN_FIELDS, 0:N_EXPERTS].reshape(NT, 1, RUN_FIELDS * N_EXPERTS)
    return pos, runs_flat, blk[:N_MOE_BLOCKS, 0], blk[0:1, 1]


def _pack_pairs(x):
    half = x.shape[1] // 2
    lo = pltpu.bitcast(x[:, :half], jnp.uint32)
    hi = pltpu.bitcast(x[:, half:], jnp.uint32)
    return (lo >> 16) | (hi & jnp.uint32(0xFFFF0000))


def _unpack_pairs(w):
    lo = pltpu.bitcast(w << 16, F32)
    hi = pltpu.bitcast(w & jnp.uint32(0xFFFF0000), F32)
    return jnp.concatenate([lo, hi], axis=1).astype(BF16)


def _run_fields(runs_ref, e):
    return runs_ref[0, e], runs_ref[0, N_EXPERTS + e], runs_ref[0, 2 * N_EXPERTS + e]


def _store_rows(lin_ref, packed):
    rows = packed.shape[0]
    for c in range(PK):
        lin_ref[pl.ds(c, rows, stride=PK), :] = packed[:, c * LANES:(c + 1) * LANES]


def _load_rows(lin_ref, rows):
    return jnp.concatenate([lin_ref[pl.ds(c, rows, stride=PK), :] for c in range(PK)], axis=1)


def _lin(ref, row, nrows):
    return ref.at[pl.ds(pl.multiple_of(row * PK, SUBLANES), nrows * PK), :]


FETCH_ROWS = 64
RUN_PIECES = (32, 16, 8, 4, 2)
TAIL_PIECES = (256, 128, 64, 32, 16, 8, 4, 2)


def _run_copies(runs, make_copy, act):
    def per_expert(e, carry):
        off, n, dst = _run_fields(runs, e)
        whole = n // FETCH_ROWS

        def chunk(k, c):
            act(make_copy(off + k * FETCH_ROWS, dst + k * FETCH_ROWS, FETCH_ROWS))
            return c

        lax.fori_loop(0, whole, chunk, 0)
        done = whole * FETCH_ROWS
        for size in RUN_PIECES:
            @pl.when((n & size) != 0)
            def _(done=done, size=size):
                act(make_copy(off + done, dst + done, size))
            done = done + (n & size)
        return carry

    lax.fori_loop(0, N_EXPERTS, per_expert, 0)


def _rows_wait(runs, make_copy):
    total = lax.fori_loop(0, N_EXPERTS, lambda e, acc: acc + runs[0, N_EXPERTS + e], jnp.int32(0))
    lax.fori_loop(0, total // FETCH_ROWS, lambda k, c: (make_copy(0, 0, FETCH_ROWS).wait(), c)[1], 0)
    for size in RUN_PIECES:
        @pl.when((total & size) != 0)
        def _(size=size):
            make_copy(0, 0, size).wait()


def _dispatch_kernel(runs_ref, prev_runs_ref, pos_ref, h2_ref, xs_ref, buf0, buf1, zero_ref, sem0, sem1, zsem,
                     ssem):
    i = pl.program_id(0)
    last = NT - 1

    def tail_copies(act):
        def per_expert(e, carry):
            row = runs_ref[0, 3 * N_EXPERTS + e]
            n = runs_ref[0, 4 * N_EXPERTS + e]
            done = jnp.int32(0)
            for size in TAIL_PIECES:
                @pl.when((n & size) != 0)
                def _(done=done, size=size):
                    act(pltpu.make_async_copy(zero_ref.at[pl.ds(0, size * PK), :], _lin(xs_ref, row + done, size), zsem))
                done = done + (n & size)
            return carry

        lax.fori_loop(0, N_EXPERTS, per_expert, 0)

    def spare_copies(act):
        used = runs_ref[0, 3 * N_EXPERTS + N_EXPERTS - 1] + runs_ref[0, 4 * N_EXPERTS + N_EXPERTS - 1]

        def spare_block(k, carry):
            act(pltpu.make_async_copy(zero_ref.at[pl.ds(0, MOE_BLK * PK), :],
                                      _lin(xs_ref, used + k * MOE_BLK, MOE_BLK), ssem))
            return carry

        lax.fori_loop(0, N_MOE_BLOCKS - used // MOE_BLK, spare_block, 0)

    start = lambda d: d.start()
    wait = lambda d: d.wait()

    @pl.when(i == 0)
    def _():
        zero_ref[...] = jnp.zeros_like(zero_ref)
        tail_copies(start)
        spare_copies(start)

    pos = pos_ref[...]
    col = lax.broadcasted_iota(jnp.int32, (TM, TS), 1).astype(F32)
    sel = jnp.where((col == pos[:, 0:1]) | (col == pos[:, 1:2]), 1.0, 0.0).astype(BF16)
    srt = lax.dot_general(sel, h2_ref[...], (((0,), (0,)), ((), ())), preferred_element_type=F32)
    packed = _pack_pairs(srt)

    for par, (buf, sem, obuf, osem) in enumerate(((buf0, sem0, buf1, sem1), (buf1, sem1, buf0, sem0))):
        @pl.when(i % 2 == par)
        def _(buf=buf, sem=sem, obuf=obuf, osem=osem):
            copy = lambda off, dst, rows: pltpu.make_async_copy(_lin(buf, off, rows), _lin(xs_ref, dst, rows), sem)
            ocopy = lambda off, dst, rows: pltpu.make_async_copy(_lin(obuf, off, rows), _lin(xs_ref, dst, rows), osem)
            _store_rows(buf, packed)

            @pl.when(i == 0)
            def _():
                tail_copies(wait)

            _run_copies(runs_ref, copy, start)

            @pl.when(i > 0)
            def _():
                _rows_wait(prev_runs_ref, ocopy)

            @pl.when(i == last)
            def _():
                _rows_wait(runs_ref, copy)
                spare_copies(wait)


def _dispatch_call(runs_flat, pos, h2):
    runs_spec = lambda shift: pl.BlockSpec((None, 1, RUN_FIELDS * N_EXPERTS), lambda i: (jnp.maximum(i - shift, 0), 0, 0),
                                           memory_space=pltpu.SMEM)
    sorted_buf = pltpu.VMEM((TS * PK, LANES), jnp.uint32)
    return pl.pallas_call(
        _dispatch_kernel,
        out_shape=jax.ShapeDtypeStruct((N_SLOTS * PK, LANES), jnp.uint32),
        grid=(NT,),
        in_specs=[runs_spec(0), runs_spec(1),
                  pl.BlockSpec((TM, LANES), lambda i: (i, 0)),
                  pl.BlockSpec((TM, D), lambda i: (i, 0))],
        out_specs=pl.BlockSpec(memory_space=pl.ANY),
        scratch_shapes=[sorted_buf, sorted_buf, pltpu.VMEM((MOE_BLK * PK, LANES), jnp.uint32),
                        pltpu.SemaphoreType.DMA, pltpu.SemaphoreType.DMA, pltpu.SemaphoreType.DMA,
                        pltpu.SemaphoreType.DMA],
        compiler_params=_cparams(("arbitrary",)),
        name="moe_dispatch",
    )(runs_flat, runs_flat, pos, h2)


def _expert_kernel(be_ref, nu_ref, xs_ref, wg_ref, wu_ref, wd_ref, ys_ref, wgb_ref, wub_ref, wdb_ref):
    b = pl.program_id(0)

    @pl.when(jnp.logical_or(b == 0, be_ref[b] != be_ref[jnp.maximum(b - 1, 0)]))
    def _():
        wgb_ref[...] = wg_ref[...].astype(BF16)
        wub_ref[...] = wu_ref[...].astype(BF16)
        wdb_ref[...] = wd_ref[...].astype(BF16)

    @pl.when(b < nu_ref[0])
    def _():
        xb = _unpack_pairs(_load_rows(xs_ref, MOE_BLK))
        g = _dot(xb, wgb_ref[...])
        u = _dot(xb, wub_ref[...])
        hmid = (g * _sigmoid(g)) * u
        y = _dot(hmid.astype(BF16), wdb_ref[...])
        _store_rows(ys_ref, _pack_pairs(y.astype(BF16).astype(F32)))

    @pl.when(b >= nu_ref[0])
    def _():
        ys_ref[...] = jnp.zeros_like(ys_ref)


def _expert_call(blk_e, n_used, xs, wg, wu, wd, l):
    wspec = lambda k, n: pl.BlockSpec((None, None, k, n), lambda b, be, nu: (l, be[b], 0, 0))
    return pl.pallas_call(
        _expert_kernel,
        out_shape=jax.ShapeDtypeStruct((N_SLOTS * PK, LANES), jnp.uint32),
        grid_spec=pltpu.PrefetchScalarGridSpec(
            num_scalar_prefetch=2,
            grid=(N_MOE_BLOCKS,),
            in_specs=[pl.BlockSpec((MOE_BLK * PK, LANES), lambda b, be, nu: (jnp.minimum(b, nu[0] - 1), 0)),
                      wspec(D, EXPERT_HIDDEN), wspec(D, EXPERT_HIDDEN), wspec(EXPERT_HIDDEN, D)],
            out_specs=pl.BlockSpec((MOE_BLK * PK, LANES), lambda b, be, nu: (b, 0)),
            scratch_shapes=[pltpu.VMEM((D, EXPERT_HIDDEN), BF16), pltpu.VMEM((D, EXPERT_HIDDEN), BF16),
                            pltpu.VMEM((EXPERT_HIDDEN, D), BF16)]),
        compiler_params=_cparams(("arbitrary",)),
        name="moe_experts",
    )(blk_e, n_used, xs, wg, wu, wd)


def _combine_kernel(runs_ref, next_runs_ref, ys_ref, pos_ref, x_ref, rt_ref, mod_ref, o_ref, buf0, buf1, sem0, sem1,
                    *, n_tiles):
    i = pl.program_id(0)
    last = n_tiles - 1
    start = lambda d: d.start()
    fetch = lambda buf, sem: (
        lambda off, dst, rows: pltpu.make_async_copy(_lin(ys_ref, dst, rows), _lin(buf, off, rows), sem))

    @pl.when(i == 0)
    def _():
        buf0[...] = jnp.zeros_like(buf0)
        buf1[...] = jnp.zeros_like(buf1)
        _run_copies(runs_ref, fetch(buf0, sem0), start)

    pos = pos_ref[...]
    rt = rt_ref[...]
    col = lax.broadcasted_iota(jnp.int32, (TM, TS), 1).astype(F32)
    pick = (jnp.where(col == pos[:, 0:1], rt[:, TOP_K:TOP_K + 1], 0.0)
            + jnp.where(col == pos[:, 1:2], rt[:, TOP_K + 1:TOP_K + 2], 0.0)).astype(BF16)

    for par, (buf, sem, obuf, osem) in enumerate(((buf0, sem0, buf1, sem1), (buf1, sem1, buf0, sem0))):
        @pl.when(i % 2 == par)
        def _(buf=buf, sem=sem, obuf=obuf, osem=osem):
            @pl.when(i < last)
            def _():
                _run_copies(next_runs_ref, fetch(obuf, osem), start)

            _rows_wait(runs_ref, fetch(buf, sem))
            ysb = _unpack_pairs(_load_rows(buf, TS))
            o_ref[...] = x_ref[...] + mod_ref[5:6, :] * _dot(pick, ysb)


def _combine_call(runs_flat, ys, pos, x, route, mods_l, n_tiles):
    tok = lambda w: pl.BlockSpec((TM, w), lambda i: (i, 0))
    runs_spec = lambda shift: pl.BlockSpec((None, 1, RUN_FIELDS * N_EXPERTS),
                                           lambda i: (jnp.minimum(i + shift, n_tiles - 1), 0, 0),
                                           memory_space=pltpu.SMEM)
    sorted_buf = pltpu.VMEM((TS * PK, LANES), jnp.uint32)
    return pl.pallas_call(
        functools.partial(_combine_kernel, n_tiles=n_tiles),
        out_shape=jax.ShapeDtypeStruct((n_tiles * TM, D), F32),
        grid=(n_tiles,),
        in_specs=[runs_spec(0), runs_spec(1),
                  pl.BlockSpec(memory_space=pl.ANY),
                  tok(LANES), tok(D), tok(LANES),
                  pl.BlockSpec((None, 6, D), lambda i: (_mod_row(i), 0, 0))],
        out_specs=tok(D),
        scratch_shapes=[sorted_buf, sorted_buf, pltpu.SemaphoreType.DMA, pltpu.SemaphoreType.DMA],
        compiler_params=_cparams(("arbitrary",)),
        name="moe_combine",
    )(runs_flat, runs_flat, ys, pos, x, route, mods_l)


def _rope_tables():
    nf = HEAD_DIM // 4
    inv = ROPE_BASE ** (-jnp.arange(nf, dtype=F32) / nf)
    t = jnp.arange(S)
    row = (t // GRID_W).astype(F32)[:, None] * inv[None, :]
    col = (t % GRID_W).astype(F32)[:, None] * inv[None, :]
    zero = jnp.zeros_like(row)
    cos = jnp.concatenate([jnp.cos(row), jnp.cos(row), jnp.cos(col), jnp.cos(col)], axis=1)
    sa = jnp.concatenate([-jnp.sin(row), zero, -jnp.sin(col), zero], axis=1)
    sb = jnp.concatenate([zero, jnp.sin(row), zero, jnp.sin(col)], axis=1)
    ident = (jnp.ones((TM, HEAD_DIM), F32), jnp.zeros((TM, HEAD_DIM), F32), jnp.zeros((TM, HEAD_DIM), F32))
    return tuple(jnp.tile(jnp.concatenate([a, b], axis=0), (1, LANES // HEAD_DIM))
                 for a, b in zip((cos, sa, sb), ident))


def _fourier_tables():
    s1 = np.arange(FS1)
    ang1 = 2.0 * np.pi * np.outer(s1, s1) / FS1
    w1 = np.concatenate([np.cos(ang1), -np.sin(ang1)], axis=0) / np.sqrt(S)
    k1 = np.arange(FS1)[:, None, None]
    k2 = np.arange(FS2)[None, :, None]
    s2 = np.arange(FS2)[None, None, :]
    ang2 = 2.0 * np.pi * ((k1 + FS1 * k2) * s2 % S) / S
    c2, sn2 = np.cos(ang2), np.sin(ang2)
    ta = np.concatenate([c2, -sn2], axis=1)
    tb = np.concatenate([sn2, c2], axis=1)
    sc = np.arange(C)
    angc = 2.0 * np.pi * np.outer(sc, sc) / C
    wc = np.concatenate([np.cos(angc), -np.sin(angc)], axis=0) / np.sqrt(C)
    return tuple(jnp.asarray(a, F32).astype(BF16) for a in (w1, ta, tb, wc))


def _channel_dft():
    cidx = np.arange(FOURIER_GROUP_W)
    ang = 2.0 * np.pi * np.outer(cidx, cidx) / FOURIER_GROUP_W
    eye = np.eye(FOURIER_W // FOURIER_GROUP_W)
    cw = np.kron(eye, np.cos(ang)) / np.sqrt(FOURIER_GROUP_W)
    sw = np.kron(eye, np.sin(ang)) / np.sqrt(FOURIER_GROUP_W)
    return jnp.asarray(np.concatenate([cw, sw], axis=0), F32)


def _pool_bands():
    t = np.arange(TP)[:, None]
    main, halo = [], []
    for w in POOL_WINDOWS:
        def hit(j):
            return ((j - t >= -(w // 2)) & (j - t <= w // 2 - 1)).astype(np.float32)
        main.append(hit(np.arange(TP)[None, :]))
        halo.append(np.concatenate([hit(np.arange(-HALO, 0)[None, :]),
                                    hit(np.arange(TP, TP + HALO)[None, :])], axis=1))
    return (jnp.asarray(np.stack(main), F32).astype(BF16), jnp.asarray(np.stack(halo), F32).astype(BF16))


def _pool_inv_counts():
    win = np.repeat(np.array(POOL_WINDOWS), POOL_GROUP_W)[None, :]

    def table(pos0, seq_len):
        pos = (pos0 + np.arange(TP))[:, None]
        lo = np.clip(pos - win // 2, 0, seq_len)
        hi = np.clip(pos - win // 2 + win, 0, seq_len)
        return 1.0 / (hi - lo)

    tabs = [table(TP, S), table(0, S), table(S - TP, S), table(0, C)]
    return jnp.asarray(np.stack(tabs), F32)


def _conv_shifts():
    i = np.arange(CONV_WIN)
    return jnp.asarray(np.stack([(i[None, :] == i[:, None] + s) for s in range(1, 8)]), F32).astype(BF16)


def _fold_kernel(a_ref, b_ref, o_ref):
    a, b = a_ref[...], b_ref[...]
    a_hi, b_hi = a.astype(BF16), b.astype(BF16)
    a_lo = (a - a_hi.astype(F32)).astype(BF16)
    b_lo = (b - b_hi.astype(F32)).astype(BF16)
    o_ref[...] = (_dot(a_hi, b_hi) + _dot(a_lo, b_hi) + _dot(a_hi, b_lo)).astype(BF16)


def _fold_fourier_weights(dftw, w_br_fourier):
    nl = w_br_fourier.shape[0]
    return pl.pallas_call(
        _fold_kernel,
        out_shape=jax.ShapeDtypeStruct((nl, 2 * FOURIER_W, D), BF16),
        grid=(nl,),
        in_specs=[pl.BlockSpec((2 * FOURIER_W, FOURIER_W), lambda l: (0, 0)),
                  pl.BlockSpec((None, FOURIER_W, D), lambda l: (l, 0, 0))],
        out_specs=pl.BlockSpec((None, 2 * FOURIER_W, D), lambda l: (l, 0, 0)),
        compiler_params=_cparams(("arbitrary",)),
        name="fold_fourier_proj",
    )(dftw, w_br_fourier)


def _block_diag(blocks):
    n, r, c = blocks.shape
    eye = jnp.eye(n, dtype=blocks.dtype)
    return (blocks[:, :, None, :] * eye[:, None, :, None]).reshape(n * r, n * c)


def kernel(x, c, ctx, c_ctx, w_ada, b_ada, g_norm_mix, g_norm_ffn, w_in, g_q, g_k, sink, w_br_attn,
           w_br_fourier, pool_w, pool_scale, w_br_pool, conv_w, conv_b, cn_g, cn_b, w_br_conv, w_gate,
           b_gate, w_out, w_router_grp, b_router_grp, w_router_exp, b_router_exp, w_e_gate, w_e_up,
           w_e_down):
    x_lat, x_ctx, split = x.reshape(N_LAT, D), ctx.reshape(N_CTX, D), True
    nl = w_ada.shape[0]
    mods = _ada_all(c, c_ctx, w_ada, b_ada).reshape(nl, 8, 6, D)
    rope_tabs = _rope_tables()
    four_tabs = _fourier_tables()
    band_main, band_halo = _pool_bands()
    inv_cnt = _pool_inv_counts()
    shifts = _conv_shifts()
    wf_all = _fold_fourier_weights(_channel_dft(), w_br_fourier)
    bd = jnp.asarray(np.kron(np.eye(LANES // HEAD_DIM), np.ones((HEAD_DIM, HEAD_DIM))), F32).astype(BF16)
    tri = jnp.asarray(np.tril(np.ones((TM, TM)), -1), F32).astype(BF16)
    rpad = jnp.zeros((nl, D, LANES - N_GROUPS - N_EXPERTS), F32)
    w_router = jnp.concatenate([w_router_grp, w_router_exp, rpad], axis=-1)
    r_hi = w_router.astype(BF16)
    r_lo = (w_router - r_hi.astype(F32)).astype(BF16)
    r_b = jnp.concatenate([b_router_grp, b_router_exp, rpad[:, 0, :]], axis=-1).reshape(nl, 1, LANES)
    stacked = tuple(w.astype(BF16) for w in (w_br_attn,)) + (wf_all,) + tuple(
        w.astype(BF16) for w in (w_br_pool, w_br_conv, w_gate, w_out)) + (r_hi, r_lo)

    for l in range(nl):
        mods_l = mods[l]
        gn = g_norm_mix[l].reshape(1, D)
        q, kv, f, p, u = _proj_call(x_lat, x_ctx, split, mods_l, gn, w_in, l, rope_tabs,
                                    jnp.tile(g_q[l], 2).reshape(1, LANES),
                                    jnp.tile(g_k[l], 2).reshape(1, LANES), bd)
        a_re, a_im = _fourier_stage1_call(f, four_tabs)
        a_lat, a_ctx = _attn_call(sink[l], q, kv)
        h_lat, h_ctx = _fourier_stage2_call(a_re, a_im, f, four_tabs, a_lat)
        zc, cact = _poolconv_call(p, u, band_main, band_halo, inv_cnt, _block_diag(pool_w[l]).astype(BF16),
                                  pool_scale[l].reshape(1, POOL_W), shifts, conv_w[l], conv_b[l].reshape(1, CONV_W),
                                  cn_g[l].reshape(1, CONV_W), cn_b[l].reshape(1, CONV_W))
        small = (b_gate[l].reshape(1, 4 * D), g_norm_ffn[l].reshape(1, D), r_b[l])
        xs, h2, route = _mix_call(x_lat, x_ctx, split, mods_l, gn, a_lat, a_ctx, h_lat, h_ctx, zc, cact, l,
                                  stacked, small)
        pos, runs_flat, blk_e, n_used = _plan_call(route, tri)
        slots = _dispatch_call(runs_flat, pos, h2)
        ys = _expert_call(blk_e, n_used, slots, w_e_gate, w_e_up, w_e_down, l)
        xs = _combine_call(runs_flat, ys, pos, xs, route, mods_l, LAT_TILES if l == nl - 1 else NT)
        x_lat, x_ctx, split = xs, xs, False
    return xs.reshape(B, S, D)
```

```python
import functools

import numpy as np
import jax
import jax.numpy as jnp
from jax import lax
from jax.experimental import pallas as pl
from jax.experimental.pallas import tpu as pltpu

F32 = jnp.float32
BF16 = jnp.bfloat16

D = 1024
B = 2
S = 8192
C = 256
GRID_W = 64
HEAD_DIM = 64
N_Q_HEADS = 8
N_KV_HEADS = 2
GQA = N_Q_HEADS // N_KV_HEADS
WINDOW = 128
ATTN_BLK = 128
ATTN_QB = 8
ROPE_BASE = 10000.0
Q_W = 512
KV_W = 128
FOURIER_W = 640
FOURIER_GROUP_W = 160
POOL_W = 640
POOL_GROUP_W = 160
POOL_WINDOWS = (2, 4, 8, 16)
CONV_W = 512
CONV_K = 31
PROJ_W = 3072
N_GROUPS = 4
EPG = 8
N_EXPERTS = 32
TOP_K = 2
EXPERT_HIDDEN = 512
MOE_BLK = 512
EPS = 1e-6
NEG_INF = -1e30
LOG2E = 1.4426950408889634

N_LAT = B * S
N_CTX = B * C
N_TOK = N_LAT + N_CTX
TM = 512
NT = N_TOK // TM
LAT_TILES = N_LAT // TM
TILES_PER_BATCH = S // TM
TP = 256
NTP = N_TOK // TP
HALO = 16
CONV_WIN = TP // 2 + 2 * HALO
N_ASSIGN = N_TOK * TOP_K
RANK_TILES = 3
RUN_FIELDS = 5
PK = D // 2 // 128
TS = 1152
N_MOE_BLOCKS = (N_ASSIGN + NT * N_EXPERTS + N_EXPERTS * (MOE_BLK - 1)) // MOE_BLK
N_SLOTS = N_MOE_BLOCKS * MOE_BLK
SUBLANES = 8
BLK_TABLE_ROWS = -(-N_MOE_BLOCKS // SUBLANES) * SUBLANES
ROPE_PAIR = HEAD_DIM // 4
FS1 = 64
FS2 = 128
F1_ROWS = 32
F2_K1 = 16
LANES = 128
VMEM_LIMIT = 56 * 1024 * 1024


def _cparams(sem, vmem=VMEM_LIMIT):
    return pltpu.CompilerParams(dimension_semantics=sem, vmem_limit_bytes=vmem)


def _const_spec(shape):
    nd = len(shape)
    return pl.BlockSpec(shape, lambda *_: (0,) * nd, pipeline_mode=pl.Buffered(1))


def _dot(a, b):
    return jnp.dot(a, b, preferred_element_type=F32)


def _modulate(x, g, shift, scale):
    y = x * lax.rsqrt(jnp.mean(x * x, axis=-1, keepdims=True) + EPS)
    return (y * g) * (1.0 + scale) + shift


def _sigmoid(x):
    return 1.0 / (1.0 + jnp.exp(-x))


def _ada_kernel(ct_ref, w_ref, b_ref, o_ref):
    ct = ct_ref[...]
    s = ct * _sigmoid(ct)
    w = w_ref[...]
    rows = [jnp.sum(w * s[:, r:r + 1], axis=0, keepdims=True) for r in range(3)]
    rows.append(jnp.zeros((5, w.shape[1]), F32))
    o_ref[...] = jnp.concatenate(rows, axis=0) + b_ref[...]


def _ada_all(c, c_ctx, w_ada, b_ada):
    ct = jnp.concatenate([c, c_ctx[None, :], jnp.zeros((5, D), F32)], axis=0).T
    cols = 1536
    nl = w_ada.shape[0]
    return pl.pallas_call(
        _ada_kernel,
        out_shape=jax.ShapeDtypeStruct((nl, 8, 6 * D), F32),
        grid=(nl, 6 * D // cols),
        in_specs=[pl.BlockSpec((D, 8), lambda l, j: (0, 0)),
                  pl.BlockSpec((None, D, cols), lambda l, j: (l, 0, j)),
                  pl.BlockSpec((None, 1, cols), lambda l, j: (l, 0, j))],
        out_specs=pl.BlockSpec((None, 8, cols), lambda l, j: (l, 0, j)),
        compiler_params=_cparams(("arbitrary", "arbitrary")),
        name="adaln",
    )(ct, w_ada, b_ada.reshape(nl, 1, 6 * D))


def _head_rms(t, g128, bd):
    outs = []
    for j in range(t.shape[1] // LANES):
        blk = t[:, j * LANES:(j + 1) * LANES]
        ss = _dot((blk * blk).astype(BF16), bd)
        outs.append(blk * lax.rsqrt(ss * (1.0 / HEAD_DIM) + EPS) * g128)
    return outs


def _rope(blocks, cos, sa, sb):
    outs = []
    for blk in blocks:
        up = pltpu.roll(blk, LANES - ROPE_PAIR, 1)
        dn = pltpu.roll(blk, ROPE_PAIR, 1)
        outs.append(blk * cos + up * sa + dn * sb)
    return outs


def _proj_kernel(xl_ref, xc_ref, mod_ref, gn_ref, w_ref, cos_ref, sa_ref, sb_ref, gq_ref, gk_ref, bd_ref,
                 q_ref, kv_ref, f_ref, p_ref, u_ref, wbf_ref, *, split):
    @pl.when(pl.program_id(0) == 0)
    def _():
        wbf_ref[...] = w_ref[...].astype(BF16)

    m = mod_ref[...]
    hb = _modulate(_stream_tile(xl_ref, xc_ref, split), gn_ref[...], m[0:1], m[1:2]).astype(BF16)
    cos, sa, sb, bd = cos_ref[...], sa_ref[...], sb_ref[...], bd_ref[...]
    o_kv, o_f, o_a = Q_W, Q_W + 2 * KV_W, Q_W + 2 * KV_W + FOURIER_W + POOL_W
    qkv = _dot(hb, wbf_ref[:, 0:o_f])
    fp = _dot(hb, wbf_ref[:, o_f:o_a])
    q = _rope(_head_rms(qkv[:, 0:Q_W], gq_ref[...], bd), cos, sa, sb)
    q_ref[...] = (jnp.concatenate(q, axis=1) * (LOG2E * HEAD_DIM ** -0.5)).astype(BF16)
    k = _rope(_head_rms(qkv[:, o_kv:o_kv + KV_W], gk_ref[...], bd), cos, sa, sb)
    kv_ref[:, 0:KV_W] = k[0].astype(BF16)
    kv_ref[:, KV_W:2 * KV_W] = qkv[:, o_kv + KV_W:o_f].astype(BF16)
    ag = _dot(hb, wbf_ref[:, o_a:PROJ_W])
    f_ref[...] = fp[:, 0:FOURIER_W].astype(BF16)
    p_ref[...] = fp[:, FOURIER_W:].astype(BF16)
    u_ref[...] = (ag[:, 0:CONV_W] * _sigmoid(ag[:, CONV_W:])).astype(BF16)


def _mod_row(i):
    return jnp.minimum(i // TILES_PER_BATCH, 2)


def _layer_spec(shape, l):
    nd = len(shape)
    return pl.BlockSpec((None,) + tuple(shape), lambda *_: (l,) + (0,) * nd, pipeline_mode=pl.Buffered(1))


def _stream_specs(split):
    first = (lambda i: (jnp.minimum(i, LAT_TILES - 1), 0)) if split else (lambda i: (i, 0))
    return [pl.BlockSpec((TM, D), first),
            pl.BlockSpec((TM, D), lambda i: (0, 0), pipeline_mode=pl.Buffered(1))]


def _stream_tile(xl_ref, xc_ref, split):
    return jnp.where(pl.program_id(0) >= LAT_TILES, xc_ref[...], xl_ref[...]) if split else xl_ref[...]


def _proj_call(x_lat, x_ctx, split, mods_l, gn, w_in, l, rope_tabs, gq128, gk128, bd):
    cos, sa, sb = rope_tabs
    tok = lambda w: pl.BlockSpec((TM, w), lambda i: (i, 0))
    rope_spec = pl.BlockSpec((TM, LANES), lambda i: (jnp.where(i < LAT_TILES, i % TILES_PER_BATCH,
                                                               TILES_PER_BATCH), 0))
    widths = (Q_W, 2 * KV_W, FOURIER_W, POOL_W, CONV_W)
    return pl.pallas_call(
        functools.partial(_proj_kernel, split=split),
        out_shape=[jax.ShapeDtypeStruct((N_TOK, w), BF16) for w in widths],
        grid=(NT,),
        in_specs=_stream_specs(split) + [
                  pl.BlockSpec((None, 6, D), lambda i: (_mod_row(i), 0, 0)),
                  _const_spec((1, D)),
                  _layer_spec((D, PROJ_W), l),
                  rope_spec, rope_spec, rope_spec,
                  _const_spec((1, LANES)), _const_spec((1, LANES)),
                  _const_spec((LANES, LANES))],
        out_specs=[tok(w) for w in widths],
        scratch_shapes=[pltpu.VMEM((D, PROJ_W), BF16)],
        compiler_params=_cparams(("arbitrary",)),
        name="proj",
    )(x_lat, x_ctx, mods_l, gn, w_in, cos, sa, sb, gq128, gk128, bd)


def _attend_many(jobs, sink_ref):
    lane = lax.broadcasted_iota(jnp.int32, (ATTN_BLK, LANES), 1)
    chains = []
    for q, kv_blocks, biases in jobs:
        for j in range(N_KV_HEADS):
            ks = slice(j * HEAD_DIM, (j + 1) * HEAD_DIM)
            vs = slice(KV_W + j * HEAD_DIM, KV_W + (j + 1) * HEAD_DIM)
            kj = jnp.concatenate([blk[:, ks] for blk in kv_blocks], axis=0)
            vj = jnp.concatenate([blk[:, vs] for blk in kv_blocks], axis=0)
            vaug = jnp.concatenate([vj, jnp.ones_like(vj)], axis=1)
            qs = jnp.concatenate([q[:, (j * GQA + g) * HEAD_DIM:(j * GQA + g + 1) * HEAD_DIM]
                                  for g in range(GQA)], axis=0)
            s = lax.dot_general(qs, kj, (((1,), (1,)), ((), ())), preferred_element_type=F32)
            chains.append((j, s, vaug, kv_blocks, biases))
    soft = []
    for j, s, vaug, kv_blocks, biases in chains:
        probs, sink_terms = [], []
        for g in range(GQA):
            sg = s[g * ATTN_BLK:(g + 1) * ATTN_BLK]
            pieces, col = [], 0
            for blk, bias in zip(kv_blocks, biases):
                piece = sg[:, col:col + blk.shape[0]]
                pieces.append(piece if bias is None else piece + bias)
                col += blk.shape[0]
            sg = jnp.concatenate(pieces, axis=1)
            sk = sink_ref[j * GQA + g] * LOG2E
            mx = jnp.maximum(jnp.max(sg, axis=-1, keepdims=True), sk)
            probs.append(jnp.exp2(sg - mx).astype(BF16))
            sink_terms.append(jnp.exp2(sk - mx))
        soft.append((jnp.concatenate(probs, axis=0), vaug, sink_terms))
    heads = []
    for p, vaug, sink_terms in soft:
        o = _dot(p, vaug)
        for g in range(GQA):
            og = o[g * ATTN_BLK:(g + 1) * ATTN_BLK]
            heads.append(og / (og[:, HEAD_DIM:HEAD_DIM + 1] + sink_terms[g]))
    outs = []
    for n in range(len(jobs)):
        hs = heads[n * N_Q_HEADS:(n + 1) * N_Q_HEADS]
        tiles = [jnp.where(lane < HEAD_DIM, hs[2 * t], pltpu.roll(hs[2 * t + 1], HEAD_DIM, 1))
                 for t in range(N_Q_HEADS // 2)]
        outs.append(jnp.concatenate(tiles, axis=1).astype(BF16))
    return outs


def _attn_latent_kernel(sink_ref, q_ref, prev_ref, cur_ref, next_ref, ctx_ref, o_ref):
    n = pl.program_id(1)
    r = lax.broadcasted_iota(jnp.int32, (ATTN_BLK, ATTN_BLK), 0)
    jj = lax.broadcasted_iota(jnp.int32, (ATTN_BLK, ATTN_BLK), 1)
    far = jnp.int32(2 * ATTN_BLK)
    off_prev = jnp.where(n > 0, 0, far)
    off_next = jnp.where(n < S // (ATTN_QB * ATTN_BLK) - 1, 0, far)
    prev_ok = jnp.where(jj - r >= 0, 0.0, NEG_INF)
    next_ok = jnp.where(r - jj >= 0, 0.0, NEG_INF)
    prev_edge = jnp.where(jj - r - off_prev >= 0, 0.0, NEG_INF)
    next_edge = jnp.where(r - jj - off_next >= 0, 0.0, NEG_INF)
    ctx = ctx_ref[...]
    rows = lambda b: slice(b * ATTN_BLK, (b + 1) * ATTN_BLK)
    blocks = [prev_ref[...]] + [cur_ref[rows(b), :] for b in range(ATTN_QB)] + [next_ref[...]]
    jobs = [(q_ref[rows(b), :], [ctx] + blocks[b:b + 3],
             [None, prev_edge if b == 0 else prev_ok, None, next_edge if b == ATTN_QB - 1 else next_ok])
            for b in range(ATTN_QB)]
    for b, out in enumerate(_attend_many(jobs, sink_ref)):
        o_ref[rows(b), :] = out


def _attn_context_kernel(sink_ref, q_ref, ctx_ref, o_ref):
    o_ref[...] = _attend_many([(q_ref[...], [ctx_ref[...]], [None])], sink_ref)[0]


def _attn_call(sink_l, q, kv):
    nb = S // ATTN_BLK
    nq = nb // ATTN_QB
    smem = pl.BlockSpec(memory_space=pltpu.SMEM)
    pair = lambda w: pl.BlockSpec((ATTN_QB * ATTN_BLK, w), lambda b, n: (b * nq + n, 0))
    prev = pl.BlockSpec((ATTN_BLK, 2 * KV_W), lambda b, n: (b * nb + jnp.maximum(ATTN_QB * n - 1, 0), 0))
    nxt = pl.BlockSpec((ATTN_BLK, 2 * KV_W),
                       lambda b, n: (b * nb + jnp.minimum(ATTN_QB * (n + 1), nb - 1), 0))
    ctxs = pl.BlockSpec((C, 2 * KV_W), lambda b, n: (N_LAT // C + b, 0))
    lat = pl.pallas_call(
        _attn_latent_kernel,
        out_shape=jax.ShapeDtypeStruct((N_LAT, Q_W), BF16),
        grid=(B, nq),
        in_specs=[smem, pair(Q_W), prev, pair(2 * KV_W), nxt, ctxs],
        out_specs=pair(Q_W),
        compiler_params=_cparams(("parallel", "parallel")),
        name="attn_latent",
    )(sink_l, q, kv, kv, kv, kv)
    ncb = C // ATTN_BLK
    base = N_LAT // ATTN_BLK
    ctx = pl.pallas_call(
        _attn_context_kernel,
        out_shape=jax.ShapeDtypeStruct((N_CTX, Q_W), BF16),
        grid=(B, ncb),
        in_specs=[smem, pl.BlockSpec((ATTN_BLK, Q_W), lambda b, n: (base + b * ncb + n, 0)), ctxs],
        out_specs=pl.BlockSpec((ATTN_BLK, Q_W), lambda b, n: (b * ncb + n, 0)),
        compiler_params=_cparams(("parallel", "parallel")),
        name="attn_context",
    )(sink_l, q, kv)
    return lat, ctx


def _f1_kernel(w_ref, f_ref, re_ref, im_ref):
    res = lax.dot_general(w_ref[...], f_ref[...], (((1,), (0,)), ((), ())), preferred_element_type=F32)
    re_ref[...] = res[:FS1].astype(BF16)
    im_ref[...] = res[FS1:].astype(BF16)


def _f2_kernel(ta_ref, tb_ref, re_ref, im_ref, after_ref, o_ref):
    del after_ref
    for i in range(F2_K1):
        res = _dot(ta_ref[i], re_ref[i]) + _dot(tb_ref[i], im_ref[i])
        o_ref[i, :, 0:FOURIER_W] = res[:FS2].astype(BF16)
        o_ref[i, :, FOURIER_W:2 * FOURIER_W] = res[FS2:].astype(BF16)


def _fc_kernel(w_ref, f_ref, o_ref):
    res = _dot(w_ref[...], f_ref[...])
    o_ref[:, 0:FOURIER_W] = res[:C].astype(BF16)
    o_ref[:, FOURIER_W:2 * FOURIER_W] = res[C:].astype(BF16)


def _fourier_stage1_call(f, tabs):
    w1 = tabs[0]
    f3 = f.reshape(N_TOK // FS2, FS2, FOURIER_W)
    blk = pl.BlockSpec((FS1, F1_ROWS, FOURIER_W), lambda b, j: (b, j, 0))
    return pl.pallas_call(
        _f1_kernel,
        out_shape=[jax.ShapeDtypeStruct((B * FS1, FS2, FOURIER_W), BF16)] * 2,
        grid=(B, FS2 // F1_ROWS),
        in_specs=[_const_spec((2 * FS1, FS1)), blk],
        out_specs=[blk, blk],
        compiler_params=_cparams(("parallel", "parallel")),
        name="fourier_stage1",
    )(w1, f3)


def _fourier_stage2_call(a_re, a_im, f, tabs, after):
    _, ta, tb, wc = tabs
    nk = FS1 // F2_K1
    aspec = pl.BlockSpec((F2_K1, FS2, FOURIER_W), lambda b, k1: (b * nk + k1, 0, 0))
    tspec = pl.BlockSpec((F2_K1, 2 * FS2, FS2), lambda b, k1: (k1, 0, 0))
    h_t = pl.pallas_call(
        _f2_kernel,
        out_shape=jax.ShapeDtypeStruct((B, FS1, FS2, 2 * FOURIER_W), BF16),
        grid=(B, nk),
        in_specs=[tspec, tspec, aspec, aspec, pl.BlockSpec(memory_space=pl.ANY)],
        out_specs=pl.BlockSpec((None, F2_K1, FS2, 2 * FOURIER_W), lambda b, k1: (b, k1, 0, 0)),
        compiler_params=_cparams(("parallel", "parallel")),
        name="fourier_stage2",
    )(ta, tb, a_re, a_im, after)
    h_lat = jnp.transpose(h_t, (0, 2, 1, 3)).reshape(N_LAT, 2 * FOURIER_W)
    h_ctx = pl.pallas_call(
        _fc_kernel,
        out_shape=jax.ShapeDtypeStruct((N_CTX, 2 * FOURIER_W), BF16),
        grid=(B,),
        in_specs=[_const_spec((2 * C, C)),
                  pl.BlockSpec((C, FOURIER_W), lambda b: (N_LAT // C + b, 0))],
        out_specs=pl.BlockSpec((C, 2 * FOURIER_W), lambda b: (b, 0)),
        compiler_params=_cparams(("parallel",)),
        name="fourier_context",
    )(wc, f)
    return h_lat, h_ctx


def _poolconv_kernel(pc_ref, pp_ref, pn_ref, uc_ref, up_ref, un_ref, bm_ref, bh_ref, ic_ref, pw_ref, ps_ref,
                     sh_ref, cw_ref, cb_ref, cg_ref, cnb_ref, z_ref, a_ref, win0_ref, win1_ref, cv_ref):
    t = pl.program_id(0)
    lat_tiles = N_LAT // TP
    per_seq = S // TP
    is_ctx = t >= lat_tiles
    first = jnp.logical_or(t % per_seq == 0, is_ctx)
    last = jnp.logical_or(t % per_seq == per_seq - 1, is_ctx)

    keep_prev = jnp.where(first, 0.0, 1.0)
    keep_next = jnp.where(last, 0.0, 1.0)

    ub = jnp.concatenate([(up_ref[...].astype(F32) * keep_prev).astype(BF16), uc_ref[...],
                          (un_ref[...].astype(F32) * keep_next).astype(BF16)], axis=0)
    off = HALO - CONV_K // 2
    half_rows = TP // 2
    wins = (win0_ref, win1_ref)
    for hf, win_ref in enumerate(wins):
        window = ub[hf * half_rows:hf * half_rows + CONV_WIN]
        win_ref[0] = window.astype(F32)
        for s in range(1, 8):
            win_ref[s] = _dot(sh_ref[s - 1], window)

    pcur = pc_ref[...]
    halo = jnp.concatenate([pp_ref[...].astype(F32) * keep_prev,
                            pn_ref[...].astype(F32) * keep_next], axis=0).astype(BF16)
    sums = []
    for gi in range(len(POOL_WINDOWS)):
        cs = slice(gi * LANES, (gi + 2) * LANES)
        sums.append(_dot(bm_ref[gi], pcur[:, cs]) + _dot(bh_ref[gi], halo[:, cs]))
    lane_t = lax.broadcasted_iota(jnp.int32, (TP, LANES), 1)
    tiles = [sums[0][:, :LANES]]
    for gi in range(1, len(POOL_WINDOWS)):
        split = gi * POOL_GROUP_W - gi * LANES
        tiles.append(jnp.where(lane_t < split, sums[gi - 1][:, LANES:], sums[gi][:, :LANES]))
    tiles.append(sums[-1][:, LANES:])
    zsum = jnp.concatenate(tiles, axis=1)
    z = zsum * ic_ref[...] - pcur.astype(F32)
    z_ref[...] = (_dot(z.astype(BF16), pw_ref[...]) * ps_ref[...]).astype(BF16)

    for hf, win_ref in enumerate(wins):
        base = hf * half_rows
        for cb in range(CONV_W // LANES):
            cs = slice(cb * LANES, (cb + 1) * LANES)
            acc = jnp.zeros((half_rows, LANES), F32) + cb_ref[:, cs]
            for j in range(CONV_K):
                s, m = (off + j) % 8, (off + j) // 8
                acc = acc + win_ref[s, 8 * m:8 * m + half_rows, cs] * cw_ref[j:j + 1, cs]
            cv_ref[base:base + half_rows, cs] = acc
    cv = cv_ref[...]
    mu = jnp.mean(cv, axis=-1, keepdims=True)
    var = jnp.mean(jnp.square(cv - mu), axis=-1, keepdims=True)
    un = (cv - mu) * lax.rsqrt(var + EPS) * cg_ref[...] + cnb_ref[...]
    a_ref[...] = (un * _sigmoid(un)).astype(BF16)


def _poolconv_call(p, u, band_main, band_halo, inv_cnt, pw_bd, pool_scale, shifts, conv_w, conv_b, cn_g, cn_b):
    nh = TP // HALO
    last_h = N_TOK // HALO - 1
    cur = lambda w: pl.BlockSpec((TP, w), lambda t: (t, 0))
    prv = lambda w: pl.BlockSpec((HALO, w), lambda t: (jnp.maximum(t * nh - 1, 0), 0))
    nxt = lambda w: pl.BlockSpec((HALO, w), lambda t: (jnp.minimum((t + 1) * nh, last_h), 0))
    per_seq = S // TP

    def kind(t):
        return jnp.where(t >= N_LAT // TP, 3, jnp.where(t % per_seq == 0, 1, jnp.where(t % per_seq == per_seq - 1, 2, 0)))

    return pl.pallas_call(
        _poolconv_kernel,
        out_shape=[jax.ShapeDtypeStruct((N_TOK, POOL_W), BF16),
                   jax.ShapeDtypeStruct((N_TOK, CONV_W), BF16)],
        grid=(NTP,),
        in_specs=[cur(POOL_W), prv(POOL_W), nxt(POOL_W), cur(CONV_W), prv(CONV_W), nxt(CONV_W),
                  _const_spec((4, TP, TP)), _const_spec((4, TP, 2 * HALO)),
                  pl.BlockSpec((None, TP, POOL_W), lambda t: (kind(t), 0, 0)),
                  _const_spec((POOL_W, POOL_W)), _const_spec((1, POOL_W)),
                  _const_spec((7, CONV_WIN, CONV_WIN)),
                  _const_spec((CONV_K, CONV_W)), _const_spec((1, CONV_W)),
                  _const_spec((1, CONV_W)), _const_spec((1, CONV_W))],
        out_specs=[cur(POOL_W), cur(CONV_W)],
        scratch_shapes=[pltpu.VMEM((8, CONV_WIN, CONV_W), F32), pltpu.VMEM((8, CONV_WIN, CONV_W), F32),
                        pltpu.VMEM((TP, CONV_W), F32)],
        compiler_params=_cparams(("parallel",)),
        name="pool_conv",
    )(p, p, p, u, u, u, band_main, band_halo, inv_cnt, pw_bd, pool_scale, shifts, conv_w, conv_b, cn_g, cn_b)


def _route(logits):
    lane = lax.broadcasted_iota(jnp.int32, logits.shape, 1)
    big = jnp.int32(LANES)
    lg = jnp.where(lane < N_GROUPS, logits, NEG_INF)
    mg = jnp.max(lg, axis=-1, keepdims=True)
    grp = jnp.min(jnp.where(lg == mg, lane, big), axis=-1, keepdims=True)
    p_grp = 1.0 / jnp.sum(jnp.exp(lg - mg), axis=-1, keepdims=True)
    lo = N_GROUPS + grp * EPG
    le = jnp.where((lane >= lo) & (lane < lo + EPG), logits, NEG_INF)
    m1 = jnp.max(le, axis=-1, keepdims=True)
    i1 = jnp.min(jnp.where(le == m1, lane, big), axis=-1, keepdims=True)
    le2 = jnp.where(lane == i1, NEG_INF, le)
    m2 = jnp.max(le2, axis=-1, keepdims=True)
    i2 = jnp.min(jnp.where(le2 == m2, lane, big), axis=-1, keepdims=True)
    r = jnp.exp(m2 - m1)
    w1 = p_grp / (1.0 + r)
    w2 = p_grp * r / (1.0 + r)
    e1 = (i1 - N_GROUPS).astype(F32)
    e2 = (i2 - N_GROUPS).astype(F32)
    return jnp.where(lane == 0, e1, jnp.where(lane == 1, e2, jnp.where(lane == 2, w1,
                     jnp.where(lane == 3, w2, 0.0))))


def _mix_kernel(xl_ref, xc_ref, mod_ref, gn_ref, al_ref, ac_ref, hl_ref, hc_ref, z_ref, cv_ref,
                wa_ref, wf_ref, wp_ref, wc_ref, wg_ref, bg_ref, wo_ref, gf_ref, rh_ref, rl_ref, rb_ref,
                xo_ref, h2_ref, rt_ref, *, split):
    is_ctx = pl.program_id(0) >= LAT_TILES
    m = mod_ref[...]
    x = _stream_tile(xl_ref, xc_ref, split)
    hb = _modulate(x, gn_ref[...], m[0:1], m[1:2]).astype(BF16)
    attn = jnp.where(is_ctx, ac_ref[...], al_ref[...])
    four = jnp.where(is_ctx, hc_ref[...], hl_ref[...])
    branches = ((attn, wa_ref), (four, wf_ref), (z_ref[...], wp_ref), (cv_ref[...], wc_ref))
    acc = None
    for bi, (inp, w_ref) in enumerate(branches):
        cs = slice(bi * D, (bi + 1) * D)
        gate = _sigmoid(_dot(hb, wg_ref[:, cs]) + bg_ref[:, cs])
        term = gate * _dot(inp, w_ref[...])
        acc = term if acc is None else acc + term
    x_new = x + m[2:3] * _dot(acc.astype(BF16), wo_ref[...])
    xo_ref[...] = x_new
    h2 = _modulate(x_new, gf_ref[...], m[3:4], m[4:5])
    hi = h2.astype(BF16)
    h2_ref[...] = hi
    lo = (h2 - hi.astype(F32)).astype(BF16)
    logits = _dot(hi, rh_ref[...]) + _dot(lo, rh_ref[...]) + _dot(hi, rl_ref[...]) + rb_ref[...]
    rt_ref[...] = _route(logits)


def _mix_call(x_lat, x_ctx, split, mods_l, gn, a_lat, a_ctx, h_lat, h_ctx, zc, cact, l, stacked, small):
    tok = lambda w: pl.BlockSpec((TM, w), lambda i: (i, 0))
    lat = lambda w: pl.BlockSpec((TM, w), lambda i: (jnp.minimum(i, LAT_TILES - 1), 0))
    wa, wf, wp, wc, wg, wo, rh, rl = stacked
    bg, gf, rb = small
    in_specs = _stream_specs(split) + [
                pl.BlockSpec((None, 6, D), lambda i: (_mod_row(i), 0, 0)), _const_spec((1, D)),
                lat(Q_W), _const_spec((N_CTX, Q_W)),
                lat(2 * FOURIER_W), _const_spec((N_CTX, 2 * FOURIER_W)),
                tok(POOL_W), tok(CONV_W)]
    in_specs += [_layer_spec(w.shape[1:], l) for w in (wa, wf, wp, wc, wg)]
    in_specs += [_const_spec(bg.shape), _layer_spec(wo.shape[1:], l), _const_spec(gf.shape),
                 _layer_spec(rh.shape[1:], l), _layer_spec(rl.shape[1:], l), _const_spec(rb.shape)]
    return pl.pallas_call(
        functools.partial(_mix_kernel, split=split),
        out_shape=[jax.ShapeDtypeStruct((N_TOK, D), F32), jax.ShapeDtypeStruct((N_TOK, D), BF16),
                   jax.ShapeDtypeStruct((N_TOK, LANES), F32)],
        grid=(NT,),
        in_specs=in_specs,
        out_specs=[tok(D), tok(D), tok(LANES)],
        compiler_params=_cparams(("parallel",)),
        name="mix",
    )(x_lat, x_ctx, mods_l, gn, a_lat, a_ctx, h_lat, h_ctx, zc, cact, wa, wf, wp, wc, wg, bg, wo, gf, rh, rl, rb)


def _onehots(route):
    lane = lax.broadcasted_iota(jnp.int32, route.shape, 1)
    e1 = route[:, 0:1].astype(jnp.int32)
    e2 = route[:, 1:2].astype(jnp.int32)
    return (lane == e1).astype(F32), (lane == e2).astype(F32)


def _lane_cumsum(row):
    lane = lax.broadcasted_iota(jnp.int32, row.shape, 1)
    sh = 1
    while sh < N_EXPERTS:
        row = row + jnp.where(lane >= sh, pltpu.roll(row, sh, 1), 0.0)
        sh *= 2
    return row


def _rank_kernel(rt_ref, tri_ref, pos_ref, meta_ref, cnt_ref, carry_ref):
    @pl.when(pl.program_id(0) == 0)
    def _():
        carry_ref[...] = jnp.zeros_like(carry_ref)

    tiles = []
    for t in range(RANK_TILES):
        oh1, oh2 = _onehots(rt_ref[t * TM:(t + 1) * TM, :])
        both = oh1 + oh2
        tiles.append((oh1, oh2, both, _dot(tri_ref[...], both.astype(BF16))))
    carry = carry_ref[0:1, :]
    lane = lax.broadcasted_iota(jnp.int32, (TM, LANES), 1)
    row = lax.broadcasted_iota(jnp.int32, meta_ref.shape[1:], 0)
    for t, (oh1, oh2, both, before) in enumerate(tiles):
        tile_cnt = jnp.sum(both, axis=0, keepdims=True)
        tile_cnt = tile_cnt + (tile_cnt - 2.0 * jnp.floor(tile_cnt * 0.5))
        tile_off = _lane_cumsum(tile_cnt) - tile_cnt
        where = before + tile_off
        p1 = jnp.sum(oh1 * where, axis=-1, keepdims=True)
        p2 = jnp.sum(oh2 * where, axis=-1, keepdims=True)
        pos_ref[t * TM:(t + 1) * TM, :] = jnp.where(lane == 0, p1, jnp.where(lane == 1, p2, 0.0))
        meta_ref[t] = jnp.where(row == 0, tile_off, jnp.where(row == 1, tile_cnt, jnp.where(row == 2, carry, 0.0)))
        carry = carry + tile_cnt
    carry_ref[...] = jnp.broadcast_to(carry, carry_ref.shape)
    cnt_ref[...] = jnp.broadcast_to(carry, cnt_ref.shape)


def _runs_kernel(meta_ref, cnt_ref, runs_ref, be_ref):
    lane = lax.broadcasted_iota(jnp.int32, (1, LANES), 1)
    counts = cnt_ref[0:1, :]
    padded = jnp.floor((counts + (MOE_BLK - 1)) * (1.0 / MOE_BLK)) * MOE_BLK
    padded = jnp.where(lane < N_EXPERTS, padded, 0.0)
    ends = _lane_cumsum(padded)
    starts = ends - padded
    for t in range(NT):
        m = meta_ref[t]
        row = lax.broadcasted_iota(jnp.int32, m.shape, 0)
        m = jnp.where(row == 2, m + starts, m)
        m = jnp.where(row == 3, starts + counts, jnp.where(row == 4, padded - counts, m))
        runs_ref[t] = m.astype(jnp.int32)
    blk = lax.broadcasted_iota(jnp.int32, be_ref.shape, 0).astype(F32) * MOE_BLK
    lane_b = lax.broadcasted_iota(jnp.int32, be_ref.shape, 1)
    done = jnp.where((ends <= blk) & (lane_b < N_EXPERTS), 1.0, 0.0)
    be = jnp.minimum(jnp.sum(done, axis=-1, keepdims=True), N_EXPERTS - 1.0)
    nblk = jnp.max(jnp.where(lane_b == N_EXPERTS - 1, ends, 0.0), axis=-1, keepdims=True) * (1.0 / MOE_BLK)
    last_row = jnp.sum(jnp.where(lane_b.astype(F32) == be, starts + counts, 0.0), axis=-1, keepdims=True)
    live = jnp.clip(last_row - blk[:, 0:1], 0.0, MOE_BLK)
    be_ref[...] = jnp.where(lane_b == 0, be, jnp.where(lane_b == 1, nblk,
                            jnp.where(lane_b == 2, live, 0.0))).astype(jnp.int32)


def _plan_call(route, tri):
    tok = pl.BlockSpec((RANK_TILES * TM, LANES), lambda i: (i, 0))
    pos, meta, counts = pl.pallas_call(
        _rank_kernel,
        out_shape=[jax.ShapeDtypeStruct((N_TOK, LANES), F32), jax.ShapeDtypeStruct((NT, 8, LANES), F32),
                   jax.ShapeDtypeStruct((8, LANES), F32)],
        grid=(NT // RANK_TILES,),
        in_specs=[tok, _const_spec((TM, TM))],
        out_specs=[tok, pl.BlockSpec((RANK_TILES, 8, LANES), lambda i: (i, 0, 0)),
                   pl.BlockSpec((8, LANES), lambda i: (0, 0))],
        scratch_shapes=[pltpu.VMEM((8, LANES), F32)],
        compiler_params=_cparams(("arbitrary",)),
        name="moe_rank",
    )(route, tri)
    runs, blk = pl.pallas_call(
        _runs_kernel,
        out_shape=[jax.ShapeDtypeStruct((NT, 8, LANES), jnp.int32),
                   jax.ShapeDtypeStruct((BLK_TABLE_ROWS, LANES), jnp.int32)],
        name="moe_runs",
    )(meta, counts)
    runs_flat = runs[:, 0:RUN_FIELDS, 0:N_EXPERTS].reshape(NT, 1, RUN_FIELDS * N_EXPERTS)
    return pos, runs_flat, (blk[:N_MOE_BLOCKS, 0], blk[0:1, 1], blk[:N_MOE_BLOCKS, 2])


def _pack_pairs(x):
    half = x.shape[1] // 2
    lo = pltpu.bitcast(x[:, :half], jnp.uint32)
    hi = pltpu.bitcast(x[:, half:], jnp.uint32)
    return (lo >> 16) | (hi & jnp.uint32(0xFFFF0000))


def _unpack_pairs(w):
    lo = pltpu.bitcast(w << 16, F32)
    hi = pltpu.bitcast(w & jnp.uint32(0xFFFF0000), F32)
    return jnp.concatenate([lo, hi], axis=1).astype(BF16)


def _run_fields(runs_ref, e):
    return runs_ref[0, e], runs_ref[0, N_EXPERTS + e], runs_ref[0, 2 * N_EXPERTS + e]


def _store_rows(lin_ref, packed):
    rows = packed.shape[0]
    for c in range(PK):
        lin_ref[pl.ds(c, rows, stride=PK), :] = packed[:, c * LANES:(c + 1) * LANES]


def _load_rows(lin_ref, rows):
    return jnp.concatenate([lin_ref[pl.ds(c, rows, stride=PK), :] for c in range(PK)], axis=1)


def _lin(ref, row, nrows):
    return ref.at[pl.ds(pl.multiple_of(row * PK, SUBLANES), nrows * PK), :]


FETCH_ROWS = 64
RUN_PIECES = (32, 16, 8, 4, 2)
TAIL_PIECES = (256, 128, 64, 32, 16, 8, 4, 2)


def _run_copies(runs, make_copy, act):
    def per_expert(e, carry):
        off, n, dst = _run_fields(runs, e)
        whole = n // FETCH_ROWS

        def chunk(k, c):
            act(make_copy(off + k * FETCH_ROWS, dst + k * FETCH_ROWS, FETCH_ROWS))
            return c

        lax.fori_loop(0, whole, chunk, 0)
        done = whole * FETCH_ROWS
        for size in RUN_PIECES:
            @pl.when((n & size) != 0)
            def _(done=done, size=size):
                act(make_copy(off + done, dst + done, size))
            done = done + (n & size)
        return carry

    lax.fori_loop(0, N_EXPERTS, per_expert, 0)


def _rows_wait(runs, make_copy):
    total = lax.fori_loop(0, N_EXPERTS, lambda e, acc: acc + runs[0, N_EXPERTS + e], jnp.int32(0))
    lax.fori_loop(0, total // FETCH_ROWS, lambda k, c: (make_copy(0, 0, FETCH_ROWS).wait(), c)[1], 0)
    for size in RUN_PIECES:
        @pl.when((total & size) != 0)
        def _(size=size):
            make_copy(0, 0, size).wait()


def _dispatch_kernel(runs_ref, prev_runs_ref, pos_ref, h2_ref, xs_ref, buf0, buf1, zero_ref, sem0, sem1, zsem,
                     ssem):
    i = pl.program_id(0)
    last = NT - 1

    def tail_copies(act):
        def per_expert(e, carry):
            row = runs_ref[0, 3 * N_EXPERTS + e]
            n = runs_ref[0, 4 * N_EXPERTS + e]
            done = jnp.int32(0)
            for size in TAIL_PIECES:
                @pl.when((n & size) != 0)
                def _(done=done, size=size):
                    act(pltpu.make_async_copy(zero_ref.at[pl.ds(0, size * PK), :], _lin(xs_ref, row + done, size), zsem))
                done = done + (n & size)
            return carry

        lax.fori_loop(0, N_EXPERTS, per_expert, 0)

    def spare_copies(act):
        used = runs_ref[0, 3 * N_EXPERTS + N_EXPERTS - 1] + runs_ref[0, 4 * N_EXPERTS + N_EXPERTS - 1]

        def spare_block(k, carry):
            act(pltpu.make_async_copy(zero_ref.at[pl.ds(0, MOE_BLK * PK), :],
                                      _lin(xs_ref, used + k * MOE_BLK, MOE_BLK), ssem))
            return carry

        lax.fori_loop(0, N_MOE_BLOCKS - used // MOE_BLK, spare_block, 0)

    start = lambda d: d.start()
    wait = lambda d: d.wait()

    @pl.when(i == 0)
    def _():
        zero_ref[...] = jnp.zeros_like(zero_ref)
        tail_copies(start)
        spare_copies(start)

    pos = pos_ref[...]
    col = lax.broadcasted_iota(jnp.int32, (TM, TS), 1).astype(F32)
    sel = jnp.where((col == pos[:, 0:1]) | (col == pos[:, 1:2]), 1.0, 0.0).astype(BF16)
    srt = lax.dot_general(sel, h2_ref[...], (((0,), (0,)), ((), ())), preferred_element_type=F32)
    packed = _pack_pairs(srt)

    for par, (buf, sem, obuf, osem) in enumerate(((buf0, sem0, buf1, sem1), (buf1, sem1, buf0, sem0))):
        @pl.when(i % 2 == par)
        def _(buf=buf, sem=sem, obuf=obuf, osem=osem):
            copy = lambda off, dst, rows: pltpu.make_async_copy(_lin(buf, off, rows), _lin(xs_ref, dst, rows), sem)
            ocopy = lambda off, dst, rows: pltpu.make_async_copy(_lin(obuf, off, rows), _lin(xs_ref, dst, rows), osem)
            _store_rows(buf, packed)

            @pl.when(i == 0)
            def _():
                tail_copies(wait)

            _run_copies(runs_ref, copy, start)

            @pl.when(i > 0)
            def _():
                _rows_wait(prev_runs_ref, ocopy)

            @pl.when(i == last)
            def _():
                _rows_wait(runs_ref, copy)
                spare_copies(wait)


def _dispatch_call(runs_flat, pos, h2):
    runs_spec = lambda shift: pl.BlockSpec((None, 1, RUN_FIELDS * N_EXPERTS), lambda i: (jnp.maximum(i - shift, 0), 0, 0),
                                           memory_space=pltpu.SMEM)
    sorted_buf = pltpu.VMEM((TS * PK, LANES), jnp.uint32)
    return pl.pallas_call(
        _dispatch_kernel,
        out_shape=jax.ShapeDtypeStruct((N_SLOTS * PK, LANES), jnp.uint32),
        grid=(NT,),
        in_specs=[runs_spec(0), runs_spec(1),
                  pl.BlockSpec((TM, LANES), lambda i: (i, 0)),
                  pl.BlockSpec((TM, D), lambda i: (i, 0))],
        out_specs=pl.BlockSpec(memory_space=pl.ANY),
        scratch_shapes=[sorted_buf, sorted_buf, pltpu.VMEM((MOE_BLK * PK, LANES), jnp.uint32),
                        pltpu.SemaphoreType.DMA, pltpu.SemaphoreType.DMA, pltpu.SemaphoreType.DMA,
                        pltpu.SemaphoreType.DMA],
        compiler_params=_cparams(("arbitrary",)),
        name="moe_dispatch",
    )(runs_flat, runs_flat, pos, h2)


def _expert_kernel(be_ref, nu_ref, live_ref, xs_ref, wg_ref, wu_ref, wd_ref, ys_ref, wgb_ref, wub_ref, wdb_ref):
    del nu_ref
    b = pl.program_id(0)
    live = live_ref[b]
    half = MOE_BLK // 2

    @pl.when(jnp.logical_or(b == 0, be_ref[b] != be_ref[jnp.maximum(b - 1, 0)]))
    def _():
        wgb_ref[...] = wg_ref[...].astype(BF16)
        wub_ref[...] = wu_ref[...].astype(BF16)
        wdb_ref[...] = wd_ref[...].astype(BF16)

    def swiglu(rows):
        xb = _unpack_pairs(_load_rows(xs_ref, rows))
        g = _dot(xb, wgb_ref[...])
        u = _dot(xb, wub_ref[...])
        hmid = (g * _sigmoid(g)) * u
        y = _dot(hmid.astype(BF16), wdb_ref[...])
        _store_rows(ys_ref, _pack_pairs(y.astype(BF16).astype(F32)))

    @pl.when(live > half)
    def _():
        swiglu(MOE_BLK)

    @pl.when(jnp.logical_and(live > 0, live <= half))
    def _():
        swiglu(half)
        ys_ref[half * PK:, :] = jnp.zeros((half * PK, LANES), jnp.uint32)

    @pl.when(live <= 0)
    def _():
        ys_ref[...] = jnp.zeros_like(ys_ref)


def _expert_call(blocks, xs, wg, wu, wd, l):
    blk_e, n_used, live = blocks
    wspec = lambda k, n: pl.BlockSpec((None, None, k, n), lambda b, be, nu, lv: (l, be[b], 0, 0))
    return pl.pallas_call(
        _expert_kernel,
        out_shape=jax.ShapeDtypeStruct((N_SLOTS * PK, LANES), jnp.uint32),
        grid_spec=pltpu.PrefetchScalarGridSpec(
            num_scalar_prefetch=3,
            grid=(N_MOE_BLOCKS,),
            in_specs=[pl.BlockSpec((MOE_BLK * PK, LANES), lambda b, be, nu, lv: (jnp.minimum(b, nu[0] - 1), 0)),
                      wspec(D, EXPERT_HIDDEN), wspec(D, EXPERT_HIDDEN), wspec(EXPERT_HIDDEN, D)],
            out_specs=pl.BlockSpec((MOE_BLK * PK, LANES), lambda b, be, nu, lv: (b, 0)),
            scratch_shapes=[pltpu.VMEM((D, EXPERT_HIDDEN), BF16), pltpu.VMEM((D, EXPERT_HIDDEN), BF16),
                            pltpu.VMEM((EXPERT_HIDDEN, D), BF16)]),
        compiler_params=_cparams(("arbitrary",)),
        name="moe_experts",
    )(blk_e, n_used, live, xs, wg, wu, wd)


def _combine_kernel(runs_ref, next_runs_ref, ys_ref, pos_ref, x_ref, rt_ref, mod_ref, o_ref, buf0, buf1, sem0, sem1,
                    *, n_tiles):
    i = pl.program_id(0)
    last = n_tiles - 1
    start = lambda d: d.start()
    fetch = lambda buf, sem: (
        lambda off, dst, rows: pltpu.make_async_copy(_lin(ys_ref, dst, rows), _lin(buf, off, rows), sem))

    @pl.when(i == 0)
    def _():
        buf0[...] = jnp.zeros_like(buf0)
        buf1[...] = jnp.zeros_like(buf1)
        _run_copies(runs_ref, fetch(buf0, sem0), start)

    pos = pos_ref[...]
    rt = rt_ref[...]
    col = lax.broadcasted_iota(jnp.int32, (TM, TS), 1).astype(F32)
    pick = (jnp.where(col == pos[:, 0:1], rt[:, TOP_K:TOP_K + 1], 0.0)
            + jnp.where(col == pos[:, 1:2], rt[:, TOP_K + 1:TOP_K + 2], 0.0)).astype(BF16)

    for par, (buf, sem, obuf, osem) in enumerate(((buf0, sem0, buf1, sem1), (buf1, sem1, buf0, sem0))):
        @pl.when(i % 2 == par)
        def _(buf=buf, sem=sem, obuf=obuf, osem=osem):
            @pl.when(i < last)
            def _():
                _run_copies(next_runs_ref, fetch(obuf, osem), start)

            _rows_wait(runs_ref, fetch(buf, sem))
            ysb = _unpack_pairs(_load_rows(buf, TS))
            o_ref[...] = x_ref[...] + mod_ref[5:6, :] * _dot(pick, ysb)


def _combine_call(runs_flat, ys, pos, x, route, mods_l, n_tiles):
    tok = lambda w: pl.BlockSpec((TM, w), lambda i: (i, 0))
    runs_spec = lambda shift: pl.BlockSpec((None, 1, RUN_FIELDS * N_EXPERTS),
                                           lambda i: (jnp.minimum(i + shift, n_tiles - 1), 0, 0),
                                           memory_space=pltpu.SMEM)
    sorted_buf = pltpu.VMEM((TS * PK, LANES), jnp.uint32)
    return pl.pallas_call(
        functools.partial(_combine_kernel, n_tiles=n_tiles),
        out_shape=jax.ShapeDtypeStruct((n_tiles * TM, D), F32),
        grid=(n_tiles,),
        in_specs=[runs_spec(0), runs_spec(1),
                  pl.BlockSpec(memory_space=pl.ANY),
                  tok(LANES), tok(D), tok(LANES),
                  pl.BlockSpec((None, 6, D), lambda i: (_mod_row(i), 0, 0))],
        out_specs=tok(D),
        scratch_shapes=[sorted_buf, sorted_buf, pltpu.SemaphoreType.DMA, pltpu.SemaphoreType.DMA],
        compiler_params=_cparams(("arbitrary",)),
        name="moe_combine",
    )(runs_flat, runs_flat, ys, pos, x, route, mods_l)


def _rope_tables():
    nf = HEAD_DIM // 4
    inv = ROPE_BASE ** (-jnp.arange(nf, dtype=F32) / nf)
    t = jnp.arange(S)
    row = (t // GRID_W).astype(F32)[:, None] * inv[None, :]
    col = (t % GRID_W).astype(F32)[:, None] * inv[None, :]
    zero = jnp.zeros_like(row)
    cos = jnp.concatenate([jnp.cos(row), jnp.cos(row), jnp.cos(col), jnp.cos(col)], axis=1)
    sa = jnp.concatenate([-jnp.sin(row), zero, -jnp.sin(col), zero], axis=1)
    sb = jnp.concatenate([zero, jnp.sin(row), zero, jnp.sin(col)], axis=1)
    ident = (jnp.ones((TM, HEAD_DIM), F32), jnp.zeros((TM, HEAD_DIM), F32), jnp.zeros((TM, HEAD_DIM), F32))
    return tuple(jnp.tile(jnp.concatenate([a, b], axis=0), (1, LANES // HEAD_DIM))
                 for a, b in zip((cos, sa, sb), ident))


def _fourier_tables():
    s1 = np.arange(FS1)
    ang1 = 2.0 * np.pi * np.outer(s1, s1) / FS1
    w1 = np.concatenate([np.cos(ang1), -np.sin(ang1)], axis=0) / np.sqrt(S)
    k1 = np.arange(FS1)[:, None, None]
    k2 = np.arange(FS2)[None, :, None]
    s2 = np.arange(FS2)[None, None, :]
    ang2 = 2.0 * np.pi * ((k1 + FS1 * k2) * s2 % S) / S
    c2, sn2 = np.cos(ang2), np.sin(ang2)
    ta = np.concatenate([c2, -sn2], axis=1)
    tb = np.concatenate([sn2, c2], axis=1)
    sc = np.arange(C)
    angc = 2.0 * np.pi * np.outer(sc, sc) / C
    wc = np.concatenate([np.cos(angc), -np.sin(angc)], axis=0) / np.sqrt(C)
    return tuple(jnp.asarray(a, F32).astype(BF16) for a in (w1, ta, tb, wc))


def _channel_dft():
    cidx = np.arange(FOURIER_GROUP_W)
    ang = 2.0 * np.pi * np.outer(cidx, cidx) / FOURIER_GROUP_W
    eye = np.eye(FOURIER_W // FOURIER_GROUP_W)
    cw = np.kron(eye, np.cos(ang)) / np.sqrt(FOURIER_GROUP_W)
    sw = np.kron(eye, np.sin(ang)) / np.sqrt(FOURIER_GROUP_W)
    return jnp.asarray(np.concatenate([cw, sw], axis=0), F32)


def _pool_bands():
    t = np.arange(TP)[:, None]
    main, halo = [], []
    for w in POOL_WINDOWS:
        def hit(j):
            return ((j - t >= -(w // 2)) & (j - t <= w // 2 - 1)).astype(np.float32)
        main.append(hit(np.arange(TP)[None, :]))
        halo.append(np.concatenate([hit(np.arange(-HALO, 0)[None, :]),
                                    hit(np.arange(TP, TP + HALO)[None, :])], axis=1))
    return (jnp.asarray(np.stack(main), F32).astype(BF16), jnp.asarray(np.stack(halo), F32).astype(BF16))


def _pool_inv_counts():
    win = np.repeat(np.array(POOL_WINDOWS), POOL_GROUP_W)[None, :]

    def table(pos0, seq_len):
        pos = (pos0 + np.arange(TP))[:, None]
        lo = np.clip(pos - win // 2, 0, seq_len)
        hi = np.clip(pos - win // 2 + win, 0, seq_len)
        return 1.0 / (hi - lo)

    tabs = [table(TP, S), table(0, S), table(S - TP, S), table(0, C)]
    return jnp.asarray(np.stack(tabs), F32)


def _conv_shifts():
    i = np.arange(CONV_WIN)
    return jnp.asarray(np.stack([(i[None, :] == i[:, None] + s) for s in range(1, 8)]), F32).astype(BF16)


def _fold_kernel(a_ref, b_ref, o_ref):
    a, b = a_ref[...], b_ref[...]
    a_hi, b_hi = a.astype(BF16), b.astype(BF16)
    a_lo = (a - a_hi.astype(F32)).astype(BF16)
    b_lo = (b - b_hi.astype(F32)).astype(BF16)
    o_ref[...] = (_dot(a_hi, b_hi) + _dot(a_lo, b_hi) + _dot(a_hi, b_lo)).astype(BF16)


def _fold_fourier_weights(dftw, w_br_fourier):
    nl = w_br_fourier.shape[0]
    return pl.pallas_call(
        _fold_kernel,
        out_shape=jax.ShapeDtypeStruct((nl, 2 * FOURIER_W, D), BF16),
        grid=(nl,),
        in_specs=[pl.BlockSpec((2 * FOURIER_W, FOURIER_W), lambda l: (0, 0)),
                  pl.BlockSpec((None, FOURIER_W, D), lambda l: (l, 0, 0))],
        out_specs=pl.BlockSpec((None, 2 * FOURIER_W, D), lambda l: (l, 0, 0)),
        compiler_params=_cparams(("arbitrary",)),
        name="fold_fourier_proj",
    )(dftw, w_br_fourier)


def _block_diag(blocks):
    n, r, c = blocks.shape
    eye = jnp.eye(n, dtype=blocks.dtype)
    return (blocks[:, :, None, :] * eye[:, None, :, None]).reshape(n * r, n * c)


def kernel(x, c, ctx, c_ctx, w_ada, b_ada, g_norm_mix, g_norm_ffn, w_in, g_q, g_k, sink, w_br_attn,
           w_br_fourier, pool_w, pool_scale, w_br_pool, conv_w, conv_b, cn_g, cn_b, w_br_conv, w_gate,
           b_gate, w_out, w_router_grp, b_router_grp, w_router_exp, b_router_exp, w_e_gate, w_e_up,
           w_e_down):
    x_lat, x_ctx, split = x.reshape(N_LAT, D), ctx.reshape(N_CTX, D), True
    nl = w_ada.shape[0]
    mods = _ada_all(c, c_ctx, w_ada, b_ada).reshape(nl, 8, 6, D)
    rope_tabs = _rope_tables()
    four_tabs = _fourier_tables()
    band_main, band_halo = _pool_bands()
    inv_cnt = _pool_inv_counts()
    shifts = _conv_shifts()
    wf_all = _fold_fourier_weights(_channel_dft(), w_br_fourier)
    bd = jnp.asarray(np.kron(np.eye(LANES // HEAD_DIM), np.ones((HEAD_DIM, HEAD_DIM))), F32).astype(BF16)
    tri = jnp.asarray(np.tril(np.ones((TM, TM)), -1), F32).astype(BF16)
    rpad = jnp.zeros((nl, D, LANES - N_GROUPS - N_EXPERTS), F32)
    w_router = jnp.concatenate([w_router_grp, w_router_exp, rpad], axis=-1)
    r_hi = w_router.astype(BF16)
    r_lo = (w_router - r_hi.astype(F32)).astype(BF16)
    r_b = jnp.concatenate([b_router_grp, b_router_exp, rpad[:, 0, :]], axis=-1).reshape(nl, 1, LANES)
    stacked = tuple(w.astype(BF16) for w in (w_br_attn,)) + (wf_all,) + tuple(
        w.astype(BF16) for w in (w_br_pool, w_br_conv, w_gate, w_out)) + (r_hi, r_lo)

    for l in range(nl):
        mods_l = mods[l]
        gn = g_norm_mix[l].reshape(1, D)
        q, kv, f, p, u = _proj_call(x_lat, x_ctx, split, mods_l, gn, w_in, l, rope_tabs,
                                    jnp.tile(g_q[l], 2).reshape(1, LANES),
                                    jnp.tile(g_k[l], 2).reshape(1, LANES), bd)
        a_re, a_im = _fourier_stage1_call(f, four_tabs)
        a_lat, a_ctx = _attn_call(sink[l], q, kv)
        h_lat, h_ctx = _fourier_stage2_call(a_re, a_im, f, four_tabs, a_lat)
        zc, cact = _poolconv_call(p, u, band_main, band_halo, inv_cnt, _block_diag(pool_w[l]).astype(BF16),
                                  pool_scale[l].reshape(1, POOL_W), shifts, conv_w[l], conv_b[l].reshape(1, CONV_W),
                                  cn_g[l].reshape(1, CONV_W), cn_b[l].reshape(1, CONV_W))
        small = (b_gate[l].reshape(1, 4 * D), g_norm_ffn[l].reshape(1, D), r_b[l])
        xs, h2, route = _mix_call(x_lat, x_ctx, split, mods_l, gn, a_lat, a_ctx, h_lat, h_ctx, zc, cact, l,
                                  stacked, small)
        pos, runs_flat, blocks = _plan_call(route, tri)
        slots = _dispatch_call(runs_flat, pos, h2)
        ys = _expert_call(blocks, slots, w_e_gate, w_e_up, w_e_down, l)
        xs = _combine_call(runs_flat, ys, pos, xs, route, mods_l, LAT_TILES if l == nl - 1 else NT)
        x_lat, x_ctx, split = xs, xs, False
    return xs.reshape(B, S, D)
```

```python
import functools

import numpy as np
import jax
import jax.numpy as jnp
from jax import lax
from jax.experimental import pallas as pl
from jax.experimental.pallas import tpu as pltpu

F32 = jnp.float32
BF16 = jnp.bfloat16

D = 1024
B = 2
S = 8192
C = 256
GRID_W = 64
HEAD_DIM = 64
N_Q_HEADS = 8
N_KV_HEADS = 2
GQA = N_Q_HEADS // N_KV_HEADS
WINDOW = 128
ATTN_BLK = 128
ATTN_QB = 8
ROPE_BASE = 10000.0
Q_W = 512
KV_W = 128
FOURIER_W = 640
FOURIER_GROUP_W = 160
POOL_W = 640
POOL_GROUP_W = 160
POOL_WINDOWS = (2, 4, 8, 16)
CONV_W = 512
CONV_K = 31
PROJ_W = 3072
N_GROUPS = 4
EPG = 8
N_EXPERTS = 32
TOP_K = 2
EXPERT_HIDDEN = 512
MOE_BLK = 512
EPS = 1e-6
NEG_INF = -1e30
LOG2E = 1.4426950408889634

N_LAT = B * S
N_CTX = B * C
N_TOK = N_LAT + N_CTX
TM = 512
NT = N_TOK // TM
LAT_TILES = N_LAT // TM
TILES_PER_BATCH = S // TM
TP = 256
NTP = N_TOK // TP
HALO = 16
CONV_WIN = TP // 2 + 2 * HALO
N_ASSIGN = N_TOK * TOP_K
W_PARTS = 2
RANK_TILES = 3
RUN_FIELDS = 5
PK = D // 2 // 128
TS = 1152
N_MOE_BLOCKS = (N_ASSIGN + NT * N_EXPERTS + N_EXPERTS * (MOE_BLK - 1)) // MOE_BLK
N_SLOTS = N_MOE_BLOCKS * MOE_BLK
SUBLANES = 8
BLK_TABLE_ROWS = -(-N_MOE_BLOCKS // SUBLANES) * SUBLANES
ROPE_PAIR = HEAD_DIM // 4
FS1 = 64
FS2 = 128
F1_ROWS = 32
F2_K1 = 16
LANES = 128
VMEM_LIMIT = 56 * 1024 * 1024


def _cparams(sem, vmem=VMEM_LIMIT):
    return pltpu.CompilerParams(dimension_semantics=sem, vmem_limit_bytes=vmem)


def _const_spec(shape):
    nd = len(shape)
    return pl.BlockSpec(shape, lambda *_: (0,) * nd, pipeline_mode=pl.Buffered(1))


def _dot(a, b):
    return jnp.dot(a, b, preferred_element_type=F32)


def _modulate(x, g, shift, scale):
    y = x * lax.rsqrt(jnp.mean(x * x, axis=-1, keepdims=True) + EPS)
    return (y * g) * (1.0 + scale) + shift


def _sigmoid(x):
    return 1.0 / (1.0 + jnp.exp(-x))


def _ada_kernel(ct_ref, w_ref, b_ref, o_ref):
    ct = ct_ref[...]
    s = ct * _sigmoid(ct)
    w = w_ref[...]
    rows = [jnp.sum(w * s[:, r:r + 1], axis=0, keepdims=True) for r in range(3)]
    rows.append(jnp.zeros((5, w.shape[1]), F32))
    o_ref[...] = jnp.concatenate(rows, axis=0) + b_ref[...]


def _ada_all(c, c_ctx, w_ada, b_ada):
    ct = jnp.concatenate([c, c_ctx[None, :], jnp.zeros((5, D), F32)], axis=0).T
    cols = 1536
    nl = w_ada.shape[0]
    return pl.pallas_call(
        _ada_kernel,
        out_shape=jax.ShapeDtypeStruct((nl, 8, 6 * D), F32),
        grid=(nl, 6 * D // cols),
        in_specs=[pl.BlockSpec((D, 8), lambda l, j: (0, 0)),
                  pl.BlockSpec((None, D, cols), lambda l, j: (l, 0, j)),
                  pl.BlockSpec((None, 1, cols), lambda l, j: (l, 0, j))],
        out_specs=pl.BlockSpec((None, 8, cols), lambda l, j: (l, 0, j)),
        compiler_params=_cparams(("arbitrary", "arbitrary")),
        name="adaln",
    )(ct, w_ada, b_ada.reshape(nl, 1, 6 * D))


def _head_rms(t, g128, bd):
    outs = []
    for j in range(t.shape[1] // LANES):
        blk = t[:, j * LANES:(j + 1) * LANES]
        ss = _dot((blk * blk).astype(BF16), bd)
        outs.append(blk * lax.rsqrt(ss * (1.0 / HEAD_DIM) + EPS) * g128)
    return outs


def _rope(blocks, cos, sa, sb):
    outs = []
    for blk in blocks:
        up = pltpu.roll(blk, LANES - ROPE_PAIR, 1)
        dn = pltpu.roll(blk, ROPE_PAIR, 1)
        outs.append(blk * cos + up * sa + dn * sb)
    return outs


def _proj_kernel(xl_ref, xc_ref, mod_ref, gn_ref, w_ref, cos_ref, sa_ref, sb_ref, gq_ref, gk_ref, bd_ref,
                 q_ref, kv_ref, f_ref, p_ref, u_ref, wbf_ref, *, split):
    @pl.when(pl.program_id(0) == 0)
    def _():
        wbf_ref[...] = w_ref[...].astype(BF16)

    m = mod_ref[...]
    hb = _modulate(_stream_tile(xl_ref, xc_ref, split), gn_ref[...], m[0:1], m[1:2]).astype(BF16)
    cos, sa, sb, bd = cos_ref[...], sa_ref[...], sb_ref[...], bd_ref[...]
    o_kv, o_f, o_a = Q_W, Q_W + 2 * KV_W, Q_W + 2 * KV_W + FOURIER_W + POOL_W
    qkv = _dot(hb, wbf_ref[:, 0:o_f])
    fp = _dot(hb, wbf_ref[:, o_f:o_a])
    q = _rope(_head_rms(qkv[:, 0:Q_W], gq_ref[...], bd), cos, sa, sb)
    q_ref[...] = (jnp.concatenate(q, axis=1) * (LOG2E * HEAD_DIM ** -0.5)).astype(BF16)
    k = _rope(_head_rms(qkv[:, o_kv:o_kv + KV_W], gk_ref[...], bd), cos, sa, sb)
    kv_ref[:, 0:KV_W] = k[0].astype(BF16)
    kv_ref[:, KV_W:2 * KV_W] = qkv[:, o_kv + KV_W:o_f].astype(BF16)
    ag = _dot(hb, wbf_ref[:, o_a:PROJ_W])
    f_ref[...] = fp[:, 0:FOURIER_W].astype(BF16)
    p_ref[...] = fp[:, FOURIER_W:].astype(BF16)
    u_ref[...] = (ag[:, 0:CONV_W] * _sigmoid(ag[:, CONV_W:])).astype(BF16)


def _mod_row(i):
    return jnp.minimum(i // TILES_PER_BATCH, 2)


def _layer_spec(shape, l):
    nd = len(shape)
    return pl.BlockSpec((None,) + tuple(shape), lambda *_: (l,) + (0,) * nd, pipeline_mode=pl.Buffered(1))


def _stream_specs(split):
    first = (lambda i: (jnp.minimum(i, LAT_TILES - 1), 0)) if split else (lambda i: (i, 0))
    return [pl.BlockSpec((TM, D), first),
            pl.BlockSpec((TM, D), lambda i: (0, 0), pipeline_mode=pl.Buffered(1))]


def _stream_tile(xl_ref, xc_ref, split):
    return jnp.where(pl.program_id(0) >= LAT_TILES, xc_ref[...], xl_ref[...]) if split else xl_ref[...]


def _proj_call(x_lat, x_ctx, split, mods_l, gn, w_in, l, rope_tabs, gq128, gk128, bd):
    cos, sa, sb = rope_tabs
    tok = lambda w: pl.BlockSpec((TM, w), lambda i: (i, 0))
    rope_spec = pl.BlockSpec((TM, LANES), lambda i: (jnp.where(i < LAT_TILES, i % TILES_PER_BATCH,
                                                               TILES_PER_BATCH), 0))
    widths = (Q_W, 2 * KV_W, FOURIER_W, POOL_W, CONV_W)
    return pl.pallas_call(
        functools.partial(_proj_kernel, split=split),
        out_shape=[jax.ShapeDtypeStruct((N_TOK, w), BF16) for w in widths],
        grid=(NT,),
        in_specs=_stream_specs(split) + [
                  pl.BlockSpec((None, 6, D), lambda i: (_mod_row(i), 0, 0)),
                  _const_spec((1, D)),
                  _layer_spec((D, PROJ_W), l),
                  rope_spec, rope_spec, rope_spec,
                  _const_spec((1, LANES)), _const_spec((1, LANES)),
                  _const_spec((LANES, LANES))],
        out_specs=[tok(w) for w in widths],
        scratch_shapes=[pltpu.VMEM((D, PROJ_W), BF16)],
        compiler_params=_cparams(("arbitrary",)),
        name="proj",
    )(x_lat, x_ctx, mods_l, gn, w_in, cos, sa, sb, gq128, gk128, bd)


def _attend_many(jobs, sink_ref):
    lane = lax.broadcasted_iota(jnp.int32, (ATTN_BLK, LANES), 1)
    chains = []
    for q, kv_blocks, biases in jobs:
        for j in range(N_KV_HEADS):
            ks = slice(j * HEAD_DIM, (j + 1) * HEAD_DIM)
            vs = slice(KV_W + j * HEAD_DIM, KV_W + (j + 1) * HEAD_DIM)
            kj = jnp.concatenate([blk[:, ks] for blk in kv_blocks], axis=0)
            vj = jnp.concatenate([blk[:, vs] for blk in kv_blocks], axis=0)
            vaug = jnp.concatenate([vj, jnp.ones_like(vj)], axis=1)
            qs = jnp.concatenate([q[:, (j * GQA + g) * HEAD_DIM:(j * GQA + g + 1) * HEAD_DIM]
                                  for g in range(GQA)], axis=0)
            s = lax.dot_general(qs, kj, (((1,), (1,)), ((), ())), preferred_element_type=F32)
            chains.append((j, s, vaug, kv_blocks, biases))
    soft = []
    for j, s, vaug, kv_blocks, biases in chains:
        probs, sink_terms = [], []
        for g in range(GQA):
            sg = s[g * ATTN_BLK:(g + 1) * ATTN_BLK]
            pieces, col = [], 0
            for blk, bias in zip(kv_blocks, biases):
                piece = sg[:, col:col + blk.shape[0]]
                pieces.append(piece if bias is None else piece + bias)
                col += blk.shape[0]
            sg = jnp.concatenate(pieces, axis=1)
            sk = sink_ref[j * GQA + g] * LOG2E
            mx = jnp.maximum(jnp.max(sg, axis=-1, keepdims=True), sk)
            probs.append(jnp.exp2(sg - mx).astype(BF16))
            sink_terms.append(jnp.exp2(sk - mx))
        soft.append((jnp.concatenate(probs, axis=0), vaug, sink_terms))
    heads = []
    for p, vaug, sink_terms in soft:
        o = _dot(p, vaug)
        for g in range(GQA):
            og = o[g * ATTN_BLK:(g + 1) * ATTN_BLK]
            heads.append(og / (og[:, HEAD_DIM:HEAD_DIM + 1] + sink_terms[g]))
    outs = []
    for n in range(len(jobs)):
        hs = heads[n * N_Q_HEADS:(n + 1) * N_Q_HEADS]
        tiles = [jnp.where(lane < HEAD_DIM, hs[2 * t], pltpu.roll(hs[2 * t + 1], HEAD_DIM, 1))
                 for t in range(N_Q_HEADS // 2)]
        outs.append(jnp.concatenate(tiles, axis=1).astype(BF16))
    return outs


def _attn_latent_kernel(sink_ref, q_ref, prev_ref, cur_ref, next_ref, ctx_ref, o_ref):
    n = pl.program_id(1)
    r = lax.broadcasted_iota(jnp.int32, (ATTN_BLK, ATTN_BLK), 0)
    jj = lax.broadcasted_iota(jnp.int32, (ATTN_BLK, ATTN_BLK), 1)
    far = jnp.int32(2 * ATTN_BLK)
    off_prev = jnp.where(n > 0, 0, far)
    off_next = jnp.where(n < S // (ATTN_QB * ATTN_BLK) - 1, 0, far)
    prev_ok = jnp.where(jj - r >= 0, 0.0, NEG_INF)
    next_ok = jnp.where(r - jj >= 0, 0.0, NEG_INF)
    prev_edge = jnp.where(jj - r - off_prev >= 0, 0.0, NEG_INF)
    next_edge = jnp.where(r - jj - off_next >= 0, 0.0, NEG_INF)
    ctx = ctx_ref[...]
    rows = lambda b: slice(b * ATTN_BLK, (b + 1) * ATTN_BLK)
    blocks = [prev_ref[...]] + [cur_ref[rows(b), :] for b in range(ATTN_QB)] + [next_ref[...]]
    jobs = [(q_ref[rows(b), :], [ctx] + blocks[b:b + 3],
             [None, prev_edge if b == 0 else prev_ok, None, next_edge if b == ATTN_QB - 1 else next_ok])
            for b in range(ATTN_QB)]
    for b, out in enumerate(_attend_many(jobs, sink_ref)):
        o_ref[rows(b), :] = out


def _attn_context_kernel(sink_ref, q_ref, ctx_ref, o_ref):
    o_ref[...] = _attend_many([(q_ref[...], [ctx_ref[...]], [None])], sink_ref)[0]


def _attn_call(sink_l, q, kv):
    nb = S // ATTN_BLK
    nq = nb // ATTN_QB
    smem = pl.BlockSpec(memory_space=pltpu.SMEM)
    pair = lambda w: pl.BlockSpec((ATTN_QB * ATTN_BLK, w), lambda b, n: (b * nq + n, 0))
    prev = pl.BlockSpec((ATTN_BLK, 2 * KV_W), lambda b, n: (b * nb + jnp.maximum(ATTN_QB * n - 1, 0), 0))
    nxt = pl.BlockSpec((ATTN_BLK, 2 * KV_W),
                       lambda b, n: (b * nb + jnp.minimum(ATTN_QB * (n + 1), nb - 1), 0))
    ctxs = pl.BlockSpec((C, 2 * KV_W), lambda b, n: (N_LAT // C + b, 0))
    lat = pl.pallas_call(
        _attn_latent_kernel,
        out_shape=jax.ShapeDtypeStruct((N_LAT, Q_W), BF16),
        grid=(B, nq),
        in_specs=[smem, pair(Q_W), prev, pair(2 * KV_W), nxt, ctxs],
        out_specs=pair(Q_W),
        compiler_params=_cparams(("parallel", "parallel")),
        name="attn_latent",
    )(sink_l, q, kv, kv, kv, kv)
    ncb = C // ATTN_BLK
    base = N_LAT // ATTN_BLK
    ctx = pl.pallas_call(
        _attn_context_kernel,
        out_shape=jax.ShapeDtypeStruct((N_CTX, Q_W), BF16),
        grid=(B, ncb),
        in_specs=[smem, pl.BlockSpec((ATTN_BLK, Q_W), lambda b, n: (base + b * ncb + n, 0)), ctxs],
        out_specs=pl.BlockSpec((ATTN_BLK, Q_W), lambda b, n: (b * ncb + n, 0)),
        compiler_params=_cparams(("parallel", "parallel")),
        name="attn_context",
    )(sink_l, q, kv)
    return lat, ctx


def _f1_kernel(w_ref, f_ref, re_ref, im_ref):
    res = lax.dot_general(w_ref[...], f_ref[...], (((1,), (0,)), ((), ())), preferred_element_type=F32)
    re_ref[...] = res[:FS1].astype(BF16)
    im_ref[...] = res[FS1:].astype(BF16)


def _f2_kernel(ta_ref, tb_ref, re_ref, im_ref, after_ref, o_ref):
    del after_ref
    for i in range(F2_K1):
        res = _dot(ta_ref[i], re_ref[i]) + _dot(tb_ref[i], im_ref[i])
        o_ref[i, :, 0:FOURIER_W] = res[:FS2].astype(BF16)
        o_ref[i, :, FOURIER_W:2 * FOURIER_W] = res[FS2:].astype(BF16)


def _fc_kernel(w_ref, f_ref, o_ref):
    res = _dot(w_ref[...], f_ref[...])
    o_ref[:, 0:FOURIER_W] = res[:C].astype(BF16)
    o_ref[:, FOURIER_W:2 * FOURIER_W] = res[C:].astype(BF16)


def _fourier_stage1_call(f, tabs):
    w1 = tabs[0]
    f3 = f.reshape(N_TOK // FS2, FS2, FOURIER_W)
    blk = pl.BlockSpec((FS1, F1_ROWS, FOURIER_W), lambda b, j: (b, j, 0))
    return pl.pallas_call(
        _f1_kernel,
        out_shape=[jax.ShapeDtypeStruct((B * FS1, FS2, FOURIER_W), BF16)] * 2,
        grid=(B, FS2 // F1_ROWS),
        in_specs=[_const_spec((2 * FS1, FS1)), blk],
        out_specs=[blk, blk],
        compiler_params=_cparams(("parallel", "parallel")),
        name="fourier_stage1",
    )(w1, f3)


def _fourier_stage2_call(a_re, a_im, f, tabs, after):
    _, ta, tb, wc = tabs
    nk = FS1 // F2_K1
    aspec = pl.BlockSpec((F2_K1, FS2, FOURIER_W), lambda b, k1: (b * nk + k1, 0, 0))
    tspec = pl.BlockSpec((F2_K1, 2 * FS2, FS2), lambda b, k1: (k1, 0, 0))
    h_t = pl.pallas_call(
        _f2_kernel,
        out_shape=jax.ShapeDtypeStruct((B, FS1, FS2, 2 * FOURIER_W), BF16),
        grid=(B, nk),
        in_specs=[tspec, tspec, aspec, aspec, pl.BlockSpec(memory_space=pl.ANY)],
        out_specs=pl.BlockSpec((None, F2_K1, FS2, 2 * FOURIER_W), lambda b, k1: (b, k1, 0, 0)),
        compiler_params=_cparams(("parallel", "parallel")),
        name="fourier_stage2",
    )(ta, tb, a_re, a_im, after)
    h_lat = jnp.transpose(h_t, (0, 2, 1, 3)).reshape(N_LAT, 2 * FOURIER_W)
    h_ctx = pl.pallas_call(
        _fc_kernel,
        out_shape=jax.ShapeDtypeStruct((N_CTX, 2 * FOURIER_W), BF16),
        grid=(B,),
        in_specs=[_const_spec((2 * C, C)),
                  pl.BlockSpec((C, FOURIER_W), lambda b: (N_LAT // C + b, 0))],
        out_specs=pl.BlockSpec((C, 2 * FOURIER_W), lambda b: (b, 0)),
        compiler_params=_cparams(("parallel",)),
        name="fourier_context",
    )(wc, f)
    return h_lat, h_ctx


def _poolconv_kernel(pc_ref, pp_ref, pn_ref, uc_ref, up_ref, un_ref, bm_ref, bh_ref, ic_ref, pw_ref, ps_ref,
                     sh_ref, cw_ref, cb_ref, cg_ref, cnb_ref, z_ref, a_ref, win0_ref, win1_ref, cv_ref):
    t = pl.program_id(0)
    lat_tiles = N_LAT // TP
    per_seq = S // TP
    is_ctx = t >= lat_tiles
    first = jnp.logical_or(t % per_seq == 0, is_ctx)
    last = jnp.logical_or(t % per_seq == per_seq - 1, is_ctx)

    keep_prev = jnp.where(first, 0.0, 1.0)
    keep_next = jnp.where(last, 0.0, 1.0)

    ub = jnp.concatenate([(up_ref[...].astype(F32) * keep_prev).astype(BF16), uc_ref[...],
                          (un_ref[...].astype(F32) * keep_next).astype(BF16)], axis=0)
    off = HALO - CONV_K // 2
    half_rows = TP // 2
    wins = (win0_ref, win1_ref)
    for hf, win_ref in enumerate(wins):
        window = ub[hf * half_rows:hf * half_rows + CONV_WIN]
        win_ref[0] = window.astype(F32)
        for s in range(1, 8):
            win_ref[s] = _dot(sh_ref[s - 1], window)

    pcur = pc_ref[...]
    halo = jnp.concatenate([pp_ref[...].astype(F32) * keep_prev,
                            pn_ref[...].astype(F32) * keep_next], axis=0).astype(BF16)
    sums = []
    for gi in range(len(POOL_WINDOWS)):
        cs = slice(gi * LANES, (gi + 2) * LANES)
        sums.append(_dot(bm_ref[gi], pcur[:, cs]) + _dot(bh_ref[gi], halo[:, cs]))
    lane_t = lax.broadcasted_iota(jnp.int32, (TP, LANES), 1)
    tiles = [sums[0][:, :LANES]]
    for gi in range(1, len(POOL_WINDOWS)):
        split = gi * POOL_GROUP_W - gi * LANES
        tiles.append(jnp.where(lane_t < split, sums[gi - 1][:, LANES:], sums[gi][:, :LANES]))
    tiles.append(sums[-1][:, LANES:])
    zsum = jnp.concatenate(tiles, axis=1)
    z = zsum * ic_ref[...] - pcur.astype(F32)
    z_ref[...] = (_dot(z.astype(BF16), pw_ref[...]) * ps_ref[...]).astype(BF16)

    for hf, win_ref in enumerate(wins):
        base = hf * half_rows
        for cb in range(CONV_W // LANES):
            cs = slice(cb * LANES, (cb + 1) * LANES)
            acc = jnp.zeros((half_rows, LANES), F32) + cb_ref[:, cs]
            for j in range(CONV_K):
                s, m = (off + j) % 8, (off + j) // 8
                acc = acc + win_ref[s, 8 * m:8 * m + half_rows, cs] * cw_ref[j:j + 1, cs]
            cv_ref[base:base + half_rows, cs] = acc
    cv = cv_ref[...]
    mu = jnp.mean(cv, axis=-1, keepdims=True)
    var = jnp.mean(jnp.square(cv - mu), axis=-1, keepdims=True)
    un = (cv - mu) * lax.rsqrt(var + EPS) * cg_ref[...] + cnb_ref[...]
    a_ref[...] = (un * _sigmoid(un)).astype(BF16)


def _poolconv_call(p, u, band_main, band_halo, inv_cnt, pw_bd, pool_scale, shifts, conv_w, conv_b, cn_g, cn_b):
    nh = TP // HALO
    last_h = N_TOK // HALO - 1
    cur = lambda w: pl.BlockSpec((TP, w), lambda t: (t, 0))
    prv = lambda w: pl.BlockSpec((HALO, w), lambda t: (jnp.maximum(t * nh - 1, 0), 0))
    nxt = lambda w: pl.BlockSpec((HALO, w), lambda t: (jnp.minimum((t + 1) * nh, last_h), 0))
    per_seq = S // TP

    def kind(t):
        return jnp.where(t >= N_LAT // TP, 3, jnp.where(t % per_seq == 0, 1, jnp.where(t % per_seq == per_seq - 1, 2, 0)))

    return pl.pallas_call(
        _poolconv_kernel,
        out_shape=[jax.ShapeDtypeStruct((N_TOK, POOL_W), BF16),
                   jax.ShapeDtypeStruct((N_TOK, CONV_W), BF16)],
        grid=(NTP,),
        in_specs=[cur(POOL_W), prv(POOL_W), nxt(POOL_W), cur(CONV_W), prv(CONV_W), nxt(CONV_W),
                  _const_spec((4, TP, TP)), _const_spec((4, TP, 2 * HALO)),
                  pl.BlockSpec((None, TP, POOL_W), lambda t: (kind(t), 0, 0)),
                  _const_spec((POOL_W, POOL_W)), _const_spec((1, POOL_W)),
                  _const_spec((7, CONV_WIN, CONV_WIN)),
                  _const_spec((CONV_K, CONV_W)), _const_spec((1, CONV_W)),
                  _const_spec((1, CONV_W)), _const_spec((1, CONV_W))],
        out_specs=[cur(POOL_W), cur(CONV_W)],
        scratch_shapes=[pltpu.VMEM((8, CONV_WIN, CONV_W), F32), pltpu.VMEM((8, CONV_WIN, CONV_W), F32),
                        pltpu.VMEM((TP, CONV_W), F32)],
        compiler_params=_cparams(("parallel",)),
        name="pool_conv",
    )(p, p, p, u, u, u, band_main, band_halo, inv_cnt, pw_bd, pool_scale, shifts, conv_w, conv_b, cn_g, cn_b)


def _route(logits):
    lane = lax.broadcasted_iota(jnp.int32, logits.shape, 1)
    big = jnp.int32(LANES)
    lg = jnp.where(lane < N_GROUPS, logits, NEG_INF)
    mg = jnp.max(lg, axis=-1, keepdims=True)
    grp = jnp.min(jnp.where(lg == mg, lane, big), axis=-1, keepdims=True)
    p_grp = 1.0 / jnp.sum(jnp.exp(lg - mg), axis=-1, keepdims=True)
    lo = N_GROUPS + grp * EPG
    le = jnp.where((lane >= lo) & (lane < lo + EPG), logits, NEG_INF)
    m1 = jnp.max(le, axis=-1, keepdims=True)
    i1 = jnp.min(jnp.where(le == m1, lane, big), axis=-1, keepdims=True)
    le2 = jnp.where(lane == i1, NEG_INF, le)
    m2 = jnp.max(le2, axis=-1, keepdims=True)
    i2 = jnp.min(jnp.where(le2 == m2, lane, big), axis=-1, keepdims=True)
    r = jnp.exp(m2 - m1)
    w1 = p_grp / (1.0 + r)
    w2 = p_grp * r / (1.0 + r)
    e1 = (i1 - N_GROUPS).astype(F32)
    e2 = (i2 - N_GROUPS).astype(F32)
    return jnp.where(lane == 0, e1, jnp.where(lane == 1, e2, jnp.where(lane == 2, w1,
                     jnp.where(lane == 3, w2, 0.0))))


def _mix_kernel(xl_ref, xc_ref, mod_ref, gn_ref, al_ref, ac_ref, hl_ref, hc_ref, z_ref, cv_ref,
                wa_ref, wf_ref, wp_ref, wc_ref, wg_ref, bg_ref, wo_ref, gf_ref, rh_ref, rl_ref, rb_ref,
                xo_ref, h2_ref, rt_ref, *, split):
    is_ctx = pl.program_id(0) >= LAT_TILES
    m = mod_ref[...]
    x = _stream_tile(xl_ref, xc_ref, split)
    hb = _modulate(x, gn_ref[...], m[0:1], m[1:2]).astype(BF16)
    attn = jnp.where(is_ctx, ac_ref[...], al_ref[...])
    four = jnp.where(is_ctx, hc_ref[...], hl_ref[...])
    branches = ((attn, wa_ref), (four, wf_ref), (z_ref[...], wp_ref), (cv_ref[...], wc_ref))
    acc = None
    for bi, (inp, w_ref) in enumerate(branches):
        cs = slice(bi * D, (bi + 1) * D)
        gate = _sigmoid(_dot(hb, wg_ref[:, cs]) + bg_ref[:, cs])
        term = gate * _dot(inp, w_ref[...])
        acc = term if acc is None else acc + term
    x_new = x + m[2:3] * _dot(acc.astype(BF16), wo_ref[...])
    xo_ref[...] = x_new
    h2 = _modulate(x_new, gf_ref[...], m[3:4], m[4:5])
    hi = h2.astype(BF16)
    h2_ref[...] = hi
    lo = (h2 - hi.astype(F32)).astype(BF16)
    logits = _dot(hi, rh_ref[...]) + _dot(lo, rh_ref[...]) + _dot(hi, rl_ref[...]) + rb_ref[...]
    rt_ref[...] = _route(logits)


def _mix_call(x_lat, x_ctx, split, mods_l, gn, a_lat, a_ctx, h_lat, h_ctx, zc, cact, l, stacked, small):
    tok = lambda w: pl.BlockSpec((TM, w), lambda i: (i, 0))
    lat = lambda w: pl.BlockSpec((TM, w), lambda i: (jnp.minimum(i, LAT_TILES - 1), 0))
    wa, wf, wp, wc, wg, wo, rh, rl = stacked
    bg, gf, rb = small
    in_specs = _stream_specs(split) + [
                pl.BlockSpec((None, 6, D), lambda i: (_mod_row(i), 0, 0)), _const_spec((1, D)),
                lat(Q_W), _const_spec((N_CTX, Q_W)),
                lat(2 * FOURIER_W), _const_spec((N_CTX, 2 * FOURIER_W)),
                tok(POOL_W), tok(CONV_W)]
    in_specs += [_layer_spec(w.shape[1:], l) for w in (wa, wf, wp, wc, wg)]
    in_specs += [_const_spec(bg.shape), _layer_spec(wo.shape[1:], l), _const_spec(gf.shape),
                 _layer_spec(rh.shape[1:], l), _layer_spec(rl.shape[1:], l), _const_spec(rb.shape)]
    return pl.pallas_call(
        functools.partial(_mix_kernel, split=split),
        out_shape=[jax.ShapeDtypeStruct((N_TOK, D), F32), jax.ShapeDtypeStruct((N_TOK, D), BF16),
                   jax.ShapeDtypeStruct((N_TOK, LANES), F32)],
        grid=(NT,),
        in_specs=in_specs,
        out_specs=[tok(D), tok(D), tok(LANES)],
        compiler_params=_cparams(("parallel",)),
        name="mix",
    )(x_lat, x_ctx, mods_l, gn, a_lat, a_ctx, h_lat, h_ctx, zc, cact, wa, wf, wp, wc, wg, bg, wo, gf, rh, rl, rb)


def _onehots(route):
    lane = lax.broadcasted_iota(jnp.int32, route.shape, 1)
    e1 = route[:, 0:1].astype(jnp.int32)
    e2 = route[:, 1:2].astype(jnp.int32)
    return (lane == e1).astype(F32), (lane == e2).astype(F32)


def _lane_cumsum(row):
    lane = lax.broadcasted_iota(jnp.int32, row.shape, 1)
    sh = 1
    while sh < N_EXPERTS:
        row = row + jnp.where(lane >= sh, pltpu.roll(row, sh, 1), 0.0)
        sh *= 2
    return row


def _rank_kernel(rt_ref, tri_ref, pos_ref, meta_ref, cnt_ref, carry_ref):
    @pl.when(pl.program_id(0) == 0)
    def _():
        carry_ref[...] = jnp.zeros_like(carry_ref)

    tiles = []
    for t in range(RANK_TILES):
        oh1, oh2 = _onehots(rt_ref[t * TM:(t + 1) * TM, :])
        both = oh1 + oh2
        tiles.append((oh1, oh2, both, _dot(tri_ref[...], both.astype(BF16))))
    carry = carry_ref[0:1, :]
    lane = lax.broadcasted_iota(jnp.int32, (TM, LANES), 1)
    row = lax.broadcasted_iota(jnp.int32, meta_ref.shape[1:], 0)
    for t, (oh1, oh2, both, before) in enumerate(tiles):
        tile_cnt = jnp.sum(both, axis=0, keepdims=True)
        tile_cnt = tile_cnt + (tile_cnt - 2.0 * jnp.floor(tile_cnt * 0.5))
        tile_off = _lane_cumsum(tile_cnt) - tile_cnt
        where = before + tile_off
        p1 = jnp.sum(oh1 * where, axis=-1, keepdims=True)
        p2 = jnp.sum(oh2 * where, axis=-1, keepdims=True)
        pos_ref[t * TM:(t + 1) * TM, :] = jnp.where(lane == 0, p1, jnp.where(lane == 1, p2, 0.0))
        meta_ref[t] = jnp.where(row == 0, tile_off, jnp.where(row == 1, tile_cnt, jnp.where(row == 2, carry, 0.0)))
        carry = carry + tile_cnt
    carry_ref[...] = jnp.broadcast_to(carry, carry_ref.shape)
    cnt_ref[...] = jnp.broadcast_to(carry, cnt_ref.shape)


def _runs_kernel(meta_ref, cnt_ref, runs_ref, be_ref):
    lane = lax.broadcasted_iota(jnp.int32, (1, LANES), 1)
    counts = cnt_ref[0:1, :]
    padded = jnp.floor((counts + (MOE_BLK - 1)) * (1.0 / MOE_BLK)) * MOE_BLK
    padded = jnp.where(lane < N_EXPERTS, padded, 0.0)
    ends = _lane_cumsum(padded)
    starts = ends - padded
    for t in range(NT):
        m = meta_ref[t]
        row = lax.broadcasted_iota(jnp.int32, m.shape, 0)
        m = jnp.where(row == 2, m + starts, m)
        m = jnp.where(row == 3, starts + counts, jnp.where(row == 4, padded - counts, m))
        runs_ref[t] = m.astype(jnp.int32)
    blk = lax.broadcasted_iota(jnp.int32, be_ref.shape, 0).astype(F32) * MOE_BLK
    lane_b = lax.broadcasted_iota(jnp.int32, be_ref.shape, 1)
    done = jnp.where((ends <= blk) & (lane_b < N_EXPERTS), 1.0, 0.0)
    be = jnp.minimum(jnp.sum(done, axis=-1, keepdims=True), N_EXPERTS - 1.0)
    nblk = jnp.max(jnp.where(lane_b == N_EXPERTS - 1, ends, 0.0), axis=-1, keepdims=True) * (1.0 / MOE_BLK)
    be_ref[...] = jnp.where(lane_b == 0, be, jnp.where(lane_b == 1, nblk, 0.0)).astype(jnp.int32)


def _plan_call(route, tri):
    tok = pl.BlockSpec((RANK_TILES * TM, LANES), lambda i: (i, 0))
    pos, meta, counts = pl.pallas_call(
        _rank_kernel,
        out_shape=[jax.ShapeDtypeStruct((N_TOK, LANES), F32), jax.ShapeDtypeStruct((NT, 8, LANES), F32),
                   jax.ShapeDtypeStruct((8, LANES), F32)],
        grid=(NT // RANK_TILES,),
        in_specs=[tok, _const_spec((TM, TM))],
        out_specs=[tok, pl.BlockSpec((RANK_TILES, 8, LANES), lambda i: (i, 0, 0)),
                   pl.BlockSpec((8, LANES), lambda i: (0, 0))],
        scratch_shapes=[pltpu.VMEM((8, LANES), F32)],
        compiler_params=_cparams(("arbitrary",)),
        name="moe_rank",
    )(route, tri)
    runs, blk = pl.pallas_call(
        _runs_kernel,
        out_shape=[jax.ShapeDtypeStruct((NT, 8, LANES), jnp.int32),
                   jax.ShapeDtypeStruct((BLK_TABLE_ROWS, LANES), jnp.int32)],
        name="moe_runs",
    )(meta, counts)
    runs_flat = runs[:, 0:RUN_FIELDS, 0:N_EXPERTS].reshape(NT, 1, RUN_FIELDS * N_EXPERTS)
    return pos, runs_flat, blk[:N_MOE_BLOCKS, 0], blk[0:1, 1]


def _pack_pairs(x):
    half = x.shape[1] // 2
    lo = pltpu.bitcast(x[:, :half], jnp.uint32)
    hi = pltpu.bitcast(x[:, half:], jnp.uint32)
    return (lo >> 16) | (hi & jnp.uint32(0xFFFF0000))


def _unpack_pairs(w):
    lo = pltpu.bitcast(w << 16, F32)
    hi = pltpu.bitcast(w & jnp.uint32(0xFFFF0000), F32)
    return jnp.concatenate([lo, hi], axis=1).astype(BF16)


def _run_fields(runs_ref, e):
    return runs_ref[0, e], runs_ref[0, N_EXPERTS + e], runs_ref[0, 2 * N_EXPERTS + e]


def _store_rows(lin_ref, packed):
    rows = packed.shape[0]
    for c in range(PK):
        lin_ref[pl.ds(c, rows, stride=PK), :] = packed[:, c * LANES:(c + 1) * LANES]


def _load_rows(lin_ref, rows):
    return jnp.concatenate([lin_ref[pl.ds(c, rows, stride=PK), :] for c in range(PK)], axis=1)


def _lin(ref, row, nrows):
    return ref.at[pl.ds(pl.multiple_of(row * PK, SUBLANES), nrows * PK), :]


FETCH_ROWS = 64
RUN_PIECES = (32, 16, 8, 4, 2)
TAIL_PIECES = (256, 128, 64, 32, 16, 8, 4, 2)


def _run_copies(runs, make_copy, act):
    def per_expert(e, carry):
        off, n, dst = _run_fields(runs, e)
        whole = n // FETCH_ROWS

        def chunk(k, c):
            act(make_copy(off + k * FETCH_ROWS, dst + k * FETCH_ROWS, FETCH_ROWS))
            return c

        lax.fori_loop(0, whole, chunk, 0)
        done = whole * FETCH_ROWS
        for size in RUN_PIECES:
            @pl.when((n & size) != 0)
            def _(done=done, size=size):
                act(make_copy(off + done, dst + done, size))
            done = done + (n & size)
        return carry

    lax.fori_loop(0, N_EXPERTS, per_expert, 0)


def _rows_wait(runs, make_copy):
    total = lax.fori_loop(0, N_EXPERTS, lambda e, acc: acc + runs[0, N_EXPERTS + e], jnp.int32(0))
    lax.fori_loop(0, total // FETCH_ROWS, lambda k, c: (make_copy(0, 0, FETCH_ROWS).wait(), c)[1], 0)
    for size in RUN_PIECES:
        @pl.when((total & size) != 0)
        def _(size=size):
            make_copy(0, 0, size).wait()


def _dispatch_kernel(runs_ref, prev_runs_ref, pos_ref, h2_ref, xs_ref, buf0, buf1, zero_ref, sem0, sem1, zsem,
                     ssem):
    i = pl.program_id(0)
    last = NT - 1

    def tail_copies(act):
        def per_expert(e, carry):
            row = runs_ref[0, 3 * N_EXPERTS + e]
            n = runs_ref[0, 4 * N_EXPERTS + e]
            done = jnp.int32(0)
            for size in TAIL_PIECES:
                @pl.when((n & size) != 0)
                def _(done=done, size=size):
                    act(pltpu.make_async_copy(zero_ref.at[pl.ds(0, size * PK), :], _lin(xs_ref, row + done, size), zsem))
                done = done + (n & size)
            return carry

        lax.fori_loop(0, N_EXPERTS, per_expert, 0)

    def spare_copies(act):
        used = runs_ref[0, 3 * N_EXPERTS + N_EXPERTS - 1] + runs_ref[0, 4 * N_EXPERTS + N_EXPERTS - 1]

        def spare_block(k, carry):
            act(pltpu.make_async_copy(zero_ref.at[pl.ds(0, MOE_BLK * PK), :],
                                      _lin(xs_ref, used + k * MOE_BLK, MOE_BLK), ssem))
            return carry

        lax.fori_loop(0, N_MOE_BLOCKS - used // MOE_BLK, spare_block, 0)

    start = lambda d: d.start()
    wait = lambda d: d.wait()

    @pl.when(i == 0)
    def _():
        zero_ref[...] = jnp.zeros_like(zero_ref)
        tail_copies(start)
        spare_copies(start)

    pos = pos_ref[...]
    col = lax.broadcasted_iota(jnp.int32, (TM, TS), 1).astype(F32)
    sel = jnp.where((col == pos[:, 0:1]) | (col == pos[:, 1:2]), 1.0, 0.0).astype(BF16)
    srt = lax.dot_general(sel, h2_ref[...], (((0,), (0,)), ((), ())), preferred_element_type=F32)
    packed = _pack_pairs(srt)

    for par, (buf, sem, obuf, osem) in enumerate(((buf0, sem0, buf1, sem1), (buf1, sem1, buf0, sem0))):
        @pl.when(i % 2 == par)
        def _(buf=buf, sem=sem, obuf=obuf, osem=osem):
            copy = lambda off, dst, rows: pltpu.make_async_copy(_lin(buf, off, rows), _lin(xs_ref, dst, rows), sem)
            ocopy = lambda off, dst, rows: pltpu.make_async_copy(_lin(obuf, off, rows), _lin(xs_ref, dst, rows), osem)
            _store_rows(buf, packed)

            @pl.when(i == 0)
            def _():
                tail_copies(wait)

            _run_copies(runs_ref, copy, start)

            @pl.when(i > 0)
            def _():
                _rows_wait(prev_runs_ref, ocopy)

            @pl.when(i == last)
            def _():
                _rows_wait(runs_ref, copy)
                spare_copies(wait)


def _dispatch_call(runs_flat, pos, h2):
    runs_spec = lambda shift: pl.BlockSpec((None, 1, RUN_FIELDS * N_EXPERTS), lambda i: (jnp.maximum(i - shift, 0), 0, 0),
                                           memory_space=pltpu.SMEM)
    sorted_buf = pltpu.VMEM((TS * PK, LANES), jnp.uint32)
    return pl.pallas_call(
        _dispatch_kernel,
        out_shape=jax.ShapeDtypeStruct((N_SLOTS * PK, LANES), jnp.uint32),
        grid=(NT,),
        in_specs=[runs_spec(0), runs_spec(1),
                  pl.BlockSpec((TM, LANES), lambda i: (i, 0)),
                  pl.BlockSpec((TM, D), lambda i: (i, 0))],
        out_specs=pl.BlockSpec(memory_space=pl.ANY),
        scratch_shapes=[sorted_buf, sorted_buf, pltpu.VMEM((MOE_BLK * PK, LANES), jnp.uint32),
                        pltpu.SemaphoreType.DMA, pltpu.SemaphoreType.DMA, pltpu.SemaphoreType.DMA,
                        pltpu.SemaphoreType.DMA],
        compiler_params=_cparams(("arbitrary",)),
        name="moe_dispatch",
    )(runs_flat, runs_flat, pos, h2)


def _expert_kernel(be_ref, nu_ref, xs_ref, *refs):
    w_refs, (ys_ref, wgb_ref, wub_ref, wdb_ref) = refs[:3 * W_PARTS], refs[3 * W_PARTS:]
    b = pl.program_id(0)

    @pl.when(jnp.logical_or(b == 0, be_ref[b] != be_ref[jnp.maximum(b - 1, 0)]))
    def _():
        for m, dst in enumerate((wgb_ref, wub_ref, wdb_ref)):
            rows = dst.shape[0] // W_PARTS
            for part in range(W_PARTS):
                dst[part * rows:(part + 1) * rows, :] = w_refs[m * W_PARTS + part][...].astype(BF16)

    @pl.when(b < nu_ref[0])
    def _():
        xb = _unpack_pairs(_load_rows(xs_ref, MOE_BLK))
        g = _dot(xb, wgb_ref[...])
        u = _dot(xb, wub_ref[...])
        hmid = (g * _sigmoid(g)) * u
        y = _dot(hmid.astype(BF16), wdb_ref[...])
        _store_rows(ys_ref, _pack_pairs(y.astype(BF16).astype(F32)))

    @pl.when(b >= nu_ref[0])
    def _():
        ys_ref[...] = jnp.zeros_like(ys_ref)


def _expert_call(blk_e, n_used, xs, wg, wu, wd, l):
    def wspecs(k, n):
        return [pl.BlockSpec((None, None, k // W_PARTS, n), lambda b, be, nu, part=part: (l, be[b], part, 0))
                for part in range(W_PARTS)]

    return pl.pallas_call(
        _expert_kernel,
        out_shape=jax.ShapeDtypeStruct((N_SLOTS * PK, LANES), jnp.uint32),
        grid_spec=pltpu.PrefetchScalarGridSpec(
            num_scalar_prefetch=2,
            grid=(N_MOE_BLOCKS,),
            in_specs=[pl.BlockSpec((MOE_BLK * PK, LANES), lambda b, be, nu: (jnp.minimum(b, nu[0] - 1), 0))]
            + wspecs(D, EXPERT_HIDDEN) + wspecs(D, EXPERT_HIDDEN) + wspecs(EXPERT_HIDDEN, D),
            out_specs=pl.BlockSpec((MOE_BLK * PK, LANES), lambda b, be, nu: (b, 0)),
            scratch_shapes=[pltpu.VMEM((D, EXPERT_HIDDEN), BF16), pltpu.VMEM((D, EXPERT_HIDDEN), BF16),
                            pltpu.VMEM((EXPERT_HIDDEN, D), BF16)]),
        compiler_params=_cparams(("arbitrary",)),
        name="moe_experts",
    )(blk_e, n_used, xs, *([wg] * W_PARTS + [wu] * W_PARTS + [wd] * W_PARTS))


def _combine_kernel(runs_ref, next_runs_ref, ys_ref, pos_ref, x_ref, rt_ref, mod_ref, o_ref, buf0, buf1, sem0, sem1,
                    *, n_tiles):
    i = pl.program_id(0)
    last = n_tiles - 1
    start = lambda d: d.start()
    fetch = lambda buf, sem: (
        lambda off, dst, rows: pltpu.make_async_copy(_lin(ys_ref, dst, rows), _lin(buf, off, rows), sem))

    @pl.when(i == 0)
    def _():
        buf0[...] = jnp.zeros_like(buf0)
        buf1[...] = jnp.zeros_like(buf1)
        _run_copies(runs_ref, fetch(buf0, sem0), start)

    pos = pos_ref[...]
    rt = rt_ref[...]
    col = lax.broadcasted_iota(jnp.int32, (TM, TS), 1).astype(F32)
    pick = (jnp.where(col == pos[:, 0:1], rt[:, TOP_K:TOP_K + 1], 0.0)
            + jnp.where(col == pos[:, 1:2], rt[:, TOP_K + 1:TOP_K + 2], 0.0)).astype(BF16)

    for par, (buf, sem, obuf, osem) in enumerate(((buf0, sem0, buf1, sem1), (buf1, sem1, buf0, sem0))):
        @pl.when(i % 2 == par)
        def _(buf=buf, sem=sem, obuf=obuf, osem=osem):
            @pl.when(i < last)
            def _():
                _run_copies(next_runs_ref, fetch(obuf, osem), start)

            _rows_wait(runs_ref, fetch(buf, sem))
            ysb = _unpack_pairs(_load_rows(buf, TS))
            o_ref[...] = x_ref[...] + mod_ref[5:6, :] * _dot(pick, ysb)


def _combine_call(runs_flat, ys, pos, x, route, mods_l, n_tiles):
    tok = lambda w: pl.BlockSpec((TM, w), lambda i: (i, 0))
    runs_spec = lambda shift: pl.BlockSpec((None, 1, RUN_FIELDS * N_EXPERTS),
                                           lambda i: (jnp.minimum(i + shift, n_tiles - 1), 0, 0),
                                           memory_space=pltpu.SMEM)
    sorted_buf = pltpu.VMEM((TS * PK, LANES), jnp.uint32)
    return pl.pallas_call(
        functools.partial(_combine_kernel, n_tiles=n_tiles),
        out_shape=jax.ShapeDtypeStruct((n_tiles * TM, D), F32),
        grid=(n_tiles,),
        in_specs=[runs_spec(0), runs_spec(1),
                  pl.BlockSpec(memory_space=pl.ANY),
                  tok(LANES), tok(D), tok(LANES),
                  pl.BlockSpec((None, 6, D), lambda i: (_mod_row(i), 0, 0))],
        out_specs=tok(D),
        scratch_shapes=[sorted_buf, sorted_buf, pltpu.SemaphoreType.DMA, pltpu.SemaphoreType.DMA],
        compiler_params=_cparams(("arbitrary",)),
        name="moe_combine",
    )(runs_flat, runs_flat, ys, pos, x, route, mods_l)


def _rope_tables():
    nf = HEAD_DIM // 4
    inv = ROPE_BASE ** (-jnp.arange(nf, dtype=F32) / nf)
    t = jnp.arange(S)
    row = (t // GRID_W).astype(F32)[:, None] * inv[None, :]
    col = (t % GRID_W).astype(F32)[:, None] * inv[None, :]
    zero = jnp.zeros_like(row)
    cos = jnp.concatenate([jnp.cos(row), jnp.cos(row), jnp.cos(col), jnp.cos(col)], axis=1)
    sa = jnp.concatenate([-jnp.sin(row), zero, -jnp.sin(col), zero], axis=1)
    sb = jnp.concatenate([zero, jnp.sin(row), zero, jnp.sin(col)], axis=1)
    ident = (jnp.ones((TM, HEAD_DIM), F32), jnp.zeros((TM, HEAD_DIM), F32), jnp.zeros((TM, HEAD_DIM), F32))
    return tuple(jnp.tile(jnp.concatenate([a, b], axis=0), (1, LANES // HEAD_DIM))
                 for a, b in zip((cos, sa, sb), ident))


def _fourier_tables():
    s1 = np.arange(FS1)
    ang1 = 2.0 * np.pi * np.outer(s1, s1) / FS1
    w1 = np.concatenate([np.cos(ang1), -np.sin(ang1)], axis=0) / np.sqrt(S)
    k1 = np.arange(FS1)[:, None, None]
    k2 = np.arange(FS2)[None, :, None]
    s2 = np.arange(FS2)[None, None, :]
    ang2 = 2.0 * np.pi * ((k1 + FS1 * k2) * s2 % S) / S
    c2, sn2 = np.cos(ang2), np.sin(ang2)
    ta = np.concatenate([c2, -sn2], axis=1)
    tb = np.concatenate([sn2, c2], axis=1)
    sc = np.arange(C)
    angc = 2.0 * np.pi * np.outer(sc, sc) / C
    wc = np.concatenate([np.cos(angc), -np.sin(angc)], axis=0) / np.sqrt(C)
    return tuple(jnp.asarray(a, F32).astype(BF16) for a in (w1, ta, tb, wc))


def _channel_dft():
    cidx = np.arange(FOURIER_GROUP_W)
    ang = 2.0 * np.pi * np.outer(cidx, cidx) / FOURIER_GROUP_W
    eye = np.eye(FOURIER_W // FOURIER_GROUP_W)
    cw = np.kron(eye, np.cos(ang)) / np.sqrt(FOURIER_GROUP_W)
    sw = np.kron(eye, np.sin(ang)) / np.sqrt(FOURIER_GROUP_W)
    return jnp.asarray(np.concatenate([cw, sw], axis=0), F32)


def _pool_bands():
    t = np.arange(TP)[:, None]
    main, halo = [], []
    for w in POOL_WINDOWS:
        def hit(j):
            return ((j - t >= -(w // 2)) & (j - t <= w // 2 - 1)).astype(np.float32)
        main.append(hit(np.arange(TP)[None, :]))
        halo.append(np.concatenate([hit(np.arange(-HALO, 0)[None, :]),
                                    hit(np.arange(TP, TP + HALO)[None, :])], axis=1))
    return (jnp.asarray(np.stack(main), F32).astype(BF16), jnp.asarray(np.stack(halo), F32).astype(BF16))


def _pool_inv_counts():
    win = np.repeat(np.array(POOL_WINDOWS), POOL_GROUP_W)[None, :]

    def table(pos0, seq_len):
        pos = (pos0 + np.arange(TP))[:, None]
        lo = np.clip(pos - win // 2, 0, seq_len)
        hi = np.clip(pos - win // 2 + win, 0, seq_len)
        return 1.0 / (hi - lo)

    tabs = [table(TP, S), table(0, S), table(S - TP, S), table(0, C)]
    return jnp.asarray(np.stack(tabs), F32)


def _conv_shifts():
    i = np.arange(CONV_WIN)
    return jnp.asarray(np.stack([(i[None, :] == i[:, None] + s) for s in range(1, 8)]), F32).astype(BF16)


def _fold_kernel(a_ref, b_ref, o_ref):
    a, b = a_ref[...], b_ref[...]
    a_hi, b_hi = a.astype(BF16), b.astype(BF16)
    a_lo = (a - a_hi.astype(F32)).astype(BF16)
    b_lo = (b - b_hi.astype(F32)).astype(BF16)
    o_ref[...] = (_dot(a_hi, b_hi) + _dot(a_lo, b_hi) + _dot(a_hi, b_lo)).astype(BF16)


def _fold_fourier_weights(dftw, w_br_fourier):
    nl = w_br_fourier.shape[0]
    return pl.pallas_call(
        _fold_kernel,
        out_shape=jax.ShapeDtypeStruct((nl, 2 * FOURIER_W, D), BF16),
        grid=(nl,),
        in_specs=[pl.BlockSpec((2 * FOURIER_W, FOURIER_W), lambda l: (0, 0)),
                  pl.BlockSpec((None, FOURIER_W, D), lambda l: (l, 0, 0))],
        out_specs=pl.BlockSpec((None, 2 * FOURIER_W, D), lambda l: (l, 0, 0)),
        compiler_params=_cparams(("arbitrary",)),
        name="fold_fourier_proj",
    )(dftw, w_br_fourier)


def _block_diag(blocks):
    n, r, c = blocks.shape
    eye = jnp.eye(n, dtype=blocks.dtype)
    return (blocks[:, :, None, :] * eye[:, None, :, None]).reshape(n * r, n * c)


def kernel(x, c, ctx, c_ctx, w_ada, b_ada, g_norm_mix, g_norm_ffn, w_in, g_q, g_k, sink, w_br_attn,
           w_br_fourier, pool_w, pool_scale, w_br_pool, conv_w, conv_b, cn_g, cn_b, w_br_conv, w_gate,
           b_gate, w_out, w_router_grp, b_router_grp, w_router_exp, b_router_exp, w_e_gate, w_e_up,
           w_e_down):
    x_lat, x_ctx, split = x.reshape(N_LAT, D), ctx.reshape(N_CTX, D), True
    nl = w_ada.shape[0]
    mods = _ada_all(c, c_ctx, w_ada, b_ada).reshape(nl, 8, 6, D)
    rope_tabs = _rope_tables()
    four_tabs = _fourier_tables()
    band_main, band_halo = _pool_bands()
    inv_cnt = _pool_inv_counts()
    shifts = _conv_shifts()
    wf_all = _fold_fourier_weights(_channel_dft(), w_br_fourier)
    bd = jnp.asarray(np.kron(np.eye(LANES // HEAD_DIM), np.ones((HEAD_DIM, HEAD_DIM))), F32).astype(BF16)
    tri = jnp.asarray(np.tril(np.ones((TM, TM)), -1), F32).astype(BF16)
    rpad = jnp.zeros((nl, D, LANES - N_GROUPS - N_EXPERTS), F32)
    w_router = jnp.concatenate([w_router_grp, w_router_exp, rpad], axis=-1)
    r_hi = w_router.astype(BF16)
    r_lo = (w_router - r_hi.astype(F32)).astype(BF16)
    r_b = jnp.concatenate([b_router_grp, b_router_exp, rpad[:, 0, :]], axis=-1).reshape(nl, 1, LANES)
    stacked = tuple(w.astype(BF16) for w in (w_br_attn,)) + (wf_all,) + tuple(
        w.astype(BF16) for w in (w_br_pool, w_br_conv, w_gate, w_out)) + (r_hi, r_lo)

    for l in range(nl):
        mods_l = mods[l]
        gn = g_norm_mix[l].reshape(1, D)
        q, kv, f, p, u = _proj_call(x_lat, x_ctx, split, mods_l, gn, w_in, l, rope_tabs,
                                    jnp.tile(g_q[l], 2).reshape(1, LANES),
                                    jnp.tile(g_k[l], 2).reshape(1, LANES), bd)
        a_re, a_im = _fourier_stage1_call(f, four_tabs)
        a_lat, a_ctx = _attn_call(sink[l], q, kv)
        h_lat, h_ctx = _fourier_stage2_call(a_re, a_im, f, four_tabs, a_lat)
        zc, cact = _poolconv_call(p, u, band_main, band_halo, inv_cnt, _block_diag(pool_w[l]).astype(BF16),
                                  pool_scale[l].reshape(1, POOL_W), shifts, conv_w[l], conv_b[l].reshape(1, CONV_W),
                                  cn_g[l].reshape(1, CONV_W), cn_b[l].reshape(1, CONV_W))
        small = (b_gate[l].reshape(1, 4 * D), g_norm_ffn[l].reshape(1, D), r_b[l])
        xs, h2, route = _mix_call(x_lat, x_ctx, split, mods_l, gn, a_lat, a_ctx, h_lat, h_ctx, zc, cact, l,
                                  stacked, small)
        pos, runs_flat, blk_e, n_used = _plan_call(route, tri)
        slots = _dispatch_call(runs_flat, pos, h2)
        ys = _expert_call(blk_e, n_used, slots, w_e_gate, w_e_up, w_e_down, l)
        xs = _combine_call(runs_flat, ys, pos, xs, route, mods_l, LAT_TILES if l == nl - 1 else NT)
        x_lat, x_ctx, split = xs, xs, False
    return xs.reshape(B, S, D)
```

```python
import functools

import numpy as np
import jax
import jax.numpy as jnp
from jax import lax
from jax.experimental import pallas as pl
from jax.experimental.pallas import tpu as pltpu

F32 = jnp.float32
BF16 = jnp.bfloat16

D = 1024
B = 2
S = 8192
C = 256
GRID_W = 64
HEAD_DIM = 64
N_Q_HEADS = 8
N_KV_HEADS = 2
GQA = N_Q_HEADS // N_KV_HEADS
WINDOW = 128
ATTN_BLK = 128
ATTN_QB = 8
ROPE_BASE = 10000.0
Q_W = 512
KV_W = 128
FOURIER_W = 640
FOURIER_GROUP_W = 160
POOL_W = 640
POOL_GROUP_W = 160
POOL_WINDOWS = (2, 4, 8, 16)
CONV_W = 512
CONV_K = 31
PROJ_W = 3072
N_GROUPS = 4
EPG = 8
N_EXPERTS = 32
TOP_K = 2
EXPERT_HIDDEN = 512
MOE_BLK = 512
EPS = 1e-6
NEG_INF = -1e30
LOG2E = 1.4426950408889634

N_LAT = B * S
N_CTX = B * C
N_TOK = N_LAT + N_CTX
TM = 512
NT = N_TOK // TM
LAT_TILES = N_LAT // TM
TILES_PER_BATCH = S // TM
TP = 256
NTP = N_TOK // TP
HALO = 16
CONV_WIN = TP // 2 + 2 * HALO
N_ASSIGN = N_TOK * TOP_K
RANK_TILES = 3
RUN_FIELDS = 5
PK = D // 2 // 128
TS = 1152
N_MOE_BLOCKS = (N_ASSIGN + NT * N_EXPERTS + N_EXPERTS * (MOE_BLK - 1)) // MOE_BLK
N_SLOTS = N_MOE_BLOCKS * MOE_BLK
SUBLANES = 8
BLK_TABLE_ROWS = -(-N_MOE_BLOCKS // SUBLANES) * SUBLANES
ROPE_PAIR = HEAD_DIM // 4
FS1 = 64
FS2 = 128
F1_ROWS = 32
F2_K1 = 16
LANES = 128
VMEM_LIMIT = 56 * 1024 * 1024


def _cparams(sem, vmem=VMEM_LIMIT):
    return pltpu.CompilerParams(dimension_semantics=sem, vmem_limit_bytes=vmem)


def _const_spec(shape):
    nd = len(shape)
    return pl.BlockSpec(shape, lambda *_: (0,) * nd, pipeline_mode=pl.Buffered(1))


def _dot(a, b):
    return jnp.dot(a, b, preferred_element_type=F32)


def _modulate(x, g, shift, scale):
    y = x * lax.rsqrt(jnp.mean(x * x, axis=-1, keepdims=True) + EPS)
    return (y * g) * (1.0 + scale) + shift


def _sigmoid(x):
    return 1.0 / (1.0 + jnp.exp(-x))


def _ada_kernel(ct_ref, w_ref, b_ref, o_ref):
    ct = ct_ref[...]
    s = ct * _sigmoid(ct)
    w = w_ref[...]
    rows = [jnp.sum(w * s[:, r:r + 1], axis=0, keepdims=True) for r in range(3)]
    rows.append(jnp.zeros((5, w.shape[1]), F32))
    o_ref[...] = jnp.concatenate(rows, axis=0) + b_ref[...]


def _ada_all(c, c_ctx, w_ada, b_ada):
    ct = jnp.concatenate([c, c_ctx[None, :], jnp.zeros((5, D), F32)], axis=0).T
    cols = 1536
    nl = w_ada.shape[0]
    return pl.pallas_call(
        _ada_kernel,
        out_shape=jax.ShapeDtypeStruct((nl, 8, 6 * D), F32),
        grid=(nl, 6 * D // cols),
        in_specs=[pl.BlockSpec((D, 8), lambda l, j: (0, 0)),
                  pl.BlockSpec((None, D, cols), lambda l, j: (l, 0, j)),
                  pl.BlockSpec((None, 1, cols), lambda l, j: (l, 0, j))],
        out_specs=pl.BlockSpec((None, 8, cols), lambda l, j: (l, 0, j)),
        compiler_params=_cparams(("arbitrary", "arbitrary")),
        name="adaln",
    )(ct, w_ada, b_ada.reshape(nl, 1, 6 * D))


def _head_rms(t, g128, bd):
    outs = []
    for j in range(t.shape[1] // LANES):
        blk = t[:, j * LANES:(j + 1) * LANES]
        ss = _dot((blk * blk).astype(BF16), bd)
        outs.append(blk * lax.rsqrt(ss * (1.0 / HEAD_DIM) + EPS) * g128)
    return outs


def _rope(blocks, cos, sa, sb):
    outs = []
    for blk in blocks:
        up = pltpu.roll(blk, LANES - ROPE_PAIR, 1)
        dn = pltpu.roll(blk, ROPE_PAIR, 1)
        outs.append(blk * cos + up * sa + dn * sb)
    return outs


def _proj_kernel(xl_ref, xc_ref, mod_ref, gn_ref, w_ref, cos_ref, sa_ref, sb_ref, gq_ref, gk_ref, bd_ref,
                 q_ref, kv_ref, f_ref, p_ref, u_ref, wbf_ref, *, split):
    @pl.when(pl.program_id(0) == 0)
    def _():
        wbf_ref[...] = w_ref[...].astype(BF16)

    m = mod_ref[...]
    hb = _modulate(_stream_tile(xl_ref, xc_ref, split), gn_ref[...], m[0:1], m[1:2]).astype(BF16)
    cos, sa, sb, bd = cos_ref[...], sa_ref[...], sb_ref[...], bd_ref[...]
    o_kv, o_f, o_a = Q_W, Q_W + 2 * KV_W, Q_W + 2 * KV_W + FOURIER_W + POOL_W
    qkv = _dot(hb, wbf_ref[:, 0:o_f])
    fp = _dot(hb, wbf_ref[:, o_f:o_a])
    q = _rope(_head_rms(qkv[:, 0:Q_W], gq_ref[...], bd), cos, sa, sb)
    q_ref[...] = (jnp.concatenate(q, axis=1) * (LOG2E * HEAD_DIM ** -0.5)).astype(BF16)
    k = _rope(_head_rms(qkv[:, o_kv:o_kv + KV_W], gk_ref[...], bd), cos, sa, sb)
    kv_ref[:, 0:KV_W] = k[0].astype(BF16)
    kv_ref[:, KV_W:2 * KV_W] = qkv[:, o_kv + KV_W:o_f].astype(BF16)
    ag = _dot(hb, wbf_ref[:, o_a:PROJ_W])
    f_ref[...] = fp[:, 0:FOURIER_W].astype(BF16)
    p_ref[...] = fp[:, FOURIER_W:].astype(BF16)
    u_ref[...] = (ag[:, 0:CONV_W] * _sigmoid(ag[:, CONV_W:])).astype(BF16)


def _mod_row(i):
    return jnp.minimum(i // TILES_PER_BATCH, 2)


def _layer_spec(shape, l):
    nd = len(shape)
    return pl.BlockSpec((None,) + tuple(shape), lambda *_: (l,) + (0,) * nd, pipeline_mode=pl.Buffered(1))


def _stream_specs(split):
    first = (lambda i: (jnp.minimum(i, LAT_TILES - 1), 0)) if split else (lambda i: (i, 0))
    return [pl.BlockSpec((TM, D), first),
            pl.BlockSpec((TM, D), lambda i: (0, 0), pipeline_mode=pl.Buffered(1))]


def _stream_tile(xl_ref, xc_ref, split):
    return jnp.where(pl.program_id(0) >= LAT_TILES, xc_ref[...], xl_ref[...]) if split else xl_ref[...]


def _proj_call(x_lat, x_ctx, split, mods_l, gn, w_in, l, rope_tabs, gq128, gk128, bd):
    cos, sa, sb = rope_tabs
    tok = lambda w: pl.BlockSpec((TM, w), lambda i: (i, 0))
    rope_spec = pl.BlockSpec((TM, LANES), lambda i: (jnp.where(i < LAT_TILES, i % TILES_PER_BATCH,
                                                               TILES_PER_BATCH), 0))
    widths = (Q_W, 2 * KV_W, FOURIER_W, POOL_W, CONV_W)
    return pl.pallas_call(
        functools.partial(_proj_kernel, split=split),
        out_shape=[jax.ShapeDtypeStruct((N_TOK, w), BF16) for w in widths],
        grid=(NT,),
        in_specs=_stream_specs(split) + [
                  pl.BlockSpec((None, 6, D), lambda i: (_mod_row(i), 0, 0)),
                  _const_spec((1, D)),
                  _layer_spec((D, PROJ_W), l),
                  rope_spec, rope_spec, rope_spec,
                  _const_spec((1, LANES)), _const_spec((1, LANES)),
                  _const_spec((LANES, LANES))],
        out_specs=[tok(w) for w in widths],
        scratch_shapes=[pltpu.VMEM((D, PROJ_W), BF16)],
        compiler_params=_cparams(("arbitrary",)),
        name="proj",
    )(x_lat, x_ctx, mods_l, gn, w_in, cos, sa, sb, gq128, gk128, bd)


def _attend_many(jobs, sink_ref):
    lane = lax.broadcasted_iota(jnp.int32, (ATTN_BLK, LANES), 1)
    chains = []
    for q, kv_blocks, biases in jobs:
        for j in range(N_KV_HEADS):
            ks = slice(j * HEAD_DIM, (j + 1) * HEAD_DIM)
            vs = slice(KV_W + j * HEAD_DIM, KV_W + (j + 1) * HEAD_DIM)
            kj = jnp.concatenate([blk[:, ks] for blk in kv_blocks], axis=0)
            vj = jnp.concatenate([blk[:, vs] for blk in kv_blocks], axis=0)
            vaug = jnp.concatenate([vj, jnp.ones_like(vj)], axis=1)
            qs = jnp.concatenate([q[:, (j * GQA + g) * HEAD_DIM:(j * GQA + g + 1) * HEAD_DIM]
                                  for g in range(GQA)], axis=0)
            s = lax.dot_general(qs, kj, (((1,), (1,)), ((), ())), preferred_element_type=F32)
            chains.append((j, s, vaug, kv_blocks, biases))
    soft = []
    for j, s, vaug, kv_blocks, biases in chains:
        probs, sink_terms = [], []
        for g in range(GQA):
            sg = s[g * ATTN_BLK:(g + 1) * ATTN_BLK]
            pieces, col = [], 0
            for blk, bias in zip(kv_blocks, biases):
                piece = sg[:, col:col + blk.shape[0]]
                pieces.append(piece if bias is None else piece + bias)
                col += blk.shape[0]
            sg = jnp.concatenate(pieces, axis=1)
            sk = sink_ref[j * GQA + g] * LOG2E
            mx = jnp.maximum(jnp.max(sg, axis=-1, keepdims=True), sk)
            probs.append(jnp.exp2(sg - mx).astype(BF16))
            sink_terms.append(jnp.exp2(sk - mx))
        soft.append((jnp.concatenate(probs, axis=0), vaug, sink_terms))
    heads = []
    for p, vaug, sink_terms in soft:
        o = _dot(p, vaug)
        for g in range(GQA):
            og = o[g * ATTN_BLK:(g + 1) * ATTN_BLK]
            heads.append(og / (og[:, HEAD_DIM:HEAD_DIM + 1] + sink_terms[g]))
    outs = []
    for n in range(len(jobs)):
        hs = heads[n * N_Q_HEADS:(n + 1) * N_Q_HEADS]
        tiles = [jnp.where(lane < HEAD_DIM, hs[2 * t], pltpu.roll(hs[2 * t + 1], HEAD_DIM, 1))
                 for t in range(N_Q_HEADS // 2)]
        outs.append(jnp.concatenate(tiles, axis=1).astype(BF16))
    return outs


def _attn_latent_kernel(sink_ref, q_ref, prev_ref, cur_ref, next_ref, ctx_ref, o_ref):
    n = pl.program_id(1)
    r = lax.broadcasted_iota(jnp.int32, (ATTN_BLK, ATTN_BLK), 0)
    jj = lax.broadcasted_iota(jnp.int32, (ATTN_BLK, ATTN_BLK), 1)
    far = jnp.int32(2 * ATTN_BLK)
    off_prev = jnp.where(n > 0, 0, far)
    off_next = jnp.where(n < S // (ATTN_QB * ATTN_BLK) - 1, 0, far)
    prev_ok = jnp.where(jj - r >= 0, 0.0, NEG_INF)
    next_ok = jnp.where(r - jj >= 0, 0.0, NEG_INF)
    prev_edge = jnp.where(jj - r - off_prev >= 0, 0.0, NEG_INF)
    next_edge = jnp.where(r - jj - off_next >= 0, 0.0, NEG_INF)
    ctx = ctx_ref[...]
    rows = lambda b: slice(b * ATTN_BLK, (b + 1) * ATTN_BLK)
    blocks = [prev_ref[...]] + [cur_ref[rows(b), :] for b in range(ATTN_QB)] + [next_ref[...]]
    jobs = [(q_ref[rows(b), :], [ctx] + blocks[b:b + 3],
             [None, prev_edge if b == 0 else prev_ok, None, next_edge if b == ATTN_QB - 1 else next_ok])
            for b in range(ATTN_QB)]
    for b, out in enumerate(_attend_many(jobs, sink_ref)):
        o_ref[rows(b), :] = out


def _attn_context_kernel(sink_ref, q_ref, ctx_ref, o_ref):
    o_ref[...] = _attend_many([(q_ref[...], [ctx_ref[...]], [None])], sink_ref)[0]


def _attn_call(sink_l, q, kv):
    nb = S // ATTN_BLK
    nq = nb // ATTN_QB
    smem = pl.BlockSpec(memory_space=pltpu.SMEM)
    pair = lambda w: pl.BlockSpec((ATTN_QB * ATTN_BLK, w), lambda b, n: (b * nq + n, 0))
    prev = pl.BlockSpec((ATTN_BLK, 2 * KV_W), lambda b, n: (b * nb + jnp.maximum(ATTN_QB * n - 1, 0), 0))
    nxt = pl.BlockSpec((ATTN_BLK, 2 * KV_W),
                       lambda b, n: (b * nb + jnp.minimum(ATTN_QB * (n + 1), nb - 1), 0))
    ctxs = pl.BlockSpec((C, 2 * KV_W), lambda b, n: (N_LAT // C + b, 0))
    lat = pl.pallas_call(
        _attn_latent_kernel,
        out_shape=jax.ShapeDtypeStruct((N_LAT, Q_W), BF16),
        grid=(B, nq),
        in_specs=[smem, pair(Q_W), prev, pair(2 * KV_W), nxt, ctxs],
        out_specs=pair(Q_W),
        compiler_params=_cparams(("parallel", "parallel")),
        name="attn_latent",
    )(sink_l, q, kv, kv, kv, kv)
    ncb = C // ATTN_BLK
    base = N_LAT // ATTN_BLK
    ctx = pl.pallas_call(
        _attn_context_kernel,
        out_shape=jax.ShapeDtypeStruct((N_CTX, Q_W), BF16),
        grid=(B, ncb),
        in_specs=[smem, pl.BlockSpec((ATTN_BLK, Q_W), lambda b, n: (base + b * ncb + n, 0)), ctxs],
        out_specs=pl.BlockSpec((ATTN_BLK, Q_W), lambda b, n: (b * ncb + n, 0)),
        compiler_params=_cparams(("parallel", "parallel")),
        name="attn_context",
    )(sink_l, q, kv)
    return lat, ctx


def _f1_kernel(w_ref, f_ref, re_ref, im_ref):
    res = lax.dot_general(w_ref[...], f_ref[...], (((1,), (0,)), ((), ())), preferred_element_type=F32)
    re_ref[...] = res[:FS1].astype(BF16)
    im_ref[...] = res[FS1:].astype(BF16)


def _f2_kernel(ta_ref, tb_ref, re_ref, im_ref, after_ref, o_ref):
    del after_ref
    for i in range(F2_K1):
        res = _dot(ta_ref[i], re_ref[i]) + _dot(tb_ref[i], im_ref[i])
        o_ref[i, :, 0:FOURIER_W] = res[:FS2].astype(BF16)
        o_ref[i, :, FOURIER_W:2 * FOURIER_W] = res[FS2:].astype(BF16)


def _fc_kernel(w_ref, f_ref, o_ref):
    res = _dot(w_ref[...], f_ref[...])
    o_ref[:, 0:FOURIER_W] = res[:C].astype(BF16)
    o_ref[:, FOURIER_W:2 * FOURIER_W] = res[C:].astype(BF16)


def _fourier_stage1_call(f, tabs):
    w1 = tabs[0]
    f3 = f.reshape(N_TOK // FS2, FS2, FOURIER_W)
    blk = pl.BlockSpec((FS1, F1_ROWS, FOURIER_W), lambda b, j: (b, j, 0))
    return pl.pallas_call(
        _f1_kernel,
        out_shape=[jax.ShapeDtypeStruct((B * FS1, FS2, FOURIER_W), BF16)] * 2,
        grid=(B, FS2 // F1_ROWS),
        in_specs=[_const_spec((2 * FS1, FS1)), blk],
        out_specs=[blk, blk],
        compiler_params=_cparams(("parallel", "parallel")),
        name="fourier_stage1",
    )(w1, f3)


def _fourier_stage2_call(a_re, a_im, f, tabs, after):
    _, ta, tb, wc = tabs
    nk = FS1 // F2_K1
    aspec = pl.BlockSpec((F2_K1, FS2, FOURIER_W), lambda b, k1: (b * nk + k1, 0, 0))
    tspec = pl.BlockSpec((F2_K1, 2 * FS2, FS2), lambda b, k1: (k1, 0, 0))
    h_t = pl.pallas_call(
        _f2_kernel,
        out_shape=jax.ShapeDtypeStruct((B, FS1, FS2, 2 * FOURIER_W), BF16),
        grid=(B, nk),
        in_specs=[tspec, tspec, aspec, aspec, pl.BlockSpec(memory_space=pl.ANY)],
        out_specs=pl.BlockSpec((None, F2_K1, FS2, 2 * FOURIER_W), lambda b, k1: (b, k1, 0, 0)),
        compiler_params=_cparams(("parallel", "parallel")),
        name="fourier_stage2",
    )(ta, tb, a_re, a_im, after)
    h_lat = jnp.transpose(h_t, (0, 2, 1, 3)).reshape(N_LAT, 2 * FOURIER_W)
    h_ctx = pl.pallas_call(
        _fc_kernel,
        out_shape=jax.ShapeDtypeStruct((N_CTX, 2 * FOURIER_W), BF16),
        grid=(B,),
        in_specs=[_const_spec((2 * C, C)),
                  pl.BlockSpec((C, FOURIER_W), lambda b: (N_LAT // C + b, 0))],
        out_specs=pl.BlockSpec((C, 2 * FOURIER_W), lambda b: (b, 0)),
        compiler_params=_cparams(("parallel",)),
        name="fourier_context",
    )(wc, f)
    return h_lat, h_ctx


def _poolconv_kernel(pc_ref, pp_ref, pn_ref, uc_ref, up_ref, un_ref, bm_ref, bh_ref, ic_ref, pw_ref, ps_ref,
                     sh_ref, cw_ref, cb_ref, cg_ref, cnb_ref, z_ref, a_ref, win0_ref, win1_ref, cv_ref):
    t = pl.program_id(0)
    lat_tiles = N_LAT // TP
    per_seq = S // TP
    is_ctx = t >= lat_tiles
    first = jnp.logical_or(t % per_seq == 0, is_ctx)
    last = jnp.logical_or(t % per_seq == per_seq - 1, is_ctx)

    keep_prev = jnp.where(first, 0.0, 1.0)
    keep_next = jnp.where(last, 0.0, 1.0)

    ub = jnp.concatenate([(up_ref[...].astype(F32) * keep_prev).astype(BF16), uc_ref[...],
                          (un_ref[...].astype(F32) * keep_next).astype(BF16)], axis=0)
    off = HALO - CONV_K // 2
    half_rows = TP // 2
    wins = (win0_ref, win1_ref)
    for hf, win_ref in enumerate(wins):
        window = ub[hf * half_rows:hf * half_rows + CONV_WIN]
        win_ref[0] = window.astype(F32)
        for s in range(1, 8):
            win_ref[s] = _dot(sh_ref[s - 1], window)

    pcur = pc_ref[...]
    halo = jnp.concatenate([pp_ref[...].astype(F32) * keep_prev,
                            pn_ref[...].astype(F32) * keep_next], axis=0).astype(BF16)
    sums = []
    for gi in range(len(POOL_WINDOWS)):
        cs = slice(gi * LANES, (gi + 2) * LANES)
        sums.append(_dot(bm_ref[gi], pcur[:, cs]) + _dot(bh_ref[gi], halo[:, cs]))
    lane_t = lax.broadcasted_iota(jnp.int32, (TP, LANES), 1)
    tiles = [sums[0][:, :LANES]]
    for gi in range(1, len(POOL_WINDOWS)):
        split = gi * POOL_GROUP_W - gi * LANES
        tiles.append(jnp.where(lane_t < split, sums[gi - 1][:, LANES:], sums[gi][:, :LANES]))
    tiles.append(sums[-1][:, LANES:])
    zsum = jnp.concatenate(tiles, axis=1)
    z = zsum * ic_ref[...] - pcur.astype(F32)
    z_ref[...] = (_dot(z.astype(BF16), pw_ref[...]) * ps_ref[...]).astype(BF16)

    for hf, win_ref in enumerate(wins):
        base = hf * half_rows
        for cb in range(CONV_W // LANES):
            cs = slice(cb * LANES, (cb + 1) * LANES)
            acc = jnp.zeros((half_rows, LANES), F32) + cb_ref[:, cs]
            for j in range(CONV_K):
                s, m = (off + j) % 8, (off + j) // 8
                acc = acc + win_ref[s, 8 * m:8 * m + half_rows, cs] * cw_ref[j:j + 1, cs]
            cv_ref[base:base + half_rows, cs] = acc
    cv = cv_ref[...]
    mu = jnp.mean(cv, axis=-1, keepdims=True)
    var = jnp.mean(jnp.square(cv - mu), axis=-1, keepdims=True)
    un = (cv - mu) * lax.rsqrt(var + EPS) * cg_ref[...] + cnb_ref[...]
    a_ref[...] = (un * _sigmoid(un)).astype(BF16)


def _poolconv_call(p, u, band_main, band_halo, inv_cnt, pw_bd, pool_scale, shifts, conv_w, conv_b, cn_g, cn_b):
    nh = TP // HALO
    last_h = N_TOK // HALO - 1
    cur = lambda w: pl.BlockSpec((TP, w), lambda t: (t, 0))
    prv = lambda w: pl.BlockSpec((HALO, w), lambda t: (jnp.maximum(t * nh - 1, 0), 0))
    nxt = lambda w: pl.BlockSpec((HALO, w), lambda t: (jnp.minimum((t + 1) * nh, last_h), 0))
    per_seq = S // TP

    def kind(t):
        return jnp.where(t >= N_LAT // TP, 3, jnp.where(t % per_seq == 0, 1, jnp.where(t % per_seq == per_seq - 1, 2, 0)))

    return pl.pallas_call(
        _poolconv_kernel,
        out_shape=[jax.ShapeDtypeStruct((N_TOK, POOL_W), BF16),
                   jax.ShapeDtypeStruct((N_TOK, CONV_W), BF16)],
        grid=(NTP,),
        in_specs=[cur(POOL_W), prv(POOL_W), nxt(POOL_W), cur(CONV_W), prv(CONV_W), nxt(CONV_W),
                  _const_spec((4, TP, TP)), _const_spec((4, TP, 2 * HALO)),
                  pl.BlockSpec((None, TP, POOL_W), lambda t: (kind(t), 0, 0)),
                  _const_spec((POOL_W, POOL_W)), _const_spec((1, POOL_W)),
                  _const_spec((7, CONV_WIN, CONV_WIN)),
                  _const_spec((CONV_K, CONV_W)), _const_spec((1, CONV_W)),
                  _const_spec((1, CONV_W)), _const_spec((1, CONV_W))],
        out_specs=[cur(POOL_W), cur(CONV_W)],
        scratch_shapes=[pltpu.VMEM((8, CONV_WIN, CONV_W), F32), pltpu.VMEM((8, CONV_WIN, CONV_W), F32),
                        pltpu.VMEM((TP, CONV_W), F32)],
        compiler_params=_cparams(("parallel",)),
        name="pool_conv",
    )(p, p, p, u, u, u, band_main, band_halo, inv_cnt, pw_bd, pool_scale, shifts, conv_w, conv_b, cn_g, cn_b)


def _route(logits):
    lane = lax.broadcasted_iota(jnp.int32, logits.shape, 1)
    big = jnp.int32(LANES)
    lg = jnp.where(lane < N_GROUPS, logits, NEG_INF)
    mg = jnp.max(lg, axis=-1, keepdims=True)
    grp = jnp.min(jnp.where(lg == mg, lane, big), axis=-1, keepdims=True)
    p_grp = 1.0 / jnp.sum(jnp.exp(lg - mg), axis=-1, keepdims=True)
    lo = N_GROUPS + grp * EPG
    le = jnp.where((lane >= lo) & (lane < lo + EPG), logits, NEG_INF)
    m1 = jnp.max(le, axis=-1, keepdims=True)
    i1 = jnp.min(jnp.where(le == m1, lane, big), axis=-1, keepdims=True)
    le2 = jnp.where(lane == i1, NEG_INF, le)
    m2 = jnp.max(le2, axis=-1, keepdims=True)
    i2 = jnp.min(jnp.where(le2 == m2, lane, big), axis=-1, keepdims=True)
    r = jnp.exp(m2 - m1)
    w1 = p_grp / (1.0 + r)
    w2 = p_grp * r / (1.0 + r)
    e1 = (i1 - N_GROUPS).astype(F32)
    e2 = (i2 - N_GROUPS).astype(F32)
    return jnp.where(lane == 0, e1, jnp.where(lane == 1, e2, jnp.where(lane == 2, w1,
                     jnp.where(lane == 3, w2, 0.0))))


def _mix_kernel(xl_ref, xc_ref, mod_ref, gn_ref, al_ref, ac_ref, hl_ref, hc_ref, z_ref, cv_ref,
                wa_ref, wf_ref, wp_ref, wc_ref, wg_ref, bg_ref, wo_ref, gf_ref, rh_ref, rl_ref, rb_ref,
                xo_ref, h2_ref, rt_ref, *, split):
    is_ctx = pl.program_id(0) >= LAT_TILES
    m = mod_ref[...]
    x = _stream_tile(xl_ref, xc_ref, split)
    hb = _modulate(x, gn_ref[...], m[0:1], m[1:2]).astype(BF16)
    attn = jnp.where(is_ctx, ac_ref[...], al_ref[...])
    four = jnp.where(is_ctx, hc_ref[...], hl_ref[...])
    branches = ((attn, wa_ref), (four, wf_ref), (z_ref[...], wp_ref), (cv_ref[...], wc_ref))
    acc = None
    for bi, (inp, w_ref) in enumerate(branches):
        cs = slice(bi * D, (bi + 1) * D)
        gate = _sigmoid(_dot(hb, wg_ref[:, cs]) + bg_ref[:, cs])
        term = gate * _dot(inp, w_ref[...])
        acc = term if acc is None else acc + term
    x_new = x + m[2:3] * _dot(acc.astype(BF16), wo_ref[...])
    xo_ref[...] = x_new
    h2 = _modulate(x_new, gf_ref[...], m[3:4], m[4:5])
    hi = h2.astype(BF16)
    h2_ref[...] = hi
    lo = (h2 - hi.astype(F32)).astype(BF16)
    logits = _dot(hi, rh_ref[...]) + _dot(lo, rh_ref[...]) + _dot(hi, rl_ref[...]) + rb_ref[...]
    rt_ref[...] = _route(logits)


def _mix_call(x_lat, x_ctx, split, mods_l, gn, a_lat, a_ctx, h_lat, h_ctx, zc, cact, l, stacked, small):
    tok = lambda w: pl.BlockSpec((TM, w), lambda i: (i, 0))
    lat = lambda w: pl.BlockSpec((TM, w), lambda i: (jnp.minimum(i, LAT_TILES - 1), 0))
    wa, wf, wp, wc, wg, wo, rh, rl = stacked
    bg, gf, rb = small
    in_specs = _stream_specs(split) + [
                pl.BlockSpec((None, 6, D), lambda i: (_mod_row(i), 0, 0)), _const_spec((1, D)),
                lat(Q_W), _const_spec((N_CTX, Q_W)),
                lat(2 * FOURIER_W), _const_spec((N_CTX, 2 * FOURIER_W)),
                tok(POOL_W), tok(CONV_W)]
    in_specs += [_layer_spec(w.shape[1:], l) for w in (wa, wf, wp, wc, wg)]
    in_specs += [_const_spec(bg.shape), _layer_spec(wo.shape[1:], l), _const_spec(gf.shape),
                 _layer_spec(rh.shape[1:], l), _layer_spec(rl.shape[1:], l), _const_spec(rb.shape)]
    return pl.pallas_call(
        functools.partial(_mix_kernel, split=split),
        out_shape=[jax.ShapeDtypeStruct((N_TOK, D), F32), jax.ShapeDtypeStruct((N_TOK, D), BF16),
                   jax.ShapeDtypeStruct((N_TOK, LANES), F32)],
        grid=(NT,),
        in_specs=in_specs,
        out_specs=[tok(D), tok(D), tok(LANES)],
        compiler_params=_cparams(("parallel",)),
        name="mix",
    )(x_lat, x_ctx, mods_l, gn, a_lat, a_ctx, h_lat, h_ctx, zc, cact, wa, wf, wp, wc, wg, bg, wo, gf, rh, rl, rb)


def _onehots(route):
    lane = lax.broadcasted_iota(jnp.int32, route.shape, 1)
    e1 = route[:, 0:1].astype(jnp.int32)
    e2 = route[:, 1:2].astype(jnp.int32)
    return (lane == e1).astype(F32), (lane == e2).astype(F32)


def _lane_cumsum(row):
    lane = lax.broadcasted_iota(jnp.int32, row.shape, 1)
    sh = 1
    while sh < N_EXPERTS:
        row = row + jnp.where(lane >= sh, pltpu.roll(row, sh, 1), 0.0)
        sh *= 2
    return row


def _rank_kernel(rt_ref, tri_ref, pos_ref, meta_ref, cnt_ref, carry_ref):
    @pl.when(pl.program_id(0) == 0)
    def _():
        carry_ref[...] = jnp.zeros_like(carry_ref)

    tiles = []
    for t in range(RANK_TILES):
        oh1, oh2 = _onehots(rt_ref[t * TM:(t + 1) * TM, :])
        both = oh1 + oh2
        tiles.append((oh1, oh2, both, _dot(tri_ref[...], both.astype(BF16))))
    carry = carry_ref[0:1, :]
    lane = lax.broadcasted_iota(jnp.int32, (TM, LANES), 1)
    row = lax.broadcasted_iota(jnp.int32, meta_ref.shape[1:], 0)
    for t, (oh1, oh2, both, before) in enumerate(tiles):
        tile_cnt = jnp.sum(both, axis=0, keepdims=True)
        tile_cnt = tile_cnt + (tile_cnt - 2.0 * jnp.floor(tile_cnt * 0.5))
        tile_off = _lane_cumsum(tile_cnt) - tile_cnt
        where = before + tile_off
        p1 = jnp.sum(oh1 * where, axis=-1, keepdims=True)
        p2 = jnp.sum(oh2 * where, axis=-1, keepdims=True)
        pos_ref[t * TM:(t + 1) * TM, :] = jnp.where(lane == 0, p1, jnp.where(lane == 1, p2, 0.0))
        meta_ref[t] = jnp.where(row == 0, tile_off, jnp.where(row == 1, tile_cnt, jnp.where(row == 2, carry, 0.0)))
        carry = carry + tile_cnt
    carry_ref[...] = jnp.broadcast_to(carry, carry_ref.shape)
    cnt_ref[...] = jnp.broadcast_to(carry, cnt_ref.shape)


def _runs_kernel(meta_ref, cnt_ref, runs_ref, be_ref):
    lane = lax.broadcasted_iota(jnp.int32, (1, LANES), 1)
    counts = cnt_ref[0:1, :]
    padded = jnp.floor((counts + (MOE_BLK - 1)) * (1.0 / MOE_BLK)) * MOE_BLK
    padded = jnp.where(lane < N_EXPERTS, padded, 0.0)
    ends = _lane_cumsum(padded)
    starts = ends - padded
    for t in range(NT):
        m = meta_ref[t]
        row = lax.broadcasted_iota(jnp.int32, m.shape, 0)
        m = jnp.where(row == 2, m + starts, m)
        m = jnp.where(row == 3, starts + counts, jnp.where(row == 4, padded - counts, m))
        runs_ref[t] = m.astype(jnp.int32)
    blk = lax.broadcasted_iota(jnp.int32, be_ref.shape, 0).astype(F32) * MOE_BLK
    lane_b = lax.broadcasted_iota(jnp.int32, be_ref.shape, 1)
    done = jnp.where((ends <= blk) & (lane_b < N_EXPERTS), 1.0, 0.0)
    be = jnp.minimum(jnp.sum(done, axis=-1, keepdims=True), N_EXPERTS - 1.0)
    nblk = jnp.max(jnp.where(lane_b == N_EXPERTS - 1, ends, 0.0), axis=-1, keepdims=True) * (1.0 / MOE_BLK)
    be_ref[...] = jnp.where(lane_b == 0, be, jnp.where(lane_b == 1, nblk, 0.0)).astype(jnp.int32)


def _plan_call(route, tri):
    tok = pl.BlockSpec((RANK_TILES * TM, LANES), lambda i: (i, 0))
    pos, meta, counts = pl.pallas_call(
        _rank_kernel,
        out_shape=[jax.ShapeDtypeStruct((N_TOK, LANES), F32), jax.ShapeDtypeStruct((NT, 8, LANES), F32),
                   jax.ShapeDtypeStruct((8, LANES), F32)],
        grid=(NT // RANK_TILES,),
        in_specs=[tok, _const_spec((TM, TM))],
        out_specs=[tok, pl.BlockSpec((RANK_TILES, 8, LANES), lambda i: (i, 0, 0)),
                   pl.BlockSpec((8, LANES), lambda i: (0, 0))],
        scratch_shapes=[pltpu.VMEM((8, LANES), F32)],
        compiler_params=_cparams(("arbitrary",)),
        name="moe_rank",
    )(route, tri)
    runs, blk = pl.pallas_call(
        _runs_kernel,
        out_shape=[jax.ShapeDtypeStruct((NT, 8, LANES), jnp.int32),
                   jax.ShapeDtypeStruct((BLK_TABLE_ROWS, LANES), jnp.int32)],
        name="moe_runs",
    )(meta, counts)
    runs_flat = runs[:, 0:RUN_FIELDS, 0:N_EXPERTS].reshape(NT, 1, RUN_FIELDS * N_EXPERTS)
    return pos, runs_flat, blk[:N_MOE_BLOCKS, 0], blk[0:1, 1]


def _pack_pairs(x):
    half = x.shape[1] // 2
    lo = pltpu.bitcast(x[:, :half], jnp.uint32)
    hi = pltpu.bitcast(x[:, half:], jnp.uint32)
    return (lo >> 16) | (hi & jnp.uint32(0xFFFF0000))


def _unpack_pairs(w):
    lo = pltpu.bitcast(w << 16, F32)
    hi = pltpu.bitcast(w & jnp.uint32(0xFFFF0000), F32)
    return jnp.concatenate([lo, hi], axis=1).astype(BF16)


def _run_fields(runs_ref, e):
    return runs_ref[0, e], runs_ref[0, N_EXPERTS + e], runs_ref[0, 2 * N_EXPERTS + e]


def _store_rows(lin_ref, packed):
    rows = packed.shape[0]
    for c in range(PK):
        lin_ref[pl.ds(c, rows, stride=PK), :] = packed[:, c * LANES:(c + 1) * LANES]


def _load_rows(lin_ref, rows):
    return jnp.concatenate([lin_ref[pl.ds(c, rows, stride=PK), :] for c in range(PK)], axis=1)


def _lin(ref, row, nrows):
    return ref.at[pl.ds(pl.multiple_of(row * PK, SUBLANES), nrows * PK), :]


FETCH_ROWS = 64
RUN_PIECES = (32, 16, 8, 4, 2)
TAIL_PIECES = (256, 128, 64, 32, 16, 8, 4, 2)


def _run_copies(runs, make_copy, act):
    def per_expert(e, carry):
        off, n, dst = _run_fields(runs, e)
        whole = n // FETCH_ROWS

        def chunk(k, c):
            act(make_copy(off + k * FETCH_ROWS, dst + k * FETCH_ROWS, FETCH_ROWS))
            return c

        lax.fori_loop(0, whole, chunk, 0)
        done = whole * FETCH_ROWS
        for k, size in enumerate(RUN_PIECES):
            @pl.when((n & size) != 0)
            def _(done=done, size=size, k=k):
                act(make_copy(off + done, dst + done, size), k)
            done = done + (n & size)
        return carry

    lax.fori_loop(0, N_EXPERTS, per_expert, 0)


def _rows_wait(runs, make_copy):
    total = lax.fori_loop(0, N_EXPERTS, lambda e, acc: acc + runs[0, N_EXPERTS + e], jnp.int32(0))
    lax.fori_loop(0, total // FETCH_ROWS, lambda k, c: (make_copy(0, 0, FETCH_ROWS).wait(), c)[1], 0)
    for size in RUN_PIECES:
        @pl.when((total & size) != 0)
        def _(size=size):
            make_copy(0, 0, size).wait()


def _dispatch_kernel(runs_ref, prev_runs_ref, pos_ref, h2_ref, xs_ref, buf0, buf1, zero_ref, sem0, sem1, zsem,
                     ssem):
    i = pl.program_id(0)
    last = NT - 1

    def tail_copies(act):
        def per_expert(e, carry):
            row = runs_ref[0, 3 * N_EXPERTS + e]
            n = runs_ref[0, 4 * N_EXPERTS + e]
            done = jnp.int32(0)
            for size in TAIL_PIECES:
                @pl.when((n & size) != 0)
                def _(done=done, size=size):
                    act(pltpu.make_async_copy(zero_ref.at[pl.ds(0, size * PK), :], _lin(xs_ref, row + done, size), zsem))
                done = done + (n & size)
            return carry

        lax.fori_loop(0, N_EXPERTS, per_expert, 0)

    def spare_copies(act):
        used = runs_ref[0, 3 * N_EXPERTS + N_EXPERTS - 1] + runs_ref[0, 4 * N_EXPERTS + N_EXPERTS - 1]

        def spare_block(k, carry):
            act(pltpu.make_async_copy(zero_ref.at[pl.ds(0, MOE_BLK * PK), :],
                                      _lin(xs_ref, used + k * MOE_BLK, MOE_BLK), ssem))
            return carry

        lax.fori_loop(0, N_MOE_BLOCKS - used // MOE_BLK, spare_block, 0)

    start = lambda d, k=0: d.start(priority=k % 2)
    wait = lambda d: d.wait()

    @pl.when(i == 0)
    def _():
        zero_ref[...] = jnp.zeros_like(zero_ref)
        tail_copies(start)
        spare_copies(start)

    pos = pos_ref[...]
    col = lax.broadcasted_iota(jnp.int32, (TM, TS), 1).astype(F32)
    sel = jnp.where((col == pos[:, 0:1]) | (col == pos[:, 1:2]), 1.0, 0.0).astype(BF16)
    srt = lax.dot_general(sel, h2_ref[...], (((0,), (0,)), ((), ())), preferred_element_type=F32)
    packed = _pack_pairs(srt)

    for par, (buf, sem, obuf, osem) in enumerate(((buf0, sem0, buf1, sem1), (buf1, sem1, buf0, sem0))):
        @pl.when(i % 2 == par)
        def _(buf=buf, sem=sem, obuf=obuf, osem=osem):
            copy = lambda off, dst, rows: pltpu.make_async_copy(_lin(buf, off, rows), _lin(xs_ref, dst, rows), sem)
            ocopy = lambda off, dst, rows: pltpu.make_async_copy(_lin(obuf, off, rows), _lin(xs_ref, dst, rows), osem)
            _store_rows(buf, packed)

            @pl.when(i == 0)
            def _():
                tail_copies(wait)

            _run_copies(runs_ref, copy, start)

            @pl.when(i > 0)
            def _():
                _rows_wait(prev_runs_ref, ocopy)

            @pl.when(i == last)
            def _():
                _rows_wait(runs_ref, copy)
                spare_copies(wait)


def _dispatch_call(runs_flat, pos, h2):
    runs_spec = lambda shift: pl.BlockSpec((None, 1, RUN_FIELDS * N_EXPERTS), lambda i: (jnp.maximum(i - shift, 0), 0, 0),
                                           memory_space=pltpu.SMEM)
    sorted_buf = pltpu.VMEM((TS * PK, LANES), jnp.uint32)
    return pl.pallas_call(
        _dispatch_kernel,
        out_shape=jax.ShapeDtypeStruct((N_SLOTS * PK, LANES), jnp.uint32),
        grid=(NT,),
        in_specs=[runs_spec(0), runs_spec(1),
                  pl.BlockSpec((TM, LANES), lambda i: (i, 0)),
                  pl.BlockSpec((TM, D), lambda i: (i, 0))],
        out_specs=pl.BlockSpec(memory_space=pl.ANY),
        scratch_shapes=[sorted_buf, sorted_buf, pltpu.VMEM((MOE_BLK * PK, LANES), jnp.uint32),
                        pltpu.SemaphoreType.DMA, pltpu.SemaphoreType.DMA, pltpu.SemaphoreType.DMA,
                        pltpu.SemaphoreType.DMA],
        compiler_params=_cparams(("arbitrary",)),
        name="moe_dispatch",
    )(runs_flat, runs_flat, pos, h2)


def _expert_kernel(be_ref, nu_ref, xs_ref, wg_ref, wu_ref, wd_ref, ys_ref, wgb_ref, wub_ref, wdb_ref):
    b = pl.program_id(0)

    @pl.when(jnp.logical_or(b == 0, be_ref[b] != be_ref[jnp.maximum(b - 1, 0)]))
    def _():
        wgb_ref[...] = wg_ref[...].astype(BF16)
        wub_ref[...] = wu_ref[...].astype(BF16)
        wdb_ref[...] = wd_ref[...].astype(BF16)

    @pl.when(b < nu_ref[0])
    def _():
        xb = _unpack_pairs(_load_rows(xs_ref, MOE_BLK))
        g = _dot(xb, wgb_ref[...])
        u = _dot(xb, wub_ref[...])
        hmid = (g * _sigmoid(g)) * u
        y = _dot(hmid.astype(BF16), wdb_ref[...])
        _store_rows(ys_ref, _pack_pairs(y.astype(BF16).astype(F32)))

    @pl.when(b >= nu_ref[0])
    def _():
        ys_ref[...] = jnp.zeros_like(ys_ref)


def _expert_call(blk_e, n_used, xs, wg, wu, wd, l):
    wspec = lambda k, n: pl.BlockSpec((None, None, k, n), lambda b, be, nu: (l, be[b], 0, 0))
    return pl.pallas_call(
        _expert_kernel,
        out_shape=jax.ShapeDtypeStruct((N_SLOTS * PK, LANES), jnp.uint32),
        grid_spec=pltpu.PrefetchScalarGridSpec(
            num_scalar_prefetch=2,
            grid=(N_MOE_BLOCKS,),
            in_specs=[pl.BlockSpec((MOE_BLK * PK, LANES), lambda b, be, nu: (jnp.minimum(b, nu[0] - 1), 0)),
                      wspec(D, EXPERT_HIDDEN), wspec(D, EXPERT_HIDDEN), wspec(EXPERT_HIDDEN, D)],
            out_specs=pl.BlockSpec((MOE_BLK * PK, LANES), lambda b, be, nu: (b, 0)),
            scratch_shapes=[pltpu.VMEM((D, EXPERT_HIDDEN), BF16), pltpu.VMEM((D, EXPERT_HIDDEN), BF16),
                            pltpu.VMEM((EXPERT_HIDDEN, D), BF16)]),
        compiler_params=_cparams(("arbitrary",)),
        name="moe_experts",
    )(blk_e, n_used, xs, wg, wu, wd)


def _combine_kernel(runs_ref, next_runs_ref, ys_ref, pos_ref, x_ref, rt_ref, mod_ref, o_ref, buf0, buf1, sem0, sem1,
                    *, n_tiles):
    i = pl.program_id(0)
    last = n_tiles - 1
    start = lambda d, k=0: d.start(priority=k % 2)
    fetch = lambda buf, sem: (
        lambda off, dst, rows: pltpu.make_async_copy(_lin(ys_ref, dst, rows), _lin(buf, off, rows), sem))

    @pl.when(i == 0)
    def _():
        buf0[...] = jnp.zeros_like(buf0)
        buf1[...] = jnp.zeros_like(buf1)
        _run_copies(runs_ref, fetch(buf0, sem0), start)

    pos = pos_ref[...]
    rt = rt_ref[...]
    col = lax.broadcasted_iota(jnp.int32, (TM, TS), 1).astype(F32)
    pick = (jnp.where(col == pos[:, 0:1], rt[:, TOP_K:TOP_K + 1], 0.0)
            + jnp.where(col == pos[:, 1:2], rt[:, TOP_K + 1:TOP_K + 2], 0.0)).astype(BF16)

    for par, (buf, sem, obuf, osem) in enumerate(((buf0, sem0, buf1, sem1), (buf1, sem1, buf0, sem0))):
        @pl.when(i % 2 == par)
        def _(buf=buf, sem=sem, obuf=obuf, osem=osem):
            @pl.when(i < last)
            def _():
                _run_copies(next_runs_ref, fetch(obuf, osem), start)

            _rows_wait(runs_ref, fetch(buf, sem))
            ysb = _unpack_pairs(_load_rows(buf, TS))
            o_ref[...] = x_ref[...] + mod_ref[5:6, :] * _dot(pick, ysb)


def _combine_call(runs_flat, ys, pos, x, route, mods_l, n_tiles):
    tok = lambda w: pl.BlockSpec((TM, w), lambda i: (i, 0))
    runs_spec = lambda shift: pl.BlockSpec((None, 1, RUN_FIELDS * N_EXPERTS),
                                           lambda i: (jnp.minimum(i + shift, n_tiles - 1), 0, 0),
                                           memory_space=pltpu.SMEM)
    sorted_buf = pltpu.VMEM((TS * PK, LANES), jnp.uint32)
    return pl.pallas_call(
        functools.partial(_combine_kernel, n_tiles=n_tiles),
        out_shape=jax.ShapeDtypeStruct((n_tiles * TM, D), F32),
        grid=(n_tiles,),
        in_specs=[runs_spec(0), runs_spec(1),
                  pl.BlockSpec(memory_space=pl.ANY),
                  tok(LANES), tok(D), tok(LANES),
                  pl.BlockSpec((None, 6, D), lambda i: (_mod_row(i), 0, 0))],
        out_specs=tok(D),
        scratch_shapes=[sorted_buf, sorted_buf, pltpu.SemaphoreType.DMA, pltpu.SemaphoreType.DMA],
        compiler_params=_cparams(("arbitrary",)),
        name="moe_combine",
    )(runs_flat, runs_flat, ys, pos, x, route, mods_l)


def _rope_tables():
    nf = HEAD_DIM // 4
    inv = ROPE_BASE ** (-jnp.arange(nf, dtype=F32) / nf)
    t = jnp.arange(S)
    row = (t // GRID_W).astype(F32)[:, None] * inv[None, :]
    col = (t % GRID_W).astype(F32)[:, None] * inv[None, :]
    zero = jnp.zeros_like(row)
    cos = jnp.concatenate([jnp.cos(row), jnp.cos(row), jnp.cos(col), jnp.cos(col)], axis=1)
    sa = jnp.concatenate([-jnp.sin(row), zero, -jnp.sin(col), zero], axis=1)
    sb = jnp.concatenate([zero, jnp.sin(row), zero, jnp.sin(col)], axis=1)
    ident = (jnp.ones((TM, HEAD_DIM), F32), jnp.zeros((TM, HEAD_DIM), F32), jnp.zeros((TM, HEAD_DIM), F32))
    return tuple(jnp.tile(jnp.concatenate([a, b], axis=0), (1, LANES // HEAD_DIM))
                 for a, b in zip((cos, sa, sb), ident))


def _fourier_tables():
    s1 = np.arange(FS1)
    ang1 = 2.0 * np.pi * np.outer(s1, s1) / FS1
    w1 = np.concatenate([np.cos(ang1), -np.sin(ang1)], axis=0) / np.sqrt(S)
    k1 = np.arange(FS1)[:, None, None]
    k2 = np.arange(FS2)[None, :, None]
    s2 = np.arange(FS2)[None, None, :]
    ang2 = 2.0 * np.pi * ((k1 + FS1 * k2) * s2 % S) / S
    c2, sn2 = np.cos(ang2), np.sin(ang2)
    ta = np.concatenate([c2, -sn2], axis=1)
    tb = np.concatenate([sn2, c2], axis=1)
    sc = np.arange(C)
    angc = 2.0 * np.pi * np.outer(sc, sc) / C
    wc = np.concatenate([np.cos(angc), -np.sin(angc)], axis=0) / np.sqrt(C)
    return tuple(jnp.asarray(a, F32).astype(BF16) for a in (w1, ta, tb, wc))


def _channel_dft():
    cidx = np.arange(FOURIER_GROUP_W)
    ang = 2.0 * np.pi * np.outer(cidx, cidx) / FOURIER_GROUP_W
    eye = np.eye(FOURIER_W // FOURIER_GROUP_W)
    cw = np.kron(eye, np.cos(ang)) / np.sqrt(FOURIER_GROUP_W)
    sw = np.kron(eye, np.sin(ang)) / np.sqrt(FOURIER_GROUP_W)
    return jnp.asarray(np.concatenate([cw, sw], axis=0), F32)


def _pool_bands():
    t = np.arange(TP)[:, None]
    main, halo = [], []
    for w in POOL_WINDOWS:
        def hit(j):
            return ((j - t >= -(w // 2)) & (j - t <= w // 2 - 1)).astype(np.float32)
        main.append(hit(np.arange(TP)[None, :]))
        halo.append(np.concatenate([hit(np.arange(-HALO, 0)[None, :]),
                                    hit(np.arange(TP, TP + HALO)[None, :])], axis=1))
    return (jnp.asarray(np.stack(main), F32).astype(BF16), jnp.asarray(np.stack(halo), F32).astype(BF16))


def _pool_inv_counts():
    win = np.repeat(np.array(POOL_WINDOWS), POOL_GROUP_W)[None, :]

    def table(pos0, seq_len):
        pos = (pos0 + np.arange(TP))[:, None]
        lo = np.clip(pos - win // 2, 0, seq_len)
        hi = np.clip(pos - win // 2 + win, 0, seq_len)
        return 1.0 / (hi - lo)

    tabs = [table(TP, S), table(0, S), table(S - TP, S), table(0, C)]
    return jnp.asarray(np.stack(tabs), F32)


def _conv_shifts():
    i = np.arange(CONV_WIN)
    return jnp.asarray(np.stack([(i[None, :] == i[:, None] + s) for s in range(1, 8)]), F32).astype(BF16)


def _fold_kernel(a_ref, b_ref, o_ref):
    a, b = a_ref[...], b_ref[...]
    a_hi, b_hi = a.astype(BF16), b.astype(BF16)
    a_lo = (a - a_hi.astype(F32)).astype(BF16)
    b_lo = (b - b_hi.astype(F32)).astype(BF16)
    o_ref[...] = (_dot(a_hi, b_hi) + _dot(a_lo, b_hi) + _dot(a_hi, b_lo)).astype(BF16)


def _fold_fourier_weights(dftw, w_br_fourier):
    nl = w_br_fourier.shape[0]
    return pl.pallas_call(
        _fold_kernel,
        out_shape=jax.ShapeDtypeStruct((nl, 2 * FOURIER_W, D), BF16),
        grid=(nl,),
        in_specs=[pl.BlockSpec((2 * FOURIER_W, FOURIER_W), lambda l: (0, 0)),
                  pl.BlockSpec((None, FOURIER_W, D), lambda l: (l, 0, 0))],
        out_specs=pl.BlockSpec((None, 2 * FOURIER_W, D), lambda l: (l, 0, 0)),
        compiler_params=_cparams(("arbitrary",)),
        name="fold_fourier_proj",
    )(dftw, w_br_fourier)


def _block_diag(blocks):
    n, r, c = blocks.shape
    eye = jnp.eye(n, dtype=blocks.dtype)
    return (blocks[:, :, None, :] * eye[:, None, :, None]).reshape(n * r, n * c)


def kernel(x, c, ctx, c_ctx, w_ada, b_ada, g_norm_mix, g_norm_ffn, w_in, g_q, g_k, sink, w_br_attn,
           w_br_fourier, pool_w, pool_scale, w_br_pool, conv_w, conv_b, cn_g, cn_b, w_br_conv, w_gate,
           b_gate, w_out, w_router_grp, b_router_grp, w_router_exp, b_router_exp, w_e_gate, w_e_up,
           w_e_down):
    x_lat, x_ctx, split = x.reshape(N_LAT, D), ctx.reshape(N_CTX, D), True
    nl = w_ada.shape[0]
    mods = _ada_all(c, c_ctx, w_ada, b_ada).reshape(nl, 8, 6, D)
    rope_tabs = _rope_tables()
    four_tabs = _fourier_tables()
    band_main, band_halo = _pool_bands()
    inv_cnt = _pool_inv_counts()
    shifts = _conv_shifts()
    wf_all = _fold_fourier_weights(_channel_dft(), w_br_fourier)
    bd = jnp.asarray(np.kron(np.eye(LANES // HEAD_DIM), np.ones((HEAD_DIM, HEAD_DIM))), F32).astype(BF16)
    tri = jnp.asarray(np.tril(np.ones((TM, TM)), -1), F32).astype(BF16)
    rpad = jnp.zeros((nl, D, LANES - N_GROUPS - N_EXPERTS), F32)
    w_router = jnp.concatenate([w_router_grp, w_router_exp, rpad], axis=-1)
    r_hi = w_router.astype(BF16)
    r_lo = (w_router - r_hi.astype(F32)).astype(BF16)
    r_b = jnp.concatenate([b_router_grp, b_router_exp, rpad[:, 0, :]], axis=-1).reshape(nl, 1, LANES)
    stacked = tuple(w.astype(BF16) for w in (w_br_attn,)) + (wf_all,) + tuple(
        w.astype(BF16) for w in (w_br_pool, w_br_conv, w_gate, w_out)) + (r_hi, r_lo)

    for l in range(nl):
        mods_l = mods[l]
        gn = g_norm_mix[l].reshape(1, D)
        q, kv, f, p, u = _proj_call(x_lat, x_ctx, split, mods_l, gn, w_in, l, rope_tabs,
                                    jnp.tile(g_q[l], 2).reshape(1, LANES),
                                    jnp.tile(g_k[l], 2).reshape(1, LANES), bd)
        a_re, a_im = _fourier_stage1_call(f, four_tabs)
        a_lat, a_ctx = _attn_call(sink[l], q, kv)
        h_lat, h_ctx = _fourier_stage2_call(a_re, a_im, f, four_tabs, a_lat)
        zc, cact = _poolconv_call(p, u, band_main, band_halo, inv_cnt, _block_diag(pool_w[l]).astype(BF16),
                                  pool_scale[l].reshape(1, POOL_W), shifts, conv_w[l], conv_b[l].reshape(1, CONV_W),
                                  cn_g[l].reshape(1, CONV_W), cn_b[l].reshape(1, CONV_W))
        small = (b_gate[l].reshape(1, 4 * D), g_norm_ffn[l].reshape(1, D), r_b[l])
        xs, h2, route = _mix_call(x_lat, x_ctx, split, mods_l, gn, a_lat, a_ctx, h_lat, h_ctx, zc, cact, l,
                                  stacked, small)
        pos, runs_flat, blk_e, n_used = _plan_call(route, tri)
        slots = _dispatch_call(runs_flat, pos, h2)
        ys = _expert_call(blk_e, n_used, slots, w_e_gate, w_e_up, w_e_down, l)
        xs = _combine_call(runs_flat, ys, pos, xs, route, mods_l, LAT_TILES if l == nl - 1 else NT)
        x_lat, x_ctx, split = xs, xs, False
    return xs.reshape(B, S, D)
```

```python
import functools

import numpy as np
import jax
import jax.numpy as jnp
from jax import lax
from jax.experimental import pallas as pl
from jax.experimental.pallas import tpu as pltpu

F32 = jnp.float32
BF16 = jnp.bfloat16

D = 1024
B = 2
S = 8192
C = 256
GRID_W = 64
HEAD_DIM = 64
N_Q_HEADS = 8
N_KV_HEADS = 2
GQA = N_Q_HEADS // N_KV_HEADS
WINDOW = 128
ATTN_BLK = 128
ATTN_QB = 8
ROPE_BASE = 10000.0
Q_W = 512
KV_W = 128
FOURIER_W = 640
FOURIER_GROUP_W = 160
POOL_W = 640
POOL_GROUP_W = 160
POOL_WINDOWS = (2, 4, 8, 16)
CONV_W = 512
CONV_K = 31
PROJ_W = 3072
N_GROUPS = 4
EPG = 8
N_EXPERTS = 32
TOP_K = 2
EXPERT_HIDDEN = 512
MOE_BLK = 512
EPS = 1e-6
NEG_INF = -1e30
LOG2E = 1.4426950408889634

N_LAT = B * S
N_CTX = B * C
N_TOK = N_LAT + N_CTX
TM = 512
NT = N_TOK // TM
LAT_TILES = N_LAT // TM
TILES_PER_BATCH = S // TM
TP = 256
NTP = N_TOK // TP
HALO = 16
CONV_WIN = TP // 2 + 2 * HALO
N_ASSIGN = N_TOK * TOP_K
RANK_TILES = 3
RUN_FIELDS = 5
PK = D // 2 // 128
TS = 1152
N_MOE_BLOCKS = (N_ASSIGN + NT * N_EXPERTS + N_EXPERTS * (MOE_BLK - 1)) // MOE_BLK
N_SLOTS = N_MOE_BLOCKS * MOE_BLK
SUBLANES = 8
BLK_TABLE_ROWS = -(-N_MOE_BLOCKS // SUBLANES) * SUBLANES
ROPE_PAIR = HEAD_DIM // 4
FS1 = 64
FS2 = 128
F1_ROWS = 32
F2_K1 = 16
LANES = 128
VMEM_LIMIT = 56 * 1024 * 1024


def _cparams(sem, vmem=VMEM_LIMIT):
    return pltpu.CompilerParams(dimension_semantics=sem, vmem_limit_bytes=vmem)


def _const_spec(shape):
    nd = len(shape)
    return pl.BlockSpec(shape, lambda *_: (0,) * nd, pipeline_mode=pl.Buffered(1))


def _dot(a, b):
    return jnp.dot(a, b, preferred_element_type=F32)


def _modulate(x, g, shift, scale):
    y = x * lax.rsqrt(jnp.mean(x * x, axis=-1, keepdims=True) + EPS)
    return (y * g) * (1.0 + scale) + shift


def _sigmoid(x):
    return 1.0 / (1.0 + jnp.exp(-x))


def _ada_kernel(ct_ref, w_ref, b_ref, o_ref):
    ct = ct_ref[...]
    s = ct * _sigmoid(ct)
    w = w_ref[...]
    rows = [jnp.sum(w * s[:, r:r + 1], axis=0, keepdims=True) for r in range(3)]
    rows.append(jnp.zeros((5, w.shape[1]), F32))
    o_ref[...] = jnp.concatenate(rows, axis=0) + b_ref[...]


def _ada_all(c, c_ctx, w_ada, b_ada):
    ct = jnp.concatenate([c, c_ctx[None, :], jnp.zeros((5, D), F32)], axis=0).T
    cols = 1536
    nl = w_ada.shape[0]
    return pl.pallas_call(
        _ada_kernel,
        out_shape=jax.ShapeDtypeStruct((nl, 8, 6 * D), F32),
        grid=(nl, 6 * D // cols),
        in_specs=[pl.BlockSpec((D, 8), lambda l, j: (0, 0)),
                  pl.BlockSpec((None, D, cols), lambda l, j: (l, 0, j)),
                  pl.BlockSpec((None, 1, cols), lambda l, j: (l, 0, j))],
        out_specs=pl.BlockSpec((None, 8, cols), lambda l, j: (l, 0, j)),
        compiler_params=_cparams(("arbitrary", "arbitrary")),
        name="adaln",
    )(ct, w_ada, b_ada.reshape(nl, 1, 6 * D))


def _head_rms(t, g128, bd):
    outs = []
    for j in range(t.shape[1] // LANES):
        blk = t[:, j * LANES:(j + 1) * LANES]
        ss = _dot((blk * blk).astype(BF16), bd)
        outs.append(blk * lax.rsqrt(ss * (1.0 / HEAD_DIM) + EPS) * g128)
    return outs


def _rope(blocks, cos, sa, sb):
    outs = []
    for blk in blocks:
        up = pltpu.roll(blk, LANES - ROPE_PAIR, 1)
        dn = pltpu.roll(blk, ROPE_PAIR, 1)
        outs.append(blk * cos + up * sa + dn * sb)
    return outs


def _proj_kernel(xl_ref, xc_ref, mod_ref, gn_ref, w_ref, cos_ref, sa_ref, sb_ref, gq_ref, gk_ref, bd_ref,
                 q_ref, kv_ref, f_ref, p_ref, u_ref, wbf_ref, *, split):
    @pl.when(pl.program_id(0) == 0)
    def _():
        wbf_ref[...] = w_ref[...].astype(BF16)

    m = mod_ref[...]
    hb = _modulate(_stream_tile(xl_ref, xc_ref, split), gn_ref[...], m[0:1], m[1:2]).astype(BF16)
    cos, sa, sb, bd = cos_ref[...], sa_ref[...], sb_ref[...], bd_ref[...]
    o_kv, o_f, o_a = Q_W, Q_W + 2 * KV_W, Q_W + 2 * KV_W + FOURIER_W + POOL_W
    qkv = _dot(hb, wbf_ref[:, 0:o_f])
    fp = _dot(hb, wbf_ref[:, o_f:o_a])
    q = _rope(_head_rms(qkv[:, 0:Q_W], gq_ref[...], bd), cos, sa, sb)
    q_ref[...] = (jnp.concatenate(q, axis=1) * (LOG2E * HEAD_DIM ** -0.5)).astype(BF16)
    k = _rope(_head_rms(qkv[:, o_kv:o_kv + KV_W], gk_ref[...], bd), cos, sa, sb)
    kv_ref[:, 0:KV_W] = k[0].astype(BF16)
    kv_ref[:, KV_W:2 * KV_W] = qkv[:, o_kv + KV_W:o_f].astype(BF16)
    ag = _dot(hb, wbf_ref[:, o_a:PROJ_W])
    f_ref[...] = fp[:, 0:FOURIER_W].astype(BF16)
    p_ref[...] = fp[:, FOURIER_W:].astype(BF16)
    u_ref[...] = (ag[:, 0:CONV_W] * _sigmoid(ag[:, CONV_W:])).astype(BF16)


def _mod_row(i):
    return jnp.minimum(i // TILES_PER_BATCH, 2)


def _layer_spec(shape, l):
    nd = len(shape)
    return pl.BlockSpec((None,) + tuple(shape), lambda *_: (l,) + (0,) * nd, pipeline_mode=pl.Buffered(1))


def _stream_specs(split):
    first = (lambda i: (jnp.minimum(i, LAT_TILES - 1), 0)) if split else (lambda i: (i, 0))
    return [pl.BlockSpec((TM, D), first),
            pl.BlockSpec((TM, D), lambda i: (0, 0), pipeline_mode=pl.Buffered(1))]


def _stream_tile(xl_ref, xc_ref, split):
    return jnp.where(pl.program_id(0) >= LAT_TILES, xc_ref[...], xl_ref[...]) if split else xl_ref[...]


def _proj_call(x_lat, x_ctx, split, mods_l, gn, w_in, l, rope_tabs, gq128, gk128, bd):
    cos, sa, sb = rope_tabs
    tok = lambda w: pl.BlockSpec((TM, w), lambda i: (i, 0))
    rope_spec = pl.BlockSpec((TM, LANES), lambda i: (jnp.where(i < LAT_TILES, i % TILES_PER_BATCH,
                                                               TILES_PER_BATCH), 0))
    widths = (Q_W, 2 * KV_W, FOURIER_W, POOL_W, CONV_W)
    return pl.pallas_call(
        functools.partial(_proj_kernel, split=split),
        out_shape=[jax.ShapeDtypeStruct((N_TOK, w), BF16) for w in widths],
        grid=(NT,),
        in_specs=_stream_specs(split) + [
                  pl.BlockSpec((None, 6, D), lambda i: (_mod_row(i), 0, 0)),
                  _const_spec((1, D)),
                  _layer_spec((D, PROJ_W), l),
                  rope_spec, rope_spec, rope_spec,
                  _const_spec((1, LANES)), _const_spec((1, LANES)),
                  _const_spec((LANES, LANES))],
        out_specs=[tok(w) for w in widths],
        scratch_shapes=[pltpu.VMEM((D, PROJ_W), BF16)],
        compiler_params=_cparams(("arbitrary",)),
        name="proj",
    )(x_lat, x_ctx, mods_l, gn, w_in, cos, sa, sb, gq128, gk128, bd)


def _attend_many(jobs, sink_ref):
    lane = lax.broadcasted_iota(jnp.int32, (ATTN_BLK, LANES), 1)
    chains = []
    for q, kv_blocks, biases in jobs:
        for j in range(N_KV_HEADS):
            ks = slice(j * HEAD_DIM, (j + 1) * HEAD_DIM)
            vs = slice(KV_W + j * HEAD_DIM, KV_W + (j + 1) * HEAD_DIM)
            kj = jnp.concatenate([blk[:, ks] for blk in kv_blocks], axis=0)
            vj = jnp.concatenate([blk[:, vs] for blk in kv_blocks], axis=0)
            vaug = jnp.concatenate([vj, jnp.ones_like(vj)], axis=1)
            qs = jnp.concatenate([q[:, (j * GQA + g) * HEAD_DIM:(j * GQA + g + 1) * HEAD_DIM]
                                  for g in range(GQA)], axis=0)
            s = lax.dot_general(qs, kj, (((1,), (1,)), ((), ())), preferred_element_type=F32)
            chains.append((j, s, vaug, kv_blocks, biases))
    soft = []
    for j, s, vaug, kv_blocks, biases in chains:
        probs, sink_terms = [], []
        for g in range(GQA):
            sg = s[g * ATTN_BLK:(g + 1) * ATTN_BLK]
            pieces, col = [], 0
            for blk, bias in zip(kv_blocks, biases):
                piece = sg[:, col:col + blk.shape[0]]
                pieces.append(piece if bias is None else piece + bias)
                col += blk.shape[0]
            sg = jnp.concatenate(pieces, axis=1)
            sk = sink_ref[j * GQA + g] * LOG2E
            mx = jnp.maximum(jnp.max(sg, axis=-1, keepdims=True), sk)
            probs.append(jnp.exp2(sg - mx).astype(BF16))
            sink_terms.append(jnp.exp2(sk - mx))
        soft.append((jnp.concatenate(probs, axis=0), vaug, sink_terms))
    heads = []
    for p, vaug, sink_terms in soft:
        o = _dot(p, vaug)
        for g in range(GQA):
            og = o[g * ATTN_BLK:(g + 1) * ATTN_BLK]
            heads.append(og / (og[:, HEAD_DIM:HEAD_DIM + 1] + sink_terms[g]))
    outs = []
    for n in range(len(jobs)):
        hs = heads[n * N_Q_HEADS:(n + 1) * N_Q_HEADS]
        tiles = [jnp.where(lane < HEAD_DIM, hs[2 * t], pltpu.roll(hs[2 * t + 1], HEAD_DIM, 1))
                 for t in range(N_Q_HEADS // 2)]
        outs.append(jnp.concatenate(tiles, axis=1).astype(BF16))
    return outs


def _attn_latent_kernel(sink_ref, q_ref, prev_ref, cur_ref, next_ref, ctx_ref, o_ref):
    n = pl.program_id(1)
    r = lax.broadcasted_iota(jnp.int32, (ATTN_BLK, ATTN_BLK), 0)
    jj = lax.broadcasted_iota(jnp.int32, (ATTN_BLK, ATTN_BLK), 1)
    far = jnp.int32(2 * ATTN_BLK)
    off_prev = jnp.where(n > 0, 0, far)
    off_next = jnp.where(n < S // (ATTN_QB * ATTN_BLK) - 1, 0, far)
    prev_ok = jnp.where(jj - r >= 0, 0.0, NEG_INF)
    next_ok = jnp.where(r - jj >= 0, 0.0, NEG_INF)
    prev_edge = jnp.where(jj - r - off_prev >= 0, 0.0, NEG_INF)
    next_edge = jnp.where(r - jj - off_next >= 0, 0.0, NEG_INF)
    ctx = ctx_ref[...]
    rows = lambda b: slice(b * ATTN_BLK, (b + 1) * ATTN_BLK)
    blocks = [prev_ref[...]] + [cur_ref[rows(b), :] for b in range(ATTN_QB)] + [next_ref[...]]
    jobs = [(q_ref[rows(b), :], [ctx] + blocks[b:b + 3],
             [None, prev_edge if b == 0 else prev_ok, None, next_edge if b == ATTN_QB - 1 else next_ok])
            for b in range(ATTN_QB)]
    for b, out in enumerate(_attend_many(jobs, sink_ref)):
        o_ref[rows(b), :] = out


def _attn_context_kernel(sink_ref, q_ref, ctx_ref, o_ref):
    o_ref[...] = _attend_many([(q_ref[...], [ctx_ref[...]], [None])], sink_ref)[0]


def _attn_call(sink_l, q, kv):
    nb = S // ATTN_BLK
    nq = nb // ATTN_QB
    smem = pl.BlockSpec(memory_space=pltpu.SMEM)
    pair = lambda w: pl.BlockSpec((ATTN_QB * ATTN_BLK, w), lambda b, n: (b * nq + n, 0))
    prev = pl.BlockSpec((ATTN_BLK, 2 * KV_W), lambda b, n: (b * nb + jnp.maximum(ATTN_QB * n - 1, 0), 0))
    nxt = pl.BlockSpec((ATTN_BLK, 2 * KV_W),
                       lambda b, n: (b * nb + jnp.minimum(ATTN_QB * (n + 1), nb - 1), 0))
    ctxs = pl.BlockSpec((C, 2 * KV_W), lambda b, n: (N_LAT // C + b, 0))
    lat = pl.pallas_call(
        _attn_latent_kernel,
        out_shape=jax.ShapeDtypeStruct((N_LAT, Q_W), BF16),
        grid=(B, nq),
        in_specs=[smem, pair(Q_W), prev, pair(2 * KV_W), nxt, ctxs],
        out_specs=pair(Q_W),
        compiler_params=_cparams(("parallel", "parallel")),
        name="attn_latent",
    )(sink_l, q, kv, kv, kv, kv)
    ncb = C // ATTN_BLK
    base = N_LAT // ATTN_BLK
    ctx = pl.pallas_call(
        _attn_context_kernel,
        out_shape=jax.ShapeDtypeStruct((N_CTX, Q_W), BF16),
        grid=(B, ncb),
        in_specs=[smem, pl.BlockSpec((ATTN_BLK, Q_W), lambda b, n: (base + b * ncb + n, 0)), ctxs],
        out_specs=pl.BlockSpec((ATTN_BLK, Q_W), lambda b, n: (b * ncb + n, 0)),
        compiler_params=_cparams(("parallel", "parallel")),
        name="attn_context",
    )(sink_l, q, kv)
    return lat, ctx


def _f1_kernel(w_ref, f_ref, re_ref, im_ref):
    res = lax.dot_general(w_ref[...], f_ref[...], (((1,), (0,)), ((), ())), preferred_element_type=F32)
    re_ref[...] = res[:FS1].astype(BF16)
    im_ref[...] = res[FS1:].astype(BF16)


def _f2_kernel(ta_ref, tb_ref, re_ref, im_ref, after_ref, o_ref):
    del after_ref
    for i in range(F2_K1):
        res = _dot(ta_ref[i], re_ref[i]) + _dot(tb_ref[i], im_ref[i])
        o_ref[i, :, 0:FOURIER_W] = res[:FS2].astype(BF16)
        o_ref[i, :, FOURIER_W:2 * FOURIER_W] = res[FS2:].astype(BF16)


def _fc_kernel(w_ref, f_ref, o_ref):
    res = _dot(w_ref[...], f_ref[...])
    o_ref[:, 0:FOURIER_W] = res[:C].astype(BF16)
    o_ref[:, FOURIER_W:2 * FOURIER_W] = res[C:].astype(BF16)


def _fourier_stage1_call(f, tabs):
    w1 = tabs[0]
    f3 = f.reshape(N_TOK // FS2, FS2, FOURIER_W)
    blk = pl.BlockSpec((FS1, F1_ROWS, FOURIER_W), lambda b, j: (b, j, 0))
    return pl.pallas_call(
        _f1_kernel,
        out_shape=[jax.ShapeDtypeStruct((B * FS1, FS2, FOURIER_W), BF16)] * 2,
        grid=(B, FS2 // F1_ROWS),
        in_specs=[_const_spec((2 * FS1, FS1)), blk],
        out_specs=[blk, blk],
        compiler_params=_cparams(("parallel", "parallel")),
        name="fourier_stage1",
    )(w1, f3)


def _fourier_stage2_call(a_re, a_im, f, tabs, after):
    _, ta, tb, wc = tabs
    nk = FS1 // F2_K1
    aspec = pl.BlockSpec((F2_K1, FS2, FOURIER_W), lambda b, k1: (b * nk + k1, 0, 0))
    tspec = pl.BlockSpec((F2_K1, 2 * FS2, FS2), lambda b, k1: (k1, 0, 0))
    h_t = pl.pallas_call(
        _f2_kernel,
        out_shape=jax.ShapeDtypeStruct((B, FS1, FS2, 2 * FOURIER_W), BF16),
        grid=(B, nk),
        in_specs=[tspec, tspec, aspec, aspec, pl.BlockSpec(memory_space=pl.ANY)],
        out_specs=pl.BlockSpec((None, F2_K1, FS2, 2 * FOURIER_W), lambda b, k1: (b, k1, 0, 0)),
        compiler_params=_cparams(("parallel", "parallel")),
        name="fourier_stage2",
    )(ta, tb, a_re, a_im, after)
    h_lat = jnp.transpose(h_t, (0, 2, 1, 3)).reshape(N_LAT, 2 * FOURIER_W)
    h_ctx = pl.pallas_call(
        _fc_kernel,
        out_shape=jax.ShapeDtypeStruct((N_CTX, 2 * FOURIER_W), BF16),
        grid=(B,),
        in_specs=[_const_spec((2 * C, C)),
                  pl.BlockSpec((C, FOURIER_W), lambda b: (N_LAT // C + b, 0))],
        out_specs=pl.BlockSpec((C, 2 * FOURIER_W), lambda b: (b, 0)),
        compiler_params=_cparams(("parallel",)),
        name="fourier_context",
    )(wc, f)
    return h_lat, h_ctx


def _poolconv_kernel(pc_ref, pp_ref, pn_ref, uc_ref, up_ref, un_ref, bm_ref, bh_ref, ic_ref, pw_ref, ps_ref,
                     sh_ref, cw_ref, cb_ref, cg_ref, cnb_ref, z_ref, a_ref, win0_ref, win1_ref, cv_ref):
    t = pl.program_id(0)
    lat_tiles = N_LAT // TP
    per_seq = S // TP
    is_ctx = t >= lat_tiles
    first = jnp.logical_or(t % per_seq == 0, is_ctx)
    last = jnp.logical_or(t % per_seq == per_seq - 1, is_ctx)

    keep_prev = jnp.where(first, 0.0, 1.0)
    keep_next = jnp.where(last, 0.0, 1.0)

    ub = jnp.concatenate([(up_ref[...].astype(F32) * keep_prev).astype(BF16), uc_ref[...],
                          (un_ref[...].astype(F32) * keep_next).astype(BF16)], axis=0)
    off = HALO - CONV_K // 2
    half_rows = TP // 2
    wins = (win0_ref, win1_ref)
    for hf, win_ref in enumerate(wins):
        window = ub[hf * half_rows:hf * half_rows + CONV_WIN]
        win_ref[0] = window.astype(F32)
        for s in range(1, 8):
            win_ref[s] = _dot(sh_ref[s - 1], window)

    pcur = pc_ref[...]
    halo = jnp.concatenate([pp_ref[...].astype(F32) * keep_prev,
                            pn_ref[...].astype(F32) * keep_next], axis=0).astype(BF16)
    sums = []
    for gi in range(len(POOL_WINDOWS)):
        cs = slice(gi * LANES, (gi + 2) * LANES)
        sums.append(_dot(bm_ref[gi], pcur[:, cs]) + _dot(bh_ref[gi], halo[:, cs]))
    lane_t = lax.broadcasted_iota(jnp.int32, (TP, LANES), 1)
    tiles = [sums[0][:, :LANES]]
    for gi in range(1, len(POOL_WINDOWS)):
        split = gi * POOL_GROUP_W - gi * LANES
        tiles.append(jnp.where(lane_t < split, sums[gi - 1][:, LANES:], sums[gi][:, :LANES]))
    tiles.append(sums[-1][:, LANES:])
    zsum = jnp.concatenate(tiles, axis=1)
    z = zsum * ic_ref[...] - pcur.astype(F32)
    z_ref[...] = (_dot(z.astype(BF16), pw_ref[...]) * ps_ref[...]).astype(BF16)

    for hf, win_ref in enumerate(wins):
        base = hf * half_rows
        for cb in range(CONV_W // LANES):
            cs = slice(cb * LANES, (cb + 1) * LANES)
            acc = jnp.zeros((half_rows, LANES), F32) + cb_ref[:, cs]
            for j in range(CONV_K):
                s, m = (off + j) % 8, (off + j) // 8
                acc = acc + win_ref[s, 8 * m:8 * m + half_rows, cs] * cw_ref[j:j + 1, cs]
            cv_ref[base:base + half_rows, cs] = acc
    cv = cv_ref[...]
    mu = jnp.mean(cv, axis=-1, keepdims=True)
    var = jnp.mean(jnp.square(cv - mu), axis=-1, keepdims=True)
    un = (cv - mu) * lax.rsqrt(var + EPS) * cg_ref[...] + cnb_ref[...]
    a_ref[...] = (un * _sigmoid(un)).astype(BF16)


def _poolconv_call(p, u, band_main, band_halo, inv_cnt, pw_bd, pool_scale, shifts, conv_w, conv_b, cn_g, cn_b):
    nh = TP // HALO
    last_h = N_TOK // HALO - 1
    cur = lambda w: pl.BlockSpec((TP, w), lambda t: (t, 0))
    prv = lambda w: pl.BlockSpec((HALO, w), lambda t: (jnp.maximum(t * nh - 1, 0), 0))
    nxt = lambda w: pl.BlockSpec((HALO, w), lambda t: (jnp.minimum((t + 1) * nh, last_h), 0))
    per_seq = S // TP

    def kind(t):
        return jnp.where(t >= N_LAT // TP, 3, jnp.where(t % per_seq == 0, 1, jnp.where(t % per_seq == per_seq - 1, 2, 0)))

    return pl.pallas_call(
        _poolconv_kernel,
        out_shape=[jax.ShapeDtypeStruct((N_TOK, POOL_W), BF16),
                   jax.ShapeDtypeStruct((N_TOK, CONV_W), BF16)],
        grid=(NTP,),
        in_specs=[cur(POOL_W), prv(POOL_W), nxt(POOL_W), cur(CONV_W), prv(CONV_W), nxt(CONV_W),
                  _const_spec((4, TP, TP)), _const_spec((4, TP, 2 * HALO)),
                  pl.BlockSpec((None, TP, POOL_W), lambda t: (kind(t), 0, 0)),
                  _const_spec((POOL_W, POOL_W)), _const_spec((1, POOL_W)),
                  _const_spec((7, CONV_WIN, CONV_WIN)),
                  _const_spec((CONV_K, CONV_W)), _const_spec((1, CONV_W)),
                  _const_spec((1, CONV_W)), _const_spec((1, CONV_W))],
        out_specs=[cur(POOL_W), cur(CONV_W)],
        scratch_shapes=[pltpu.VMEM((8, CONV_WIN, CONV_W), F32), pltpu.VMEM((8, CONV_WIN, CONV_W), F32),
                        pltpu.VMEM((TP, CONV_W), F32)],
        compiler_params=_cparams(("parallel",)),
        name="pool_conv",
    )(p, p, p, u, u, u, band_main, band_halo, inv_cnt, pw_bd, pool_scale, shifts, conv_w, conv_b, cn_g, cn_b)


def _route(logits):
    lane = lax.broadcasted_iota(jnp.int32, logits.shape, 1)
    big = jnp.int32(LANES)
    lg = jnp.where(lane < N_GROUPS, logits, NEG_INF)
    mg = jnp.max(lg, axis=-1, keepdims=True)
    grp = jnp.min(jnp.where(lg == mg, lane, big), axis=-1, keepdims=True)
    p_grp = 1.0 / jnp.sum(jnp.exp(lg - mg), axis=-1, keepdims=True)
    lo = N_GROUPS + grp * EPG
    le = jnp.where((lane >= lo) & (lane < lo + EPG), logits, NEG_INF)
    m1 = jnp.max(le, axis=-1, keepdims=True)
    i1 = jnp.min(jnp.where(le == m1, lane, big), axis=-1, keepdims=True)
    le2 = jnp.where(lane == i1, NEG_INF, le)
    m2 = jnp.max(le2, axis=-1, keepdims=True)
    i2 = jnp.min(jnp.where(le2 == m2, lane, big), axis=-1, keepdims=True)
    r = jnp.exp(m2 - m1)
    w1 = p_grp / (1.0 + r)
    w2 = p_grp * r / (1.0 + r)
    e1 = (i1 - N_GROUPS).astype(F32)
    e2 = (i2 - N_GROUPS).astype(F32)
    return jnp.where(lane == 0, e1, jnp.where(lane == 1, e2, jnp.where(lane == 2, w1,
                     jnp.where(lane == 3, w2, 0.0))))


def _mix_kernel(xl_ref, xc_ref, mod_ref, gn_ref, al_ref, ac_ref, hl_ref, hc_ref, z_ref, cv_ref,
                wa_ref, wf_ref, wp_ref, wc_ref, wg_ref, bg_ref, wo_ref, gf_ref, rh_ref, rl_ref, rb_ref,
                xo_ref, h2_ref, rt_ref, *, split):
    is_ctx = pl.program_id(0) >= LAT_TILES
    m = mod_ref[...]
    x = _stream_tile(xl_ref, xc_ref, split)
    hb = _modulate(x, gn_ref[...], m[0:1], m[1:2]).astype(BF16)
    attn = jnp.where(is_ctx, ac_ref[...], al_ref[...])
    four = jnp.where(is_ctx, hc_ref[...], hl_ref[...])
    branches = ((attn, wa_ref), (four, wf_ref), (z_ref[...], wp_ref), (cv_ref[...], wc_ref))
    acc = None
    for bi, (inp, w_ref) in enumerate(branches):
        cs = slice(bi * D, (bi + 1) * D)
        gate = _sigmoid(_dot(hb, wg_ref[:, cs]) + bg_ref[:, cs])
        term = gate * _dot(inp, w_ref[...])
        acc = term if acc is None else acc + term
    x_new = x + m[2:3] * _dot(acc.astype(BF16), wo_ref[...])
    xo_ref[...] = x_new
    h2 = _modulate(x_new, gf_ref[...], m[3:4], m[4:5])
    hi = h2.astype(BF16)
    h2_ref[...] = hi
    lo = (h2 - hi.astype(F32)).astype(BF16)
    logits = _dot(hi, rh_ref[...]) + _dot(lo, rh_ref[...]) + _dot(hi, rl_ref[...]) + rb_ref[...]
    rt_ref[...] = _route(logits)


def _mix_call(x_lat, x_ctx, split, mods_l, gn, a_lat, a_ctx, h_lat, h_ctx, zc, cact, l, stacked, small):
    tok = lambda w: pl.BlockSpec((TM, w), lambda i: (i, 0))
    lat = lambda w: pl.BlockSpec((TM, w), lambda i: (jnp.minimum(i, LAT_TILES - 1), 0))
    wa, wf, wp, wc, wg, wo, rh, rl = stacked
    bg, gf, rb = small
    in_specs = _stream_specs(split) + [
                pl.BlockSpec((None, 6, D), lambda i: (_mod_row(i), 0, 0)), _const_spec((1, D)),
                lat(Q_W), _const_spec((N_CTX, Q_W)),
                lat(2 * FOURIER_W), _const_spec((N_CTX, 2 * FOURIER_W)),
                tok(POOL_W), tok(CONV_W)]
    in_specs += [_layer_spec(w.shape[1:], l) for w in (wa, wf, wp, wc, wg)]
    in_specs += [_const_spec(bg.shape), _layer_spec(wo.shape[1:], l), _const_spec(gf.shape),
                 _layer_spec(rh.shape[1:], l), _layer_spec(rl.shape[1:], l), _const_spec(rb.shape)]
    return pl.pallas_call(
        functools.partial(_mix_kernel, split=split),
        out_shape=[jax.ShapeDtypeStruct((N_TOK, D), F32), jax.ShapeDtypeStruct((N_TOK, D), BF16),
                   jax.ShapeDtypeStruct((N_TOK, LANES), F32)],
        grid=(NT,),
        in_specs=in_specs,
        out_specs=[tok(D), tok(D), tok(LANES)],
        compiler_params=_cparams(("parallel",)),
        name="mix",
    )(x_lat, x_ctx, mods_l, gn, a_lat, a_ctx, h_lat, h_ctx, zc, cact, wa, wf, wp, wc, wg, bg, wo, gf, rh, rl, rb)


def _onehots(route):
    lane = lax.broadcasted_iota(jnp.int32, route.shape, 1)
    e1 = route[:, 0:1].astype(jnp.int32)
    e2 = route[:, 1:2].astype(jnp.int32)
    return (lane == e1).astype(F32), (lane == e2).astype(F32)


def _lane_cumsum(row):
    lane = lax.broadcasted_iota(jnp.int32, row.shape, 1)
    sh = 1
    while sh < N_EXPERTS:
        row = row + jnp.where(lane >= sh, pltpu.roll(row, sh, 1), 0.0)
        sh *= 2
    return row


def _rank_kernel(rt_ref, tri_ref, pos_ref, meta_ref, cnt_ref, carry_ref):
    @pl.when(pl.program_id(0) == 0)
    def _():
        carry_ref[...] = jnp.zeros_like(carry_ref)

    tiles = []
    for t in range(RANK_TILES):
        oh1, oh2 = _onehots(rt_ref[t * TM:(t + 1) * TM, :])
        both = oh1 + oh2
        tiles.append((oh1, oh2, both, _dot(tri_ref[...], both.astype(BF16))))
    carry = carry_ref[0:1, :]
    lane = lax.broadcasted_iota(jnp.int32, (TM, LANES), 1)
    row = lax.broadcasted_iota(jnp.int32, meta_ref.shape[1:], 0)
    for t, (oh1, oh2, both, before) in enumerate(tiles):
        tile_cnt = jnp.sum(both, axis=0, keepdims=True)
        tile_cnt = tile_cnt + (tile_cnt - 2.0 * jnp.floor(tile_cnt * 0.5))
        tile_off = _lane_cumsum(tile_cnt) - tile_cnt
        where = before + tile_off
        p1 = jnp.sum(oh1 * where, axis=-1, keepdims=True)
        p2 = jnp.sum(oh2 * where, axis=-1, keepdims=True)
        pos_ref[t * TM:(t + 1) * TM, :] = jnp.where(lane == 0, p1, jnp.where(lane == 1, p2, 0.0))
        meta_ref[t] = jnp.where(row == 0, tile_off, jnp.where(row == 1, tile_cnt, jnp.where(row == 2, carry, 0.0)))
        carry = carry + tile_cnt
    carry_ref[...] = jnp.broadcast_to(carry, carry_ref.shape)
    cnt_ref[...] = jnp.broadcast_to(carry, cnt_ref.shape)


def _runs_kernel(meta_ref, cnt_ref, runs_ref, be_ref):
    lane = lax.broadcasted_iota(jnp.int32, (1, LANES), 1)
    counts = cnt_ref[0:1, :]
    padded = jnp.floor((counts + (MOE_BLK - 1)) * (1.0 / MOE_BLK)) * MOE_BLK
    padded = jnp.where(lane < N_EXPERTS, padded, 0.0)
    ends = _lane_cumsum(padded)
    starts = ends - padded
    for t in range(NT):
        m = meta_ref[t]
        row = lax.broadcasted_iota(jnp.int32, m.shape, 0)
        m = jnp.where(row == 2, m + starts, m)
        m = jnp.where(row == 3, starts + counts, jnp.where(row == 4, padded - counts, m))
        runs_ref[t] = m.astype(jnp.int32)
    blk = lax.broadcasted_iota(jnp.int32, be_ref.shape, 0).astype(F32) * MOE_BLK
    lane_b = lax.broadcasted_iota(jnp.int32, be_ref.shape, 1)
    done = jnp.where((ends <= blk) & (lane_b < N_EXPERTS), 1.0, 0.0)
    be = jnp.minimum(jnp.sum(done, axis=-1, keepdims=True), N_EXPERTS - 1.0)
    nblk = jnp.max(jnp.where(lane_b == N_EXPERTS - 1, ends, 0.0), axis=-1, keepdims=True) * (1.0 / MOE_BLK)
    be_ref[...] = jnp.where(lane_b == 0, be, jnp.where(lane_b == 1, nblk, 0.0)).astype(jnp.int32)


def _plan_call(route, tri):
    tok = pl.BlockSpec((RANK_TILES * TM, LANES), lambda i: (i, 0))
    pos, meta, counts = pl.pallas_call(
        _rank_kernel,
        out_shape=[jax.ShapeDtypeStruct((N_TOK, LANES), F32), jax.ShapeDtypeStruct((NT, 8, LANES), F32),
                   jax.ShapeDtypeStruct((8, LANES), F32)],
        grid=(NT // RANK_TILES,),
        in_specs=[tok, _const_spec((TM, TM))],
        out_specs=[tok, pl.BlockSpec((RANK_TILES, 8, LANES), lambda i: (i, 0, 0)),
                   pl.BlockSpec((8, LANES), lambda i: (0, 0))],
        scratch_shapes=[pltpu.VMEM((8, LANES), F32)],
        compiler_params=_cparams(("arbitrary",)),
        name="moe_rank",
    )(route, tri)
    runs, blk = pl.pallas_call(
        _runs_kernel,
        out_shape=[jax.ShapeDtypeStruct((NT, 8, LANES), jnp.int32),
                   jax.ShapeDtypeStruct((BLK_TABLE_ROWS, LANES), jnp.int32)],
        name="moe_runs",
    )(meta, counts)
    runs_flat = runs[:, 0:RUN_FIELDS, 0:N_EXPERTS].reshape(NT, 1, RUN_FIELDS * N_EXPERTS)
    return pos, runs_flat, blk[:N_MOE_BLOCKS, 0], blk[0:1, 1]


def _pack_pairs(x):
    half = x.shape[1] // 2
    lo = pltpu.bitcast(x[:, :half], jnp.uint32)
    hi = pltpu.bitcast(x[:, half:], jnp.uint32)
    return (lo >> 16) | (hi & jnp.uint32(0xFFFF0000))


def _unpack_pairs(w):
    lo = pltpu.bitcast(w << 16, F32)
    hi = pltpu.bitcast(w & jnp.uint32(0xFFFF0000), F32)
    return jnp.concatenate([lo, hi], axis=1).astype(BF16)


def _run_fields(runs_ref, e):
    return runs_ref[0, e], runs_ref[0, N_EXPERTS + e], runs_ref[0, 2 * N_EXPERTS + e]


def _store_rows(lin_ref, packed):
    rows = packed.shape[0]
    for c in range(PK):
        lin_ref[pl.ds(c, rows, stride=PK), :] = packed[:, c * LANES:(c + 1) * LANES]


def _load_rows(lin_ref, rows):
    return jnp.concatenate([lin_ref[pl.ds(c, rows, stride=PK), :] for c in range(PK)], axis=1)


def _lin(ref, row, nrows):
    return ref.at[pl.ds(pl.multiple_of(row * PK, SUBLANES), nrows * PK), :]


FETCH_ROWS = 64
RUN_PIECES = (32, 16, 8, 4, 2)
TAIL_PIECES = (256, 128, 64, 32, 16, 8, 4, 2)


def _run_copies(runs, make_copy, act):
    def per_expert(e, carry):
        off, n, dst = _run_fields(runs, e)
        whole = n // FETCH_ROWS

        def chunk(k, c):
            act(make_copy(off + k * FETCH_ROWS, dst + k * FETCH_ROWS, FETCH_ROWS))
            return c

        lax.fori_loop(0, whole, chunk, 0)
        done = whole * FETCH_ROWS
        for size in RUN_PIECES:
            @pl.when((n & size) != 0)
            def _(done=done, size=size):
                act(make_copy(off + done, dst + done, size))
            done = done + (n & size)
        return carry

    lax.fori_loop(0, N_EXPERTS, per_expert, 0)


def _rows_wait(runs, make_copy):
    total = lax.fori_loop(0, N_EXPERTS, lambda e, acc: acc + runs[0, N_EXPERTS + e], jnp.int32(0))
    lax.fori_loop(0, total // FETCH_ROWS, lambda k, c: (make_copy(0, 0, FETCH_ROWS).wait(), c)[1], 0)
    for size in RUN_PIECES:
        @pl.when((total & size) != 0)
        def _(size=size):
            make_copy(0, 0, size).wait()


def _dispatch_kernel(runs_ref, prev_runs_ref, pos_ref, h2_ref, xs_ref, buf0, buf1, zero_ref, sem0, sem1, zsem,
                     ssem):
    i = pl.program_id(0)
    last = NT - 1

    def tail_copies(act):
        def per_expert(e, carry):
            row = runs_ref[0, 3 * N_EXPERTS + e]
            n = runs_ref[0, 4 * N_EXPERTS + e]
            done = jnp.int32(0)
            for size in TAIL_PIECES:
                @pl.when((n & size) != 0)
                def _(done=done, size=size):
                    act(pltpu.make_async_copy(zero_ref.at[pl.ds(0, size * PK), :], _lin(xs_ref, row + done, size), zsem))
                done = done + (n & size)
            return carry

        lax.fori_loop(0, N_EXPERTS, per_expert, 0)

    def spare_copies(act):
        used = runs_ref[0, 3 * N_EXPERTS + N_EXPERTS - 1] + runs_ref[0, 4 * N_EXPERTS + N_EXPERTS - 1]

        def spare_block(k, carry):
            act(pltpu.make_async_copy(zero_ref.at[pl.ds(0, MOE_BLK * PK), :],
                                      _lin(xs_ref, used + k * MOE_BLK, MOE_BLK), ssem))
            return carry

        lax.fori_loop(0, N_MOE_BLOCKS - used // MOE_BLK, spare_block, 0)

    start = lambda d: d.start()
    wait = lambda d: d.wait()

    @pl.when(i == 0)
    def _():
        zero_ref[...] = jnp.zeros_like(zero_ref)
        tail_copies(start)
        spare_copies(start)

    pos = pos_ref[...]
    col = lax.broadcasted_iota(jnp.int32, (TM, TS), 1).astype(F32)
    sel = jnp.where((col == pos[:, 0:1]) | (col == pos[:, 1:2]), 1.0, 0.0).astype(BF16)
    srt = lax.dot_general(sel, h2_ref[...], (((0,), (0,)), ((), ())), preferred_element_type=F32)
    packed = _pack_pairs(srt)

    for par, (buf, sem, obuf, osem) in enumerate(((buf0, sem0, buf1, sem1), (buf1, sem1, buf0, sem0))):
        @pl.when(i % 2 == par)
        def _(buf=buf, sem=sem, obuf=obuf, osem=osem):
            copy = lambda off, dst, rows: pltpu.make_async_copy(_lin(buf, off, rows), _lin(xs_ref, dst, rows), sem)
            ocopy = lambda off, dst, rows: pltpu.make_async_copy(_lin(obuf, off, rows), _lin(xs_ref, dst, rows), osem)
            _store_rows(buf, packed)

            @pl.when(i == 0)
            def _():
                tail_copies(wait)

            _run_copies(runs_ref, copy, start)

            @pl.when(i > 0)
            def _():
                _rows_wait(prev_runs_ref, ocopy)

            @pl.when(i == last)
            def _():
                _rows_wait(runs_ref, copy)
                spare_copies(wait)


def _dispatch_call(runs_flat, pos, h2):
    runs_spec = lambda shift: pl.BlockSpec((None, 1, RUN_FIELDS * N_EXPERTS), lambda i: (jnp.maximum(i - shift, 0), 0, 0),
                                           memory_space=pltpu.SMEM)
    sorted_buf = pltpu.VMEM((TS * PK, LANES), jnp.uint32)
    return pl.pallas_call(
        _dispatch_kernel,
        out_shape=jax.ShapeDtypeStruct((N_SLOTS * PK, LANES), jnp.uint32),
        grid=(NT,),
        in_specs=[runs_spec(0), runs_spec(1),
                  pl.BlockSpec((TM, LANES), lambda i: (i, 0)),
                  pl.BlockSpec((TM, D), lambda i: (i, 0))],
        out_specs=pl.BlockSpec(memory_space=pl.ANY),
        scratch_shapes=[sorted_buf, sorted_buf, pltpu.VMEM((MOE_BLK * PK, LANES), jnp.uint32),
                        pltpu.SemaphoreType.DMA, pltpu.SemaphoreType.DMA, pltpu.SemaphoreType.DMA,
                        pltpu.SemaphoreType.DMA],
        compiler_params=_cparams(("arbitrary",)),
        name="moe_dispatch",
    )(runs_flat, runs_flat, pos, h2)


def _expert_kernel(be_ref, nu_ref, nxt_ref, slot_ref, xs_ref, wg_hbm, wu_hbm, wd_hbm, ys_ref,
                   wgb_ref, wub_ref, wdb_ref, wg_buf, wu_buf, wd_buf, sems, *, layer):
    b = pl.program_id(0)

    def copies(e, slot):
        return [pltpu.make_async_copy(hbm.at[layer, e], buf.at[slot], sems.at[slot])
                for hbm, buf in ((wg_hbm, wg_buf), (wu_hbm, wu_buf), (wd_hbm, wd_buf))]

    @pl.when(b == 0)
    def _():
        for cp in copies(be_ref[0], 0):
            cp.start()

    @pl.when(jnp.logical_or(b == 0, be_ref[b] != be_ref[jnp.maximum(b - 1, 0)]))
    def _():
        slot = slot_ref[b]
        for cp in copies(be_ref[b], slot):
            cp.wait()
        wgb_ref[...] = wg_buf[slot].astype(BF16)
        wub_ref[...] = wu_buf[slot].astype(BF16)
        wdb_ref[...] = wd_buf[slot].astype(BF16)

        @pl.when(nxt_ref[b] >= 0)
        def _():
            for cp in copies(nxt_ref[b], 1 - slot):
                cp.start()

    @pl.when(b < nu_ref[0])
    def _():
        xb = _unpack_pairs(_load_rows(xs_ref, MOE_BLK))
        g = _dot(xb, wgb_ref[...])
        u = _dot(xb, wub_ref[...])
        hmid = (g * _sigmoid(g)) * u
        y = _dot(hmid.astype(BF16), wdb_ref[...])
        _store_rows(ys_ref, _pack_pairs(y.astype(BF16).astype(F32)))

    @pl.when(b >= nu_ref[0])
    def _():
        ys_ref[...] = jnp.zeros_like(ys_ref)


def _expert_call(blk_e, n_used, xs, wg, wu, wd, l):
    idx = jnp.arange(N_MOE_BLOCKS, dtype=jnp.int32)
    change = jnp.concatenate([jnp.ones((1,), bool), blk_e[1:] != blk_e[:-1]])
    slot = (jnp.cumsum(change.astype(jnp.int32)) - 1) % 2
    nxt_start = jnp.where(change, idx, N_MOE_BLOCKS)
    nxt_start = lax.cummin(jnp.concatenate([nxt_start[1:], jnp.full((1,), N_MOE_BLOCKS, jnp.int32)]), reverse=True)
    nxt = jnp.where(nxt_start < N_MOE_BLOCKS, blk_e[jnp.minimum(nxt_start, N_MOE_BLOCKS - 1)], -1)
    hbm = pl.BlockSpec(memory_space=pl.ANY)
    return pl.pallas_call(
        functools.partial(_expert_kernel, layer=l),
        out_shape=jax.ShapeDtypeStruct((N_SLOTS * PK, LANES), jnp.uint32),
        grid_spec=pltpu.PrefetchScalarGridSpec(
            num_scalar_prefetch=4,
            grid=(N_MOE_BLOCKS,),
            in_specs=[pl.BlockSpec((MOE_BLK * PK, LANES), lambda b, be, nu, nx, sl: (jnp.minimum(b, nu[0] - 1), 0)),
                      hbm, hbm, hbm],
            out_specs=pl.BlockSpec((MOE_BLK * PK, LANES), lambda b, be, nu, nx, sl: (b, 0)),
            scratch_shapes=[pltpu.VMEM((D, EXPERT_HIDDEN), BF16), pltpu.VMEM((D, EXPERT_HIDDEN), BF16),
                            pltpu.VMEM((EXPERT_HIDDEN, D), BF16),
                            pltpu.VMEM((2, D, EXPERT_HIDDEN), F32), pltpu.VMEM((2, D, EXPERT_HIDDEN), F32),
                            pltpu.VMEM((2, EXPERT_HIDDEN, D), F32), pltpu.SemaphoreType.DMA((2,))]),
        compiler_params=_cparams(("arbitrary",)),
        name="moe_experts",
    )(blk_e, n_used, nxt.astype(jnp.int32), slot.astype(jnp.int32), xs, wg, wu, wd)


def _combine_kernel(runs_ref, next_runs_ref, ys_ref, pos_ref, x_ref, rt_ref, mod_ref, o_ref, buf0, buf1, sem0, sem1,
                    *, n_tiles):
    i = pl.program_id(0)
    last = n_tiles - 1
    start = lambda d: d.start()
    fetch = lambda buf, sem: (
        lambda off, dst, rows: pltpu.make_async_copy(_lin(ys_ref, dst, rows), _lin(buf, off, rows), sem))

    @pl.when(i == 0)
    def _():
        buf0[...] = jnp.zeros_like(buf0)
        buf1[...] = jnp.zeros_like(buf1)
        _run_copies(runs_ref, fetch(buf0, sem0), start)

    pos = pos_ref[...]
    rt = rt_ref[...]
    col = lax.broadcasted_iota(jnp.int32, (TM, TS), 1).astype(F32)
    pick = (jnp.where(col == pos[:, 0:1], rt[:, TOP_K:TOP_K + 1], 0.0)
            + jnp.where(col == pos[:, 1:2], rt[:, TOP_K + 1:TOP_K + 2], 0.0)).astype(BF16)

    for par, (buf, sem, obuf, osem) in enumerate(((buf0, sem0, buf1, sem1), (buf1, sem1, buf0, sem0))):
        @pl.when(i % 2 == par)
        def _(buf=buf, sem=sem, obuf=obuf, osem=osem):
            @pl.when(i < last)
            def _():
                _run_copies(next_runs_ref, fetch(obuf, osem), start)

            _rows_wait(runs_ref, fetch(buf, sem))
            ysb = _unpack_pairs(_load_rows(buf, TS))
            o_ref[...] = x_ref[...] + mod_ref[5:6, :] * _dot(pick, ysb)


def _combine_call(runs_flat, ys, pos, x, route, mods_l, n_tiles):
    tok = lambda w: pl.BlockSpec((TM, w), lambda i: (i, 0))
    runs_spec = lambda shift: pl.BlockSpec((None, 1, RUN_FIELDS * N_EXPERTS),
                                           lambda i: (jnp.minimum(i + shift, n_tiles - 1), 0, 0),
                                           memory_space=pltpu.SMEM)
    sorted_buf = pltpu.VMEM((TS * PK, LANES), jnp.uint32)
    return pl.pallas_call(
        functools.partial(_combine_kernel, n_tiles=n_tiles),
        out_shape=jax.ShapeDtypeStruct((n_tiles * TM, D), F32),
        grid=(n_tiles,),
        in_specs=[runs_spec(0), runs_spec(1),
                  pl.BlockSpec(memory_space=pl.ANY),
                  tok(LANES), tok(D), tok(LANES),
                  pl.BlockSpec((None, 6, D), lambda i: (_mod_row(i), 0, 0))],
        out_specs=tok(D),
        scratch_shapes=[sorted_buf, sorted_buf, pltpu.SemaphoreType.DMA, pltpu.SemaphoreType.DMA],
        compiler_params=_cparams(("arbitrary",)),
        name="moe_combine",
    )(runs_flat, runs_flat, ys, pos, x, route, mods_l)


def _rope_tables():
    nf = HEAD_DIM // 4
    inv = ROPE_BASE ** (-jnp.arange(nf, dtype=F32) / nf)
    t = jnp.arange(S)
    row = (t // GRID_W).astype(F32)[:, None] * inv[None, :]
    col = (t % GRID_W).astype(F32)[:, None] * inv[None, :]
    zero = jnp.zeros_like(row)
    cos = jnp.concatenate([jnp.cos(row), jnp.cos(row), jnp.cos(col), jnp.cos(col)], axis=1)
    sa = jnp.concatenate([-jnp.sin(row), zero, -jnp.sin(col), zero], axis=1)
    sb = jnp.concatenate([zero, jnp.sin(row), zero, jnp.sin(col)], axis=1)
    ident = (jnp.ones((TM, HEAD_DIM), F32), jnp.zeros((TM, HEAD_DIM), F32), jnp.zeros((TM, HEAD_DIM), F32))
    return tuple(jnp.tile(jnp.concatenate([a, b], axis=0), (1, LANES // HEAD_DIM))
                 for a, b in zip((cos, sa, sb), ident))


def _fourier_tables():
    s1 = np.arange(FS1)
    ang1 = 2.0 * np.pi * np.outer(s1, s1) / FS1
    w1 = np.concatenate([np.cos(ang1), -np.sin(ang1)], axis=0) / np.sqrt(S)
    k1 = np.arange(FS1)[:, None, None]
    k2 = np.arange(FS2)[None, :, None]
    s2 = np.arange(FS2)[None, None, :]
    ang2 = 2.0 * np.pi * ((k1 + FS1 * k2) * s2 % S) / S
    c2, sn2 = np.cos(ang2), np.sin(ang2)
    ta = np.concatenate([c2, -sn2], axis=1)
    tb = np.concatenate([sn2, c2], axis=1)
    sc = np.arange(C)
    angc = 2.0 * np.pi * np.outer(sc, sc) / C
    wc = np.concatenate([np.cos(angc), -np.sin(angc)], axis=0) / np.sqrt(C)
    return tuple(jnp.asarray(a, F32).astype(BF16) for a in (w1, ta, tb, wc))


def _channel_dft():
    cidx = np.arange(FOURIER_GROUP_W)
    ang = 2.0 * np.pi * np.outer(cidx, cidx) / FOURIER_GROUP_W
    eye = np.eye(FOURIER_W // FOURIER_GROUP_W)
    cw = np.kron(eye, np.cos(ang)) / np.sqrt(FOURIER_GROUP_W)
    sw = np.kron(eye, np.sin(ang)) / np.sqrt(FOURIER_GROUP_W)
    return jnp.asarray(np.concatenate([cw, sw], axis=0), F32)


def _pool_bands():
    t = np.arange(TP)[:, None]
    main, halo = [], []
    for w in POOL_WINDOWS:
        def hit(j):
            return ((j - t >= -(w // 2)) & (j - t <= w // 2 - 1)).astype(np.float32)
        main.append(hit(np.arange(TP)[None, :]))
        halo.append(np.concatenate([hit(np.arange(-HALO, 0)[None, :]),
                                    hit(np.arange(TP, TP + HALO)[None, :])], axis=1))
    return (jnp.asarray(np.stack(main), F32).astype(BF16), jnp.asarray(np.stack(halo), F32).astype(BF16))


def _pool_inv_counts():
    win = np.repeat(np.array(POOL_WINDOWS), POOL_GROUP_W)[None, :]

    def table(pos0, seq_len):
        pos = (pos0 + np.arange(TP))[:, None]
        lo = np.clip(pos - win // 2, 0, seq_len)
        hi = np.clip(pos - win // 2 + win, 0, seq_len)
        return 1.0 / (hi - lo)

    tabs = [table(TP, S), table(0, S), table(S - TP, S), table(0, C)]
    return jnp.asarray(np.stack(tabs), F32)


def _conv_shifts():
    i = np.arange(CONV_WIN)
    return jnp.asarray(np.stack([(i[None, :] == i[:, None] + s) for s in range(1, 8)]), F32).astype(BF16)


def _fold_kernel(a_ref, b_ref, o_ref):
    a, b = a_ref[...], b_ref[...]
    a_hi, b_hi = a.astype(BF16), b.astype(BF16)
    a_lo = (a - a_hi.astype(F32)).astype(BF16)
    b_lo = (b - b_hi.astype(F32)).astype(BF16)
    o_ref[...] = (_dot(a_hi, b_hi) + _dot(a_lo, b_hi) + _dot(a_hi, b_lo)).astype(BF16)


def _fold_fourier_weights(dftw, w_br_fourier):
    nl = w_br_fourier.shape[0]
    return pl.pallas_call(
        _fold_kernel,
        out_shape=jax.ShapeDtypeStruct((nl, 2 * FOURIER_W, D), BF16),
        grid=(nl,),
        in_specs=[pl.BlockSpec((2 * FOURIER_W, FOURIER_W), lambda l: (0, 0)),
                  pl.BlockSpec((None, FOURIER_W, D), lambda l: (l, 0, 0))],
        out_specs=pl.BlockSpec((None, 2 * FOURIER_W, D), lambda l: (l, 0, 0)),
        compiler_params=_cparams(("arbitrary",)),
        name="fold_fourier_proj",
    )(dftw, w_br_fourier)


def _block_diag(blocks):
    n, r, c = blocks.shape
    eye = jnp.eye(n, dtype=blocks.dtype)
    return (blocks[:, :, None, :] * eye[:, None, :, None]).reshape(n * r, n * c)


def kernel(x, c, ctx, c_ctx, w_ada, b_ada, g_norm_mix, g_norm_ffn, w_in, g_q, g_k, sink, w_br_attn,
           w_br_fourier, pool_w, pool_scale, w_br_pool, conv_w, conv_b, cn_g, cn_b, w_br_conv, w_gate,
           b_gate, w_out, w_router_grp, b_router_grp, w_router_exp, b_router_exp, w_e_gate, w_e_up,
           w_e_down):
    x_lat, x_ctx, split = x.reshape(N_LAT, D), ctx.reshape(N_CTX, D), True
    nl = w_ada.shape[0]
    mods = _ada_all(c, c_ctx, w_ada, b_ada).reshape(nl, 8, 6, D)
    rope_tabs = _rope_tables()
    four_tabs = _fourier_tables()
    band_main, band_halo = _pool_bands()
    inv_cnt = _pool_inv_counts()
    shifts = _conv_shifts()
    wf_all = _fold_fourier_weights(_channel_dft(), w_br_fourier)
    bd = jnp.asarray(np.kron(np.eye(LANES // HEAD_DIM), np.ones((HEAD_DIM, HEAD_DIM))), F32).astype(BF16)
    tri = jnp.asarray(np.tril(np.ones((TM, TM)), -1), F32).astype(BF16)
    rpad = jnp.zeros((nl, D, LANES - N_GROUPS - N_EXPERTS), F32)
    w_router = jnp.concatenate([w_router_grp, w_router_exp, rpad], axis=-1)
    r_hi = w_router.astype(BF16)
    r_lo = (w_router - r_hi.astype(F32)).astype(BF16)
    r_b = jnp.concatenate([b_router_grp, b_router_exp, rpad[:, 0, :]], axis=-1).reshape(nl, 1, LANES)
    stacked = tuple(w.astype(BF16) for w in (w_br_attn,)) + (wf_all,) + tuple(
        w.astype(BF16) for w in (w_br_pool, w_br_conv, w_gate, w_out)) + (r_hi, r_lo)

    for l in range(nl):
        mods_l = mods[l]
        gn = g_norm_mix[l].reshape(1, D)
        q, kv, f, p, u = _proj_call(x_lat, x_ctx, split, mods_l, gn, w_in, l, rope_tabs,
                                    jnp.tile(g_q[l], 2).reshape(1, LANES),
                                    jnp.tile(g_k[l], 2).reshape(1, LANES), bd)
        a_re, a_im = _fourier_stage1_call(f, four_tabs)
        a_lat, a_ctx = _attn_call(sink[l], q, kv)
        h_lat, h_ctx = _fourier_stage2_call(a_re, a_im, f, four_tabs, a_lat)
        zc, cact = _poolconv_call(p, u, band_main, band_halo, inv_cnt, _block_diag(pool_w[l]).astype(BF16),
                                  pool_scale[l].reshape(1, POOL_W), shifts, conv_w[l], conv_b[l].reshape(1, CONV_W),
                                  cn_g[l].reshape(1, CONV_W), cn_b[l].reshape(1, CONV_W))
        small = (b_gate[l].reshape(1, 4 * D), g_norm_ffn[l].reshape(1, D), r_b[l])
        xs, h2, route = _mix_call(x_lat, x_ctx, split, mods_l, gn, a_lat, a_ctx, h_lat, h_ctx, zc, cact, l,
                                  stacked, small)
        pos, runs_flat, blk_e, n_used = _plan_call(route, tri)
        slots = _dispatch_call(runs_flat, pos, h2)
        ys = _expert_call(blk_e, n_used, slots, w_e_gate, w_e_up, w_e_down, l)
        xs = _combine_call(runs_flat, ys, pos, xs, route, mods_l, LAT_TILES if l == nl - 1 else NT)
        x_lat, x_ctx, split = xs, xs, False
    return xs.reshape(B, S, D)
```
